```python
import jax, jax.numpy as jnp
from jax import lax
import numpy as np

D_MODEL = 1024
BATCH = 8
SEQ = 16384
DEPTH = 1

N_META = 16
CHUNK = 64
EPS = 1e-6
M_HEADS = 4
M_DV = D_MODEL // M_HEADS
M_DQK = M_DV // 2
M_QK = M_HEADS * M_DQK
M_V = M_HEADS * M_DV
CONV_W = 4
F_BIAS = 3.0
G_HEADS = 4
G_DV = D_MODEL // G_HEADS
G_DK = G_DV // 2
G_QK = G_HEADS * G_DK
G_V = G_HEADS * G_DV
G_RANK = 16
G_TAU = 16.0
D_FF = ((8 * D_MODEL + 3 * 256 - 1) // (3 * 256)) * 256
PROJ_WIDTHS = (M_QK, M_QK, M_V, M_HEADS, M_HEADS, M_V, G_QK, G_QK, G_V, G_RANK, G_V, D_MODEL, D_MODEL)
N_PROJ = sum(PROJ_WIDTHS)

kernel_name = 'hybrid_mlstm_gla_block'


def rmsnorm(x, g):
    xf = x.astype(jnp.float32)
    y = xf * lax.rsqrt(jnp.mean(xf * xf, axis=-1, keepdims=True) + EPS)
    return (y * g.astype(jnp.float32)).astype(x.dtype)


def head_rmsnorm(h, g):
    y = h * lax.rsqrt(jnp.mean(h * h, axis=-1, keepdims=True) + EPS)
    return y * g.astype(jnp.float32)


def split_cols(p):
    idx = np.cumsum(np.array(PROJ_WIDTHS))[:-1]
    return jnp.split(p, idx, axis=-1)


def to_chunks(t, n_heads):
    b, tp = t.shape[:2]
    t = t.reshape(b, tp // CHUNK, CHUNK, n_heads, -1)
    return jnp.transpose(t, (0, 3, 1, 2, 4)).astype(jnp.float32)


def from_chunks(t):
    b, h, nc, l, d = t.shape
    return jnp.transpose(t, (0, 2, 3, 1, 4)).reshape(b, nc * l, h, d)


def causal_depthwise_conv(x, w, bias):
    k = w.shape[0]
    y = lax.conv_general_dilated(x, w[:, None, :].astype(x.dtype), window_strides=(1,),
                                 padding=[(k - 1, 0)], dimension_numbers=('NWC', 'WIO', 'NWC'),
                                 feature_group_count=x.shape[-1])
    return y + bias.astype(x.dtype)


def mlstm_chunkwise(q, k, v, logi, logf):
    L = q.shape[3]
    b = jnp.cumsum(logf, axis=-1)
    g = b[..., -1]
    causal = jnp.tril(jnp.ones((L, L), dtype=bool))
    dmat = jnp.where(causal, b[..., :, None] - b[..., None, :] + logi[..., None, :], -jnp.inf)
    wlog = g[..., None] - b + logi

    def step(carry, inp):
        C, n, m = carry
        kc, vc, wc, gc = inp
        m_new = jnp.maximum(gc + m, jnp.max(wc, axis=-1))
        a = jnp.exp(gc + m - m_new)
        w = jnp.exp(wc - m_new[..., None])
        C_new = a[..., None, None] * C + jnp.einsum('bhl,bhlv,bhlk->bhvk', w, vc, kc)
        n_new = a[..., None] * n + jnp.einsum('bhl,bhlk->bhk', w, kc)
        return (C_new, n_new, m_new), (C, n, m)

    bsz, nh, _, _, dqk = q.shape
    dv = v.shape[-1]
    init = (jnp.zeros((bsz, nh, dv, dqk), jnp.float32), jnp.zeros((bsz, nh, dqk), jnp.float32),
            jnp.zeros((bsz, nh), jnp.float32))
    xs = (jnp.moveaxis(k, 2, 0), jnp.moveaxis(v, 2, 0), jnp.moveaxis(wlog, 2, 0), jnp.moveaxis(g, 2, 0))
    _, (Cs, ns, ms) = lax.scan(step, init, xs)
    Cs = jnp.moveaxis(Cs, 0, 2)
    ns = jnp.moveaxis(ns, 0, 2)
    ms = jnp.moveaxis(ms, 0, 2)

    inter_log = b + ms[..., None]
    m_row = jnp.maximum(inter_log, jnp.max(dmat, axis=-1))
    sim = jnp.einsum('bhcjd,bhcsd->bhcjs', q, k)
    wts = jnp.exp(dmat - m_row[..., None]) * sim
    a_inter = jnp.exp(inter_log - m_row)
    num = (a_inter[..., None] * jnp.einsum('bhcvd,bhcjd->bhcjv', Cs, q)
           + jnp.einsum('bhcjs,bhcsv->bhcjv', wts, v))
    den = a_inter * jnp.einsum('bhcd,bhcjd->bhcj', ns, q) + jnp.sum(wts, axis=-1)
    return num / jnp.maximum(jnp.abs(den), jnp.exp(-m_row))[..., None]


def gla_chunked(q, k, v, loga):
    L = q.shape[3]
    bc = jnp.cumsum(loga, axis=3)
    btot = bc[..., -1, :]
    q_dec = q * jnp.exp(bc)
    k_inv = k * jnp.exp(-bc)
    k_end = k * jnp.exp(btot[..., None, :] - bc)
    causal = jnp.tril(jnp.ones((L, L), dtype=bool))
    att = jnp.where(causal, jnp.einsum('bhcjd,bhcsd->bhcjs', q_dec, k_inv), 0.0)
    intra = jnp.einsum('bhcjs,bhcsv->bhcjv', att, v)

    def step(S, inp):
        ke, vc, bt = inp
        S_new = jnp.exp(bt)[..., None] * S + jnp.einsum('bhlk,bhlv->bhkv', ke, vc)
        return S_new, S

    bsz, nh, _, _, dk = q.shape
    dv = v.shape[-1]
    S0 = jnp.zeros((bsz, nh, dk, dv), jnp.float32)
    _, S_prev = lax.scan(step, S0, (jnp.moveaxis(k_end, 2, 0), jnp.moveaxis(v, 2, 0), jnp.moveaxis(btot, 2, 0)))
    S_prev = jnp.moveaxis(S_prev, 0, 2)
    return intra + jnp.einsum('bhcjk,bhckv->bhcjv', q_dec, S_prev)


def _fwd_setup_inputs(seed: int = 0) -> dict:
    key = jax.random.key(seed)
    ks = jax.random.split(key, 24)
    f32 = jnp.float32
    nrm = lambda k, shape, s: jax.random.normal(k, shape, f32) * s
    m_gate_b = jnp.stack([nrm(ks[6], (DEPTH, M_HEADS), 0.01),
                          F_BIAS + nrm(ks[7], (DEPTH, M_HEADS), 0.1)], axis=1)
    return {
        'x': nrm(ks[0], (BATCH, SEQ, D_MODEL), 1.0),
        'meta_tokens': nrm(ks[1], (N_META, D_MODEL), 1.0),
        'norm1_g': 1.0 + nrm(ks[2], (DEPTH, D_MODEL), 0.02),
        'w_in': nrm(ks[3], (DEPTH, D_MODEL, N_PROJ), D_MODEL ** -0.5),
        'conv_w': nrm(ks[4], (DEPTH, CONV_W, 2 * M_QK), CONV_W ** -0.5),
        'conv_b': nrm(ks[5], (DEPTH, 2 * M_QK), 0.01),
        'm_gate_b': m_gate_b,
        'g_a2': nrm(ks[8], (DEPTH, G_RANK, G_QK), G_RANK ** -0.5),
        'g_a2_b': nrm(ks[9], (DEPTH, G_QK), 0.01),
        'm_head_g': 1.0 + nrm(ks[10], (DEPTH, M_HEADS, M_DV), 0.02),
        'g_head_g': 1.0 + nrm(ks[11], (DEPTH, G_HEADS, G_DV), 0.02),
        'w_branch_m': nrm(ks[12], (DEPTH, M_V, D_MODEL), M_V ** -0.5),
        'w_branch_g': nrm(ks[13], (DEPTH, G_V, D_MODEL), G_V ** -0.5),
        'w_out': nrm(ks[14], (DEPTH, D_MODEL, D_MODEL), D_MODEL ** -0.5),
        'norm2_g': 1.0 + nrm(ks[15], (DEPTH, D_MODEL), 0.02),
        'w_ff_gate': nrm(ks[16], (DEPTH, D_MODEL, D_FF), D_MODEL ** -0.5),
        'w_ff_up': nrm(ks[17], (DEPTH, D_MODEL, D_FF), D_MODEL ** -0.5),
        'w_ff_down': nrm(ks[18], (DEPTH, D_FF, D_MODEL), D_FF ** -0.5),
        'final_g': 1.0 + nrm(ks[19], (D_MODEL,), 0.02),
    }


def _fwd_reference(x, meta_tokens, norm1_g, w_in, conv_w, conv_b, m_gate_b, g_a2, g_a2_b, m_head_g, g_head_g,
              w_branch_m, w_branch_g, w_out, norm2_g, w_ff_gate, w_ff_up, w_ff_down, final_g):
    f32 = jnp.float32
    bsz, _, d = x.shape
    dt = x.dtype
    n_pad = CHUNK - N_META
    meta = jnp.broadcast_to(meta_tokens.astype(dt)[None], (bsz, N_META, d))
    h = jnp.concatenate([jnp.zeros((bsz, n_pad, d), dt), meta, x], axis=1)
    tp = h.shape[1]
    valid = (jnp.arange(tp) >= n_pad)[None, :, None]

    for l in range(DEPTH):
        xn = rmsnorm(h, norm1_g[l])
        proj = jnp.where(valid, xn @ w_in[l].astype(dt), 0.0).astype(dt)
        (mq, mk, mv, mi, mf, mo, gq, gk, gv, ga, gr, gate_m, gate_g) = split_cols(proj)

        mqk = jax.nn.silu(causal_depthwise_conv(jnp.concatenate([mq, mk], axis=-1), conv_w[l], conv_b[l]))
        mq, mk = jnp.split(mqk, 2, axis=-1)
        logi = jnp.where(valid, mi.astype(f32) + m_gate_b[l, 0], -jnp.inf)
        logf = jnp.where(valid, jax.nn.log_sigmoid(mf.astype(f32) + m_gate_b[l, 1]), 0.0)
        hm = mlstm_chunkwise(to_chunks(mq, M_HEADS) * (M_DQK ** -0.5), to_chunks(mk, M_HEADS),
                             to_chunks(mv, M_HEADS), to_chunks(logi, M_HEADS)[..., 0],
                             to_chunks(logf, M_HEADS)[..., 0])
        hm = head_rmsnorm(from_chunks(hm), m_head_g[l]) * jax.nn.sigmoid(mo.astype(f32)).reshape(bsz, tp, M_HEADS, M_DV)
        y_m = hm.reshape(bsz, tp, M_V).astype(dt)

        za = ga @ g_a2[l].astype(dt) + g_a2_b[l].astype(dt)
        loga = jnp.where(valid, jax.nn.log_sigmoid(za.astype(f32)) / G_TAU, 0.0)
        hg = gla_chunked(to_chunks(gq, G_HEADS) * (G_DK ** -0.5), to_chunks(gk, G_HEADS),
                         to_chunks(gv, G_HEADS), to_chunks(loga, G_HEADS))
        hg = head_rmsnorm(from_chunks(hg), g_head_g[l]) * jax.nn.silu(gr.astype(f32)).reshape(bsz, tp, G_HEADS, G_DV)
        y_g = hg.reshape(bsz, tp, G_V).astype(dt)

        merged = (jax.nn.sigmoid(gate_m) * (y_m @ w_branch_m[l].astype(dt))
                  + jax.nn.sigmoid(gate_g) * (y_g @ w_branch_g[l].astype(dt)))
        h = h + merged @ w_out[l].astype(dt)

        hn = rmsnorm(h, norm2_g[l])
        ff = jax.nn.silu(hn @ w_ff_gate[l].astype(dt)) * (hn @ w_ff_up[l].astype(dt))
        h = h + ff @ w_ff_down[l].astype(dt)

    out = rmsnorm(h, final_g)
    return out[:, CHUNK:, :]


import jax as _jax
import jax.numpy as _jnp

TWIN_FORMAT = 'train_step'
FWD_PARAMS = ['x', 'meta_tokens', 'norm1_g', 'w_in', 'conv_w', 'conv_b', 'm_gate_b', 'g_a2', 'g_a2_b', 'm_head_g', 'g_head_g', 'w_branch_m', 'w_branch_g', 'w_out', 'norm2_g', 'w_ff_gate', 'w_ff_up', 'w_ff_down', 'final_g']
TWIN_WEIGHTS = ['meta_tokens', 'norm1_g', 'w_in', 'conv_w', 'conv_b', 'm_gate_b', 'g_a2', 'g_a2_b', 'm_head_g', 'g_head_g', 'w_branch_m', 'w_branch_g', 'w_out', 'norm2_g', 'w_ff_gate', 'w_ff_up', 'w_ff_down', 'final_g']
TWIN_DIFF_INPUT = 'x'
TWIN_INPUTS = ['x', 'meta_tokens', 'norm1_g', 'w_in', 'conv_w', 'conv_b', 'm_gate_b', 'g_a2', 'g_a2_b', 'm_head_g', 'g_head_g', 'w_branch_m', 'w_branch_g', 'w_out', 'norm2_g', 'w_ff_gate', 'w_ff_up', 'w_ff_down', 'final_g', 'loss_target', 'm_meta_tokens', 'm_norm1_g', 'm_w_in', 'm_conv_w', 'm_conv_b', 'm_m_gate_b', 'm_g_a2', 'm_g_a2_b', 'm_m_head_g', 'm_g_head_g', 'm_w_branch_m', 'm_w_branch_g', 'm_w_out', 'm_norm2_g', 'm_w_ff_gate', 'm_w_ff_up', 'm_w_ff_down', 'm_final_g', 'v_meta_tokens', 'v_norm1_g', 'v_w_in', 'v_conv_w', 'v_conv_b', 'v_m_gate_b', 'v_g_a2', 'v_g_a2_b', 'v_m_head_g', 'v_g_head_g', 'v_w_branch_m', 'v_w_branch_g', 'v_w_out', 'v_norm2_g', 'v_w_ff_gate', 'v_w_ff_up', 'v_w_ff_down', 'v_final_g']
TWIN_OUTPUTS = ['loss', 'grad_x', 'grad_meta_tokens', 'grad_norm1_g', 'grad_w_in', 'grad_conv_w', 'grad_conv_b', 'grad_m_gate_b', 'grad_g_a2', 'grad_g_a2_b', 'grad_m_head_g', 'grad_g_head_g', 'grad_w_branch_m', 'grad_w_branch_g', 'grad_w_out', 'grad_norm2_g', 'grad_w_ff_gate', 'grad_w_ff_up', 'grad_w_ff_down', 'grad_final_g', 'delta_meta_tokens', 'delta_norm1_g', 'delta_w_in', 'delta_conv_w', 'delta_conv_b', 'delta_m_gate_b', 'delta_g_a2', 'delta_g_a2_b', 'delta_m_head_g', 'delta_g_head_g', 'delta_w_branch_m', 'delta_w_branch_g', 'delta_w_out', 'delta_norm2_g', 'delta_w_ff_gate', 'delta_w_ff_up', 'delta_w_ff_down', 'delta_final_g', 'new_m_meta_tokens', 'new_m_norm1_g', 'new_m_w_in', 'new_m_conv_w', 'new_m_conv_b', 'new_m_m_gate_b', 'new_m_g_a2', 'new_m_g_a2_b', 'new_m_m_head_g', 'new_m_g_head_g', 'new_m_w_branch_m', 'new_m_w_branch_g', 'new_m_w_out', 'new_m_norm2_g', 'new_m_w_ff_gate', 'new_m_w_ff_up', 'new_m_w_ff_down', 'new_m_final_g', 'new_v_meta_tokens', 'new_v_norm1_g', 'new_v_w_in', 'new_v_conv_w', 'new_v_conv_b', 'new_v_m_gate_b', 'new_v_g_a2', 'new_v_g_a2_b', 'new_v_m_head_g', 'new_v_g_head_g', 'new_v_w_branch_m', 'new_v_w_branch_g', 'new_v_w_out', 'new_v_norm2_g', 'new_v_w_ff_gate', 'new_v_w_ff_up', 'new_v_w_ff_down', 'new_v_final_g']
TWIN_LEAF_KINDS = {'loss': 'loss', 'grad_x': 'grad_x', 'grad_meta_tokens': 'grad_w', 'grad_norm1_g': 'grad_w', 'grad_w_in': 'grad_w', 'grad_conv_w': 'grad_w', 'grad_conv_b': 'grad_w', 'grad_m_gate_b': 'grad_w', 'grad_g_a2': 'grad_w', 'grad_g_a2_b': 'grad_w', 'grad_m_head_g': 'grad_w', 'grad_g_head_g': 'grad_w', 'grad_w_branch_m': 'grad_w', 'grad_w_branch_g': 'grad_w', 'grad_w_out': 'grad_w', 'grad_norm2_g': 'grad_w', 'grad_w_ff_gate': 'grad_w', 'grad_w_ff_up': 'grad_w', 'grad_w_ff_down': 'grad_w', 'grad_final_g': 'grad_w', 'delta_meta_tokens': 'delta_w', 'delta_norm1_g': 'delta_w', 'delta_w_in': 'delta_w', 'delta_conv_w': 'delta_w', 'delta_conv_b': 'delta_w', 'delta_m_gate_b': 'delta_w', 'delta_g_a2': 'delta_w', 'delta_g_a2_b': 'delta_w', 'delta_m_head_g': 'delta_w', 'delta_g_head_g': 'delta_w', 'delta_w_branch_m': 'delta_w', 'delta_w_branch_g': 'delta_w', 'delta_w_out': 'delta_w', 'delta_norm2_g': 'delta_w', 'delta_w_ff_gate': 'delta_w', 'delta_w_ff_up': 'delta_w', 'delta_w_ff_down': 'delta_w', 'delta_final_g': 'delta_w', 'new_m_meta_tokens': 'new_m', 'new_m_norm1_g': 'new_m', 'new_m_w_in': 'new_m', 'new_m_conv_w': 'new_m', 'new_m_conv_b': 'new_m', 'new_m_m_gate_b': 'new_m', 'new_m_g_a2': 'new_m', 'new_m_g_a2_b': 'new_m', 'new_m_m_head_g': 'new_m', 'new_m_g_head_g': 'new_m', 'new_m_w_branch_m': 'new_m', 'new_m_w_branch_g': 'new_m', 'new_m_w_out': 'new_m', 'new_m_norm2_g': 'new_m', 'new_m_w_ff_gate': 'new_m', 'new_m_w_ff_up': 'new_m', 'new_m_w_ff_down': 'new_m', 'new_m_final_g': 'new_m', 'new_v_meta_tokens': 'new_v', 'new_v_norm1_g': 'new_v', 'new_v_w_in': 'new_v', 'new_v_conv_w': 'new_v', 'new_v_conv_b': 'new_v', 'new_v_m_gate_b': 'new_v', 'new_v_g_a2': 'new_v', 'new_v_g_a2_b': 'new_v', 'new_v_m_head_g': 'new_v', 'new_v_g_head_g': 'new_v', 'new_v_w_branch_m': 'new_v', 'new_v_w_branch_g': 'new_v', 'new_v_w_out': 'new_v', 'new_v_norm2_g': 'new_v', 'new_v_w_ff_gate': 'new_v', 'new_v_w_ff_up': 'new_v', 'new_v_w_ff_down': 'new_v', 'new_v_final_g': 'new_v'}


def _forward(args):
    return _fwd_reference(*[args[k] for k in FWD_PARAMS])


def _output_shape():
    def fwd():
        inp = _fwd_setup_inputs(0)
        return _fwd_reference(*[inp[k] for k in FWD_PARAMS])
    out = _jax.eval_shape(fwd)
    return out.shape, out.dtype

N_MICROBATCH = 1
ADAM_LR = 0.001
ADAM_B1 = 0.9
ADAM_B2 = 0.999
ADAM_EPS = 1e-08
ADAM_WD = 0.01
ADAM_STEP = 10
PER_EXAMPLE_BATCH_AXIS = {'x': 0, 'loss_target': 0}
SHARED_INPUTS = []
_WEIGHT_DTYPES = {'meta_tokens': _jnp.float32, 'norm1_g': _jnp.float32, 'w_in': _jnp.float32, 'conv_w': _jnp.float32, 'conv_b': _jnp.float32, 'm_gate_b': _jnp.float32, 'g_a2': _jnp.float32, 'g_a2_b': _jnp.float32, 'm_head_g': _jnp.float32, 'g_head_g': _jnp.float32, 'w_branch_m': _jnp.float32, 'w_branch_g': _jnp.float32, 'w_out': _jnp.float32, 'norm2_g': _jnp.float32, 'w_ff_gate': _jnp.float32, 'w_ff_up': _jnp.float32, 'w_ff_down': _jnp.float32, 'final_g': _jnp.float32}
MOMENT_SCALE = {'meta_tokens': 9.936508e-03, 'norm1_g': 3.355007e-01, 'w_in': 1.084151e-01, 'conv_w': 1.140406e-01, 'conv_b': 1.027269e-01, 'm_gate_b': 1.264881e+00, 'g_a2': 1.960548e-02, 'g_a2_b': 7.557260e-02, 'm_head_g': 1.118965e-01, 'g_head_g': 1.343668e-01, 'w_branch_m': 1.063800e-01, 'w_branch_g': 1.169691e-01, 'w_out': 1.584804e-01, 'norm2_g': 2.542604e-01, 'w_ff_gate': 1.056372e-01, 'w_ff_up': 1.027701e-01, 'w_ff_down': 1.698009e-01, 'final_g': 1.277248e+02}


def _to_microbatches(a, axis):
    t = _jnp.moveaxis(a, axis, 0)
    t = t.reshape((N_MICROBATCH, t.shape[0] // N_MICROBATCH) + t.shape[1:])
    return _jnp.moveaxis(t, 1, axis + 1)


def setup_inputs(seed: int = 0) -> dict:
    inp = _fwd_setup_inputs(seed)
    key = _jax.random.fold_in(_jax.random.key(seed), 7919)
    shape, _ = _output_shape()
    out = dict(inp)
    out["loss_target"] = _jax.random.normal(_jax.random.fold_in(key, 0), shape, _jnp.float32)
    for i, name in enumerate(TWIN_WEIGHTS):
        w = inp[name].astype(_jnp.float32)
        if MOMENT_SCALE is None:
            s = _jnp.sqrt(_jnp.mean(_jnp.square(w)) + 1e-30)
        else:
            s = MOMENT_SCALE[name]
        km, kv = _jax.random.split(_jax.random.fold_in(key, i + 1))
        out[name] = w
        out["m_" + name] = s * _jax.random.normal(km, w.shape, _jnp.float32)
        out["v_" + name] = (s * s) * _jax.random.uniform(kv, w.shape, _jnp.float32, 0.5, 1.5)
    if N_MICROBATCH > 1:
        for name, axis in PER_EXAMPLE_BATCH_AXIS.items():
            out[name] = _to_microbatches(out[name], axis)
    return {'x': out['x'], 'meta_tokens': out['meta_tokens'], 'norm1_g': out['norm1_g'], 'w_in': out['w_in'], 'conv_w': out['conv_w'], 'conv_b': out['conv_b'], 'm_gate_b': out['m_gate_b'], 'g_a2': out['g_a2'], 'g_a2_b': out['g_a2_b'], 'm_head_g': out['m_head_g'], 'g_head_g': out['g_head_g'], 'w_branch_m': out['w_branch_m'], 'w_branch_g': out['w_branch_g'], 'w_out': out['w_out'], 'norm2_g': out['norm2_g'], 'w_ff_gate': out['w_ff_gate'], 'w_ff_up': out['w_ff_up'], 'w_ff_down': out['w_ff_down'], 'final_g': out['final_g'], 'loss_target': out['loss_target'], 'm_meta_tokens': out['m_meta_tokens'], 'm_norm1_g': out['m_norm1_g'], 'm_w_in': out['m_w_in'], 'm_conv_w': out['m_conv_w'], 'm_conv_b': out['m_conv_b'], 'm_m_gate_b': out['m_m_gate_b'], 'm_g_a2': out['m_g_a2'], 'm_g_a2_b': out['m_g_a2_b'], 'm_m_head_g': out['m_m_head_g'], 'm_g_head_g': out['m_g_head_g'], 'm_w_branch_m': out['m_w_branch_m'], 'm_w_branch_g': out['m_w_branch_g'], 'm_w_out': out['m_w_out'], 'm_norm2_g': out['m_norm2_g'], 'm_w_ff_gate': out['m_w_ff_gate'], 'm_w_ff_up': out['m_w_ff_up'], 'm_w_ff_down': out['m_w_ff_down'], 'm_final_g': out['m_final_g'], 'v_meta_tokens': out['v_meta_tokens'], 'v_norm1_g': out['v_norm1_g'], 'v_w_in': out['v_w_in'], 'v_conv_w': out['v_conv_w'], 'v_conv_b': out['v_conv_b'], 'v_m_gate_b': out['v_m_gate_b'], 'v_g_a2': out['v_g_a2'], 'v_g_a2_b': out['v_g_a2_b'], 'v_m_head_g': out['v_m_head_g'], 'v_g_head_g': out['v_g_head_g'], 'v_w_branch_m': out['v_w_branch_m'], 'v_w_branch_g': out['v_w_branch_g'], 'v_w_out': out['v_w_out'], 'v_norm2_g': out['v_norm2_g'], 'v_w_ff_gate': out['v_w_ff_gate'], 'v_w_ff_up': out['v_w_ff_up'], 'v_w_ff_down': out['v_w_ff_down'], 'v_final_g': out['v_final_g']}


def _loss(weights, diff, rest, loss_target):
    with _jax.named_scope("forward"):
        args = {**rest, TWIN_DIFF_INPUT: diff, **{k: w.astype(_WEIGHT_DTYPES[k]) for k, w in weights.items()}}
        y = _forward(args)
    with _jax.named_scope("loss_head"):
        err = _jnp.square(y.astype(_jnp.float32) - loss_target)
        return 0.5 * _jnp.sum(_jnp.mean(err, axis=-1)) if err.ndim else 0.5 * err


def _adamw(w, g, m, v):
    m = ADAM_B1 * m + (1.0 - ADAM_B1) * g
    v = ADAM_B2 * v + (1.0 - ADAM_B2) * _jnp.square(g)
    m_hat = m / (1.0 - ADAM_B1 ** ADAM_STEP)
    v_hat = v / (1.0 - ADAM_B2 ** ADAM_STEP)
    delta = -ADAM_LR * (m_hat / (_jnp.sqrt(v_hat) + ADAM_EPS) + ADAM_WD * w)
    return delta, m, v


def reference(x, meta_tokens, norm1_g, w_in, conv_w, conv_b, m_gate_b, g_a2, g_a2_b, m_head_g, g_head_g, w_branch_m, w_branch_g, w_out, norm2_g, w_ff_gate, w_ff_up, w_ff_down, final_g, loss_target, m_meta_tokens, m_norm1_g, m_w_in, m_conv_w, m_conv_b, m_m_gate_b, m_g_a2, m_g_a2_b, m_m_head_g, m_g_head_g, m_w_branch_m, m_w_branch_g, m_w_out, m_norm2_g, m_w_ff_gate, m_w_ff_up, m_w_ff_down, m_final_g, v_meta_tokens, v_norm1_g, v_w_in, v_conv_w, v_conv_b, v_m_gate_b, v_g_a2, v_g_a2_b, v_m_head_g, v_g_head_g, v_w_branch_m, v_w_branch_g, v_w_out, v_norm2_g, v_w_ff_gate, v_w_ff_up, v_w_ff_down, v_final_g):
    given = dict(x=x, meta_tokens=meta_tokens, norm1_g=norm1_g, w_in=w_in, conv_w=conv_w, conv_b=conv_b, m_gate_b=m_gate_b, g_a2=g_a2, g_a2_b=g_a2_b, m_head_g=m_head_g, g_head_g=g_head_g, w_branch_m=w_branch_m, w_branch_g=w_branch_g, w_out=w_out, norm2_g=norm2_g, w_ff_gate=w_ff_gate, w_ff_up=w_ff_up, w_ff_down=w_ff_down, final_g=final_g, loss_target=loss_target, m_meta_tokens=m_meta_tokens, m_norm1_g=m_norm1_g, m_w_in=m_w_in, m_conv_w=m_conv_w, m_conv_b=m_conv_b, m_m_gate_b=m_m_gate_b, m_g_a2=m_g_a2, m_g_a2_b=m_g_a2_b, m_m_head_g=m_m_head_g, m_g_head_g=m_g_head_g, m_w_branch_m=m_w_branch_m, m_w_branch_g=m_w_branch_g, m_w_out=m_w_out, m_norm2_g=m_norm2_g, m_w_ff_gate=m_w_ff_gate, m_w_ff_up=m_w_ff_up, m_w_ff_down=m_w_ff_down, m_final_g=m_final_g, v_meta_tokens=v_meta_tokens, v_norm1_g=v_norm1_g, v_w_in=v_w_in, v_conv_w=v_conv_w, v_conv_b=v_conv_b, v_m_gate_b=v_m_gate_b, v_g_a2=v_g_a2, v_g_a2_b=v_g_a2_b, v_m_head_g=v_m_head_g, v_g_head_g=v_g_head_g, v_w_branch_m=v_w_branch_m, v_w_branch_g=v_w_branch_g, v_w_out=v_w_out, v_norm2_g=v_norm2_g, v_w_ff_gate=v_w_ff_gate, v_w_ff_up=v_w_ff_up, v_w_ff_down=v_w_ff_down, v_final_g=v_final_g)
    weights = {n: given[n] for n in TWIN_WEIGHTS}
    shared = {n: given[n] for n in SHARED_INPUTS}
    per_example = {n: given[n] for n in ['x']}
    grad_fn = _jax.value_and_grad(_loss, argnums=(0, 1))

    def one_microbatch(ex, loss_target):
        ex = dict(ex)
        diff = ex.pop(TWIN_DIFF_INPUT)
        return grad_fn(weights, diff, {**shared, **ex}, loss_target)

    if N_MICROBATCH == 1:
        loss, (grad_w, grad_x) = one_microbatch(per_example, given["loss_target"])
    else:
        def body(carry, xs):
            loss_sum, grad_sum = carry
            l_k, (gw_k, gx_k) = one_microbatch(xs[0], xs[1])
            with _jax.named_scope("update"):
                return (loss_sum + l_k, _jax.tree.map(_jnp.add, grad_sum, gw_k)), gx_k

        init = (_jnp.zeros((), _jnp.float32), _jax.tree.map(_jnp.zeros_like, weights))
        (loss, grad_w), grad_x = _jax.lax.scan(body, init, (per_example, given["loss_target"]))
    with _jax.named_scope("update"):
        delta_w, new_m, new_v = {}, {}, {}
        for n in TWIN_WEIGHTS:
            delta_w[n], new_m[n], new_v[n] = _adamw(weights[n], grad_w[n], given["m_" + n], given["v_" + n])
    return (loss, grad_x, *[grad_w[n] for n in TWIN_WEIGHTS], *[delta_w[n] for n in TWIN_WEIGHTS],
            *[new_m[n] for n in TWIN_WEIGHTS], *[new_v[n] for n in TWIN_WEIGHTS])
```

```python
import functools

import jax
import jax.numpy as jnp
from jax import lax
from jax.experimental import pallas as pl
from jax.experimental.pallas import tpu as pltpu

F32 = jnp.float32
BF = jnp.bfloat16
HI = lax.Precision.HIGHEST
MESH = pl.DeviceIdType.MESH

D = 1024
N_META = 16
CHUNK = 64
EPS = 1e-6
NH = 4
DV = 256
DQK = 128
G_RANK = 16
G_TAU = 16.0
D_FF = 2816
TM = 512
FIRST_VALID = TM - N_META
CPB = TM // CHUNK
NEG = -1e30
N_BIG = 8192
N_SMALL = 128
N_ALL = N_BIG + N_SMALL
VMEM_LIMIT = 56 * 1024 * 1024

ADAM_LR, ADAM_B1, ADAM_B2, ADAM_EPS, ADAM_WD, ADAM_STEP = 0.001, 0.9, 0.999, 1e-08, 0.01, 10

NT_DIMS = (((1,), (1,)), ((), ()))
TN_DIMS = (((0,), (0,)), ((), ()))


def _nt(a, b, **kw):
    return lax.dot_general(a, b, NT_DIMS, preferred_element_type=F32, **kw)


def _tn(a, b, **kw):
    return lax.dot_general(a, b, TN_DIMS, preferred_element_type=F32, **kw)


def _nn(a, b, **kw):
    return jnp.dot(a, b, preferred_element_type=F32, **kw)


def _params(**kw):
    return pltpu.CompilerParams(vmem_limit_bytes=VMEM_LIMIT, **kw)


def _sigmoid(x):
    return 1.0 / (1.0 + jnp.exp(-x))


def _logsig(x):
    return jnp.minimum(x, 0.0) - jnp.log(1.0 + jnp.exp(-jnp.abs(x)))


def _mm(a, b, *, nt, out_dtype, tn, tk=None, tm=TM, addend=None, name):
    m, k = a.shape
    n = b.shape[0] if nt else b.shape[1]
    tk = k if tk is None else tk
    nk = k // tk
    assert m % tm == 0 and n % tn == 0 and k % tk == 0
    dims = NT_DIMS if nt else (((1,), (0,)), ((), ()))

    def body(*refs):
        if addend is None:
            a_ref, b_ref, o_ref, acc_ref = refs
        else:
            a_ref, b_ref, add_ref, o_ref, acc_ref = refs
        kk = pl.program_id(2)
        part = lax.dot_general(a_ref[...].astype(BF), b_ref[...].astype(BF), dims, preferred_element_type=F32)

        @pl.when(kk == 0)
        def _():
            acc_ref[...] = part

        @pl.when(kk > 0)
        def _():
            acc_ref[...] += part

        @pl.when(kk == nk - 1)
        def _():
            r = acc_ref[...]
            if addend is not None:
                r = r + add_ref[...].astype(F32)
            o_ref[...] = r.astype(o_ref.dtype)

    in_specs = [pl.BlockSpec((tm, tk), lambda j, i, kk: (i, kk)),
                pl.BlockSpec((tn, tk), lambda j, i, kk: (j, kk)) if nt else pl.BlockSpec((tk, tn), lambda j, i, kk: (kk, j))]
    args = [a, b]
    if addend is not None:
        in_specs.append(pl.BlockSpec((tm, tn), lambda j, i, kk: (i, j)))
        args.append(addend)
    return pl.pallas_call(
        body, grid=(n // tn, m // tm, nk), in_specs=in_specs,
        out_specs=pl.BlockSpec((tm, tn), lambda j, i, kk: (i, j)),
        out_shape=jax.ShapeDtypeStruct((m, n), out_dtype),
        scratch_shapes=[pltpu.VMEM((tm, tn), F32)], compiler_params=_params(), name=name)(*args)


def _mm_tn(a, b, *, tm, tn, tk=TM, name):
    t, m = a.shape
    n = b.shape[1]
    assert t % tk == 0 and m % tm == 0 and n % tn == 0

    def body(a_ref, b_ref, o_ref):
        part = _tn(a_ref[...].astype(BF), b_ref[...].astype(BF))

        @pl.when(pl.program_id(2) == 0)
        def _():
            o_ref[...] = part

        @pl.when(pl.program_id(2) > 0)
        def _():
            o_ref[...] += part

    return pl.pallas_call(
        body, grid=(m // tm, n // tn, t // tk),
        in_specs=[pl.BlockSpec((tk, tm), lambda i, j, kk: (kk, i)), pl.BlockSpec((tk, tn), lambda i, j, kk: (kk, j))],
        out_specs=pl.BlockSpec((tm, tn), lambda i, j, kk: (i, j)),
        out_shape=jax.ShapeDtypeStruct((m, n), F32), compiler_params=_params(), name=name)(a, b)


def _row_spec(width, col=0):
    return pl.BlockSpec((TM, width), lambda i: (i, col))


def _full_spec(shape):
    return pl.BlockSpec(shape, lambda i: (0,) * len(shape))


def _rms_fwd(h, g, name):
    tp = h.shape[0]

    def body(h_ref, g_ref, xn_ref, r_ref):
        x = h_ref[...]
        r = lax.rsqrt(jnp.mean(x * x, axis=1, keepdims=True) + EPS)
        xn_ref[...] = (x * r * g_ref[...]).astype(BF)
        r_ref[...] = r

    return pl.pallas_call(
        body, grid=(tp // TM,), in_specs=[_row_spec(D), _full_spec((1, D))],
        out_specs=[_row_spec(D), _row_spec(1)],
        out_shape=(jax.ShapeDtypeStruct((tp, D), BF), jax.ShapeDtypeStruct((tp, 1), F32)),
        compiler_params=_params(), name=name)(h, g)


def _rms_bwd(dxn, h, rstd, g, dres, name):
    tp = h.shape[0]

    def body(dxn_ref, h_ref, r_ref, g_ref, dres_ref, dh_ref, dg_ref):
        r = r_ref[...]
        xh = h_ref[...] * r
        dxn_v = dxn_ref[...].astype(F32)
        dxh = dxn_v * g_ref[...]
        dh = r * (dxh - xh * jnp.mean(dxh * xh, axis=1, keepdims=True))
        dh_ref[...] = dh + dres_ref[...]
        part = jnp.sum(dxn_v * xh, axis=0, keepdims=True)

        @pl.when(pl.program_id(0) == 0)
        def _():
            dg_ref[...] = part

        @pl.when(pl.program_id(0) > 0)
        def _():
            dg_ref[...] += part

    return pl.pallas_call(
        body, grid=(tp // TM,),
        in_specs=[_row_spec(D), _row_spec(D), _row_spec(1), _full_spec((1, D)), _row_spec(D)],
        out_specs=[_row_spec(D), _full_spec((1, D))],
        out_shape=(jax.ShapeDtypeStruct((tp, D), F32), jax.ShapeDtypeStruct((1, D), F32)),
        compiler_params=_params(), name=name)(dxn, h, rstd, g, dres)


def _final_loss(h2, target, gf, name):
    tp = h2.shape[0]

    def body(h_ref, t_ref, g_ref, dh_ref, loss_ref, dg_ref):
        i = pl.program_id(0)
        live = (i > 0).astype(F32)
        x = h_ref[...]
        r = lax.rsqrt(jnp.mean(x * x, axis=1, keepdims=True) + EPS)
        xh = x * r
        e = xh * g_ref[...] - t_ref[...]
        row_loss = jnp.mean(e * e, axis=1, keepdims=True)
        loss_part = 0.5 * live * jnp.sum(row_loss, axis=0, keepdims=True)
        dout = e * (live / D)
        dg_part = jnp.sum(dout * xh, axis=0, keepdims=True)
        dxh = dout * g_ref[...]
        dh_ref[...] = r * (dxh - xh * jnp.mean(dxh * xh, axis=1, keepdims=True))

        @pl.when(i == 0)
        def _():
            loss_ref[...] = loss_part
            dg_ref[...] = dg_part

        @pl.when(i > 0)
        def _():
            loss_ref[...] += loss_part
            dg_ref[...] += dg_part

    return pl.pallas_call(
        body, grid=(tp // TM,),
        in_specs=[_row_spec(D), pl.BlockSpec((TM, D), lambda i: (jnp.maximum(i - 1, 0), 0)), _full_spec((1, D))],
        out_specs=[_row_spec(D), _full_spec((1, 1)), _full_spec((1, D))],
        out_shape=(jax.ShapeDtypeStruct((tp, D), F32), jax.ShapeDtypeStruct((1, 1), F32), jax.ShapeDtypeStruct((1, D), F32)),
        compiler_params=_params(), name=name)(h2, target, gf)


def _shift_down(x, halo, k):
    rk = pltpu.roll(x, k, 0)
    io = lax.broadcasted_iota(jnp.int32, (8, x.shape[1]), 0)
    top = jnp.where(io < k, pltpu.roll(halo, k, 0), rk[0:8])
    return jnp.concatenate([top, rk[8:]], axis=0)


def _shift_up(x, nxt, k):
    n = x.shape[0]
    rk = pltpu.roll(x, n - k, 0)
    io = lax.broadcasted_iota(jnp.int32, (8, x.shape[1]), 0)
    bot = jnp.where(io >= 8 - k, pltpu.roll(nxt, 8 - k, 0), rk[n - 8:n])
    return jnp.concatenate([rk[:n - 8], bot], axis=0)


def _conv_pre(x, halo, w_ref, b_ref):
    c = x * w_ref[3:4, :] + b_ref[...]
    shifted = []
    for k in (1, 2, 3):
        s = _shift_down(x, halo, k)
        shifted.append(s)
        c = c + s * w_ref[3 - k:4 - k, :]
    return c, shifted


def _qk_scale():
    col = lax.broadcasted_iota(jnp.int32, (1, D), 1)
    return jnp.where(col < NH * DQK, DQK ** -0.5, 1.0).astype(F32)


def _halo_prev_spec():
    return pl.BlockSpec((8, D), lambda i: (jnp.maximum(i * (TM // 8) - 1, 0), 0))


def _conv_fwd(pbig, w, b, name):
    tp = pbig.shape[0]

    def body(x_ref, halo_ref, w_ref, b_ref, o_ref):
        x = x_ref[...].astype(F32)
        halo = jnp.where(pl.program_id(0) > 0, halo_ref[...].astype(F32), 0.0)
        c, _ = _conv_pre(x, halo, w_ref, b_ref)
        o_ref[...] = (c * _sigmoid(c) * _qk_scale()).astype(BF)

    return pl.pallas_call(
        body, grid=(tp // TM,),
        in_specs=[_row_spec(D, 0), _halo_prev_spec(), _full_spec((4, D)), _full_spec((1, D))],
        out_specs=_row_spec(D), out_shape=jax.ShapeDtypeStruct((tp, D), BF),
        compiler_params=_params(), name=name)(pbig, pbig, w, b)


def _conv_bwd_pre(dqk, pbig, w, b, name):
    tp = pbig.shape[0]

    def body(d_ref, x_ref, halo_ref, w_ref, b_ref, dc_ref, dwb_ref):
        x = x_ref[...].astype(F32)
        halo = jnp.where(pl.program_id(0) > 0, halo_ref[...].astype(F32), 0.0)
        c, shifted = _conv_pre(x, halo, w_ref, b_ref)
        sg = _sigmoid(c)
        dc = d_ref[...] * _qk_scale() * (sg * (1.0 + c * (1.0 - sg)))
        dc_ref[...] = dc
        taps = [shifted[2], shifted[1], shifted[0], x]
        rows = [jnp.sum(dc * t, axis=0, keepdims=True) for t in taps] + [jnp.sum(dc, axis=0, keepdims=True)]
        io = lax.broadcasted_iota(jnp.int32, (8, D), 0)
        part = jnp.zeros((8, D), F32)
        for r, v in enumerate(rows):
            part = jnp.where(io == r, v, part)

        @pl.when(pl.program_id(0) == 0)
        def _():
            dwb_ref[...] = part

        @pl.when(pl.program_id(0) > 0)
        def _():
            dwb_ref[...] += part

    return pl.pallas_call(
        body, grid=(tp // TM,),
        in_specs=[_row_spec(D), _row_spec(D, 0), _halo_prev_spec(), _full_spec((4, D)), _full_spec((1, D))],
        out_specs=[_row_spec(D), _full_spec((8, D))],
        out_shape=(jax.ShapeDtypeStruct((tp, D), F32), jax.ShapeDtypeStruct((8, D), F32)),
        compiler_params=_params(), name=name)(dqk, pbig, pbig, w, b)


def _conv_bwd_in(dc, w, name):
    tp = dc.shape[0]
    nb = tp // TM

    def body(d_ref, nxt_ref, w_ref, o_ref):
        d = d_ref[...]
        nxt = jnp.where(pl.program_id(0) < nb - 1, nxt_ref[...], 0.0)
        acc = d * w_ref[3:4, :]
        for k in (1, 2, 3):
            acc = acc + _shift_up(d, nxt, k) * w_ref[3 - k:4 - k, :]
        o_ref[...] = acc.astype(BF)

    return pl.pallas_call(
        body, grid=(nb,),
        in_specs=[_row_spec(D), pl.BlockSpec((8, D), lambda i: (jnp.minimum((i + 1) * (TM // 8), tp // 8 - 1), 0)),
                  _full_spec((4, D))],
        out_specs=_row_spec(D), out_shape=jax.ShapeDtypeStruct((tp, D), BF),
        compiler_params=_params(), name=name)(dc, dc, w)


def _merge_fwd(pm, pg, pbig, name):
    tp = pm.shape[0]

    def body(pm_ref, pg_ref, gm_ref, gg_ref, o_ref):
        o_ref[...] = (_sigmoid(gm_ref[...].astype(F32)) * pm_ref[...].astype(F32)
                      + _sigmoid(gg_ref[...].astype(F32)) * pg_ref[...].astype(F32)).astype(BF)

    return pl.pallas_call(
        body, grid=(tp // TM,), in_specs=[_row_spec(D), _row_spec(D), _row_spec(D, 6), _row_spec(D, 7)],
        out_specs=_row_spec(D), out_shape=jax.ShapeDtypeStruct((tp, D), BF),
        compiler_params=_params(), name=name)(pm, pg, pbig, pbig)


def _merge_bwd(dmerged, pm, pg, pbig, name):
    tp = pm.shape[0]

    def body(d_ref, pm_ref, pg_ref, gm_ref, gg_ref, dpm_ref, dpg_ref, dgm_ref, dgg_ref):
        d = d_ref[...].astype(F32)
        sm = _sigmoid(gm_ref[...].astype(F32))
        sg = _sigmoid(gg_ref[...].astype(F32))
        dpm_ref[...] = (d * sm).astype(BF)
        dpg_ref[...] = (d * sg).astype(BF)
        dgm_ref[...] = (d * pm_ref[...].astype(F32) * sm * (1.0 - sm)).astype(BF)
        dgg_ref[...] = (d * pg_ref[...].astype(F32) * sg * (1.0 - sg)).astype(BF)

    shp = jax.ShapeDtypeStruct((tp, D), BF)
    return pl.pallas_call(
        body, grid=(tp // TM,), in_specs=[_row_spec(D), _row_spec(D), _row_spec(D), _row_spec(D, 6), _row_spec(D, 7)],
        out_specs=[_row_spec(D)] * 4, out_shape=(shp,) * 4,
        compiler_params=_params(), name=name)(dmerged, pm, pg, pbig, pbig)


def _swiglu_fwd(gu, name):
    tp = gu.shape[0]

    def body(g_ref, u_ref, o_ref):
        g = g_ref[...].astype(F32)
        o_ref[...] = (g * _sigmoid(g) * u_ref[...].astype(F32)).astype(BF)

    return pl.pallas_call(
        body, grid=(tp // TM,), in_specs=[_row_spec(D_FF, 0), _row_spec(D_FF, 1)],
        out_specs=_row_spec(D_FF), out_shape=jax.ShapeDtypeStruct((tp, D_FF), BF),
        compiler_params=_params(), name=name)(gu, gu)


def _swiglu_bwd(dff, gu, name):
    tp = gu.shape[0]

    def body(d_ref, g_ref, u_ref, o_ref):
        d = d_ref[...].astype(F32)
        g = g_ref[...].astype(F32)
        u = u_ref[...].astype(F32)
        sg = _sigmoid(g)
        o_ref[:, 0:D_FF] = (d * u * sg * (1.0 + g * (1.0 - sg))).astype(BF)
        o_ref[:, D_FF:2 * D_FF] = (d * g * sg).astype(BF)

    return pl.pallas_call(
        body, grid=(tp // TM,), in_specs=[_row_spec(D_FF), _row_spec(D_FF, 0), _row_spec(D_FF, 1)],
        out_specs=_row_spec(2 * D_FF), out_shape=jax.ShapeDtypeStruct((tp, 2 * D_FF), BF),
        compiler_params=_params(), name=name)(dff, gu, gu)


def _adamw(w, g, m, v, name):
    rows, cols = w.shape
    tr = 128 if rows % 128 == 0 else rows

    def body(w_ref, g_ref, m_ref, v_ref, d_ref, nm_ref, nv_ref):
        gv = g_ref[...]
        nm = ADAM_B1 * m_ref[...] + (1.0 - ADAM_B1) * gv
        nv = ADAM_B2 * v_ref[...] + (1.0 - ADAM_B2) * (gv * gv)
        m_hat = nm / (1.0 - ADAM_B1 ** ADAM_STEP)
        v_hat = nv / (1.0 - ADAM_B2 ** ADAM_STEP)
        d_ref[...] = -ADAM_LR * (m_hat / (jnp.sqrt(v_hat) + ADAM_EPS) + ADAM_WD * w_ref[...])
        nm_ref[...] = nm
        nv_ref[...] = nv

    spec = pl.BlockSpec((tr, cols), lambda i: (i, 0))
    shp = jax.ShapeDtypeStruct((rows, cols), F32)
    return pl.pallas_call(body, grid=(rows // tr,), in_specs=[spec] * 4, out_specs=[spec] * 3,
                          out_shape=(shp,) * 3, compiler_params=_params(), name=name)(w, g, m, v)


def _row_tile(rows, cap=512):
    best = rows
    for cand in range(8, min(rows, cap) + 1, 8):
        if rows % cand == 0:
            best = cand
    return best


def _add2(a, b, name):
    rows, cols = a.shape
    tr = _row_tile(rows)

    def body(a_ref, b_ref, o_ref):
        o_ref[...] = a_ref[...] + b_ref[...]

    spec = pl.BlockSpec((tr, cols), lambda i: (i, 0))
    return pl.pallas_call(body, grid=(rows // tr,), in_specs=[spec] * 2, out_specs=spec,
                          out_shape=jax.ShapeDtypeStruct((rows, cols), F32), compiler_params=_params(), name=name)(a, b)


def _add4(r, name):
    _, rows, cols = r.shape
    tr = _row_tile(rows)

    def body(r_ref, o_ref):
        o_ref[...] = ((r_ref[0] + r_ref[1]) + r_ref[2]) + r_ref[3]

    return pl.pallas_call(body, grid=(rows // tr,), in_specs=[pl.BlockSpec((4, tr, cols), lambda i: (0, i, 0))],
                          out_specs=pl.BlockSpec((tr, cols), lambda i: (i, 0)),
                          out_shape=jax.ShapeDtypeStruct((rows, cols), F32), compiler_params=_params(), name=name)(r)


def _chunk_consts():
    r2 = lax.broadcasted_iota(jnp.int32, (CHUNK, CHUNK), 0)
    c2 = lax.broadcasted_iota(jnp.int32, (CHUNK, CHUNK), 1)
    tri = r2 >= c2
    return dict(tri=tri, tril_f=tri.astype(F32), triu_f=(r2 <= c2).astype(F32),
                lane=lax.broadcasted_iota(jnp.int32, (CHUNK, N_SMALL), 1),
                rowio=lax.broadcasted_iota(jnp.int32, (CHUNK, 1), 0),
                ones=jnp.ones((CHUNK, N_SMALL), F32))


def _valid_rows(block, c):
    row = block * TM + c * CHUNK + lax.broadcasted_iota(jnp.int32, (CHUNK, 1), 0)
    return row >= FIRST_VALID


def _col(x, lane, idx):
    return jnp.sum(jnp.where(lane == idx, x, 0.0), axis=1, keepdims=True)


def _last_row(x, rowio):
    return jnp.sum(jnp.where(rowio == CHUNK - 1, x, 0.0), axis=0, keepdims=True)


def _sum_all(x):
    return jnp.sum(jnp.sum(x, axis=1, keepdims=True), axis=0, keepdims=True)


def _headnorm_fwd(hm, gain, gate_act):
    rs = lax.rsqrt(jnp.mean(hm * hm, axis=1, keepdims=True) + EPS)
    return hm * rs * gain * gate_act


def _headnorm_bwd(dy, hm, gain, gate_act):
    rs = lax.rsqrt(jnp.mean(hm * hm, axis=1, keepdims=True) + EPS)
    xh = hm * rs
    dact = dy * xh * gain
    dgain = jnp.sum(dy * gate_act * xh, axis=0, keepdims=True)
    dxh = dy * gate_act * gain
    dhm = rs * (dxh - xh * jnp.mean(dxh * xh, axis=1, keepdims=True))
    return dhm, dact, dgain


def _mlstm_gates(sm, gbias, valid, k):
    pre = sm + gbias
    lf = jnp.where(valid, _logsig(pre), 0.0)
    b_all = _nn(k["tril_f"], lf, precision=HI)
    li_all = jnp.where(valid, pre, NEG)
    return pre, li_all, b_all


def _mlstm_head(h, qh, kh, vh, li_all, b_all, c_st, n_row, m11, k):
    lane, tri, rowio = k["lane"], k["tri"], k["rowio"]
    b_col = _col(b_all, lane, NH + h)
    li_col = _col(li_all, lane, h)
    sel = jnp.where(lane == h, 1.0, 0.0) - jnp.where(lane == NH + h, 1.0, 0.0)
    x = jnp.where(lane < NH, li_all, jnp.where(lane < 2 * NH, b_all, 0.0))
    ubc = _nt(sel, x, precision=HI)
    dmat = jnp.where(tri, b_col + ubc, NEG)
    m_row = jnp.maximum(b_col + m11, jnp.max(dmat, axis=1, keepdims=True))
    e = jnp.exp(dmat - m_row)
    w_mat = e * _nt(qh, kh)
    a = jnp.exp(b_col + m11 - m_row)
    cb = c_st.astype(BF)
    cq = _nt(qh, cb)
    qf = qh.astype(F32)
    nq = jnp.sum(qf * n_row, axis=1, keepdims=True)
    num = a * cq + _nn(w_mat.astype(BF), vh)
    den = a * nq + jnp.sum(w_mat, axis=1, keepdims=True)
    floor = jnp.exp(-m_row)
    r = jnp.maximum(jnp.abs(den), floor)
    hm = num / r
    g = _last_row(b_col, rowio)
    wlog = g - b_col + li_col
    m_new = jnp.maximum(g + m11, jnp.max(wlog, axis=0, keepdims=True))
    a_s = jnp.exp(g + m11 - m_new)
    w = jnp.exp(wlog - m_new)
    return dict(e=e, w_mat=w_mat, a=a, cb=cb, cq=cq, qf=qf, nq=nq, den=den, floor=floor, r=r, hm=hm,
                m_new=m_new, a_s=a_s, w=w)


def _mlstm_fwd(qk, pbig, small, gbias, headg, name):
    tp = qk.shape[0]
    nb = tp // TM

    def body(qk_ref, v_ref, mo_ref, sm_ref, gb_ref, hg_ref, y_ref, cs_ref, ns_ref, c_scr, n_scr):
        blk = pl.program_id(0)

        @pl.when(blk == 0)
        def _():
            c_scr[...] = jnp.zeros_like(c_scr)
            n_scr[...] = jnp.zeros_like(n_scr)

        k = _chunk_consts()
        io8 = lax.broadcasted_iota(jnp.int32, (8, DQK), 0)

        def chunk(c, carry):
            r0 = pl.multiple_of(c * CHUNK, CHUNK)
            rows = pl.ds(r0, CHUNK)
            valid = _valid_rows(blk, c)
            _, li_all, b_all = _mlstm_gates(sm_ref[rows, :], gb_ref[...], valid, k)
            for h in range(NH):
                qh = qk_ref[rows, h * DQK:(h + 1) * DQK]
                kh = qk_ref[rows, NH * DQK + h * DQK:NH * DQK + (h + 1) * DQK]
                vh = v_ref[rows, h * DV:(h + 1) * DV]
                c_st = c_scr[h]
                n_row = n_scr[h, 0:1, :]
                m11 = jnp.max(n_scr[h, 1:2, :], axis=1, keepdims=True)
                f = _mlstm_head(h, qh, kh, vh, li_all, b_all, c_st, n_row, m11, k)
                gate = _sigmoid(mo_ref[rows, h * DV:(h + 1) * DV].astype(F32))
                y_ref[rows, h * DV:(h + 1) * DV] = _headnorm_fwd(f["hm"], hg_ref[:, h * DV:(h + 1) * DV], gate).astype(BF)
                cs_ref[c, h] = f["cb"]
                ns_ref[c, h] = jnp.where(io8 == 0, n_row, jnp.where(io8 == 1, m11, 0.0))
                wk = f["w"] * kh.astype(F32)
                c_scr[h] = f["a_s"] * c_st + _tn(vh, wk.astype(BF))
                n_scr[h, 0:1, :] = f["a_s"] * n_row + jnp.sum(wk, axis=0, keepdims=True)
                n_scr[h, 1:2, :] = jnp.broadcast_to(f["m_new"], (1, DQK))
            return carry

        lax.fori_loop(0, CPB, chunk, 0)

    return pl.pallas_call(
        body, grid=(nb,),
        in_specs=[_row_spec(D), _row_spec(D, 1), _row_spec(D, 2), _row_spec(N_SMALL), _full_spec((1, N_SMALL)), _full_spec((1, D))],
        out_specs=[_row_spec(D), pl.BlockSpec((CPB, NH, DV, DQK), lambda i: (i, 0, 0, 0)),
                   pl.BlockSpec((CPB, NH, 8, DQK), lambda i: (i, 0, 0, 0))],
        out_shape=(jax.ShapeDtypeStruct((tp, D), BF), jax.ShapeDtypeStruct((tp // CHUNK, NH, DV, DQK), BF),
                   jax.ShapeDtypeStruct((tp // CHUNK, NH, 8, DQK), F32)),
        scratch_shapes=[pltpu.VMEM((NH, DV, DQK), F32), pltpu.VMEM((NH, 8, DQK), F32)],
        compiler_params=_params(), name=name)(qk, pbig, pbig, small, gbias, headg)


def _mlstm_bwd(dy, qk, pbig, small, gbias, headg, cs, ns, name):
    tp = qk.shape[0]
    nb = tp // TM

    def body(dy_ref, qk_ref, v_ref, mo_ref, sm_ref, gb_ref, hg_ref, cs_ref, ns_ref,
             dqk_ref, dv_ref, dmo_ref, dsm_ref, dgb_ref, dhg_ref, dc_scr, dn_scr):
        step = pl.program_id(0)
        blk = nb - 1 - step

        @pl.when(step == 0)
        def _():
            dc_scr[...] = jnp.zeros_like(dc_scr)
            dn_scr[...] = jnp.zeros_like(dn_scr)
            dgb_ref[...] = jnp.zeros_like(dgb_ref)
            dhg_ref[...] = jnp.zeros_like(dhg_ref)

        k = _chunk_consts()
        lane, rowio = k["lane"], k["rowio"]

        def chunk(cc, carry):
            c = CPB - 1 - cc
            r0 = pl.multiple_of(c * CHUNK, CHUNK)
            rows = pl.ds(r0, CHUNK)
            valid = _valid_rows(blk, c)
            pre, li_all, b_all = _mlstm_gates(sm_ref[rows, :], gb_ref[...], valid, k)
            dli_all = jnp.zeros((CHUNK, N_SMALL), F32)
            db_all = jnp.zeros((CHUNK, N_SMALL), F32)
            for h in range(NH):
                qh = qk_ref[rows, h * DQK:(h + 1) * DQK]
                kh = qk_ref[rows, NH * DQK + h * DQK:NH * DQK + (h + 1) * DQK]
                vh = v_ref[rows, h * DV:(h + 1) * DV]
                c_st = cs_ref[c, h].astype(F32)
                n_row = ns_ref[c, h, 0:1, :]
                m11 = jnp.max(ns_ref[c, h, 1:2, :], axis=1, keepdims=True)
                f = _mlstm_head(h, qh, kh, vh, li_all, b_all, c_st, n_row, m11, k)
                gain = hg_ref[:, h * DV:(h + 1) * DV]
                gate = _sigmoid(mo_ref[rows, h * DV:(h + 1) * DV].astype(F32))
                dhm, dgate, dgain = _headnorm_bwd(dy_ref[rows, h * DV:(h + 1) * DV].astype(F32), f["hm"], gain, gate)
                dmo_ref[rows, h * DV:(h + 1) * DV] = (dgate * gate * (1.0 - gate)).astype(BF)
                dhg_ref[:, h * DV:(h + 1) * DV] += dgain

                dc_new = dc_scr[h]
                dn_new = dn_scr[h]
                dcb = dc_new.astype(BF)
                kf = kh.astype(F32)
                a, w, r = f["a"], f["w"], f["r"]
                dnum = dhm / r
                dr = -jnp.sum(dhm * f["hm"], axis=1, keepdims=True) / r
                dden = jnp.where(jnp.abs(f["den"]) > f["floor"], dr * jnp.sign(f["den"]), 0.0)
                dnb = dnum.astype(BF)
                dw_mat = _nt(dnb, vh) + dden
                dsim = (f["e"] * dw_mat).astype(BF)
                gm = f["w_mat"] * dw_mat
                vdc = _nn(vh, dcb)
                kdc = _nt(kh, dcb)
                dv_ref[rows, h * DV:(h + 1) * DV] = (_tn(f["w_mat"].astype(BF), dnb) + w * kdc).astype(BF)
                adden = a * dden
                dqk_ref[rows, h * DQK:(h + 1) * DQK] = _nn(dsim, kh) + a * _nn(dnb, f["cb"]) + adden * n_row
                dqk_ref[rows, NH * DQK + h * DQK:NH * DQK + (h + 1) * DQK] = _tn(dsim, qh) + w * vdc + w * dn_new
                da = jnp.sum(dnum * f["cq"], axis=1, keepdims=True) + dden * f["nq"]
                dw = jnp.sum(vdc * kf, axis=1, keepdims=True) + jnp.sum(kf * dn_new, axis=1, keepdims=True)
                da_s = _sum_all(dc_new * c_st) + jnp.sum(dn_new * n_row, axis=1, keepdims=True)
                wdw = w * dw
                rs = jnp.sum(gm, axis=1, keepdims=True)
                cs_col = _col(_tn(gm, k["ones"], precision=HI), lane, 0)
                dg = f["a_s"] * da_s + jnp.sum(wdw, axis=0, keepdims=True)
                db = a * da + rs - cs_col - wdw + jnp.where(rowio == CHUNK - 1, dg, 0.0)
                dli_all = dli_all + jnp.where(lane == h, cs_col + wdw, 0.0)
                db_all = db_all + jnp.where(lane == NH + h, db, 0.0)
                dc_scr[h] = f["a_s"] * dc_new + _tn((a * dnum).astype(BF), qh)
                dn_scr[h] = f["a_s"] * dn_new + jnp.sum(adden * f["qf"], axis=0, keepdims=True)
            dlf_all = _nn(k["triu_f"], db_all, precision=HI)
            dsm = jnp.where(valid, dli_all + dlf_all * _sigmoid(-pre), 0.0)
            dsm = jnp.where(lane < 2 * NH, dsm, 0.0)
            dsm_ref[rows, :] = dsm
            dgb_ref[0:1, :] += jnp.sum(dsm, axis=0, keepdims=True)
            return carry

        lax.fori_loop(0, CPB, chunk, 0)

    rev = lambda col: (lambda i: (nb - 1 - i, col))
    rspec = lambda width, col=0: pl.BlockSpec((TM, width), rev(col))
    return pl.pallas_call(
        body, grid=(nb,),
        in_specs=[rspec(D), rspec(D), rspec(D, 1), rspec(D, 2), rspec(N_SMALL), _full_spec((1, N_SMALL)), _full_spec((1, D)),
                  pl.BlockSpec((CPB, NH, DV, DQK), lambda i: (nb - 1 - i, 0, 0, 0)),
                  pl.BlockSpec((CPB, NH, 8, DQK), lambda i: (nb - 1 - i, 0, 0, 0))],
        out_specs=[rspec(D), rspec(D), rspec(D), rspec(N_SMALL), _full_spec((8, N_SMALL)), _full_spec((1, D))],
        out_shape=(jax.ShapeDtypeStruct((tp, D), F32), jax.ShapeDtypeStruct((tp, D), BF), jax.ShapeDtypeStruct((tp, D), BF),
                   jax.ShapeDtypeStruct((tp, N_SMALL), F32), jax.ShapeDtypeStruct((8, N_SMALL), F32),
                   jax.ShapeDtypeStruct((1, D), F32)),
        scratch_shapes=[pltpu.VMEM((NH, DV, DQK), F32), pltpu.VMEM((NH, 1, DQK), F32)],
        compiler_params=_params(), name=name)(dy, qk, pbig, pbig, small, gbias, headg, cs, ns)


def _gla_loga(sm_ref, a2_ref, a2b_ref, blk):
    za = _nn(sm_ref[...].astype(BF), a2_ref[...]) + a2b_ref[...]
    row = blk * TM + lax.broadcasted_iota(jnp.int32, (TM, 1), 0)
    return za, jnp.where(row >= FIRST_VALID, _logsig(za) / G_TAU, 0.0)


def _gla_head(h, q_ref, k_ref, rows, bc, btot, k):
    sl = slice(h * DQK, (h + 1) * DQK)
    bch = bc[:, sl]
    bth = btot[:, sl]
    gq = q_ref[rows, h * DQK:(h + 1) * DQK].astype(F32)
    gk = k_ref[rows, NH * DQK + h * DQK:NH * DQK + (h + 1) * DQK].astype(F32)
    e_pos = jnp.exp(bch) * (DQK ** -0.5)
    e_neg = jnp.exp(-bch)
    e_end = jnp.exp(bth - bch)
    qd = gq * e_pos
    ki = gk * e_neg
    ke = gk * e_end
    att = jnp.where(k["tri"], _nt(qd.astype(BF), ki.astype(BF)), 0.0)
    return dict(e_pos=e_pos, e_neg=e_neg, e_end=e_end, qd=qd, ki=ki, ke=ke, att=att, decay=jnp.exp(bth))


def _gla_fwd(pbig, small, a2p, a2b, headg, name):
    tp = pbig.shape[0]
    nb = tp // TM

    def body(qk_ref, v_ref, gr_ref, sm_ref, a2_ref, a2b_ref, hg_ref, y_ref, ss_ref, s_scr, lg_scr):
        blk = pl.program_id(0)

        @pl.when(blk == 0)
        def _():
            s_scr[...] = jnp.zeros_like(s_scr)

        k = _chunk_consts()
        _, loga = _gla_loga(sm_ref, a2_ref, a2b_ref, blk)
        lg_scr[...] = loga

        def chunk(c, carry):
            r0 = pl.multiple_of(c * CHUNK, CHUNK)
            rows = pl.ds(r0, CHUNK)
            bc = _nn(k["tril_f"], lg_scr[rows, :], precision=HI)
            btot = _last_row(bc, k["rowio"])
            for h in range(NH):
                f = _gla_head(h, qk_ref, qk_ref, rows, bc, btot, k)
                vh = v_ref[rows, h * DV:(h + 1) * DV]
                s_st = s_scr[h]
                sb = s_st.astype(BF)
                qdb = f["qd"].astype(BF)
                o = _nn(f["att"].astype(BF), vh) + _nt(qdb, sb)
                gr = gr_ref[rows, h * DV:(h + 1) * DV].astype(F32)
                y_ref[rows, h * DV:(h + 1) * DV] = _headnorm_fwd(o, hg_ref[:, h * DV:(h + 1) * DV], gr * _sigmoid(gr)).astype(BF)
                ss_ref[c, h] = sb
                s_scr[h] = s_st * f["decay"] + _tn(vh, f["ke"].astype(BF))
            return carry

        lax.fori_loop(0, CPB, chunk, 0)

    return pl.pallas_call(
        body, grid=(nb,),
        in_specs=[_row_spec(D, 3), _row_spec(D, 4), _row_spec(D, 5), _row_spec(N_SMALL),
                  _full_spec((N_SMALL, NH * DQK)), _full_spec((1, NH * DQK)), _full_spec((1, D))],
        out_specs=[_row_spec(D), pl.BlockSpec((CPB, NH, DV, DQK), lambda i: (i, 0, 0, 0))],
        out_shape=(jax.ShapeDtypeStruct((tp, D), BF), jax.ShapeDtypeStruct((tp // CHUNK, NH, DV, DQK), BF)),
        scratch_shapes=[pltpu.VMEM((NH, DV, DQK), F32), pltpu.VMEM((TM, NH * DQK), F32)],
        compiler_params=_params(), name=name)(pbig, pbig, pbig, small, a2p, a2b, headg)


def _gla_bwd(dy, pbig, small, a2p, a2b, headg, ss, name):
    tp = pbig.shape[0]
    nb = tp // TM
    nqk = NH * DQK

    def body(dy_ref, qk_ref, v_ref, gr_ref, sm_ref, a2_ref, a2b_ref, hg_ref, ss_ref,
             dqk_ref, dv_ref, dgr_ref, dsm_ref, da2_ref, da2b_ref, dhg_ref, ds_scr, lg_scr, dza_scr):
        step = pl.program_id(0)
        blk = nb - 1 - step

        @pl.when(step == 0)
        def _():
            ds_scr[...] = jnp.zeros_like(ds_scr)
            da2_ref[...] = jnp.zeros_like(da2_ref)
            da2b_ref[...] = jnp.zeros_like(da2b_ref)
            dhg_ref[...] = jnp.zeros_like(dhg_ref)

        k = _chunk_consts()
        rowio = k["rowio"]
        za, loga = _gla_loga(sm_ref, a2_ref, a2b_ref, blk)
        lg_scr[...] = loga

        def chunk(cc, carry):
            c = CPB - 1 - cc
            r0 = pl.multiple_of(c * CHUNK, CHUNK)
            rows = pl.ds(r0, CHUNK)
            bc = _nn(k["tril_f"], lg_scr[rows, :], precision=HI)
            btot = _last_row(bc, rowio)
            dbc_parts = []
            for h in range(NH):
                f = _gla_head(h, qk_ref, qk_ref, rows, bc, btot, k)
                vh = v_ref[rows, h * DV:(h + 1) * DV]
                s_b = ss_ref[c, h]
                s_f = s_b.astype(F32)
                qdb = f["qd"].astype(BF)
                attb = f["att"].astype(BF)
                o = _nn(attb, vh) + _nt(qdb, s_b)
                gr = gr_ref[rows, h * DV:(h + 1) * DV].astype(F32)
                sg = _sigmoid(gr)
                gain = hg_ref[:, h * DV:(h + 1) * DV]
                do, dact, dgain = _headnorm_bwd(dy_ref[rows, h * DV:(h + 1) * DV].astype(F32), o, gain, gr * sg)
                dgr_ref[rows, h * DV:(h + 1) * DV] = (dact * sg * (1.0 + gr * (1.0 - sg))).astype(BF)
                dhg_ref[:, h * DV:(h + 1) * DV] += dgain

                ds_new = ds_scr[h]
                dsb = ds_new.astype(BF)
                dob = do.astype(BF)
                keb = f["ke"].astype(BF)
                dv_ref[rows, h * DV:(h + 1) * DV] = (_tn(attb, dob) + _nt(keb, dsb)).astype(BF)
                datt = jnp.where(k["tri"], _nt(dob, vh), 0.0).astype(BF)
                dqd = _nn(datt, f["ki"].astype(BF)) + _nn(dob, s_b)
                dki = _tn(datt, qdb)
                dke = _nn(vh, dsb)
                dqk_ref[rows, h * DQK:(h + 1) * DQK] = (dqd * f["e_pos"]).astype(BF)
                dqk_ref[rows, nqk + h * DQK:nqk + (h + 1) * DQK] = (dki * f["e_neg"] + dke * f["e_end"]).astype(BF)
                dke_ke = dke * f["ke"]
                dbtot = jnp.sum(dke_ke, axis=0, keepdims=True) + jnp.sum(ds_new * s_f, axis=0, keepdims=True) * f["decay"]
                dbc_parts.append(dqd * f["qd"] - dki * f["ki"] - dke_ke + jnp.where(rowio == CHUNK - 1, dbtot, 0.0))
                ds_scr[h] = ds_new * f["decay"] + _tn(dob, qdb)
            dbc = jnp.concatenate(dbc_parts, axis=1)
            dza_scr[rows, :] = _nn(k["triu_f"], dbc, precision=HI)
            return carry

        lax.fori_loop(0, CPB, chunk, 0)
        row = blk * TM + lax.broadcasted_iota(jnp.int32, (TM, 1), 0)
        dza = jnp.where(row >= FIRST_VALID, dza_scr[...] * (_sigmoid(-za) / G_TAU), 0.0)
        dzb = dza.astype(BF)
        dsm_ref[...] = _nt(dzb, a2_ref[...])
        da2_ref[...] += _tn(sm_ref[...].astype(BF), dzb)
        da2b_ref[...] += jnp.sum(dza, axis=0, keepdims=True)

    rspec = lambda width, col=0: pl.BlockSpec((TM, width), lambda i: (nb - 1 - i, col))
    return pl.pallas_call(
        body, grid=(nb,),
        in_specs=[rspec(D), rspec(D, 3), rspec(D, 4), rspec(D, 5), rspec(N_SMALL),
                  _full_spec((N_SMALL, nqk)), _full_spec((1, nqk)), _full_spec((1, D)),
                  pl.BlockSpec((CPB, NH, DV, DQK), lambda i: (nb - 1 - i, 0, 0, 0))],
        out_specs=[rspec(D), rspec(D), rspec(D), rspec(N_SMALL), _full_spec((N_SMALL, nqk)), _full_spec((1, nqk)),
                   _full_spec((1, D))],
        out_shape=(jax.ShapeDtypeStruct((tp, D), BF), jax.ShapeDtypeStruct((tp, D), BF), jax.ShapeDtypeStruct((tp, D), BF),
                   jax.ShapeDtypeStruct((tp, N_SMALL), F32), jax.ShapeDtypeStruct((N_SMALL, nqk), F32),
                   jax.ShapeDtypeStruct((1, nqk), F32), jax.ShapeDtypeStruct((1, D), F32)),
        scratch_shapes=[pltpu.VMEM((NH, DV, DQK), F32), pltpu.VMEM((TM, nqk), F32), pltpu.VMEM((TM, nqk), F32)],
        compiler_params=_params(), name=name)(dy, pbig, pbig, pbig, small, a2p, a2b, headg, ss)


ANY = pl.BlockSpec(memory_space=pl.ANY)


def _place():
    return lax.axis_index("x"), lax.axis_index("y"), lax.axis_index("c")


def _all_gather_chips(p, name):
    _, rh, cols = p.shape

    def body(p_ref, o_ref, send_sems, recv_sems, local_sem):
        x, y, c = _place()
        chips = [(1 - x, y), (x, 1 - y), (1 - x, 1 - y)]
        me = 2 * x + y

        def copy(kk, src, slot, to):
            return pltpu.make_async_remote_copy(src_ref=src, dst_ref=o_ref.at[slot, c],
                                                send_sem=send_sems.at[kk], recv_sem=recv_sems.at[kk],
                                                device_id=to, device_id_type=MESH)

        mine = pltpu.make_async_copy(p_ref, o_ref.at[me], local_sem)
        mine.start()
        first = [copy(j, p_ref.at[c], me, (*chip, c)) for j, chip in enumerate(chips)]
        for cp in first:
            cp.start()
        passed = []
        for j, (cx, cy) in enumerate(chips):
            slot = 2 * cx + cy
            copy(j, p_ref.at[c], slot, (cx, cy, c)).wait_recv()
            fwd = copy(3 + j, o_ref.at[slot, c], slot, (x, y, 1 - c))
            fwd.start()
            passed.append(fwd)
        for j, (cx, cy) in enumerate(chips):
            slot = 2 * cx + cy
            pltpu.make_async_remote_copy(src_ref=o_ref.at[slot, 1 - c], dst_ref=o_ref.at[slot, 1 - c],
                                         send_sem=send_sems.at[3 + j], recv_sem=recv_sems.at[3 + j],
                                         device_id=(x, y, 1 - c), device_id_type=MESH).wait_recv()
        for cp in first + passed:
            cp.wait_send()
        mine.wait()

    return pl.pallas_call(
        body, in_specs=[ANY], out_specs=ANY, out_shape=jax.ShapeDtypeStruct((4, 2, rh, cols), p.dtype),
        scratch_shapes=[pltpu.SemaphoreType.DMA((6,)), pltpu.SemaphoreType.DMA((6,)), pltpu.SemaphoreType.DMA],
        name=name)(p)


def _swap_halves(g, name):
    _, _, rh, cols = g.shape

    def body(g_ref, own_ref, got_ref, send_sem, recv_sem, local_sem):
        x, y, c = _place()
        keep = pltpu.make_async_copy(g_ref.at[c], own_ref, local_sem)
        keep.start()
        cp = pltpu.make_async_remote_copy(src_ref=g_ref.at[1 - c], dst_ref=got_ref, send_sem=send_sem, recv_sem=recv_sem,
                                          device_id=(x, y, 1 - c), device_id_type=MESH)
        cp.start()
        cp.wait()
        keep.wait()

    shp = jax.ShapeDtypeStruct((4, rh, cols), g.dtype)
    return pl.pallas_call(
        body, in_specs=[ANY], out_specs=[ANY, ANY], out_shape=(shp, shp),
        scratch_shapes=[pltpu.SemaphoreType.DMA, pltpu.SemaphoreType.DMA, pltpu.SemaphoreType.DMA], name=name)(g)


def _scatter_chips(s, name):
    _, rh, cols = s.shape

    def body(s_ref, o_ref, send_sems, recv_sems, local_sem):
        x, y, c = _place()
        chips = [(1 - x, y), (x, 1 - y), (1 - x, 1 - y)]
        me = 2 * x + y
        keep = pltpu.make_async_copy(s_ref.at[me], o_ref.at[me], local_sem)
        keep.start()
        cps = [pltpu.make_async_remote_copy(src_ref=s_ref.at[2 * cx + cy], dst_ref=o_ref.at[me],
                                            send_sem=send_sems.at[j], recv_sem=recv_sems.at[j],
                                            device_id=(cx, cy, c), device_id_type=MESH)
               for j, (cx, cy) in enumerate(chips)]
        for cp in cps:
            cp.start()
        for j, (cx, cy) in enumerate(chips):
            pltpu.make_async_remote_copy(src_ref=s_ref.at[me], dst_ref=o_ref.at[2 * cx + cy],
                                         send_sem=send_sems.at[j], recv_sem=recv_sems.at[j],
                                         device_id=(cx, cy, c), device_id_type=MESH).wait_recv()
        for cp in cps:
            cp.wait_send()
        keep.wait()

    return pl.pallas_call(
        body, in_specs=[ANY], out_specs=ANY, out_shape=jax.ShapeDtypeStruct(s.shape, s.dtype),
        scratch_shapes=[pltpu.SemaphoreType.DMA((3,)), pltpu.SemaphoreType.DMA((3,)), pltpu.SemaphoreType.DMA], name=name)(s)


def _join_halves(f, name):
    rh, cols = f.shape

    def body(f_ref, o_ref, send_sem, recv_sem, local_sem):
        x, y, c = _place()
        keep = pltpu.make_async_copy(f_ref, o_ref.at[c], local_sem)
        keep.start()
        cp = pltpu.make_async_remote_copy(src_ref=f_ref, dst_ref=o_ref.at[c], send_sem=send_sem, recv_sem=recv_sem,
                                          device_id=(x, y, 1 - c), device_id_type=MESH)
        cp.start()
        cp.wait_send()
        pltpu.make_async_remote_copy(src_ref=f_ref, dst_ref=o_ref.at[1 - c], send_sem=send_sem, recv_sem=recv_sem,
                                     device_id=(x, y, 1 - c), device_id_type=MESH).wait_recv()
        keep.wait()

    return pl.pallas_call(
        body, in_specs=[ANY], out_specs=ANY, out_shape=jax.ShapeDtypeStruct((2, rh, cols), f.dtype),
        scratch_shapes=[pltpu.SemaphoreType.DMA, pltpu.SemaphoreType.DMA, pltpu.SemaphoreType.DMA], name=name)(f)


PACK_COLS = 1024


def _pack(parts, row_multiple):
    flat = jnp.concatenate([p.reshape(-1) for p in parts])
    unit = 2 * row_multiple * PACK_COLS
    total = -(-flat.shape[0] // unit) * unit
    flat = jnp.pad(flat, (0, total - flat.shape[0]))
    return flat.reshape(2, total // (2 * PACK_COLS), PACK_COLS)


def _unpack(flat, shapes):
    out, off = [], 0
    for shp in shapes:
        n = 1
        for s in shp:
            n *= s
        out.append(flat[off:off + n].reshape(shp))
        off += n
    return out


BIG_SHAPES = [(D, 2054), (256, D), (256, D), (256, D), (D, 704), (D, 704), (704, D)]
SMALL_SHARD_SHAPES = [(N_META, 256), (4, 256), (G_RANK, 128), (NH, 64), (NH, 64)]
REPL_SHAPES = [(1, D), (1, D), (1, 2, NH), (1, NH * DQK), (1, D), (D,)]


def _cat_cols(shards):
    return jnp.concatenate(shards, axis=-1)


def _cat_rows(shards):
    return jnp.concatenate(shards, axis=0)


def kernel(x, meta_tokens, norm1_g, w_in, conv_w, conv_b, m_gate_b, g_a2, g_a2_b, m_head_g, g_head_g, w_branch_m, w_branch_g, w_out, norm2_g, w_ff_gate, w_ff_up, w_ff_down, final_g, loss_target, m_meta_tokens, m_norm1_g, m_w_in, m_conv_w, m_conv_b, m_m_gate_b, m_g_a2, m_g_a2_b, m_m_head_g, m_g_head_g, m_w_branch_m, m_w_branch_g, m_w_out, m_norm2_g, m_w_ff_gate, m_w_ff_up, m_w_ff_down, m_final_g, v_meta_tokens, v_norm1_g, v_w_in, v_conv_w, v_conv_b, v_m_gate_b, v_g_a2, v_g_a2_b, v_m_head_g, v_g_head_g, v_w_branch_m, v_w_branch_g, v_w_out, v_norm2_g, v_w_ff_gate, v_w_ff_up, v_w_ff_down, v_final_g):
    w = _gather_weights(w_in, w_branch_m, w_branch_g, w_out, w_ff_gate, w_ff_up, w_ff_down, meta_tokens, conv_w, g_a2, m_head_g, g_head_g)
    loss_local, dh0, local = _local_step(x[0], loss_target[0], w, norm1_g, conv_b, m_gate_b, g_a2_b, norm2_g, final_g)
    grads = _reduce_grads(local)

    weights = [w_in, w_branch_m, w_branch_g, w_out, w_ff_gate, w_ff_up, w_ff_down, meta_tokens, conv_w, g_a2, m_head_g, g_head_g,
               norm1_g, conv_b, m_gate_b, g_a2_b, norm2_g, final_g]
    moms = [m_w_in, m_w_branch_m, m_w_branch_g, m_w_out, m_w_ff_gate, m_w_ff_up, m_w_ff_down, m_meta_tokens, m_conv_w, m_g_a2,
            m_m_head_g, m_g_head_g, m_norm1_g, m_conv_b, m_m_gate_b, m_g_a2_b, m_norm2_g, m_final_g]
    vels = [v_w_in, v_w_branch_m, v_w_branch_g, v_w_out, v_w_ff_gate, v_w_ff_up, v_w_ff_down, v_meta_tokens, v_conv_w, v_g_a2,
            v_m_head_g, v_g_head_g, v_norm1_g, v_conv_b, v_m_gate_b, v_g_a2_b, v_norm2_g, v_final_g]
    res = {}
    for nm, wt, g, m, v in zip(PACK_ORDER, weights, grads, moms, vels):
        two_d = (wt.size // wt.shape[-1], wt.shape[-1])
        d, nm_, nv_ = _adamw(wt.reshape(two_d), g.reshape(two_d), m.reshape(two_d), v.reshape(two_d), "adamw_" + nm)
        res[nm] = (g.reshape(wt.shape), d.reshape(wt.shape), nm_.reshape(wt.shape), nv_.reshape(wt.shape))

    order = ["meta_tokens", "norm1_g", "w_in", "conv_w", "conv_b", "m_gate_b", "g_a2", "g_a2_b", "m_head_g", "g_head_g",
             "w_branch_m", "w_branch_g", "w_out", "norm2_g", "w_ff_gate", "w_ff_up", "w_ff_down", "final_g"]
    loss = lax.psum(loss_local[0, 0], ("x", "y", "c"))
    grad_x = dh0[TM:].reshape(x.shape)
    return (loss, grad_x, *[res[n][0] for n in order], *[res[n][1] for n in order],
            *[res[n][2] for n in order], *[res[n][3] for n in order])


PACK_ORDER = ["w_in", "w_branch_m", "w_branch_g", "w_out", "w_ff_gate", "w_ff_up", "w_ff_down", "meta_tokens", "conv_w", "g_a2",
              "m_head_g", "g_head_g", "norm1_g", "conv_b", "m_gate_b", "g_a2_b", "norm2_g", "final_g"]


def _gather_weights(w_in, w_branch_m, w_branch_g, w_out, w_ff_gate, w_ff_up, w_ff_down, meta_tokens, conv_w, g_a2, m_head_g, g_head_g):
    big_local =[w_in[0], w_branch_m[0], w_branch_g[0], w_out[0], w_ff_gate[0], w_ff_up[0], w_ff_down[0]]
    small_local = [meta_tokens, conv_w[0], g_a2[0], m_head_g[0], g_head_g[0]]
    big_all = _all_gather_chips(_pack([a.astype(BF) for a in big_local], 16), "gather_big")
    small_all = _all_gather_chips(_pack(small_local, 8), "gather_small")
    big_sh = [_unpack(big_all[q].reshape(-1), BIG_SHAPES) for q in range(4)]
    small_sh = [_unpack(small_all[q].reshape(-1), SMALL_SHARD_SHAPES) for q in range(4)]
    w_in_f = _cat_cols([s[0] for s in big_sh])
    wbm = _cat_rows([s[1] for s in big_sh])
    wbg = _cat_rows([s[2] for s in big_sh])
    wout = _cat_rows([s[3] for s in big_sh])
    wgu = jnp.concatenate([_cat_cols([s[4] for s in big_sh]), _cat_cols([s[5] for s in big_sh])], axis=1)
    wdown = _cat_rows([s[6] for s in big_sh])
    meta_f = _cat_cols([s[0] for s in small_sh])
    convw_f = _cat_cols([s[1] for s in small_sh])
    ga2_f = _cat_cols([s[2] for s in small_sh])
    mhg_f = _cat_cols([s[3] for s in small_sh]).reshape(1, D)
    ghg_f = _cat_cols([s[4] for s in small_sh]).reshape(1, D)
    return dict(w_in=w_in_f, wbm=wbm, wbg=wbg, wout=wout, wgu=wgu, wdown=wdown, meta=meta_f, convw=convw_f, ga2=ga2_f,
                mhg=mhg_f, ghg=ghg_f)


def _local_step(x0, target, w, norm1_g, conv_b, m_gate_b, g_a2_b, norm2_g, final_g):
    w_in_f, wbm, wbg, wout, wgu, wdown = w["w_in"], w["wbm"], w["wbg"], w["wout"], w["wgu"], w["wdown"]
    meta_f, convw_f, ga2_f, mhg_f, ghg_f = w["meta"], w["convw"], w["ga2"], w["mhg"], w["ghg"]
    w_big = jnp.concatenate([w_in_f[:, 0:2048], w_in_f[:, 2056:5128], w_in_f[:, 5144:8216]], axis=1)
    w_small = jnp.concatenate([w_in_f[:, 2048:2056], w_in_f[:, 5128:5144], jnp.zeros((D, N_SMALL - 24), BF)], axis=1)
    w_all = jnp.concatenate([w_big, w_small], axis=1)
    gbias = jnp.concatenate([m_gate_b.reshape(1, 2 * NH), jnp.zeros((1, N_SMALL - 2 * NH), F32)], axis=1)
    a2p = jnp.concatenate([jnp.zeros((8, NH * DQK), F32), ga2_f, jnp.zeros((N_SMALL - 24, NH * DQK), F32)], axis=0).astype(BF)
    convb = conv_b.reshape(1, D)
    g1 = norm1_g.reshape(1, D)
    g2 = norm2_g.reshape(1, D)
    gf = final_g.reshape(1, D)
    h0 = jnp.concatenate([jnp.zeros((FIRST_VALID, D), F32), meta_f, x0], axis=0)

    xn1, rstd1 = _rms_fwd(h0, g1, "rms1")
    pbig = _mm(xn1, w_big, nt=False, out_dtype=BF, tn=1024, name="proj_big")
    small = _mm(xn1, w_small, nt=False, out_dtype=F32, tn=N_SMALL, name="proj_small")
    qk = _conv_fwd(pbig, convw_f, convb, "conv_fwd")
    y_m, m_cs, m_ns = _mlstm_fwd(qk, pbig, small, gbias, mhg_f, "mlstm_fwd")
    y_g, g_ss = _gla_fwd(pbig, small, a2p, g_a2_b, ghg_f, "gla_fwd")
    p_m = _mm(y_m, wbm, nt=False, out_dtype=BF, tn=1024, name="branch_m")
    p_g = _mm(y_g, wbg, nt=False, out_dtype=BF, tn=1024, name="branch_g")
    merged = _merge_fwd(p_m, p_g, pbig, "merge_fwd")
    h1 = _mm(merged, wout, nt=False, out_dtype=F32, tn=1024, addend=h0, name="out_proj")
    hn, rstd2 = _rms_fwd(h1, g2, "rms2")
    gu = _mm(hn, wgu, nt=False, out_dtype=BF, tn=1408, name="ff_in")
    ff = _swiglu_fwd(gu, "swiglu_fwd")
    h2 = _mm(ff, wdown, nt=False, out_dtype=F32, tn=1024, addend=h1, name="ff_down")
    dh2, loss_local, d_final_g = _final_loss(h2, target, gf, "final_loss")

    dff = _mm(dh2, wdown, nt=True, out_dtype=BF, tn=1408, name="d_ff")
    d_wdown = _mm_tn(ff, dh2, tm=1408, tn=1024, name="dw_ff_down")
    dgu = _swiglu_bwd(dff, gu, "swiglu_bwd")
    dhn = _mm(dgu, wgu, nt=True, out_dtype=F32, tn=1024, tk=2816, name="d_hn")
    d_wgu = _mm_tn(hn, dgu, tm=1024, tn=1408, name="dw_ff_in")
    dh1, d_g2 = _rms_bwd(dhn, h1, rstd2, g2, dh2, "rms2_bwd")
    dmerged = _mm(dh1, wout, nt=True, out_dtype=BF, tn=1024, name="d_merged")
    d_wout = _mm_tn(merged, dh1, tm=1024, tn=1024, name="dw_out")
    dp_m, dp_g, dgm, dgg = _merge_bwd(dmerged, p_m, p_g, pbig, "merge_bwd")
    dy_m = _mm(dp_m, wbm, nt=True, out_dtype=BF, tn=1024, name="d_ym")
    dy_g = _mm(dp_g, wbg, nt=True, out_dtype=BF, tn=1024, name="d_yg")
    d_wbm = _mm_tn(y_m, dp_m, tm=1024, tn=1024, name="dw_branch_m")
    d_wbg = _mm_tn(y_g, dp_g, tm=1024, tn=1024, name="dw_branch_g")
    dqk_m, dv_m, dmo, dsm_m, d_gbias, d_mhg = _mlstm_bwd(dy_m, qk, pbig, small, gbias, mhg_f, m_cs, m_ns, "mlstm_bwd")
    dconv, d_convwb = _conv_bwd_pre(dqk_m, pbig, convw_f, convb, "conv_bwd_pre")
    dqk_pre = _conv_bwd_in(dconv, convw_f, "conv_bwd_in")
    dqk_g, dv_g, dgr, dsm_g, d_a2p, d_a2b, d_ghg = _gla_bwd(dy_g, pbig, small, a2p, g_a2_b, ghg_f, g_ss, "gla_bwd")
    dsmall = _add2(dsm_m, dsm_g, "dsmall_add").astype(BF)
    dproj = jnp.concatenate([dqk_pre, dv_m, dmo, dqk_g, dv_g, dgr, dgm, dgg, dsmall], axis=1)
    dxn = _mm(dproj, w_all, nt=True, out_dtype=F32, tn=1024, tk=1664, name="d_xn")
    d_wall = _mm_tn(xn1, dproj, tm=1024, tn=1664, name="dw_in")
    dh0, d_g1 = _rms_bwd(dxn, h0, rstd1, g1, dh1, "rms1_bwd")

    d_win = jnp.concatenate([d_wall[:, 0:2048], d_wall[:, N_BIG:N_BIG + 8], d_wall[:, 2048:5120],
                             d_wall[:, N_BIG + 8:N_BIG + 24], d_wall[:, 5120:N_BIG]], axis=1)
    local = [d_win, d_wbm, d_wbg, d_wout, d_wgu[:, 0:D_FF], d_wgu[:, D_FF:2 * D_FF], d_wdown,
             dh0[FIRST_VALID:TM], d_convwb[0:4], d_a2p[8:24], d_mhg.reshape(NH, DV), d_ghg.reshape(NH, DV),
             d_g1, d_convwb[4:5], d_gbias[0:1, 0:2 * NH].reshape(1, 2, NH), d_a2b, d_g2, d_final_g.reshape(D)]
    return loss_local, dh0, local


def _reduce_grads(local):
    n_sharded = len(BIG_SHAPES) + len(SMALL_SHARD_SHAPES)
    payload = []
    for q in range(4):
        parts = []
        for g, shp in zip(local[:n_sharded], BIG_SHAPES + SMALL_SHARD_SHAPES):
            if g.shape[0] == shp[0]:
                parts.append(g[:, q * shp[1]:(q + 1) * shp[1]])
            else:
                parts.append(g[q * shp[0]:(q + 1) * shp[0]])
        payload.append(_pack(parts + local[n_sharded:], 8))
    g_all = jnp.stack(payload, axis=1)
    rh = g_all.shape[2]
    own, got = _swap_halves(g_all, "reduce_siblings")
    chip_sum = _add2(own.reshape(4 * rh, PACK_COLS), got.reshape(4 * rh, PACK_COLS), "reduce_add2").reshape(4, rh, PACK_COLS)
    from_chips = _scatter_chips(chip_sum, "reduce_chips")
    half = _add4(from_chips, "reduce_add4")
    full = _join_halves(half, "reduce_join").reshape(-1)
    return _unpack(full, BIG_SHAPES + SMALL_SHARD_SHAPES + REPL_SHAPES)
```

```python
import functools

import jax
import jax.numpy as jnp
from jax import lax
from jax.experimental import pallas as pl
from jax.experimental.pallas import tpu as pltpu

F32 = jnp.float32
BF = jnp.bfloat16
HI = lax.Precision.HIGHEST
MESH = pl.DeviceIdType.MESH

D = 1024
N_META = 16
CHUNK = 64
EPS = 1e-6
NH = 4
DV = 256
DQK = 128
G_RANK = 16
G_TAU = 16.0
D_FF = 2816
TM = 512
FIRST_VALID = TM - N_META
CPB = TM // CHUNK
NEG = -1e30
N_BIG = 8192
CB_GQK, CB_GV, CB_GR, CB_MQK, CB_GM, CB_GG, CB_MV, CB_MO = range(8)
N_SMALL = 128
N_ALL = N_BIG + N_SMALL
VMEM_LIMIT = 56 * 1024 * 1024

ADAM_LR, ADAM_B1, ADAM_B2, ADAM_EPS, ADAM_WD, ADAM_STEP = 0.001, 0.9, 0.999, 1e-08, 0.01, 10

NT_DIMS = (((1,), (1,)), ((), ()))
TN_DIMS = (((0,), (0,)), ((), ()))


def _nt(a, b, **kw):
    return lax.dot_general(a, b, NT_DIMS, preferred_element_type=F32, **kw)


def _tn(a, b, **kw):
    return lax.dot_general(a, b, TN_DIMS, preferred_element_type=F32, **kw)


def _nn(a, b, **kw):
    return jnp.dot(a, b, preferred_element_type=F32, **kw)


def _params(**kw):
    return pltpu.CompilerParams(vmem_limit_bytes=VMEM_LIMIT, **kw)


def _sigmoid(x):
    return 1.0 / (1.0 + jnp.exp(-x))


def _logsig(x):
    return jnp.minimum(x, 0.0) - jnp.log(1.0 + jnp.exp(-jnp.abs(x)))


def _mm(a, b, *, nt, out_dtype, tn, tk=None, tm=TM, addend=None, name):
    m, k = a.shape
    n = b.shape[0] if nt else b.shape[1]
    tk = k if tk is None else tk
    nk = k // tk
    assert m % tm == 0 and n % tn == 0 and k % tk == 0
    dims = NT_DIMS if nt else (((1,), (0,)), ((), ()))

    def body(*refs):
        if addend is None:
            a_ref, b_ref, o_ref, acc_ref = refs
        else:
            a_ref, b_ref, add_ref, o_ref, acc_ref = refs
        kk = pl.program_id(2)
        part = lax.dot_general(a_ref[...].astype(BF), b_ref[...].astype(BF), dims, preferred_element_type=F32)

        @pl.when(kk == 0)
        def _():
            acc_ref[...] = part

        @pl.when(kk > 0)
        def _():
            acc_ref[...] += part

        @pl.when(kk == nk - 1)
        def _():
            r = acc_ref[...]
            if addend is not None:
                r = r + add_ref[...].astype(F32)
            o_ref[...] = r.astype(o_ref.dtype)

    in_specs = [pl.BlockSpec((tm, tk), lambda j, i, kk: (i, kk)),
                pl.BlockSpec((tn, tk), lambda j, i, kk: (j, kk)) if nt else pl.BlockSpec((tk, tn), lambda j, i, kk: (kk, j))]
    args = [a, b]
    if addend is not None:
        in_specs.append(pl.BlockSpec((tm, tn), lambda j, i, kk: (i, j)))
        args.append(addend)
    return pl.pallas_call(
        body, grid=(n // tn, m // tm, nk), in_specs=in_specs,
        out_specs=pl.BlockSpec((tm, tn), lambda j, i, kk: (i, j)),
        out_shape=jax.ShapeDtypeStruct((m, n), out_dtype),
        scratch_shapes=[pltpu.VMEM((tm, tn), F32)], compiler_params=_params(), name=name)(*args)


def _mm_tn(a, b, *, tm, tn, tk=TM, name):
    t, m = a.shape
    n = b.shape[1]
    assert t % tk == 0 and m % tm == 0 and n % tn == 0

    def body(a_ref, b_ref, o_ref):
        part = _tn(a_ref[...].astype(BF), b_ref[...].astype(BF))

        @pl.when(pl.program_id(2) == 0)
        def _():
            o_ref[...] = part

        @pl.when(pl.program_id(2) > 0)
        def _():
            o_ref[...] += part

    return pl.pallas_call(
        body, grid=(m // tm, n // tn, t // tk),
        in_specs=[pl.BlockSpec((tk, tm), lambda i, j, kk: (kk, i)), pl.BlockSpec((tk, tn), lambda i, j, kk: (kk, j))],
        out_specs=pl.BlockSpec((tm, tn), lambda i, j, kk: (i, j)),
        out_shape=jax.ShapeDtypeStruct((m, n), F32), compiler_params=_params(), name=name)(a, b)


ANY = pl.BlockSpec(memory_space=pl.ANY)


def _row_spec(width, col=0):
    return pl.BlockSpec((TM, width), lambda i: (i, col))


def _full_spec(shape):
    return pl.BlockSpec(shape, lambda i: (0,) * len(shape))


def _rms_fwd(h, g, name):
    tp = h.shape[0]

    def body(h_ref, g_ref, xn_ref, r_ref):
        x = h_ref[...]
        r = lax.rsqrt(jnp.mean(x * x, axis=1, keepdims=True) + EPS)
        xn_ref[...] = (x * r * g_ref[...]).astype(BF)
        r_ref[...] = r

    return pl.pallas_call(
        body, grid=(tp // TM,), in_specs=[_row_spec(D), _full_spec((1, D))],
        out_specs=[_row_spec(D), _row_spec(1)],
        out_shape=(jax.ShapeDtypeStruct((tp, D), BF), jax.ShapeDtypeStruct((tp, 1), F32)),
        compiler_params=_params(), name=name)(h, g)


def _rms_bwd(dxn, h, rstd, g, dres, name):
    tp = h.shape[0]

    def body(dxn_ref, h_ref, r_ref, g_ref, dres_ref, dh_ref, dg_ref):
        r = r_ref[...]
        xh = h_ref[...] * r
        dxn_v = dxn_ref[...].astype(F32)
        dxh = dxn_v * g_ref[...]
        dh = r * (dxh - xh * jnp.mean(dxh * xh, axis=1, keepdims=True))
        dh_ref[...] = dh + dres_ref[...]
        part = jnp.sum(dxn_v * xh, axis=0, keepdims=True)

        @pl.when(pl.program_id(0) == 0)
        def _():
            dg_ref[...] = part

        @pl.when(pl.program_id(0) > 0)
        def _():
            dg_ref[...] += part

    return pl.pallas_call(
        body, grid=(tp // TM,),
        in_specs=[_row_spec(D), _row_spec(D), _row_spec(1), _full_spec((1, D)), _row_spec(D)],
        out_specs=[_row_spec(D), _full_spec((1, D))],
        out_shape=(jax.ShapeDtypeStruct((tp, D), F32), jax.ShapeDtypeStruct((1, D), F32)),
        compiler_params=_params(), name=name)(dxn, h, rstd, g, dres)


def _final_loss(h2, target, gf, name):
    tp = h2.shape[0]

    def body(h_ref, t_ref, g_ref, dh_ref, loss_ref, dg_ref):
        i = pl.program_id(0)
        live = (i > 0).astype(F32)
        x = h_ref[...]
        r = lax.rsqrt(jnp.mean(x * x, axis=1, keepdims=True) + EPS)
        xh = x * r
        e = xh * g_ref[...] - t_ref[...]
        row_loss = jnp.mean(e * e, axis=1, keepdims=True)
        loss_part = 0.5 * live * jnp.sum(row_loss, axis=0, keepdims=True)
        dout = e * (live / D)
        dg_part = jnp.sum(dout * xh, axis=0, keepdims=True)
        dxh = dout * g_ref[...]
        dh_ref[...] = r * (dxh - xh * jnp.mean(dxh * xh, axis=1, keepdims=True))

        @pl.when(i == 0)
        def _():
            loss_ref[...] = loss_part
            dg_ref[...] = dg_part

        @pl.when(i > 0)
        def _():
            loss_ref[...] += loss_part
            dg_ref[...] += dg_part

    return pl.pallas_call(
        body, grid=(tp // TM,),
        in_specs=[_row_spec(D), pl.BlockSpec((TM, D), lambda i: (jnp.maximum(i - 1, 0), 0)), _full_spec((1, D))],
        out_specs=[_row_spec(D), _full_spec((1, 1)), _full_spec((1, D))],
        out_shape=(jax.ShapeDtypeStruct((tp, D), F32), jax.ShapeDtypeStruct((1, 1), F32), jax.ShapeDtypeStruct((1, D), F32)),
        compiler_params=_params(), name=name)(h2, target, gf)


def _shift_down(x, halo, k):
    rk = pltpu.roll(x, k, 0)
    io = lax.broadcasted_iota(jnp.int32, (8, x.shape[1]), 0)
    top = jnp.where(io < k, pltpu.roll(halo, k, 0), rk[0:8])
    return jnp.concatenate([top, rk[8:]], axis=0)


def _shift_up(x, nxt, k):
    n = x.shape[0]
    rk = pltpu.roll(x, n - k, 0)
    io = lax.broadcasted_iota(jnp.int32, (8, x.shape[1]), 0)
    bot = jnp.where(io >= 8 - k, pltpu.roll(nxt, 8 - k, 0), rk[n - 8:n])
    return jnp.concatenate([rk[:n - 8], bot], axis=0)


def _conv_pre(x, halo, w_ref, b_ref):
    c = x * w_ref[3:4, :] + b_ref[...]
    shifted = []
    for k in (1, 2, 3):
        s = _shift_down(x, halo, k)
        shifted.append(s)
        c = c + s * w_ref[3 - k:4 - k, :]
    return c, shifted


def _qk_scale():
    col = lax.broadcasted_iota(jnp.int32, (1, D), 1)
    return jnp.where(col < NH * DQK, DQK ** -0.5, 1.0).astype(F32)


def _halo_prev_spec():
    return pl.BlockSpec((8, D), lambda i: (jnp.maximum(i * (TM // 8) - 1, 0), CB_MQK))


def _conv_fwd(pbig, w, b, name):
    tp = pbig.shape[0]

    def body(x_ref, halo_ref, w_ref, b_ref, o_ref):
        x = x_ref[...].astype(F32)
        halo = jnp.where(pl.program_id(0) > 0, halo_ref[...].astype(F32), 0.0)
        c, _ = _conv_pre(x, halo, w_ref, b_ref)
        o_ref[...] = (c * _sigmoid(c) * _qk_scale()).astype(BF)

    return pl.pallas_call(
        body, grid=(tp // TM,),
        in_specs=[_row_spec(D, CB_MQK), _halo_prev_spec(), _full_spec((4, D)), _full_spec((1, D))],
        out_specs=_row_spec(D), out_shape=jax.ShapeDtypeStruct((tp, D), BF),
        compiler_params=_params(), name=name)(pbig, pbig, w, b)


def _conv_bwd_pre(dqk, pbig, w, b, name):
    tp = pbig.shape[0]

    def body(d_ref, x_ref, halo_ref, w_ref, b_ref, dc_ref, dwb_ref):
        x = x_ref[...].astype(F32)
        halo = jnp.where(pl.program_id(0) > 0, halo_ref[...].astype(F32), 0.0)
        c, shifted = _conv_pre(x, halo, w_ref, b_ref)
        sg = _sigmoid(c)
        dc = d_ref[...] * _qk_scale() * (sg * (1.0 + c * (1.0 - sg)))
        dc_ref[...] = dc
        taps = [shifted[2], shifted[1], shifted[0], x]
        rows = [jnp.sum(dc * t, axis=0, keepdims=True) for t in taps] + [jnp.sum(dc, axis=0, keepdims=True)]
        io = lax.broadcasted_iota(jnp.int32, (8, D), 0)
        part = jnp.zeros((8, D), F32)
        for r, v in enumerate(rows):
            part = jnp.where(io == r, v, part)

        @pl.when(pl.program_id(0) == 0)
        def _():
            dwb_ref[...] = part

        @pl.when(pl.program_id(0) > 0)
        def _():
            dwb_ref[...] += part

    return pl.pallas_call(
        body, grid=(tp // TM,),
        in_specs=[_row_spec(D), _row_spec(D, CB_MQK), _halo_prev_spec(), _full_spec((4, D)), _full_spec((1, D))],
        out_specs=[_row_spec(D), _full_spec((8, D))],
        out_shape=(jax.ShapeDtypeStruct((tp, D), F32), jax.ShapeDtypeStruct((8, D), F32)),
        compiler_params=_params(), name=name)(dqk, pbig, pbig, w, b)


def _conv_bwd_in(dc, w, dproj, name):
    tp = dc.shape[0]
    nb = tp // TM

    def body(d_ref, nxt_ref, w_ref, _, o_ref):
        d = d_ref[...]
        nxt = jnp.where(pl.program_id(0) < nb - 1, nxt_ref[...], 0.0)
        acc = d * w_ref[3:4, :]
        for k in (1, 2, 3):
            acc = acc + _shift_up(d, nxt, k) * w_ref[3 - k:4 - k, :]
        o_ref[...] = acc.astype(BF)

    return pl.pallas_call(
        body, grid=(nb,),
        in_specs=[_row_spec(D), pl.BlockSpec((8, D), lambda i: (jnp.minimum((i + 1) * (TM // 8), tp // 8 - 1), 0)),
                  _full_spec((4, D)), ANY],
        out_specs=_row_spec(D, CB_MQK), out_shape=jax.ShapeDtypeStruct(dproj.shape, BF),
        input_output_aliases={3: 0}, compiler_params=_params(), name=name)(dc, dc, w, dproj)


def _merge_fwd(pm, pg, pbig, name):
    tp = pm.shape[0]

    def body(pm_ref, pg_ref, gm_ref, gg_ref, o_ref):
        o_ref[...] = (_sigmoid(gm_ref[...].astype(F32)) * pm_ref[...].astype(F32)
                      + _sigmoid(gg_ref[...].astype(F32)) * pg_ref[...].astype(F32)).astype(BF)

    return pl.pallas_call(
        body, grid=(tp // TM,), in_specs=[_row_spec(D), _row_spec(D), _row_spec(D, CB_GM), _row_spec(D, CB_GG)],
        out_specs=_row_spec(D), out_shape=jax.ShapeDtypeStruct((tp, D), BF),
        compiler_params=_params(), name=name)(pm, pg, pbig, pbig)


def _merge_bwd(dmerged, pm, pg, pbig, name):
    tp = pm.shape[0]

    def body(d_ref, pm_ref, pg_ref, gm_ref, gg_ref, dpm_ref, dpg_ref, dproj_ref):
        d = d_ref[...].astype(F32)
        sm = _sigmoid(gm_ref[...].astype(F32))
        sg = _sigmoid(gg_ref[...].astype(F32))
        dpm_ref[...] = (d * sm).astype(BF)
        dpg_ref[...] = (d * sg).astype(BF)
        dproj_ref[:, 0:D] = (d * pm_ref[...].astype(F32) * sm * (1.0 - sm)).astype(BF)
        dproj_ref[:, D:2 * D] = (d * pg_ref[...].astype(F32) * sg * (1.0 - sg)).astype(BF)

    shp = jax.ShapeDtypeStruct((tp, D), BF)
    return pl.pallas_call(
        body, grid=(tp // TM,), in_specs=[_row_spec(D), _row_spec(D), _row_spec(D), _row_spec(D, CB_GM), _row_spec(D, CB_GG)],
        out_specs=[_row_spec(D), _row_spec(D), _row_spec(2 * D, CB_GM // 2)],
        out_shape=(shp, shp, jax.ShapeDtypeStruct((tp, N_ALL), BF)),
        compiler_params=_params(), name=name)(dmerged, pm, pg, pbig, pbig)


def _swiglu_fwd(gu, name):
    tp = gu.shape[0]

    def body(g_ref, u_ref, o_ref):
        g = g_ref[...].astype(F32)
        o_ref[...] = (g * _sigmoid(g) * u_ref[...].astype(F32)).astype(BF)

    return pl.pallas_call(
        body, grid=(tp // TM,), in_specs=[_row_spec(D_FF, 0), _row_spec(D_FF, 1)],
        out_specs=_row_spec(D_FF), out_shape=jax.ShapeDtypeStruct((tp, D_FF), BF),
        compiler_params=_params(), name=name)(gu, gu)


def _swiglu_bwd(dff, gu, name):
    tp = gu.shape[0]

    def body(d_ref, g_ref, u_ref, o_ref):
        d = d_ref[...].astype(F32)
        g = g_ref[...].astype(F32)
        u = u_ref[...].astype(F32)
        sg = _sigmoid(g)
        o_ref[:, 0:D_FF] = (d * u * sg * (1.0 + g * (1.0 - sg))).astype(BF)
        o_ref[:, D_FF:2 * D_FF] = (d * g * sg).astype(BF)

    return pl.pallas_call(
        body, grid=(tp // TM,), in_specs=[_row_spec(D_FF), _row_spec(D_FF, 0), _row_spec(D_FF, 1)],
        out_specs=_row_spec(2 * D_FF), out_shape=jax.ShapeDtypeStruct((tp, 2 * D_FF), BF),
        compiler_params=_params(), name=name)(dff, gu, gu)


def _adamw(w, g, m, v, name):
    rows, cols = w.shape
    tr = 128 if rows % 128 == 0 else rows

    def body(w_ref, g_ref, m_ref, v_ref, d_ref, nm_ref, nv_ref):
        gv = g_ref[...]
        nm = ADAM_B1 * m_ref[...] + (1.0 - ADAM_B1) * gv
        nv = ADAM_B2 * v_ref[...] + (1.0 - ADAM_B2) * (gv * gv)
        m_hat = nm / (1.0 - ADAM_B1 ** ADAM_STEP)
        v_hat = nv / (1.0 - ADAM_B2 ** ADAM_STEP)
        d_ref[...] = -ADAM_LR * (m_hat / (jnp.sqrt(v_hat) + ADAM_EPS) + ADAM_WD * w_ref[...])
        nm_ref[...] = nm
        nv_ref[...] = nv

    spec = pl.BlockSpec((tr, cols), lambda i: (i, 0))
    shp = jax.ShapeDtypeStruct((rows, cols), F32)
    return pl.pallas_call(body, grid=(rows // tr,), in_specs=[spec] * 4, out_specs=[spec] * 3,
                          out_shape=(shp,) * 3, compiler_params=_params(), name=name)(w, g, m, v)


def _place_small(dsmall, dproj, name):
    tp = dsmall.shape[0]

    def body(s_ref, _, o_ref):
        o_ref[...] = s_ref[...]

    return pl.pallas_call(
        body, grid=(tp // TM,), in_specs=[_row_spec(N_SMALL), ANY], out_specs=_row_spec(N_SMALL, N_BIG // N_SMALL),
        out_shape=jax.ShapeDtypeStruct(dproj.shape, dproj.dtype), input_output_aliases={1: 0},
        compiler_params=_params(), name=name)(dsmall, dproj)


def _row_tile(rows, cap=512):
    best = rows
    for cand in range(8, min(rows, cap) + 1, 8):
        if rows % cand == 0:
            best = cand
    return best


def _add2(a, b, name):
    rows, cols = a.shape
    tr = _row_tile(rows)

    def body(a_ref, b_ref, o_ref):
        o_ref[...] = a_ref[...] + b_ref[...]

    spec = pl.BlockSpec((tr, cols), lambda i: (i, 0))
    return pl.pallas_call(body, grid=(rows // tr,), in_specs=[spec] * 2, out_specs=spec,
                          out_shape=jax.ShapeDtypeStruct((rows, cols), F32), compiler_params=_params(), name=name)(a, b)


def _add4(r, name):
    _, rows, cols = r.shape
    tr = _row_tile(rows, 256)

    def body(r_ref, o_ref):
        o_ref[...] = ((r_ref[0] + r_ref[1]) + r_ref[2]) + r_ref[3]

    return pl.pallas_call(body, grid=(rows // tr,), in_specs=[pl.BlockSpec((4, tr, cols), lambda i: (0, i, 0))],
                          out_specs=pl.BlockSpec((tr, cols), lambda i: (i, 0)),
                          out_shape=jax.ShapeDtypeStruct((rows, cols), F32), compiler_params=_params(), name=name)(r)


def _chunk_consts():
    r2 = lax.broadcasted_iota(jnp.int32, (CHUNK, CHUNK), 0)
    c2 = lax.broadcasted_iota(jnp.int32, (CHUNK, CHUNK), 1)
    tri = r2 >= c2
    return dict(tri=tri, tril_f=tri.astype(F32), triu_f=(r2 <= c2).astype(F32),
                lane=lax.broadcasted_iota(jnp.int32, (CHUNK, N_SMALL), 1),
                rowio=lax.broadcasted_iota(jnp.int32, (CHUNK, 1), 0),
                ones=jnp.ones((CHUNK, N_SMALL), F32))


def _valid_rows(block, c):
    row = block * TM + c * CHUNK + lax.broadcasted_iota(jnp.int32, (CHUNK, 1), 0)
    return row >= FIRST_VALID


def _col(x, lane, idx):
    return jnp.sum(jnp.where(lane == idx, x, 0.0), axis=1, keepdims=True)


def _last_row(x, rowio):
    return jnp.sum(jnp.where(rowio == CHUNK - 1, x, 0.0), axis=0, keepdims=True)


def _sum_all(x):
    return jnp.sum(jnp.sum(x, axis=1, keepdims=True), axis=0, keepdims=True)


def _headnorm_fwd(hm, gain, gate_act):
    rs = lax.rsqrt(jnp.mean(hm * hm, axis=1, keepdims=True) + EPS)
    return hm * rs * gain * gate_act


def _headnorm_bwd(dy, hm, gain, gate_act):
    rs = lax.rsqrt(jnp.mean(hm * hm, axis=1, keepdims=True) + EPS)
    xh = hm * rs
    dact = dy * xh * gain
    dgain = jnp.sum(dy * gate_act * xh, axis=0, keepdims=True)
    dxh = dy * gate_act * gain
    dhm = rs * (dxh - xh * jnp.mean(dxh * xh, axis=1, keepdims=True))
    return dhm, dact, dgain


def _mlstm_gates(sm, gbias, valid, k):
    pre = sm + gbias
    lf = jnp.where(valid, _logsig(pre), 0.0)
    b_all = _nn(k["tril_f"], lf, precision=HI)
    li_all = jnp.where(valid, pre, NEG)
    return pre, li_all, b_all


def _mlstm_head(h, qh, kh, vh, li_all, b_all, c_st, n_row, m11, k):
    lane, tri, rowio = k["lane"], k["tri"], k["rowio"]
    b_col = _col(b_all, lane, NH + h)
    li_col = _col(li_all, lane, h)
    sel = jnp.where(lane == h, 1.0, 0.0) - jnp.where(lane == NH + h, 1.0, 0.0)
    x = jnp.where(lane < NH, li_all, jnp.where(lane < 2 * NH, b_all, 0.0))
    ubc = _nt(sel, x, precision=HI)
    dmat = jnp.where(tri, b_col + ubc, NEG)
    m_row = jnp.maximum(b_col + m11, jnp.max(dmat, axis=1, keepdims=True))
    e = jnp.exp(dmat - m_row)
    w_mat = e * _nt(qh, kh)
    a = jnp.exp(b_col + m11 - m_row)
    cb = c_st.astype(BF)
    cq = _nt(qh, cb)
    qf = qh.astype(F32)
    nq = jnp.sum(qf * n_row, axis=1, keepdims=True)
    num = a * cq + _nn(w_mat.astype(BF), vh)
    den = a * nq + jnp.sum(w_mat, axis=1, keepdims=True)
    floor = jnp.exp(-m_row)
    r = jnp.maximum(jnp.abs(den), floor)
    hm = num / r
    g = _last_row(b_col, rowio)
    wlog = g - b_col + li_col
    m_new = jnp.maximum(g + m11, jnp.max(wlog, axis=0, keepdims=True))
    a_s = jnp.exp(g + m11 - m_new)
    w = jnp.exp(wlog - m_new)
    return dict(e=e, w_mat=w_mat, a=a, cb=cb, cq=cq, qf=qf, nq=nq, den=den, floor=floor, r=r, hm=hm,
                m_new=m_new, a_s=a_s, w=w)


def _mlstm_fwd(qk, pbig, small, gbias, headg, name):
    tp = qk.shape[0]
    nb = tp // TM

    def body(qk_ref, v_ref, mo_ref, sm_ref, gb_ref, hg_ref, y_ref, cs_ref, ns_ref, c_scr, n_scr):
        blk = pl.program_id(0)

        @pl.when(blk == 0)
        def _():
            c_scr[...] = jnp.zeros_like(c_scr)
            n_scr[...] = jnp.zeros_like(n_scr)

        k = _chunk_consts()
        io8 = lax.broadcasted_iota(jnp.int32, (8, DQK), 0)

        def chunk(c, carry):
            r0 = pl.multiple_of(c * CHUNK, CHUNK)
            rows = pl.ds(r0, CHUNK)
            valid = _valid_rows(blk, c)
            _, li_all, b_all = _mlstm_gates(sm_ref[rows, :], gb_ref[...], valid, k)
            for h in range(NH):
                qh = qk_ref[rows, h * DQK:(h + 1) * DQK]
                kh = qk_ref[rows, NH * DQK + h * DQK:NH * DQK + (h + 1) * DQK]
                vh = v_ref[rows, h * DV:(h + 1) * DV]
                c_st = c_scr[h]
                n_row = n_scr[h, 0:1, :]
                m11 = jnp.max(n_scr[h, 1:2, :], axis=1, keepdims=True)
                f = _mlstm_head(h, qh, kh, vh, li_all, b_all, c_st, n_row, m11, k)
                gate = _sigmoid(mo_ref[rows, h * DV:(h + 1) * DV].astype(F32))
                y_ref[rows, h * DV:(h + 1) * DV] = _headnorm_fwd(f["hm"], hg_ref[:, h * DV:(h + 1) * DV], gate).astype(BF)
                cs_ref[c, h] = f["cb"]
                ns_ref[c, h] = jnp.where(io8 == 0, n_row, jnp.where(io8 == 1, m11, 0.0))
                wk = f["w"] * kh.astype(F32)
                c_scr[h] = f["a_s"] * c_st + _tn(vh, wk.astype(BF))
                n_scr[h, 0:1, :] = f["a_s"] * n_row + jnp.sum(wk, axis=0, keepdims=True)
                n_scr[h, 1:2, :] = jnp.broadcast_to(f["m_new"], (1, DQK))
            return carry

        lax.fori_loop(0, CPB, chunk, 0)

    return pl.pallas_call(
        body, grid=(nb,),
        in_specs=[_row_spec(D), _row_spec(D, CB_MV), _row_spec(D, CB_MO), _row_spec(N_SMALL), _full_spec((1, N_SMALL)), _full_spec((1, D))],
        out_specs=[_row_spec(D), pl.BlockSpec((CPB, NH, DV, DQK), lambda i: (i, 0, 0, 0)),
                   pl.BlockSpec((CPB, NH, 8, DQK), lambda i: (i, 0, 0, 0))],
        out_shape=(jax.ShapeDtypeStruct((tp, D), BF), jax.ShapeDtypeStruct((tp // CHUNK, NH, DV, DQK), BF),
                   jax.ShapeDtypeStruct((tp // CHUNK, NH, 8, DQK), F32)),
        scratch_shapes=[pltpu.VMEM((NH, DV, DQK), F32), pltpu.VMEM((NH, 8, DQK), F32)],
        compiler_params=_params(), name=name)(qk, pbig, pbig, small, gbias, headg)


def _mlstm_bwd(dy, qk, pbig, small, gbias, headg, cs, ns, dproj, name):
    tp = qk.shape[0]
    nb = tp // TM

    def body(dy_ref, qk_ref, v_ref, mo_ref, sm_ref, gb_ref, hg_ref, cs_ref, ns_ref, _,
             dqk_ref, dproj_ref, dsm_ref, dgb_ref, dhg_ref, dc_scr, dn_scr):
        step = pl.program_id(0)
        blk = nb - 1 - step

        @pl.when(step == 0)
        def _():
            dc_scr[...] = jnp.zeros_like(dc_scr)
            dn_scr[...] = jnp.zeros_like(dn_scr)
            dgb_ref[...] = jnp.zeros_like(dgb_ref)
            dhg_ref[...] = jnp.zeros_like(dhg_ref)

        k = _chunk_consts()
        lane, rowio = k["lane"], k["rowio"]

        def chunk(cc, carry):
            c = CPB - 1 - cc
            r0 = pl.multiple_of(c * CHUNK, CHUNK)
            rows = pl.ds(r0, CHUNK)
            valid = _valid_rows(blk, c)
            pre, li_all, b_all = _mlstm_gates(sm_ref[rows, :], gb_ref[...], valid, k)
            dli_all = jnp.zeros((CHUNK, N_SMALL), F32)
            db_all = jnp.zeros((CHUNK, N_SMALL), F32)
            for h in range(NH):
                qh = qk_ref[rows, h * DQK:(h + 1) * DQK]
                kh = qk_ref[rows, NH * DQK + h * DQK:NH * DQK + (h + 1) * DQK]
                vh = v_ref[rows, h * DV:(h + 1) * DV]
                c_st = cs_ref[c, h].astype(F32)
                n_row = ns_ref[c, h, 0:1, :]
                m11 = jnp.max(ns_ref[c, h, 1:2, :], axis=1, keepdims=True)
                f = _mlstm_head(h, qh, kh, vh, li_all, b_all, c_st, n_row, m11, k)
                gain = hg_ref[:, h * DV:(h + 1) * DV]
                gate = _sigmoid(mo_ref[rows, h * DV:(h + 1) * DV].astype(F32))
                dhm, dgate, dgain = _headnorm_bwd(dy_ref[rows, h * DV:(h + 1) * DV].astype(F32), f["hm"], gain, gate)
                dproj_ref[rows, D + h * DV:D + (h + 1) * DV] = (dgate * gate * (1.0 - gate)).astype(BF)
                dhg_ref[:, h * DV:(h + 1) * DV] += dgain

                dc_new = dc_scr[h]
                dn_new = dn_scr[h]
                dcb = dc_new.astype(BF)
                kf = kh.astype(F32)
                a, w, r = f["a"], f["w"], f["r"]
                dnum = dhm / r
                dr = -jnp.sum(dhm * f["hm"], axis=1, keepdims=True) / r
                dden = jnp.where(jnp.abs(f["den"]) > f["floor"], dr * jnp.sign(f["den"]), 0.0)
                dnb = dnum.astype(BF)
                dw_mat = _nt(dnb, vh) + dden
                dsim = (f["e"] * dw_mat).astype(BF)
                gm = f["w_mat"] * dw_mat
                vdc = _nn(vh, dcb)
                kdc = _nt(kh, dcb)
                dproj_ref[rows, h * DV:(h + 1) * DV] = (_tn(f["w_mat"].astype(BF), dnb) + w * kdc).astype(BF)
                adden = a * dden
                dqk_ref[rows, h * DQK:(h + 1) * DQK] = _nn(dsim, kh) + a * _nn(dnb, f["cb"]) + adden * n_row
                dqk_ref[rows, NH * DQK + h * DQK:NH * DQK + (h + 1) * DQK] = _tn(dsim, qh) + w * vdc + w * dn_new
                da = jnp.sum(dnum * f["cq"], axis=1, keepdims=True) + dden * f["nq"]
                dw = jnp.sum(vdc * kf, axis=1, keepdims=True) + jnp.sum(kf * dn_new, axis=1, keepdims=True)
                da_s = _sum_all(dc_new * c_st) + jnp.sum(dn_new * n_row, axis=1, keepdims=True)
                wdw = w * dw
                rs = jnp.sum(gm, axis=1, keepdims=True)
                cs_col = _col(_tn(gm, k["ones"], precision=HI), lane, 0)
                dg = f["a_s"] * da_s + jnp.sum(wdw, axis=0, keepdims=True)
                db = a * da + rs - cs_col - wdw + jnp.where(rowio == CHUNK - 1, dg, 0.0)
                dli_all = dli_all + jnp.where(lane == h, cs_col + wdw, 0.0)
                db_all = db_all + jnp.where(lane == NH + h, db, 0.0)
                dc_scr[h] = f["a_s"] * dc_new + _tn((a * dnum).astype(BF), qh)
                dn_scr[h] = f["a_s"] * dn_new + jnp.sum(adden * f["qf"], axis=0, keepdims=True)
            dlf_all = _nn(k["triu_f"], db_all, precision=HI)
            dsm = jnp.where(valid, dli_all + dlf_all * _sigmoid(-pre), 0.0)
            dsm = jnp.where(lane < 2 * NH, dsm, 0.0)
            dsm_ref[rows, :] = dsm
            dgb_ref[0:1, :] += jnp.sum(dsm, axis=0, keepdims=True)
            return carry

        lax.fori_loop(0, CPB, chunk, 0)

    rev = lambda col: (lambda i: (nb - 1 - i, col))
    rspec = lambda width, col=0: pl.BlockSpec((TM, width), rev(col))
    return pl.pallas_call(
        body, grid=(nb,),
        in_specs=[rspec(D), rspec(D), rspec(D, CB_MV), rspec(D, CB_MO), rspec(N_SMALL), _full_spec((1, N_SMALL)), _full_spec((1, D)),
                  pl.BlockSpec((CPB, NH, DV, DQK), lambda i: (nb - 1 - i, 0, 0, 0)),
                  pl.BlockSpec((CPB, NH, 8, DQK), lambda i: (nb - 1 - i, 0, 0, 0)), ANY],
        out_specs=[rspec(D), rspec(2 * D, CB_MV // 2), rspec(N_SMALL), _full_spec((8, N_SMALL)), _full_spec((1, D))],
        out_shape=(jax.ShapeDtypeStruct((tp, D), F32), jax.ShapeDtypeStruct(dproj.shape, BF),
                   jax.ShapeDtypeStruct((tp, N_SMALL), F32), jax.ShapeDtypeStruct((8, N_SMALL), F32),
                   jax.ShapeDtypeStruct((1, D), F32)),
        scratch_shapes=[pltpu.VMEM((NH, DV, DQK), F32), pltpu.VMEM((NH, 1, DQK), F32)],
        input_output_aliases={9: 1}, compiler_params=_params(), name=name)(dy, qk, pbig, pbig, small, gbias, headg, cs, ns, dproj)


def _gla_loga(sm_ref, a2_ref, a2b_ref, blk):
    za = _nn(sm_ref[...].astype(BF), a2_ref[...]) + a2b_ref[...]
    row = blk * TM + lax.broadcasted_iota(jnp.int32, (TM, 1), 0)
    return za, jnp.where(row >= FIRST_VALID, _logsig(za) / G_TAU, 0.0)


def _gla_head(h, q_ref, k_ref, rows, bc, btot, k):
    sl = slice(h * DQK, (h + 1) * DQK)
    bch = bc[:, sl]
    bth = btot[:, sl]
    gq = q_ref[rows, h * DQK:(h + 1) * DQK].astype(F32)
    gk = k_ref[rows, NH * DQK + h * DQK:NH * DQK + (h + 1) * DQK].astype(F32)
    e_pos = jnp.exp(bch) * (DQK ** -0.5)
    e_neg = jnp.exp(-bch)
    e_end = jnp.exp(bth - bch)
    qd = gq * e_pos
    ki = gk * e_neg
    ke = gk * e_end
    att = jnp.where(k["tri"], _nt(qd.astype(BF), ki.astype(BF)), 0.0)
    return dict(e_pos=e_pos, e_neg=e_neg, e_end=e_end, qd=qd, ki=ki, ke=ke, att=att, decay=jnp.exp(bth))


def _gla_fwd(pbig, small, a2p, a2b, headg, name):
    tp = pbig.shape[0]
    nb = tp // TM

    def body(qk_ref, v_ref, gr_ref, sm_ref, a2_ref, a2b_ref, hg_ref, y_ref, ss_ref, s_scr, lg_scr):
        blk = pl.program_id(0)

        @pl.when(blk == 0)
        def _():
            s_scr[...] = jnp.zeros_like(s_scr)

        k = _chunk_consts()
        _, loga = _gla_loga(sm_ref, a2_ref, a2b_ref, blk)
        lg_scr[...] = loga

        def chunk(c, carry):
            r0 = pl.multiple_of(c * CHUNK, CHUNK)
            rows = pl.ds(r0, CHUNK)
            bc = _nn(k["tril_f"], lg_scr[rows, :], precision=HI)
            btot = _last_row(bc, k["rowio"])
            for h in range(NH):
                f = _gla_head(h, qk_ref, qk_ref, rows, bc, btot, k)
                vh = v_ref[rows, h * DV:(h + 1) * DV]
                s_st = s_scr[h]
                sb = s_st.astype(BF)
                qdb = f["qd"].astype(BF)
                o = _nn(f["att"].astype(BF), vh) + _nt(qdb, sb)
                gr = gr_ref[rows, h * DV:(h + 1) * DV].astype(F32)
                y_ref[rows, h * DV:(h + 1) * DV] = _headnorm_fwd(o, hg_ref[:, h * DV:(h + 1) * DV], gr * _sigmoid(gr)).astype(BF)
                ss_ref[c, h] = sb
                s_scr[h] = s_st * f["decay"] + _tn(vh, f["ke"].astype(BF))
            return carry

        lax.fori_loop(0, CPB, chunk, 0)

    return pl.pallas_call(
        body, grid=(nb,),
        in_specs=[_row_spec(D, CB_GQK), _row_spec(D, CB_GV), _row_spec(D, CB_GR), _row_spec(N_SMALL),
                  _full_spec((N_SMALL, NH * DQK)), _full_spec((1, NH * DQK)), _full_spec((1, D))],
        out_specs=[_row_spec(D), pl.BlockSpec((CPB, NH, DV, DQK), lambda i: (i, 0, 0, 0))],
        out_shape=(jax.ShapeDtypeStruct((tp, D), BF), jax.ShapeDtypeStruct((tp // CHUNK, NH, DV, DQK), BF)),
        scratch_shapes=[pltpu.VMEM((NH, DV, DQK), F32), pltpu.VMEM((TM, NH * DQK), F32)],
        compiler_params=_params(), name=name)(pbig, pbig, pbig, small, a2p, a2b, headg)


def _gla_bwd(dy, pbig, small, a2p, a2b, headg, ss, dsm_m, dproj, name):
    tp = pbig.shape[0]
    nb = tp // TM
    nqk = NH * DQK

    def body(dy_ref, qk_ref, v_ref, gr_ref, sm_ref, a2_ref, a2b_ref, hg_ref, ss_ref, dsmm_ref, _,
             dproj_ref, dsm_ref, da2_ref, da2b_ref, dhg_ref, ds_scr, lg_scr, dza_scr):
        step = pl.program_id(0)
        blk = nb - 1 - step

        @pl.when(step == 0)
        def _():
            ds_scr[...] = jnp.zeros_like(ds_scr)
            da2_ref[...] = jnp.zeros_like(da2_ref)
            da2b_ref[...] = jnp.zeros_like(da2b_ref)
            dhg_ref[...] = jnp.zeros_like(dhg_ref)

        k = _chunk_consts()
        rowio = k["rowio"]
        za, loga = _gla_loga(sm_ref, a2_ref, a2b_ref, blk)
        lg_scr[...] = loga

        def chunk(cc, carry):
            c = CPB - 1 - cc
            r0 = pl.multiple_of(c * CHUNK, CHUNK)
            rows = pl.ds(r0, CHUNK)
            bc = _nn(k["tril_f"], lg_scr[rows, :], precision=HI)
            btot = _last_row(bc, rowio)
            dbc_parts = []
            for h in range(NH):
                f = _gla_head(h, qk_ref, qk_ref, rows, bc, btot, k)
                vh = v_ref[rows, h * DV:(h + 1) * DV]
                s_b = ss_ref[c, h]
                s_f = s_b.astype(F32)
                qdb = f["qd"].astype(BF)
                attb = f["att"].astype(BF)
                o = _nn(attb, vh) + _nt(qdb, s_b)
                gr = gr_ref[rows, h * DV:(h + 1) * DV].astype(F32)
                sg = _sigmoid(gr)
                gain = hg_ref[:, h * DV:(h + 1) * DV]
                do, dact, dgain = _headnorm_bwd(dy_ref[rows, h * DV:(h + 1) * DV].astype(F32), o, gain, gr * sg)
                dproj_ref[rows, 2 * D + h * DV:2 * D + (h + 1) * DV] = (dact * sg * (1.0 + gr * (1.0 - sg))).astype(BF)
                dhg_ref[:, h * DV:(h + 1) * DV] += dgain

                ds_new = ds_scr[h]
                dsb = ds_new.astype(BF)
                dob = do.astype(BF)
                keb = f["ke"].astype(BF)
                dproj_ref[rows, D + h * DV:D + (h + 1) * DV] = (_tn(attb, dob) + _nt(keb, dsb)).astype(BF)
                datt = jnp.where(k["tri"], _nt(dob, vh), 0.0).astype(BF)
                dqd = _nn(datt, f["ki"].astype(BF)) + _nn(dob, s_b)
                dki = _tn(datt, qdb)
                dke = _nn(vh, dsb)
                dproj_ref[rows, h * DQK:(h + 1) * DQK] = (dqd * f["e_pos"]).astype(BF)
                dproj_ref[rows, nqk + h * DQK:nqk + (h + 1) * DQK] = (dki * f["e_neg"] + dke * f["e_end"]).astype(BF)
                dke_ke = dke * f["ke"]
                dbtot = jnp.sum(dke_ke, axis=0, keepdims=True) + jnp.sum(ds_new * s_f, axis=0, keepdims=True) * f["decay"]
                dbc_parts.append(dqd * f["qd"] - dki * f["ki"] - dke_ke + jnp.where(rowio == CHUNK - 1, dbtot, 0.0))
                ds_scr[h] = ds_new * f["decay"] + _tn(dob, qdb)
            dbc = jnp.concatenate(dbc_parts, axis=1)
            dza_scr[rows, :] = _nn(k["triu_f"], dbc, precision=HI)
            return carry

        lax.fori_loop(0, CPB, chunk, 0)
        row = blk * TM + lax.broadcasted_iota(jnp.int32, (TM, 1), 0)
        dza = jnp.where(row >= FIRST_VALID, dza_scr[...] * (_sigmoid(-za) / G_TAU), 0.0)
        dzb = dza.astype(BF)
        dsm_ref[...] = (_nt(dzb, a2_ref[...]) + dsmm_ref[...]).astype(BF)
        da2_ref[...] += _tn(sm_ref[...].astype(BF), dzb)
        da2b_ref[...] += jnp.sum(dza, axis=0, keepdims=True)

    rspec = lambda width, col=0: pl.BlockSpec((TM, width), lambda i: (nb - 1 - i, col))
    return pl.pallas_call(
        body, grid=(nb,),
        in_specs=[rspec(D), rspec(D, CB_GQK), rspec(D, CB_GV), rspec(D, CB_GR), rspec(N_SMALL),
                  _full_spec((N_SMALL, nqk)), _full_spec((1, nqk)), _full_spec((1, D)),
                  pl.BlockSpec((CPB, NH, DV, DQK), lambda i: (nb - 1 - i, 0, 0, 0)), rspec(N_SMALL), ANY],
        out_specs=[rspec(3 * D, 0), rspec(N_SMALL), _full_spec((N_SMALL, nqk)), _full_spec((1, nqk)), _full_spec((1, D))],
        out_shape=(jax.ShapeDtypeStruct(dproj.shape, BF),
                   jax.ShapeDtypeStruct((tp, N_SMALL), BF), jax.ShapeDtypeStruct((N_SMALL, nqk), F32),
                   jax.ShapeDtypeStruct((1, nqk), F32), jax.ShapeDtypeStruct((1, D), F32)),
        scratch_shapes=[pltpu.VMEM((NH, DV, DQK), F32), pltpu.VMEM((TM, nqk), F32), pltpu.VMEM((TM, nqk), F32)],
        input_output_aliases={10: 0}, compiler_params=_params(), name=name)(dy, pbig, pbig, pbig, small, a2p, a2b, headg, ss, dsm_m, dproj)


PIECE_BYTES = 1 << 20
MAX_PIECES = 32


def _place():
    return lax.axis_index("x"), lax.axis_index("y"), lax.axis_index("c")


def _piece_rows(rows, row_bytes, align):
    want = min(MAX_PIECES, max(1, -(-rows * row_bytes // PIECE_BYTES)))
    best = rows
    for k in range(1, want + 1):
        if rows % k == 0 and (rows // k) % align == 0:
            best = rows // k
    return best


def _remote(src, dst, send_sems, recv_sems, k, to):
    return pltpu.make_async_remote_copy(src_ref=src, dst_ref=dst, send_sem=send_sems.at[k], recv_sem=recv_sems.at[k],
                                        device_id=to, device_id_type=MESH)


def _all_gather_chips(p, name):
    r, n = p.shape
    rh = r // 2
    align = 32 // p.dtype.itemsize
    assert r % (2 * align) == 0
    cr = _piece_rows(rh, n * p.dtype.itemsize, align)

    def body(p_ref, o_ref, send_sems, recv_sems, local_sem):
        x, y, c = _place()
        chips = [(1 - x, y), (x, 1 - y), (1 - x, 1 - y)]
        me = 2 * x + y
        sib = (x, y, 1 - c)

        def half(hc, piece=None):
            if piece is None:
                return pl.ds(pl.multiple_of(hc * rh, align), rh)
            return pl.ds(pl.multiple_of(hc * rh + piece * cr, align), cr)

        for i in range(r // cr):
            rows = pl.ds(i * cr, cr)
            pltpu.make_async_copy(p_ref.at[rows], o_ref.at[me, rows], local_sem).start()
        first = [_remote(p_ref.at[half(c)], o_ref.at[me, half(c)], send_sems, recv_sems, j, (*chip, c))
                 for j, chip in enumerate(chips)]
        for cp in first:
            cp.start()
        for j, (cx, cy) in enumerate(chips):
            slot = 2 * cx + cy
            _remote(p_ref.at[half(c)], o_ref.at[slot, half(c)], send_sems, recv_sems, j, (cx, cy, c)).wait_recv()
            for i in range(rh // cr):
                _remote(o_ref.at[slot, half(c, i)], o_ref.at[slot, half(c, i)], send_sems, recv_sems, 3 + j, sib).start()
        for j, (cx, cy) in enumerate(chips):
            slot = 2 * cx + cy
            block = _remote(o_ref.at[slot, half(c)], o_ref.at[slot, half(1 - c)], send_sems, recv_sems, 3 + j, sib)
            block.wait_send()
            block.wait_recv()
        for cp in first:
            cp.wait_send()
        pltpu.make_async_copy(p_ref, o_ref.at[me], local_sem).wait()

    return pl.pallas_call(
        body, in_specs=[ANY], out_specs=ANY, out_shape=jax.ShapeDtypeStruct((4, r, n), p.dtype),
        scratch_shapes=[pltpu.SemaphoreType.DMA((6,)), pltpu.SemaphoreType.DMA((6,)), pltpu.SemaphoreType.DMA],
        name=name)(p)


def _swap_halves(items, name):
    k = len(items)

    def body(*refs):
        a_refs, own_refs, got_refs = refs[:k], refs[k:2 * k], refs[2 * k:3 * k]
        send_sems, recv_sems, local_sems = refs[3 * k:]
        x, y, c = _place()
        sib = (x, y, 1 - c)
        for i, a in enumerate(items):
            _, r, n = a.shape
            rh = r // 2
            cr = _piece_rows(rh, n * 4, 8)
            for q in range(4):
                for t in range(rh // cr):
                    mine = pl.ds(pl.multiple_of(c * rh + t * cr, 8), cr)
                    other = pl.ds(pl.multiple_of((1 - c) * rh + t * cr, 8), cr)
                    dst = pl.ds(t * cr, cr)
                    pltpu.make_async_copy(a_refs[i].at[q, mine], own_refs[i].at[q, dst], local_sems.at[i]).start()
                    _remote(a_refs[i].at[q, other], got_refs[i].at[q, dst], send_sems, recv_sems, i, sib).start()
        for i, a in enumerate(items):
            rh = a.shape[1] // 2
            block = _remote(a_refs[i].at[:, pl.ds(0, rh)], got_refs[i], send_sems, recv_sems, i, sib)
            block.wait_send()
            block.wait_recv()
            pltpu.make_async_copy(a_refs[i].at[:, pl.ds(0, rh)], own_refs[i], local_sems.at[i]).wait()

    shps = tuple(jax.ShapeDtypeStruct((4, a.shape[1] // 2, a.shape[2]), a.dtype) for a in items)
    outs = pl.pallas_call(
        body, in_specs=[ANY] * k, out_specs=[ANY] * (2 * k), out_shape=shps + shps,
        scratch_shapes=[pltpu.SemaphoreType.DMA((k,)), pltpu.SemaphoreType.DMA((k,)), pltpu.SemaphoreType.DMA((k,))],
        name=name)(*items)
    return outs[:k], outs[k:]


def _scatter_chips(items, name):
    k = len(items)

    def body(*refs):
        s_refs, o_refs = refs[:k], refs[k:2 * k]
        send_sems, recv_sems, local_sems = refs[2 * k:]
        x, y, c = _place()
        chips = [(1 - x, y), (x, 1 - y), (1 - x, 1 - y)]
        me = 2 * x + y
        sent = []
        for i, s in enumerate(items):
            _, rh, n = s.shape
            cr = _piece_rows(rh, n * 4, 8)
            for t in range(rh // cr):
                rows = pl.ds(t * cr, cr)
                pltpu.make_async_copy(s_refs[i].at[me, rows], o_refs[i].at[me, rows], local_sems.at[i]).start()
            for j, (cx, cy) in enumerate(chips):
                cp = _remote(s_refs[i].at[2 * cx + cy], o_refs[i].at[me], send_sems, recv_sems, 3 * i + j, (cx, cy, c))
                cp.start()
                sent.append(cp)
        for i in range(k):
            for j, (cx, cy) in enumerate(chips):
                _remote(s_refs[i].at[me], o_refs[i].at[2 * cx + cy], send_sems, recv_sems, 3 * i + j, (cx, cy, c)).wait_recv()
        for cp in sent:
            cp.wait_send()
        for i in range(k):
            pltpu.make_async_copy(s_refs[i].at[me], o_refs[i].at[me], local_sems.at[i]).wait()

    return pl.pallas_call(
        body, in_specs=[ANY] * k, out_specs=[ANY] * k, out_shape=tuple(jax.ShapeDtypeStruct(s.shape, s.dtype) for s in items),
        scratch_shapes=[pltpu.SemaphoreType.DMA((3 * k,)), pltpu.SemaphoreType.DMA((3 * k,)), pltpu.SemaphoreType.DMA((k,))],
        name=name)(*items)


def _join_halves(items, name):
    k = len(items)

    def body(*refs):
        f_refs, o_refs = refs[:k], refs[k:2 * k]
        send_sems, recv_sems, local_sems = refs[2 * k:]
        x, y, c = _place()
        sib = (x, y, 1 - c)
        for i, f in enumerate(items):
            rh, n = f.shape
            cr = _piece_rows(rh, n * 4, 8)
            for t in range(rh // cr):
                rows = pl.ds(t * cr, cr)
                pltpu.make_async_copy(f_refs[i].at[rows], o_refs[i].at[c, rows], local_sems.at[i]).start()
                _remote(f_refs[i].at[rows], o_refs[i].at[c, rows], send_sems, recv_sems, i, sib).start()
        for i in range(k):
            block = _remote(f_refs[i], o_refs[i].at[1 - c], send_sems, recv_sems, i, sib)
            block.wait_send()
            block.wait_recv()
            pltpu.make_async_copy(f_refs[i], o_refs[i].at[c], local_sems.at[i]).wait()

    return pl.pallas_call(
        body, in_specs=[ANY] * k, out_specs=[ANY] * k,
        out_shape=tuple(jax.ShapeDtypeStruct((2,) + f.shape, f.dtype) for f in items),
        scratch_shapes=[pltpu.SemaphoreType.DMA((k,)), pltpu.SemaphoreType.DMA((k,)), pltpu.SemaphoreType.DMA((k,))],
        name=name)(*items)


SMALL_ROWS = 16
SMALL_SHARD_SHAPES = [(N_META, 256), (4, 256), (G_RANK, 128), (NH, 64), (NH, 64)]
REPL_SHAPES = [(1, D), (1, D), (1, 2, NH), (1, NH * DQK), (1, D), (D,)]
W_IN_SHARD = 2054


def _pack_small(parts):
    flat = jnp.concatenate([p.reshape(-1) for p in parts])
    return jnp.pad(flat, (0, SMALL_ROWS * D - flat.shape[0])).reshape(SMALL_ROWS, D)


def _unpack_small(block, shapes):
    flat, out, off = block.reshape(-1), [], 0
    for shp in shapes:
        n = 1
        for s in shp:
            n *= s
        out.append(flat[off:off + n].reshape(shp))
        off += n
    return out


def _proj_cols_from_w_in(w_in_f):
    w_big = jnp.concatenate([w_in_f[:, 3080:5128], w_in_f[:, 5144:6168], w_in_f[:, 0:1024], w_in_f[:, 6168:8216],
                             w_in_f[:, 1024:2048], w_in_f[:, 2056:3080]], axis=1)
    w_small = jnp.concatenate([w_in_f[:, 2048:2056], w_in_f[:, 5128:5144], jnp.zeros((D, N_SMALL - 24), w_in_f.dtype)], axis=1)
    return w_big, w_small


def _w_in_from_proj_cols(d_wall):
    big, small = d_wall[:, 0:N_BIG], d_wall[:, N_BIG:N_ALL]
    return jnp.concatenate([big[:, 3072:4096], big[:, 6144:7168], small[:, 0:8], big[:, 7168:8192], big[:, 0:2048],
                            small[:, 8:24], big[:, 2048:3072], big[:, 4096:6144]], axis=1)


def kernel(x, meta_tokens, norm1_g, w_in, conv_w, conv_b, m_gate_b, g_a2, g_a2_b, m_head_g, g_head_g, w_branch_m, w_branch_g, w_out, norm2_g, w_ff_gate, w_ff_up, w_ff_down, final_g, loss_target, m_meta_tokens, m_norm1_g, m_w_in, m_conv_w, m_conv_b, m_m_gate_b, m_g_a2, m_g_a2_b, m_m_head_g, m_g_head_g, m_w_branch_m, m_w_branch_g, m_w_out, m_norm2_g, m_w_ff_gate, m_w_ff_up, m_w_ff_down, m_final_g, v_meta_tokens, v_norm1_g, v_w_in, v_conv_w, v_conv_b, v_m_gate_b, v_g_a2, v_g_a2_b, v_m_head_g, v_g_head_g, v_w_branch_m, v_w_branch_g, v_w_out, v_norm2_g, v_w_ff_gate, v_w_ff_up, v_w_ff_down, v_final_g):
    w = _gather_weights(w_in, w_branch_m, w_branch_g, w_out, w_ff_gate, w_ff_up, w_ff_down, meta_tokens, conv_w, g_a2, m_head_g, g_head_g)
    loss_local, dh0, local = _local_step(x[0], loss_target[0], w, norm1_g, conv_b, m_gate_b, g_a2_b, norm2_g, final_g)
    grads = _reduce_grads(local)

    weights = [w_in, w_branch_m, w_branch_g, w_out, w_ff_gate, w_ff_up, w_ff_down, meta_tokens, conv_w, g_a2, m_head_g, g_head_g,
               norm1_g, conv_b, m_gate_b, g_a2_b, norm2_g, final_g]
    moms = [m_w_in, m_w_branch_m, m_w_branch_g, m_w_out, m_w_ff_gate, m_w_ff_up, m_w_ff_down, m_meta_tokens, m_conv_w, m_g_a2,
            m_m_head_g, m_g_head_g, m_norm1_g, m_conv_b, m_m_gate_b, m_g_a2_b, m_norm2_g, m_final_g]
    vels = [v_w_in, v_w_branch_m, v_w_branch_g, v_w_out, v_w_ff_gate, v_w_ff_up, v_w_ff_down, v_meta_tokens, v_conv_w, v_g_a2,
            v_m_head_g, v_g_head_g, v_norm1_g, v_conv_b, v_m_gate_b, v_g_a2_b, v_norm2_g, v_final_g]
    res = {}
    for nm, wt, g, m, v in zip(PACK_ORDER, weights, grads, moms, vels):
        two_d = (wt.size // wt.shape[-1], wt.shape[-1])
        d, nm_, nv_ = _adamw(wt.reshape(two_d), g.reshape(two_d), m.reshape(two_d), v.reshape(two_d), "adamw_" + nm)
        res[nm] = (g.reshape(wt.shape), d.reshape(wt.shape), nm_.reshape(wt.shape), nv_.reshape(wt.shape))

    order = ["meta_tokens", "norm1_g", "w_in", "conv_w", "conv_b", "m_gate_b", "g_a2", "g_a2_b", "m_head_g", "g_head_g",
             "w_branch_m", "w_branch_g", "w_out", "norm2_g", "w_ff_gate", "w_ff_up", "w_ff_down", "final_g"]
    loss = lax.psum(loss_local[0, 0], ("x", "y", "c"))
    grad_x = dh0[TM:].reshape(x.shape)
    return (loss, grad_x, *[res[n][0] for n in order], *[res[n][1] for n in order],
            *[res[n][2] for n in order], *[res[n][3] for n in order])


PACK_ORDER = ["w_in", "w_branch_m", "w_branch_g", "w_out", "w_ff_gate", "w_ff_up", "w_ff_down", "meta_tokens", "conv_w", "g_a2",
              "m_head_g", "g_head_g", "norm1_g", "conv_b", "m_gate_b", "g_a2_b", "norm2_g", "final_g"]


def _gather_weights(w_in, w_branch_m, w_branch_g, w_out, w_ff_gate, w_ff_up, w_ff_down, meta_tokens, conv_w, g_a2, m_head_g, g_head_g):
    bf = lambda a: a.astype(BF)
    rows_local = jnp.concatenate([bf(w_branch_m[0]), bf(w_branch_g[0]), bf(w_out[0]), bf(w_ff_down[0]),
                                  bf(w_ff_gate[0].T), bf(w_ff_up[0].T)], axis=0)
    rows_all = _all_gather_chips(rows_local, "gather_rows")
    win_all = _all_gather_chips(bf(w_in[0]), "gather_w_in")
    small_all = _all_gather_chips(_pack_small([meta_tokens, conv_w[0], g_a2[0], m_head_g[0], g_head_g[0]]), "gather_small")
    cut = lambda lo, hi: rows_all[:, lo:hi].reshape(4 * (hi - lo), D)
    wbm, wbg, wout, wdown = cut(0, 256), cut(256, 512), cut(512, 768), cut(768, 1472)
    wgu_t = jnp.concatenate([cut(1472, 2176), cut(2176, 2880)], axis=0)
    w_in_f = jnp.concatenate([win_all[q] for q in range(4)], axis=1)
    small_sh = [_unpack_small(small_all[q], SMALL_SHARD_SHAPES) for q in range(4)]
    cat = lambda i: jnp.concatenate([s[i] for s in small_sh], axis=-1)
    return dict(w_in=w_in_f, wbm=wbm, wbg=wbg, wout=wout, wgu_t=wgu_t, wdown=wdown, meta=cat(0), convw=cat(1), ga2=cat(2),
                mhg=cat(3).reshape(1, D), ghg=cat(4).reshape(1, D))


def _local_step(x0, target, w, norm1_g, conv_b, m_gate_b, g_a2_b, norm2_g, final_g):
    w_in_f, wbm, wbg, wout, wgu_t, wdown = w["w_in"], w["wbm"], w["wbg"], w["wout"], w["wgu_t"], w["wdown"]
    meta_f, convw_f, ga2_f, mhg_f, ghg_f = w["meta"], w["convw"], w["ga2"], w["mhg"], w["ghg"]
    w_big, w_small = _proj_cols_from_w_in(w_in_f)
    w_all = jnp.concatenate([w_big, w_small], axis=1)
    gbias = jnp.concatenate([m_gate_b.reshape(1, 2 * NH), jnp.zeros((1, N_SMALL - 2 * NH), F32)], axis=1)
    a2p = jnp.concatenate([jnp.zeros((8, NH * DQK), F32), ga2_f, jnp.zeros((N_SMALL - 24, NH * DQK), F32)], axis=0).astype(BF)
    convb = conv_b.reshape(1, D)
    g1 = norm1_g.reshape(1, D)
    g2 = norm2_g.reshape(1, D)
    gf = final_g.reshape(1, D)
    h0 = jnp.concatenate([jnp.zeros((FIRST_VALID, D), F32), meta_f, x0], axis=0)

    xn1, rstd1 = _rms_fwd(h0, g1, "rms1")
    pbig = _mm(xn1, w_big, nt=False, out_dtype=BF, tn=1024, name="proj_big")
    small = _mm(xn1, w_small, nt=False, out_dtype=F32, tn=N_SMALL, name="proj_small")
    qk = _conv_fwd(pbig, convw_f, convb, "conv_fwd")
    y_m, m_cs, m_ns = _mlstm_fwd(qk, pbig, small, gbias, mhg_f, "mlstm_fwd")
    y_g, g_ss = _gla_fwd(pbig, small, a2p, g_a2_b, ghg_f, "gla_fwd")
    p_m = _mm(y_m, wbm, nt=False, out_dtype=BF, tn=1024, name="branch_m")
    p_g = _mm(y_g, wbg, nt=False, out_dtype=BF, tn=1024, name="branch_g")
    merged = _merge_fwd(p_m, p_g, pbig, "merge_fwd")
    h1 = _mm(merged, wout, nt=False, out_dtype=F32, tn=1024, addend=h0, name="out_proj")
    hn, rstd2 = _rms_fwd(h1, g2, "rms2")
    gu = _mm(hn, wgu_t, nt=True, out_dtype=BF, tn=1408, name="ff_in")
    ff = _swiglu_fwd(gu, "swiglu_fwd")
    h2 = _mm(ff, wdown, nt=False, out_dtype=F32, tn=1024, addend=h1, name="ff_down")
    dh2, loss_local, d_final_g = _final_loss(h2, target, gf, "final_loss")

    dff = _mm(dh2, wdown, nt=True, out_dtype=BF, tn=1408, name="d_ff")
    d_wdown = _mm_tn(ff, dh2, tm=1408, tn=1024, name="dw_ff_down")
    dgu = _swiglu_bwd(dff, gu, "swiglu_bwd")
    dhn = _mm(dgu, wgu_t, nt=False, out_dtype=F32, tn=1024, tk=2816, name="d_hn")
    d_wgu_t = _mm_tn(dgu, hn, tm=1408, tn=1024, name="dw_ff_in")
    dh1, d_g2 = _rms_bwd(dhn, h1, rstd2, g2, dh2, "rms2_bwd")
    dmerged = _mm(dh1, wout, nt=True, out_dtype=BF, tn=1024, name="d_merged")
    d_wout = _mm_tn(merged, dh1, tm=1024, tn=1024, name="dw_out")
    dp_m, dp_g, dproj = _merge_bwd(dmerged, p_m, p_g, pbig, "merge_bwd")
    dy_m = _mm(dp_m, wbm, nt=True, out_dtype=BF, tn=1024, name="d_ym")
    dy_g = _mm(dp_g, wbg, nt=True, out_dtype=BF, tn=1024, name="d_yg")
    d_wbm = _mm_tn(y_m, dp_m, tm=1024, tn=1024, name="dw_branch_m")
    d_wbg = _mm_tn(y_g, dp_g, tm=1024, tn=1024, name="dw_branch_g")
    dqk_m, dproj, dsm_m, d_gbias, d_mhg = _mlstm_bwd(dy_m, qk, pbig, small, gbias, mhg_f, m_cs, m_ns, dproj, "mlstm_bwd")
    dconv, d_convwb = _conv_bwd_pre(dqk_m, pbig, convw_f, convb, "conv_bwd_pre")
    dproj = _conv_bwd_in(dconv, convw_f, dproj, "conv_bwd_in")
    dproj, dsmall, d_a2p, d_a2b, d_ghg = _gla_bwd(dy_g, pbig, small, a2p, g_a2_b, ghg_f, g_ss, dsm_m, dproj, "gla_bwd")
    dproj = _place_small(dsmall, dproj, "dproj_small")
    dxn = _mm(dproj, w_all, nt=True, out_dtype=F32, tn=1024, tk=1664, name="d_xn")
    d_wall = _mm_tn(xn1, dproj, tm=1024, tn=1664, name="dw_in")
    dh0, d_g1 = _rms_bwd(dxn, h0, rstd1, g1, dh1, "rms1_bwd")

    small_sharded = [dh0[FIRST_VALID:TM], d_convwb[0:4], d_a2p[8:24], d_mhg.reshape(NH, DV), d_ghg.reshape(NH, DV)]
    replicated = [d_g1, d_convwb[4:5], d_gbias[0:1, 0:2 * NH].reshape(1, 2, NH), d_a2b, d_g2, d_final_g.reshape(D)]
    local = dict(w_all=d_wall, wbm=d_wbm, wbg=d_wbg, wout=d_wout, wdown=d_wdown, wgu_t=d_wgu_t,
                 small_sharded=small_sharded, replicated=replicated)
    return loss_local, dh0, local


def _reduce_grads(local):
    d_win = _w_in_from_proj_cols(local["w_all"])
    win4 = jnp.stack([d_win[:, q * W_IN_SHARD:(q + 1) * W_IN_SHARD] for q in range(4)])
    small4 = jnp.stack([_pack_small([g[:, q * shp[1]:(q + 1) * shp[1]] for g, shp in zip(local["small_sharded"], SMALL_SHARD_SHAPES)]
                                    + local["replicated"]) for q in range(4)])
    gu = local["wgu_t"].reshape(2, 4, D_FF // 4, D)
    items = [win4, local["wbm"].reshape(4, 256, D), local["wbg"].reshape(4, 256, D), local["wout"].reshape(4, 256, D),
             local["wdown"].reshape(4, D_FF // 4, D), gu[0], gu[1], small4]
    own, got = _swap_halves(items, "reduce_siblings")
    sums = [_add2(o.reshape(-1, o.shape[2]), g.reshape(-1, g.shape[2]), f"reduce_add2_{i}").reshape(o.shape)
            for i, (o, g) in enumerate(zip(own, got))]
    from_chips = _scatter_chips(sums, "reduce_chips")
    halves = [_add4(f, f"reduce_add4_{i}") for i, f in enumerate(from_chips)]
    full = [f.reshape(2 * f.shape[1], f.shape[2]) for f in _join_halves(halves, "reduce_join")]
    smalls = _unpack_small(full[7], SMALL_SHARD_SHAPES + REPL_SHAPES)
    return [full[0], full[1], full[2], full[3], full[5].T, full[6].T, full[4]] + smalls
```

```python
import functools

import jax
import jax.numpy as jnp
from jax import lax
from jax.experimental import pallas as pl
from jax.experimental.pallas import tpu as pltpu

F32 = jnp.float32
BF = jnp.bfloat16
HI = lax.Precision.HIGHEST
MESH = pl.DeviceIdType.MESH

D = 1024
N_META = 16
CHUNK = 64
EPS = 1e-6
NH = 4
DV = 256
DQK = 128
G_RANK = 16
G_TAU = 16.0
D_FF = 2816
TM = 512
FIRST_VALID = TM - N_META
CPB = TM // CHUNK
NEG = -1e30
N_BIG = 8192
CB_GQK, CB_GV, CB_GR, CB_MQK, CB_GM, CB_GG, CB_MV, CB_MO = range(8)
N_SMALL = 128
N_ALL = N_BIG + N_SMALL
VMEM_LIMIT = 56 * 1024 * 1024

ADAM_LR, ADAM_B1, ADAM_B2, ADAM_EPS, ADAM_WD, ADAM_STEP = 0.001, 0.9, 0.999, 1e-08, 0.01, 10

NT_DIMS = (((1,), (1,)), ((), ()))
TN_DIMS = (((0,), (0,)), ((), ()))


def _nt(a, b, **kw):
    return lax.dot_general(a, b, NT_DIMS, preferred_element_type=F32, **kw)


def _tn(a, b, **kw):
    return lax.dot_general(a, b, TN_DIMS, preferred_element_type=F32, **kw)


def _nn(a, b, **kw):
    return jnp.dot(a, b, preferred_element_type=F32, **kw)


def _params(**kw):
    return pltpu.CompilerParams(vmem_limit_bytes=VMEM_LIMIT, **kw)


def _sigmoid(x):
    return 1.0 / (1.0 + jnp.exp(-x))


def _logsig(x):
    return jnp.minimum(x, 0.0) - jnp.log(1.0 + jnp.exp(-jnp.abs(x)))


def _mm(a, b, *, nt, out_dtype, tn, tk=None, tm=TM, addend=None, name):
    m, k = a.shape
    n = b.shape[0] if nt else b.shape[1]
    tk = k if tk is None else tk
    nk = k // tk
    assert m % tm == 0 and n % tn == 0 and k % tk == 0
    dims = NT_DIMS if nt else (((1,), (0,)), ((), ()))

    def body(*refs):
        if addend is None:
            a_ref, b_ref, o_ref, acc_ref = refs
        else:
            a_ref, b_ref, add_ref, o_ref, acc_ref = refs
        kk = pl.program_id(2)
        part = lax.dot_general(a_ref[...].astype(BF), b_ref[...].astype(BF), dims, preferred_element_type=F32)

        @pl.when(kk == 0)
        def _():
            acc_ref[...] = part

        @pl.when(kk > 0)
        def _():
            acc_ref[...] += part

        @pl.when(kk == nk - 1)
        def _():
            r = acc_ref[...]
            if addend is not None:
                r = r + add_ref[...].astype(F32)
            o_ref[...] = r.astype(o_ref.dtype)

    in_specs = [pl.BlockSpec((tm, tk), lambda j, i, kk: (i, kk)),
                pl.BlockSpec((tn, tk), lambda j, i, kk: (j, kk)) if nt else pl.BlockSpec((tk, tn), lambda j, i, kk: (kk, j))]
    args = [a, b]
    if addend is not None:
        in_specs.append(pl.BlockSpec((tm, tn), lambda j, i, kk: (i, j)))
        args.append(addend)
    return pl.pallas_call(
        body, grid=(n // tn, m // tm, nk), in_specs=in_specs,
        out_specs=pl.BlockSpec((tm, tn), lambda j, i, kk: (i, j)),
        out_shape=jax.ShapeDtypeStruct((m, n), out_dtype),
        scratch_shapes=[pltpu.VMEM((tm, tn), F32)], compiler_params=_params(), name=name)(*args)


def _mm_tn(a, b, *, tm, tn, tk=TM, name):
    t, m = a.shape
    n = b.shape[1]
    assert t % tk == 0 and m % tm == 0 and n % tn == 0

    def body(a_ref, b_ref, o_ref):
        part = _tn(a_ref[...].astype(BF), b_ref[...].astype(BF))

        @pl.when(pl.program_id(2) == 0)
        def _():
            o_ref[...] = part

        @pl.when(pl.program_id(2) > 0)
        def _():
            o_ref[...] += part

    return pl.pallas_call(
        body, grid=(m // tm, n // tn, t // tk),
        in_specs=[pl.BlockSpec((tk, tm), lambda i, j, kk: (kk, i)), pl.BlockSpec((tk, tn), lambda i, j, kk: (kk, j))],
        out_specs=pl.BlockSpec((tm, tn), lambda i, j, kk: (i, j)),
        out_shape=jax.ShapeDtypeStruct((m, n), F32), compiler_params=_params(), name=name)(a, b)


ANY = pl.BlockSpec(memory_space=pl.ANY)


def _row_spec(width, col=0):
    return pl.BlockSpec((TM, width), lambda i: (i, col))


def _full_spec(shape):
    return pl.BlockSpec(shape, lambda i: (0,) * len(shape))


def _rms_fwd(h, g, name):
    tp = h.shape[0]

    def body(h_ref, g_ref, xn_ref, r_ref):
        x = h_ref[...]
        r = lax.rsqrt(jnp.mean(x * x, axis=1, keepdims=True) + EPS)
        xn_ref[...] = (x * r * g_ref[...]).astype(BF)
        r_ref[...] = r

    return pl.pallas_call(
        body, grid=(tp // TM,), in_specs=[_row_spec(D), _full_spec((1, D))],
        out_specs=[_row_spec(D), _row_spec(1)],
        out_shape=(jax.ShapeDtypeStruct((tp, D), BF), jax.ShapeDtypeStruct((tp, 1), F32)),
        compiler_params=_params(), name=name)(h, g)


def _rms_bwd(dxn, h, rstd, g, dres, name):
    tp = h.shape[0]

    def body(dxn_ref, h_ref, r_ref, g_ref, dres_ref, dh_ref, dg_ref):
        r = r_ref[...]
        xh = h_ref[...] * r
        dxn_v = dxn_ref[...].astype(F32)
        dxh = dxn_v * g_ref[...]
        dh = r * (dxh - xh * jnp.mean(dxh * xh, axis=1, keepdims=True))
        dh_ref[...] = dh + dres_ref[...]
        part = jnp.sum(dxn_v * xh, axis=0, keepdims=True)

        @pl.when(pl.program_id(0) == 0)
        def _():
            dg_ref[...] = part

        @pl.when(pl.program_id(0) > 0)
        def _():
            dg_ref[...] += part

    return pl.pallas_call(
        body, grid=(tp // TM,),
        in_specs=[_row_spec(D), _row_spec(D), _row_spec(1), _full_spec((1, D)), _row_spec(D)],
        out_specs=[_row_spec(D), _full_spec((1, D))],
        out_shape=(jax.ShapeDtypeStruct((tp, D), F32), jax.ShapeDtypeStruct((1, D), F32)),
        compiler_params=_params(), name=name)(dxn, h, rstd, g, dres)


def _final_loss(h2, target, gf, name):
    tp = h2.shape[0]

    def body(h_ref, t_ref, g_ref, dh_ref, loss_ref, dg_ref):
        i = pl.program_id(0)
        live = (i > 0).astype(F32)
        x = h_ref[...]
        r = lax.rsqrt(jnp.mean(x * x, axis=1, keepdims=True) + EPS)
        xh = x * r
        e = xh * g_ref[...] - t_ref[...]
        row_loss = jnp.mean(e * e, axis=1, keepdims=True)
        loss_part = 0.5 * live * jnp.sum(row_loss, axis=0, keepdims=True)
        dout = e * (live / D)
        dg_part = jnp.sum(dout * xh, axis=0, keepdims=True)
        dxh = dout * g_ref[...]
        dh_ref[...] = r * (dxh - xh * jnp.mean(dxh * xh, axis=1, keepdims=True))

        @pl.when(i == 0)
        def _():
            loss_ref[...] = loss_part
            dg_ref[...] = dg_part

        @pl.when(i > 0)
        def _():
            loss_ref[...] += loss_part
            dg_ref[...] += dg_part

    return pl.pallas_call(
        body, grid=(tp // TM,),
        in_specs=[_row_spec(D), pl.BlockSpec((TM, D), lambda i: (jnp.maximum(i - 1, 0), 0)), _full_spec((1, D))],
        out_specs=[_row_spec(D), _full_spec((1, 1)), _full_spec((1, D))],
        out_shape=(jax.ShapeDtypeStruct((tp, D), F32), jax.ShapeDtypeStruct((1, 1), F32), jax.ShapeDtypeStruct((1, D), F32)),
        compiler_params=_params(), name=name)(h2, target, gf)


def _shift_down(x, halo, k):
    rk = pltpu.roll(x, k, 0)
    io = lax.broadcasted_iota(jnp.int32, (8, x.shape[1]), 0)
    top = jnp.where(io < k, pltpu.roll(halo, k, 0), rk[0:8])
    return jnp.concatenate([top, rk[8:]], axis=0)


def _shift_up(x, nxt, k):
    n = x.shape[0]
    rk = pltpu.roll(x, n - k, 0)
    io = lax.broadcasted_iota(jnp.int32, (8, x.shape[1]), 0)
    bot = jnp.where(io >= 8 - k, pltpu.roll(nxt, 8 - k, 0), rk[n - 8:n])
    return jnp.concatenate([rk[:n - 8], bot], axis=0)


def _conv_pre(x, halo, w_ref, b_ref):
    c = x * w_ref[3:4, :] + b_ref[...]
    shifted = []
    for k in (1, 2, 3):
        s = _shift_down(x, halo, k)
        shifted.append(s)
        c = c + s * w_ref[3 - k:4 - k, :]
    return c, shifted


def _qk_scale():
    col = lax.broadcasted_iota(jnp.int32, (1, D), 1)
    return jnp.where(col < NH * DQK, DQK ** -0.5, 1.0).astype(F32)


def _halo_prev_spec():
    return pl.BlockSpec((8, D), lambda i: (jnp.maximum(i * (TM // 8) - 1, 0), CB_MQK))


def _conv_fwd(pbig, w, b, name):
    tp = pbig.shape[0]

    def body(x_ref, halo_ref, w_ref, b_ref, o_ref):
        x = x_ref[...].astype(F32)
        halo = jnp.where(pl.program_id(0) > 0, halo_ref[...].astype(F32), 0.0)
        c, _ = _conv_pre(x, halo, w_ref, b_ref)
        o_ref[...] = (c * _sigmoid(c) * _qk_scale()).astype(BF)

    return pl.pallas_call(
        body, grid=(tp // TM,),
        in_specs=[_row_spec(D, CB_MQK), _halo_prev_spec(), _full_spec((4, D)), _full_spec((1, D))],
        out_specs=_row_spec(D), out_shape=jax.ShapeDtypeStruct((tp, D), BF),
        compiler_params=_params(), name=name)(pbig, pbig, w, b)


def _conv_bwd_pre(dqk, pbig, w, b, name):
    tp = pbig.shape[0]

    def body(d_ref, x_ref, halo_ref, w_ref, b_ref, dc_ref, dwb_ref):
        x = x_ref[...].astype(F32)
        halo = jnp.where(pl.program_id(0) > 0, halo_ref[...].astype(F32), 0.0)
        c, shifted = _conv_pre(x, halo, w_ref, b_ref)
        sg = _sigmoid(c)
        dc = d_ref[...] * _qk_scale() * (sg * (1.0 + c * (1.0 - sg)))
        dc_ref[...] = dc
        taps = [shifted[2], shifted[1], shifted[0], x]
        rows = [jnp.sum(dc * t, axis=0, keepdims=True) for t in taps] + [jnp.sum(dc, axis=0, keepdims=True)]
        io = lax.broadcasted_iota(jnp.int32, (8, D), 0)
        part = jnp.zeros((8, D), F32)
        for r, v in enumerate(rows):
            part = jnp.where(io == r, v, part)

        @pl.when(pl.program_id(0) == 0)
        def _():
            dwb_ref[...] = part

        @pl.when(pl.program_id(0) > 0)
        def _():
            dwb_ref[...] += part

    return pl.pallas_call(
        body, grid=(tp // TM,),
        in_specs=[_row_spec(D), _row_spec(D, CB_MQK), _halo_prev_spec(), _full_spec((4, D)), _full_spec((1, D))],
        out_specs=[_row_spec(D), _full_spec((8, D))],
        out_shape=(jax.ShapeDtypeStruct((tp, D), F32), jax.ShapeDtypeStruct((8, D), F32)),
        compiler_params=_params(), name=name)(dqk, pbig, pbig, w, b)


def _conv_bwd_in(dc, w, dproj, name):
    tp = dc.shape[0]
    nb = tp // TM

    def body(d_ref, nxt_ref, w_ref, _, o_ref):
        d = d_ref[...]
        nxt = jnp.where(pl.program_id(0) < nb - 1, nxt_ref[...], 0.0)
        acc = d * w_ref[3:4, :]
        for k in (1, 2, 3):
            acc = acc + _shift_up(d, nxt, k) * w_ref[3 - k:4 - k, :]
        o_ref[...] = acc.astype(BF)

    return pl.pallas_call(
        body, grid=(nb,),
        in_specs=[_row_spec(D), pl.BlockSpec((8, D), lambda i: (jnp.minimum((i + 1) * (TM // 8), tp // 8 - 1), 0)),
                  _full_spec((4, D)), ANY],
        out_specs=_row_spec(D, CB_MQK), out_shape=jax.ShapeDtypeStruct(dproj.shape, BF),
        input_output_aliases={3: 0}, compiler_params=_params(), name=name)(dc, dc, w, dproj)


def _merge_fwd(pm, pg, pbig, name):
    tp = pm.shape[0]

    def body(pm_ref, pg_ref, gm_ref, gg_ref, o_ref):
        o_ref[...] = (_sigmoid(gm_ref[...].astype(F32)) * pm_ref[...].astype(F32)
                      + _sigmoid(gg_ref[...].astype(F32)) * pg_ref[...].astype(F32)).astype(BF)

    return pl.pallas_call(
        body, grid=(tp // TM,), in_specs=[_row_spec(D), _row_spec(D), _row_spec(D, CB_GM), _row_spec(D, CB_GG)],
        out_specs=_row_spec(D), out_shape=jax.ShapeDtypeStruct((tp, D), BF),
        compiler_params=_params(), name=name)(pm, pg, pbig, pbig)


def _merge_bwd(dmerged, pm, pg, pbig, name):
    tp = pm.shape[0]

    def body(d_ref, pm_ref, pg_ref, gm_ref, gg_ref, dpm_ref, dpg_ref, dproj_ref):
        d = d_ref[...].astype(F32)
        sm = _sigmoid(gm_ref[...].astype(F32))
        sg = _sigmoid(gg_ref[...].astype(F32))
        dpm_ref[...] = (d * sm).astype(BF)
        dpg_ref[...] = (d * sg).astype(BF)
        dproj_ref[:, 0:D] = (d * pm_ref[...].astype(F32) * sm * (1.0 - sm)).astype(BF)
        dproj_ref[:, D:2 * D] = (d * pg_ref[...].astype(F32) * sg * (1.0 - sg)).astype(BF)

    shp = jax.ShapeDtypeStruct((tp, D), BF)
    return pl.pallas_call(
        body, grid=(tp // TM,), in_specs=[_row_spec(D), _row_spec(D), _row_spec(D), _row_spec(D, CB_GM), _row_spec(D, CB_GG)],
        out_specs=[_row_spec(D), _row_spec(D), _row_spec(2 * D, CB_GM // 2)],
        out_shape=(shp, shp, jax.ShapeDtypeStruct((tp, N_ALL), BF)),
        compiler_params=_params(), name=name)(dmerged, pm, pg, pbig, pbig)


def _swiglu_fwd(gu, name):
    tp = gu.shape[0]

    def body(g_ref, u_ref, o_ref):
        g = g_ref[...].astype(F32)
        o_ref[...] = (g * _sigmoid(g) * u_ref[...].astype(F32)).astype(BF)

    return pl.pallas_call(
        body, grid=(tp // TM,), in_specs=[_row_spec(D_FF, 0), _row_spec(D_FF, 1)],
        out_specs=_row_spec(D_FF), out_shape=jax.ShapeDtypeStruct((tp, D_FF), BF),
        compiler_params=_params(), name=name)(gu, gu)


def _swiglu_bwd(dff, gu, name):
    tp = gu.shape[0]

    def body(d_ref, g_ref, u_ref, o_ref):
        d = d_ref[...].astype(F32)
        g = g_ref[...].astype(F32)
        u = u_ref[...].astype(F32)
        sg = _sigmoid(g)
        o_ref[:, 0:D_FF] = (d * u * sg * (1.0 + g * (1.0 - sg))).astype(BF)
        o_ref[:, D_FF:2 * D_FF] = (d * g * sg).astype(BF)

    return pl.pallas_call(
        body, grid=(tp // TM,), in_specs=[_row_spec(D_FF), _row_spec(D_FF, 0), _row_spec(D_FF, 1)],
        out_specs=_row_spec(2 * D_FF), out_shape=jax.ShapeDtypeStruct((tp, 2 * D_FF), BF),
        compiler_params=_params(), name=name)(dff, gu, gu)


def _adamw(w, g, m, v, name):
    rows, cols = w.shape
    tr = 128 if rows % 128 == 0 else rows

    def body(w_ref, g_ref, m_ref, v_ref, d_ref, nm_ref, nv_ref):
        gv = g_ref[...]
        nm = ADAM_B1 * m_ref[...] + (1.0 - ADAM_B1) * gv
        nv = ADAM_B2 * v_ref[...] + (1.0 - ADAM_B2) * (gv * gv)
        m_hat = nm / (1.0 - ADAM_B1 ** ADAM_STEP)
        v_hat = nv / (1.0 - ADAM_B2 ** ADAM_STEP)
        d_ref[...] = -ADAM_LR * (m_hat / (jnp.sqrt(v_hat) + ADAM_EPS) + ADAM_WD * w_ref[...])
        nm_ref[...] = nm
        nv_ref[...] = nv

    spec = pl.BlockSpec((tr, cols), lambda i: (i, 0))
    shp = jax.ShapeDtypeStruct((rows, cols), F32)
    return pl.pallas_call(body, grid=(rows // tr,), in_specs=[spec] * 4, out_specs=[spec] * 3,
                          out_shape=(shp,) * 3, compiler_params=_params(), name=name)(w, g, m, v)


def _place_small(dsmall, dproj, name):
    tp = dsmall.shape[0]

    def body(s_ref, _, o_ref):
        o_ref[...] = s_ref[...]

    return pl.pallas_call(
        body, grid=(tp // TM,), in_specs=[_row_spec(N_SMALL), ANY], out_specs=_row_spec(N_SMALL, N_BIG // N_SMALL),
        out_shape=jax.ShapeDtypeStruct(dproj.shape, dproj.dtype), input_output_aliases={1: 0},
        compiler_params=_params(), name=name)(dsmall, dproj)


def _row_tile(rows, cap=512):
    best = rows
    for cand in range(8, min(rows, cap) + 1, 8):
        if rows % cand == 0:
            best = cand
    return best


def _add2(a, b, out_dtype, name):
    rows, cols = a.shape
    tr = _row_tile(rows)

    def body(a_ref, b_ref, o_ref):
        o_ref[...] = (a_ref[...] + b_ref[...]).astype(o_ref.dtype)

    spec = pl.BlockSpec((tr, cols), lambda i: (i, 0))
    return pl.pallas_call(body, grid=(rows // tr,), in_specs=[spec] * 2, out_specs=spec,
                          out_shape=jax.ShapeDtypeStruct((rows, cols), out_dtype), compiler_params=_params(), name=name)(a, b)


def _add4(first, rest, name):
    rows, cols = first.shape
    tr = _row_tile(rows, 256)

    def body(f_ref, r_ref, o_ref):
        up = lambda v: v.astype(F32)
        o_ref[...] = ((up(f_ref[...]) + up(r_ref[0])) + up(r_ref[1])) + up(r_ref[2])

    return pl.pallas_call(body, grid=(rows // tr,),
                          in_specs=[pl.BlockSpec((tr, cols), lambda i: (i, 0)), pl.BlockSpec((3, tr, cols), lambda i: (0, i, 0))],
                          out_specs=pl.BlockSpec((tr, cols), lambda i: (i, 0)),
                          out_shape=jax.ShapeDtypeStruct((rows, cols), F32), compiler_params=_params(), name=name)(first, rest)


def _chunk_consts():
    r2 = lax.broadcasted_iota(jnp.int32, (CHUNK, CHUNK), 0)
    c2 = lax.broadcasted_iota(jnp.int32, (CHUNK, CHUNK), 1)
    tri = r2 >= c2
    return dict(tri=tri, tril_f=tri.astype(F32), triu_f=(r2 <= c2).astype(F32),
                lane=lax.broadcasted_iota(jnp.int32, (CHUNK, N_SMALL), 1),
                rowio=lax.broadcasted_iota(jnp.int32, (CHUNK, 1), 0),
                ones=jnp.ones((CHUNK, N_SMALL), F32))


def _valid_rows(block, c):
    row = block * TM + c * CHUNK + lax.broadcasted_iota(jnp.int32, (CHUNK, 1), 0)
    return row >= FIRST_VALID


def _col(x, lane, idx):
    return jnp.sum(jnp.where(lane == idx, x, 0.0), axis=1, keepdims=True)


def _last_row(x, rowio):
    return jnp.sum(jnp.where(rowio == CHUNK - 1, x, 0.0), axis=0, keepdims=True)


def _sum_all(x):
    return jnp.sum(jnp.sum(x, axis=1, keepdims=True), axis=0, keepdims=True)


def _headnorm_fwd(hm, gain, gate_act):
    rs = lax.rsqrt(jnp.mean(hm * hm, axis=1, keepdims=True) + EPS)
    return hm * rs * gain * gate_act


def _headnorm_bwd(dy, hm, gain, gate_act):
    rs = lax.rsqrt(jnp.mean(hm * hm, axis=1, keepdims=True) + EPS)
    xh = hm * rs
    dact = dy * xh * gain
    dgain = jnp.sum(dy * gate_act * xh, axis=0, keepdims=True)
    dxh = dy * gate_act * gain
    dhm = rs * (dxh - xh * jnp.mean(dxh * xh, axis=1, keepdims=True))
    return dhm, dact, dgain


def _mlstm_gates(sm, gbias, valid, k):
    pre = sm + gbias
    lf = jnp.where(valid, _logsig(pre), 0.0)
    b_all = _nn(k["tril_f"], lf, precision=HI)
    li_all = jnp.where(valid, pre, NEG)
    return pre, li_all, b_all


def _mlstm_head(h, qh, kh, vh, li_all, b_all, c_st, n_row, m11, k):
    lane, tri, rowio = k["lane"], k["tri"], k["rowio"]
    b_col = _col(b_all, lane, NH + h)
    li_col = _col(li_all, lane, h)
    sel = jnp.where(lane == h, 1.0, 0.0) - jnp.where(lane == NH + h, 1.0, 0.0)
    x = jnp.where(lane < NH, li_all, jnp.where(lane < 2 * NH, b_all, 0.0))
    ubc = _nt(sel, x, precision=HI)
    dmat = jnp.where(tri, b_col + ubc, NEG)
    m_row = jnp.maximum(b_col + m11, jnp.max(dmat, axis=1, keepdims=True))
    e = jnp.exp(dmat - m_row)
    w_mat = e * _nt(qh, kh)
    a = jnp.exp(b_col + m11 - m_row)
    cb = c_st.astype(BF)
    cq = _nt(qh, cb)
    qf = qh.astype(F32)
    nq = jnp.sum(qf * n_row, axis=1, keepdims=True)
    num = a * cq + _nn(w_mat.astype(BF), vh)
    den = a * nq + jnp.sum(w_mat, axis=1, keepdims=True)
    floor = jnp.exp(-m_row)
    r = jnp.maximum(jnp.abs(den), floor)
    hm = num / r
    g = _last_row(b_col, rowio)
    wlog = g - b_col + li_col
    m_new = jnp.maximum(g + m11, jnp.max(wlog, axis=0, keepdims=True))
    a_s = jnp.exp(g + m11 - m_new)
    w = jnp.exp(wlog - m_new)
    return dict(e=e, w_mat=w_mat, a=a, cb=cb, cq=cq, qf=qf, nq=nq, den=den, floor=floor, r=r, hm=hm,
                m_new=m_new, a_s=a_s, w=w)


def _mlstm_fwd(qk, pbig, small, gbias, headg, name):
    tp = qk.shape[0]
    nb = tp // TM

    def body(qk_ref, v_ref, mo_ref, sm_ref, gb_ref, hg_ref, y_ref, cs_ref, ns_ref, c_scr, n_scr):
        blk = pl.program_id(0)

        @pl.when(blk == 0)
        def _():
            c_scr[...] = jnp.zeros_like(c_scr)
            n_scr[...] = jnp.zeros_like(n_scr)

        k = _chunk_consts()
        io8 = lax.broadcasted_iota(jnp.int32, (8, DQK), 0)

        def chunk(c, carry):
            r0 = pl.multiple_of(c * CHUNK, CHUNK)
            rows = pl.ds(r0, CHUNK)
            valid = _valid_rows(blk, c)
            _, li_all, b_all = _mlstm_gates(sm_ref[rows, :], gb_ref[...], valid, k)
            for h in range(NH):
                qh = qk_ref[rows, h * DQK:(h + 1) * DQK]
                kh = qk_ref[rows, NH * DQK + h * DQK:NH * DQK + (h + 1) * DQK]
                vh = v_ref[rows, h * DV:(h + 1) * DV]
                c_st = c_scr[h]
                n_row = n_scr[h, 0:1, :]
                m11 = jnp.max(n_scr[h, 1:2, :], axis=1, keepdims=True)
                f = _mlstm_head(h, qh, kh, vh, li_all, b_all, c_st, n_row, m11, k)
                gate = _sigmoid(mo_ref[rows, h * DV:(h + 1) * DV].astype(F32))
                y_ref[rows, h * DV:(h + 1) * DV] = _headnorm_fwd(f["hm"], hg_ref[:, h * DV:(h + 1) * DV], gate).astype(BF)
                cs_ref[c, h] = f["cb"]
                ns_ref[c, h] = jnp.where(io8 == 0, n_row, jnp.where(io8 == 1, m11, 0.0))
                wk = f["w"] * kh.astype(F32)
                c_scr[h] = f["a_s"] * c_st + _tn(vh, wk.astype(BF))
                n_scr[h, 0:1, :] = f["a_s"] * n_row + jnp.sum(wk, axis=0, keepdims=True)
                n_scr[h, 1:2, :] = jnp.broadcast_to(f["m_new"], (1, DQK))
            return carry

        lax.fori_loop(0, CPB, chunk, 0)

    return pl.pallas_call(
        body, grid=(nb,),
        in_specs=[_row_spec(D), _row_spec(D, CB_MV), _row_spec(D, CB_MO), _row_spec(N_SMALL), _full_spec((1, N_SMALL)), _full_spec((1, D))],
        out_specs=[_row_spec(D), pl.BlockSpec((CPB, NH, DV, DQK), lambda i: (i, 0, 0, 0)),
                   pl.BlockSpec((CPB, NH, 8, DQK), lambda i: (i, 0, 0, 0))],
        out_shape=(jax.ShapeDtypeStruct((tp, D), BF), jax.ShapeDtypeStruct((tp // CHUNK, NH, DV, DQK), BF),
                   jax.ShapeDtypeStruct((tp // CHUNK, NH, 8, DQK), F32)),
        scratch_shapes=[pltpu.VMEM((NH, DV, DQK), F32), pltpu.VMEM((NH, 8, DQK), F32)],
        compiler_params=_params(), name=name)(qk, pbig, pbig, small, gbias, headg)


def _mlstm_bwd(dy, qk, pbig, small, gbias, headg, cs, ns, dproj, name):
    tp = qk.shape[0]
    nb = tp // TM

    def body(dy_ref, qk_ref, v_ref, mo_ref, sm_ref, gb_ref, hg_ref, cs_ref, ns_ref, _,
             dqk_ref, dproj_ref, dsm_ref, dgb_ref, dhg_ref, dc_scr, dn_scr):
        step = pl.program_id(0)
        blk = nb - 1 - step

        @pl.when(step == 0)
        def _():
            dc_scr[...] = jnp.zeros_like(dc_scr)
            dn_scr[...] = jnp.zeros_like(dn_scr)
            dgb_ref[...] = jnp.zeros_like(dgb_ref)
            dhg_ref[...] = jnp.zeros_like(dhg_ref)

        k = _chunk_consts()
        lane, rowio = k["lane"], k["rowio"]

        def chunk(cc, carry):
            c = CPB - 1 - cc
            r0 = pl.multiple_of(c * CHUNK, CHUNK)
            rows = pl.ds(r0, CHUNK)
            valid = _valid_rows(blk, c)
            pre, li_all, b_all = _mlstm_gates(sm_ref[rows, :], gb_ref[...], valid, k)
            dli_all = jnp.zeros((CHUNK, N_SMALL), F32)
            db_all = jnp.zeros((CHUNK, N_SMALL), F32)
            for h in range(NH):
                qh = qk_ref[rows, h * DQK:(h + 1) * DQK]
                kh = qk_ref[rows, NH * DQK + h * DQK:NH * DQK + (h + 1) * DQK]
                vh = v_ref[rows, h * DV:(h + 1) * DV]
                c_st = cs_ref[c, h].astype(F32)
                n_row = ns_ref[c, h, 0:1, :]
                m11 = jnp.max(ns_ref[c, h, 1:2, :], axis=1, keepdims=True)
                f = _mlstm_head(h, qh, kh, vh, li_all, b_all, c_st, n_row, m11, k)
                gain = hg_ref[:, h * DV:(h + 1) * DV]
                gate = _sigmoid(mo_ref[rows, h * DV:(h + 1) * DV].astype(F32))
                dhm, dgate, dgain = _headnorm_bwd(dy_ref[rows, h * DV:(h + 1) * DV].astype(F32), f["hm"], gain, gate)
                dproj_ref[rows, D + h * DV:D + (h + 1) * DV] = (dgate * gate * (1.0 - gate)).astype(BF)
                dhg_ref[:, h * DV:(h + 1) * DV] += dgain

                dc_new = dc_scr[h]
                dn_new = dn_scr[h]
                dcb = dc_new.astype(BF)
                kf = kh.astype(F32)
                a, w, r = f["a"], f["w"], f["r"]
                dnum = dhm / r
                dr = -jnp.sum(dhm * f["hm"], axis=1, keepdims=True) / r
                dden = jnp.where(jnp.abs(f["den"]) > f["floor"], dr * jnp.sign(f["den"]), 0.0)
                dnb = dnum.astype(BF)
                dw_mat = _nt(dnb, vh) + dden
                dsim = (f["e"] * dw_mat).astype(BF)
                gm = f["w_mat"] * dw_mat
                vdc = _nn(vh, dcb)
                kdc = _nt(kh, dcb)
                dproj_ref[rows, h * DV:(h + 1) * DV] = (_tn(f["w_mat"].astype(BF), dnb) + w * kdc).astype(BF)
                adden = a * dden
                dqk_ref[rows, h * DQK:(h + 1) * DQK] = _nn(dsim, kh) + a * _nn(dnb, f["cb"]) + adden * n_row
                dqk_ref[rows, NH * DQK + h * DQK:NH * DQK + (h + 1) * DQK] = _tn(dsim, qh) + w * vdc + w * dn_new
                da = jnp.sum(dnum * f["cq"], axis=1, keepdims=True) + dden * f["nq"]
                dw = jnp.sum(vdc * kf, axis=1, keepdims=True) + jnp.sum(kf * dn_new, axis=1, keepdims=True)
                da_s = _sum_all(dc_new * c_st) + jnp.sum(dn_new * n_row, axis=1, keepdims=True)
                wdw = w * dw
                rs = jnp.sum(gm, axis=1, keepdims=True)
                cs_col = _col(_tn(gm, k["ones"], precision=HI), lane, 0)
                dg = f["a_s"] * da_s + jnp.sum(wdw, axis=0, keepdims=True)
                db = a * da + rs - cs_col - wdw + jnp.where(rowio == CHUNK - 1, dg, 0.0)
                dli_all = dli_all + jnp.where(lane == h, cs_col + wdw, 0.0)
                db_all = db_all + jnp.where(lane == NH + h, db, 0.0)
                dc_scr[h] = f["a_s"] * dc_new + _tn((a * dnum).astype(BF), qh)
                dn_scr[h] = f["a_s"] * dn_new + jnp.sum(adden * f["qf"], axis=0, keepdims=True)
            dlf_all = _nn(k["triu_f"], db_all, precision=HI)
            dsm = jnp.where(valid, dli_all + dlf_all * _sigmoid(-pre), 0.0)
            dsm = jnp.where(lane < 2 * NH, dsm, 0.0)
            dsm_ref[rows, :] = dsm
            dgb_ref[0:1, :] += jnp.sum(dsm, axis=0, keepdims=True)
            return carry

        lax.fori_loop(0, CPB, chunk, 0)

    rev = lambda col: (lambda i: (nb - 1 - i, col))
    rspec = lambda width, col=0: pl.BlockSpec((TM, width), rev(col))
    return pl.pallas_call(
        body, grid=(nb,),
        in_specs=[rspec(D), rspec(D), rspec(D, CB_MV), rspec(D, CB_MO), rspec(N_SMALL), _full_spec((1, N_SMALL)), _full_spec((1, D)),
                  pl.BlockSpec((CPB, NH, DV, DQK), lambda i: (nb - 1 - i, 0, 0, 0)),
                  pl.BlockSpec((CPB, NH, 8, DQK), lambda i: (nb - 1 - i, 0, 0, 0)), ANY],
        out_specs=[rspec(D), rspec(2 * D, CB_MV // 2), rspec(N_SMALL), _full_spec((8, N_SMALL)), _full_spec((1, D))],
        out_shape=(jax.ShapeDtypeStruct((tp, D), F32), jax.ShapeDtypeStruct(dproj.shape, BF),
                   jax.ShapeDtypeStruct((tp, N_SMALL), F32), jax.ShapeDtypeStruct((8, N_SMALL), F32),
                   jax.ShapeDtypeStruct((1, D), F32)),
        scratch_shapes=[pltpu.VMEM((NH, DV, DQK), F32), pltpu.VMEM((NH, 1, DQK), F32)],
        input_output_aliases={9: 1}, compiler_params=_params(), name=name)(dy, qk, pbig, pbig, small, gbias, headg, cs, ns, dproj)


def _gla_loga(sm_ref, a2_ref, a2b_ref, blk):
    za = _nn(sm_ref[...].astype(BF), a2_ref[...]) + a2b_ref[...]
    row = blk * TM + lax.broadcasted_iota(jnp.int32, (TM, 1), 0)
    return za, jnp.where(row >= FIRST_VALID, _logsig(za) / G_TAU, 0.0)


def _gla_head(h, q_ref, k_ref, rows, bc, btot, k):
    sl = slice(h * DQK, (h + 1) * DQK)
    bch = bc[:, sl]
    bth = btot[:, sl]
    gq = q_ref[rows, h * DQK:(h + 1) * DQK].astype(F32)
    gk = k_ref[rows, NH * DQK + h * DQK:NH * DQK + (h + 1) * DQK].astype(F32)
    e_pos = jnp.exp(bch) * (DQK ** -0.5)
    e_neg = jnp.exp(-bch)
    e_end = jnp.exp(bth - bch)
    qd = gq * e_pos
    ki = gk * e_neg
    ke = gk * e_end
    att = jnp.where(k["tri"], _nt(qd.astype(BF), ki.astype(BF)), 0.0)
    return dict(e_pos=e_pos, e_neg=e_neg, e_end=e_end, qd=qd, ki=ki, ke=ke, att=att, decay=jnp.exp(bth))


def _gla_fwd(pbig, small, a2p, a2b, headg, name):
    tp = pbig.shape[0]
    nb = tp // TM

    def body(qk_ref, v_ref, gr_ref, sm_ref, a2_ref, a2b_ref, hg_ref, y_ref, ss_ref, s_scr, lg_scr):
        blk = pl.program_id(0)

        @pl.when(blk == 0)
        def _():
            s_scr[...] = jnp.zeros_like(s_scr)

        k = _chunk_consts()
        _, loga = _gla_loga(sm_ref, a2_ref, a2b_ref, blk)
        lg_scr[...] = loga

        def chunk(c, carry):
            r0 = pl.multiple_of(c * CHUNK, CHUNK)
            rows = pl.ds(r0, CHUNK)
            bc = _nn(k["tril_f"], lg_scr[rows, :], precision=HI)
            btot = _last_row(bc, k["rowio"])
            for h in range(NH):
                f = _gla_head(h, qk_ref, qk_ref, rows, bc, btot, k)
                vh = v_ref[rows, h * DV:(h + 1) * DV]
                s_st = s_scr[h]
                sb = s_st.astype(BF)
                qdb = f["qd"].astype(BF)
                o = _nn(f["att"].astype(BF), vh) + _nt(qdb, sb)
                gr = gr_ref[rows, h * DV:(h + 1) * DV].astype(F32)
                y_ref[rows, h * DV:(h + 1) * DV] = _headnorm_fwd(o, hg_ref[:, h * DV:(h + 1) * DV], gr * _sigmoid(gr)).astype(BF)
                ss_ref[c, h] = sb
                s_scr[h] = s_st * f["decay"] + _tn(vh, f["ke"].astype(BF))
            return carry

        lax.fori_loop(0, CPB, chunk, 0)

    return pl.pallas_call(
        body, grid=(nb,),
        in_specs=[_row_spec(D, CB_GQK), _row_spec(D, CB_GV), _row_spec(D, CB_GR), _row_spec(N_SMALL),
                  _full_spec((N_SMALL, NH * DQK)), _full_spec((1, NH * DQK)), _full_spec((1, D))],
        out_specs=[_row_spec(D), pl.BlockSpec((CPB, NH, DV, DQK), lambda i: (i, 0, 0, 0))],
        out_shape=(jax.ShapeDtypeStruct((tp, D), BF), jax.ShapeDtypeStruct((tp // CHUNK, NH, DV, DQK), BF)),
        scratch_shapes=[pltpu.VMEM((NH, DV, DQK), F32), pltpu.VMEM((TM, NH * DQK), F32)],
        compiler_params=_params(), name=name)(pbig, pbig, pbig, small, a2p, a2b, headg)


def _gla_bwd(dy, pbig, small, a2p, a2b, headg, ss, dsm_m, dproj, name):
    tp = pbig.shape[0]
    nb = tp // TM
    nqk = NH * DQK

    def body(dy_ref, qk_ref, v_ref, gr_ref, sm_ref, a2_ref, a2b_ref, hg_ref, ss_ref, dsmm_ref, _,
             dproj_ref, dsm_ref, da2_ref, da2b_ref, dhg_ref, ds_scr, lg_scr, dza_scr):
        step = pl.program_id(0)
        blk = nb - 1 - step

        @pl.when(step == 0)
        def _():
            ds_scr[...] = jnp.zeros_like(ds_scr)
            da2_ref[...] = jnp.zeros_like(da2_ref)
            da2b_ref[...] = jnp.zeros_like(da2b_ref)
            dhg_ref[...] = jnp.zeros_like(dhg_ref)

        k = _chunk_consts()
        rowio = k["rowio"]
        za, loga = _gla_loga(sm_ref, a2_ref, a2b_ref, blk)
        lg_scr[...] = loga

        def chunk(cc, carry):
            c = CPB - 1 - cc
            r0 = pl.multiple_of(c * CHUNK, CHUNK)
            rows = pl.ds(r0, CHUNK)
            bc = _nn(k["tril_f"], lg_scr[rows, :], precision=HI)
            btot = _last_row(bc, rowio)
            dbc_parts = []
            for h in range(NH):
                f = _gla_head(h, qk_ref, qk_ref, rows, bc, btot, k)
                vh = v_ref[rows, h * DV:(h + 1) * DV]
                s_b = ss_ref[c, h]
                s_f = s_b.astype(F32)
                qdb = f["qd"].astype(BF)
                attb = f["att"].astype(BF)
                o = _nn(attb, vh) + _nt(qdb, s_b)
                gr = gr_ref[rows, h * DV:(h + 1) * DV].astype(F32)
                sg = _sigmoid(gr)
                gain = hg_ref[:, h * DV:(h + 1) * DV]
                do, dact, dgain = _headnorm_bwd(dy_ref[rows, h * DV:(h + 1) * DV].astype(F32), o, gain, gr * sg)
                dproj_ref[rows, 2 * D + h * DV:2 * D + (h + 1) * DV] = (dact * sg * (1.0 + gr * (1.0 - sg))).astype(BF)
                dhg_ref[:, h * DV:(h + 1) * DV] += dgain

                ds_new = ds_scr[h]
                dsb = ds_new.astype(BF)
                dob = do.astype(BF)
                keb = f["ke"].astype(BF)
                dproj_ref[rows, D + h * DV:D + (h + 1) * DV] = (_tn(attb, dob) + _nt(keb, dsb)).astype(BF)
                datt = jnp.where(k["tri"], _nt(dob, vh), 0.0).astype(BF)
                dqd = _nn(datt, f["ki"].astype(BF)) + _nn(dob, s_b)
                dki = _tn(datt, qdb)
                dke = _nn(vh, dsb)
                dproj_ref[rows, h * DQK:(h + 1) * DQK] = (dqd * f["e_pos"]).astype(BF)
                dproj_ref[rows, nqk + h * DQK:nqk + (h + 1) * DQK] = (dki * f["e_neg"] + dke * f["e_end"]).astype(BF)
                dke_ke = dke * f["ke"]
                dbtot = jnp.sum(dke_ke, axis=0, keepdims=True) + jnp.sum(ds_new * s_f, axis=0, keepdims=True) * f["decay"]
                dbc_parts.append(dqd * f["qd"] - dki * f["ki"] - dke_ke + jnp.where(rowio == CHUNK - 1, dbtot, 0.0))
                ds_scr[h] = ds_new * f["decay"] + _tn(dob, qdb)
            dbc = jnp.concatenate(dbc_parts, axis=1)
            dza_scr[rows, :] = _nn(k["triu_f"], dbc, precision=HI)
            return carry

        lax.fori_loop(0, CPB, chunk, 0)
        row = blk * TM + lax.broadcasted_iota(jnp.int32, (TM, 1), 0)
        dza = jnp.where(row >= FIRST_VALID, dza_scr[...] * (_sigmoid(-za) / G_TAU), 0.0)
        dzb = dza.astype(BF)
        dsm_ref[...] = (_nt(dzb, a2_ref[...]) + dsmm_ref[...]).astype(BF)
        da2_ref[...] += _tn(sm_ref[...].astype(BF), dzb)
        da2b_ref[...] += jnp.sum(dza, axis=0, keepdims=True)

    rspec = lambda width, col=0: pl.BlockSpec((TM, width), lambda i: (nb - 1 - i, col))
    return pl.pallas_call(
        body, grid=(nb,),
        in_specs=[rspec(D), rspec(D, CB_GQK), rspec(D, CB_GV), rspec(D, CB_GR), rspec(N_SMALL),
                  _full_spec((N_SMALL, nqk)), _full_spec((1, nqk)), _full_spec((1, D)),
                  pl.BlockSpec((CPB, NH, DV, DQK), lambda i: (nb - 1 - i, 0, 0, 0)), rspec(N_SMALL), ANY],
        out_specs=[rspec(3 * D, 0), rspec(N_SMALL), _full_spec((N_SMALL, nqk)), _full_spec((1, nqk)), _full_spec((1, D))],
        out_shape=(jax.ShapeDtypeStruct(dproj.shape, BF),
                   jax.ShapeDtypeStruct((tp, N_SMALL), BF), jax.ShapeDtypeStruct((N_SMALL, nqk), F32),
                   jax.ShapeDtypeStruct((1, nqk), F32), jax.ShapeDtypeStruct((1, D), F32)),
        scratch_shapes=[pltpu.VMEM((NH, DV, DQK), F32), pltpu.VMEM((TM, nqk), F32), pltpu.VMEM((TM, nqk), F32)],
        input_output_aliases={10: 0}, compiler_params=_params(), name=name)(dy, pbig, pbig, pbig, small, a2p, a2b, headg, ss, dsm_m, dproj)


PIECE_BYTES = 1 << 20
MAX_PIECES = 32


def _place():
    return lax.axis_index("x"), lax.axis_index("y"), lax.axis_index("c")


def _piece_rows(rows, row_bytes, align):
    want = min(MAX_PIECES, max(1, -(-rows * row_bytes // PIECE_BYTES)))
    best = rows
    for k in range(1, want + 1):
        if rows % k == 0 and (rows // k) % align == 0:
            best = rows // k
    return best


def _remote(src, dst, send_sems, recv_sems, k, to):
    return pltpu.make_async_remote_copy(src_ref=src, dst_ref=dst, send_sem=send_sems.at[k], recv_sem=recv_sems.at[k],
                                        device_id=to, device_id_type=MESH)


def _all_gather_chips(p, name):
    r, n = p.shape
    rh = r // 2
    align = 32 // p.dtype.itemsize
    assert r % (2 * align) == 0
    cr = _piece_rows(rh, n * p.dtype.itemsize, align)

    def body(p_ref, o_ref, send_sems, recv_sems):
        x, y, c = _place()
        chips = [(1 - x, y), (x, 1 - y), (1 - x, 1 - y)]
        sib = (x, y, 1 - c)

        def half(hc, piece=None):
            if piece is None:
                return pl.ds(pl.multiple_of(hc * rh, align), rh)
            return pl.ds(pl.multiple_of(hc * rh + piece * cr, align), cr)

        first = [_remote(p_ref.at[half(c)], o_ref.at[j, half(c)], send_sems, recv_sems, j, (*chip, c))
                 for j, chip in enumerate(chips)]
        for cp in first:
            cp.start()
        for j, cp in enumerate(first):
            cp.wait_recv()
            for i in range(rh // cr):
                _remote(o_ref.at[j, half(c, i)], o_ref.at[j, half(c, i)], send_sems, recv_sems, 3 + j, sib).start()
        for j in range(3):
            block = _remote(o_ref.at[j, half(c)], o_ref.at[j, half(1 - c)], send_sems, recv_sems, 3 + j, sib)
            block.wait_send()
            block.wait_recv()
        for cp in first:
            cp.wait_send()

    return pl.pallas_call(
        body, in_specs=[ANY], out_specs=ANY, out_shape=jax.ShapeDtypeStruct((3, r, n), p.dtype),
        scratch_shapes=[pltpu.SemaphoreType.DMA((6,)), pltpu.SemaphoreType.DMA((6,))],
        name=name)(p)


def _by_chip(mine, others):
    me = 2 * lax.axis_index("x") + lax.axis_index("y")
    by_mask = jnp.stack([mine, others[1], others[0], others[2]])
    return [lax.dynamic_index_in_dim(by_mask, q ^ me, 0, keepdims=False) for q in range(4)]


def _swap_halves(items, name):
    k = len(items)

    def body(*refs):
        a_refs, got_refs = refs[:k], refs[k:2 * k]
        send_sems, recv_sems = refs[2 * k:]
        x, y, c = _place()
        sib = (x, y, 1 - c)
        for i, a in enumerate(items):
            _, r, n = a.shape
            rh = r // 2
            cr = _piece_rows(rh, n * a.dtype.itemsize, 8)
            for q in range(4):
                for t in range(rh // cr):
                    other = pl.ds(pl.multiple_of((1 - c) * rh + t * cr, 8), cr)
                    _remote(a_refs[i].at[q, other], got_refs[i].at[q, pl.ds(t * cr, cr)], send_sems, recv_sems, i, sib).start()
        for i, a in enumerate(items):
            block = _remote(a_refs[i].at[:, pl.ds(0, a.shape[1] // 2)], got_refs[i], send_sems, recv_sems, i, sib)
            block.wait_send()
            block.wait_recv()

    return pl.pallas_call(
        body, in_specs=[ANY] * k, out_specs=[ANY] * k,
        out_shape=tuple(jax.ShapeDtypeStruct((4, a.shape[1] // 2, a.shape[2]), a.dtype) for a in items),
        scratch_shapes=[pltpu.SemaphoreType.DMA((k,)), pltpu.SemaphoreType.DMA((k,))], name=name)(*items)


def _scatter_chips(items, name):
    k = len(items)

    def body(*refs):
        s_refs, o_refs = refs[:k], refs[k:2 * k]
        send_sems, recv_sems = refs[2 * k:]
        x, y, c = _place()
        chips = [(1 - x, y), (x, 1 - y), (1 - x, 1 - y)]
        sent = []
        for i in range(k):
            for j, (cx, cy) in enumerate(chips):
                cp = _remote(s_refs[i].at[2 * cx + cy], o_refs[i].at[j], send_sems, recv_sems, 3 * i + j, (cx, cy, c))
                cp.start()
                sent.append(cp)
        for cp in sent:
            cp.wait_recv()
        for cp in sent:
            cp.wait_send()

    return pl.pallas_call(
        body, in_specs=[ANY] * k, out_specs=[ANY] * k,
        out_shape=tuple(jax.ShapeDtypeStruct((3,) + s.shape[1:], s.dtype) for s in items),
        scratch_shapes=[pltpu.SemaphoreType.DMA((3 * k,)), pltpu.SemaphoreType.DMA((3 * k,))], name=name)(*items)


def _join_halves(items, name):
    k = len(items)

    def body(*refs):
        f_refs, o_refs = refs[:k], refs[k:2 * k]
        send_sems, recv_sems = refs[2 * k:]
        x, y, c = _place()
        sib = (x, y, 1 - c)
        for i, f in enumerate(items):
            rh, n = f.shape
            cr = _piece_rows(rh, n * f.dtype.itemsize, 8)
            for t in range(rh // cr):
                rows = pl.ds(t * cr, cr)
                _remote(f_refs[i].at[rows], o_refs[i].at[rows], send_sems, recv_sems, i, sib).start()
        for i in range(k):
            block = _remote(f_refs[i], o_refs[i], send_sems, recv_sems, i, sib)
            block.wait_send()
            block.wait_recv()

    return pl.pallas_call(
        body, in_specs=[ANY] * k, out_specs=[ANY] * k, out_shape=tuple(jax.ShapeDtypeStruct(f.shape, f.dtype) for f in items),
        scratch_shapes=[pltpu.SemaphoreType.DMA((k,)), pltpu.SemaphoreType.DMA((k,))], name=name)(*items)


SMALL_ROWS = 16
SMALL_SHARD_SHAPES = [(N_META, 256), (4, 256), (G_RANK, 128), (NH, 64), (NH, 64)]
REPL_SHAPES = [(1, D), (1, D), (1, 2, NH), (1, NH * DQK), (1, D), (D,)]
W_IN_SHARD = 2054


def _pack_small(parts):
    flat = jnp.concatenate([p.reshape(-1) for p in parts])
    return jnp.pad(flat, (0, SMALL_ROWS * D - flat.shape[0])).reshape(SMALL_ROWS, D)


def _unpack_small(block, shapes):
    flat, out, off = block.reshape(-1), [], 0
    for shp in shapes:
        n = 1
        for s in shp:
            n *= s
        out.append(flat[off:off + n].reshape(shp))
        off += n
    return out


def _proj_cols_from_w_in(w_in_f):
    w_big = jnp.concatenate([w_in_f[:, 3080:5128], w_in_f[:, 5144:6168], w_in_f[:, 0:1024], w_in_f[:, 6168:8216],
                             w_in_f[:, 1024:2048], w_in_f[:, 2056:3080]], axis=1)
    w_small = jnp.concatenate([w_in_f[:, 2048:2056], w_in_f[:, 5128:5144], jnp.zeros((D, N_SMALL - 24), w_in_f.dtype)], axis=1)
    return w_big, w_small


def _w_in_from_proj_cols(d_wall):
    big, small = d_wall[:, 0:N_BIG], d_wall[:, N_BIG:N_ALL]
    return jnp.concatenate([big[:, 3072:4096], big[:, 6144:7168], small[:, 0:8], big[:, 7168:8192], big[:, 0:2048],
                            small[:, 8:24], big[:, 2048:3072], big[:, 4096:6144]], axis=1)


def kernel(x, meta_tokens, norm1_g, w_in, conv_w, conv_b, m_gate_b, g_a2, g_a2_b, m_head_g, g_head_g, w_branch_m, w_branch_g, w_out, norm2_g, w_ff_gate, w_ff_up, w_ff_down, final_g, loss_target, m_meta_tokens, m_norm1_g, m_w_in, m_conv_w, m_conv_b, m_m_gate_b, m_g_a2, m_g_a2_b, m_m_head_g, m_g_head_g, m_w_branch_m, m_w_branch_g, m_w_out, m_norm2_g, m_w_ff_gate, m_w_ff_up, m_w_ff_down, m_final_g, v_meta_tokens, v_norm1_g, v_w_in, v_conv_w, v_conv_b, v_m_gate_b, v_g_a2, v_g_a2_b, v_m_head_g, v_g_head_g, v_w_branch_m, v_w_branch_g, v_w_out, v_norm2_g, v_w_ff_gate, v_w_ff_up, v_w_ff_down, v_final_g):
    w = _gather_weights(w_in, w_branch_m, w_branch_g, w_out, w_ff_gate, w_ff_up, w_ff_down, meta_tokens, conv_w, g_a2, m_head_g, g_head_g)
    loss_local, dh0, local = _local_step(x[0], loss_target[0], w, norm1_g, conv_b, m_gate_b, g_a2_b, norm2_g, final_g)
    grads = _reduce_grads(local)

    weights = [w_in, w_branch_m, w_branch_g, w_out, w_ff_gate, w_ff_up, w_ff_down, meta_tokens, conv_w, g_a2, m_head_g, g_head_g,
               norm1_g, conv_b, m_gate_b, g_a2_b, norm2_g, final_g]
    moms = [m_w_in, m_w_branch_m, m_w_branch_g, m_w_out, m_w_ff_gate, m_w_ff_up, m_w_ff_down, m_meta_tokens, m_conv_w, m_g_a2,
            m_m_head_g, m_g_head_g, m_norm1_g, m_conv_b, m_m_gate_b, m_g_a2_b, m_norm2_g, m_final_g]
    vels = [v_w_in, v_w_branch_m, v_w_branch_g, v_w_out, v_w_ff_gate, v_w_ff_up, v_w_ff_down, v_meta_tokens, v_conv_w, v_g_a2,
            v_m_head_g, v_g_head_g, v_norm1_g, v_conv_b, v_m_gate_b, v_g_a2_b, v_norm2_g, v_final_g]
    res = {}
    for nm, wt, g, m, v in zip(PACK_ORDER, weights, grads, moms, vels):
        two_d = (wt.size // wt.shape[-1], wt.shape[-1])
        d, nm_, nv_ = _adamw(wt.reshape(two_d), g.reshape(two_d), m.reshape(two_d), v.reshape(two_d), "adamw_" + nm)
        res[nm] = (g.reshape(wt.shape), d.reshape(wt.shape), nm_.reshape(wt.shape), nv_.reshape(wt.shape))

    order = ["meta_tokens", "norm1_g", "w_in", "conv_w", "conv_b", "m_gate_b", "g_a2", "g_a2_b", "m_head_g", "g_head_g",
             "w_branch_m", "w_branch_g", "w_out", "norm2_g", "w_ff_gate", "w_ff_up", "w_ff_down", "final_g"]
    loss = lax.psum(loss_local[0, 0], ("x", "y", "c"))
    grad_x = dh0[TM:].reshape(x.shape)
    return (loss, grad_x, *[res[n][0] for n in order], *[res[n][1] for n in order],
            *[res[n][2] for n in order], *[res[n][3] for n in order])


PACK_ORDER = ["w_in", "w_branch_m", "w_branch_g", "w_out", "w_ff_gate", "w_ff_up", "w_ff_down", "meta_tokens", "conv_w", "g_a2",
              "m_head_g", "g_head_g", "norm1_g", "conv_b", "m_gate_b", "g_a2_b", "norm2_g", "final_g"]


def _gather_weights(w_in, w_branch_m, w_branch_g, w_out, w_ff_gate, w_ff_up, w_ff_down, meta_tokens, conv_w, g_a2, m_head_g, g_head_g):
    bf = lambda a: a.astype(BF)
    rows_local = jnp.concatenate([bf(w_branch_m[0]), bf(w_branch_g[0]), bf(w_out[0]), bf(w_ff_down[0]),
                                  bf(w_ff_gate[0].T), bf(w_ff_up[0].T)], axis=0)
    win_local = bf(w_in[0])
    small_local = _pack_small([meta_tokens, conv_w[0], g_a2[0], m_head_g[0], g_head_g[0]])
    rows_all = jnp.stack(_by_chip(rows_local, _all_gather_chips(rows_local, "gather_rows")))
    win_all = _by_chip(win_local, _all_gather_chips(win_local, "gather_w_in"))
    small_all = _by_chip(small_local, _all_gather_chips(small_local, "gather_small"))
    cut = lambda lo, hi: rows_all[:, lo:hi].reshape(4 * (hi - lo), D)
    wbm, wbg, wout, wdown = cut(0, 256), cut(256, 512), cut(512, 768), cut(768, 1472)
    wgu_t = jnp.concatenate([cut(1472, 2176), cut(2176, 2880)], axis=0)
    w_in_f = jnp.concatenate([win_all[q] for q in range(4)], axis=1)
    small_sh = [_unpack_small(small_all[q], SMALL_SHARD_SHAPES) for q in range(4)]
    cat = lambda i: jnp.concatenate([s[i] for s in small_sh], axis=-1)
    return dict(w_in=w_in_f, wbm=wbm, wbg=wbg, wout=wout, wgu_t=wgu_t, wdown=wdown, meta=cat(0), convw=cat(1), ga2=cat(2),
                mhg=cat(3).reshape(1, D), ghg=cat(4).reshape(1, D))


def _local_step(x0, target, w, norm1_g, conv_b, m_gate_b, g_a2_b, norm2_g, final_g):
    w_in_f, wbm, wbg, wout, wgu_t, wdown = w["w_in"], w["wbm"], w["wbg"], w["wout"], w["wgu_t"], w["wdown"]
    meta_f, convw_f, ga2_f, mhg_f, ghg_f = w["meta"], w["convw"], w["ga2"], w["mhg"], w["ghg"]
    w_big, w_small = _proj_cols_from_w_in(w_in_f)
    w_all = jnp.concatenate([w_big, w_small], axis=1)
    gbias = jnp.concatenate([m_gate_b.reshape(1, 2 * NH), jnp.zeros((1, N_SMALL - 2 * NH), F32)], axis=1)
    a2p = jnp.concatenate([jnp.zeros((8, NH * DQK), F32), ga2_f, jnp.zeros((N_SMALL - 24, NH * DQK), F32)], axis=0).astype(BF)
    convb = conv_b.reshape(1, D)
    g1 = norm1_g.reshape(1, D)
    g2 = norm2_g.reshape(1, D)
    gf = final_g.reshape(1, D)
    h0 = jnp.concatenate([jnp.zeros((FIRST_VALID, D), F32), meta_f, x0], axis=0)

    xn1, rstd1 = _rms_fwd(h0, g1, "rms1")
    pbig = _mm(xn1, w_big, nt=False, out_dtype=BF, tn=1024, name="proj_big")
    small = _mm(xn1, w_small, nt=False, out_dtype=F32, tn=N_SMALL, name="proj_small")
    qk = _conv_fwd(pbig, convw_f, convb, "conv_fwd")
    y_m, m_cs, m_ns = _mlstm_fwd(qk, pbig, small, gbias, mhg_f, "mlstm_fwd")
    y_g, g_ss = _gla_fwd(pbig, small, a2p, g_a2_b, ghg_f, "gla_fwd")
    p_m = _mm(y_m, wbm, nt=False, out_dtype=BF, tn=1024, name="branch_m")
    p_g = _mm(y_g, wbg, nt=False, out_dtype=BF, tn=1024, name="branch_g")
    merged = _merge_fwd(p_m, p_g, pbig, "merge_fwd")
    h1 = _mm(merged, wout, nt=False, out_dtype=F32, tn=1024, addend=h0, name="out_proj")
    hn, rstd2 = _rms_fwd(h1, g2, "rms2")
    gu = _mm(hn, wgu_t, nt=True, out_dtype=BF, tn=1408, name="ff_in")
    ff = _swiglu_fwd(gu, "swiglu_fwd")
    h2 = _mm(ff, wdown, nt=False, out_dtype=F32, tn=1024, addend=h1, name="ff_down")
    dh2, loss_local, d_final_g = _final_loss(h2, target, gf, "final_loss")

    dff = _mm(dh2, wdown, nt=True, out_dtype=BF, tn=1408, name="d_ff")
    d_wdown = _mm_tn(ff, dh2, tm=1408, tn=1024, name="dw_ff_down")
    dgu = _swiglu_bwd(dff, gu, "swiglu_bwd")
    dhn = _mm(dgu, wgu_t, nt=False, out_dtype=F32, tn=1024, tk=2816, name="d_hn")
    d_wgu_t = _mm_tn(dgu, hn, tm=1408, tn=1024, name="dw_ff_in")
    dh1, d_g2 = _rms_bwd(dhn, h1, rstd2, g2, dh2, "rms2_bwd")
    dmerged = _mm(dh1, wout, nt=True, out_dtype=BF, tn=1024, name="d_merged")
    d_wout = _mm_tn(merged, dh1, tm=1024, tn=1024, name="dw_out")
    dp_m, dp_g, dproj = _merge_bwd(dmerged, p_m, p_g, pbig, "merge_bwd")
    dy_m = _mm(dp_m, wbm, nt=True, out_dtype=BF, tn=1024, name="d_ym")
    dy_g = _mm(dp_g, wbg, nt=True, out_dtype=BF, tn=1024, name="d_yg")
    d_wbm = _mm_tn(y_m, dp_m, tm=1024, tn=1024, name="dw_branch_m")
    d_wbg = _mm_tn(y_g, dp_g, tm=1024, tn=1024, name="dw_branch_g")
    dqk_m, dproj, dsm_m, d_gbias, d_mhg = _mlstm_bwd(dy_m, qk, pbig, small, gbias, mhg_f, m_cs, m_ns, dproj, "mlstm_bwd")
    dconv, d_convwb = _conv_bwd_pre(dqk_m, pbig, convw_f, convb, "conv_bwd_pre")
    dproj = _conv_bwd_in(dconv, convw_f, dproj, "conv_bwd_in")
    dproj, dsmall, d_a2p, d_a2b, d_ghg = _gla_bwd(dy_g, pbig, small, a2p, g_a2_b, ghg_f, g_ss, dsm_m, dproj, "gla_bwd")
    dproj = _place_small(dsmall, dproj, "dproj_small")
    dxn = _mm(dproj, w_all, nt=True, out_dtype=F32, tn=1024, tk=1664, name="d_xn")
    d_wall = _mm_tn(xn1, dproj, tm=1024, tn=1664, name="dw_in")
    dh0, d_g1 = _rms_bwd(dxn, h0, rstd1, g1, dh1, "rms1_bwd")

    small_sharded = [dh0[FIRST_VALID:TM], d_convwb[0:4], d_a2p[8:24], d_mhg.reshape(NH, DV), d_ghg.reshape(NH, DV)]
    replicated = [d_g1, d_convwb[4:5], d_gbias[0:1, 0:2 * NH].reshape(1, 2, NH), d_a2b, d_g2, d_final_g.reshape(D)]
    local = dict(w_all=d_wall, wbm=d_wbm, wbg=d_wbg, wout=d_wout, wdown=d_wdown, wgu_t=d_wgu_t,
                 small_sharded=small_sharded, replicated=replicated)
    return loss_local, dh0, local


def _reduce_grads(local):
    d_win = _w_in_from_proj_cols(local["w_all"])
    win4 = jnp.stack([d_win[:, q * W_IN_SHARD:(q + 1) * W_IN_SHARD] for q in range(4)])
    small4 = jnp.stack([_pack_small([g[:, q * shp[1]:(q + 1) * shp[1]] for g, shp in zip(local["small_sharded"], SMALL_SHARD_SHAPES)]
                                    + local["replicated"]) for q in range(4)])
    gu = local["wgu_t"].reshape(2, 4, D_FF // 4, D)
    items = [win4, local["wbm"].reshape(4, 256, D), local["wbg"].reshape(4, 256, D), local["wout"].reshape(4, 256, D),
             local["wdown"].reshape(4, D_FF // 4, D), gu[0], gu[1], small4]
    c = lax.axis_index("c")
    me = 2 * lax.axis_index("x") + lax.axis_index("y")
    got = _swap_halves(items, "reduce_siblings")
    sums = []
    for i, (a, g) in enumerate(zip(items, got)):
        rh, n = g.shape[1], g.shape[2]
        own = lax.dynamic_slice_in_dim(a, c * rh, rh, axis=1)
        sums.append(_add2(own.reshape(-1, n), g.reshape(-1, n), F32 if i == len(items) - 1 else BF, f"reduce_add2_{i}").reshape(g.shape))
    from_chips = _scatter_chips(sums, "reduce_chips")
    halves = []
    for i, (s, f) in enumerate(zip(sums, from_chips)):
        mine = lax.dynamic_index_in_dim(s, me, 0, keepdims=False)
        if i == len(items) - 1:
            by_chip = _by_chip(mine, f)
            mine, f = by_chip[0], jnp.stack(by_chip[1:])
        halves.append(_add4(mine, f, f"reduce_add4_{i}"))
    got = _join_halves(halves, "reduce_join")
    full = [jnp.where(c == 0, jnp.concatenate([h, g], axis=0), jnp.concatenate([g, h], axis=0)) for h, g in zip(halves, got)]
    smalls = _unpack_small(full[7], SMALL_SHARD_SHAPES + REPL_SHAPES)
    return [full[0], full[1], full[2], full[3], full[5].T, full[6].T, full[4]] + smalls
```

```python
import functools

import jax
import jax.numpy as jnp
from jax import lax
from jax.experimental import pallas as pl
from jax.experimental.pallas import tpu as pltpu

F32 = jnp.float32
BF = jnp.bfloat16
HI = lax.Precision.HIGHEST
MESH = pl.DeviceIdType.MESH

D = 1024
N_META = 16
CHUNK = 128
EPS = 1e-6
NH = 4
DV = 256
DQK = 128
G_RANK = 16
G_TAU = 16.0
D_FF = 2816
TM = 512
FIRST_VALID = TM - N_META
CPB = TM // CHUNK
NEG = -1e30
N_BIG = 8192
CB_GQK, CB_GV, CB_GR, CB_MQK, CB_GM, CB_GG, CB_MV, CB_MO = range(8)
N_SMALL = 128
N_ALL = N_BIG + N_SMALL
VMEM_LIMIT = 56 * 1024 * 1024

ADAM_LR, ADAM_B1, ADAM_B2, ADAM_EPS, ADAM_WD, ADAM_STEP = 0.001, 0.9, 0.999, 1e-08, 0.01, 10

NT_DIMS = (((1,), (1,)), ((), ()))
TN_DIMS = (((0,), (0,)), ((), ()))


def _nt(a, b, **kw):
    return lax.dot_general(a, b, NT_DIMS, preferred_element_type=F32, **kw)


def _tn(a, b, **kw):
    return lax.dot_general(a, b, TN_DIMS, preferred_element_type=F32, **kw)


def _nn(a, b, **kw):
    return jnp.dot(a, b, preferred_element_type=F32, **kw)


def _params(**kw):
    return pltpu.CompilerParams(vmem_limit_bytes=VMEM_LIMIT, **kw)


def _sigmoid(x):
    return 1.0 / (1.0 + jnp.exp(-x))


def _logsig(x):
    return jnp.minimum(x, 0.0) - jnp.log(1.0 + jnp.exp(-jnp.abs(x)))


def _mm(a, b, *, nt, out_dtype, tn, tk=None, tm=TM, addend=None, name):
    m, k = a.shape
    n = b.shape[0] if nt else b.shape[1]
    tk = k if tk is None else tk
    nk = k // tk
    assert m % tm == 0 and n % tn == 0 and k % tk == 0
    dims = NT_DIMS if nt else (((1,), (0,)), ((), ()))

    def body(*refs):
        if addend is None:
            a_ref, b_ref, o_ref, acc_ref = refs
        else:
            a_ref, b_ref, add_ref, o_ref, acc_ref = refs
        kk = pl.program_id(2)
        part = lax.dot_general(a_ref[...].astype(BF), b_ref[...].astype(BF), dims, preferred_element_type=F32)

        @pl.when(kk == 0)
        def _():
            acc_ref[...] = part

        @pl.when(kk > 0)
        def _():
            acc_ref[...] += part

        @pl.when(kk == nk - 1)
        def _():
            r = acc_ref[...]
            if addend is not None:
                r = r + add_ref[...].astype(F32)
            o_ref[...] = r.astype(o_ref.dtype)

    in_specs = [pl.BlockSpec((tm, tk), lambda j, i, kk: (i, kk)),
                pl.BlockSpec((tn, tk), lambda j, i, kk: (j, kk)) if nt else pl.BlockSpec((tk, tn), lambda j, i, kk: (kk, j))]
    args = [a, b]
    if addend is not None:
        in_specs.append(pl.BlockSpec((tm, tn), lambda j, i, kk: (i, j)))
        args.append(addend)
    return pl.pallas_call(
        body, grid=(n // tn, m // tm, nk), in_specs=in_specs,
        out_specs=pl.BlockSpec((tm, tn), lambda j, i, kk: (i, j)),
        out_shape=jax.ShapeDtypeStruct((m, n), out_dtype),
        scratch_shapes=[pltpu.VMEM((tm, tn), F32)], compiler_params=_params(), name=name)(*args)


def _mm_tn(a, b, *, tm, tn, tk=TM, name):
    t, m = a.shape
    n = b.shape[1]
    assert t % tk == 0 and m % tm == 0 and n % tn == 0

    def body(a_ref, b_ref, o_ref):
        part = _tn(a_ref[...].astype(BF), b_ref[...].astype(BF))

        @pl.when(pl.program_id(2) == 0)
        def _():
            o_ref[...] = part

        @pl.when(pl.program_id(2) > 0)
        def _():
            o_ref[...] += part

    return pl.pallas_call(
        body, grid=(m // tm, n // tn, t // tk),
        in_specs=[pl.BlockSpec((tk, tm), lambda i, j, kk: (kk, i)), pl.BlockSpec((tk, tn), lambda i, j, kk: (kk, j))],
        out_specs=pl.BlockSpec((tm, tn), lambda i, j, kk: (i, j)),
        out_shape=jax.ShapeDtypeStruct((m, n), F32), compiler_params=_params(), name=name)(a, b)


ANY = pl.BlockSpec(memory_space=pl.ANY)


def _row_spec(width, col=0):
    return pl.BlockSpec((TM, width), lambda i: (i, col))


def _full_spec(shape):
    return pl.BlockSpec(shape, lambda i: (0,) * len(shape))


def _rms_fwd(h, g, name):
    tp = h.shape[0]

    def body(h_ref, g_ref, xn_ref, r_ref):
        x = h_ref[...]
        r = lax.rsqrt(jnp.mean(x * x, axis=1, keepdims=True) + EPS)
        xn_ref[...] = (x * r * g_ref[...]).astype(BF)
        r_ref[...] = r

    return pl.pallas_call(
        body, grid=(tp // TM,), in_specs=[_row_spec(D), _full_spec((1, D))],
        out_specs=[_row_spec(D), _row_spec(1)],
        out_shape=(jax.ShapeDtypeStruct((tp, D), BF), jax.ShapeDtypeStruct((tp, 1), F32)),
        compiler_params=_params(), name=name)(h, g)


def _rms_bwd(dxn, h, rstd, g, dres, name):
    tp = h.shape[0]

    def body(dxn_ref, h_ref, r_ref, g_ref, dres_ref, dh_ref, dg_ref):
        r = r_ref[...]
        xh = h_ref[...] * r
        dxn_v = dxn_ref[...].astype(F32)
        dxh = dxn_v * g_ref[...]
        dh = r * (dxh - xh * jnp.mean(dxh * xh, axis=1, keepdims=True))
        dh_ref[...] = dh + dres_ref[...]
        part = jnp.sum(dxn_v * xh, axis=0, keepdims=True)

        @pl.when(pl.program_id(0) == 0)
        def _():
            dg_ref[...] = part

        @pl.when(pl.program_id(0) > 0)
        def _():
            dg_ref[...] += part

    return pl.pallas_call(
        body, grid=(tp // TM,),
        in_specs=[_row_spec(D), _row_spec(D), _row_spec(1), _full_spec((1, D)), _row_spec(D)],
        out_specs=[_row_spec(D), _full_spec((1, D))],
        out_shape=(jax.ShapeDtypeStruct((tp, D), F32), jax.ShapeDtypeStruct((1, D), F32)),
        compiler_params=_params(), name=name)(dxn, h, rstd, g, dres)


def _final_loss(h2, target, gf, name):
    tp = h2.shape[0]

    def body(h_ref, t_ref, g_ref, dh_ref, loss_ref, dg_ref):
        i = pl.program_id(0)
        live = (i > 0).astype(F32)
        x = h_ref[...]
        r = lax.rsqrt(jnp.mean(x * x, axis=1, keepdims=True) + EPS)
        xh = x * r
        e = xh * g_ref[...] - t_ref[...]
        row_loss = jnp.mean(e * e, axis=1, keepdims=True)
        loss_part = 0.5 * live * jnp.sum(row_loss, axis=0, keepdims=True)
        dout = e * (live / D)
        dg_part = jnp.sum(dout * xh, axis=0, keepdims=True)
        dxh = dout * g_ref[...]
        dh_ref[...] = r * (dxh - xh * jnp.mean(dxh * xh, axis=1, keepdims=True))

        @pl.when(i == 0)
        def _():
            loss_ref[...] = loss_part
            dg_ref[...] = dg_part

        @pl.when(i > 0)
        def _():
            loss_ref[...] += loss_part
            dg_ref[...] += dg_part

    return pl.pallas_call(
        body, grid=(tp // TM,),
        in_specs=[_row_spec(D), pl.BlockSpec((TM, D), lambda i: (jnp.maximum(i - 1, 0), 0)), _full_spec((1, D))],
        out_specs=[_row_spec(D), _full_spec((1, 1)), _full_spec((1, D))],
        out_shape=(jax.ShapeDtypeStruct((tp, D), F32), jax.ShapeDtypeStruct((1, 1), F32), jax.ShapeDtypeStruct((1, D), F32)),
        compiler_params=_params(), name=name)(h2, target, gf)


def _shift_down(x, halo, k):
    rk = pltpu.roll(x, k, 0)
    io = lax.broadcasted_iota(jnp.int32, (8, x.shape[1]), 0)
    top = jnp.where(io < k, pltpu.roll(halo, k, 0), rk[0:8])
    return jnp.concatenate([top, rk[8:]], axis=0)


def _shift_up(x, nxt, k):
    n = x.shape[0]
    rk = pltpu.roll(x, n - k, 0)
    io = lax.broadcasted_iota(jnp.int32, (8, x.shape[1]), 0)
    bot = jnp.where(io >= 8 - k, pltpu.roll(nxt, 8 - k, 0), rk[n - 8:n])
    return jnp.concatenate([rk[:n - 8], bot], axis=0)


def _conv_pre(x, halo, w_ref, b_ref):
    c = x * w_ref[3:4, :] + b_ref[...]
    shifted = []
    for k in (1, 2, 3):
        s = _shift_down(x, halo, k)
        shifted.append(s)
        c = c + s * w_ref[3 - k:4 - k, :]
    return c, shifted


def _qk_scale():
    col = lax.broadcasted_iota(jnp.int32, (1, D), 1)
    return jnp.where(col < NH * DQK, DQK ** -0.5, 1.0).astype(F32)


def _halo_prev_spec():
    return pl.BlockSpec((8, D), lambda i: (jnp.maximum(i * (TM // 8) - 1, 0), CB_MQK))


def _conv_fwd(pbig, w, b, name):
    tp = pbig.shape[0]

    def body(x_ref, halo_ref, w_ref, b_ref, o_ref):
        x = x_ref[...].astype(F32)
        halo = jnp.where(pl.program_id(0) > 0, halo_ref[...].astype(F32), 0.0)
        c, _ = _conv_pre(x, halo, w_ref, b_ref)
        o_ref[...] = (c * _sigmoid(c) * _qk_scale()).astype(BF)

    return pl.pallas_call(
        body, grid=(tp // TM,),
        in_specs=[_row_spec(D, CB_MQK), _halo_prev_spec(), _full_spec((4, D)), _full_spec((1, D))],
        out_specs=_row_spec(D), out_shape=jax.ShapeDtypeStruct((tp, D), BF),
        compiler_params=_params(), name=name)(pbig, pbig, w, b)


def _conv_bwd_pre(dqk, pbig, w, b, name):
    tp = pbig.shape[0]

    def body(d_ref, x_ref, halo_ref, w_ref, b_ref, dc_ref, dwb_ref):
        x = x_ref[...].astype(F32)
        halo = jnp.where(pl.program_id(0) > 0, halo_ref[...].astype(F32), 0.0)
        c, shifted = _conv_pre(x, halo, w_ref, b_ref)
        sg = _sigmoid(c)
        dc = d_ref[...] * _qk_scale() * (sg * (1.0 + c * (1.0 - sg)))
        dc_ref[...] = dc
        taps = [shifted[2], shifted[1], shifted[0], x]
        rows = [jnp.sum(dc * t, axis=0, keepdims=True) for t in taps] + [jnp.sum(dc, axis=0, keepdims=True)]
        io = lax.broadcasted_iota(jnp.int32, (8, D), 0)
        part = jnp.zeros((8, D), F32)
        for r, v in enumerate(rows):
            part = jnp.where(io == r, v, part)

        @pl.when(pl.program_id(0) == 0)
        def _():
            dwb_ref[...] = part

        @pl.when(pl.program_id(0) > 0)
        def _():
            dwb_ref[...] += part

    return pl.pallas_call(
        body, grid=(tp // TM,),
        in_specs=[_row_spec(D), _row_spec(D, CB_MQK), _halo_prev_spec(), _full_spec((4, D)), _full_spec((1, D))],
        out_specs=[_row_spec(D), _full_spec((8, D))],
        out_shape=(jax.ShapeDtypeStruct((tp, D), F32), jax.ShapeDtypeStruct((8, D), F32)),
        compiler_params=_params(), name=name)(dqk, pbig, pbig, w, b)


def _conv_bwd_in(dc, w, dproj, name):
    tp = dc.shape[0]
    nb = tp // TM

    def body(d_ref, nxt_ref, w_ref, _, o_ref):
        d = d_ref[...]
        nxt = jnp.where(pl.program_id(0) < nb - 1, nxt_ref[...], 0.0)
        acc = d * w_ref[3:4, :]
        for k in (1, 2, 3):
            acc = acc + _shift_up(d, nxt, k) * w_ref[3 - k:4 - k, :]
        o_ref[...] = acc.astype(BF)

    return pl.pallas_call(
        body, grid=(nb,),
        in_specs=[_row_spec(D), pl.BlockSpec((8, D), lambda i: (jnp.minimum((i + 1) * (TM // 8), tp // 8 - 1), 0)),
                  _full_spec((4, D)), ANY],
        out_specs=_row_spec(D, CB_MQK), out_shape=jax.ShapeDtypeStruct(dproj.shape, BF),
        input_output_aliases={3: 0}, compiler_params=_params(), name=name)(dc, dc, w, dproj)


def _merge_fwd(pm, pg, pbig, name):
    tp = pm.shape[0]

    def body(pm_ref, pg_ref, gm_ref, gg_ref, o_ref):
        o_ref[...] = (_sigmoid(gm_ref[...].astype(F32)) * pm_ref[...].astype(F32)
                      + _sigmoid(gg_ref[...].astype(F32)) * pg_ref[...].astype(F32)).astype(BF)

    return pl.pallas_call(
        body, grid=(tp // TM,), in_specs=[_row_spec(D), _row_spec(D), _row_spec(D, CB_GM), _row_spec(D, CB_GG)],
        out_specs=_row_spec(D), out_shape=jax.ShapeDtypeStruct((tp, D), BF),
        compiler_params=_params(), name=name)(pm, pg, pbig, pbig)


def _merge_bwd(dmerged, pm, pg, pbig, name):
    tp = pm.shape[0]

    def body(d_ref, pm_ref, pg_ref, gm_ref, gg_ref, dpm_ref, dpg_ref, dproj_ref):
        d = d_ref[...].astype(F32)
        sm = _sigmoid(gm_ref[...].astype(F32))
        sg = _sigmoid(gg_ref[...].astype(F32))
        dpm_ref[...] = (d * sm).astype(BF)
        dpg_ref[...] = (d * sg).astype(BF)
        dproj_ref[:, 0:D] = (d * pm_ref[...].astype(F32) * sm * (1.0 - sm)).astype(BF)
        dproj_ref[:, D:2 * D] = (d * pg_ref[...].astype(F32) * sg * (1.0 - sg)).astype(BF)

    shp = jax.ShapeDtypeStruct((tp, D), BF)
    return pl.pallas_call(
        body, grid=(tp // TM,), in_specs=[_row_spec(D), _row_spec(D), _row_spec(D), _row_spec(D, CB_GM), _row_spec(D, CB_GG)],
        out_specs=[_row_spec(D), _row_spec(D), _row_spec(2 * D, CB_GM // 2)],
        out_shape=(shp, shp, jax.ShapeDtypeStruct((tp, N_ALL), BF)),
        compiler_params=_params(), name=name)(dmerged, pm, pg, pbig, pbig)


def _swiglu_fwd(gu, name):
    tp = gu.shape[0]

    def body(g_ref, u_ref, o_ref):
        g = g_ref[...].astype(F32)
        o_ref[...] = (g * _sigmoid(g) * u_ref[...].astype(F32)).astype(BF)

    return pl.pallas_call(
        body, grid=(tp // TM,), in_specs=[_row_spec(D_FF, 0), _row_spec(D_FF, 1)],
        out_specs=_row_spec(D_FF), out_shape=jax.ShapeDtypeStruct((tp, D_FF), BF),
        compiler_params=_params(), name=name)(gu, gu)


def _swiglu_bwd(dff, gu, name):
    tp = gu.shape[0]

    def body(d_ref, g_ref, u_ref, o_ref):
        d = d_ref[...].astype(F32)
        g = g_ref[...].astype(F32)
        u = u_ref[...].astype(F32)
        sg = _sigmoid(g)
        o_ref[:, 0:D_FF] = (d * u * sg * (1.0 + g * (1.0 - sg))).astype(BF)
        o_ref[:, D_FF:2 * D_FF] = (d * g * sg).astype(BF)

    return pl.pallas_call(
        body, grid=(tp // TM,), in_specs=[_row_spec(D_FF), _row_spec(D_FF, 0), _row_spec(D_FF, 1)],
        out_specs=_row_spec(2 * D_FF), out_shape=jax.ShapeDtypeStruct((tp, 2 * D_FF), BF),
        compiler_params=_params(), name=name)(dff, gu, gu)


def _adamw(w, g, m, v, name):
    rows, cols = w.shape
    tr = 128 if rows % 128 == 0 else rows

    def body(w_ref, g_ref, m_ref, v_ref, d_ref, nm_ref, nv_ref):
        gv = g_ref[...]
        nm = ADAM_B1 * m_ref[...] + (1.0 - ADAM_B1) * gv
        nv = ADAM_B2 * v_ref[...] + (1.0 - ADAM_B2) * (gv * gv)
        m_hat = nm / (1.0 - ADAM_B1 ** ADAM_STEP)
        v_hat = nv / (1.0 - ADAM_B2 ** ADAM_STEP)
        d_ref[...] = -ADAM_LR * (m_hat / (jnp.sqrt(v_hat) + ADAM_EPS) + ADAM_WD * w_ref[...])
        nm_ref[...] = nm
        nv_ref[...] = nv

    spec = pl.BlockSpec((tr, cols), lambda i: (i, 0))
    shp = jax.ShapeDtypeStruct((rows, cols), F32)
    return pl.pallas_call(body, grid=(rows // tr,), in_specs=[spec] * 4, out_specs=[spec] * 3,
                          out_shape=(shp,) * 3, compiler_params=_params(), name=name)(w, g, m, v)


def _place_small(dsmall, dproj, name):
    tp = dsmall.shape[0]

    def body(s_ref, _, o_ref):
        o_ref[...] = s_ref[...]

    return pl.pallas_call(
        body, grid=(tp // TM,), in_specs=[_row_spec(N_SMALL), ANY], out_specs=_row_spec(N_SMALL, N_BIG // N_SMALL),
        out_shape=jax.ShapeDtypeStruct(dproj.shape, dproj.dtype), input_output_aliases={1: 0},
        compiler_params=_params(), name=name)(dsmall, dproj)


def _row_tile(rows, cap=512):
    best = rows
    for cand in range(8, min(rows, cap) + 1, 8):
        if rows % cand == 0:
            best = cand
    return best


def _add2(a, b, out_dtype, name):
    rows, cols = a.shape
    tr = _row_tile(rows)

    def body(a_ref, b_ref, o_ref):
        o_ref[...] = (a_ref[...] + b_ref[...]).astype(o_ref.dtype)

    spec = pl.BlockSpec((tr, cols), lambda i: (i, 0))
    return pl.pallas_call(body, grid=(rows // tr,), in_specs=[spec] * 2, out_specs=spec,
                          out_shape=jax.ShapeDtypeStruct((rows, cols), out_dtype), compiler_params=_params(), name=name)(a, b)


def _add4(first, rest, name):
    rows, cols = first.shape
    tr = _row_tile(rows, 256)

    def body(f_ref, r_ref, o_ref):
        up = lambda v: v.astype(F32)
        o_ref[...] = ((up(f_ref[...]) + up(r_ref[0])) + up(r_ref[1])) + up(r_ref[2])

    return pl.pallas_call(body, grid=(rows // tr,),
                          in_specs=[pl.BlockSpec((tr, cols), lambda i: (i, 0)), pl.BlockSpec((3, tr, cols), lambda i: (0, i, 0))],
                          out_specs=pl.BlockSpec((tr, cols), lambda i: (i, 0)),
                          out_shape=jax.ShapeDtypeStruct((rows, cols), F32), compiler_params=_params(), name=name)(first, rest)


def _chunk_consts():
    r2 = lax.broadcasted_iota(jnp.int32, (CHUNK, CHUNK), 0)
    c2 = lax.broadcasted_iota(jnp.int32, (CHUNK, CHUNK), 1)
    tri = r2 >= c2
    return dict(tri=tri, tril_f=tri.astype(F32), triu_f=(r2 <= c2).astype(F32),
                lane=lax.broadcasted_iota(jnp.int32, (CHUNK, N_SMALL), 1),
                rowio=lax.broadcasted_iota(jnp.int32, (CHUNK, 1), 0),
                ones=jnp.ones((CHUNK, N_SMALL), F32))


def _valid_rows(block, c):
    row = block * TM + c * CHUNK + lax.broadcasted_iota(jnp.int32, (CHUNK, 1), 0)
    return row >= FIRST_VALID


def _col(x, lane, idx):
    return jnp.sum(jnp.where(lane == idx, x, 0.0), axis=1, keepdims=True)


def _last_row(x, rowio):
    return jnp.sum(jnp.where(rowio == CHUNK - 1, x, 0.0), axis=0, keepdims=True)


def _sum_all(x):
    return jnp.sum(jnp.sum(x, axis=1, keepdims=True), axis=0, keepdims=True)


def _headnorm_fwd(hm, gain, gate_act):
    rs = lax.rsqrt(jnp.mean(hm * hm, axis=1, keepdims=True) + EPS)
    return hm * rs * gain * gate_act


def _headnorm_bwd(dy, hm, gain, gate_act):
    rs = lax.rsqrt(jnp.mean(hm * hm, axis=1, keepdims=True) + EPS)
    xh = hm * rs
    dact = dy * xh * gain
    dgain = jnp.sum(dy * gate_act * xh, axis=0, keepdims=True)
    dxh = dy * gate_act * gain
    dhm = rs * (dxh - xh * jnp.mean(dxh * xh, axis=1, keepdims=True))
    return dhm, dact, dgain


def _mlstm_gates(sm, gbias, valid, k):
    pre = sm + gbias
    lf = jnp.where(valid, _logsig(pre), 0.0)
    b_all = _nn(k["tril_f"], lf, precision=HI)
    li_all = jnp.where(valid, pre, NEG)
    return pre, li_all, b_all


def _mlstm_open(h, qh, kh, c_st, li_all, b_all, k):
    lane = k["lane"]
    sel = jnp.where(lane == h, 1.0, 0.0) - jnp.where(lane == NH + h, 1.0, 0.0)
    x = jnp.where(lane < NH, li_all, jnp.where(lane < 2 * NH, b_all, 0.0))
    cb = c_st.astype(BF)
    return dict(ubc=_nt(sel, x, precision=HI), sim=_nt(qh, kh), cb=cb, cq=_nt(qh, cb))


def _mlstm_weights(h, f, qh, vh, li_all, b_all, n_row, m11, k):
    lane, tri, rowio = k["lane"], k["tri"], k["rowio"]
    b_col = _col(b_all, lane, NH + h)
    li_col = _col(li_all, lane, h)
    dmat = jnp.where(tri, b_col + f["ubc"], NEG)
    m_row = jnp.maximum(b_col + m11, jnp.max(dmat, axis=1, keepdims=True))
    e = jnp.exp(dmat - m_row)
    w_mat = e * f["sim"]
    a = jnp.exp(b_col + m11 - m_row)
    qf = qh.astype(F32)
    nq = jnp.sum(qf * n_row, axis=1, keepdims=True)
    g = _last_row(b_col, rowio)
    wlog = g - b_col + li_col
    m_new = jnp.maximum(g + m11, jnp.max(wlog, axis=0, keepdims=True))
    a_s = jnp.exp(g + m11 - m_new)
    w = jnp.exp(wlog - m_new)
    return dict(f, e=e, w_mat=w_mat, a=a, qf=qf, nq=nq, m_row=m_row, m_new=m_new, a_s=a_s, w=w,
                wv=_nn(w_mat.astype(BF), vh))


def _mlstm_out(f):
    num = f["a"] * f["cq"] + f["wv"]
    den = f["a"] * f["nq"] + jnp.sum(f["w_mat"], axis=1, keepdims=True)
    floor = jnp.exp(-f["m_row"])
    r = jnp.maximum(jnp.abs(den), floor)
    return dict(f, den=den, floor=floor, r=r, hm=num / r)


def _mlstm_fwd(qk, pbig, small, gbias, headg, name):
    tp = qk.shape[0]
    nb = tp // TM

    def body(qk_ref, v_ref, mo_ref, sm_ref, gb_ref, hg_ref, y_ref, cs_ref, ns_ref, c_scr, n_scr):
        blk = pl.program_id(0)

        @pl.when(blk == 0)
        def _():
            c_scr[...] = jnp.zeros_like(c_scr)
            n_scr[...] = jnp.zeros_like(n_scr)

        k = _chunk_consts()
        io8 = lax.broadcasted_iota(jnp.int32, (8, DQK), 0)

        def chunk(c, carry):
            r0 = pl.multiple_of(c * CHUNK, CHUNK)
            rows = pl.ds(r0, CHUNK)
            valid = _valid_rows(blk, c)
            _, li_all, b_all = _mlstm_gates(sm_ref[rows, :], gb_ref[...], valid, k)
            heads = range(NH)
            qs = [qk_ref[rows, h * DQK:(h + 1) * DQK] for h in heads]
            ks = [qk_ref[rows, NH * DQK + h * DQK:NH * DQK + (h + 1) * DQK] for h in heads]
            vs = [v_ref[rows, h * DV:(h + 1) * DV] for h in heads]
            cst = [c_scr[h] for h in heads]
            nrow = [n_scr[h, 0:1, :] for h in heads]
            m11 = [jnp.max(n_scr[h, 1:2, :], axis=1, keepdims=True) for h in heads]
            f = [_mlstm_open(h, qs[h], ks[h], cst[h], li_all, b_all, k) for h in heads]
            f = [_mlstm_weights(h, f[h], qs[h], vs[h], li_all, b_all, nrow[h], m11[h], k) for h in heads]
            wk = [f[h]["w"] * ks[h].astype(F32) for h in heads]
            kv = [_tn(vs[h], wk[h].astype(BF)) for h in heads]
            for h in heads:
                hm = _mlstm_out(f[h])["hm"]
                gate = _sigmoid(mo_ref[rows, h * DV:(h + 1) * DV].astype(F32))
                y_ref[rows, h * DV:(h + 1) * DV] = _headnorm_fwd(hm, hg_ref[:, h * DV:(h + 1) * DV], gate).astype(BF)
                cs_ref[c, h] = f[h]["cb"]
                ns_ref[c, h] = jnp.where(io8 == 0, nrow[h], jnp.where(io8 == 1, m11[h], 0.0))
                c_scr[h] = f[h]["a_s"] * cst[h] + kv[h]
                n_scr[h, 0:1, :] = f[h]["a_s"] * nrow[h] + jnp.sum(wk[h], axis=0, keepdims=True)
                n_scr[h, 1:2, :] = jnp.broadcast_to(f[h]["m_new"], (1, DQK))
            return carry

        lax.fori_loop(0, CPB, chunk, 0, unroll=2)

    return pl.pallas_call(
        body, grid=(nb,),
        in_specs=[_row_spec(D), _row_spec(D, CB_MV), _row_spec(D, CB_MO), _row_spec(N_SMALL), _full_spec((1, N_SMALL)), _full_spec((1, D))],
        out_specs=[_row_spec(D), pl.BlockSpec((CPB, NH, DV, DQK), lambda i: (i, 0, 0, 0)),
                   pl.BlockSpec((CPB, NH, 8, DQK), lambda i: (i, 0, 0, 0))],
        out_shape=(jax.ShapeDtypeStruct((tp, D), BF), jax.ShapeDtypeStruct((tp // CHUNK, NH, DV, DQK), BF),
                   jax.ShapeDtypeStruct((tp // CHUNK, NH, 8, DQK), F32)),
        scratch_shapes=[pltpu.VMEM((NH, DV, DQK), F32), pltpu.VMEM((NH, 8, DQK), F32)],
        compiler_params=_params(), name=name)(qk, pbig, pbig, small, gbias, headg)


def _mlstm_bwd(dy, qk, pbig, small, gbias, headg, cs, ns, dproj, name):
    tp = qk.shape[0]
    nb = tp // TM

    def body(dy_ref, qk_ref, v_ref, mo_ref, sm_ref, gb_ref, hg_ref, cs_ref, ns_ref, _,
             dqk_ref, dproj_ref, dsm_ref, dgb_ref, dhg_ref, dc_scr, dn_scr):
        step = pl.program_id(0)
        blk = nb - 1 - step

        @pl.when(step == 0)
        def _():
            dc_scr[...] = jnp.zeros_like(dc_scr)
            dn_scr[...] = jnp.zeros_like(dn_scr)
            dgb_ref[...] = jnp.zeros_like(dgb_ref)
            dhg_ref[...] = jnp.zeros_like(dhg_ref)

        k = _chunk_consts()
        lane, rowio = k["lane"], k["rowio"]

        def chunk(cc, carry):
            c = CPB - 1 - cc
            r0 = pl.multiple_of(c * CHUNK, CHUNK)
            rows = pl.ds(r0, CHUNK)
            valid = _valid_rows(blk, c)
            pre, li_all, b_all = _mlstm_gates(sm_ref[rows, :], gb_ref[...], valid, k)
            dli_all = jnp.zeros((CHUNK, N_SMALL), F32)
            db_all = jnp.zeros((CHUNK, N_SMALL), F32)
            heads = range(NH)
            qs = [qk_ref[rows, h * DQK:(h + 1) * DQK] for h in heads]
            ks = [qk_ref[rows, NH * DQK + h * DQK:NH * DQK + (h + 1) * DQK] for h in heads]
            vs = [v_ref[rows, h * DV:(h + 1) * DV] for h in heads]
            cst = [cs_ref[c, h].astype(F32) for h in heads]
            nrow = [ns_ref[c, h, 0:1, :] for h in heads]
            m11 = [jnp.max(ns_ref[c, h, 1:2, :], axis=1, keepdims=True) for h in heads]
            f = [_mlstm_open(h, qs[h], ks[h], cst[h], li_all, b_all, k) for h in heads]
            f = [_mlstm_weights(h, f[h], qs[h], vs[h], li_all, b_all, nrow[h], m11[h], k) for h in heads]
            f = [_mlstm_out(f[h]) for h in heads]
            t = []
            for h in heads:
                gain = hg_ref[:, h * DV:(h + 1) * DV]
                gate = _sigmoid(mo_ref[rows, h * DV:(h + 1) * DV].astype(F32))
                dhm, dgate, dgain = _headnorm_bwd(dy_ref[rows, h * DV:(h + 1) * DV].astype(F32), f[h]["hm"], gain, gate)
                dproj_ref[rows, D + h * DV:D + (h + 1) * DV] = (dgate * gate * (1.0 - gate)).astype(BF)
                dhg_ref[:, h * DV:(h + 1) * DV] += dgain
                r, den = f[h]["r"], f[h]["den"]
                dnum = dhm / r
                dr = -jnp.sum(dhm * f[h]["hm"], axis=1, keepdims=True) / r
                dden = jnp.where(jnp.abs(den) > f[h]["floor"], dr * jnp.sign(den), 0.0)
                dnb = dnum.astype(BF)
                dc_new = dc_scr[h]
                dcb = dc_new.astype(BF)
                t.append(dict(dnum=dnum, dden=dden, dnb=dnb, dc_new=dc_new, dn_new=dn_scr[h],
                              dwm=_nt(dnb, vs[h]), vdc=_nn(vs[h], dcb), kdc=_nt(ks[h], dcb)))
            for h in heads:
                dw_mat = t[h]["dwm"] + t[h]["dden"]
                dsim = (f[h]["e"] * dw_mat).astype(BF)
                gm = f[h]["w_mat"] * dw_mat
                t[h].update(gm=gm, dv0=_tn(f[h]["w_mat"].astype(BF), t[h]["dnb"]), dq0=_nn(dsim, ks[h]),
                            dq1=_nn(t[h]["dnb"], f[h]["cb"]), dk0=_tn(dsim, qs[h]),
                            dcq=_tn((f[h]["a"] * t[h]["dnum"]).astype(BF), qs[h]), cs2=_tn(gm, k["ones"], precision=HI))
            for h in heads:
                a, w, a_s = f[h]["a"], f[h]["w"], f[h]["a_s"]
                dnum, dden, dc_new, dn_new, vdc, gm = (t[h][n] for n in ("dnum", "dden", "dc_new", "dn_new", "vdc", "gm"))
                kf = ks[h].astype(F32)
                dproj_ref[rows, h * DV:(h + 1) * DV] = (t[h]["dv0"] + w * t[h]["kdc"]).astype(BF)
                adden = a * dden
                dqk_ref[rows, h * DQK:(h + 1) * DQK] = t[h]["dq0"] + a * t[h]["dq1"] + adden * nrow[h]
                dqk_ref[rows, NH * DQK + h * DQK:NH * DQK + (h + 1) * DQK] = t[h]["dk0"] + w * vdc + w * dn_new
                da = jnp.sum(dnum * f[h]["cq"], axis=1, keepdims=True) + dden * f[h]["nq"]
                dw = jnp.sum(vdc * kf, axis=1, keepdims=True) + jnp.sum(kf * dn_new, axis=1, keepdims=True)
                da_s = _sum_all(dc_new * cst[h]) + jnp.sum(dn_new * nrow[h], axis=1, keepdims=True)
                wdw = w * dw
                rs = jnp.sum(gm, axis=1, keepdims=True)
                cs_col = _col(t[h]["cs2"], lane, 0)
                dg = a_s * da_s + jnp.sum(wdw, axis=0, keepdims=True)
                db = a * da + rs - cs_col - wdw + jnp.where(rowio == CHUNK - 1, dg, 0.0)
                dli_all = dli_all + jnp.where(lane == h, cs_col + wdw, 0.0)
                db_all = db_all + jnp.where(lane == NH + h, db, 0.0)
                dc_scr[h] = a_s * dc_new + t[h]["dcq"]
                dn_scr[h] = a_s * dn_new + jnp.sum(adden * f[h]["qf"], axis=0, keepdims=True)
            dlf_all = _nn(k["triu_f"], db_all, precision=HI)
            dsm = jnp.where(valid, dli_all + dlf_all * _sigmoid(-pre), 0.0)
            dsm = jnp.where(lane < 2 * NH, dsm, 0.0)
            dsm_ref[rows, :] = dsm
            dgb_ref[0:1, :] += jnp.sum(dsm, axis=0, keepdims=True)
            return carry

        lax.fori_loop(0, CPB, chunk, 0, unroll=2)

    rev = lambda col: (lambda i: (nb - 1 - i, col))
    rspec = lambda width, col=0: pl.BlockSpec((TM, width), rev(col))
    return pl.pallas_call(
        body, grid=(nb,),
        in_specs=[rspec(D), rspec(D), rspec(D, CB_MV), rspec(D, CB_MO), rspec(N_SMALL), _full_spec((1, N_SMALL)), _full_spec((1, D)),
                  pl.BlockSpec((CPB, NH, DV, DQK), lambda i: (nb - 1 - i, 0, 0, 0)),
                  pl.BlockSpec((CPB, NH, 8, DQK), lambda i: (nb - 1 - i, 0, 0, 0)), ANY],
        out_specs=[rspec(D), rspec(2 * D, CB_MV // 2), rspec(N_SMALL), _full_spec((8, N_SMALL)), _full_spec((1, D))],
        out_shape=(jax.ShapeDtypeStruct((tp, D), F32), jax.ShapeDtypeStruct(dproj.shape, BF),
                   jax.ShapeDtypeStruct((tp, N_SMALL), F32), jax.ShapeDtypeStruct((8, N_SMALL), F32),
                   jax.ShapeDtypeStruct((1, D), F32)),
        scratch_shapes=[pltpu.VMEM((NH, DV, DQK), F32), pltpu.VMEM((NH, 1, DQK), F32)],
        input_output_aliases={9: 1}, compiler_params=_params(), name=name)(dy, qk, pbig, pbig, small, gbias, headg, cs, ns, dproj)


def _gla_loga(sm_ref, a2_ref, a2b_ref, blk):
    za = _nn(sm_ref[...].astype(BF), a2_ref[...]) + a2b_ref[...]
    row = blk * TM + lax.broadcasted_iota(jnp.int32, (TM, 1), 0)
    return za, jnp.where(row >= FIRST_VALID, _logsig(za) / G_TAU, 0.0)


def _gla_head(h, q_ref, k_ref, rows, bc, btot, k):
    sl = slice(h * DQK, (h + 1) * DQK)
    bch = bc[:, sl]
    bth = btot[:, sl]
    gq = q_ref[rows, h * DQK:(h + 1) * DQK].astype(F32)
    gk = k_ref[rows, NH * DQK + h * DQK:NH * DQK + (h + 1) * DQK].astype(F32)
    e_pos = jnp.exp(bch) * (DQK ** -0.5)
    e_neg = jnp.exp(-bch)
    e_end = jnp.exp(bth - bch)
    qd = gq * e_pos
    ki = gk * e_neg
    ke = gk * e_end
    att = jnp.where(k["tri"], _nt(qd.astype(BF), ki.astype(BF)), 0.0)
    return dict(e_pos=e_pos, e_neg=e_neg, e_end=e_end, qd=qd, ki=ki, ke=ke, att=att, decay=jnp.exp(bth))


def _gla_fwd(pbig, small, a2p, a2b, headg, name):
    tp = pbig.shape[0]
    nb = tp // TM

    def body(qk_ref, v_ref, gr_ref, sm_ref, a2_ref, a2b_ref, hg_ref, y_ref, ss_ref, s_scr, lg_scr):
        blk = pl.program_id(0)

        @pl.when(blk == 0)
        def _():
            s_scr[...] = jnp.zeros_like(s_scr)

        k = _chunk_consts()
        _, loga = _gla_loga(sm_ref, a2_ref, a2b_ref, blk)
        lg_scr[...] = loga

        def chunk(c, carry):
            r0 = pl.multiple_of(c * CHUNK, CHUNK)
            rows = pl.ds(r0, CHUNK)
            bc = _nn(k["tril_f"], lg_scr[rows, :], precision=HI)
            btot = _last_row(bc, k["rowio"])
            heads = range(NH)
            f = [_gla_head(h, qk_ref, qk_ref, rows, bc, btot, k) for h in heads]
            vs = [v_ref[rows, h * DV:(h + 1) * DV] for h in heads]
            sst = [s_scr[h] for h in heads]
            sbs = [s.astype(BF) for s in sst]
            inter = [_nt(f[h]["qd"].astype(BF), sbs[h]) for h in heads]
            intra = [_nn(f[h]["att"].astype(BF), vs[h]) for h in heads]
            kv = [_tn(vs[h], f[h]["ke"].astype(BF)) for h in heads]
            for h in heads:
                gr = gr_ref[rows, h * DV:(h + 1) * DV].astype(F32)
                y_ref[rows, h * DV:(h + 1) * DV] = _headnorm_fwd(intra[h] + inter[h], hg_ref[:, h * DV:(h + 1) * DV],
                                                                   gr * _sigmoid(gr)).astype(BF)
                ss_ref[c, h] = sbs[h]
                s_scr[h] = sst[h] * f[h]["decay"] + kv[h]
            return carry

        lax.fori_loop(0, CPB, chunk, 0, unroll=2)

    return pl.pallas_call(
        body, grid=(nb,),
        in_specs=[_row_spec(D, CB_GQK), _row_spec(D, CB_GV), _row_spec(D, CB_GR), _row_spec(N_SMALL),
                  _full_spec((N_SMALL, NH * DQK)), _full_spec((1, NH * DQK)), _full_spec((1, D))],
        out_specs=[_row_spec(D), pl.BlockSpec((CPB, NH, DV, DQK), lambda i: (i, 0, 0, 0))],
        out_shape=(jax.ShapeDtypeStruct((tp, D), BF), jax.ShapeDtypeStruct((tp // CHUNK, NH, DV, DQK), BF)),
        scratch_shapes=[pltpu.VMEM((NH, DV, DQK), F32), pltpu.VMEM((TM, NH * DQK), F32)],
        compiler_params=_params(), name=name)(pbig, pbig, pbig, small, a2p, a2b, headg)


def _gla_bwd(dy, pbig, small, a2p, a2b, headg, ss, dsm_m, dproj, name):
    tp = pbig.shape[0]
    nb = tp // TM
    nqk = NH * DQK

    def body(dy_ref, qk_ref, v_ref, gr_ref, sm_ref, a2_ref, a2b_ref, hg_ref, ss_ref, dsmm_ref, _,
             dproj_ref, dsm_ref, da2_ref, da2b_ref, dhg_ref, ds_scr, lg_scr, dza_scr):
        step = pl.program_id(0)
        blk = nb - 1 - step

        @pl.when(step == 0)
        def _():
            ds_scr[...] = jnp.zeros_like(ds_scr)
            da2_ref[...] = jnp.zeros_like(da2_ref)
            da2b_ref[...] = jnp.zeros_like(da2b_ref)
            dhg_ref[...] = jnp.zeros_like(dhg_ref)

        k = _chunk_consts()
        rowio = k["rowio"]
        za, loga = _gla_loga(sm_ref, a2_ref, a2b_ref, blk)
        lg_scr[...] = loga

        def chunk(cc, carry):
            c = CPB - 1 - cc
            r0 = pl.multiple_of(c * CHUNK, CHUNK)
            rows = pl.ds(r0, CHUNK)
            bc = _nn(k["tril_f"], lg_scr[rows, :], precision=HI)
            btot = _last_row(bc, rowio)
            heads = range(NH)
            f = [_gla_head(h, qk_ref, qk_ref, rows, bc, btot, k) for h in heads]
            vs = [v_ref[rows, h * DV:(h + 1) * DV] for h in heads]
            sbs = [ss_ref[c, h] for h in heads]
            qdb = [f[h]["qd"].astype(BF) for h in heads]
            attb = [f[h]["att"].astype(BF) for h in heads]
            inter = [_nt(qdb[h], sbs[h]) for h in heads]
            intra = [_nn(attb[h], vs[h]) for h in heads]
            dsn = [ds_scr[h] for h in heads]
            dsb = [d.astype(BF) for d in dsn]
            dke = [_nn(vs[h], dsb[h]) for h in heads]
            dv1 = [_nt(f[h]["ke"].astype(BF), dsb[h]) for h in heads]
            t = []
            for h in heads:
                gr = gr_ref[rows, h * DV:(h + 1) * DV].astype(F32)
                sg = _sigmoid(gr)
                gain = hg_ref[:, h * DV:(h + 1) * DV]
                do, dact, dgain = _headnorm_bwd(dy_ref[rows, h * DV:(h + 1) * DV].astype(F32), intra[h] + inter[h], gain, gr * sg)
                dproj_ref[rows, 2 * D + h * DV:2 * D + (h + 1) * DV] = (dact * sg * (1.0 + gr * (1.0 - sg))).astype(BF)
                dhg_ref[:, h * DV:(h + 1) * DV] += dgain
                dob = do.astype(BF)
                t.append(dict(dob=dob, datt=_nt(dob, vs[h]), dv0=_tn(attb[h], dob), dq1=_nn(dob, sbs[h]), dsq=_tn(dob, qdb[h])))
            for h in heads:
                datt = jnp.where(k["tri"], t[h]["datt"], 0.0).astype(BF)
                t[h].update(dq0=_nn(datt, f[h]["ki"].astype(BF)), dki=_tn(datt, qdb[h]))
            dbc_parts = []
            for h in heads:
                dqd = t[h]["dq0"] + t[h]["dq1"]
                dki = t[h]["dki"]
                dproj_ref[rows, D + h * DV:D + (h + 1) * DV] = (t[h]["dv0"] + dv1[h]).astype(BF)
                dproj_ref[rows, h * DQK:(h + 1) * DQK] = (dqd * f[h]["e_pos"]).astype(BF)
                dproj_ref[rows, nqk + h * DQK:nqk + (h + 1) * DQK] = (dki * f[h]["e_neg"] + dke[h] * f[h]["e_end"]).astype(BF)
                dke_ke = dke[h] * f[h]["ke"]
                dbtot = (jnp.sum(dke_ke, axis=0, keepdims=True)
                         + jnp.sum(dsn[h] * sbs[h].astype(F32), axis=0, keepdims=True) * f[h]["decay"])
                dbc_parts.append(dqd * f[h]["qd"] - dki * f[h]["ki"] - dke_ke + jnp.where(rowio == CHUNK - 1, dbtot, 0.0))
                ds_scr[h] = dsn[h] * f[h]["decay"] + t[h]["dsq"]
            dbc = jnp.concatenate(dbc_parts, axis=1)
            dza_scr[rows, :] = _nn(k["triu_f"], dbc, precision=HI)
            return carry

        lax.fori_loop(0, CPB, chunk, 0, unroll=2)
        row = blk * TM + lax.broadcasted_iota(jnp.int32, (TM, 1), 0)
        dza = jnp.where(row >= FIRST_VALID, dza_scr[...] * (_sigmoid(-za) / G_TAU), 0.0)
        dzb = dza.astype(BF)
        dsm_ref[...] = (_nt(dzb, a2_ref[...]) + dsmm_ref[...]).astype(BF)
        da2_ref[...] += _tn(sm_ref[...].astype(BF), dzb)
        da2b_ref[...] += jnp.sum(dza, axis=0, keepdims=True)

    rspec = lambda width, col=0: pl.BlockSpec((TM, width), lambda i: (nb - 1 - i, col))
    return pl.pallas_call(
        body, grid=(nb,),
        in_specs=[rspec(D), rspec(D, CB_GQK), rspec(D, CB_GV), rspec(D, CB_GR), rspec(N_SMALL),
                  _full_spec((N_SMALL, nqk)), _full_spec((1, nqk)), _full_spec((1, D)),
                  pl.BlockSpec((CPB, NH, DV, DQK), lambda i: (nb - 1 - i, 0, 0, 0)), rspec(N_SMALL), ANY],
        out_specs=[rspec(3 * D, 0), rspec(N_SMALL), _full_spec((N_SMALL, nqk)), _full_spec((1, nqk)), _full_spec((1, D))],
        out_shape=(jax.ShapeDtypeStruct(dproj.shape, BF),
                   jax.ShapeDtypeStruct((tp, N_SMALL), BF), jax.ShapeDtypeStruct((N_SMALL, nqk), F32),
                   jax.ShapeDtypeStruct((1, nqk), F32), jax.ShapeDtypeStruct((1, D), F32)),
        scratch_shapes=[pltpu.VMEM((NH, DV, DQK), F32), pltpu.VMEM((TM, nqk), F32), pltpu.VMEM((TM, nqk), F32)],
        input_output_aliases={10: 0}, compiler_params=_params(), name=name)(dy, pbig, pbig, pbig, small, a2p, a2b, headg, ss, dsm_m, dproj)


PIECE_BYTES = 1 << 20
MAX_PIECES = 32


def _place():
    return lax.axis_index("x"), lax.axis_index("y"), lax.axis_index("c")


def _piece_rows(rows, row_bytes, align):
    want = min(MAX_PIECES, max(1, -(-rows * row_bytes // PIECE_BYTES)))
    best = rows
    for k in range(1, want + 1):
        if rows % k == 0 and (rows // k) % align == 0:
            best = rows // k
    return best


def _remote(src, dst, send_sems, recv_sems, k, to):
    return pltpu.make_async_remote_copy(src_ref=src, dst_ref=dst, send_sem=send_sems.at[k], recv_sem=recv_sems.at[k],
                                        device_id=to, device_id_type=MESH)


def _all_gather_chips(p, name):
    r, n = p.shape
    rh = r // 2
    align = 32 // p.dtype.itemsize
    assert r % (2 * align) == 0
    cr = _piece_rows(rh, n * p.dtype.itemsize, align)

    def body(p_ref, o_ref, send_sems, recv_sems):
        x, y, c = _place()
        chips = [(1 - x, y), (x, 1 - y), (1 - x, 1 - y)]
        sib = (x, y, 1 - c)

        def half(hc, piece=None):
            if piece is None:
                return pl.ds(pl.multiple_of(hc * rh, align), rh)
            return pl.ds(pl.multiple_of(hc * rh + piece * cr, align), cr)

        first = [_remote(p_ref.at[half(c)], o_ref.at[j, half(c)], send_sems, recv_sems, j, (*chip, c))
                 for j, chip in enumerate(chips)]
        for cp in first:
            cp.start()
        for j, cp in enumerate(first):
            cp.wait_recv()
            for i in range(rh // cr):
                _remote(o_ref.at[j, half(c, i)], o_ref.at[j, half(c, i)], send_sems, recv_sems, 3 + j, sib).start()
        for j in range(3):
            block = _remote(o_ref.at[j, half(c)], o_ref.at[j, half(1 - c)], send_sems, recv_sems, 3 + j, sib)
            block.wait_send()
            block.wait_recv()
        for cp in first:
            cp.wait_send()

    return pl.pallas_call(
        body, in_specs=[ANY], out_specs=ANY, out_shape=jax.ShapeDtypeStruct((3, r, n), p.dtype),
        scratch_shapes=[pltpu.SemaphoreType.DMA((6,)), pltpu.SemaphoreType.DMA((6,))],
        name=name)(p)


def _by_chip(mine, others):
    me = 2 * lax.axis_index("x") + lax.axis_index("y")
    by_mask = jnp.stack([mine, others[1], others[0], others[2]])
    return [lax.dynamic_index_in_dim(by_mask, q ^ me, 0, keepdims=False) for q in range(4)]


def _swap_halves(items, name):
    k = len(items)

    def body(*refs):
        a_refs, got_refs = refs[:k], refs[k:2 * k]
        send_sems, recv_sems = refs[2 * k:]
        x, y, c = _place()
        sib = (x, y, 1 - c)
        for i, a in enumerate(items):
            _, r, n = a.shape
            rh = r // 2
            cr = _piece_rows(rh, n * a.dtype.itemsize, 8)
            for q in range(4):
                for t in range(rh // cr):
                    other = pl.ds(pl.multiple_of((1 - c) * rh + t * cr, 8), cr)
                    _remote(a_refs[i].at[q, other], got_refs[i].at[q, pl.ds(t * cr, cr)], send_sems, recv_sems, i, sib).start()
        for i, a in enumerate(items):
            block = _remote(a_refs[i].at[:, pl.ds(0, a.shape[1] // 2)], got_refs[i], send_sems, recv_sems, i, sib)
            block.wait_send()
            block.wait_recv()

    return pl.pallas_call(
        body, in_specs=[ANY] * k, out_specs=[ANY] * k,
        out_shape=tuple(jax.ShapeDtypeStruct((4, a.shape[1] // 2, a.shape[2]), a.dtype) for a in items),
        scratch_shapes=[pltpu.SemaphoreType.DMA((k,)), pltpu.SemaphoreType.DMA((k,))], name=name)(*items)


def _scatter_chips(items, name):
    k = len(items)

    def body(*refs):
        s_refs, o_refs = refs[:k], refs[k:2 * k]
        send_sems, recv_sems = refs[2 * k:]
        x, y, c = _place()
        chips = [(1 - x, y), (x, 1 - y), (1 - x, 1 - y)]
        sent = []
        for i in range(k):
            for j, (cx, cy) in enumerate(chips):
                cp = _remote(s_refs[i].at[2 * cx + cy], o_refs[i].at[j], send_sems, recv_sems, 3 * i + j, (cx, cy, c))
                cp.start()
                sent.append(cp)
        for cp in sent:
            cp.wait_recv()
        for cp in sent:
            cp.wait_send()

    return pl.pallas_call(
        body, in_specs=[ANY] * k, out_specs=[ANY] * k,
        out_shape=tuple(jax.ShapeDtypeStruct((3,) + s.shape[1:], s.dtype) for s in items),
        scratch_shapes=[pltpu.SemaphoreType.DMA((3 * k,)), pltpu.SemaphoreType.DMA((3 * k,))], name=name)(*items)


def _join_halves(items, name):
    k = len(items)

    def body(*refs):
        f_refs, o_refs = refs[:k], refs[k:2 * k]
        send_sems, recv_sems = refs[2 * k:]
        x, y, c = _place()
        sib = (x, y, 1 - c)
        for i, f in enumerate(items):
            rh, n = f.shape
            cr = _piece_rows(rh, n * f.dtype.itemsize, 8)
            for t in range(rh // cr):
                rows = pl.ds(t * cr, cr)
                _remote(f_refs[i].at[rows], o_refs[i].at[rows], send_sems, recv_sems, i, sib).start()
        for i in range(k):
            block = _remote(f_refs[i], o_refs[i], send_sems, recv_sems, i, sib)
            block.wait_send()
            block.wait_recv()

    return pl.pallas_call(
        body, in_specs=[ANY] * k, out_specs=[ANY] * k, out_shape=tuple(jax.ShapeDtypeStruct(f.shape, f.dtype) for f in items),
        scratch_shapes=[pltpu.SemaphoreType.DMA((k,)), pltpu.SemaphoreType.DMA((k,))], name=name)(*items)


SMALL_ROWS = 16
SMALL_SHARD_SHAPES = [(N_META, 256), (4, 256), (G_RANK, 128), (NH, 64), (NH, 64)]
REPL_SHAPES = [(1, D), (1, D), (1, 2, NH), (1, NH * DQK), (1, D), (D,)]
W_IN_SHARD = 2054


def _pack_small(parts):
    flat = jnp.concatenate([p.reshape(-1) for p in parts])
    return jnp.pad(flat, (0, SMALL_ROWS * D - flat.shape[0])).reshape(SMALL_ROWS, D)


def _unpack_small(block, shapes):
    flat, out, off = block.reshape(-1), [], 0
    for shp in shapes:
        n = 1
        for s in shp:
            n *= s
        out.append(flat[off:off + n].reshape(shp))
        off += n
    return out


def _proj_cols_from_w_in(w_in_f):
    w_big = jnp.concatenate([w_in_f[:, 3080:5128], w_in_f[:, 5144:6168], w_in_f[:, 0:1024], w_in_f[:, 6168:8216],
                             w_in_f[:, 1024:2048], w_in_f[:, 2056:3080]], axis=1)
    w_small = jnp.concatenate([w_in_f[:, 2048:2056], w_in_f[:, 5128:5144], jnp.zeros((D, N_SMALL - 24), w_in_f.dtype)], axis=1)
    return w_big, w_small


def _w_in_from_proj_cols(d_wall):
    big, small = d_wall[:, 0:N_BIG], d_wall[:, N_BIG:N_ALL]
    return jnp.concatenate([big[:, 3072:4096], big[:, 6144:7168], small[:, 0:8], big[:, 7168:8192], big[:, 0:2048],
                            small[:, 8:24], big[:, 2048:3072], big[:, 4096:6144]], axis=1)


def kernel(x, meta_tokens, norm1_g, w_in, conv_w, conv_b, m_gate_b, g_a2, g_a2_b, m_head_g, g_head_g, w_branch_m, w_branch_g, w_out, norm2_g, w_ff_gate, w_ff_up, w_ff_down, final_g, loss_target, m_meta_tokens, m_norm1_g, m_w_in, m_conv_w, m_conv_b, m_m_gate_b, m_g_a2, m_g_a2_b, m_m_head_g, m_g_head_g, m_w_branch_m, m_w_branch_g, m_w_out, m_norm2_g, m_w_ff_gate, m_w_ff_up, m_w_ff_down, m_final_g, v_meta_tokens, v_norm1_g, v_w_in, v_conv_w, v_conv_b, v_m_gate_b, v_g_a2, v_g_a2_b, v_m_head_g, v_g_head_g, v_w_branch_m, v_w_branch_g, v_w_out, v_norm2_g, v_w_ff_gate, v_w_ff_up, v_w_ff_down, v_final_g):
    w = _gather_weights(w_in, w_branch_m, w_branch_g, w_out, w_ff_gate, w_ff_up, w_ff_down, meta_tokens, conv_w, g_a2, m_head_g, g_head_g)
    loss_local, dh0, local = _local_step(x[0], loss_target[0], w, norm1_g, conv_b, m_gate_b, g_a2_b, norm2_g, final_g)
    grads = _reduce_grads(local)

    weights = [w_in, w_branch_m, w_branch_g, w_out, w_ff_gate, w_ff_up, w_ff_down, meta_tokens, conv_w, g_a2, m_head_g, g_head_g,
               norm1_g, conv_b, m_gate_b, g_a2_b, norm2_g, final_g]
    moms = [m_w_in, m_w_branch_m, m_w_branch_g, m_w_out, m_w_ff_gate, m_w_ff_up, m_w_ff_down, m_meta_tokens, m_conv_w, m_g_a2,
            m_m_head_g, m_g_head_g, m_norm1_g, m_conv_b, m_m_gate_b, m_g_a2_b, m_norm2_g, m_final_g]
    vels = [v_w_in, v_w_branch_m, v_w_branch_g, v_w_out, v_w_ff_gate, v_w_ff_up, v_w_ff_down, v_meta_tokens, v_conv_w, v_g_a2,
            v_m_head_g, v_g_head_g, v_norm1_g, v_conv_b, v_m_gate_b, v_g_a2_b, v_norm2_g, v_final_g]
    res = {}
    for nm, wt, g, m, v in zip(PACK_ORDER, weights, grads, moms, vels):
        two_d = (wt.size // wt.shape[-1], wt.shape[-1])
        d, nm_, nv_ = _adamw(wt.reshape(two_d), g.reshape(two_d), m.reshape(two_d), v.reshape(two_d), "adamw_" + nm)
        res[nm] = (g.reshape(wt.shape), d.reshape(wt.shape), nm_.reshape(wt.shape), nv_.reshape(wt.shape))

    order = ["meta_tokens", "norm1_g", "w_in", "conv_w", "conv_b", "m_gate_b", "g_a2", "g_a2_b", "m_head_g", "g_head_g",
             "w_branch_m", "w_branch_g", "w_out", "norm2_g", "w_ff_gate", "w_ff_up", "w_ff_down", "final_g"]
    loss = lax.psum(loss_local[0, 0], ("x", "y", "c"))
    grad_x = dh0[TM:].reshape(x.shape)
    return (loss, grad_x, *[res[n][0] for n in order], *[res[n][1] for n in order],
            *[res[n][2] for n in order], *[res[n][3] for n in order])


PACK_ORDER = ["w_in", "w_branch_m", "w_branch_g", "w_out", "w_ff_gate", "w_ff_up", "w_ff_down", "meta_tokens", "conv_w", "g_a2",
              "m_head_g", "g_head_g", "norm1_g", "conv_b", "m_gate_b", "g_a2_b", "norm2_g", "final_g"]


def _gather_weights(w_in, w_branch_m, w_branch_g, w_out, w_ff_gate, w_ff_up, w_ff_down, meta_tokens, conv_w, g_a2, m_head_g, g_head_g):
    bf = lambda a: a.astype(BF)
    rows_local = jnp.concatenate([bf(w_branch_m[0]), bf(w_branch_g[0]), bf(w_out[0]), bf(w_ff_down[0]),
                                  bf(w_ff_gate[0].T), bf(w_ff_up[0].T)], axis=0)
    win_local = bf(w_in[0])
    small_local = _pack_small([meta_tokens, conv_w[0], g_a2[0], m_head_g[0], g_head_g[0]])
    rows_all = jnp.stack(_by_chip(rows_local, _all_gather_chips(rows_local, "gather_rows")))
    win_all = _by_chip(win_local, _all_gather_chips(win_local, "gather_w_in"))
    small_all = _by_chip(small_local, _all_gather_chips(small_local, "gather_small"))
    cut = lambda lo, hi: rows_all[:, lo:hi].reshape(4 * (hi - lo), D)
    wbm, wbg, wout, wdown = cut(0, 256), cut(256, 512), cut(512, 768), cut(768, 1472)
    wgu_t = jnp.concatenate([cut(1472, 2176), cut(2176, 2880)], axis=0)
    w_in_f = jnp.concatenate([win_all[q] for q in range(4)], axis=1)
    small_sh = [_unpack_small(small_all[q], SMALL_SHARD_SHAPES) for q in range(4)]
    cat = lambda i: jnp.concatenate([s[i] for s in small_sh], axis=-1)
    return dict(w_in=w_in_f, wbm=wbm, wbg=wbg, wout=wout, wgu_t=wgu_t, wdown=wdown, meta=cat(0), convw=cat(1), ga2=cat(2),
                mhg=cat(3).reshape(1, D), ghg=cat(4).reshape(1, D))


def _local_step(x0, target, w, norm1_g, conv_b, m_gate_b, g_a2_b, norm2_g, final_g):
    w_in_f, wbm, wbg, wout, wgu_t, wdown = w["w_in"], w["wbm"], w["wbg"], w["wout"], w["wgu_t"], w["wdown"]
    meta_f, convw_f, ga2_f, mhg_f, ghg_f = w["meta"], w["convw"], w["ga2"], w["mhg"], w["ghg"]
    w_big, w_small = _proj_cols_from_w_in(w_in_f)
    w_all = jnp.concatenate([w_big, w_small], axis=1)
    gbias = jnp.concatenate([m_gate_b.reshape(1, 2 * NH), jnp.zeros((1, N_SMALL - 2 * NH), F32)], axis=1)
    a2p = jnp.concatenate([jnp.zeros((8, NH * DQK), F32), ga2_f, jnp.zeros((N_SMALL - 24, NH * DQK), F32)], axis=0).astype(BF)
    convb = conv_b.reshape(1, D)
    g1 = norm1_g.reshape(1, D)
    g2 = norm2_g.reshape(1, D)
    gf = final_g.reshape(1, D)
    h0 = jnp.concatenate([jnp.zeros((FIRST_VALID, D), F32), meta_f, x0], axis=0)

    xn1, rstd1 = _rms_fwd(h0, g1, "rms1")
    pbig = _mm(xn1, w_big, nt=False, out_dtype=BF, tn=1024, name="proj_big")
    small = _mm(xn1, w_small, nt=False, out_dtype=F32, tn=N_SMALL, name="proj_small")
    qk = _conv_fwd(pbig, convw_f, convb, "conv_fwd")
    y_m, m_cs, m_ns = _mlstm_fwd(qk, pbig, small, gbias, mhg_f, "mlstm_fwd")
    y_g, g_ss = _gla_fwd(pbig, small, a2p, g_a2_b, ghg_f, "gla_fwd")
    p_m = _mm(y_m, wbm, nt=False, out_dtype=BF, tn=1024, name="branch_m")
    p_g = _mm(y_g, wbg, nt=False, out_dtype=BF, tn=1024, name="branch_g")
    merged = _merge_fwd(p_m, p_g, pbig, "merge_fwd")
    h1 = _mm(merged, wout, nt=False, out_dtype=F32, tn=1024, addend=h0, name="out_proj")
    hn, rstd2 = _rms_fwd(h1, g2, "rms2")
    gu = _mm(hn, wgu_t, nt=True, out_dtype=BF, tn=1408, name="ff_in")
    ff = _swiglu_fwd(gu, "swiglu_fwd")
    h2 = _mm(ff, wdown, nt=False, out_dtype=F32, tn=1024, addend=h1, name="ff_down")
    dh2, loss_local, d_final_g = _final_loss(h2, target, gf, "final_loss")

    dff = _mm(dh2, wdown, nt=True, out_dtype=BF, tn=1408, name="d_ff")
    d_wdown = _mm_tn(ff, dh2, tm=1408, tn=1024, name="dw_ff_down")
    dgu = _swiglu_bwd(dff, gu, "swiglu_bwd")
    dhn = _mm(dgu, wgu_t, nt=False, out_dtype=F32, tn=1024, tk=2816, name="d_hn")
    d_wgu_t = _mm_tn(dgu, hn, tm=1408, tn=1024, name="dw_ff_in")
    dh1, d_g2 = _rms_bwd(dhn, h1, rstd2, g2, dh2, "rms2_bwd")
    dmerged = _mm(dh1, wout, nt=True, out_dtype=BF, tn=1024, name="d_merged")
    d_wout = _mm_tn(merged, dh1, tm=1024, tn=1024, name="dw_out")
    dp_m, dp_g, dproj = _merge_bwd(dmerged, p_m, p_g, pbig, "merge_bwd")
    dy_m = _mm(dp_m, wbm, nt=True, out_dtype=BF, tn=1024, name="d_ym")
    dy_g = _mm(dp_g, wbg, nt=True, out_dtype=BF, tn=1024, name="d_yg")
    d_wbm = _mm_tn(y_m, dp_m, tm=1024, tn=1024, name="dw_branch_m")
    d_wbg = _mm_tn(y_g, dp_g, tm=1024, tn=1024, name="dw_branch_g")
    dqk_m, dproj, dsm_m, d_gbias, d_mhg = _mlstm_bwd(dy_m, qk, pbig, small, gbias, mhg_f, m_cs, m_ns, dproj, "mlstm_bwd")
    dconv, d_convwb = _conv_bwd_pre(dqk_m, pbig, convw_f, convb, "conv_bwd_pre")
    dproj = _conv_bwd_in(dconv, convw_f, dproj, "conv_bwd_in")
    dproj, dsmall, d_a2p, d_a2b, d_ghg = _gla_bwd(dy_g, pbig, small, a2p, g_a2_b, ghg_f, g_ss, dsm_m, dproj, "gla_bwd")
    dproj = _place_small(dsmall, dproj, "dproj_small")
    dxn = _mm(dproj, w_all, nt=True, out_dtype=F32, tn=1024, tk=1664, name="d_xn")
    d_wall = _mm_tn(xn1, dproj, tm=1024, tn=1664, name="dw_in")
    dh0, d_g1 = _rms_bwd(dxn, h0, rstd1, g1, dh1, "rms1_bwd")

    small_sharded = [dh0[FIRST_VALID:TM], d_convwb[0:4], d_a2p[8:24], d_mhg.reshape(NH, DV), d_ghg.reshape(NH, DV)]
    replicated = [d_g1, d_convwb[4:5], d_gbias[0:1, 0:2 * NH].reshape(1, 2, NH), d_a2b, d_g2, d_final_g.reshape(D)]
    local = dict(w_all=d_wall, wbm=d_wbm, wbg=d_wbg, wout=d_wout, wdown=d_wdown, wgu_t=d_wgu_t,
                 small_sharded=small_sharded, replicated=replicated)
    return loss_local, dh0, local


def _reduce_grads(local):
    d_win = _w_in_from_proj_cols(local["w_all"])
    win4 = jnp.stack([d_win[:, q * W_IN_SHARD:(q + 1) * W_IN_SHARD] for q in range(4)])
    small4 = jnp.stack([_pack_small([g[:, q * shp[1]:(q + 1) * shp[1]] for g, shp in zip(local["small_sharded"], SMALL_SHARD_SHAPES)]
                                    + local["replicated"]) for q in range(4)])
    gu = local["wgu_t"].reshape(2, 4, D_FF // 4, D)
    items = [win4, local["wbm"].reshape(4, 256, D), local["wbg"].reshape(4, 256, D), local["wout"].reshape(4, 256, D),
             local["wdown"].reshape(4, D_FF // 4, D), gu[0], gu[1], small4]
    c = lax.axis_index("c")
    me = 2 * lax.axis_index("x") + lax.axis_index("y")
    got = _swap_halves(items, "reduce_siblings")
    sums = []
    for i, (a, g) in enumerate(zip(items, got)):
        rh, n = g.shape[1], g.shape[2]
        own = lax.dynamic_slice_in_dim(a, c * rh, rh, axis=1)
        sums.append(_add2(own.reshape(-1, n), g.reshape(-1, n), F32 if i == len(items) - 1 else BF, f"reduce_add2_{i}").reshape(g.shape))
    from_chips = _scatter_chips(sums, "reduce_chips")
    halves = []
    for i, (s, f) in enumerate(zip(sums, from_chips)):
        mine = lax.dynamic_index_in_dim(s, me, 0, keepdims=False)
        if i == len(items) - 1:
            by_chip = _by_chip(mine, f)
            mine, f = by_chip[0], jnp.stack(by_chip[1:])
        halves.append(_add4(mine, f, f"reduce_add4_{i}"))
    got = _join_halves(halves, "reduce_join")
    full = [jnp.where(c == 0, jnp.concatenate([h, g], axis=0), jnp.concatenate([g, h], axis=0)) for h, g in zip(halves, got)]
    smalls = _unpack_small(full[7], SMALL_SHARD_SHAPES + REPL_SHAPES)
    return [full[0], full[1], full[2], full[3], full[5].T, full[6].T, full[4]] + smalls
```

```python
import functools

import jax
import jax.numpy as jnp
from jax import lax
from jax.experimental import pallas as pl
from jax.experimental.pallas import tpu as pltpu

F32 = jnp.float32
BF = jnp.bfloat16
HI = lax.Precision.HIGHEST
MESH = pl.DeviceIdType.MESH

D = 1024
N_META = 16
CHUNK = 128
EPS = 1e-6
NH = 4
DV = 256
DQK = 128
G_RANK = 16
G_TAU = 16.0
D_FF = 2816
TM = 512
FIRST_VALID = TM - N_META
CPB = TM // CHUNK
NEG = -1e30
N_BIG = 8192
CB_GQK, CB_GV, CB_GR, CB_MQK, CB_GM, CB_GG, CB_MV, CB_MO = range(8)
N_SMALL = 128
N_ALL = N_BIG + N_SMALL
VMEM_LIMIT = 56 * 1024 * 1024

ADAM_LR, ADAM_B1, ADAM_B2, ADAM_EPS, ADAM_WD, ADAM_STEP = 0.001, 0.9, 0.999, 1e-08, 0.01, 10

NT_DIMS = (((1,), (1,)), ((), ()))
TN_DIMS = (((0,), (0,)), ((), ()))


def _nt(a, b, **kw):
    return lax.dot_general(a, b, NT_DIMS, preferred_element_type=F32, **kw)


def _tn(a, b, **kw):
    return lax.dot_general(a, b, TN_DIMS, preferred_element_type=F32, **kw)


def _nn(a, b, **kw):
    return jnp.dot(a, b, preferred_element_type=F32, **kw)


def _params(**kw):
    return pltpu.CompilerParams(vmem_limit_bytes=VMEM_LIMIT, **kw)


def _sigmoid(x):
    return 1.0 / (1.0 + jnp.exp(-x))


def _logsig(x):
    return jnp.minimum(x, 0.0) - jnp.log(1.0 + jnp.exp(-jnp.abs(x)))


def _mm_rows(rows):
    return 3 * TM if rows % (3 * TM) == 0 else TM


def _mm(a, b, *, nt, out_dtype, tn, tk=None, tm=None, addend=None, name):
    m, k = a.shape
    n = b.shape[0] if nt else b.shape[1]
    tk = k if tk is None else tk
    tm = _mm_rows(m) if tm is None else tm
    nk = k // tk
    assert m % tm == 0 and n % tn == 0 and k % tk == 0
    dims = NT_DIMS if nt else (((1,), (0,)), ((), ()))

    def finish(r, add_ref, o_ref):
        if add_ref is not None:
            r = r + add_ref[...].astype(F32)
        o_ref[...] = r.astype(o_ref.dtype)

    def body(*refs):
        a_ref, b_ref = refs[:2]
        add_ref = refs[2] if addend is not None else None
        o_ref = refs[3] if addend is not None else refs[2]
        part = lax.dot_general(a_ref[...].astype(BF), b_ref[...].astype(BF), dims, preferred_element_type=F32)
        if nk == 1:
            finish(part, add_ref, o_ref)
            return
        acc_ref = refs[-1]
        kk = pl.program_id(2)

        @pl.when(kk == 0)
        def _():
            acc_ref[...] = part

        @pl.when(jnp.logical_and(kk > 0, kk < nk - 1))
        def _():
            acc_ref[...] += part

        @pl.when(kk == nk - 1)
        def _():
            finish(acc_ref[...] + part, add_ref, o_ref)

    in_specs = [pl.BlockSpec((tm, tk), lambda j, i, kk: (i, kk)),
                pl.BlockSpec((tn, tk), lambda j, i, kk: (j, kk)) if nt else pl.BlockSpec((tk, tn), lambda j, i, kk: (kk, j))]
    args = [a, b]
    if addend is not None:
        in_specs.append(pl.BlockSpec((tm, tn), lambda j, i, kk: (i, j)))
        args.append(addend)
    return pl.pallas_call(
        body, grid=(n // tn, m // tm, nk), in_specs=in_specs,
        out_specs=pl.BlockSpec((tm, tn), lambda j, i, kk: (i, j)),
        out_shape=jax.ShapeDtypeStruct((m, n), out_dtype),
        scratch_shapes=[pltpu.VMEM((tm, tn), F32)] if nk > 1 else [], compiler_params=_params(), name=name)(*args)


def _mm_tn(a, b, *, tm, tn, tk=None, name):
    t, m = a.shape
    n = b.shape[1]
    tk = _mm_rows(t) if tk is None else tk
    assert t % tk == 0 and m % tm == 0 and n % tn == 0

    def body(a_ref, b_ref, o_ref):
        part = _tn(a_ref[...].astype(BF), b_ref[...].astype(BF))

        @pl.when(pl.program_id(2) == 0)
        def _():
            o_ref[...] = part

        @pl.when(pl.program_id(2) > 0)
        def _():
            o_ref[...] += part

    return pl.pallas_call(
        body, grid=(m // tm, n // tn, t // tk),
        in_specs=[pl.BlockSpec((tk, tm), lambda i, j, kk: (kk, i)), pl.BlockSpec((tk, tn), lambda i, j, kk: (kk, j))],
        out_specs=pl.BlockSpec((tm, tn), lambda i, j, kk: (i, j)),
        out_shape=jax.ShapeDtypeStruct((m, n), F32), compiler_params=_params(), name=name)(a, b)


ANY = pl.BlockSpec(memory_space=pl.ANY)


def _row_spec(width, col=0):
    return pl.BlockSpec((TM, width), lambda i: (i, col))


def _full_spec(shape):
    return pl.BlockSpec(shape, lambda i: (0,) * len(shape))


def _rms_fwd(h, g, name):
    tp = h.shape[0]

    def body(h_ref, g_ref, xn_ref, r_ref):
        x = h_ref[...]
        r = lax.rsqrt(jnp.mean(x * x, axis=1, keepdims=True) + EPS)
        xn_ref[...] = (x * r * g_ref[...]).astype(BF)
        r_ref[...] = r

    return pl.pallas_call(
        body, grid=(tp // TM,), in_specs=[_row_spec(D), _full_spec((1, D))],
        out_specs=[_row_spec(D), _row_spec(1)],
        out_shape=(jax.ShapeDtypeStruct((tp, D), BF), jax.ShapeDtypeStruct((tp, 1), F32)),
        compiler_params=_params(), name=name)(h, g)


def _rms_bwd(dxn, h, rstd, g, dres, name, split_first=False):
    tp = h.shape[0]

    def body(dxn_ref, h_ref, r_ref, g_ref, dres_ref, *outs):
        r = r_ref[...]
        xh = h_ref[...] * r
        dxn_v = dxn_ref[...].astype(F32)
        dxh = dxn_v * g_ref[...]
        dh = r * (dxh - xh * jnp.mean(dxh * xh, axis=1, keepdims=True)) + dres_ref[...]
        if split_first:
            first_ref, dh_ref, dg_ref = outs

            @pl.when(pl.program_id(0) == 0)
            def _():
                first_ref[...] = dh
        else:
            dh_ref, dg_ref = outs
        dh_ref[...] = dh
        part = jnp.sum(dxn_v * xh, axis=0, keepdims=True)

        @pl.when(pl.program_id(0) == 0)
        def _():
            dg_ref[...] = part

        @pl.when(pl.program_id(0) > 0)
        def _():
            dg_ref[...] += part

    if split_first:
        out_specs = [_full_spec((TM, D)), pl.BlockSpec((TM, D), lambda i: (jnp.maximum(i - 1, 0), 0)), _full_spec((1, D))]
        out_shape = (jax.ShapeDtypeStruct((TM, D), F32), jax.ShapeDtypeStruct((tp - TM, D), F32), jax.ShapeDtypeStruct((1, D), F32))
    else:
        out_specs = [_row_spec(D), _full_spec((1, D))]
        out_shape = (jax.ShapeDtypeStruct((tp, D), F32), jax.ShapeDtypeStruct((1, D), F32))
    return pl.pallas_call(
        body, grid=(tp // TM,),
        in_specs=[_row_spec(D), _row_spec(D), _row_spec(1), _full_spec((1, D)), _row_spec(D)],
        out_specs=out_specs, out_shape=out_shape, compiler_params=_params(), name=name)(dxn, h, rstd, g, dres)


def _final_loss(h2, target, gf, name):
    tp = h2.shape[0]

    def body(h_ref, t_ref, g_ref, dh_ref, loss_ref, dg_ref):
        i = pl.program_id(0)
        live = (i > 0).astype(F32)
        x = h_ref[...]
        r = lax.rsqrt(jnp.mean(x * x, axis=1, keepdims=True) + EPS)
        xh = x * r
        e = xh * g_ref[...] - t_ref[...]
        row_loss = jnp.mean(e * e, axis=1, keepdims=True)
        loss_part = 0.5 * live * jnp.sum(row_loss, axis=0, keepdims=True)
        dout = e * (live / D)
        dg_part = jnp.sum(dout * xh, axis=0, keepdims=True)
        dxh = dout * g_ref[...]
        dh_ref[...] = r * (dxh - xh * jnp.mean(dxh * xh, axis=1, keepdims=True))

        @pl.when(i == 0)
        def _():
            loss_ref[...] = loss_part
            dg_ref[...] = dg_part

        @pl.when(i > 0)
        def _():
            loss_ref[...] += loss_part
            dg_ref[...] += dg_part

    return pl.pallas_call(
        body, grid=(tp // TM,),
        in_specs=[_row_spec(D), pl.BlockSpec((TM, D), lambda i: (jnp.maximum(i - 1, 0), 0)), _full_spec((1, D))],
        out_specs=[_row_spec(D), _full_spec((1, 1)), _full_spec((1, D))],
        out_shape=(jax.ShapeDtypeStruct((tp, D), F32), jax.ShapeDtypeStruct((1, 1), F32), jax.ShapeDtypeStruct((1, D), F32)),
        compiler_params=_params(), name=name)(h2, target, gf)


def _shift_down(x, halo, k):
    rk = pltpu.roll(x, k, 0)
    io = lax.broadcasted_iota(jnp.int32, (8, x.shape[1]), 0)
    top = jnp.where(io < k, pltpu.roll(halo, k, 0), rk[0:8])
    return jnp.concatenate([top, rk[8:]], axis=0)


def _shift_up(x, nxt, k):
    n = x.shape[0]
    rk = pltpu.roll(x, n - k, 0)
    io = lax.broadcasted_iota(jnp.int32, (8, x.shape[1]), 0)
    bot = jnp.where(io >= 8 - k, pltpu.roll(nxt, 8 - k, 0), rk[n - 8:n])
    return jnp.concatenate([rk[:n - 8], bot], axis=0)


def _conv_pre(x, halo, w_ref, b_ref):
    c = x * w_ref[3:4, :] + b_ref[...]
    shifted = []
    for k in (1, 2, 3):
        s = _shift_down(x, halo, k)
        shifted.append(s)
        c = c + s * w_ref[3 - k:4 - k, :]
    return c, shifted


def _qk_scale():
    col = lax.broadcasted_iota(jnp.int32, (1, D), 1)
    return jnp.where(col < NH * DQK, DQK ** -0.5, 1.0).astype(F32)


def _halo_prev_spec():
    return pl.BlockSpec((8, D), lambda i: (jnp.maximum(i * (TM // 8) - 1, 0), CB_MQK))


def _conv_fwd(pbig, w, b, name):
    tp = pbig.shape[0]

    def body(x_ref, halo_ref, w_ref, b_ref, o_ref):
        x = x_ref[...].astype(F32)
        halo = jnp.where(pl.program_id(0) > 0, halo_ref[...].astype(F32), 0.0)
        c, _ = _conv_pre(x, halo, w_ref, b_ref)
        o_ref[...] = (c * _sigmoid(c) * _qk_scale()).astype(BF)

    return pl.pallas_call(
        body, grid=(tp // TM,),
        in_specs=[_row_spec(D, CB_MQK), _halo_prev_spec(), _full_spec((4, D)), _full_spec((1, D))],
        out_specs=_row_spec(D), out_shape=jax.ShapeDtypeStruct((tp, D), BF),
        compiler_params=_params(), name=name)(pbig, pbig, w, b)


def _conv_bwd_pre(dqk, pbig, w, b, name):
    tp = pbig.shape[0]

    def body(d_ref, x_ref, halo_ref, w_ref, b_ref, dc_ref, dwb_ref):
        x = x_ref[...].astype(F32)
        halo = jnp.where(pl.program_id(0) > 0, halo_ref[...].astype(F32), 0.0)
        c, shifted = _conv_pre(x, halo, w_ref, b_ref)
        sg = _sigmoid(c)
        dc = d_ref[...] * _qk_scale() * (sg * (1.0 + c * (1.0 - sg)))
        dc_ref[...] = dc
        taps = [shifted[2], shifted[1], shifted[0], x]
        rows = [jnp.sum(dc * t, axis=0, keepdims=True) for t in taps] + [jnp.sum(dc, axis=0, keepdims=True)]
        io = lax.broadcasted_iota(jnp.int32, (8, D), 0)
        part = jnp.zeros((8, D), F32)
        for r, v in enumerate(rows):
            part = jnp.where(io == r, v, part)

        @pl.when(pl.program_id(0) == 0)
        def _():
            dwb_ref[...] = part

        @pl.when(pl.program_id(0) > 0)
        def _():
            dwb_ref[...] += part

    return pl.pallas_call(
        body, grid=(tp // TM,),
        in_specs=[_row_spec(D), _row_spec(D, CB_MQK), _halo_prev_spec(), _full_spec((4, D)), _full_spec((1, D))],
        out_specs=[_row_spec(D), _full_spec((8, D))],
        out_shape=(jax.ShapeDtypeStruct((tp, D), F32), jax.ShapeDtypeStruct((8, D), F32)),
        compiler_params=_params(), name=name)(dqk, pbig, pbig, w, b)


def _conv_bwd_in(dc, w, dproj, name):
    tp = dc.shape[0]
    nb = tp // TM

    def body(d_ref, nxt_ref, w_ref, _, o_ref):
        d = d_ref[...]
        nxt = jnp.where(pl.program_id(0) < nb - 1, nxt_ref[...], 0.0)
        acc = d * w_ref[3:4, :]
        for k in (1, 2, 3):
            acc = acc + _shift_up(d, nxt, k) * w_ref[3 - k:4 - k, :]
        o_ref[...] = acc.astype(BF)

    return pl.pallas_call(
        body, grid=(nb,),
        in_specs=[_row_spec(D), pl.BlockSpec((8, D), lambda i: (jnp.minimum((i + 1) * (TM // 8), tp // 8 - 1), 0)),
                  _full_spec((4, D)), ANY],
        out_specs=_row_spec(D, CB_MQK), out_shape=jax.ShapeDtypeStruct(dproj.shape, BF),
        input_output_aliases={3: 0}, compiler_params=_params(), name=name)(dc, dc, w, dproj)


def _merge_fwd(pm, pg, pbig, name):
    tp = pm.shape[0]

    def body(pm_ref, pg_ref, gm_ref, gg_ref, o_ref):
        o_ref[...] = (_sigmoid(gm_ref[...].astype(F32)) * pm_ref[...].astype(F32)
                      + _sigmoid(gg_ref[...].astype(F32)) * pg_ref[...].astype(F32)).astype(BF)

    return pl.pallas_call(
        body, grid=(tp // TM,), in_specs=[_row_spec(D), _row_spec(D), _row_spec(D, CB_GM), _row_spec(D, CB_GG)],
        out_specs=_row_spec(D), out_shape=jax.ShapeDtypeStruct((tp, D), BF),
        compiler_params=_params(), name=name)(pm, pg, pbig, pbig)


def _merge_bwd(dmerged, pm, pg, pbig, name):
    tp = pm.shape[0]

    def body(d_ref, pm_ref, pg_ref, gm_ref, gg_ref, dpm_ref, dpg_ref, dproj_ref):
        d = d_ref[...].astype(F32)
        sm = _sigmoid(gm_ref[...].astype(F32))
        sg = _sigmoid(gg_ref[...].astype(F32))
        dpm_ref[...] = (d * sm).astype(BF)
        dpg_ref[...] = (d * sg).astype(BF)
        dproj_ref[:, 0:D] = (d * pm_ref[...].astype(F32) * sm * (1.0 - sm)).astype(BF)
        dproj_ref[:, D:2 * D] = (d * pg_ref[...].astype(F32) * sg * (1.0 - sg)).astype(BF)

    shp = jax.ShapeDtypeStruct((tp, D), BF)
    return pl.pallas_call(
        body, grid=(tp // TM,), in_specs=[_row_spec(D), _row_spec(D), _row_spec(D), _row_spec(D, CB_GM), _row_spec(D, CB_GG)],
        out_specs=[_row_spec(D), _row_spec(D), _row_spec(2 * D, CB_GM // 2)],
        out_shape=(shp, shp, jax.ShapeDtypeStruct((tp, N_ALL), BF)),
        compiler_params=_params(), name=name)(dmerged, pm, pg, pbig, pbig)


def _swiglu_fwd(gu, name):
    tp = gu.shape[0]

    def body(g_ref, u_ref, o_ref):
        g = g_ref[...].astype(F32)
        o_ref[...] = (g * _sigmoid(g) * u_ref[...].astype(F32)).astype(BF)

    return pl.pallas_call(
        body, grid=(tp // TM,), in_specs=[_row_spec(D_FF, 0), _row_spec(D_FF, 1)],
        out_specs=_row_spec(D_FF), out_shape=jax.ShapeDtypeStruct((tp, D_FF), BF),
        compiler_params=_params(), name=name)(gu, gu)


def _swiglu_bwd(dff, gu, name):
    tp = gu.shape[0]

    def body(d_ref, g_ref, u_ref, o_ref):
        d = d_ref[...].astype(F32)
        g = g_ref[...].astype(F32)
        u = u_ref[...].astype(F32)
        sg = _sigmoid(g)
        o_ref[:, 0:D_FF] = (d * u * sg * (1.0 + g * (1.0 - sg))).astype(BF)
        o_ref[:, D_FF:2 * D_FF] = (d * g * sg).astype(BF)

    return pl.pallas_call(
        body, grid=(tp // TM,), in_specs=[_row_spec(D_FF), _row_spec(D_FF, 0), _row_spec(D_FF, 1)],
        out_specs=_row_spec(2 * D_FF), out_shape=jax.ShapeDtypeStruct((tp, 2 * D_FF), BF),
        compiler_params=_params(), name=name)(dff, gu, gu)


def _adamw(w, g, m, v, name):
    rows, cols = w.shape
    tr = 128 if rows % 128 == 0 else rows

    def body(w_ref, g_ref, m_ref, v_ref, d_ref, nm_ref, nv_ref):
        gv = g_ref[...]
        nm = ADAM_B1 * m_ref[...] + (1.0 - ADAM_B1) * gv
        nv = ADAM_B2 * v_ref[...] + (1.0 - ADAM_B2) * (gv * gv)
        m_hat = nm / (1.0 - ADAM_B1 ** ADAM_STEP)
        v_hat = nv / (1.0 - ADAM_B2 ** ADAM_STEP)
        d_ref[...] = -ADAM_LR * (m_hat / (jnp.sqrt(v_hat) + ADAM_EPS) + ADAM_WD * w_ref[...])
        nm_ref[...] = nm
        nv_ref[...] = nv

    spec = pl.BlockSpec((tr, cols), lambda i: (i, 0))
    shp = jax.ShapeDtypeStruct((rows, cols), F32)
    return pl.pallas_call(body, grid=(rows // tr,), in_specs=[spec] * 4, out_specs=[spec] * 3,
                          out_shape=(shp,) * 3, compiler_params=_params(), name=name)(w, g, m, v)


def _place_small(dsmall, dproj, name):
    tp = dsmall.shape[0]

    def body(s_ref, _, o_ref):
        o_ref[...] = s_ref[...]

    return pl.pallas_call(
        body, grid=(tp // TM,), in_specs=[_row_spec(N_SMALL), ANY], out_specs=_row_spec(N_SMALL, N_BIG // N_SMALL),
        out_shape=jax.ShapeDtypeStruct(dproj.shape, dproj.dtype), input_output_aliases={1: 0},
        compiler_params=_params(), name=name)(dsmall, dproj)


def _row_tile(rows, cap=512):
    best = rows
    for cand in range(8, min(rows, cap) + 1, 8):
        if rows % cand == 0:
            best = cand
    return best


def _add2(a, b, out_dtype, name):
    rows, cols = a.shape
    tr = _row_tile(rows)

    def body(a_ref, b_ref, o_ref):
        o_ref[...] = (a_ref[...] + b_ref[...]).astype(o_ref.dtype)

    spec = pl.BlockSpec((tr, cols), lambda i: (i, 0))
    return pl.pallas_call(body, grid=(rows // tr,), in_specs=[spec] * 2, out_specs=spec,
                          out_shape=jax.ShapeDtypeStruct((rows, cols), out_dtype), compiler_params=_params(), name=name)(a, b)


def _add4(first, rest, name):
    rows, cols = first.shape
    tr = _row_tile(rows, 256)

    def body(f_ref, r_ref, o_ref):
        up = lambda v: v.astype(F32)
        o_ref[...] = ((up(f_ref[...]) + up(r_ref[0])) + up(r_ref[1])) + up(r_ref[2])

    return pl.pallas_call(body, grid=(rows // tr,),
                          in_specs=[pl.BlockSpec((tr, cols), lambda i: (i, 0)), pl.BlockSpec((3, tr, cols), lambda i: (0, i, 0))],
                          out_specs=pl.BlockSpec((tr, cols), lambda i: (i, 0)),
                          out_shape=jax.ShapeDtypeStruct((rows, cols), F32), compiler_params=_params(), name=name)(first, rest)


def _chunk_consts():
    r2 = lax.broadcasted_iota(jnp.int32, (CHUNK, CHUNK), 0)
    c2 = lax.broadcasted_iota(jnp.int32, (CHUNK, CHUNK), 1)
    tri = r2 >= c2
    return dict(tri=tri, tril_f=tri.astype(F32), triu_f=(r2 <= c2).astype(F32),
                lane=lax.broadcasted_iota(jnp.int32, (CHUNK, N_SMALL), 1),
                rowio=lax.broadcasted_iota(jnp.int32, (CHUNK, 1), 0),
                ones=jnp.ones((CHUNK, N_SMALL), F32))


def _valid_rows(block, c):
    row = block * TM + c * CHUNK + lax.broadcasted_iota(jnp.int32, (CHUNK, 1), 0)
    return row >= FIRST_VALID


def _col(x, lane, idx):
    return jnp.sum(jnp.where(lane == idx, x, 0.0), axis=1, keepdims=True)


def _last_row(x, rowio):
    return jnp.sum(jnp.where(rowio == CHUNK - 1, x, 0.0), axis=0, keepdims=True)


def _sum_all(x):
    return jnp.sum(jnp.sum(x, axis=1, keepdims=True), axis=0, keepdims=True)


def _headnorm_fwd(hm, gain, gate_act):
    rs = lax.rsqrt(jnp.mean(hm * hm, axis=1, keepdims=True) + EPS)
    return hm * rs * gain * gate_act


def _headnorm_bwd(dy, hm, gain, gate_act):
    rs = lax.rsqrt(jnp.mean(hm * hm, axis=1, keepdims=True) + EPS)
    xh = hm * rs
    dact = dy * xh * gain
    dgain = jnp.sum(dy * gate_act * xh, axis=0, keepdims=True)
    dxh = dy * gate_act * gain
    dhm = rs * (dxh - xh * jnp.mean(dxh * xh, axis=1, keepdims=True))
    return dhm, dact, dgain


def _mlstm_gates(sm, gbias, valid, k):
    pre = sm + gbias
    lf = jnp.where(valid, _logsig(pre), 0.0)
    b_all = _nn(k["tril_f"], lf, precision=HI)
    li_all = jnp.where(valid, pre, NEG)
    return pre, li_all, b_all


def _mlstm_open(h, qh, kh, c_st, li_all, b_all, k):
    lane = k["lane"]
    sel = jnp.where(lane == h, 1.0, 0.0) - jnp.where(lane == NH + h, 1.0, 0.0)
    x = jnp.where(lane < NH, li_all, jnp.where(lane < 2 * NH, b_all, 0.0))
    cb = c_st.astype(BF)
    return dict(ubc=_nt(sel, x, precision=HI), sim=_nt(qh, kh), cb=cb, cq=_nt(qh, cb))


def _mlstm_weights(h, f, qh, vh, li_all, b_all, n_row, m11, k):
    lane, tri, rowio = k["lane"], k["tri"], k["rowio"]
    b_col = _col(b_all, lane, NH + h)
    li_col = _col(li_all, lane, h)
    dmat = jnp.where(tri, b_col + f["ubc"], NEG)
    m_row = jnp.maximum(b_col + m11, jnp.max(dmat, axis=1, keepdims=True))
    e = jnp.exp(dmat - m_row)
    w_mat = e * f["sim"]
    a = jnp.exp(b_col + m11 - m_row)
    qf = qh.astype(F32)
    nq = jnp.sum(qf * n_row, axis=1, keepdims=True)
    g = _last_row(b_col, rowio)
    wlog = g - b_col + li_col
    m_new = jnp.maximum(g + m11, jnp.max(wlog, axis=0, keepdims=True))
    a_s = jnp.exp(g + m11 - m_new)
    w = jnp.exp(wlog - m_new)
    return dict(f, e=e, w_mat=w_mat, a=a, qf=qf, nq=nq, m_row=m_row, m_new=m_new, a_s=a_s, w=w,
                wv=_nn(w_mat.astype(BF), vh))


def _mlstm_out(f):
    num = f["a"] * f["cq"] + f["wv"]
    den = f["a"] * f["nq"] + jnp.sum(f["w_mat"], axis=1, keepdims=True)
    floor = jnp.exp(-f["m_row"])
    r = jnp.maximum(jnp.abs(den), floor)
    return dict(f, den=den, floor=floor, r=r, hm=num / r)


def _mlstm_fwd(qk, pbig, small, gbias, headg, name):
    tp = qk.shape[0]
    nb = tp // TM

    def body(qk_ref, v_ref, mo_ref, sm_ref, gb_ref, hg_ref, y_ref, cs_ref, ns_ref, c_scr, n_scr):
        blk = pl.program_id(0)

        @pl.when(blk == 0)
        def _():
            c_scr[...] = jnp.zeros_like(c_scr)
            n_scr[...] = jnp.zeros_like(n_scr)

        k = _chunk_consts()
        io8 = lax.broadcasted_iota(jnp.int32, (8, DQK), 0)

        def chunk(c, carry):
            r0 = pl.multiple_of(c * CHUNK, CHUNK)
            rows = pl.ds(r0, CHUNK)
            valid = _valid_rows(blk, c)
            _, li_all, b_all = _mlstm_gates(sm_ref[rows, :], gb_ref[...], valid, k)
            heads = range(NH)
            qs = [qk_ref[rows, h * DQK:(h + 1) * DQK] for h in heads]
            ks = [qk_ref[rows, NH * DQK + h * DQK:NH * DQK + (h + 1) * DQK] for h in heads]
            vs = [v_ref[rows, h * DV:(h + 1) * DV] for h in heads]
            cst = [c_scr[h] for h in heads]
            nrow = [n_scr[h, 0:1, :] for h in heads]
            m11 = [jnp.max(n_scr[h, 1:2, :], axis=1, keepdims=True) for h in heads]
            f = [_mlstm_open(h, qs[h], ks[h], cst[h], li_all, b_all, k) for h in heads]
            f = [_mlstm_weights(h, f[h], qs[h], vs[h], li_all, b_all, nrow[h], m11[h], k) for h in heads]
            wk = [f[h]["w"] * ks[h].astype(F32) for h in heads]
            kv = [_tn(vs[h], wk[h].astype(BF)) for h in heads]
            for h in heads:
                hm = _mlstm_out(f[h])["hm"]
                gate = _sigmoid(mo_ref[rows, h * DV:(h + 1) * DV].astype(F32))
                y_ref[rows, h * DV:(h + 1) * DV] = _headnorm_fwd(hm, hg_ref[:, h * DV:(h + 1) * DV], gate).astype(BF)
                cs_ref[c, h] = f[h]["cb"]
                ns_ref[c, h] = jnp.where(io8 == 0, nrow[h], jnp.where(io8 == 1, m11[h], 0.0))
                c_scr[h] = f[h]["a_s"] * cst[h] + kv[h]
                n_scr[h, 0:1, :] = f[h]["a_s"] * nrow[h] + jnp.sum(wk[h], axis=0, keepdims=True)
                n_scr[h, 1:2, :] = jnp.broadcast_to(f[h]["m_new"], (1, DQK))
            return carry

        lax.fori_loop(0, CPB, chunk, 0, unroll=2)

    return pl.pallas_call(
        body, grid=(nb,),
        in_specs=[_row_spec(D), _row_spec(D, CB_MV), _row_spec(D, CB_MO), _row_spec(N_SMALL), _full_spec((1, N_SMALL)), _full_spec((1, D))],
        out_specs=[_row_spec(D), pl.BlockSpec((CPB, NH, DV, DQK), lambda i: (i, 0, 0, 0)),
                   pl.BlockSpec((CPB, NH, 8, DQK), lambda i: (i, 0, 0, 0))],
        out_shape=(jax.ShapeDtypeStruct((tp, D), BF), jax.ShapeDtypeStruct((tp // CHUNK, NH, DV, DQK), BF),
                   jax.ShapeDtypeStruct((tp // CHUNK, NH, 8, DQK), F32)),
        scratch_shapes=[pltpu.VMEM((NH, DV, DQK), F32), pltpu.VMEM((NH, 8, DQK), F32)],
        compiler_params=_params(), name=name)(qk, pbig, pbig, small, gbias, headg)


def _mlstm_bwd(dy, qk, pbig, small, gbias, headg, cs, ns, dproj, name):
    tp = qk.shape[0]
    nb = tp // TM

    def body(dy_ref, qk_ref, v_ref, mo_ref, sm_ref, gb_ref, hg_ref, cs_ref, ns_ref, _,
             dqk_ref, dproj_ref, dsm_ref, dgb_ref, dhg_ref, dc_scr, dn_scr):
        step = pl.program_id(0)
        blk = nb - 1 - step

        @pl.when(step == 0)
        def _():
            dc_scr[...] = jnp.zeros_like(dc_scr)
            dn_scr[...] = jnp.zeros_like(dn_scr)
            dgb_ref[...] = jnp.zeros_like(dgb_ref)
            dhg_ref[...] = jnp.zeros_like(dhg_ref)

        k = _chunk_consts()
        lane, rowio = k["lane"], k["rowio"]

        def chunk(cc, carry):
            c = CPB - 1 - cc
            r0 = pl.multiple_of(c * CHUNK, CHUNK)
            rows = pl.ds(r0, CHUNK)
            valid = _valid_rows(blk, c)
            pre, li_all, b_all = _mlstm_gates(sm_ref[rows, :], gb_ref[...], valid, k)
            dli_all = jnp.zeros((CHUNK, N_SMALL), F32)
            db_all = jnp.zeros((CHUNK, N_SMALL), F32)
            heads = range(NH)
            qs = [qk_ref[rows, h * DQK:(h + 1) * DQK] for h in heads]
            ks = [qk_ref[rows, NH * DQK + h * DQK:NH * DQK + (h + 1) * DQK] for h in heads]
            vs = [v_ref[rows, h * DV:(h + 1) * DV] for h in heads]
            cst = [cs_ref[c, h].astype(F32) for h in heads]
            nrow = [ns_ref[c, h, 0:1, :] for h in heads]
            m11 = [jnp.max(ns_ref[c, h, 1:2, :], axis=1, keepdims=True) for h in heads]
            f = [_mlstm_open(h, qs[h], ks[h], cst[h], li_all, b_all, k) for h in heads]
            f = [_mlstm_weights(h, f[h], qs[h], vs[h], li_all, b_all, nrow[h], m11[h], k) for h in heads]
            f = [_mlstm_out(f[h]) for h in heads]
            t = []
            for h in heads:
                gain = hg_ref[:, h * DV:(h + 1) * DV]
                gate = _sigmoid(mo_ref[rows, h * DV:(h + 1) * DV].astype(F32))
                dhm, dgate, dgain = _headnorm_bwd(dy_ref[rows, h * DV:(h + 1) * DV].astype(F32), f[h]["hm"], gain, gate)
                dproj_ref[rows, D + h * DV:D + (h + 1) * DV] = (dgate * gate * (1.0 - gate)).astype(BF)
                dhg_ref[:, h * DV:(h + 1) * DV] += dgain
                r, den = f[h]["r"], f[h]["den"]
                dnum = dhm / r
                dr = -jnp.sum(dhm * f[h]["hm"], axis=1, keepdims=True) / r
                dden = jnp.where(jnp.abs(den) > f[h]["floor"], dr * jnp.sign(den), 0.0)
                dnb = dnum.astype(BF)
                dc_new = dc_scr[h]
                dcb = dc_new.astype(BF)
                t.append(dict(dnum=dnum, dden=dden, dnb=dnb, dc_new=dc_new, dn_new=dn_scr[h],
                              dwm=_nt(dnb, vs[h]), vdc=_nn(vs[h], dcb), kdc=_nt(ks[h], dcb)))
            for h in heads:
                dw_mat = t[h]["dwm"] + t[h]["dden"]
                dsim = (f[h]["e"] * dw_mat).astype(BF)
                gm = f[h]["w_mat"] * dw_mat
                t[h].update(gm=gm, dv0=_tn(f[h]["w_mat"].astype(BF), t[h]["dnb"]), dq0=_nn(dsim, ks[h]),
                            dq1=_nn(t[h]["dnb"], f[h]["cb"]), dk0=_tn(dsim, qs[h]),
                            dcq=_tn((f[h]["a"] * t[h]["dnum"]).astype(BF), qs[h]), cs2=_tn(gm, k["ones"], precision=HI))
            for h in heads:
                a, w, a_s = f[h]["a"], f[h]["w"], f[h]["a_s"]
                dnum, dden, dc_new, dn_new, vdc, gm = (t[h][n] for n in ("dnum", "dden", "dc_new", "dn_new", "vdc", "gm"))
                kf = ks[h].astype(F32)
                dproj_ref[rows, h * DV:(h + 1) * DV] = (t[h]["dv0"] + w * t[h]["kdc"]).astype(BF)
                adden = a * dden
                dqk_ref[rows, h * DQK:(h + 1) * DQK] = t[h]["dq0"] + a * t[h]["dq1"] + adden * nrow[h]
                dqk_ref[rows, NH * DQK + h * DQK:NH * DQK + (h + 1) * DQK] = t[h]["dk0"] + w * vdc + w * dn_new
                da = jnp.sum(dnum * f[h]["cq"], axis=1, keepdims=True) + dden * f[h]["nq"]
                dw = jnp.sum(vdc * kf, axis=1, keepdims=True) + jnp.sum(kf * dn_new, axis=1, keepdims=True)
                da_s = _sum_all(dc_new * cst[h]) + jnp.sum(dn_new * nrow[h], axis=1, keepdims=True)
                wdw = w * dw
                rs = jnp.sum(gm, axis=1, keepdims=True)
                cs_col = _col(t[h]["cs2"], lane, 0)
                dg = a_s * da_s + jnp.sum(wdw, axis=0, keepdims=True)
                db = a * da + rs - cs_col - wdw + jnp.where(rowio == CHUNK - 1, dg, 0.0)
                dli_all = dli_all + jnp.where(lane == h, cs_col + wdw, 0.0)
                db_all = db_all + jnp.where(lane == NH + h, db, 0.0)
                dc_scr[h] = a_s * dc_new + t[h]["dcq"]
                dn_scr[h] = a_s * dn_new + jnp.sum(adden * f[h]["qf"], axis=0, keepdims=True)
            dlf_all = _nn(k["triu_f"], db_all, precision=HI)
            dsm = jnp.where(valid, dli_all + dlf_all * _sigmoid(-pre), 0.0)
            dsm = jnp.where(lane < 2 * NH, dsm, 0.0)
            dsm_ref[rows, :] = dsm
            dgb_ref[0:1, :] += jnp.sum(dsm, axis=0, keepdims=True)
            return carry

        lax.fori_loop(0, CPB, chunk, 0, unroll=2)

    rev = lambda col: (lambda i: (nb - 1 - i, col))
    rspec = lambda width, col=0: pl.BlockSpec((TM, width), rev(col))
    return pl.pallas_call(
        body, grid=(nb,),
        in_specs=[rspec(D), rspec(D), rspec(D, CB_MV), rspec(D, CB_MO), rspec(N_SMALL), _full_spec((1, N_SMALL)), _full_spec((1, D)),
                  pl.BlockSpec((CPB, NH, DV, DQK), lambda i: (nb - 1 - i, 0, 0, 0)),
                  pl.BlockSpec((CPB, NH, 8, DQK), lambda i: (nb - 1 - i, 0, 0, 0)), ANY],
        out_specs=[rspec(D), rspec(2 * D, CB_MV // 2), rspec(N_SMALL), _full_spec((8, N_SMALL)), _full_spec((1, D))],
        out_shape=(jax.ShapeDtypeStruct((tp, D), F32), jax.ShapeDtypeStruct(dproj.shape, BF),
                   jax.ShapeDtypeStruct((tp, N_SMALL), F32), jax.ShapeDtypeStruct((8, N_SMALL), F32),
                   jax.ShapeDtypeStruct((1, D), F32)),
        scratch_shapes=[pltpu.VMEM((NH, DV, DQK), F32), pltpu.VMEM((NH, 1, DQK), F32)],
        input_output_aliases={9: 1}, compiler_params=_params(), name=name)(dy, qk, pbig, pbig, small, gbias, headg, cs, ns, dproj)


def _gla_loga(sm_ref, a2_ref, a2b_ref, blk):
    za = _nn(sm_ref[...].astype(BF), a2_ref[...]) + a2b_ref[...]
    row = blk * TM + lax.broadcasted_iota(jnp.int32, (TM, 1), 0)
    return za, jnp.where(row >= FIRST_VALID, _logsig(za) / G_TAU, 0.0)


def _gla_head(h, q_ref, k_ref, rows, bc, btot, k):
    sl = slice(h * DQK, (h + 1) * DQK)
    bch = bc[:, sl]
    bth = btot[:, sl]
    gq = q_ref[rows, h * DQK:(h + 1) * DQK].astype(F32)
    gk = k_ref[rows, NH * DQK + h * DQK:NH * DQK + (h + 1) * DQK].astype(F32)
    e_pos = jnp.exp(bch) * (DQK ** -0.5)
    e_neg = jnp.exp(-bch)
    e_end = jnp.exp(bth - bch)
    qd = gq * e_pos
    ki = gk * e_neg
    ke = gk * e_end
    att = jnp.where(k["tri"], _nt(qd.astype(BF), ki.astype(BF)), 0.0)
    return dict(e_pos=e_pos, e_neg=e_neg, e_end=e_end, qd=qd, ki=ki, ke=ke, att=att, decay=jnp.exp(bth))


def _gla_fwd(pbig, small, a2p, a2b, headg, name):
    tp = pbig.shape[0]
    nb = tp // TM

    def body(qk_ref, v_ref, gr_ref, sm_ref, a2_ref, a2b_ref, hg_ref, y_ref, ss_ref, s_scr, lg_scr):
        blk = pl.program_id(0)

        @pl.when(blk == 0)
        def _():
            s_scr[...] = jnp.zeros_like(s_scr)

        k = _chunk_consts()
        _, loga = _gla_loga(sm_ref, a2_ref, a2b_ref, blk)
        lg_scr[...] = loga

        def chunk(c, carry):
            r0 = pl.multiple_of(c * CHUNK, CHUNK)
            rows = pl.ds(r0, CHUNK)
            bc = _nn(k["tril_f"], lg_scr[rows, :], precision=HI)
            btot = _last_row(bc, k["rowio"])
            heads = range(NH)
            f = [_gla_head(h, qk_ref, qk_ref, rows, bc, btot, k) for h in heads]
            vs = [v_ref[rows, h * DV:(h + 1) * DV] for h in heads]
            sst = [s_scr[h] for h in heads]
            sbs = [s.astype(BF) for s in sst]
            inter = [_nt(f[h]["qd"].astype(BF), sbs[h]) for h in heads]
            intra = [_nn(f[h]["att"].astype(BF), vs[h]) for h in heads]
            kv = [_tn(vs[h], f[h]["ke"].astype(BF)) for h in heads]
            for h in heads:
                gr = gr_ref[rows, h * DV:(h + 1) * DV].astype(F32)
                y_ref[rows, h * DV:(h + 1) * DV] = _headnorm_fwd(intra[h] + inter[h], hg_ref[:, h * DV:(h + 1) * DV],
                                                                   gr * _sigmoid(gr)).astype(BF)
                ss_ref[c, h] = sbs[h]
                s_scr[h] = sst[h] * f[h]["decay"] + kv[h]
            return carry

        lax.fori_loop(0, CPB, chunk, 0, unroll=2)

    return pl.pallas_call(
        body, grid=(nb,),
        in_specs=[_row_spec(D, CB_GQK), _row_spec(D, CB_GV), _row_spec(D, CB_GR), _row_spec(N_SMALL),
                  _full_spec((N_SMALL, NH * DQK)), _full_spec((1, NH * DQK)), _full_spec((1, D))],
        out_specs=[_row_spec(D), pl.BlockSpec((CPB, NH, DV, DQK), lambda i: (i, 0, 0, 0))],
        out_shape=(jax.ShapeDtypeStruct((tp, D), BF), jax.ShapeDtypeStruct((tp // CHUNK, NH, DV, DQK), BF)),
        scratch_shapes=[pltpu.VMEM((NH, DV, DQK), F32), pltpu.VMEM((TM, NH * DQK), F32)],
        compiler_params=_params(), name=name)(pbig, pbig, pbig, small, a2p, a2b, headg)


def _gla_bwd(dy, pbig, small, a2p, a2b, headg, ss, dsm_m, dproj, name):
    tp = pbig.shape[0]
    nb = tp // TM
    nqk = NH * DQK

    def body(dy_ref, qk_ref, v_ref, gr_ref, sm_ref, a2_ref, a2b_ref, hg_ref, ss_ref, dsmm_ref, _,
             dproj_ref, dsm_ref, da2_ref, da2b_ref, dhg_ref, ds_scr, lg_scr, dza_scr):
        step = pl.program_id(0)
        blk = nb - 1 - step

        @pl.when(step == 0)
        def _():
            ds_scr[...] = jnp.zeros_like(ds_scr)
            da2_ref[...] = jnp.zeros_like(da2_ref)
            da2b_ref[...] = jnp.zeros_like(da2b_ref)
            dhg_ref[...] = jnp.zeros_like(dhg_ref)

        k = _chunk_consts()
        rowio = k["rowio"]
        za, loga = _gla_loga(sm_ref, a2_ref, a2b_ref, blk)
        lg_scr[...] = loga

        def chunk(cc, carry):
            c = CPB - 1 - cc
            r0 = pl.multiple_of(c * CHUNK, CHUNK)
            rows = pl.ds(r0, CHUNK)
            bc = _nn(k["tril_f"], lg_scr[rows, :], precision=HI)
            btot = _last_row(bc, rowio)
            heads = range(NH)
            f = [_gla_head(h, qk_ref, qk_ref, rows, bc, btot, k) for h in heads]
            vs = [v_ref[rows, h * DV:(h + 1) * DV] for h in heads]
            sbs = [ss_ref[c, h] for h in heads]
            qdb = [f[h]["qd"].astype(BF) for h in heads]
            attb = [f[h]["att"].astype(BF) for h in heads]
            inter = [_nt(qdb[h], sbs[h]) for h in heads]
            intra = [_nn(attb[h], vs[h]) for h in heads]
            dsn = [ds_scr[h] for h in heads]
            dsb = [d.astype(BF) for d in dsn]
            dke = [_nn(vs[h], dsb[h]) for h in heads]
            dv1 = [_nt(f[h]["ke"].astype(BF), dsb[h]) for h in heads]
            t = []
            for h in heads:
                gr = gr_ref[rows, h * DV:(h + 1) * DV].astype(F32)
                sg = _sigmoid(gr)
                gain = hg_ref[:, h * DV:(h + 1) * DV]
                do, dact, dgain = _headnorm_bwd(dy_ref[rows, h * DV:(h + 1) * DV].astype(F32), intra[h] + inter[h], gain, gr * sg)
                dproj_ref[rows, 2 * D + h * DV:2 * D + (h + 1) * DV] = (dact * sg * (1.0 + gr * (1.0 - sg))).astype(BF)
                dhg_ref[:, h * DV:(h + 1) * DV] += dgain
                dob = do.astype(BF)
                t.append(dict(dob=dob, datt=_nt(dob, vs[h]), dv0=_tn(attb[h], dob), dq1=_nn(dob, sbs[h]), dsq=_tn(dob, qdb[h])))
            for h in heads:
                datt = jnp.where(k["tri"], t[h]["datt"], 0.0).astype(BF)
                t[h].update(dq0=_nn(datt, f[h]["ki"].astype(BF)), dki=_tn(datt, qdb[h]))
            dbc_parts = []
            for h in heads:
                dqd = t[h]["dq0"] + t[h]["dq1"]
                dki = t[h]["dki"]
                dproj_ref[rows, D + h * DV:D + (h + 1) * DV] = (t[h]["dv0"] + dv1[h]).astype(BF)
                dproj_ref[rows, h * DQK:(h + 1) * DQK] = (dqd * f[h]["e_pos"]).astype(BF)
                dproj_ref[rows, nqk + h * DQK:nqk + (h + 1) * DQK] = (dki * f[h]["e_neg"] + dke[h] * f[h]["e_end"]).astype(BF)
                dke_ke = dke[h] * f[h]["ke"]
                dbtot = (jnp.sum(dke_ke, axis=0, keepdims=True)
                         + jnp.sum(dsn[h] * sbs[h].astype(F32), axis=0, keepdims=True) * f[h]["decay"])
                dbc_parts.append(dqd * f[h]["qd"] - dki * f[h]["ki"] - dke_ke + jnp.where(rowio == CHUNK - 1, dbtot, 0.0))
                ds_scr[h] = dsn[h] * f[h]["decay"] + t[h]["dsq"]
            dbc = jnp.concatenate(dbc_parts, axis=1)
            dza_scr[rows, :] = _nn(k["triu_f"], dbc, precision=HI)
            return carry

        lax.fori_loop(0, CPB, chunk, 0, unroll=2)
        row = blk * TM + lax.broadcasted_iota(jnp.int32, (TM, 1), 0)
        dza = jnp.where(row >= FIRST_VALID, dza_scr[...] * (_sigmoid(-za) / G_TAU), 0.0)
        dzb = dza.astype(BF)
        dsm_ref[...] = (_nt(dzb, a2_ref[...]) + dsmm_ref[...]).astype(BF)
        da2_ref[...] += _tn(sm_ref[...].astype(BF), dzb)
        da2b_ref[...] += jnp.sum(dza, axis=0, keepdims=True)

    rspec = lambda width, col=0: pl.BlockSpec((TM, width), lambda i: (nb - 1 - i, col))
    return pl.pallas_call(
        body, grid=(nb,),
        in_specs=[rspec(D), rspec(D, CB_GQK), rspec(D, CB_GV), rspec(D, CB_GR), rspec(N_SMALL),
                  _full_spec((N_SMALL, nqk)), _full_spec((1, nqk)), _full_spec((1, D)),
                  pl.BlockSpec((CPB, NH, DV, DQK), lambda i: (nb - 1 - i, 0, 0, 0)), rspec(N_SMALL), ANY],
        out_specs=[rspec(3 * D, 0), rspec(N_SMALL), _full_spec((N_SMALL, nqk)), _full_spec((1, nqk)), _full_spec((1, D))],
        out_shape=(jax.ShapeDtypeStruct(dproj.shape, BF),
                   jax.ShapeDtypeStruct((tp, N_SMALL), BF), jax.ShapeDtypeStruct((N_SMALL, nqk), F32),
                   jax.ShapeDtypeStruct((1, nqk), F32), jax.ShapeDtypeStruct((1, D), F32)),
        scratch_shapes=[pltpu.VMEM((NH, DV, DQK), F32), pltpu.VMEM((TM, nqk), F32), pltpu.VMEM((TM, nqk), F32)],
        input_output_aliases={10: 0}, compiler_params=_params(), name=name)(dy, pbig, pbig, pbig, small, a2p, a2b, headg, ss, dsm_m, dproj)


PIECE_BYTES = 1 << 20
MAX_PIECES = 32


def _place():
    return lax.axis_index("x"), lax.axis_index("y"), lax.axis_index("c")


def _piece_rows(rows, row_bytes, align):
    want = min(MAX_PIECES, max(1, -(-rows * row_bytes // PIECE_BYTES)))
    best = rows
    for k in range(1, want + 1):
        if rows % k == 0 and (rows // k) % align == 0:
            best = rows // k
    return best


def _remote(src, dst, send_sems, recv_sems, k, to):
    return pltpu.make_async_remote_copy(src_ref=src, dst_ref=dst, send_sem=send_sems.at[k], recv_sem=recv_sems.at[k],
                                        device_id=to, device_id_type=MESH)


def _all_gather_chips(p, name):
    r, n = p.shape
    rh = r // 2
    align = 32 // p.dtype.itemsize
    assert r % (2 * align) == 0
    cr = _piece_rows(rh, n * p.dtype.itemsize, align)

    def body(p_ref, o_ref, send_sems, recv_sems):
        x, y, c = _place()
        chips = [(1 - x, y), (x, 1 - y), (1 - x, 1 - y)]
        sib = (x, y, 1 - c)

        def half(hc, piece=None):
            if piece is None:
                return pl.ds(pl.multiple_of(hc * rh, align), rh)
            return pl.ds(pl.multiple_of(hc * rh + piece * cr, align), cr)

        first = [_remote(p_ref.at[half(c)], o_ref.at[j, half(c)], send_sems, recv_sems, j, (*chip, c))
                 for j, chip in enumerate(chips)]
        for cp in first:
            cp.start()
        for j, cp in enumerate(first):
            cp.wait_recv()
            for i in range(rh // cr):
                _remote(o_ref.at[j, half(c, i)], o_ref.at[j, half(c, i)], send_sems, recv_sems, 3 + j, sib).start()
        for j in range(3):
            block = _remote(o_ref.at[j, half(c)], o_ref.at[j, half(1 - c)], send_sems, recv_sems, 3 + j, sib)
            block.wait_send()
            block.wait_recv()
        for cp in first:
            cp.wait_send()

    return pl.pallas_call(
        body, in_specs=[ANY], out_specs=ANY, out_shape=jax.ShapeDtypeStruct((3, r, n), p.dtype),
        scratch_shapes=[pltpu.SemaphoreType.DMA((6,)), pltpu.SemaphoreType.DMA((6,))],
        name=name)(p)


def _by_chip(mine, others):
    me = 2 * lax.axis_index("x") + lax.axis_index("y")
    by_mask = jnp.stack([mine, others[1], others[0], others[2]])
    return [lax.dynamic_index_in_dim(by_mask, q ^ me, 0, keepdims=False) for q in range(4)]


def _swap_halves(items, name):
    k = len(items)

    def body(*refs):
        a_refs, got_refs = refs[:k], refs[k:2 * k]
        send_sems, recv_sems = refs[2 * k:]
        x, y, c = _place()
        sib = (x, y, 1 - c)
        for i, a in enumerate(items):
            _, r, n = a.shape
            rh = r // 2
            cr = _piece_rows(rh, n * a.dtype.itemsize, 8)
            for q in range(4):
                for t in range(rh // cr):
                    other = pl.ds(pl.multiple_of((1 - c) * rh + t * cr, 8), cr)
                    _remote(a_refs[i].at[q, other], got_refs[i].at[q, pl.ds(t * cr, cr)], send_sems, recv_sems, i, sib).start()
        for i, a in enumerate(items):
            block = _remote(a_refs[i].at[:, pl.ds(0, a.shape[1] // 2)], got_refs[i], send_sems, recv_sems, i, sib)
            block.wait_send()
            block.wait_recv()

    return pl.pallas_call(
        body, in_specs=[ANY] * k, out_specs=[ANY] * k,
        out_shape=tuple(jax.ShapeDtypeStruct((4, a.shape[1] // 2, a.shape[2]), a.dtype) for a in items),
        scratch_shapes=[pltpu.SemaphoreType.DMA((k,)), pltpu.SemaphoreType.DMA((k,))], name=name)(*items)


def _scatter_chips(items, name):
    k = len(items)

    def body(*refs):
        s_refs, o_refs = refs[:k], refs[k:2 * k]
        send_sems, recv_sems = refs[2 * k:]
        x, y, c = _place()
        chips = [(1 - x, y), (x, 1 - y), (1 - x, 1 - y)]
        sent = []
        for i in range(k):
            for j, (cx, cy) in enumerate(chips):
                cp = _remote(s_refs[i].at[2 * cx + cy], o_refs[i].at[j], send_sems, recv_sems, 3 * i + j, (cx, cy, c))
                cp.start()
                sent.append(cp)
        for cp in sent:
            cp.wait_recv()
        for cp in sent:
            cp.wait_send()

    return pl.pallas_call(
        body, in_specs=[ANY] * k, out_specs=[ANY] * k,
        out_shape=tuple(jax.ShapeDtypeStruct((3,) + s.shape[1:], s.dtype) for s in items),
        scratch_shapes=[pltpu.SemaphoreType.DMA((3 * k,)), pltpu.SemaphoreType.DMA((3 * k,))], name=name)(*items)


def _join_halves(items, name):
    k = len(items)

    def body(*refs):
        f_refs, o_refs = refs[:k], refs[k:2 * k]
        send_sems, recv_sems = refs[2 * k:]
        x, y, c = _place()
        sib = (x, y, 1 - c)
        for i, f in enumerate(items):
            rh, n = f.shape
            cr = _piece_rows(rh, n * f.dtype.itemsize, 8)
            for t in range(rh // cr):
                rows = pl.ds(t * cr, cr)
                _remote(f_refs[i].at[rows], o_refs[i].at[rows], send_sems, recv_sems, i, sib).start()
        for i in range(k):
            block = _remote(f_refs[i], o_refs[i], send_sems, recv_sems, i, sib)
            block.wait_send()
            block.wait_recv()

    return pl.pallas_call(
        body, in_specs=[ANY] * k, out_specs=[ANY] * k, out_shape=tuple(jax.ShapeDtypeStruct(f.shape, f.dtype) for f in items),
        scratch_shapes=[pltpu.SemaphoreType.DMA((k,)), pltpu.SemaphoreType.DMA((k,))], name=name)(*items)


SMALL_ROWS = 16
SMALL_SHARD_SHAPES = [(N_META, 256), (4, 256), (G_RANK, 128), (NH, 64), (NH, 64)]
REPL_SHAPES = [(1, D), (1, D), (1, 2, NH), (1, NH * DQK), (1, D), (D,)]
W_IN_SHARD = 2054


def _pack_small(parts):
    flat = jnp.concatenate([p.reshape(-1) for p in parts])
    return jnp.pad(flat, (0, SMALL_ROWS * D - flat.shape[0])).reshape(SMALL_ROWS, D)


def _unpack_small(block, shapes):
    flat, out, off = block.reshape(-1), [], 0
    for shp in shapes:
        n = 1
        for s in shp:
            n *= s
        out.append(flat[off:off + n].reshape(shp))
        off += n
    return out


def _proj_cols_from_w_in(w_in_f):
    w_big = jnp.concatenate([w_in_f[:, 3080:5128], w_in_f[:, 5144:6168], w_in_f[:, 0:1024], w_in_f[:, 6168:8216],
                             w_in_f[:, 1024:2048], w_in_f[:, 2056:3080]], axis=1)
    w_small = jnp.concatenate([w_in_f[:, 2048:2056], w_in_f[:, 5128:5144], jnp.zeros((D, N_SMALL - 24), w_in_f.dtype)], axis=1)
    return w_big, w_small


def _w_in_from_proj_cols(d_wall):
    big, small = d_wall[:, 0:N_BIG], d_wall[:, N_BIG:N_ALL]
    return jnp.concatenate([big[:, 3072:4096], big[:, 6144:7168], small[:, 0:8], big[:, 7168:8192], big[:, 0:2048],
                            small[:, 8:24], big[:, 2048:3072], big[:, 4096:6144]], axis=1)


def kernel(x, meta_tokens, norm1_g, w_in, conv_w, conv_b, m_gate_b, g_a2, g_a2_b, m_head_g, g_head_g, w_branch_m, w_branch_g, w_out, norm2_g, w_ff_gate, w_ff_up, w_ff_down, final_g, loss_target, m_meta_tokens, m_norm1_g, m_w_in, m_conv_w, m_conv_b, m_m_gate_b, m_g_a2, m_g_a2_b, m_m_head_g, m_g_head_g, m_w_branch_m, m_w_branch_g, m_w_out, m_norm2_g, m_w_ff_gate, m_w_ff_up, m_w_ff_down, m_final_g, v_meta_tokens, v_norm1_g, v_w_in, v_conv_w, v_conv_b, v_m_gate_b, v_g_a2, v_g_a2_b, v_m_head_g, v_g_head_g, v_w_branch_m, v_w_branch_g, v_w_out, v_norm2_g, v_w_ff_gate, v_w_ff_up, v_w_ff_down, v_final_g):
    w = _gather_weights(w_in, w_branch_m, w_branch_g, w_out, w_ff_gate, w_ff_up, w_ff_down, meta_tokens, conv_w, g_a2, m_head_g, g_head_g)
    loss_local, dx, local = _local_step(x[0], loss_target[0], w, norm1_g, conv_b, m_gate_b, g_a2_b, norm2_g, final_g)
    grads = _reduce_grads(local)

    weights = [w_in, w_branch_m, w_branch_g, w_out, w_ff_gate, w_ff_up, w_ff_down, meta_tokens, conv_w, g_a2, m_head_g, g_head_g,
               norm1_g, conv_b, m_gate_b, g_a2_b, norm2_g, final_g]
    moms = [m_w_in, m_w_branch_m, m_w_branch_g, m_w_out, m_w_ff_gate, m_w_ff_up, m_w_ff_down, m_meta_tokens, m_conv_w, m_g_a2,
            m_m_head_g, m_g_head_g, m_norm1_g, m_conv_b, m_m_gate_b, m_g_a2_b, m_norm2_g, m_final_g]
    vels = [v_w_in, v_w_branch_m, v_w_branch_g, v_w_out, v_w_ff_gate, v_w_ff_up, v_w_ff_down, v_meta_tokens, v_conv_w, v_g_a2,
            v_m_head_g, v_g_head_g, v_norm1_g, v_conv_b, v_m_gate_b, v_g_a2_b, v_norm2_g, v_final_g]
    res = {}
    for nm, wt, g, m, v in zip(PACK_ORDER, weights, grads, moms, vels):
        two_d = (wt.size // wt.shape[-1], wt.shape[-1])
        d, nm_, nv_ = _adamw(wt.reshape(two_d), g.reshape(two_d), m.reshape(two_d), v.reshape(two_d), "adamw_" + nm)
        res[nm] = (g.reshape(wt.shape), d.reshape(wt.shape), nm_.reshape(wt.shape), nv_.reshape(wt.shape))

    order = ["meta_tokens", "norm1_g", "w_in", "conv_w", "conv_b", "m_gate_b", "g_a2", "g_a2_b", "m_head_g", "g_head_g",
             "w_branch_m", "w_branch_g", "w_out", "norm2_g", "w_ff_gate", "w_ff_up", "w_ff_down", "final_g"]
    loss = lax.psum(loss_local[0, 0], ("x", "y", "c"))
    grad_x = dx.reshape(x.shape)
    return (loss, grad_x, *[res[n][0] for n in order], *[res[n][1] for n in order],
            *[res[n][2] for n in order], *[res[n][3] for n in order])


PACK_ORDER = ["w_in", "w_branch_m", "w_branch_g", "w_out", "w_ff_gate", "w_ff_up", "w_ff_down", "meta_tokens", "conv_w", "g_a2",
              "m_head_g", "g_head_g", "norm1_g", "conv_b", "m_gate_b", "g_a2_b", "norm2_g", "final_g"]


def _gather_weights(w_in, w_branch_m, w_branch_g, w_out, w_ff_gate, w_ff_up, w_ff_down, meta_tokens, conv_w, g_a2, m_head_g, g_head_g):
    bf = lambda a: a.astype(BF)
    rows_local = jnp.concatenate([bf(w_branch_m[0]), bf(w_branch_g[0]), bf(w_out[0]), bf(w_ff_down[0]),
                                  bf(w_ff_gate[0].T), bf(w_ff_up[0].T)], axis=0)
    win_local = bf(w_in[0])
    small_local = _pack_small([meta_tokens, conv_w[0], g_a2[0], m_head_g[0], g_head_g[0]])
    rows_all = jnp.stack(_by_chip(rows_local, _all_gather_chips(rows_local, "gather_rows")))
    win_all = _by_chip(win_local, _all_gather_chips(win_local, "gather_w_in"))
    small_all = _by_chip(small_local, _all_gather_chips(small_local, "gather_small"))
    cut = lambda lo, hi: rows_all[:, lo:hi].reshape(4 * (hi - lo), D)
    wbm, wbg, wout, wdown = cut(0, 256), cut(256, 512), cut(512, 768), cut(768, 1472)
    wgu_t = jnp.concatenate([cut(1472, 2176), cut(2176, 2880)], axis=0)
    w_in_f = jnp.concatenate([win_all[q] for q in range(4)], axis=1)
    small_sh = [_unpack_small(small_all[q], SMALL_SHARD_SHAPES) for q in range(4)]
    cat = lambda i: jnp.concatenate([s[i] for s in small_sh], axis=-1)
    return dict(w_in=w_in_f, wbm=wbm, wbg=wbg, wout=wout, wgu_t=wgu_t, wdown=wdown, meta=cat(0), convw=cat(1), ga2=cat(2),
                mhg=cat(3).reshape(1, D), ghg=cat(4).reshape(1, D))


def _local_step(x0, target, w, norm1_g, conv_b, m_gate_b, g_a2_b, norm2_g, final_g):
    w_in_f, wbm, wbg, wout, wgu_t, wdown = w["w_in"], w["wbm"], w["wbg"], w["wout"], w["wgu_t"], w["wdown"]
    meta_f, convw_f, ga2_f, mhg_f, ghg_f = w["meta"], w["convw"], w["ga2"], w["mhg"], w["ghg"]
    w_big, w_small = _proj_cols_from_w_in(w_in_f)
    w_all = jnp.concatenate([w_big, w_small], axis=1)
    gbias = jnp.concatenate([m_gate_b.reshape(1, 2 * NH), jnp.zeros((1, N_SMALL - 2 * NH), F32)], axis=1)
    a2p = jnp.concatenate([jnp.zeros((8, NH * DQK), F32), ga2_f, jnp.zeros((N_SMALL - 24, NH * DQK), F32)], axis=0).astype(BF)
    convb = conv_b.reshape(1, D)
    g1 = norm1_g.reshape(1, D)
    g2 = norm2_g.reshape(1, D)
    gf = final_g.reshape(1, D)
    h0 = jnp.concatenate([jnp.zeros((FIRST_VALID, D), F32), meta_f, x0], axis=0)

    xn1, rstd1 = _rms_fwd(h0, g1, "rms1")
    pbig = _mm(xn1, w_big, nt=False, out_dtype=BF, tn=1024, name="proj_big")
    small = _mm(xn1, w_small, nt=False, out_dtype=F32, tn=N_SMALL, name="proj_small")
    qk = _conv_fwd(pbig, convw_f, convb, "conv_fwd")
    y_m, m_cs, m_ns = _mlstm_fwd(qk, pbig, small, gbias, mhg_f, "mlstm_fwd")
    y_g, g_ss = _gla_fwd(pbig, small, a2p, g_a2_b, ghg_f, "gla_fwd")
    p_m = _mm(y_m, wbm, nt=False, out_dtype=BF, tn=1024, name="branch_m")
    p_g = _mm(y_g, wbg, nt=False, out_dtype=BF, tn=1024, name="branch_g")
    merged = _merge_fwd(p_m, p_g, pbig, "merge_fwd")
    h1 = _mm(merged, wout, nt=False, out_dtype=F32, tn=1024, addend=h0, name="out_proj")
    hn, rstd2 = _rms_fwd(h1, g2, "rms2")
    gu = _mm(hn, wgu_t, nt=True, out_dtype=BF, tn=1408, name="ff_in")
    ff = _swiglu_fwd(gu, "swiglu_fwd")
    h2 = _mm(ff, wdown, nt=False, out_dtype=F32, tn=1024, tm=TM, addend=h1, name="ff_down")
    dh2, loss_local, d_final_g = _final_loss(h2, target, gf, "final_loss")

    dff = _mm(dh2, wdown, nt=True, out_dtype=BF, tn=1408, name="d_ff")
    d_wdown = _mm_tn(ff, dh2, tm=1408, tn=1024, name="dw_ff_down")
    dgu = _swiglu_bwd(dff, gu, "swiglu_bwd")
    dhn = _mm(dgu, wgu_t, nt=False, out_dtype=F32, tn=1024, tk=2816, tm=TM, name="d_hn")
    d_wgu_t = _mm_tn(dgu, hn, tm=1408, tn=1024, name="dw_ff_in")
    dh1, d_g2 = _rms_bwd(dhn, h1, rstd2, g2, dh2, "rms2_bwd")
    dmerged = _mm(dh1, wout, nt=True, out_dtype=BF, tn=1024, name="d_merged")
    d_wout = _mm_tn(merged, dh1, tm=1024, tn=1024, name="dw_out")
    dp_m, dp_g, dproj = _merge_bwd(dmerged, p_m, p_g, pbig, "merge_bwd")
    dy_m = _mm(dp_m, wbm, nt=True, out_dtype=BF, tn=1024, name="d_ym")
    dy_g = _mm(dp_g, wbg, nt=True, out_dtype=BF, tn=1024, name="d_yg")
    d_wbm = _mm_tn(y_m, dp_m, tm=1024, tn=1024, name="dw_branch_m")
    d_wbg = _mm_tn(y_g, dp_g, tm=1024, tn=1024, name="dw_branch_g")
    dqk_m, dproj, dsm_m, d_gbias, d_mhg = _mlstm_bwd(dy_m, qk, pbig, small, gbias, mhg_f, m_cs, m_ns, dproj, "mlstm_bwd")
    dconv, d_convwb = _conv_bwd_pre(dqk_m, pbig, convw_f, convb, "conv_bwd_pre")
    dproj = _conv_bwd_in(dconv, convw_f, dproj, "conv_bwd_in")
    dproj, dsmall, d_a2p, d_a2b, d_ghg = _gla_bwd(dy_g, pbig, small, a2p, g_a2_b, ghg_f, g_ss, dsm_m, dproj, "gla_bwd")
    dproj = _place_small(dsmall, dproj, "dproj_small")
    dxn = _mm(dproj, w_all, nt=True, out_dtype=F32, tn=1024, tk=1664, name="d_xn")
    d_wall = _mm_tn(xn1, dproj, tm=1024, tn=1664, name="dw_in")
    dh_first, dx, d_g1 = _rms_bwd(dxn, h0, rstd1, g1, dh1, "rms1_bwd", split_first=True)

    small_sharded = [dh_first[FIRST_VALID:TM], d_convwb[0:4], d_a2p[8:24], d_mhg.reshape(NH, DV), d_ghg.reshape(NH, DV)]
    replicated = [d_g1, d_convwb[4:5], d_gbias[0:1, 0:2 * NH].reshape(1, 2, NH), d_a2b, d_g2, d_final_g.reshape(D)]
    local = dict(w_all=d_wall, wbm=d_wbm, wbg=d_wbg, wout=d_wout, wdown=d_wdown, wgu_t=d_wgu_t,
                 small_sharded=small_sharded, replicated=replicated)
    return loss_local, dx, local


def _reduce_grads(local):
    d_win = _w_in_from_proj_cols(local["w_all"])
    win4 = jnp.stack([d_win[:, q * W_IN_SHARD:(q + 1) * W_IN_SHARD] for q in range(4)])
    small4 = jnp.stack([_pack_small([g[:, q * shp[1]:(q + 1) * shp[1]] for g, shp in zip(local["small_sharded"], SMALL_SHARD_SHAPES)]
                                    + local["replicated"]) for q in range(4)])
    gu = local["wgu_t"].reshape(2, 4, D_FF // 4, D)
    items = [win4, local["wbm"].reshape(4, 256, D), local["wbg"].reshape(4, 256, D), local["wout"].reshape(4, 256, D),
             local["wdown"].reshape(4, D_FF // 4, D), gu[0], gu[1], small4]
    c = lax.axis_index("c")
    me = 2 * lax.axis_index("x") + lax.axis_index("y")
    got = _swap_halves(items, "reduce_siblings")
    sums = []
    for i, (a, g) in enumerate(zip(items, got)):
        rh, n = g.shape[1], g.shape[2]
        own = lax.dynamic_slice_in_dim(a, c * rh, rh, axis=1)
        sums.append(_add2(own.reshape(-1, n), g.reshape(-1, n), F32 if i == len(items) - 1 else BF, f"reduce_add2_{i}").reshape(g.shape))
    from_chips = _scatter_chips(sums, "reduce_chips")
    halves = []
    for i, (s, f) in enumerate(zip(sums, from_chips)):
        mine = lax.dynamic_index_in_dim(s, me, 0, keepdims=False)
        if i == len(items) - 1:
            by_chip = _by_chip(mine, f)
            mine, f = by_chip[0], jnp.stack(by_chip[1:])
        halves.append(_add4(mine, f, f"reduce_add4_{i}"))
    got = _join_halves(halves, "reduce_join")
    full = [jnp.where(c == 0, jnp.concatenate([h, g], axis=0), jnp.concatenate([g, h], axis=0)) for h, g in zip(halves, got)]
    smalls = _unpack_small(full[7], SMALL_SHARD_SHAPES + REPL_SHAPES)
    return [full[0], full[1], full[2], full[3], full[5].T, full[6].T, full[4]] + smalls
```

```python
import functools

import jax
import jax.numpy as jnp
from jax import lax
from jax.experimental import pallas as pl
from jax.experimental.pallas import tpu as pltpu

F32 = jnp.float32
BF = jnp.bfloat16
HI = lax.Precision.HIGHEST
MESH = pl.DeviceIdType.MESH

D = 1024
N_META = 16
CHUNK = 128
EPS = 1e-6
NH = 4
DV = 256
DQK = 128
G_RANK = 16
G_TAU = 16.0
D_FF = 2816
TM = 512
FIRST_VALID = TM - N_META
CPB = TM // CHUNK
NEG = -1e30
N_BIG = 8192
CB_GQK, CB_GV, CB_GR, CB_MQK, CB_GM, CB_GG, CB_MV, CB_MO = range(8)
N_SMALL = 128
N_ALL = N_BIG + N_SMALL
VMEM_LIMIT = 56 * 1024 * 1024

ADAM_LR, ADAM_B1, ADAM_B2, ADAM_EPS, ADAM_WD, ADAM_STEP = 0.001, 0.9, 0.999, 1e-08, 0.01, 10

NT_DIMS = (((1,), (1,)), ((), ()))
TN_DIMS = (((0,), (0,)), ((), ()))


def _nt(a, b, **kw):
    return lax.dot_general(a, b, NT_DIMS, preferred_element_type=F32, **kw)


def _tn(a, b, **kw):
    return lax.dot_general(a, b, TN_DIMS, preferred_element_type=F32, **kw)


def _nn(a, b, **kw):
    return jnp.dot(a, b, preferred_element_type=F32, **kw)


def _params(**kw):
    return pltpu.CompilerParams(vmem_limit_bytes=VMEM_LIMIT, **kw)


def _sigmoid(x):
    return 1.0 / (1.0 + jnp.exp(-x))


def _logsig(x):
    return jnp.minimum(x, 0.0) - jnp.log(1.0 + jnp.exp(-jnp.abs(x)))


def _mm_rows(rows):
    return 3 * TM if rows % (3 * TM) == 0 else TM


def _mm(a, b, *, nt, out_dtype, tn, tk=None, tm=None, addend=None, name):
    m, k = a.shape
    n = b.shape[0] if nt else b.shape[1]
    tk = k if tk is None else tk
    tm = _mm_rows(m) if tm is None else tm
    nk = k // tk
    assert m % tm == 0 and n % tn == 0 and k % tk == 0
    dims = NT_DIMS if nt else (((1,), (0,)), ((), ()))

    def finish(r, add_ref, o_ref):
        if add_ref is not None:
            r = r + add_ref[...].astype(F32)
        o_ref[...] = r.astype(o_ref.dtype)

    def body(*refs):
        a_ref, b_ref = refs[:2]
        add_ref = refs[2] if addend is not None else None
        o_ref = refs[3] if addend is not None else refs[2]
        part = lax.dot_general(a_ref[...].astype(BF), b_ref[...].astype(BF), dims, preferred_element_type=F32)
        if nk == 1:
            finish(part, add_ref, o_ref)
            return
        acc_ref = refs[-1]
        kk = pl.program_id(2)

        @pl.when(kk == 0)
        def _():
            acc_ref[...] = part

        @pl.when(jnp.logical_and(kk > 0, kk < nk - 1))
        def _():
            acc_ref[...] += part

        @pl.when(kk == nk - 1)
        def _():
            finish(acc_ref[...] + part, add_ref, o_ref)

    in_specs = [pl.BlockSpec((tm, tk), lambda j, i, kk: (i, kk)),
                pl.BlockSpec((tn, tk), lambda j, i, kk: (j, kk)) if nt else pl.BlockSpec((tk, tn), lambda j, i, kk: (kk, j))]
    args = [a, b]
    if addend is not None:
        in_specs.append(pl.BlockSpec((tm, tn), lambda j, i, kk: (i, j)))
        args.append(addend)
    return pl.pallas_call(
        body, grid=(n // tn, m // tm, nk), in_specs=in_specs,
        out_specs=pl.BlockSpec((tm, tn), lambda j, i, kk: (i, j)),
        out_shape=jax.ShapeDtypeStruct((m, n), out_dtype),
        scratch_shapes=[pltpu.VMEM((tm, tn), F32)] if nk > 1 else [], compiler_params=_params(), name=name)(*args)


def _mm_tn(a, b, *, tm, tn, tk=None, name):
    t, m = a.shape
    n = b.shape[1]
    tk = _mm_rows(t) if tk is None else tk
    assert t % tk == 0 and m % tm == 0 and n % tn == 0

    def body(a_ref, b_ref, o_ref):
        part = _tn(a_ref[...].astype(BF), b_ref[...].astype(BF))

        @pl.when(pl.program_id(2) == 0)
        def _():
            o_ref[...] = part

        @pl.when(pl.program_id(2) > 0)
        def _():
            o_ref[...] += part

    return pl.pallas_call(
        body, grid=(m // tm, n // tn, t // tk),
        in_specs=[pl.BlockSpec((tk, tm), lambda i, j, kk: (kk, i)), pl.BlockSpec((tk, tn), lambda i, j, kk: (kk, j))],
        out_specs=pl.BlockSpec((tm, tn), lambda i, j, kk: (i, j)),
        out_shape=jax.ShapeDtypeStruct((m, n), F32), compiler_params=_params(), name=name)(a, b)


ANY = pl.BlockSpec(memory_space=pl.ANY)


def _row_spec(width, col=0):
    return pl.BlockSpec((TM, width), lambda i: (i, col))


def _full_spec(shape):
    return pl.BlockSpec(shape, lambda i: (0,) * len(shape))


def _rms_fwd(h, g, name):
    tp = h.shape[0]

    def body(h_ref, g_ref, xn_ref, r_ref):
        x = h_ref[...]
        r = lax.rsqrt(jnp.mean(x * x, axis=1, keepdims=True) + EPS)
        xn_ref[...] = (x * r * g_ref[...]).astype(BF)
        r_ref[...] = r

    return pl.pallas_call(
        body, grid=(tp // TM,), in_specs=[_row_spec(D), _full_spec((1, D))],
        out_specs=[_row_spec(D), _row_spec(1)],
        out_shape=(jax.ShapeDtypeStruct((tp, D), BF), jax.ShapeDtypeStruct((tp, 1), F32)),
        compiler_params=_params(), name=name)(h, g)


def _rms_bwd(dxn, h, rstd, g, dres, name, split_first=False):
    tp = h.shape[0]

    def body(dxn_ref, h_ref, r_ref, g_ref, dres_ref, *outs):
        r = r_ref[...]
        xh = h_ref[...] * r
        dxn_v = dxn_ref[...].astype(F32)
        dxh = dxn_v * g_ref[...]
        dh = r * (dxh - xh * jnp.mean(dxh * xh, axis=1, keepdims=True)) + dres_ref[...]
        if split_first:
            first_ref, dh_ref, dg_ref = outs

            @pl.when(pl.program_id(0) == 0)
            def _():
                first_ref[...] = dh
        else:
            dh_ref, dg_ref = outs
        dh_ref[...] = dh
        part = jnp.sum(dxn_v * xh, axis=0, keepdims=True)

        @pl.when(pl.program_id(0) == 0)
        def _():
            dg_ref[...] = part

        @pl.when(pl.program_id(0) > 0)
        def _():
            dg_ref[...] += part

    if split_first:
        out_specs = [_full_spec((TM, D)), pl.BlockSpec((TM, D), lambda i: (jnp.maximum(i - 1, 0), 0)), _full_spec((1, D))]
        out_shape = (jax.ShapeDtypeStruct((TM, D), F32), jax.ShapeDtypeStruct((tp - TM, D), F32), jax.ShapeDtypeStruct((1, D), F32))
    else:
        out_specs = [_row_spec(D), _full_spec((1, D))]
        out_shape = (jax.ShapeDtypeStruct((tp, D), F32), jax.ShapeDtypeStruct((1, D), F32))
    return pl.pallas_call(
        body, grid=(tp // TM,),
        in_specs=[_row_spec(D), _row_spec(D), _row_spec(1), _full_spec((1, D)), _row_spec(D)],
        out_specs=out_specs, out_shape=out_shape, compiler_params=_params(), name=name)(dxn, h, rstd, g, dres)


def _shift_down(x, halo, k):
    rk = pltpu.roll(x, k, 0)
    io = lax.broadcasted_iota(jnp.int32, (8, x.shape[1]), 0)
    top = jnp.where(io < k, pltpu.roll(halo, k, 0), rk[0:8])
    return jnp.concatenate([top, rk[8:]], axis=0)


def _shift_up(x, nxt, k):
    n = x.shape[0]
    rk = pltpu.roll(x, n - k, 0)
    io = lax.broadcasted_iota(jnp.int32, (8, x.shape[1]), 0)
    bot = jnp.where(io >= 8 - k, pltpu.roll(nxt, 8 - k, 0), rk[n - 8:n])
    return jnp.concatenate([rk[:n - 8], bot], axis=0)


def _conv_pre(x, halo, w_ref, b_ref):
    c = x * w_ref[3:4, :] + b_ref[...]
    shifted = []
    for k in (1, 2, 3):
        s = _shift_down(x, halo, k)
        shifted.append(s)
        c = c + s * w_ref[3 - k:4 - k, :]
    return c, shifted


def _qk_scale():
    col = lax.broadcasted_iota(jnp.int32, (1, D), 1)
    return jnp.where(col < NH * DQK, DQK ** -0.5, 1.0).astype(F32)


def _halo_prev_spec():
    return pl.BlockSpec((8, D), lambda i: (jnp.maximum(i * (TM // 8) - 1, 0), CB_MQK))


def _conv_fwd(pbig, w, b, name):
    tp = pbig.shape[0]

    def body(x_ref, halo_ref, w_ref, b_ref, o_ref):
        x = x_ref[...].astype(F32)
        halo = jnp.where(pl.program_id(0) > 0, halo_ref[...].astype(F32), 0.0)
        c, _ = _conv_pre(x, halo, w_ref, b_ref)
        o_ref[...] = (c * _sigmoid(c) * _qk_scale()).astype(BF)

    return pl.pallas_call(
        body, grid=(tp // TM,),
        in_specs=[_row_spec(D, CB_MQK), _halo_prev_spec(), _full_spec((4, D)), _full_spec((1, D))],
        out_specs=_row_spec(D), out_shape=jax.ShapeDtypeStruct((tp, D), BF),
        compiler_params=_params(), name=name)(pbig, pbig, w, b)


def _conv_bwd_pre(dqk, pbig, w, b, name):
    tp = pbig.shape[0]

    def body(d_ref, x_ref, halo_ref, w_ref, b_ref, dc_ref, dwb_ref):
        x = x_ref[...].astype(F32)
        halo = jnp.where(pl.program_id(0) > 0, halo_ref[...].astype(F32), 0.0)
        c, shifted = _conv_pre(x, halo, w_ref, b_ref)
        sg = _sigmoid(c)
        dc = d_ref[...] * _qk_scale() * (sg * (1.0 + c * (1.0 - sg)))
        dc_ref[...] = dc
        taps = [shifted[2], shifted[1], shifted[0], x]
        rows = [jnp.sum(dc * t, axis=0, keepdims=True) for t in taps] + [jnp.sum(dc, axis=0, keepdims=True)]
        io = lax.broadcasted_iota(jnp.int32, (8, D), 0)
        part = jnp.zeros((8, D), F32)
        for r, v in enumerate(rows):
            part = jnp.where(io == r, v, part)

        @pl.when(pl.program_id(0) == 0)
        def _():
            dwb_ref[...] = part

        @pl.when(pl.program_id(0) > 0)
        def _():
            dwb_ref[...] += part

    return pl.pallas_call(
        body, grid=(tp // TM,),
        in_specs=[_row_spec(D), _row_spec(D, CB_MQK), _halo_prev_spec(), _full_spec((4, D)), _full_spec((1, D))],
        out_specs=[_row_spec(D), _full_spec((8, D))],
        out_shape=(jax.ShapeDtypeStruct((tp, D), F32), jax.ShapeDtypeStruct((8, D), F32)),
        compiler_params=_params(), name=name)(dqk, pbig, pbig, w, b)


def _conv_bwd_in(dc, w, dproj, name):
    tp = dc.shape[0]
    nb = tp // TM

    def body(d_ref, nxt_ref, w_ref, _, o_ref):
        d = d_ref[...]
        nxt = jnp.where(pl.program_id(0) < nb - 1, nxt_ref[...], 0.0)
        acc = d * w_ref[3:4, :]
        for k in (1, 2, 3):
            acc = acc + _shift_up(d, nxt, k) * w_ref[3 - k:4 - k, :]
        o_ref[...] = acc.astype(BF)

    return pl.pallas_call(
        body, grid=(nb,),
        in_specs=[_row_spec(D), pl.BlockSpec((8, D), lambda i: (jnp.minimum((i + 1) * (TM // 8), tp // 8 - 1), 0)),
                  _full_spec((4, D)), ANY],
        out_specs=_row_spec(D, CB_MQK), out_shape=jax.ShapeDtypeStruct(dproj.shape, BF),
        input_output_aliases={3: 0}, compiler_params=_params(), name=name)(dc, dc, w, dproj)


def _mm_fused(inputs, products, *, nt, m, n, tm, tn, outs, epilogue, name, nk=1):
    dims = NT_DIMS if nt else (((1,), (0,)), ((), ()))
    nin = len(inputs)
    assert nk == 1 or len(products) == 1

    def body(*refs):
        in_refs, out_refs = refs[:nin], refs[nin:nin + len(outs)]
        i = pl.program_id(1)
        prods = [lax.dot_general(in_refs[ia][...].astype(BF), in_refs[ib][...].astype(BF), dims, preferred_element_type=F32)
                 for ia, ib in products]
        if nk == 1:
            epilogue(prods, in_refs, out_refs, i)
            return
        acc_ref = refs[-1]
        kk = pl.program_id(2)

        @pl.when(kk == 0)
        def _():
            acc_ref[...] = prods[0]

        @pl.when(jnp.logical_and(kk > 0, kk < nk - 1))
        def _():
            acc_ref[...] += prods[0]

        @pl.when(kk == nk - 1)
        def _():
            epilogue([acc_ref[...] + prods[0]], in_refs, out_refs, i)

    return pl.pallas_call(
        body, grid=(n // tn, m // tm, nk), in_specs=[s for _, s in inputs], out_specs=[s for _, s in outs],
        out_shape=tuple(sh for sh, _ in outs), scratch_shapes=[pltpu.VMEM((tm, tn), F32)] if nk > 1 else [],
        compiler_params=_params(), name=name)(*[a for a, _ in inputs])


def _blk(rows, width, col=None, row=None):
    return pl.BlockSpec((rows, width), lambda j, i, kk: ((i if row is None else row(i)), (0 if col is None else col(j, kk))))


FF_TN = D_FF // 2


def _ffn_weight_rows(wg_t, wu_t):
    return jnp.concatenate([wg_t[0:FF_TN], wu_t[0:FF_TN], wg_t[FF_TN:], wu_t[FF_TN:]], axis=0)


def _ffn_in(hn, wgu_t, name):
    tp = hn.shape[0]
    tm = TM

    def epilogue(prods, in_refs, out_refs, i):
        g, u = prods
        out_refs[0][:, 0:FF_TN] = g.astype(BF)
        out_refs[0][:, FF_TN:2 * FF_TN] = u.astype(BF)
        out_refs[1][...] = (g * _sigmoid(g) * u).astype(BF)

    wspec = lambda off: pl.BlockSpec((FF_TN, D), lambda j, i, kk: (2 * j + off, 0))
    return _mm_fused(
        [(hn, _blk(tm, D)), (wgu_t, wspec(0)), (wgu_t, wspec(1))], [(0, 1), (0, 2)], nt=True, m=tp, n=D_FF, tm=tm, tn=FF_TN,
        outs=[(jax.ShapeDtypeStruct((tp, 2 * D_FF), BF), _blk(tm, 2 * FF_TN, lambda j, kk: j)),
              (jax.ShapeDtypeStruct((tp, D_FF), BF), _blk(tm, FF_TN, lambda j, kk: j))],
        epilogue=epilogue, name=name)


def _ffn_down_loss(ff, wdown, h1, target, gf, name):
    tp = ff.shape[0]

    def epilogue(prods, in_refs, out_refs, i):
        live = (i > 0).astype(F32)
        g = in_refs[4][...]
        x = prods[0] + in_refs[2][...]
        r = lax.rsqrt(jnp.mean(x * x, axis=1, keepdims=True) + EPS)
        xh = x * r
        e = xh * g - in_refs[3][...]
        loss_part = 0.5 * live * jnp.sum(jnp.mean(e * e, axis=1, keepdims=True), axis=0, keepdims=True)
        dout = e * (live / D)
        dg_part = jnp.sum(dout * xh, axis=0, keepdims=True)
        dxh = dout * g
        out_refs[0][...] = r * (dxh - xh * jnp.mean(dxh * xh, axis=1, keepdims=True))

        @pl.when(i == 0)
        def _():
            out_refs[1][...] = loss_part
            out_refs[2][...] = dg_part

        @pl.when(i > 0)
        def _():
            out_refs[1][...] += loss_part
            out_refs[2][...] += dg_part

    const = lambda shape: pl.BlockSpec(shape, lambda j, i, kk: (0,) * len(shape))
    return _mm_fused(
        [(ff, _blk(TM, D_FF)), (wdown, const((D_FF, D))), (h1, _blk(TM, D)),
         (target, _blk(TM, D, row=lambda i: jnp.maximum(i - 1, 0))), (gf, const((1, D)))],
        [(0, 1)], nt=False, m=tp, n=D, tm=TM, tn=D,
        outs=[(jax.ShapeDtypeStruct((tp, D), F32), _blk(TM, D)), (jax.ShapeDtypeStruct((1, 1), F32), const((1, 1))),
              (jax.ShapeDtypeStruct((1, D), F32), const((1, D)))],
        epilogue=epilogue, name=name)


def _ffn_d_hidden(dh2, wdown, gu, name):
    tp = dh2.shape[0]

    def epilogue(prods, in_refs, out_refs, i):
        d = prods[0]
        g = in_refs[2][:, 0:FF_TN].astype(F32)
        u = in_refs[2][:, FF_TN:2 * FF_TN].astype(F32)
        sg = _sigmoid(g)
        out_refs[0][:, 0:FF_TN] = (d * u * sg * (1.0 + g * (1.0 - sg))).astype(BF)
        out_refs[0][:, FF_TN:2 * FF_TN] = (d * g * sg).astype(BF)

    return _mm_fused(
        [(dh2, _blk(TM, D)), (wdown, pl.BlockSpec((FF_TN, D), lambda j, i, kk: (j, 0))), (gu, _blk(TM, 2 * FF_TN, lambda j, kk: j))],
        [(0, 1)], nt=True, m=tp, n=D_FF, tm=TM, tn=FF_TN,
        outs=[(jax.ShapeDtypeStruct((tp, 2 * D_FF), BF), _blk(TM, 2 * FF_TN, lambda j, kk: j))],
        epilogue=epilogue, name=name)[0]


def _ffn_d_in(dgu, wgu_t, h1, rstd, g2, dh2, name):
    tp = dgu.shape[0]
    nk = 2

    def epilogue(prods, in_refs, out_refs, i):
        r = in_refs[3][...]
        xh = in_refs[2][...] * r
        dxn = prods[0]
        dxh = dxn * in_refs[4][...]
        out_refs[0][...] = r * (dxh - xh * jnp.mean(dxh * xh, axis=1, keepdims=True)) + in_refs[5][...]
        part = jnp.sum(dxn * xh, axis=0, keepdims=True)

        @pl.when(i == 0)
        def _():
            out_refs[1][...] = part

        @pl.when(i > 0)
        def _():
            out_refs[1][...] += part

    const = lambda shape: pl.BlockSpec(shape, lambda j, i, kk: (0,) * len(shape))
    return _mm_fused(
        [(dgu, pl.BlockSpec((TM, D_FF), lambda j, i, kk: (i, kk))), (wgu_t, pl.BlockSpec((D_FF, D), lambda j, i, kk: (kk, 0))),
         (h1, _blk(TM, D)), (rstd, _blk(TM, 1)), (g2, const((1, D))), (dh2, _blk(TM, D))],
        [(0, 1)], nt=False, m=tp, n=D, tm=TM, tn=D, nk=nk,
        outs=[(jax.ShapeDtypeStruct((tp, D), F32), _blk(TM, D)), (jax.ShapeDtypeStruct((1, D), F32), const((1, D)))],
        epilogue=epilogue, name=name)


def _branch_merge(y_m, y_g, wbm, wbg, pbig, name):
    tp = y_m.shape[0]

    def epilogue(prods, in_refs, out_refs, i):
        pm, pg = prods[0].astype(BF), prods[1].astype(BF)
        out_refs[0][...] = pm
        out_refs[1][...] = pg
        out_refs[2][...] = (_sigmoid(in_refs[4][...].astype(F32)) * pm.astype(F32)
                            + _sigmoid(in_refs[5][...].astype(F32)) * pg.astype(F32)).astype(BF)

    const = lambda shape: pl.BlockSpec(shape, lambda j, i, kk: (0,) * len(shape))
    shp = jax.ShapeDtypeStruct((tp, D), BF)
    return _mm_fused(
        [(y_m, _blk(TM, D)), (wbm, const((D, D))), (y_g, _blk(TM, D)), (wbg, const((D, D))),
         (pbig, _blk(TM, D, lambda j, kk: CB_GM)), (pbig, _blk(TM, D, lambda j, kk: CB_GG))],
        [(0, 1), (2, 3)], nt=False, m=tp, n=D, tm=TM, tn=D,
        outs=[(shp, _blk(TM, D)), (shp, _blk(TM, D)), (shp, _blk(TM, D))], epilogue=epilogue, name=name)


def _merge_d(dh1, wout, pm, pg, pbig, name):
    tp = dh1.shape[0]

    def epilogue(prods, in_refs, out_refs, i):
        d = prods[0]
        sm = _sigmoid(in_refs[4][...].astype(F32))
        sg = _sigmoid(in_refs[5][...].astype(F32))
        out_refs[0][...] = (d * sm).astype(BF)
        out_refs[1][...] = (d * sg).astype(BF)
        out_refs[2][:, 0:D] = (d * in_refs[2][...].astype(F32) * sm * (1.0 - sm)).astype(BF)
        out_refs[2][:, D:2 * D] = (d * in_refs[3][...].astype(F32) * sg * (1.0 - sg)).astype(BF)

    const = lambda shape: pl.BlockSpec(shape, lambda j, i, kk: (0,) * len(shape))
    shp = jax.ShapeDtypeStruct((tp, D), BF)
    return _mm_fused(
        [(dh1, _blk(TM, D)), (wout, const((D, D))), (pm, _blk(TM, D)), (pg, _blk(TM, D)),
         (pbig, _blk(TM, D, lambda j, kk: CB_GM)), (pbig, _blk(TM, D, lambda j, kk: CB_GG))],
        [(0, 1)], nt=True, m=tp, n=D, tm=TM, tn=D,
        outs=[(shp, _blk(TM, D)), (shp, _blk(TM, D)),
              (jax.ShapeDtypeStruct((tp, N_ALL), BF), _blk(TM, 2 * D, lambda j, kk: CB_GM // 2))],
        epilogue=epilogue, name=name)


def _out_proj_norm(merged, wout, h0, g2, name):
    tp = merged.shape[0]
    tm = _mm_rows(tp)

    def epilogue(prods, in_refs, out_refs, i):
        x = prods[0] + in_refs[2][...]
        r = lax.rsqrt(jnp.mean(x * x, axis=1, keepdims=True) + EPS)
        out_refs[0][...] = x
        out_refs[1][...] = (x * r * in_refs[3][...]).astype(BF)
        out_refs[2][...] = r

    const = lambda shape: pl.BlockSpec(shape, lambda j, i, kk: (0,) * len(shape))
    return _mm_fused(
        [(merged, _blk(tm, D)), (wout, const((D, D))), (h0, _blk(tm, D)), (g2, const((1, D)))],
        [(0, 1)], nt=False, m=tp, n=D, tm=tm, tn=D,
        outs=[(jax.ShapeDtypeStruct((tp, D), F32), _blk(tm, D)), (jax.ShapeDtypeStruct((tp, D), BF), _blk(tm, D)),
              (jax.ShapeDtypeStruct((tp, 1), F32), _blk(tm, 1))],
        epilogue=epilogue, name=name)


def _adamw(w, g, m, v, name):
    rows, cols = w.shape
    tr = 128 if rows % 128 == 0 else rows

    def body(w_ref, g_ref, m_ref, v_ref, d_ref, nm_ref, nv_ref):
        gv = g_ref[...]
        nm = ADAM_B1 * m_ref[...] + (1.0 - ADAM_B1) * gv
        nv = ADAM_B2 * v_ref[...] + (1.0 - ADAM_B2) * (gv * gv)
        m_hat = nm / (1.0 - ADAM_B1 ** ADAM_STEP)
        v_hat = nv / (1.0 - ADAM_B2 ** ADAM_STEP)
        d_ref[...] = -ADAM_LR * (m_hat / (jnp.sqrt(v_hat) + ADAM_EPS) + ADAM_WD * w_ref[...])
        nm_ref[...] = nm
        nv_ref[...] = nv

    spec = pl.BlockSpec((tr, cols), lambda i: (i, 0))
    shp = jax.ShapeDtypeStruct((rows, cols), F32)
    return pl.pallas_call(body, grid=(rows // tr,), in_specs=[spec] * 4, out_specs=[spec] * 3,
                          out_shape=(shp,) * 3, compiler_params=_params(), name=name)(w, g, m, v)


def _place_small(dsmall, dproj, name):
    tp = dsmall.shape[0]

    def body(s_ref, _, o_ref):
        o_ref[...] = s_ref[...]

    return pl.pallas_call(
        body, grid=(tp // TM,), in_specs=[_row_spec(N_SMALL), ANY], out_specs=_row_spec(N_SMALL, N_BIG // N_SMALL),
        out_shape=jax.ShapeDtypeStruct(dproj.shape, dproj.dtype), input_output_aliases={1: 0},
        compiler_params=_params(), name=name)(dsmall, dproj)


def _row_tile(rows, cap=512):
    best = rows
    for cand in range(8, min(rows, cap) + 1, 8):
        if rows % cand == 0:
            best = cand
    return best


def _add2(a, b, out_dtype, name):
    rows, cols = a.shape
    tr = _row_tile(rows)

    def body(a_ref, b_ref, o_ref):
        o_ref[...] = (a_ref[...] + b_ref[...]).astype(o_ref.dtype)

    spec = pl.BlockSpec((tr, cols), lambda i: (i, 0))
    return pl.pallas_call(body, grid=(rows // tr,), in_specs=[spec] * 2, out_specs=spec,
                          out_shape=jax.ShapeDtypeStruct((rows, cols), out_dtype), compiler_params=_params(), name=name)(a, b)


def _add4(first, rest, name):
    rows, cols = first.shape
    tr = _row_tile(rows, 256)

    def body(f_ref, r_ref, o_ref):
        up = lambda v: v.astype(F32)
        o_ref[...] = ((up(f_ref[...]) + up(r_ref[0])) + up(r_ref[1])) + up(r_ref[2])

    return pl.pallas_call(body, grid=(rows // tr,),
                          in_specs=[pl.BlockSpec((tr, cols), lambda i: (i, 0)), pl.BlockSpec((3, tr, cols), lambda i: (0, i, 0))],
                          out_specs=pl.BlockSpec((tr, cols), lambda i: (i, 0)),
                          out_shape=jax.ShapeDtypeStruct((rows, cols), F32), compiler_params=_params(), name=name)(first, rest)


def _chunk_consts():
    r2 = lax.broadcasted_iota(jnp.int32, (CHUNK, CHUNK), 0)
    c2 = lax.broadcasted_iota(jnp.int32, (CHUNK, CHUNK), 1)
    tri = r2 >= c2
    return dict(tri=tri, tril_f=tri.astype(F32), triu_f=(r2 <= c2).astype(F32),
                lane=lax.broadcasted_iota(jnp.int32, (CHUNK, N_SMALL), 1),
                rowio=lax.broadcasted_iota(jnp.int32, (CHUNK, 1), 0),
                ones=jnp.ones((CHUNK, N_SMALL), F32))


def _valid_rows(block, c):
    row = block * TM + c * CHUNK + lax.broadcasted_iota(jnp.int32, (CHUNK, 1), 0)
    return row >= FIRST_VALID


def _col(x, lane, idx):
    return jnp.sum(jnp.where(lane == idx, x, 0.0), axis=1, keepdims=True)


def _last_row(x, rowio):
    return jnp.sum(jnp.where(rowio == CHUNK - 1, x, 0.0), axis=0, keepdims=True)


def _sum_all(x):
    return jnp.sum(jnp.sum(x, axis=1, keepdims=True), axis=0, keepdims=True)


def _headnorm_fwd(hm, gain, gate_act):
    rs = lax.rsqrt(jnp.mean(hm * hm, axis=1, keepdims=True) + EPS)
    return hm * rs * gain * gate_act


def _headnorm_bwd(dy, hm, gain, gate_act):
    rs = lax.rsqrt(jnp.mean(hm * hm, axis=1, keepdims=True) + EPS)
    xh = hm * rs
    dact = dy * xh * gain
    dgain = jnp.sum(dy * gate_act * xh, axis=0, keepdims=True)
    dxh = dy * gate_act * gain
    dhm = rs * (dxh - xh * jnp.mean(dxh * xh, axis=1, keepdims=True))
    return dhm, dact, dgain


def _mlstm_gates(sm, gbias, valid, k):
    pre = sm + gbias
    lf = jnp.where(valid, _logsig(pre), 0.0)
    b_all = _nn(k["tril_f"], lf, precision=HI)
    li_all = jnp.where(valid, pre, NEG)
    return pre, li_all, b_all


def _mlstm_open(h, qh, kh, c_st, li_all, b_all, k):
    lane = k["lane"]
    sel = jnp.where(lane == h, 1.0, 0.0) - jnp.where(lane == NH + h, 1.0, 0.0)
    x = jnp.where(lane < NH, li_all, jnp.where(lane < 2 * NH, b_all, 0.0))
    cb = c_st.astype(BF)
    return dict(ubc=_nt(sel, x, precision=HI), sim=_nt(qh, kh), cb=cb, cq=_nt(qh, cb))


def _mlstm_weights(h, f, qh, vh, li_all, b_all, n_row, m11, k):
    lane, tri, rowio = k["lane"], k["tri"], k["rowio"]
    b_col = _col(b_all, lane, NH + h)
    li_col = _col(li_all, lane, h)
    dmat = jnp.where(tri, b_col + f["ubc"], NEG)
    m_row = jnp.maximum(b_col + m11, jnp.max(dmat, axis=1, keepdims=True))
    e = jnp.exp(dmat - m_row)
    w_mat = e * f["sim"]
    a = jnp.exp(b_col + m11 - m_row)
    qf = qh.astype(F32)
    nq = jnp.sum(qf * n_row, axis=1, keepdims=True)
    g = _last_row(b_col, rowio)
    wlog = g - b_col + li_col
    m_new = jnp.maximum(g + m11, jnp.max(wlog, axis=0, keepdims=True))
    a_s = jnp.exp(g + m11 - m_new)
    w = jnp.exp(wlog - m_new)
    return dict(f, e=e, w_mat=w_mat, a=a, qf=qf, nq=nq, m_row=m_row, m_new=m_new, a_s=a_s, w=w,
                wv=_nn(w_mat.astype(BF), vh))


def _mlstm_out(f):
    num = f["a"] * f["cq"] + f["wv"]
    den = f["a"] * f["nq"] + jnp.sum(f["w_mat"], axis=1, keepdims=True)
    floor = jnp.exp(-f["m_row"])
    r = jnp.maximum(jnp.abs(den), floor)
    return dict(f, den=den, floor=floor, r=r, hm=num / r)


def _mlstm_fwd(qk, pbig, small, gbias, headg, name):
    tp = qk.shape[0]
    nb = tp // TM

    def body(qk_ref, v_ref, mo_ref, sm_ref, gb_ref, hg_ref, y_ref, cs_ref, ns_ref, c_scr, n_scr):
        blk = pl.program_id(0)

        @pl.when(blk == 0)
        def _():
            c_scr[...] = jnp.zeros_like(c_scr)
            n_scr[...] = jnp.zeros_like(n_scr)

        k = _chunk_consts()
        io8 = lax.broadcasted_iota(jnp.int32, (8, DQK), 0)

        def chunk(c, carry):
            r0 = pl.multiple_of(c * CHUNK, CHUNK)
            rows = pl.ds(r0, CHUNK)
            valid = _valid_rows(blk, c)
            _, li_all, b_all = _mlstm_gates(sm_ref[rows, :], gb_ref[...], valid, k)
            heads = range(NH)
            qs = [qk_ref[rows, h * DQK:(h + 1) * DQK] for h in heads]
            ks = [qk_ref[rows, NH * DQK + h * DQK:NH * DQK + (h + 1) * DQK] for h in heads]
            vs = [v_ref[rows, h * DV:(h + 1) * DV] for h in heads]
            cst = [c_scr[h] for h in heads]
            nrow = [n_scr[h, 0:1, :] for h in heads]
            m11 = [jnp.max(n_scr[h, 1:2, :], axis=1, keepdims=True) for h in heads]
            f = [_mlstm_open(h, qs[h], ks[h], cst[h], li_all, b_all, k) for h in heads]
            f = [_mlstm_weights(h, f[h], qs[h], vs[h], li_all, b_all, nrow[h], m11[h], k) for h in heads]
            wk = [f[h]["w"] * ks[h].astype(F32) for h in heads]
            kv = [_tn(vs[h], wk[h].astype(BF)) for h in heads]
            for h in heads:
                hm = _mlstm_out(f[h])["hm"]
                gate = _sigmoid(mo_ref[rows, h * DV:(h + 1) * DV].astype(F32))
                y_ref[rows, h * DV:(h + 1) * DV] = _headnorm_fwd(hm, hg_ref[:, h * DV:(h + 1) * DV], gate).astype(BF)
                cs_ref[c, h] = f[h]["cb"]
                ns_ref[c, h] = jnp.where(io8 == 0, nrow[h], jnp.where(io8 == 1, m11[h], 0.0))
                c_scr[h] = f[h]["a_s"] * cst[h] + kv[h]
                n_scr[h, 0:1, :] = f[h]["a_s"] * nrow[h] + jnp.sum(wk[h], axis=0, keepdims=True)
                n_scr[h, 1:2, :] = jnp.broadcast_to(f[h]["m_new"], (1, DQK))
            return carry

        lax.fori_loop(0, CPB, chunk, 0, unroll=2)

    return pl.pallas_call(
        body, grid=(nb,),
        in_specs=[_row_spec(D), _row_spec(D, CB_MV), _row_spec(D, CB_MO), _row_spec(N_SMALL), _full_spec((1, N_SMALL)), _full_spec((1, D))],
        out_specs=[_row_spec(D), pl.BlockSpec((CPB, NH, DV, DQK), lambda i: (i, 0, 0, 0)),
                   pl.BlockSpec((CPB, NH, 8, DQK), lambda i: (i, 0, 0, 0))],
        out_shape=(jax.ShapeDtypeStruct((tp, D), BF), jax.ShapeDtypeStruct((tp // CHUNK, NH, DV, DQK), BF),
                   jax.ShapeDtypeStruct((tp // CHUNK, NH, 8, DQK), F32)),
        scratch_shapes=[pltpu.VMEM((NH, DV, DQK), F32), pltpu.VMEM((NH, 8, DQK), F32)],
        compiler_params=_params(), name=name)(qk, pbig, pbig, small, gbias, headg)


def _mlstm_bwd(dy, qk, pbig, small, gbias, headg, cs, ns, dproj, name):
    tp = qk.shape[0]
    nb = tp // TM

    def body(dy_ref, qk_ref, v_ref, mo_ref, sm_ref, gb_ref, hg_ref, cs_ref, ns_ref, _,
             dqk_ref, dproj_ref, dsm_ref, dgb_ref, dhg_ref, dc_scr, dn_scr):
        step = pl.program_id(0)
        blk = nb - 1 - step

        @pl.when(step == 0)
        def _():
            dc_scr[...] = jnp.zeros_like(dc_scr)
            dn_scr[...] = jnp.zeros_like(dn_scr)
            dgb_ref[...] = jnp.zeros_like(dgb_ref)
            dhg_ref[...] = jnp.zeros_like(dhg_ref)

        k = _chunk_consts()
        lane, rowio = k["lane"], k["rowio"]

        def chunk(cc, carry):
            c = CPB - 1 - cc
            r0 = pl.multiple_of(c * CHUNK, CHUNK)
            rows = pl.ds(r0, CHUNK)
            valid = _valid_rows(blk, c)
            pre, li_all, b_all = _mlstm_gates(sm_ref[rows, :], gb_ref[...], valid, k)
            dli_all = jnp.zeros((CHUNK, N_SMALL), F32)
            db_all = jnp.zeros((CHUNK, N_SMALL), F32)
            heads = range(NH)
            qs = [qk_ref[rows, h * DQK:(h + 1) * DQK] for h in heads]
            ks = [qk_ref[rows, NH * DQK + h * DQK:NH * DQK + (h + 1) * DQK] for h in heads]
            vs = [v_ref[rows, h * DV:(h + 1) * DV] for h in heads]
            cst = [cs_ref[c, h].astype(F32) for h in heads]
            nrow = [ns_ref[c, h, 0:1, :] for h in heads]
            m11 = [jnp.max(ns_ref[c, h, 1:2, :], axis=1, keepdims=True) for h in heads]
            f = [_mlstm_open(h, qs[h], ks[h], cst[h], li_all, b_all, k) for h in heads]
            f = [_mlstm_weights(h, f[h], qs[h], vs[h], li_all, b_all, nrow[h], m11[h], k) for h in heads]
            f = [_mlstm_out(f[h]) for h in heads]
            t = []
            for h in heads:
                gain = hg_ref[:, h * DV:(h + 1) * DV]
                gate = _sigmoid(mo_ref[rows, h * DV:(h + 1) * DV].astype(F32))
                dhm, dgate, dgain = _headnorm_bwd(dy_ref[rows, h * DV:(h + 1) * DV].astype(F32), f[h]["hm"], gain, gate)
                dproj_ref[rows, D + h * DV:D + (h + 1) * DV] = (dgate * gate * (1.0 - gate)).astype(BF)
                dhg_ref[:, h * DV:(h + 1) * DV] += dgain
                r, den = f[h]["r"], f[h]["den"]
                dnum = dhm / r
                dr = -jnp.sum(dhm * f[h]["hm"], axis=1, keepdims=True) / r
                dden = jnp.where(jnp.abs(den) > f[h]["floor"], dr * jnp.sign(den), 0.0)
                dnb = dnum.astype(BF)
                dc_new = dc_scr[h]
                dcb = dc_new.astype(BF)
                t.append(dict(dnum=dnum, dden=dden, dnb=dnb, dc_new=dc_new, dn_new=dn_scr[h],
                              dwm=_nt(dnb, vs[h]), vdc=_nn(vs[h], dcb), kdc=_nt(ks[h], dcb)))
            for h in heads:
                dw_mat = t[h]["dwm"] + t[h]["dden"]
                dsim = (f[h]["e"] * dw_mat).astype(BF)
                gm = f[h]["w_mat"] * dw_mat
                t[h].update(gm=gm, dv0=_tn(f[h]["w_mat"].astype(BF), t[h]["dnb"]), dq0=_nn(dsim, ks[h]),
                            dq1=_nn(t[h]["dnb"], f[h]["cb"]), dk0=_tn(dsim, qs[h]),
                            dcq=_tn((f[h]["a"] * t[h]["dnum"]).astype(BF), qs[h]), cs2=_tn(gm, k["ones"], precision=HI))
            for h in heads:
                a, w, a_s = f[h]["a"], f[h]["w"], f[h]["a_s"]
                dnum, dden, dc_new, dn_new, vdc, gm = (t[h][n] for n in ("dnum", "dden", "dc_new", "dn_new", "vdc", "gm"))
                kf = ks[h].astype(F32)
                dproj_ref[rows, h * DV:(h + 1) * DV] = (t[h]["dv0"] + w * t[h]["kdc"]).astype(BF)
                adden = a * dden
                dqk_ref[rows, h * DQK:(h + 1) * DQK] = t[h]["dq0"] + a * t[h]["dq1"] + adden * nrow[h]
                dqk_ref[rows, NH * DQK + h * DQK:NH * DQK + (h + 1) * DQK] = t[h]["dk0"] + w * vdc + w * dn_new
                da = jnp.sum(dnum * f[h]["cq"], axis=1, keepdims=True) + dden * f[h]["nq"]
                dw = jnp.sum(vdc * kf, axis=1, keepdims=True) + jnp.sum(kf * dn_new, axis=1, keepdims=True)
                da_s = _sum_all(dc_new * cst[h]) + jnp.sum(dn_new * nrow[h], axis=1, keepdims=True)
                wdw = w * dw
                rs = jnp.sum(gm, axis=1, keepdims=True)
                cs_col = _col(t[h]["cs2"], lane, 0)
                dg = a_s * da_s + jnp.sum(wdw, axis=0, keepdims=True)
                db = a * da + rs - cs_col - wdw + jnp.where(rowio == CHUNK - 1, dg, 0.0)
                dli_all = dli_all + jnp.where(lane == h, cs_col + wdw, 0.0)
                db_all = db_all + jnp.where(lane == NH + h, db, 0.0)
                dc_scr[h] = a_s * dc_new + t[h]["dcq"]
                dn_scr[h] = a_s * dn_new + jnp.sum(adden * f[h]["qf"], axis=0, keepdims=True)
            dlf_all = _nn(k["triu_f"], db_all, precision=HI)
            dsm = jnp.where(valid, dli_all + dlf_all * _sigmoid(-pre), 0.0)
            dsm = jnp.where(lane < 2 * NH, dsm, 0.0)
            dsm_ref[rows, :] = dsm
            dgb_ref[0:1, :] += jnp.sum(dsm, axis=0, keepdims=True)
            return carry

        lax.fori_loop(0, CPB, chunk, 0, unroll=2)

    rev = lambda col: (lambda i: (nb - 1 - i, col))
    rspec = lambda width, col=0: pl.BlockSpec((TM, width), rev(col))
    return pl.pallas_call(
        body, grid=(nb,),
        in_specs=[rspec(D), rspec(D), rspec(D, CB_MV), rspec(D, CB_MO), rspec(N_SMALL), _full_spec((1, N_SMALL)), _full_spec((1, D)),
                  pl.BlockSpec((CPB, NH, DV, DQK), lambda i: (nb - 1 - i, 0, 0, 0)),
                  pl.BlockSpec((CPB, NH, 8, DQK), lambda i: (nb - 1 - i, 0, 0, 0)), ANY],
        out_specs=[rspec(D), rspec(2 * D, CB_MV // 2), rspec(N_SMALL), _full_spec((8, N_SMALL)), _full_spec((1, D))],
        out_shape=(jax.ShapeDtypeStruct((tp, D), F32), jax.ShapeDtypeStruct(dproj.shape, BF),
                   jax.ShapeDtypeStruct((tp, N_SMALL), F32), jax.ShapeDtypeStruct((8, N_SMALL), F32),
                   jax.ShapeDtypeStruct((1, D), F32)),
        scratch_shapes=[pltpu.VMEM((NH, DV, DQK), F32), pltpu.VMEM((NH, 1, DQK), F32)],
        input_output_aliases={9: 1}, compiler_params=_params(), name=name)(dy, qk, pbig, pbig, small, gbias, headg, cs, ns, dproj)


def _gla_loga(sm_ref, a2_ref, a2b_ref, blk):
    za = _nn(sm_ref[...].astype(BF), a2_ref[...]) + a2b_ref[...]
    row = blk * TM + lax.broadcasted_iota(jnp.int32, (TM, 1), 0)
    return za, jnp.where(row >= FIRST_VALID, _logsig(za) / G_TAU, 0.0)


def _gla_head(h, q_ref, k_ref, rows, bc, btot, k):
    sl = slice(h * DQK, (h + 1) * DQK)
    bch = bc[:, sl]
    bth = btot[:, sl]
    gq = q_ref[rows, h * DQK:(h + 1) * DQK].astype(F32)
    gk = k_ref[rows, NH * DQK + h * DQK:NH * DQK + (h + 1) * DQK].astype(F32)
    e_pos = jnp.exp(bch) * (DQK ** -0.5)
    e_neg = jnp.exp(-bch)
    e_end = jnp.exp(bth - bch)
    qd = gq * e_pos
    ki = gk * e_neg
    ke = gk * e_end
    att = jnp.where(k["tri"], _nt(qd.astype(BF), ki.astype(BF)), 0.0)
    return dict(e_pos=e_pos, e_neg=e_neg, e_end=e_end, qd=qd, ki=ki, ke=ke, att=att, decay=jnp.exp(bth))


def _gla_fwd(pbig, small, a2p, a2b, headg, name):
    tp = pbig.shape[0]
    nb = tp // TM

    def body(qk_ref, v_ref, gr_ref, sm_ref, a2_ref, a2b_ref, hg_ref, y_ref, ss_ref, s_scr, lg_scr):
        blk = pl.program_id(0)

        @pl.when(blk == 0)
        def _():
            s_scr[...] = jnp.zeros_like(s_scr)

        k = _chunk_consts()
        _, loga = _gla_loga(sm_ref, a2_ref, a2b_ref, blk)
        lg_scr[...] = loga

        def chunk(c, carry):
            r0 = pl.multiple_of(c * CHUNK, CHUNK)
            rows = pl.ds(r0, CHUNK)
            bc = _nn(k["tril_f"], lg_scr[rows, :], precision=HI)
            btot = _last_row(bc, k["rowio"])
            heads = range(NH)
            f = [_gla_head(h, qk_ref, qk_ref, rows, bc, btot, k) for h in heads]
            vs = [v_ref[rows, h * DV:(h + 1) * DV] for h in heads]
            sst = [s_scr[h] for h in heads]
            sbs = [s.astype(BF) for s in sst]
            inter = [_nt(f[h]["qd"].astype(BF), sbs[h]) for h in heads]
            intra = [_nn(f[h]["att"].astype(BF), vs[h]) for h in heads]
            kv = [_tn(vs[h], f[h]["ke"].astype(BF)) for h in heads]
            for h in heads:
                gr = gr_ref[rows, h * DV:(h + 1) * DV].astype(F32)
                y_ref[rows, h * DV:(h + 1) * DV] = _headnorm_fwd(intra[h] + inter[h], hg_ref[:, h * DV:(h + 1) * DV],
                                                                   gr * _sigmoid(gr)).astype(BF)
                ss_ref[c, h] = sbs[h]
                s_scr[h] = sst[h] * f[h]["decay"] + kv[h]
            return carry

        lax.fori_loop(0, CPB, chunk, 0, unroll=2)

    return pl.pallas_call(
        body, grid=(nb,),
        in_specs=[_row_spec(D, CB_GQK), _row_spec(D, CB_GV), _row_spec(D, CB_GR), _row_spec(N_SMALL),
                  _full_spec((N_SMALL, NH * DQK)), _full_spec((1, NH * DQK)), _full_spec((1, D))],
        out_specs=[_row_spec(D), pl.BlockSpec((CPB, NH, DV, DQK), lambda i: (i, 0, 0, 0))],
        out_shape=(jax.ShapeDtypeStruct((tp, D), BF), jax.ShapeDtypeStruct((tp // CHUNK, NH, DV, DQK), BF)),
        scratch_shapes=[pltpu.VMEM((NH, DV, DQK), F32), pltpu.VMEM((TM, NH * DQK), F32)],
        compiler_params=_params(), name=name)(pbig, pbig, pbig, small, a2p, a2b, headg)


def _gla_bwd(dy, pbig, small, a2p, a2b, headg, ss, dsm_m, dproj, name):
    tp = pbig.shape[0]
    nb = tp // TM
    nqk = NH * DQK

    def body(dy_ref, qk_ref, v_ref, gr_ref, sm_ref, a2_ref, a2b_ref, hg_ref, ss_ref, dsmm_ref, _,
             dproj_ref, dsm_ref, da2_ref, da2b_ref, dhg_ref, ds_scr, lg_scr, dza_scr):
        step = pl.program_id(0)
        blk = nb - 1 - step

        @pl.when(step == 0)
        def _():
            ds_scr[...] = jnp.zeros_like(ds_scr)
            da2_ref[...] = jnp.zeros_like(da2_ref)
            da2b_ref[...] = jnp.zeros_like(da2b_ref)
            dhg_ref[...] = jnp.zeros_like(dhg_ref)

        k = _chunk_consts()
        rowio = k["rowio"]
        za, loga = _gla_loga(sm_ref, a2_ref, a2b_ref, blk)
        lg_scr[...] = loga

        def chunk(cc, carry):
            c = CPB - 1 - cc
            r0 = pl.multiple_of(c * CHUNK, CHUNK)
            rows = pl.ds(r0, CHUNK)
            bc = _nn(k["tril_f"], lg_scr[rows, :], precision=HI)
            btot = _last_row(bc, rowio)
            heads = range(NH)
            f = [_gla_head(h, qk_ref, qk_ref, rows, bc, btot, k) for h in heads]
            vs = [v_ref[rows, h * DV:(h + 1) * DV] for h in heads]
            sbs = [ss_ref[c, h] for h in heads]
            qdb = [f[h]["qd"].astype(BF) for h in heads]
            attb = [f[h]["att"].astype(BF) for h in heads]
            inter = [_nt(qdb[h], sbs[h]) for h in heads]
            intra = [_nn(attb[h], vs[h]) for h in heads]
            dsn = [ds_scr[h] for h in heads]
            dsb = [d.astype(BF) for d in dsn]
            dke = [_nn(vs[h], dsb[h]) for h in heads]
            dv1 = [_nt(f[h]["ke"].astype(BF), dsb[h]) for h in heads]
            t = []
            for h in heads:
                gr = gr_ref[rows, h * DV:(h + 1) * DV].astype(F32)
                sg = _sigmoid(gr)
                gain = hg_ref[:, h * DV:(h + 1) * DV]
                do, dact, dgain = _headnorm_bwd(dy_ref[rows, h * DV:(h + 1) * DV].astype(F32), intra[h] + inter[h], gain, gr * sg)
                dproj_ref[rows, 2 * D + h * DV:2 * D + (h + 1) * DV] = (dact * sg * (1.0 + gr * (1.0 - sg))).astype(BF)
                dhg_ref[:, h * DV:(h + 1) * DV] += dgain
                dob = do.astype(BF)
                t.append(dict(dob=dob, datt=_nt(dob, vs[h]), dv0=_tn(attb[h], dob), dq1=_nn(dob, sbs[h]), dsq=_tn(dob, qdb[h])))
            for h in heads:
                datt = jnp.where(k["tri"], t[h]["datt"], 0.0).astype(BF)
                t[h].update(dq0=_nn(datt, f[h]["ki"].astype(BF)), dki=_tn(datt, qdb[h]))
            dbc_parts = []
            for h in heads:
                dqd = t[h]["dq0"] + t[h]["dq1"]
                dki = t[h]["dki"]
                dproj_ref[rows, D + h * DV:D + (h + 1) * DV] = (t[h]["dv0"] + dv1[h]).astype(BF)
                dproj_ref[rows, h * DQK:(h + 1) * DQK] = (dqd * f[h]["e_pos"]).astype(BF)
                dproj_ref[rows, nqk + h * DQK:nqk + (h + 1) * DQK] = (dki * f[h]["e_neg"] + dke[h] * f[h]["e_end"]).astype(BF)
                dke_ke = dke[h] * f[h]["ke"]
                dbtot = (jnp.sum(dke_ke, axis=0, keepdims=True)
                         + jnp.sum(dsn[h] * sbs[h].astype(F32), axis=0, keepdims=True) * f[h]["decay"])
                dbc_parts.append(dqd * f[h]["qd"] - dki * f[h]["ki"] - dke_ke + jnp.where(rowio == CHUNK - 1, dbtot, 0.0))
                ds_scr[h] = dsn[h] * f[h]["decay"] + t[h]["dsq"]
            dbc = jnp.concatenate(dbc_parts, axis=1)
            dza_scr[rows, :] = _nn(k["triu_f"], dbc, precision=HI)
            return carry

        lax.fori_loop(0, CPB, chunk, 0, unroll=2)
        row = blk * TM + lax.broadcasted_iota(jnp.int32, (TM, 1), 0)
        dza = jnp.where(row >= FIRST_VALID, dza_scr[...] * (_sigmoid(-za) / G_TAU), 0.0)
        dzb = dza.astype(BF)
        dsm_ref[...] = (_nt(dzb, a2_ref[...]) + dsmm_ref[...]).astype(BF)
        da2_ref[...] += _tn(sm_ref[...].astype(BF), dzb)
        da2b_ref[...] += jnp.sum(dza, axis=0, keepdims=True)

    rspec = lambda width, col=0: pl.BlockSpec((TM, width), lambda i: (nb - 1 - i, col))
    return pl.pallas_call(
        body, grid=(nb,),
        in_specs=[rspec(D), rspec(D, CB_GQK), rspec(D, CB_GV), rspec(D, CB_GR), rspec(N_SMALL),
                  _full_spec((N_SMALL, nqk)), _full_spec((1, nqk)), _full_spec((1, D)),
                  pl.BlockSpec((CPB, NH, DV, DQK), lambda i: (nb - 1 - i, 0, 0, 0)), rspec(N_SMALL), ANY],
        out_specs=[rspec(3 * D, 0), rspec(N_SMALL), _full_spec((N_SMALL, nqk)), _full_spec((1, nqk)), _full_spec((1, D))],
        out_shape=(jax.ShapeDtypeStruct(dproj.shape, BF),
                   jax.ShapeDtypeStruct((tp, N_SMALL), BF), jax.ShapeDtypeStruct((N_SMALL, nqk), F32),
                   jax.ShapeDtypeStruct((1, nqk), F32), jax.ShapeDtypeStruct((1, D), F32)),
        scratch_shapes=[pltpu.VMEM((NH, DV, DQK), F32), pltpu.VMEM((TM, nqk), F32), pltpu.VMEM((TM, nqk), F32)],
        input_output_aliases={10: 0}, compiler_params=_params(), name=name)(dy, pbig, pbig, pbig, small, a2p, a2b, headg, ss, dsm_m, dproj)


PIECE_BYTES = 1 << 20
MAX_PIECES = 32


def _place():
    return lax.axis_index("x"), lax.axis_index("y"), lax.axis_index("c")


def _piece_rows(rows, row_bytes, align):
    want = min(MAX_PIECES, max(1, -(-rows * row_bytes // PIECE_BYTES)))
    best = rows
    for k in range(1, want + 1):
        if rows % k == 0 and (rows // k) % align == 0:
            best = rows // k
    return best


def _remote(src, dst, send_sems, recv_sems, k, to):
    return pltpu.make_async_remote_copy(src_ref=src, dst_ref=dst, send_sem=send_sems.at[k], recv_sem=recv_sems.at[k],
                                        device_id=to, device_id_type=MESH)


def _all_gather_chips(p, name):
    r, n = p.shape
    rh = r // 2
    align = 32 // p.dtype.itemsize
    assert r % (2 * align) == 0
    cr = _piece_rows(rh, n * p.dtype.itemsize, align)

    def body(p_ref, o_ref, send_sems, recv_sems):
        x, y, c = _place()
        chips = [(1 - x, y), (x, 1 - y), (1 - x, 1 - y)]
        sib = (x, y, 1 - c)

        def half(hc, piece=None):
            if piece is None:
                return pl.ds(pl.multiple_of(hc * rh, align), rh)
            return pl.ds(pl.multiple_of(hc * rh + piece * cr, align), cr)

        first = [_remote(p_ref.at[half(c)], o_ref.at[j, half(c)], send_sems, recv_sems, j, (*chip, c))
                 for j, chip in enumerate(chips)]
        for cp in first:
            cp.start()
        for j, cp in enumerate(first):
            cp.wait_recv()
            for i in range(rh // cr):
                _remote(o_ref.at[j, half(c, i)], o_ref.at[j, half(c, i)], send_sems, recv_sems, 3 + j, sib).start()
        for j in range(3):
            block = _remote(o_ref.at[j, half(c)], o_ref.at[j, half(1 - c)], send_sems, recv_sems, 3 + j, sib)
            block.wait_send()
            block.wait_recv()
        for cp in first:
            cp.wait_send()

    return pl.pallas_call(
        body, in_specs=[ANY], out_specs=ANY, out_shape=jax.ShapeDtypeStruct((3, r, n), p.dtype),
        scratch_shapes=[pltpu.SemaphoreType.DMA((6,)), pltpu.SemaphoreType.DMA((6,))],
        name=name)(p)


def _by_chip(mine, others):
    me = 2 * lax.axis_index("x") + lax.axis_index("y")
    by_mask = jnp.stack([mine, others[1], others[0], others[2]])
    return [lax.dynamic_index_in_dim(by_mask, q ^ me, 0, keepdims=False) for q in range(4)]


def _swap_halves(items, name):
    k = len(items)

    def body(*refs):
        a_refs, got_refs = refs[:k], refs[k:2 * k]
        send_sems, recv_sems = refs[2 * k:]
        x, y, c = _place()
        sib = (x, y, 1 - c)
        for i, a in enumerate(items):
            _, r, n = a.shape
            rh = r // 2
            cr = _piece_rows(rh, n * a.dtype.itemsize, 8)
            for q in range(4):
                for t in range(rh // cr):
                    other = pl.ds(pl.multiple_of((1 - c) * rh + t * cr, 8), cr)
                    _remote(a_refs[i].at[q, other], got_refs[i].at[q, pl.ds(t * cr, cr)], send_sems, recv_sems, i, sib).start()
        for i, a in enumerate(items):
            block = _remote(a_refs[i].at[:, pl.ds(0, a.shape[1] // 2)], got_refs[i], send_sems, recv_sems, i, sib)
            block.wait_send()
            block.wait_recv()

    return pl.pallas_call(
        body, in_specs=[ANY] * k, out_specs=[ANY] * k,
        out_shape=tuple(jax.ShapeDtypeStruct((4, a.shape[1] // 2, a.shape[2]), a.dtype) for a in items),
        scratch_shapes=[pltpu.SemaphoreType.DMA((k,)), pltpu.SemaphoreType.DMA((k,))], name=name)(*items)


def _scatter_chips(items, name):
    k = len(items)

    def body(*refs):
        s_refs, o_refs = refs[:k], refs[k:2 * k]
        send_sems, recv_sems = refs[2 * k:]
        x, y, c = _place()
        chips = [(1 - x, y), (x, 1 - y), (1 - x, 1 - y)]
        sent = []
        for i in range(k):
            for j, (cx, cy) in enumerate(chips):
                cp = _remote(s_refs[i].at[2 * cx + cy], o_refs[i].at[j], send_sems, recv_sems, 3 * i + j, (cx, cy, c))
                cp.start()
                sent.append(cp)
        for cp in sent:
            cp.wait_recv()
        for cp in sent:
            cp.wait_send()

    return pl.pallas_call(
        body, in_specs=[ANY] * k, out_specs=[ANY] * k,
        out_shape=tuple(jax.ShapeDtypeStruct((3,) + s.shape[1:], s.dtype) for s in items),
        scratch_shapes=[pltpu.SemaphoreType.DMA((3 * k,)), pltpu.SemaphoreType.DMA((3 * k,))], name=name)(*items)


def _join_halves(items, name):
    k = len(items)

    def body(*refs):
        f_refs, o_refs = refs[:k], refs[k:2 * k]
        send_sems, recv_sems = refs[2 * k:]
        x, y, c = _place()
        sib = (x, y, 1 - c)
        for i, f in enumerate(items):
            rh, n = f.shape
            cr = _piece_rows(rh, n * f.dtype.itemsize, 8)
            for t in range(rh // cr):
                rows = pl.ds(t * cr, cr)
                _remote(f_refs[i].at[rows], o_refs[i].at[rows], send_sems, recv_sems, i, sib).start()
        for i in range(k):
            block = _remote(f_refs[i], o_refs[i], send_sems, recv_sems, i, sib)
            block.wait_send()
            block.wait_recv()

    return pl.pallas_call(
        body, in_specs=[ANY] * k, out_specs=[ANY] * k, out_shape=tuple(jax.ShapeDtypeStruct(f.shape, f.dtype) for f in items),
        scratch_shapes=[pltpu.SemaphoreType.DMA((k,)), pltpu.SemaphoreType.DMA((k,))], name=name)(*items)


SMALL_ROWS = 16
SMALL_SHARD_SHAPES = [(N_META, 256), (4, 256), (G_RANK, 128), (NH, 64), (NH, 64)]
REPL_SHAPES = [(1, D), (1, D), (1, 2, NH), (1, NH * DQK), (1, D), (D,)]
W_IN_SHARD = 2054


def _pack_small(parts):
    flat = jnp.concatenate([p.reshape(-1) for p in parts])
    return jnp.pad(flat, (0, SMALL_ROWS * D - flat.shape[0])).reshape(SMALL_ROWS, D)


def _unpack_small(block, shapes):
    flat, out, off = block.reshape(-1), [], 0
    for shp in shapes:
        n = 1
        for s in shp:
            n *= s
        out.append(flat[off:off + n].reshape(shp))
        off += n
    return out


def _proj_cols_from_w_in(w_in_f):
    w_big = jnp.concatenate([w_in_f[:, 3080:5128], w_in_f[:, 5144:6168], w_in_f[:, 0:1024], w_in_f[:, 6168:8216],
                             w_in_f[:, 1024:2048], w_in_f[:, 2056:3080]], axis=1)
    w_small = jnp.concatenate([w_in_f[:, 2048:2056], w_in_f[:, 5128:5144], jnp.zeros((D, N_SMALL - 24), w_in_f.dtype)], axis=1)
    return w_big, w_small


def _w_in_from_proj_cols(d_wall):
    big, small = d_wall[:, 0:N_BIG], d_wall[:, N_BIG:N_ALL]
    return jnp.concatenate([big[:, 3072:4096], big[:, 6144:7168], small[:, 0:8], big[:, 7168:8192], big[:, 0:2048],
                            small[:, 8:24], big[:, 2048:3072], big[:, 4096:6144]], axis=1)


def kernel(x, meta_tokens, norm1_g, w_in, conv_w, conv_b, m_gate_b, g_a2, g_a2_b, m_head_g, g_head_g, w_branch_m, w_branch_g, w_out, norm2_g, w_ff_gate, w_ff_up, w_ff_down, final_g, loss_target, m_meta_tokens, m_norm1_g, m_w_in, m_conv_w, m_conv_b, m_m_gate_b, m_g_a2, m_g_a2_b, m_m_head_g, m_g_head_g, m_w_branch_m, m_w_branch_g, m_w_out, m_norm2_g, m_w_ff_gate, m_w_ff_up, m_w_ff_down, m_final_g, v_meta_tokens, v_norm1_g, v_w_in, v_conv_w, v_conv_b, v_m_gate_b, v_g_a2, v_g_a2_b, v_m_head_g, v_g_head_g, v_w_branch_m, v_w_branch_g, v_w_out, v_norm2_g, v_w_ff_gate, v_w_ff_up, v_w_ff_down, v_final_g):
    w = _gather_weights(w_in, w_branch_m, w_branch_g, w_out, w_ff_gate, w_ff_up, w_ff_down, meta_tokens, conv_w, g_a2, m_head_g, g_head_g)
    loss_local, dx, local = _local_step(x[0], loss_target[0], w, norm1_g, conv_b, m_gate_b, g_a2_b, norm2_g, final_g)
    grads = _reduce_grads(local)

    weights = [w_in, w_branch_m, w_branch_g, w_out, w_ff_gate, w_ff_up, w_ff_down, meta_tokens, conv_w, g_a2, m_head_g, g_head_g,
               norm1_g, conv_b, m_gate_b, g_a2_b, norm2_g, final_g]
    moms = [m_w_in, m_w_branch_m, m_w_branch_g, m_w_out, m_w_ff_gate, m_w_ff_up, m_w_ff_down, m_meta_tokens, m_conv_w, m_g_a2,
            m_m_head_g, m_g_head_g, m_norm1_g, m_conv_b, m_m_gate_b, m_g_a2_b, m_norm2_g, m_final_g]
    vels = [v_w_in, v_w_branch_m, v_w_branch_g, v_w_out, v_w_ff_gate, v_w_ff_up, v_w_ff_down, v_meta_tokens, v_conv_w, v_g_a2,
            v_m_head_g, v_g_head_g, v_norm1_g, v_conv_b, v_m_gate_b, v_g_a2_b, v_norm2_g, v_final_g]
    res = {}
    for nm, wt, g, m, v in zip(PACK_ORDER, weights, grads, moms, vels):
        two_d = (wt.size // wt.shape[-1], wt.shape[-1])
        d, nm_, nv_ = _adamw(wt.reshape(two_d), g.reshape(two_d), m.reshape(two_d), v.reshape(two_d), "adamw_" + nm)
        res[nm] = (g.reshape(wt.shape), d.reshape(wt.shape), nm_.reshape(wt.shape), nv_.reshape(wt.shape))

    order = ["meta_tokens", "norm1_g", "w_in", "conv_w", "conv_b", "m_gate_b", "g_a2", "g_a2_b", "m_head_g", "g_head_g",
             "w_branch_m", "w_branch_g", "w_out", "norm2_g", "w_ff_gate", "w_ff_up", "w_ff_down", "final_g"]
    loss = lax.psum(loss_local[0, 0], ("x", "y", "c"))
    grad_x = dx.reshape(x.shape)
    return (loss, grad_x, *[res[n][0] for n in order], *[res[n][1] for n in order],
            *[res[n][2] for n in order], *[res[n][3] for n in order])


PACK_ORDER = ["w_in", "w_branch_m", "w_branch_g", "w_out", "w_ff_gate", "w_ff_up", "w_ff_down", "meta_tokens", "conv_w", "g_a2",
              "m_head_g", "g_head_g", "norm1_g", "conv_b", "m_gate_b", "g_a2_b", "norm2_g", "final_g"]


def _gather_weights(w_in, w_branch_m, w_branch_g, w_out, w_ff_gate, w_ff_up, w_ff_down, meta_tokens, conv_w, g_a2, m_head_g, g_head_g):
    bf = lambda a: a.astype(BF)
    rows_local = jnp.concatenate([bf(w_branch_m[0]), bf(w_branch_g[0]), bf(w_out[0]), bf(w_ff_down[0]),
                                  bf(w_ff_gate[0].T), bf(w_ff_up[0].T)], axis=0)
    win_local = bf(w_in[0])
    small_local = _pack_small([meta_tokens, conv_w[0], g_a2[0], m_head_g[0], g_head_g[0]])
    rows_all = jnp.stack(_by_chip(rows_local, _all_gather_chips(rows_local, "gather_rows")))
    win_all = _by_chip(win_local, _all_gather_chips(win_local, "gather_w_in"))
    small_all = _by_chip(small_local, _all_gather_chips(small_local, "gather_small"))
    cut = lambda lo, hi: rows_all[:, lo:hi].reshape(4 * (hi - lo), D)
    wbm, wbg, wout, wdown = cut(0, 256), cut(256, 512), cut(512, 768), cut(768, 1472)
    wgu_t = _ffn_weight_rows(cut(1472, 2176), cut(2176, 2880))
    w_in_f = jnp.concatenate([win_all[q] for q in range(4)], axis=1)
    small_sh = [_unpack_small(small_all[q], SMALL_SHARD_SHAPES) for q in range(4)]
    cat = lambda i: jnp.concatenate([s[i] for s in small_sh], axis=-1)
    return dict(w_in=w_in_f, wbm=wbm, wbg=wbg, wout=wout, wgu_t=wgu_t, wdown=wdown, meta=cat(0), convw=cat(1), ga2=cat(2),
                mhg=cat(3).reshape(1, D), ghg=cat(4).reshape(1, D))


def _local_step(x0, target, w, norm1_g, conv_b, m_gate_b, g_a2_b, norm2_g, final_g):
    w_in_f, wbm, wbg, wout, wgu_t, wdown = w["w_in"], w["wbm"], w["wbg"], w["wout"], w["wgu_t"], w["wdown"]
    meta_f, convw_f, ga2_f, mhg_f, ghg_f = w["meta"], w["convw"], w["ga2"], w["mhg"], w["ghg"]
    w_big, w_small = _proj_cols_from_w_in(w_in_f)
    w_all = jnp.concatenate([w_big, w_small], axis=1)
    gbias = jnp.concatenate([m_gate_b.reshape(1, 2 * NH), jnp.zeros((1, N_SMALL - 2 * NH), F32)], axis=1)
    a2p = jnp.concatenate([jnp.zeros((8, NH * DQK), F32), ga2_f, jnp.zeros((N_SMALL - 24, NH * DQK), F32)], axis=0).astype(BF)
    convb = conv_b.reshape(1, D)
    g1 = norm1_g.reshape(1, D)
    g2 = norm2_g.reshape(1, D)
    gf = final_g.reshape(1, D)
    h0 = jnp.concatenate([jnp.zeros((FIRST_VALID, D), F32), meta_f, x0], axis=0)

    xn1, rstd1 = _rms_fwd(h0, g1, "rms1")
    pbig = _mm(xn1, w_big, nt=False, out_dtype=BF, tn=1024, name="proj_big")
    small = _mm(xn1, w_small, nt=False, out_dtype=F32, tn=N_SMALL, name="proj_small")
    qk = _conv_fwd(pbig, convw_f, convb, "conv_fwd")
    y_m, m_cs, m_ns = _mlstm_fwd(qk, pbig, small, gbias, mhg_f, "mlstm_fwd")
    y_g, g_ss = _gla_fwd(pbig, small, a2p, g_a2_b, ghg_f, "gla_fwd")
    p_m, p_g, merged = _branch_merge(y_m, y_g, wbm, wbg, pbig, "branch_merge")
    h1, hn, rstd2 = _out_proj_norm(merged, wout, h0, g2, "out_proj")
    gu, ff = _ffn_in(hn, wgu_t, "ff_in")
    dh2, loss_local, d_final_g = _ffn_down_loss(ff, wdown, h1, target, gf, "ff_down_loss")

    d_wdown = _mm_tn(ff, dh2, tm=1408, tn=1024, name="dw_ff_down")
    dgu = _ffn_d_hidden(dh2, wdown, gu, "d_ff")
    d_wgu_t = _mm_tn(dgu, hn, tm=1408, tn=1024, name="dw_ff_in")
    dh1, d_g2 = _ffn_d_in(dgu, wgu_t, h1, rstd2, g2, dh2, "d_hn")
    d_wout = _mm_tn(merged, dh1, tm=1024, tn=1024, name="dw_out")
    dp_m, dp_g, dproj = _merge_d(dh1, wout, p_m, p_g, pbig, "d_merged")
    dy_m = _mm(dp_m, wbm, nt=True, out_dtype=BF, tn=1024, name="d_ym")
    dy_g = _mm(dp_g, wbg, nt=True, out_dtype=BF, tn=1024, name="d_yg")
    d_wbm = _mm_tn(y_m, dp_m, tm=1024, tn=1024, name="dw_branch_m")
    d_wbg = _mm_tn(y_g, dp_g, tm=1024, tn=1024, name="dw_branch_g")
    dqk_m, dproj, dsm_m, d_gbias, d_mhg = _mlstm_bwd(dy_m, qk, pbig, small, gbias, mhg_f, m_cs, m_ns, dproj, "mlstm_bwd")
    dconv, d_convwb = _conv_bwd_pre(dqk_m, pbig, convw_f, convb, "conv_bwd_pre")
    dproj = _conv_bwd_in(dconv, convw_f, dproj, "conv_bwd_in")
    dproj, dsmall, d_a2p, d_a2b, d_ghg = _gla_bwd(dy_g, pbig, small, a2p, g_a2_b, ghg_f, g_ss, dsm_m, dproj, "gla_bwd")
    dproj = _place_small(dsmall, dproj, "dproj_small")
    dxn = _mm(dproj, w_all, nt=True, out_dtype=F32, tn=1024, tk=1664, name="d_xn")
    d_wall = _mm_tn(xn1, dproj, tm=1024, tn=1664, name="dw_in")
    dh_first, dx, d_g1 = _rms_bwd(dxn, h0, rstd1, g1, dh1, "rms1_bwd", split_first=True)

    small_sharded = [dh_first[FIRST_VALID:TM], d_convwb[0:4], d_a2p[8:24], d_mhg.reshape(NH, DV), d_ghg.reshape(NH, DV)]
    replicated = [d_g1, d_convwb[4:5], d_gbias[0:1, 0:2 * NH].reshape(1, 2, NH), d_a2b, d_g2, d_final_g.reshape(D)]
    local = dict(w_all=d_wall, wbm=d_wbm, wbg=d_wbg, wout=d_wout, wdown=d_wdown, wgu_t=d_wgu_t,
                 small_sharded=small_sharded, replicated=replicated)
    return loss_local, dx, local


def _reduce_grads(local):
    d_win = _w_in_from_proj_cols(local["w_all"])
    win4 = jnp.stack([d_win[:, q * W_IN_SHARD:(q + 1) * W_IN_SHARD] for q in range(4)])
    small4 = jnp.stack([_pack_small([g[:, q * shp[1]:(q + 1) * shp[1]] for g, shp in zip(local["small_sharded"], SMALL_SHARD_SHAPES)]
                                    + local["replicated"]) for q in range(4)])
    fq = D_FF // 4
    gu4 = jnp.transpose(local["wgu_t"].reshape(2, 2, 2, fq, D), (0, 2, 1, 3, 4)).reshape(4, 2 * fq, D)
    sq4 = jnp.concatenate([local["wbm"].reshape(4, 256, D), local["wbg"].reshape(4, 256, D), local["wout"].reshape(4, 256, D)], axis=1)
    items = [win4, sq4, local["wdown"].reshape(4, fq, D), gu4, small4]
    c = lax.axis_index("c")
    me = 2 * lax.axis_index("x") + lax.axis_index("y")
    got = _swap_halves(items, "reduce_siblings")
    sums = []
    for i, (a, g) in enumerate(zip(items, got)):
        rh, n = g.shape[1], g.shape[2]
        own = lax.dynamic_slice_in_dim(a, c * rh, rh, axis=1)
        sums.append(_add2(own.reshape(-1, n), g.reshape(-1, n), F32 if i == len(items) - 1 else BF, f"reduce_add2_{i}").reshape(g.shape))
    from_chips = _scatter_chips(sums, "reduce_chips")
    halves = []
    for i, (s, f) in enumerate(zip(sums, from_chips)):
        mine = lax.dynamic_index_in_dim(s, me, 0, keepdims=False)
        if i == len(items) - 1:
            by_chip = _by_chip(mine, f)
            mine, f = by_chip[0], jnp.stack(by_chip[1:])
        halves.append(_add4(mine, f, f"reduce_add4_{i}"))
    got = _join_halves(halves, "reduce_join")
    full = [jnp.where(c == 0, jnp.concatenate([h, g], axis=0), jnp.concatenate([g, h], axis=0)) for h, g in zip(halves, got)]
    smalls = _unpack_small(full[4], SMALL_SHARD_SHAPES + REPL_SHAPES)
    return [full[0], full[1][0:256], full[1][256:512], full[1][512:768], full[3][0:fq].T, full[3][fq:2 * fq].T, full[2]] + smalls
```

```python
import functools

import jax
import jax.numpy as jnp
from jax import lax
from jax.experimental import pallas as pl
from jax.experimental.pallas import tpu as pltpu

F32 = jnp.float32
BF = jnp.bfloat16
HI = lax.Precision.HIGHEST
MESH = pl.DeviceIdType.MESH

D = 1024
N_META = 16
CHUNK = 128
EPS = 1e-6
NH = 4
DV = 256
DQK = 128
G_RANK = 16
G_TAU = 16.0
D_FF = 2816
TM = 512
FIRST_VALID = TM - N_META
CPB = TM // CHUNK
NEG = -1e30
N_BIG = 8192
CB_GQK, CB_GV, CB_GR, CB_MQK, CB_GM, CB_GG, CB_MV, CB_MO = range(8)
N_SMALL = 128
N_ALL = N_BIG + N_SMALL
VMEM_LIMIT = 56 * 1024 * 1024

ADAM_LR, ADAM_B1, ADAM_B2, ADAM_EPS, ADAM_WD, ADAM_STEP = 0.001, 0.9, 0.999, 1e-08, 0.01, 10

NT_DIMS = (((1,), (1,)), ((), ()))
TN_DIMS = (((0,), (0,)), ((), ()))


def _nt(a, b, **kw):
    return lax.dot_general(a, b, NT_DIMS, preferred_element_type=F32, **kw)


def _tn(a, b, **kw):
    return lax.dot_general(a, b, TN_DIMS, preferred_element_type=F32, **kw)


def _nn(a, b, **kw):
    return jnp.dot(a, b, preferred_element_type=F32, **kw)


def _params(**kw):
    return pltpu.CompilerParams(vmem_limit_bytes=VMEM_LIMIT, **kw)


def _sigmoid(x):
    return 0.5 * jnp.tanh(0.5 * x) + 0.5


def _logsig(x):
    return jnp.minimum(x, 0.0) - jnp.log(1.0 + jnp.exp(-jnp.abs(x)))


def _mm_rows(rows):
    return 3 * TM if rows % (3 * TM) == 0 else TM


def _mm(a, b, *, nt, out_dtype, tn, tk=None, tm=None, addend=None, name):
    m, k = a.shape
    n = b.shape[0] if nt else b.shape[1]
    tk = k if tk is None else tk
    tm = _mm_rows(m) if tm is None else tm
    nk = k // tk
    assert m % tm == 0 and n % tn == 0 and k % tk == 0
    dims = NT_DIMS if nt else (((1,), (0,)), ((), ()))

    def finish(r, add_ref, o_ref):
        if add_ref is not None:
            r = r + add_ref[...].astype(F32)
        o_ref[...] = r.astype(o_ref.dtype)

    def body(*refs):
        a_ref, b_ref = refs[:2]
        add_ref = refs[2] if addend is not None else None
        o_ref = refs[3] if addend is not None else refs[2]
        part = lax.dot_general(a_ref[...].astype(BF), b_ref[...].astype(BF), dims, preferred_element_type=F32)
        if nk == 1:
            finish(part, add_ref, o_ref)
            return
        acc_ref = refs[-1]
        kk = pl.program_id(2)

        @pl.when(kk == 0)
        def _():
            acc_ref[...] = part

        @pl.when(jnp.logical_and(kk > 0, kk < nk - 1))
        def _():
            acc_ref[...] += part

        @pl.when(kk == nk - 1)
        def _():
            finish(acc_ref[...] + part, add_ref, o_ref)

    in_specs = [pl.BlockSpec((tm, tk), lambda j, i, kk: (i, kk)),
                pl.BlockSpec((tn, tk), lambda j, i, kk: (j, kk)) if nt else pl.BlockSpec((tk, tn), lambda j, i, kk: (kk, j))]
    args = [a, b]
    if addend is not None:
        in_specs.append(pl.BlockSpec((tm, tn), lambda j, i, kk: (i, j)))
        args.append(addend)
    return pl.pallas_call(
        body, grid=(n // tn, m // tm, nk), in_specs=in_specs,
        out_specs=pl.BlockSpec((tm, tn), lambda j, i, kk: (i, j)),
        out_shape=jax.ShapeDtypeStruct((m, n), out_dtype),
        scratch_shapes=[pltpu.VMEM((tm, tn), F32)] if nk > 1 else [], compiler_params=_params(), name=name)(*args)


def _mm_tn(a, b, *, tm, tn, tk=None, name):
    t, m = a.shape
    n = b.shape[1]
    tk = _mm_rows(t) if tk is None else tk
    assert t % tk == 0 and m % tm == 0 and n % tn == 0

    def body(a_ref, b_ref, o_ref):
        part = _tn(a_ref[...].astype(BF), b_ref[...].astype(BF))

        @pl.when(pl.program_id(2) == 0)
        def _():
            o_ref[...] = part

        @pl.when(pl.program_id(2) > 0)
        def _():
            o_ref[...] += part

    return pl.pallas_call(
        body, grid=(m // tm, n // tn, t // tk),
        in_specs=[pl.BlockSpec((tk, tm), lambda i, j, kk: (kk, i)), pl.BlockSpec((tk, tn), lambda i, j, kk: (kk, j))],
        out_specs=pl.BlockSpec((tm, tn), lambda i, j, kk: (i, j)),
        out_shape=jax.ShapeDtypeStruct((m, n), F32), compiler_params=_params(), name=name)(a, b)


ANY = pl.BlockSpec(memory_space=pl.ANY)


def _row_spec(width, col=0):
    return pl.BlockSpec((TM, width), lambda i: (i, col))


def _full_spec(shape):
    return pl.BlockSpec(shape, lambda i: (0,) * len(shape))


def _rms_fwd(h, g, name):
    tp = h.shape[0]

    def body(h_ref, g_ref, xn_ref, r_ref):
        x = h_ref[...]
        r = lax.rsqrt(jnp.mean(x * x, axis=1, keepdims=True) + EPS)
        xn_ref[...] = (x * r * g_ref[...]).astype(BF)
        r_ref[...] = r

    return pl.pallas_call(
        body, grid=(tp // TM,), in_specs=[_row_spec(D), _full_spec((1, D))],
        out_specs=[_row_spec(D), _row_spec(1)],
        out_shape=(jax.ShapeDtypeStruct((tp, D), BF), jax.ShapeDtypeStruct((tp, 1), F32)),
        compiler_params=_params(), name=name)(h, g)


def _rms_bwd(dxn, h, rstd, g, dres, name, split_first=False):
    tp = h.shape[0]

    def body(dxn_ref, h_ref, r_ref, g_ref, dres_ref, *outs):
        r = r_ref[...]
        xh = h_ref[...] * r
        dxn_v = dxn_ref[...].astype(F32)
        dxh = dxn_v * g_ref[...]
        dh = r * (dxh - xh * jnp.mean(dxh * xh, axis=1, keepdims=True)) + dres_ref[...]
        if split_first:
            first_ref, dh_ref, dg_ref = outs

            @pl.when(pl.program_id(0) == 0)
            def _():
                first_ref[...] = dh
        else:
            dh_ref, dg_ref = outs
        dh_ref[...] = dh
        part = jnp.sum(dxn_v * xh, axis=0, keepdims=True)

        @pl.when(pl.program_id(0) == 0)
        def _():
            dg_ref[...] = part

        @pl.when(pl.program_id(0) > 0)
        def _():
            dg_ref[...] += part

    if split_first:
        out_specs = [_full_spec((TM, D)), pl.BlockSpec((TM, D), lambda i: (jnp.maximum(i - 1, 0), 0)), _full_spec((1, D))]
        out_shape = (jax.ShapeDtypeStruct((TM, D), F32), jax.ShapeDtypeStruct((tp - TM, D), F32), jax.ShapeDtypeStruct((1, D), F32))
    else:
        out_specs = [_row_spec(D), _full_spec((1, D))]
        out_shape = (jax.ShapeDtypeStruct((tp, D), F32), jax.ShapeDtypeStruct((1, D), F32))
    return pl.pallas_call(
        body, grid=(tp // TM,),
        in_specs=[_row_spec(D), _row_spec(D), _row_spec(1), _full_spec((1, D)), _row_spec(D)],
        out_specs=out_specs, out_shape=out_shape, compiler_params=_params(), name=name)(dxn, h, rstd, g, dres)


def _shift_down(x, halo, k):
    rk = pltpu.roll(x, k, 0)
    io = lax.broadcasted_iota(jnp.int32, (8, x.shape[1]), 0)
    top = jnp.where(io < k, pltpu.roll(halo, k, 0), rk[0:8])
    return jnp.concatenate([top, rk[8:]], axis=0)


def _shift_up(x, nxt, k):
    n = x.shape[0]
    rk = pltpu.roll(x, n - k, 0)
    io = lax.broadcasted_iota(jnp.int32, (8, x.shape[1]), 0)
    bot = jnp.where(io >= 8 - k, pltpu.roll(nxt, 8 - k, 0), rk[n - 8:n])
    return jnp.concatenate([rk[:n - 8], bot], axis=0)


def _conv_pre(x, halo, w_ref, b_ref):
    c = x * w_ref[3:4, :] + b_ref[...]
    shifted = []
    for k in (1, 2, 3):
        s = _shift_down(x, halo, k)
        shifted.append(s)
        c = c + s * w_ref[3 - k:4 - k, :]
    return c, shifted


def _qk_scale():
    col = lax.broadcasted_iota(jnp.int32, (1, D), 1)
    return jnp.where(col < NH * DQK, DQK ** -0.5, 1.0).astype(F32)


def _halo_prev_spec():
    return pl.BlockSpec((8, D), lambda i: (jnp.maximum(i * (TM // 8) - 1, 0), CB_MQK))


def _conv_fwd(pbig, w, b, name):
    tp = pbig.shape[0]

    def body(x_ref, halo_ref, w_ref, b_ref, o_ref):
        x = x_ref[...].astype(F32)
        halo = jnp.where(pl.program_id(0) > 0, halo_ref[...].astype(F32), 0.0)
        c, _ = _conv_pre(x, halo, w_ref, b_ref)
        o_ref[...] = (c * _sigmoid(c) * _qk_scale()).astype(BF)

    return pl.pallas_call(
        body, grid=(tp // TM,),
        in_specs=[_row_spec(D, CB_MQK), _halo_prev_spec(), _full_spec((4, D)), _full_spec((1, D))],
        out_specs=_row_spec(D), out_shape=jax.ShapeDtypeStruct((tp, D), BF),
        compiler_params=_params(), name=name)(pbig, pbig, w, b)


def _conv_bwd_pre(dqk, pbig, w, b, name):
    tp = pbig.shape[0]

    def body(d_ref, x_ref, halo_ref, w_ref, b_ref, dc_ref, dwb_ref):
        x = x_ref[...].astype(F32)
        halo = jnp.where(pl.program_id(0) > 0, halo_ref[...].astype(F32), 0.0)
        c, shifted = _conv_pre(x, halo, w_ref, b_ref)
        sg = _sigmoid(c)
        dc = d_ref[...] * _qk_scale() * (sg * (1.0 + c * (1.0 - sg)))
        dc_ref[...] = dc
        taps = [shifted[2], shifted[1], shifted[0], x]
        rows = [jnp.sum(dc * t, axis=0, keepdims=True) for t in taps] + [jnp.sum(dc, axis=0, keepdims=True)]
        io = lax.broadcasted_iota(jnp.int32, (8, D), 0)
        part = jnp.zeros((8, D), F32)
        for r, v in enumerate(rows):
            part = jnp.where(io == r, v, part)

        @pl.when(pl.program_id(0) == 0)
        def _():
            dwb_ref[...] = part

        @pl.when(pl.program_id(0) > 0)
        def _():
            dwb_ref[...] += part

    return pl.pallas_call(
        body, grid=(tp // TM,),
        in_specs=[_row_spec(D), _row_spec(D, CB_MQK), _halo_prev_spec(), _full_spec((4, D)), _full_spec((1, D))],
        out_specs=[_row_spec(D), _full_spec((8, D))],
        out_shape=(jax.ShapeDtypeStruct((tp, D), F32), jax.ShapeDtypeStruct((8, D), F32)),
        compiler_params=_params(), name=name)(dqk, pbig, pbig, w, b)


def _conv_bwd_in(dc, w, dproj, name):
    tp = dc.shape[0]
    nb = tp // TM

    def body(d_ref, nxt_ref, w_ref, _, o_ref):
        d = d_ref[...]
        nxt = jnp.where(pl.program_id(0) < nb - 1, nxt_ref[...], 0.0)
        acc = d * w_ref[3:4, :]
        for k in (1, 2, 3):
            acc = acc + _shift_up(d, nxt, k) * w_ref[3 - k:4 - k, :]
        o_ref[...] = acc.astype(BF)

    return pl.pallas_call(
        body, grid=(nb,),
        in_specs=[_row_spec(D), pl.BlockSpec((8, D), lambda i: (jnp.minimum((i + 1) * (TM // 8), tp // 8 - 1), 0)),
                  _full_spec((4, D)), ANY],
        out_specs=_row_spec(D, CB_MQK), out_shape=jax.ShapeDtypeStruct(dproj.shape, BF),
        input_output_aliases={3: 0}, compiler_params=_params(), name=name)(dc, dc, w, dproj)


def _mm_fused(inputs, products, *, nt, m, n, tm, tn, outs, epilogue, name, nk=1, sub=None):
    dims = NT_DIMS if nt else (((1,), (0,)), ((), ()))
    nin = len(inputs)
    assert nk == 1 or (len(products) == 1 and sub is None)

    def body(*refs):
        in_refs, out_refs = refs[:nin], refs[nin:nin + len(outs)]
        i = pl.program_id(1)
        if sub is not None:
            lhs = {ia: in_refs[ia][...].astype(BF) for ia, _ in products}

            def dots(cols):
                return [lax.dot_general(lhs[ia], (in_refs[ib][cols, :] if nt else in_refs[ib][:, cols]).astype(BF),
                                        dims, preferred_element_type=F32) for ia, ib in products]

            slices = [slice(s, min(s + sub, tn)) for s in range(0, tn, sub)]
            prods = dots(slices[0])
            for idx, cols in enumerate(slices):
                nxt = dots(slices[idx + 1]) if idx + 1 < len(slices) else None
                epilogue(prods, in_refs, out_refs, i, cols)
                prods = nxt
            return
        prods = [lax.dot_general(in_refs[ia][...].astype(BF), in_refs[ib][...].astype(BF), dims, preferred_element_type=F32)
                 for ia, ib in products]
        if nk == 1:
            epilogue(prods, in_refs, out_refs, i, slice(None))
            return
        acc_ref = refs[-1]
        kk = pl.program_id(2)

        @pl.when(kk == 0)
        def _():
            acc_ref[...] = prods[0]

        @pl.when(jnp.logical_and(kk > 0, kk < nk - 1))
        def _():
            acc_ref[...] += prods[0]

        @pl.when(kk == nk - 1)
        def _():
            epilogue([acc_ref[...] + prods[0]], in_refs, out_refs, i, slice(None))

    return pl.pallas_call(
        body, grid=(n // tn, m // tm, nk), in_specs=[s for _, s in inputs], out_specs=[s for _, s in outs],
        out_shape=tuple(sh for sh, _ in outs), scratch_shapes=[pltpu.VMEM((tm, tn), F32)] if nk > 1 else [],
        compiler_params=_params(), name=name)(*[a for a, _ in inputs])


SUB_COLS = 256


def _cols_at(cols, offset):
    return slice(cols.start + offset, cols.stop + offset)


def _blk(rows, width, col=None, row=None):
    return pl.BlockSpec((rows, width), lambda j, i, kk: ((i if row is None else row(i)), (0 if col is None else col(j, kk))))


FF_TN = D_FF // 2


def _ffn_weight_rows(wg_t, wu_t):
    return jnp.concatenate([wg_t[0:FF_TN], wu_t[0:FF_TN], wg_t[FF_TN:], wu_t[FF_TN:]], axis=0)


def _ffn_in(hn, wgu_t, name):
    tp = hn.shape[0]
    tm = _mm_rows(tp)

    def epilogue(prods, in_refs, out_refs, i, cols):
        g, u = prods
        out_refs[0][:, cols] = g.astype(BF)
        out_refs[0][:, _cols_at(cols, FF_TN)] = u.astype(BF)
        out_refs[1][:, cols] = (g * _sigmoid(g) * u).astype(BF)

    wspec = lambda off: pl.BlockSpec((FF_TN, D), lambda j, i, kk: (2 * j + off, 0))
    return _mm_fused(
        [(hn, _blk(tm, D)), (wgu_t, wspec(0)), (wgu_t, wspec(1))], [(0, 1), (0, 2)], nt=True, m=tp, n=D_FF, tm=tm, tn=FF_TN,
        outs=[(jax.ShapeDtypeStruct((tp, 2 * D_FF), BF), _blk(tm, 2 * FF_TN, lambda j, kk: j)),
              (jax.ShapeDtypeStruct((tp, D_FF), BF), _blk(tm, FF_TN, lambda j, kk: j))],
        epilogue=epilogue, name=name, sub=SUB_COLS)


def _ffn_down_loss(ff, wdown, h1, target, gf, name):
    tp = ff.shape[0]

    def epilogue(prods, in_refs, out_refs, i, cols):
        live = (i > 0).astype(F32)
        g = in_refs[4][...]
        x = prods[0] + in_refs[2][...]
        r = lax.rsqrt(jnp.mean(x * x, axis=1, keepdims=True) + EPS)
        xh = x * r
        e = xh * g - in_refs[3][...]
        loss_part = 0.5 * live * jnp.sum(jnp.mean(e * e, axis=1, keepdims=True), axis=0, keepdims=True)
        dout = e * (live / D)
        dg_part = jnp.sum(dout * xh, axis=0, keepdims=True)
        dxh = dout * g
        out_refs[0][...] = r * (dxh - xh * jnp.mean(dxh * xh, axis=1, keepdims=True))

        @pl.when(i == 0)
        def _():
            out_refs[1][...] = loss_part
            out_refs[2][...] = dg_part

        @pl.when(i > 0)
        def _():
            out_refs[1][...] += loss_part
            out_refs[2][...] += dg_part

    const = lambda shape: pl.BlockSpec(shape, lambda j, i, kk: (0,) * len(shape))
    return _mm_fused(
        [(ff, _blk(TM, D_FF)), (wdown, const((D_FF, D))), (h1, _blk(TM, D)),
         (target, _blk(TM, D, row=lambda i: jnp.maximum(i - 1, 0))), (gf, const((1, D)))],
        [(0, 1)], nt=False, m=tp, n=D, tm=TM, tn=D,
        outs=[(jax.ShapeDtypeStruct((tp, D), F32), _blk(TM, D)), (jax.ShapeDtypeStruct((1, 1), F32), const((1, 1))),
              (jax.ShapeDtypeStruct((1, D), F32), const((1, D)))],
        epilogue=epilogue, name=name)


def _ffn_d_hidden(dh2, wdown, gu, name):
    tp = dh2.shape[0]

    def epilogue(prods, in_refs, out_refs, i, cols):
        d = prods[0]
        g = in_refs[2][:, cols].astype(F32)
        u = in_refs[2][:, _cols_at(cols, FF_TN)].astype(F32)
        sg = _sigmoid(g)
        out_refs[0][:, cols] = (d * u * sg * (1.0 + g * (1.0 - sg))).astype(BF)
        out_refs[0][:, _cols_at(cols, FF_TN)] = (d * g * sg).astype(BF)

    return _mm_fused(
        [(dh2, _blk(TM, D)), (wdown, pl.BlockSpec((FF_TN, D), lambda j, i, kk: (j, 0))), (gu, _blk(TM, 2 * FF_TN, lambda j, kk: j))],
        [(0, 1)], nt=True, m=tp, n=D_FF, tm=TM, tn=FF_TN,
        outs=[(jax.ShapeDtypeStruct((tp, 2 * D_FF), BF), _blk(TM, 2 * FF_TN, lambda j, kk: j))],
        epilogue=epilogue, name=name, sub=SUB_COLS)[0]


def _ffn_d_in(dgu, wgu_t, h1, rstd, g2, dh2, name):
    tp = dgu.shape[0]
    nk = 2

    def epilogue(prods, in_refs, out_refs, i, cols):
        r = in_refs[3][...]
        xh = in_refs[2][...] * r
        dxn = prods[0]
        dxh = dxn * in_refs[4][...]
        out_refs[0][...] = r * (dxh - xh * jnp.mean(dxh * xh, axis=1, keepdims=True)) + in_refs[5][...]
        part = jnp.sum(dxn * xh, axis=0, keepdims=True)

        @pl.when(i == 0)
        def _():
            out_refs[1][...] = part

        @pl.when(i > 0)
        def _():
            out_refs[1][...] += part

    const = lambda shape: pl.BlockSpec(shape, lambda j, i, kk: (0,) * len(shape))
    return _mm_fused(
        [(dgu, pl.BlockSpec((TM, D_FF), lambda j, i, kk: (i, kk))), (wgu_t, pl.BlockSpec((D_FF, D), lambda j, i, kk: (kk, 0))),
         (h1, _blk(TM, D)), (rstd, _blk(TM, 1)), (g2, const((1, D))), (dh2, _blk(TM, D))],
        [(0, 1)], nt=False, m=tp, n=D, tm=TM, tn=D, nk=nk,
        outs=[(jax.ShapeDtypeStruct((tp, D), F32), _blk(TM, D)), (jax.ShapeDtypeStruct((1, D), F32), const((1, D)))],
        epilogue=epilogue, name=name)


def _branch_merge(y_m, y_g, wbm, wbg, pbig, name):
    tp = y_m.shape[0]

    def epilogue(prods, in_refs, out_refs, i, cols):
        pm, pg = prods[0].astype(BF), prods[1].astype(BF)
        out_refs[0][:, cols] = pm
        out_refs[1][:, cols] = pg
        out_refs[2][:, cols] = (_sigmoid(in_refs[4][:, cols].astype(F32)) * pm.astype(F32)
                                + _sigmoid(in_refs[5][:, cols].astype(F32)) * pg.astype(F32)).astype(BF)

    const = lambda shape: pl.BlockSpec(shape, lambda j, i, kk: (0,) * len(shape))
    shp = jax.ShapeDtypeStruct((tp, D), BF)
    return _mm_fused(
        [(y_m, _blk(TM, D)), (wbm, const((D, D))), (y_g, _blk(TM, D)), (wbg, const((D, D))),
         (pbig, _blk(TM, D, lambda j, kk: CB_GM)), (pbig, _blk(TM, D, lambda j, kk: CB_GG))],
        [(0, 1), (2, 3)], nt=False, m=tp, n=D, tm=TM, tn=D,
        outs=[(shp, _blk(TM, D)), (shp, _blk(TM, D)), (shp, _blk(TM, D))], epilogue=epilogue, name=name, sub=SUB_COLS)


def _merge_d(dh1, wout, pm, pg, pbig, name):
    tp = dh1.shape[0]

    def epilogue(prods, in_refs, out_refs, i, cols):
        d = prods[0]
        sm = _sigmoid(in_refs[4][:, cols].astype(F32))
        sg = _sigmoid(in_refs[5][:, cols].astype(F32))
        out_refs[0][:, cols] = (d * sm).astype(BF)
        out_refs[1][:, cols] = (d * sg).astype(BF)
        out_refs[2][:, cols] = (d * in_refs[2][:, cols].astype(F32) * sm * (1.0 - sm)).astype(BF)
        out_refs[2][:, _cols_at(cols, D)] = (d * in_refs[3][:, cols].astype(F32) * sg * (1.0 - sg)).astype(BF)

    const = lambda shape: pl.BlockSpec(shape, lambda j, i, kk: (0,) * len(shape))
    shp = jax.ShapeDtypeStruct((tp, D), BF)
    return _mm_fused(
        [(dh1, _blk(TM, D)), (wout, const((D, D))), (pm, _blk(TM, D)), (pg, _blk(TM, D)),
         (pbig, _blk(TM, D, lambda j, kk: CB_GM)), (pbig, _blk(TM, D, lambda j, kk: CB_GG))],
        [(0, 1)], nt=True, m=tp, n=D, tm=TM, tn=D,
        outs=[(shp, _blk(TM, D)), (shp, _blk(TM, D)),
              (jax.ShapeDtypeStruct((tp, N_ALL), BF), _blk(TM, 2 * D, lambda j, kk: CB_GM // 2))],
        epilogue=epilogue, name=name, sub=SUB_COLS)


def _out_proj_norm(merged, wout, h0, g2, name):
    tp = merged.shape[0]
    tm = _mm_rows(tp)

    def epilogue(prods, in_refs, out_refs, i, cols):
        x = prods[0] + in_refs[2][...]
        r = lax.rsqrt(jnp.mean(x * x, axis=1, keepdims=True) + EPS)
        out_refs[0][...] = x
        out_refs[1][...] = (x * r * in_refs[3][...]).astype(BF)
        out_refs[2][...] = r

    const = lambda shape: pl.BlockSpec(shape, lambda j, i, kk: (0,) * len(shape))
    return _mm_fused(
        [(merged, _blk(tm, D)), (wout, const((D, D))), (h0, _blk(tm, D)), (g2, const((1, D)))],
        [(0, 1)], nt=False, m=tp, n=D, tm=tm, tn=D,
        outs=[(jax.ShapeDtypeStruct((tp, D), F32), _blk(tm, D)), (jax.ShapeDtypeStruct((tp, D), BF), _blk(tm, D)),
              (jax.ShapeDtypeStruct((tp, 1), F32), _blk(tm, 1))],
        epilogue=epilogue, name=name)


def _adamw(w, g, m, v, name):
    rows, cols = w.shape
    tr = 128 if rows % 128 == 0 else rows

    def body(w_ref, g_ref, m_ref, v_ref, d_ref, nm_ref, nv_ref):
        gv = g_ref[...]
        nm = ADAM_B1 * m_ref[...] + (1.0 - ADAM_B1) * gv
        nv = ADAM_B2 * v_ref[...] + (1.0 - ADAM_B2) * (gv * gv)
        m_hat = nm / (1.0 - ADAM_B1 ** ADAM_STEP)
        v_hat = nv / (1.0 - ADAM_B2 ** ADAM_STEP)
        d_ref[...] = -ADAM_LR * (m_hat / (jnp.sqrt(v_hat) + ADAM_EPS) + ADAM_WD * w_ref[...])
        nm_ref[...] = nm
        nv_ref[...] = nv

    spec = pl.BlockSpec((tr, cols), lambda i: (i, 0))
    shp = jax.ShapeDtypeStruct((rows, cols), F32)
    return pl.pallas_call(body, grid=(rows // tr,), in_specs=[spec] * 4, out_specs=[spec] * 3,
                          out_shape=(shp,) * 3, compiler_params=_params(), name=name)(w, g, m, v)


def _place_small(dsmall, dproj, name):
    tp = dsmall.shape[0]

    def body(s_ref, _, o_ref):
        o_ref[...] = s_ref[...]

    return pl.pallas_call(
        body, grid=(tp // TM,), in_specs=[_row_spec(N_SMALL), ANY], out_specs=_row_spec(N_SMALL, N_BIG // N_SMALL),
        out_shape=jax.ShapeDtypeStruct(dproj.shape, dproj.dtype), input_output_aliases={1: 0},
        compiler_params=_params(), name=name)(dsmall, dproj)


def _row_tile(rows, cap=512):
    best = rows
    for cand in range(8, min(rows, cap) + 1, 8):
        if rows % cand == 0:
            best = cand
    return best


def _add2(a, b, out_dtype, name):
    rows, cols = a.shape
    tr = _row_tile(rows)

    def body(a_ref, b_ref, o_ref):
        o_ref[...] = (a_ref[...] + b_ref[...]).astype(o_ref.dtype)

    spec = pl.BlockSpec((tr, cols), lambda i: (i, 0))
    return pl.pallas_call(body, grid=(rows // tr,), in_specs=[spec] * 2, out_specs=spec,
                          out_shape=jax.ShapeDtypeStruct((rows, cols), out_dtype), compiler_params=_params(), name=name)(a, b)


def _add4(first, rest, name):
    rows, cols = first.shape
    tr = _row_tile(rows, 256)

    def body(f_ref, r_ref, o_ref):
        up = lambda v: v.astype(F32)
        o_ref[...] = ((up(f_ref[...]) + up(r_ref[0])) + up(r_ref[1])) + up(r_ref[2])

    return pl.pallas_call(body, grid=(rows // tr,),
                          in_specs=[pl.BlockSpec((tr, cols), lambda i: (i, 0)), pl.BlockSpec((3, tr, cols), lambda i: (0, i, 0))],
                          out_specs=pl.BlockSpec((tr, cols), lambda i: (i, 0)),
                          out_shape=jax.ShapeDtypeStruct((rows, cols), F32), compiler_params=_params(), name=name)(first, rest)


def _chunk_consts():
    r2 = lax.broadcasted_iota(jnp.int32, (CHUNK, CHUNK), 0)
    c2 = lax.broadcasted_iota(jnp.int32, (CHUNK, CHUNK), 1)
    tri = r2 >= c2
    return dict(tri=tri, tril_f=tri.astype(F32), triu_f=(r2 <= c2).astype(F32),
                lane=lax.broadcasted_iota(jnp.int32, (CHUNK, N_SMALL), 1),
                rowio=lax.broadcasted_iota(jnp.int32, (CHUNK, 1), 0),
                ones=jnp.ones((CHUNK, N_SMALL), F32))


def _valid_rows(block, c):
    row = block * TM + c * CHUNK + lax.broadcasted_iota(jnp.int32, (CHUNK, 1), 0)
    return row >= FIRST_VALID


def _col(x, lane, idx):
    return jnp.sum(jnp.where(lane == idx, x, 0.0), axis=1, keepdims=True)


def _last_row(x, rowio):
    return jnp.sum(jnp.where(rowio == CHUNK - 1, x, 0.0), axis=0, keepdims=True)


def _sum_all(x):
    return jnp.sum(jnp.sum(x, axis=1, keepdims=True), axis=0, keepdims=True)


def _headnorm_fwd(hm, gain, gate_act):
    rs = lax.rsqrt(jnp.mean(hm * hm, axis=1, keepdims=True) + EPS)
    return hm * rs * gain * gate_act


def _headnorm_bwd(dy, hm, gain, gate_act):
    rs = lax.rsqrt(jnp.mean(hm * hm, axis=1, keepdims=True) + EPS)
    xh = hm * rs
    dact = dy * xh * gain
    dgain = jnp.sum(dy * gate_act * xh, axis=0, keepdims=True)
    dxh = dy * gate_act * gain
    dhm = rs * (dxh - xh * jnp.mean(dxh * xh, axis=1, keepdims=True))
    return dhm, dact, dgain


def _mlstm_gates(sm, gbias, valid, k):
    pre = sm + gbias
    lf = jnp.where(valid, _logsig(pre), 0.0)
    b_all = _nn(k["tril_f"], lf, precision=HI)
    li_all = jnp.where(valid, pre, NEG)
    return pre, li_all, b_all


def _mlstm_open(h, qh, kh, c_st, li_all, b_all, k):
    lane = k["lane"]
    sel = jnp.where(lane == h, 1.0, 0.0) - jnp.where(lane == NH + h, 1.0, 0.0)
    x = jnp.where(lane < NH, li_all, jnp.where(lane < 2 * NH, b_all, 0.0))
    cb = c_st.astype(BF)
    return dict(ubc=_nt(sel, x, precision=HI), sim=_nt(qh, kh), cb=cb, cq=_nt(qh, cb))


def _mlstm_weights(h, f, qh, vh, li_all, b_all, n_row, m11, k):
    lane, tri, rowio = k["lane"], k["tri"], k["rowio"]
    b_col = _col(b_all, lane, NH + h)
    li_col = _col(li_all, lane, h)
    dmat = jnp.where(tri, b_col + f["ubc"], NEG)
    m_row = jnp.maximum(b_col + m11, jnp.max(dmat, axis=1, keepdims=True))
    e = jnp.exp(dmat - m_row)
    w_mat = e * f["sim"]
    a = jnp.exp(b_col + m11 - m_row)
    qf = qh.astype(F32)
    nq = jnp.sum(qf * n_row, axis=1, keepdims=True)
    g = _last_row(b_col, rowio)
    wlog = g - b_col + li_col
    m_new = jnp.maximum(g + m11, jnp.max(wlog, axis=0, keepdims=True))
    a_s = jnp.exp(g + m11 - m_new)
    w = jnp.exp(wlog - m_new)
    return dict(f, e=e, w_mat=w_mat, a=a, qf=qf, nq=nq, m_row=m_row, m_new=m_new, a_s=a_s, w=w,
                wv=_nn(w_mat.astype(BF), vh))


def _mlstm_out(f):
    num = f["a"] * f["cq"] + f["wv"]
    den = f["a"] * f["nq"] + jnp.sum(f["w_mat"], axis=1, keepdims=True)
    floor = jnp.exp(-f["m_row"])
    r = jnp.maximum(jnp.abs(den), floor)
    return dict(f, den=den, floor=floor, r=r, hm=num / r)


def _mlstm_fwd(qk, pbig, small, gbias, headg, name):
    tp = qk.shape[0]
    nb = tp // TM

    def body(qk_ref, v_ref, mo_ref, sm_ref, gb_ref, hg_ref, y_ref, cs_ref, ns_ref, c_scr, n_scr):
        blk = pl.program_id(0)

        @pl.when(blk == 0)
        def _():
            c_scr[...] = jnp.zeros_like(c_scr)
            n_scr[...] = jnp.zeros_like(n_scr)

        k = _chunk_consts()
        io8 = lax.broadcasted_iota(jnp.int32, (8, DQK), 0)

        def chunk(c, carry):
            r0 = pl.multiple_of(c * CHUNK, CHUNK)
            rows = pl.ds(r0, CHUNK)
            valid = _valid_rows(blk, c)
            _, li_all, b_all = _mlstm_gates(sm_ref[rows, :], gb_ref[...], valid, k)
            heads = range(NH)
            qs = [qk_ref[rows, h * DQK:(h + 1) * DQK] for h in heads]
            ks = [qk_ref[rows, NH * DQK + h * DQK:NH * DQK + (h + 1) * DQK] for h in heads]
            vs = [v_ref[rows, h * DV:(h + 1) * DV] for h in heads]
            cst = [c_scr[h] for h in heads]
            nrow = [n_scr[h, 0:1, :] for h in heads]
            m11 = [jnp.max(n_scr[h, 1:2, :], axis=1, keepdims=True) for h in heads]
            f = [_mlstm_open(h, qs[h], ks[h], cst[h], li_all, b_all, k) for h in heads]
            f = [_mlstm_weights(h, f[h], qs[h], vs[h], li_all, b_all, nrow[h], m11[h], k) for h in heads]
            wk = [f[h]["w"] * ks[h].astype(F32) for h in heads]
            kv = [_tn(vs[h], wk[h].astype(BF)) for h in heads]
            for h in heads:
                hm = _mlstm_out(f[h])["hm"]
                gate = _sigmoid(mo_ref[rows, h * DV:(h + 1) * DV].astype(F32))
                y_ref[rows, h * DV:(h + 1) * DV] = _headnorm_fwd(hm, hg_ref[:, h * DV:(h + 1) * DV], gate).astype(BF)
                cs_ref[c, h] = f[h]["cb"]
                ns_ref[c, h] = jnp.where(io8 == 0, nrow[h], jnp.where(io8 == 1, m11[h], 0.0))
                c_scr[h] = f[h]["a_s"] * cst[h] + kv[h]
                n_scr[h, 0:1, :] = f[h]["a_s"] * nrow[h] + jnp.sum(wk[h], axis=0, keepdims=True)
                n_scr[h, 1:2, :] = jnp.broadcast_to(f[h]["m_new"], (1, DQK))
            return carry

        lax.fori_loop(0, CPB, chunk, 0, unroll=2)

    return pl.pallas_call(
        body, grid=(nb,),
        in_specs=[_row_spec(D), _row_spec(D, CB_MV), _row_spec(D, CB_MO), _row_spec(N_SMALL), _full_spec((1, N_SMALL)), _full_spec((1, D))],
        out_specs=[_row_spec(D), pl.BlockSpec((CPB, NH, DV, DQK), lambda i: (i, 0, 0, 0)),
                   pl.BlockSpec((CPB, NH, 8, DQK), lambda i: (i, 0, 0, 0))],
        out_shape=(jax.ShapeDtypeStruct((tp, D), BF), jax.ShapeDtypeStruct((tp // CHUNK, NH, DV, DQK), BF),
                   jax.ShapeDtypeStruct((tp // CHUNK, NH, 8, DQK), F32)),
        scratch_shapes=[pltpu.VMEM((NH, DV, DQK), F32), pltpu.VMEM((NH, 8, DQK), F32)],
        compiler_params=_params(), name=name)(qk, pbig, pbig, small, gbias, headg)


def _mlstm_bwd(dy, qk, pbig, small, gbias, headg, cs, ns, dproj, name):
    tp = qk.shape[0]
    nb = tp // TM

    def body(dy_ref, qk_ref, v_ref, mo_ref, sm_ref, gb_ref, hg_ref, cs_ref, ns_ref, _,
             dqk_ref, dproj_ref, dsm_ref, dgb_ref, dhg_ref, dc_scr, dn_scr):
        step = pl.program_id(0)
        blk = nb - 1 - step

        @pl.when(step == 0)
        def _():
            dc_scr[...] = jnp.zeros_like(dc_scr)
            dn_scr[...] = jnp.zeros_like(dn_scr)
            dgb_ref[...] = jnp.zeros_like(dgb_ref)
            dhg_ref[...] = jnp.zeros_like(dhg_ref)

        k = _chunk_consts()
        lane, rowio = k["lane"], k["rowio"]

        def chunk(cc, carry):
            c = CPB - 1 - cc
            r0 = pl.multiple_of(c * CHUNK, CHUNK)
            rows = pl.ds(r0, CHUNK)
            valid = _valid_rows(blk, c)
            pre, li_all, b_all = _mlstm_gates(sm_ref[rows, :], gb_ref[...], valid, k)
            dli_all = jnp.zeros((CHUNK, N_SMALL), F32)
            db_all = jnp.zeros((CHUNK, N_SMALL), F32)
            heads = range(NH)
            qs = [qk_ref[rows, h * DQK:(h + 1) * DQK] for h in heads]
            ks = [qk_ref[rows, NH * DQK + h * DQK:NH * DQK + (h + 1) * DQK] for h in heads]
            vs = [v_ref[rows, h * DV:(h + 1) * DV] for h in heads]
            cst = [cs_ref[c, h].astype(F32) for h in heads]
            nrow = [ns_ref[c, h, 0:1, :] for h in heads]
            m11 = [jnp.max(ns_ref[c, h, 1:2, :], axis=1, keepdims=True) for h in heads]
            f = [_mlstm_open(h, qs[h], ks[h], cst[h], li_all, b_all, k) for h in heads]
            f = [_mlstm_weights(h, f[h], qs[h], vs[h], li_all, b_all, nrow[h], m11[h], k) for h in heads]
            f = [_mlstm_out(f[h]) for h in heads]
            t = []
            for h in heads:
                gain = hg_ref[:, h * DV:(h + 1) * DV]
                gate = _sigmoid(mo_ref[rows, h * DV:(h + 1) * DV].astype(F32))
                dhm, dgate, dgain = _headnorm_bwd(dy_ref[rows, h * DV:(h + 1) * DV].astype(F32), f[h]["hm"], gain, gate)
                dproj_ref[rows, D + h * DV:D + (h + 1) * DV] = (dgate * gate * (1.0 - gate)).astype(BF)
                dhg_ref[:, h * DV:(h + 1) * DV] += dgain
                r, den = f[h]["r"], f[h]["den"]
                dnum = dhm / r
                dr = -jnp.sum(dhm * f[h]["hm"], axis=1, keepdims=True) / r
                dden = jnp.where(jnp.abs(den) > f[h]["floor"], dr * jnp.sign(den), 0.0)
                dnb = dnum.astype(BF)
                dc_new = dc_scr[h]
                dcb = dc_new.astype(BF)
                t.append(dict(dnum=dnum, dden=dden, dnb=dnb, dc_new=dc_new, dn_new=dn_scr[h],
                              dwm=_nt(dnb, vs[h]), vdc=_nn(vs[h], dcb), kdc=_nt(ks[h], dcb)))
            for h in heads:
                dw_mat = t[h]["dwm"] + t[h]["dden"]
                dsim = (f[h]["e"] * dw_mat).astype(BF)
                gm = f[h]["w_mat"] * dw_mat
                t[h].update(gm=gm, dv0=_tn(f[h]["w_mat"].astype(BF), t[h]["dnb"]), dq0=_nn(dsim, ks[h]),
                            dq1=_nn(t[h]["dnb"], f[h]["cb"]), dk0=_tn(dsim, qs[h]),
                            dcq=_tn((f[h]["a"] * t[h]["dnum"]).astype(BF), qs[h]), cs2=_tn(gm, k["ones"], precision=HI))
            for h in heads:
                a, w, a_s = f[h]["a"], f[h]["w"], f[h]["a_s"]
                dnum, dden, dc_new, dn_new, vdc, gm = (t[h][n] for n in ("dnum", "dden", "dc_new", "dn_new", "vdc", "gm"))
                kf = ks[h].astype(F32)
                dproj_ref[rows, h * DV:(h + 1) * DV] = (t[h]["dv0"] + w * t[h]["kdc"]).astype(BF)
                adden = a * dden
                dqk_ref[rows, h * DQK:(h + 1) * DQK] = t[h]["dq0"] + a * t[h]["dq1"] + adden * nrow[h]
                dqk_ref[rows, NH * DQK + h * DQK:NH * DQK + (h + 1) * DQK] = t[h]["dk0"] + w * vdc + w * dn_new
                da = jnp.sum(dnum * f[h]["cq"], axis=1, keepdims=True) + dden * f[h]["nq"]
                dw = jnp.sum(vdc * kf, axis=1, keepdims=True) + jnp.sum(kf * dn_new, axis=1, keepdims=True)
                da_s = _sum_all(dc_new * cst[h]) + jnp.sum(dn_new * nrow[h], axis=1, keepdims=True)
                wdw = w * dw
                rs = jnp.sum(gm, axis=1, keepdims=True)
                cs_col = _col(t[h]["cs2"], lane, 0)
                dg = a_s * da_s + jnp.sum(wdw, axis=0, keepdims=True)
                db = a * da + rs - cs_col - wdw + jnp.where(rowio == CHUNK - 1, dg, 0.0)
                dli_all = dli_all + jnp.where(lane == h, cs_col + wdw, 0.0)
                db_all = db_all + jnp.where(lane == NH + h, db, 0.0)
                dc_scr[h] = a_s * dc_new + t[h]["dcq"]
                dn_scr[h] = a_s * dn_new + jnp.sum(adden * f[h]["qf"], axis=0, keepdims=True)
            dlf_all = _nn(k["triu_f"], db_all, precision=HI)
            dsm = jnp.where(valid, dli_all + dlf_all * _sigmoid(-pre), 0.0)
            dsm = jnp.where(lane < 2 * NH, dsm, 0.0)
            dsm_ref[rows, :] = dsm
            dgb_ref[0:1, :] += jnp.sum(dsm, axis=0, keepdims=True)
            return carry

        lax.fori_loop(0, CPB, chunk, 0, unroll=2)

    rev = lambda col: (lambda i: (nb - 1 - i, col))
    rspec = lambda width, col=0: pl.BlockSpec((TM, width), rev(col))
    return pl.pallas_call(
        body, grid=(nb,),
        in_specs=[rspec(D), rspec(D), rspec(D, CB_MV), rspec(D, CB_MO), rspec(N_SMALL), _full_spec((1, N_SMALL)), _full_spec((1, D)),
                  pl.BlockSpec((CPB, NH, DV, DQK), lambda i: (nb - 1 - i, 0, 0, 0)),
                  pl.BlockSpec((CPB, NH, 8, DQK), lambda i: (nb - 1 - i, 0, 0, 0)), ANY],
        out_specs=[rspec(D), rspec(2 * D, CB_MV // 2), rspec(N_SMALL), _full_spec((8, N_SMALL)), _full_spec((1, D))],
        out_shape=(jax.ShapeDtypeStruct((tp, D), F32), jax.ShapeDtypeStruct(dproj.shape, BF),
                   jax.ShapeDtypeStruct((tp, N_SMALL), F32), jax.ShapeDtypeStruct((8, N_SMALL), F32),
                   jax.ShapeDtypeStruct((1, D), F32)),
        scratch_shapes=[pltpu.VMEM((NH, DV, DQK), F32), pltpu.VMEM((NH, 1, DQK), F32)],
        input_output_aliases={9: 1}, compiler_params=_params(), name=name)(dy, qk, pbig, pbig, small, gbias, headg, cs, ns, dproj)


def _gla_loga(sm_ref, a2_ref, a2b_ref, blk):
    za = _nn(sm_ref[...].astype(BF), a2_ref[...]) + a2b_ref[...]
    row = blk * TM + lax.broadcasted_iota(jnp.int32, (TM, 1), 0)
    return za, jnp.where(row >= FIRST_VALID, _logsig(za) / G_TAU, 0.0)


def _gla_head(h, q_ref, k_ref, rows, bc, btot, k):
    sl = slice(h * DQK, (h + 1) * DQK)
    bch = bc[:, sl]
    bth = btot[:, sl]
    gq = q_ref[rows, h * DQK:(h + 1) * DQK].astype(F32)
    gk = k_ref[rows, NH * DQK + h * DQK:NH * DQK + (h + 1) * DQK].astype(F32)
    e_pos = jnp.exp(bch) * (DQK ** -0.5)
    e_neg = jnp.exp(-bch)
    e_end = jnp.exp(bth - bch)
    qd = gq * e_pos
    ki = gk * e_neg
    ke = gk * e_end
    att = jnp.where(k["tri"], _nt(qd.astype(BF), ki.astype(BF)), 0.0)
    return dict(e_pos=e_pos, e_neg=e_neg, e_end=e_end, qd=qd, ki=ki, ke=ke, att=att, decay=jnp.exp(bth))


def _gla_fwd(pbig, small, a2p, a2b, headg, name):
    tp = pbig.shape[0]
    nb = tp // TM

    def body(qk_ref, v_ref, gr_ref, sm_ref, a2_ref, a2b_ref, hg_ref, y_ref, ss_ref, s_scr, lg_scr):
        blk = pl.program_id(0)

        @pl.when(blk == 0)
        def _():
            s_scr[...] = jnp.zeros_like(s_scr)

        k = _chunk_consts()
        _, loga = _gla_loga(sm_ref, a2_ref, a2b_ref, blk)
        lg_scr[...] = loga

        def chunk(c, carry):
            r0 = pl.multiple_of(c * CHUNK, CHUNK)
            rows = pl.ds(r0, CHUNK)
            bc = _nn(k["tril_f"], lg_scr[rows, :], precision=HI)
            btot = _last_row(bc, k["rowio"])
            heads = range(NH)
            f = [_gla_head(h, qk_ref, qk_ref, rows, bc, btot, k) for h in heads]
            vs = [v_ref[rows, h * DV:(h + 1) * DV] for h in heads]
            sst = [s_scr[h] for h in heads]
            sbs = [s.astype(BF) for s in sst]
            inter = [_nt(f[h]["qd"].astype(BF), sbs[h]) for h in heads]
            intra = [_nn(f[h]["att"].astype(BF), vs[h]) for h in heads]
            kv = [_tn(vs[h], f[h]["ke"].astype(BF)) for h in heads]
            for h in heads:
                gr = gr_ref[rows, h * DV:(h + 1) * DV].astype(F32)
                y_ref[rows, h * DV:(h + 1) * DV] = _headnorm_fwd(intra[h] + inter[h], hg_ref[:, h * DV:(h + 1) * DV],
                                                                   gr * _sigmoid(gr)).astype(BF)
                ss_ref[c, h] = sbs[h]
                s_scr[h] = sst[h] * f[h]["decay"] + kv[h]
            return carry

        lax.fori_loop(0, CPB, chunk, 0, unroll=2)

    return pl.pallas_call(
        body, grid=(nb,),
        in_specs=[_row_spec(D, CB_GQK), _row_spec(D, CB_GV), _row_spec(D, CB_GR), _row_spec(N_SMALL),
                  _full_spec((N_SMALL, NH * DQK)), _full_spec((1, NH * DQK)), _full_spec((1, D))],
        out_specs=[_row_spec(D), pl.BlockSpec((CPB, NH, DV, DQK), lambda i: (i, 0, 0, 0))],
        out_shape=(jax.ShapeDtypeStruct((tp, D), BF), jax.ShapeDtypeStruct((tp // CHUNK, NH, DV, DQK), BF)),
        scratch_shapes=[pltpu.VMEM((NH, DV, DQK), F32), pltpu.VMEM((TM, NH * DQK), F32)],
        compiler_params=_params(), name=name)(pbig, pbig, pbig, small, a2p, a2b, headg)


def _gla_bwd(dy, pbig, small, a2p, a2b, headg, ss, dsm_m, dproj, name):
    tp = pbig.shape[0]
    nb = tp // TM
    nqk = NH * DQK

    def body(dy_ref, qk_ref, v_ref, gr_ref, sm_ref, a2_ref, a2b_ref, hg_ref, ss_ref, dsmm_ref, _,
             dproj_ref, dsm_ref, da2_ref, da2b_ref, dhg_ref, ds_scr, lg_scr, dza_scr):
        step = pl.program_id(0)
        blk = nb - 1 - step

        @pl.when(step == 0)
        def _():
            ds_scr[...] = jnp.zeros_like(ds_scr)
            da2_ref[...] = jnp.zeros_like(da2_ref)
            da2b_ref[...] = jnp.zeros_like(da2b_ref)
            dhg_ref[...] = jnp.zeros_like(dhg_ref)

        k = _chunk_consts()
        rowio = k["rowio"]
        za, loga = _gla_loga(sm_ref, a2_ref, a2b_ref, blk)
        lg_scr[...] = loga

        def chunk(cc, carry):
            c = CPB - 1 - cc
            r0 = pl.multiple_of(c * CHUNK, CHUNK)
            rows = pl.ds(r0, CHUNK)
            bc = _nn(k["tril_f"], lg_scr[rows, :], precision=HI)
            btot = _last_row(bc, rowio)
            heads = range(NH)
            f = [_gla_head(h, qk_ref, qk_ref, rows, bc, btot, k) for h in heads]
            vs = [v_ref[rows, h * DV:(h + 1) * DV] for h in heads]
            sbs = [ss_ref[c, h] for h in heads]
            qdb = [f[h]["qd"].astype(BF) for h in heads]
            attb = [f[h]["att"].astype(BF) for h in heads]
            inter = [_nt(qdb[h], sbs[h]) for h in heads]
            intra = [_nn(attb[h], vs[h]) for h in heads]
            dsn = [ds_scr[h] for h in heads]
            dsb = [d.astype(BF) for d in dsn]
            dke = [_nn(vs[h], dsb[h]) for h in heads]
            dv1 = [_nt(f[h]["ke"].astype(BF), dsb[h]) for h in heads]
            t = []
            for h in heads:
                gr = gr_ref[rows, h * DV:(h + 1) * DV].astype(F32)
                sg = _sigmoid(gr)
                gain = hg_ref[:, h * DV:(h + 1) * DV]
                do, dact, dgain = _headnorm_bwd(dy_ref[rows, h * DV:(h + 1) * DV].astype(F32), intra[h] + inter[h], gain, gr * sg)
                dproj_ref[rows, 2 * D + h * DV:2 * D + (h + 1) * DV] = (dact * sg * (1.0 + gr * (1.0 - sg))).astype(BF)
                dhg_ref[:, h * DV:(h + 1) * DV] += dgain
                dob = do.astype(BF)
                t.append(dict(dob=dob, datt=_nt(dob, vs[h]), dv0=_tn(attb[h], dob), dq1=_nn(dob, sbs[h]), dsq=_tn(dob, qdb[h])))
            for h in heads:
                datt = jnp.where(k["tri"], t[h]["datt"], 0.0).astype(BF)
                t[h].update(dq0=_nn(datt, f[h]["ki"].astype(BF)), dki=_tn(datt, qdb[h]))
            dbc_parts = []
            for h in heads:
                dqd = t[h]["dq0"] + t[h]["dq1"]
                dki = t[h]["dki"]
                dproj_ref[rows, D + h * DV:D + (h + 1) * DV] = (t[h]["dv0"] + dv1[h]).astype(BF)
                dproj_ref[rows, h * DQK:(h + 1) * DQK] = (dqd * f[h]["e_pos"]).astype(BF)
                dproj_ref[rows, nqk + h * DQK:nqk + (h + 1) * DQK] = (dki * f[h]["e_neg"] + dke[h] * f[h]["e_end"]).astype(BF)
                dke_ke = dke[h] * f[h]["ke"]
                dbtot = (jnp.sum(dke_ke, axis=0, keepdims=True)
                         + jnp.sum(dsn[h] * sbs[h].astype(F32), axis=0, keepdims=True) * f[h]["decay"])
                dbc_parts.append(dqd * f[h]["qd"] - dki * f[h]["ki"] - dke_ke + jnp.where(rowio == CHUNK - 1, dbtot, 0.0))
                ds_scr[h] = dsn[h] * f[h]["decay"] + t[h]["dsq"]
            dbc = jnp.concatenate(dbc_parts, axis=1)
            dza_scr[rows, :] = _nn(k["triu_f"], dbc, precision=HI)
            return carry

        lax.fori_loop(0, CPB, chunk, 0, unroll=2)
        row = blk * TM + lax.broadcasted_iota(jnp.int32, (TM, 1), 0)
        dza = jnp.where(row >= FIRST_VALID, dza_scr[...] * (_sigmoid(-za) / G_TAU), 0.0)
        dzb = dza.astype(BF)
        dsm_ref[...] = (_nt(dzb, a2_ref[...]) + dsmm_ref[...]).astype(BF)
        da2_ref[...] += _tn(sm_ref[...].astype(BF), dzb)
        da2b_ref[...] += jnp.sum(dza, axis=0, keepdims=True)

    rspec = lambda width, col=0: pl.BlockSpec((TM, width), lambda i: (nb - 1 - i, col))
    return pl.pallas_call(
        body, grid=(nb,),
        in_specs=[rspec(D), rspec(D, CB_GQK), rspec(D, CB_GV), rspec(D, CB_GR), rspec(N_SMALL),
                  _full_spec((N_SMALL, nqk)), _full_spec((1, nqk)), _full_spec((1, D)),
                  pl.BlockSpec((CPB, NH, DV, DQK), lambda i: (nb - 1 - i, 0, 0, 0)), rspec(N_SMALL), ANY],
        out_specs=[rspec(3 * D, 0), rspec(N_SMALL), _full_spec((N_SMALL, nqk)), _full_spec((1, nqk)), _full_spec((1, D))],
        out_shape=(jax.ShapeDtypeStruct(dproj.shape, BF),
                   jax.ShapeDtypeStruct((tp, N_SMALL), BF), jax.ShapeDtypeStruct((N_SMALL, nqk), F32),
                   jax.ShapeDtypeStruct((1, nqk), F32), jax.ShapeDtypeStruct((1, D), F32)),
        scratch_shapes=[pltpu.VMEM((NH, DV, DQK), F32), pltpu.VMEM((TM, nqk), F32), pltpu.VMEM((TM, nqk), F32)],
        input_output_aliases={10: 0}, compiler_params=_params(), name=name)(dy, pbig, pbig, pbig, small, a2p, a2b, headg, ss, dsm_m, dproj)


PIECE_BYTES = 1 << 20
MAX_PIECES = 32


def _place():
    return lax.axis_index("x"), lax.axis_index("y"), lax.axis_index("c")


def _piece_rows(rows, row_bytes, align):
    want = min(MAX_PIECES, max(1, -(-rows * row_bytes // PIECE_BYTES)))
    best = rows
    for k in range(1, want + 1):
        if rows % k == 0 and (rows // k) % align == 0:
            best = rows // k
    return best


def _remote(src, dst, send_sems, recv_sems, k, to):
    return pltpu.make_async_remote_copy(src_ref=src, dst_ref=dst, send_sem=send_sems.at[k], recv_sem=recv_sems.at[k],
                                        device_id=to, device_id_type=MESH)


def _all_gather_chips(p, name):
    r, n = p.shape
    rh = r // 2
    align = 32 // p.dtype.itemsize
    assert r % (2 * align) == 0
    cr = _piece_rows(rh, n * p.dtype.itemsize, align)

    def body(p_ref, o_ref, send_sems, recv_sems):
        x, y, c = _place()
        chips = [(1 - x, y), (x, 1 - y), (1 - x, 1 - y)]
        sib = (x, y, 1 - c)

        def half(hc, piece=None):
            if piece is None:
                return pl.ds(pl.multiple_of(hc * rh, align), rh)
            return pl.ds(pl.multiple_of(hc * rh + piece * cr, align), cr)

        first = [_remote(p_ref.at[half(c)], o_ref.at[j, half(c)], send_sems, recv_sems, j, (*chip, c))
                 for j, chip in enumerate(chips)]
        for cp in first:
            cp.start()
        for j, cp in enumerate(first):
            cp.wait_recv()
            for i in range(rh // cr):
                _remote(o_ref.at[j, half(c, i)], o_ref.at[j, half(c, i)], send_sems, recv_sems, 3 + j, sib).start()
        for j in range(3):
            block = _remote(o_ref.at[j, half(c)], o_ref.at[j, half(1 - c)], send_sems, recv_sems, 3 + j, sib)
            block.wait_send()
            block.wait_recv()
        for cp in first:
            cp.wait_send()

    return pl.pallas_call(
        body, in_specs=[ANY], out_specs=ANY, out_shape=jax.ShapeDtypeStruct((3, r, n), p.dtype),
        scratch_shapes=[pltpu.SemaphoreType.DMA((6,)), pltpu.SemaphoreType.DMA((6,))],
        name=name)(p)


def _by_chip(mine, others):
    me = 2 * lax.axis_index("x") + lax.axis_index("y")
    by_mask = jnp.stack([mine, others[1], others[0], others[2]])
    return [lax.dynamic_index_in_dim(by_mask, q ^ me, 0, keepdims=False) for q in range(4)]


def _swap_halves(items, name):
    k = len(items)

    def body(*refs):
        a_refs, got_refs = refs[:k], refs[k:2 * k]
        send_sems, recv_sems = refs[2 * k:]
        x, y, c = _place()
        sib = (x, y, 1 - c)
        for i, a in enumerate(items):
            _, r, n = a.shape
            rh = r // 2
            cr = _piece_rows(rh, n * a.dtype.itemsize, 8)
            for q in range(4):
                for t in range(rh // cr):
                    other = pl.ds(pl.multiple_of((1 - c) * rh + t * cr, 8), cr)
                    _remote(a_refs[i].at[q, other], got_refs[i].at[q, pl.ds(t * cr, cr)], send_sems, recv_sems, i, sib).start()
        for i, a in enumerate(items):
            block = _remote(a_refs[i].at[:, pl.ds(0, a.shape[1] // 2)], got_refs[i], send_sems, recv_sems, i, sib)
            block.wait_send()
            block.wait_recv()

    return pl.pallas_call(
        body, in_specs=[ANY] * k, out_specs=[ANY] * k,
        out_shape=tuple(jax.ShapeDtypeStruct((4, a.shape[1] // 2, a.shape[2]), a.dtype) for a in items),
        scratch_shapes=[pltpu.SemaphoreType.DMA((k,)), pltpu.SemaphoreType.DMA((k,))], name=name)(*items)


def _scatter_chips(items, name):
    k = len(items)

    def body(*refs):
        s_refs, o_refs = refs[:k], refs[k:2 * k]
        send_sems, recv_sems = refs[2 * k:]
        x, y, c = _place()
        chips = [(1 - x, y), (x, 1 - y), (1 - x, 1 - y)]
        sent = []
        for i in range(k):
            for j, (cx, cy) in enumerate(chips):
                cp = _remote(s_refs[i].at[2 * cx + cy], o_refs[i].at[j], send_sems, recv_sems, 3 * i + j, (cx, cy, c))
                cp.start()
                sent.append(cp)
        for cp in sent:
            cp.wait_recv()
        for cp in sent:
            cp.wait_send()

    return pl.pallas_call(
        body, in_specs=[ANY] * k, out_specs=[ANY] * k,
        out_shape=tuple(jax.ShapeDtypeStruct((3,) + s.shape[1:], s.dtype) for s in items),
        scratch_shapes=[pltpu.SemaphoreType.DMA((3 * k,)), pltpu.SemaphoreType.DMA((3 * k,))], name=name)(*items)


def _join_halves(items, name):
    k = len(items)

    def body(*refs):
        f_refs, o_refs = refs[:k], refs[k:2 * k]
        send_sems, recv_sems = refs[2 * k:]
        x, y, c = _place()
        sib = (x, y, 1 - c)
        for i, f in enumerate(items):
            rh, n = f.shape
            cr = _piece_rows(rh, n * f.dtype.itemsize, 8)
            for t in range(rh // cr):
                rows = pl.ds(t * cr, cr)
                _remote(f_refs[i].at[rows], o_refs[i].at[rows], send_sems, recv_sems, i, sib).start()
        for i in range(k):
            block = _remote(f_refs[i], o_refs[i], send_sems, recv_sems, i, sib)
            block.wait_send()
            block.wait_recv()

    return pl.pallas_call(
        body, in_specs=[ANY] * k, out_specs=[ANY] * k, out_shape=tuple(jax.ShapeDtypeStruct(f.shape, f.dtype) for f in items),
        scratch_shapes=[pltpu.SemaphoreType.DMA((k,)), pltpu.SemaphoreType.DMA((k,))], name=name)(*items)


SMALL_ROWS = 16
SMALL_SHARD_SHAPES = [(N_META, 256), (4, 256), (G_RANK, 128), (NH, 64), (NH, 64)]
REPL_SHAPES = [(1, D), (1, D), (1, 2, NH), (1, NH * DQK), (1, D), (D,)]
W_IN_SHARD = 2054


def _pack_small(parts):
    flat = jnp.concatenate([p.reshape(-1) for p in parts])
    return jnp.pad(flat, (0, SMALL_ROWS * D - flat.shape[0])).reshape(SMALL_ROWS, D)


def _unpack_small(block, shapes):
    flat, out, off = block.reshape(-1), [], 0
    for shp in shapes:
        n = 1
        for s in shp:
            n *= s
        out.append(flat[off:off + n].reshape(shp))
        off += n
    return out


def _proj_cols_from_w_in(w_in_f):
    w_big = jnp.concatenate([w_in_f[:, 3080:5128], w_in_f[:, 5144:6168], w_in_f[:, 0:1024], w_in_f[:, 6168:8216],
                             w_in_f[:, 1024:2048], w_in_f[:, 2056:3080]], axis=1)
    w_small = jnp.concatenate([w_in_f[:, 2048:2056], w_in_f[:, 5128:5144], jnp.zeros((D, N_SMALL - 24), w_in_f.dtype)], axis=1)
    return w_big, w_small


def _w_in_from_proj_cols(d_wall):
    big, small = d_wall[:, 0:N_BIG], d_wall[:, N_BIG:N_ALL]
    return jnp.concatenate([big[:, 3072:4096], big[:, 6144:7168], small[:, 0:8], big[:, 7168:8192], big[:, 0:2048],
                            small[:, 8:24], big[:, 2048:3072], big[:, 4096:6144]], axis=1)


def kernel(x, meta_tokens, norm1_g, w_in, conv_w, conv_b, m_gate_b, g_a2, g_a2_b, m_head_g, g_head_g, w_branch_m, w_branch_g, w_out, norm2_g, w_ff_gate, w_ff_up, w_ff_down, final_g, loss_target, m_meta_tokens, m_norm1_g, m_w_in, m_conv_w, m_conv_b, m_m_gate_b, m_g_a2, m_g_a2_b, m_m_head_g, m_g_head_g, m_w_branch_m, m_w_branch_g, m_w_out, m_norm2_g, m_w_ff_gate, m_w_ff_up, m_w_ff_down, m_final_g, v_meta_tokens, v_norm1_g, v_w_in, v_conv_w, v_conv_b, v_m_gate_b, v_g_a2, v_g_a2_b, v_m_head_g, v_g_head_g, v_w_branch_m, v_w_branch_g, v_w_out, v_norm2_g, v_w_ff_gate, v_w_ff_up, v_w_ff_down, v_final_g):
    w = _gather_weights(w_in, w_branch_m, w_branch_g, w_out, w_ff_gate, w_ff_up, w_ff_down, meta_tokens, conv_w, g_a2, m_head_g, g_head_g)
    loss_local, dx, local = _local_step(x[0], loss_target[0], w, norm1_g, conv_b, m_gate_b, g_a2_b, norm2_g, final_g)
    grads = _reduce_grads(local)

    weights = [w_in, w_branch_m, w_branch_g, w_out, w_ff_gate, w_ff_up, w_ff_down, meta_tokens, conv_w, g_a2, m_head_g, g_head_g,
               norm1_g, conv_b, m_gate_b, g_a2_b, norm2_g, final_g]
    moms = [m_w_in, m_w_branch_m, m_w_branch_g, m_w_out, m_w_ff_gate, m_w_ff_up, m_w_ff_down, m_meta_tokens, m_conv_w, m_g_a2,
            m_m_head_g, m_g_head_g, m_norm1_g, m_conv_b, m_m_gate_b, m_g_a2_b, m_norm2_g, m_final_g]
    vels = [v_w_in, v_w_branch_m, v_w_branch_g, v_w_out, v_w_ff_gate, v_w_ff_up, v_w_ff_down, v_meta_tokens, v_conv_w, v_g_a2,
            v_m_head_g, v_g_head_g, v_norm1_g, v_conv_b, v_m_gate_b, v_g_a2_b, v_norm2_g, v_final_g]
    res = {}
    for nm, wt, g, m, v in zip(PACK_ORDER, weights, grads, moms, vels):
        two_d = (wt.size // wt.shape[-1], wt.shape[-1])
        d, nm_, nv_ = _adamw(wt.reshape(two_d), g.reshape(two_d), m.reshape(two_d), v.reshape(two_d), "adamw_" + nm)
        res[nm] = (g.reshape(wt.shape), d.reshape(wt.shape), nm_.reshape(wt.shape), nv_.reshape(wt.shape))

    order = ["meta_tokens", "norm1_g", "w_in", "conv_w", "conv_b", "m_gate_b", "g_a2", "g_a2_b", "m_head_g", "g_head_g",
             "w_branch_m", "w_branch_g", "w_out", "norm2_g", "w_ff_gate", "w_ff_up", "w_ff_down", "final_g"]
    loss = lax.psum(loss_local[0, 0], ("x", "y", "c"))
    grad_x = dx.reshape(x.shape)
    return (loss, grad_x, *[res[n][0] for n in order], *[res[n][1] for n in order],
            *[res[n][2] for n in order], *[res[n][3] for n in order])


PACK_ORDER = ["w_in", "w_branch_m", "w_branch_g", "w_out", "w_ff_gate", "w_ff_up", "w_ff_down", "meta_tokens", "conv_w", "g_a2",
              "m_head_g", "g_head_g", "norm1_g", "conv_b", "m_gate_b", "g_a2_b", "norm2_g", "final_g"]


def _gather_weights(w_in, w_branch_m, w_branch_g, w_out, w_ff_gate, w_ff_up, w_ff_down, meta_tokens, conv_w, g_a2, m_head_g, g_head_g):
    bf = lambda a: a.astype(BF)
    rows_local = jnp.concatenate([bf(w_branch_m[0]), bf(w_branch_g[0]), bf(w_out[0]), bf(w_ff_down[0]),
                                  bf(w_ff_gate[0].T), bf(w_ff_up[0].T)], axis=0)
    win_local = bf(w_in[0])
    small_local = _pack_small([meta_tokens, conv_w[0], g_a2[0], m_head_g[0], g_head_g[0]])
    rows_all = jnp.stack(_by_chip(rows_local, _all_gather_chips(rows_local, "gather_rows")))
    win_all = _by_chip(win_local, _all_gather_chips(win_local, "gather_w_in"))
    small_all = _by_chip(small_local, _all_gather_chips(small_local, "gather_small"))
    cut = lambda lo, hi: rows_all[:, lo:hi].reshape(4 * (hi - lo), D)
    wbm, wbg, wout, wdown = cut(0, 256), cut(256, 512), cut(512, 768), cut(768, 1472)
    wgu_t = _ffn_weight_rows(cut(1472, 2176), cut(2176, 2880))
    w_in_f = jnp.concatenate([win_all[q] for q in range(4)], axis=1)
    small_sh = [_unpack_small(small_all[q], SMALL_SHARD_SHAPES) for q in range(4)]
    cat = lambda i: jnp.concatenate([s[i] for s in small_sh], axis=-1)
    return dict(w_in=w_in_f, wbm=wbm, wbg=wbg, wout=wout, wgu_t=wgu_t, wdown=wdown, meta=cat(0), convw=cat(1), ga2=cat(2),
                mhg=cat(3).reshape(1, D), ghg=cat(4).reshape(1, D))


def _local_step(x0, target, w, norm1_g, conv_b, m_gate_b, g_a2_b, norm2_g, final_g):
    w_in_f, wbm, wbg, wout, wgu_t, wdown = w["w_in"], w["wbm"], w["wbg"], w["wout"], w["wgu_t"], w["wdown"]
    meta_f, convw_f, ga2_f, mhg_f, ghg_f = w["meta"], w["convw"], w["ga2"], w["mhg"], w["ghg"]
    w_big, w_small = _proj_cols_from_w_in(w_in_f)
    w_all = jnp.concatenate([w_big, w_small], axis=1)
    gbias = jnp.concatenate([m_gate_b.reshape(1, 2 * NH), jnp.zeros((1, N_SMALL - 2 * NH), F32)], axis=1)
    a2p = jnp.concatenate([jnp.zeros((8, NH * DQK), F32), ga2_f, jnp.zeros((N_SMALL - 24, NH * DQK), F32)], axis=0).astype(BF)
    convb = conv_b.reshape(1, D)
    g1 = norm1_g.reshape(1, D)
    g2 = norm2_g.reshape(1, D)
    gf = final_g.reshape(1, D)
    h0 = jnp.concatenate([jnp.zeros((FIRST_VALID, D), F32), meta_f, x0], axis=0)

    xn1, rstd1 = _rms_fwd(h0, g1, "rms1")
    pbig = _mm(xn1, w_big, nt=False, out_dtype=BF, tn=1024, name="proj_big")
    small = _mm(xn1, w_small, nt=False, out_dtype=F32, tn=N_SMALL, name="proj_small")
    qk = _conv_fwd(pbig, convw_f, convb, "conv_fwd")
    y_m, m_cs, m_ns = _mlstm_fwd(qk, pbig, small, gbias, mhg_f, "mlstm_fwd")
    y_g, g_ss = _gla_fwd(pbig, small, a2p, g_a2_b, ghg_f, "gla_fwd")
    p_m, p_g, merged = _branch_merge(y_m, y_g, wbm, wbg, pbig, "branch_merge")
    h1, hn, rstd2 = _out_proj_norm(merged, wout, h0, g2, "out_proj")
    gu, ff = _ffn_in(hn, wgu_t, "ff_in")
    dh2, loss_local, d_final_g = _ffn_down_loss(ff, wdown, h1, target, gf, "ff_down_loss")

    d_wdown = _mm_tn(ff, dh2, tm=1408, tn=1024, name="dw_ff_down")
    dgu = _ffn_d_hidden(dh2, wdown, gu, "d_ff")
    d_wgu_t = _mm_tn(dgu, hn, tm=1408, tn=1024, name="dw_ff_in")
    dh1, d_g2 = _ffn_d_in(dgu, wgu_t, h1, rstd2, g2, dh2, "d_hn")
    d_wout = _mm_tn(merged, dh1, tm=1024, tn=1024, name="dw_out")
    dp_m, dp_g, dproj = _merge_d(dh1, wout, p_m, p_g, pbig, "d_merged")
    dy_m = _mm(dp_m, wbm, nt=True, out_dtype=BF, tn=1024, name="d_ym")
    dy_g = _mm(dp_g, wbg, nt=True, out_dtype=BF, tn=1024, name="d_yg")
    d_wbm = _mm_tn(y_m, dp_m, tm=1024, tn=1024, name="dw_branch_m")
    d_wbg = _mm_tn(y_g, dp_g, tm=1024, tn=1024, name="dw_branch_g")
    dqk_m, dproj, dsm_m, d_gbias, d_mhg = _mlstm_bwd(dy_m, qk, pbig, small, gbias, mhg_f, m_cs, m_ns, dproj, "mlstm_bwd")
    dconv, d_convwb = _conv_bwd_pre(dqk_m, pbig, convw_f, convb, "conv_bwd_pre")
    dproj = _conv_bwd_in(dconv, convw_f, dproj, "conv_bwd_in")
    dproj, dsmall, d_a2p, d_a2b, d_ghg = _gla_bwd(dy_g, pbig, small, a2p, g_a2_b, ghg_f, g_ss, dsm_m, dproj, "gla_bwd")
    dproj = _place_small(dsmall, dproj, "dproj_small")
    dxn = _mm(dproj, w_all, nt=True, out_dtype=F32, tn=1024, tk=1664, name="d_xn")
    d_wall = _mm_tn(xn1, dproj, tm=1024, tn=1664, name="dw_in")
    dh_first, dx, d_g1 = _rms_bwd(dxn, h0, rstd1, g1, dh1, "rms1_bwd", split_first=True)

    small_sharded = [dh_first[FIRST_VALID:TM], d_convwb[0:4], d_a2p[8:24], d_mhg.reshape(NH, DV), d_ghg.reshape(NH, DV)]
    replicated = [d_g1, d_convwb[4:5], d_gbias[0:1, 0:2 * NH].reshape(1, 2, NH), d_a2b, d_g2, d_final_g.reshape(D)]
    local = dict(w_all=d_wall, wbm=d_wbm, wbg=d_wbg, wout=d_wout, wdown=d_wdown, wgu_t=d_wgu_t,
                 small_sharded=small_sharded, replicated=replicated)
    return loss_local, dx, local


def _reduce_grads(local):
    d_win = _w_in_from_proj_cols(local["w_all"])
    win4 = jnp.stack([d_win[:, q * W_IN_SHARD:(q + 1) * W_IN_SHARD] for q in range(4)])
    small4 = jnp.stack([_pack_small([g[:, q * shp[1]:(q + 1) * shp[1]] for g, shp in zip(local["small_sharded"], SMALL_SHARD_SHAPES)]
                                    + local["replicated"]) for q in range(4)])
    fq = D_FF // 4
    gu4 = jnp.transpose(local["wgu_t"].reshape(2, 2, 2, fq, D), (0, 2, 1, 3, 4)).reshape(4, 2 * fq, D)
    sq4 = jnp.concatenate([local["wbm"].reshape(4, 256, D), local["wbg"].reshape(4, 256, D), local["wout"].reshape(4, 256, D)], axis=1)
    items = [win4, sq4, local["wdown"].reshape(4, fq, D), gu4, small4]
    c = lax.axis_index("c")
    me = 2 * lax.axis_index("x") + lax.axis_index("y")
    got = _swap_halves(items, "reduce_siblings")
    sums = []
    for i, (a, g) in enumerate(zip(items, got)):
        rh, n = g.shape[1], g.shape[2]
        own = lax.dynamic_slice_in_dim(a, c * rh, rh, axis=1)
        sums.append(_add2(own.reshape(-1, n), g.reshape(-1, n), F32 if i == len(items) - 1 else BF, f"reduce_add2_{i}").reshape(g.shape))
    from_chips = _scatter_chips(sums, "reduce_chips")
    halves = []
    for i, (s, f) in enumerate(zip(sums, from_chips)):
        mine = lax.dynamic_index_in_dim(s, me, 0, keepdims=False)
        if i == len(items) - 1:
            by_chip = _by_chip(mine, f)
            mine, f = by_chip[0], jnp.stack(by_chip[1:])
        halves.append(_add4(mine, f, f"reduce_add4_{i}"))
    got = _join_halves(halves, "reduce_join")
    full = [jnp.where(c == 0, jnp.concatenate([h, g], axis=0), jnp.concatenate([g, h], axis=0)) for h, g in zip(halves, got)]
    smalls = _unpack_small(full[4], SMALL_SHARD_SHAPES + REPL_SHAPES)
    return [full[0], full[1][0:256], full[1][256:512], full[1][512:768], full[3][0:fq].T, full[3][fq:2 * fq].T, full[2]] + smalls
```

```python
import functools

import jax
import jax.numpy as jnp
from jax import lax
from jax.experimental import pallas as pl
from jax.experimental.pallas import tpu as pltpu

F32 = jnp.float32
BF = jnp.bfloat16
HI = lax.Precision.HIGHEST
MESH = pl.DeviceIdType.MESH

D = 1024
N_META = 16
CHUNK = 128
EPS = 1e-6
NH = 4
DV = 256
DQK = 128
G_RANK = 16
G_TAU = 16.0
D_FF = 2816
TM = 512
FIRST_VALID = TM - N_META
CPB = TM // CHUNK
NEG = -1e30
N_BIG = 8192
CB_GQK, CB_GV, CB_GR, CB_MQK, CB_GM, CB_GG, CB_MV, CB_MO = range(8)
N_SMALL = 128
N_ALL = N_BIG + N_SMALL
VMEM_LIMIT = 56 * 1024 * 1024

ADAM_LR, ADAM_B1, ADAM_B2, ADAM_EPS, ADAM_WD, ADAM_STEP = 0.001, 0.9, 0.999, 1e-08, 0.01, 10

NT_DIMS = (((1,), (1,)), ((), ()))
TN_DIMS = (((0,), (0,)), ((), ()))


def _nt(a, b, **kw):
    return lax.dot_general(a, b, NT_DIMS, preferred_element_type=F32, **kw)


def _tn(a, b, **kw):
    return lax.dot_general(a, b, TN_DIMS, preferred_element_type=F32, **kw)


def _nn(a, b, **kw):
    return jnp.dot(a, b, preferred_element_type=F32, **kw)


def _params(**kw):
    return pltpu.CompilerParams(vmem_limit_bytes=VMEM_LIMIT, **kw)


def _sigmoid(x):
    return 0.5 * jnp.tanh(0.5 * x) + 0.5


def _logsig(x):
    return jnp.minimum(x, 0.0) - jnp.log(1.0 + jnp.exp(-jnp.abs(x)))


def _mm_rows(rows):
    return 3 * TM if rows % (3 * TM) == 0 else TM


def _mm(a, b, *, nt, out_dtype, tn, tk=None, tm=None, name, ride=()):
    m, k = a.shape
    n = b.shape[0] if nt else b.shape[1]
    tk = k if tk is None else tk
    tm = _mm_rows(m) if tm is None else tm
    nk = k // tk
    nj, ni, nr = n // tn, m // tm, len(ride)
    assert m % tm == 0 and n % tn == 0 and k % tk == 0
    dims = NT_DIMS if nt else (((1,), (0,)), ((), ()))

    def body(*refs):
        a_ref, b_ref = refs[:2]
        ride_in, o_ref, ride_out = refs[2:2 + nr], refs[2 + nr], refs[3 + nr:3 + 2 * nr]
        j, i, kk = pl.program_id(0), pl.program_id(1), pl.program_id(2)
        if nr:
            sent = _scatter_copies(ride_in, ride_out, *refs[3 + 2 * nr:5 + 2 * nr])

            @pl.when(jnp.logical_and(jnp.logical_and(j == 0, i == 0), kk == 0))
            def _():
                for cp in sent:
                    cp.start()

        part = lax.dot_general(a_ref[...].astype(BF), b_ref[...].astype(BF), dims, preferred_element_type=F32)
        if nk == 1:
            o_ref[...] = part.astype(o_ref.dtype)
        else:
            acc_ref = refs[-1]

            @pl.when(kk == 0)
            def _():
                acc_ref[...] = part

            @pl.when(jnp.logical_and(kk > 0, kk < nk - 1))
            def _():
                acc_ref[...] += part

            @pl.when(kk == nk - 1)
            def _():
                o_ref[...] = (acc_ref[...] + part).astype(o_ref.dtype)

        if nr:
            @pl.when(jnp.logical_and(jnp.logical_and(j == nj - 1, i == ni - 1), kk == nk - 1))
            def _():
                for cp in sent:
                    cp.wait_recv()
                for cp in sent:
                    cp.wait_send()

    ride_shapes, ride_sems = _scatter_shapes(ride) if nr else ((), [])
    outs = pl.pallas_call(
        body, grid=(nj, ni, nk),
        in_specs=[pl.BlockSpec((tm, tk), lambda j, i, kk: (i, kk)),
                  pl.BlockSpec((tn, tk), lambda j, i, kk: (j, kk)) if nt else pl.BlockSpec((tk, tn), lambda j, i, kk: (kk, j))]
                 + [ANY] * nr,
        out_specs=[pl.BlockSpec((tm, tn), lambda j, i, kk: (i, j))] + [ANY] * nr,
        out_shape=(jax.ShapeDtypeStruct((m, n), out_dtype),) + tuple(ride_shapes),
        scratch_shapes=ride_sems + ([pltpu.VMEM((tm, tn), F32)] if nk > 1 else []),
        compiler_params=_params(), name=name)(a, b, *ride)
    return (outs[0], list(outs[1:])) if nr else outs[0]


def _mm_tn(a, b, *, tm, tn, tk=None, name):
    t, m = a.shape
    n = b.shape[1]
    tk = _mm_rows(t) if tk is None else tk
    assert t % tk == 0 and m % tm == 0 and n % tn == 0

    def body(a_ref, b_ref, o_ref):
        part = _tn(a_ref[...].astype(BF), b_ref[...].astype(BF))

        @pl.when(pl.program_id(2) == 0)
        def _():
            o_ref[...] = part

        @pl.when(pl.program_id(2) > 0)
        def _():
            o_ref[...] += part

    return pl.pallas_call(
        body, grid=(m // tm, n // tn, t // tk),
        in_specs=[pl.BlockSpec((tk, tm), lambda i, j, kk: (kk, i)), pl.BlockSpec((tk, tn), lambda i, j, kk: (kk, j))],
        out_specs=pl.BlockSpec((tm, tn), lambda i, j, kk: (i, j)),
        out_shape=jax.ShapeDtypeStruct((m, n), F32), compiler_params=_params(), name=name)(a, b)


ANY = pl.BlockSpec(memory_space=pl.ANY)


def _row_spec(width, col=0):
    return pl.BlockSpec((TM, width), lambda i: (i, col))


def _full_spec(shape):
    return pl.BlockSpec(shape, lambda i: (0,) * len(shape))


def _rms_fwd(h, g, name):
    tp = h.shape[0]

    def body(h_ref, g_ref, xn_ref, r_ref):
        x = h_ref[...]
        r = lax.rsqrt(jnp.mean(x * x, axis=1, keepdims=True) + EPS)
        xn_ref[...] = (x * r * g_ref[...]).astype(BF)
        r_ref[...] = r

    return pl.pallas_call(
        body, grid=(tp // TM,), in_specs=[_row_spec(D), _full_spec((1, D))],
        out_specs=[_row_spec(D), _row_spec(1)],
        out_shape=(jax.ShapeDtypeStruct((tp, D), BF), jax.ShapeDtypeStruct((tp, 1), F32)),
        compiler_params=_params(), name=name)(h, g)


def _rms_bwd(dxn, h, rstd, g, dres, name, split_first=False):
    tp = h.shape[0]

    def body(dxn_ref, h_ref, r_ref, g_ref, dres_ref, *outs):
        r = r_ref[...]
        xh = h_ref[...] * r
        dxn_v = dxn_ref[...].astype(F32)
        dxh = dxn_v * g_ref[...]
        dh = r * (dxh - xh * jnp.mean(dxh * xh, axis=1, keepdims=True)) + dres_ref[...]
        if split_first:
            first_ref, dh_ref, dg_ref = outs

            @pl.when(pl.program_id(0) == 0)
            def _():
                first_ref[...] = dh
        else:
            dh_ref, dg_ref = outs
        dh_ref[...] = dh
        part = jnp.sum(dxn_v * xh, axis=0, keepdims=True)

        @pl.when(pl.program_id(0) == 0)
        def _():
            dg_ref[...] = part

        @pl.when(pl.program_id(0) > 0)
        def _():
            dg_ref[...] += part

    if split_first:
        out_specs = [_full_spec((TM, D)), pl.BlockSpec((TM, D), lambda i: (jnp.maximum(i - 1, 0), 0)), _full_spec((1, D))]
        out_shape = (jax.ShapeDtypeStruct((TM, D), F32), jax.ShapeDtypeStruct((tp - TM, D), F32), jax.ShapeDtypeStruct((1, D), F32))
    else:
        out_specs = [_row_spec(D), _full_spec((1, D))]
        out_shape = (jax.ShapeDtypeStruct((tp, D), F32), jax.ShapeDtypeStruct((1, D), F32))
    return pl.pallas_call(
        body, grid=(tp // TM,),
        in_specs=[_row_spec(D), _row_spec(D), _row_spec(1), _full_spec((1, D)), _row_spec(D)],
        out_specs=out_specs, out_shape=out_shape, compiler_params=_params(), name=name)(dxn, h, rstd, g, dres)


def _shift_down(x, halo, k):
    rk = pltpu.roll(x, k, 0)
    io = lax.broadcasted_iota(jnp.int32, (8, x.shape[1]), 0)
    top = jnp.where(io < k, pltpu.roll(halo, k, 0), rk[0:8])
    return jnp.concatenate([top, rk[8:]], axis=0)


def _shift_up(x, nxt, k):
    n = x.shape[0]
    rk = pltpu.roll(x, n - k, 0)
    io = lax.broadcasted_iota(jnp.int32, (8, x.shape[1]), 0)
    bot = jnp.where(io >= 8 - k, pltpu.roll(nxt, 8 - k, 0), rk[n - 8:n])
    return jnp.concatenate([rk[:n - 8], bot], axis=0)


def _conv_pre(x, halo, w_ref, b_ref):
    c = x * w_ref[3:4, :] + b_ref[...]
    shifted = []
    for k in (1, 2, 3):
        s = _shift_down(x, halo, k)
        shifted.append(s)
        c = c + s * w_ref[3 - k:4 - k, :]
    return c, shifted


def _qk_scale():
    col = lax.broadcasted_iota(jnp.int32, (1, D), 1)
    return jnp.where(col < NH * DQK, DQK ** -0.5, 1.0).astype(F32)


def _halo_prev_spec():
    return pl.BlockSpec((8, D), lambda i: (jnp.maximum(i * (TM // 8) - 1, 0), CB_MQK))


def _conv_fwd(pbig, w, b, name):
    tp = pbig.shape[0]

    def body(x_ref, halo_ref, w_ref, b_ref, o_ref):
        x = x_ref[...].astype(F32)
        halo = jnp.where(pl.program_id(0) > 0, halo_ref[...].astype(F32), 0.0)
        c, _ = _conv_pre(x, halo, w_ref, b_ref)
        o_ref[...] = (c * _sigmoid(c) * _qk_scale()).astype(BF)

    return pl.pallas_call(
        body, grid=(tp // TM,),
        in_specs=[_row_spec(D, CB_MQK), _halo_prev_spec(), _full_spec((4, D)), _full_spec((1, D))],
        out_specs=_row_spec(D), out_shape=jax.ShapeDtypeStruct((tp, D), BF),
        compiler_params=_params(), name=name)(pbig, pbig, w, b)


def _conv_bwd_pre(dqk, pbig, w, b, name):
    tp = pbig.shape[0]

    def body(d_ref, x_ref, halo_ref, w_ref, b_ref, dc_ref, dwb_ref):
        x = x_ref[...].astype(F32)
        halo = jnp.where(pl.program_id(0) > 0, halo_ref[...].astype(F32), 0.0)
        c, shifted = _conv_pre(x, halo, w_ref, b_ref)
        sg = _sigmoid(c)
        dc = d_ref[...] * _qk_scale() * (sg * (1.0 + c * (1.0 - sg)))
        dc_ref[...] = dc
        taps = [shifted[2], shifted[1], shifted[0], x]
        rows = [jnp.sum(dc * t, axis=0, keepdims=True) for t in taps] + [jnp.sum(dc, axis=0, keepdims=True)]
        io = lax.broadcasted_iota(jnp.int32, (8, D), 0)
        part = jnp.zeros((8, D), F32)
        for r, v in enumerate(rows):
            part = jnp.where(io == r, v, part)

        @pl.when(pl.program_id(0) == 0)
        def _():
            dwb_ref[...] = part

        @pl.when(pl.program_id(0) > 0)
        def _():
            dwb_ref[...] += part

    return pl.pallas_call(
        body, grid=(tp // TM,),
        in_specs=[_row_spec(D), _row_spec(D, CB_MQK), _halo_prev_spec(), _full_spec((4, D)), _full_spec((1, D))],
        out_specs=[_row_spec(D), _full_spec((8, D))],
        out_shape=(jax.ShapeDtypeStruct((tp, D), F32), jax.ShapeDtypeStruct((8, D), F32)),
        compiler_params=_params(), name=name)(dqk, pbig, pbig, w, b)


def _conv_bwd_in(dc, w, dproj, name):
    tp = dc.shape[0]
    nb = tp // TM

    def body(d_ref, nxt_ref, w_ref, _, o_ref):
        d = d_ref[...]
        nxt = jnp.where(pl.program_id(0) < nb - 1, nxt_ref[...], 0.0)
        acc = d * w_ref[3:4, :]
        for k in (1, 2, 3):
            acc = acc + _shift_up(d, nxt, k) * w_ref[3 - k:4 - k, :]
        o_ref[...] = acc.astype(BF)

    return pl.pallas_call(
        body, grid=(nb,),
        in_specs=[_row_spec(D), pl.BlockSpec((8, D), lambda i: (jnp.minimum((i + 1) * (TM // 8), tp // 8 - 1), 0)),
                  _full_spec((4, D)), ANY],
        out_specs=_row_spec(D, CB_MQK), out_shape=jax.ShapeDtypeStruct(dproj.shape, BF),
        input_output_aliases={3: 0}, compiler_params=_params(), name=name)(dc, dc, w, dproj)


def _mm_fused(inputs, products, *, nt, m, n, tm, tn, outs, epilogue, name, nk=1, sub=None):
    dims = NT_DIMS if nt else (((1,), (0,)), ((), ()))
    nin = len(inputs)
    assert nk == 1 or (len(products) == 1 and sub is None)

    def body(*refs):
        in_refs, out_refs = refs[:nin], refs[nin:nin + len(outs)]
        i = pl.program_id(1)
        if sub is not None:
            lhs = {ia: in_refs[ia][...].astype(BF) for ia, _ in products}

            def dots(cols):
                return [lax.dot_general(lhs[ia], (in_refs[ib][cols, :] if nt else in_refs[ib][:, cols]).astype(BF),
                                        dims, preferred_element_type=F32) for ia, ib in products]

            slices = [slice(s, min(s + sub, tn)) for s in range(0, tn, sub)]
            prods = dots(slices[0])
            for idx, cols in enumerate(slices):
                nxt = dots(slices[idx + 1]) if idx + 1 < len(slices) else None
                epilogue(prods, in_refs, out_refs, i, cols)
                prods = nxt
            return
        prods = [lax.dot_general(in_refs[ia][...].astype(BF), in_refs[ib][...].astype(BF), dims, preferred_element_type=F32)
                 for ia, ib in products]
        if nk == 1:
            epilogue(prods, in_refs, out_refs, i, slice(None))
            return
        acc_ref = refs[-1]
        kk = pl.program_id(2)

        @pl.when(kk == 0)
        def _():
            acc_ref[...] = prods[0]

        @pl.when(jnp.logical_and(kk > 0, kk < nk - 1))
        def _():
            acc_ref[...] += prods[0]

        @pl.when(kk == nk - 1)
        def _():
            epilogue([acc_ref[...] + prods[0]], in_refs, out_refs, i, slice(None))

    return pl.pallas_call(
        body, grid=(n // tn, m // tm, nk), in_specs=[s for _, s in inputs], out_specs=[s for _, s in outs],
        out_shape=tuple(sh for sh, _ in outs), scratch_shapes=[pltpu.VMEM((tm, tn), F32)] if nk > 1 else [],
        compiler_params=_params(), name=name)(*[a for a, _ in inputs])


SUB_COLS = 256


def _cols_at(cols, offset):
    return slice(cols.start + offset, cols.stop + offset)


def _blk(rows, width, col=None, row=None):
    return pl.BlockSpec((rows, width), lambda j, i, kk: ((i if row is None else row(i)), (0 if col is None else col(j, kk))))


FF_TN = D_FF // 2


def _ffn_weight_rows(wg_t, wu_t):
    return jnp.concatenate([wg_t[0:FF_TN], wu_t[0:FF_TN], wg_t[FF_TN:], wu_t[FF_TN:]], axis=0)


def _ffn_in(hn, wgu_t, name):
    tp = hn.shape[0]
    tm = _mm_rows(tp)

    def epilogue(prods, in_refs, out_refs, i, cols):
        g, u = prods
        out_refs[0][:, cols] = g.astype(BF)
        out_refs[0][:, _cols_at(cols, FF_TN)] = u.astype(BF)
        out_refs[1][:, cols] = (g * _sigmoid(g) * u).astype(BF)

    wspec = lambda off: pl.BlockSpec((FF_TN, D), lambda j, i, kk: (2 * j + off, 0))
    return _mm_fused(
        [(hn, _blk(tm, D)), (wgu_t, wspec(0)), (wgu_t, wspec(1))], [(0, 1), (0, 2)], nt=True, m=tp, n=D_FF, tm=tm, tn=FF_TN,
        outs=[(jax.ShapeDtypeStruct((tp, 2 * D_FF), BF), _blk(tm, 2 * FF_TN, lambda j, kk: j)),
              (jax.ShapeDtypeStruct((tp, D_FF), BF), _blk(tm, FF_TN, lambda j, kk: j))],
        epilogue=epilogue, name=name, sub=SUB_COLS)


def _ffn_down_loss(ff, wdown, h1, target, gf, name):
    tp = ff.shape[0]

    def epilogue(prods, in_refs, out_refs, i, cols):
        live = (i > 0).astype(F32)
        g = in_refs[4][...]
        x = prods[0] + in_refs[2][...]
        r = lax.rsqrt(jnp.mean(x * x, axis=1, keepdims=True) + EPS)
        xh = x * r
        e = xh * g - in_refs[3][...]
        loss_part = 0.5 * live * jnp.sum(jnp.mean(e * e, axis=1, keepdims=True), axis=0, keepdims=True)
        dout = e * (live / D)
        dg_part = jnp.sum(dout * xh, axis=0, keepdims=True)
        dxh = dout * g
        out_refs[0][...] = r * (dxh - xh * jnp.mean(dxh * xh, axis=1, keepdims=True))

        @pl.when(i == 0)
        def _():
            out_refs[1][...] = loss_part
            out_refs[2][...] = dg_part

        @pl.when(i > 0)
        def _():
            out_refs[1][...] += loss_part
            out_refs[2][...] += dg_part

    const = lambda shape: pl.BlockSpec(shape, lambda j, i, kk: (0,) * len(shape))
    return _mm_fused(
        [(ff, _blk(TM, D_FF)), (wdown, const((D_FF, D))), (h1, _blk(TM, D)),
         (target, _blk(TM, D, row=lambda i: jnp.maximum(i - 1, 0))), (gf, const((1, D)))],
        [(0, 1)], nt=False, m=tp, n=D, tm=TM, tn=D,
        outs=[(jax.ShapeDtypeStruct((tp, D), F32), _blk(TM, D)), (jax.ShapeDtypeStruct((1, 1), F32), const((1, 1))),
              (jax.ShapeDtypeStruct((1, D), F32), const((1, D)))],
        epilogue=epilogue, name=name)


def _ffn_d_hidden(dh2, wdown, gu, name):
    tp = dh2.shape[0]

    def epilogue(prods, in_refs, out_refs, i, cols):
        d = prods[0]
        g = in_refs[2][:, cols].astype(F32)
        u = in_refs[2][:, _cols_at(cols, FF_TN)].astype(F32)
        sg = _sigmoid(g)
        out_refs[0][:, cols] = (d * u * sg * (1.0 + g * (1.0 - sg))).astype(BF)
        out_refs[0][:, _cols_at(cols, FF_TN)] = (d * g * sg).astype(BF)

    return _mm_fused(
        [(dh2, _blk(TM, D)), (wdown, pl.BlockSpec((FF_TN, D), lambda j, i, kk: (j, 0))), (gu, _blk(TM, 2 * FF_TN, lambda j, kk: j))],
        [(0, 1)], nt=True, m=tp, n=D_FF, tm=TM, tn=FF_TN,
        outs=[(jax.ShapeDtypeStruct((tp, 2 * D_FF), BF), _blk(TM, 2 * FF_TN, lambda j, kk: j))],
        epilogue=epilogue, name=name, sub=SUB_COLS)[0]


def _ffn_d_in(dgu, wgu_t, h1, rstd, g2, dh2, name):
    tp = dgu.shape[0]
    nk = 2

    def epilogue(prods, in_refs, out_refs, i, cols):
        r = in_refs[3][...]
        xh = in_refs[2][...] * r
        dxn = prods[0]
        dxh = dxn * in_refs[4][...]
        out_refs[0][...] = r * (dxh - xh * jnp.mean(dxh * xh, axis=1, keepdims=True)) + in_refs[5][...]
        part = jnp.sum(dxn * xh, axis=0, keepdims=True)

        @pl.when(i == 0)
        def _():
            out_refs[1][...] = part

        @pl.when(i > 0)
        def _():
            out_refs[1][...] += part

    const = lambda shape: pl.BlockSpec(shape, lambda j, i, kk: (0,) * len(shape))
    return _mm_fused(
        [(dgu, pl.BlockSpec((TM, D_FF), lambda j, i, kk: (i, kk))), (wgu_t, pl.BlockSpec((D_FF, D), lambda j, i, kk: (kk, 0))),
         (h1, _blk(TM, D)), (rstd, _blk(TM, 1)), (g2, const((1, D))), (dh2, _blk(TM, D))],
        [(0, 1)], nt=False, m=tp, n=D, tm=TM, tn=D, nk=nk,
        outs=[(jax.ShapeDtypeStruct((tp, D), F32), _blk(TM, D)), (jax.ShapeDtypeStruct((1, D), F32), const((1, D)))],
        epilogue=epilogue, name=name)


def _branch_merge(y_m, y_g, wbm, wbg, pbig, name):
    tp = y_m.shape[0]

    def epilogue(prods, in_refs, out_refs, i, cols):
        pm, pg = prods[0].astype(BF), prods[1].astype(BF)
        out_refs[0][:, cols] = pm
        out_refs[1][:, cols] = pg
        out_refs[2][:, cols] = (_sigmoid(in_refs[4][:, cols].astype(F32)) * pm.astype(F32)
                                + _sigmoid(in_refs[5][:, cols].astype(F32)) * pg.astype(F32)).astype(BF)

    const = lambda shape: pl.BlockSpec(shape, lambda j, i, kk: (0,) * len(shape))
    shp = jax.ShapeDtypeStruct((tp, D), BF)
    return _mm_fused(
        [(y_m, _blk(TM, D)), (wbm, const((D, D))), (y_g, _blk(TM, D)), (wbg, const((D, D))),
         (pbig, _blk(TM, D, lambda j, kk: CB_GM)), (pbig, _blk(TM, D, lambda j, kk: CB_GG))],
        [(0, 1), (2, 3)], nt=False, m=tp, n=D, tm=TM, tn=D,
        outs=[(shp, _blk(TM, D)), (shp, _blk(TM, D)), (shp, _blk(TM, D))], epilogue=epilogue, name=name, sub=SUB_COLS)


def _merge_d(dh1, wout, pm, pg, pbig, name):
    tp = dh1.shape[0]

    def epilogue(prods, in_refs, out_refs, i, cols):
        d = prods[0]
        sm = _sigmoid(in_refs[4][:, cols].astype(F32))
        sg = _sigmoid(in_refs[5][:, cols].astype(F32))
        out_refs[0][:, cols] = (d * sm).astype(BF)
        out_refs[1][:, cols] = (d * sg).astype(BF)
        out_refs[2][:, cols] = (d * in_refs[2][:, cols].astype(F32) * sm * (1.0 - sm)).astype(BF)
        out_refs[2][:, _cols_at(cols, D)] = (d * in_refs[3][:, cols].astype(F32) * sg * (1.0 - sg)).astype(BF)

    const = lambda shape: pl.BlockSpec(shape, lambda j, i, kk: (0,) * len(shape))
    shp = jax.ShapeDtypeStruct((tp, D), BF)
    return _mm_fused(
        [(dh1, _blk(TM, D)), (wout, const((D, D))), (pm, _blk(TM, D)), (pg, _blk(TM, D)),
         (pbig, _blk(TM, D, lambda j, kk: CB_GM)), (pbig, _blk(TM, D, lambda j, kk: CB_GG))],
        [(0, 1)], nt=True, m=tp, n=D, tm=TM, tn=D,
        outs=[(shp, _blk(TM, D)), (shp, _blk(TM, D)),
              (jax.ShapeDtypeStruct((tp, N_ALL), BF), _blk(TM, 2 * D, lambda j, kk: CB_GM // 2))],
        epilogue=epilogue, name=name, sub=SUB_COLS)


def _out_proj_norm(merged, wout, h0, g2, name):
    tp = merged.shape[0]
    tm = _mm_rows(tp)

    def epilogue(prods, in_refs, out_refs, i, cols):
        x = prods[0] + in_refs[2][...]
        r = lax.rsqrt(jnp.mean(x * x, axis=1, keepdims=True) + EPS)
        out_refs[0][...] = x
        out_refs[1][...] = (x * r * in_refs[3][...]).astype(BF)
        out_refs[2][...] = r

    const = lambda shape: pl.BlockSpec(shape, lambda j, i, kk: (0,) * len(shape))
    return _mm_fused(
        [(merged, _blk(tm, D)), (wout, const((D, D))), (h0, _blk(tm, D)), (g2, const((1, D)))],
        [(0, 1)], nt=False, m=tp, n=D, tm=tm, tn=D,
        outs=[(jax.ShapeDtypeStruct((tp, D), F32), _blk(tm, D)), (jax.ShapeDtypeStruct((tp, D), BF), _blk(tm, D)),
              (jax.ShapeDtypeStruct((tp, 1), F32), _blk(tm, 1))],
        epilogue=epilogue, name=name)


def _adamw(w, g, m, v, name):
    rows, cols = w.shape
    tr = 128 if rows % 128 == 0 else rows

    def body(w_ref, g_ref, m_ref, v_ref, d_ref, nm_ref, nv_ref):
        gv = g_ref[...]
        nm = ADAM_B1 * m_ref[...] + (1.0 - ADAM_B1) * gv
        nv = ADAM_B2 * v_ref[...] + (1.0 - ADAM_B2) * (gv * gv)
        m_hat = nm / (1.0 - ADAM_B1 ** ADAM_STEP)
        v_hat = nv / (1.0 - ADAM_B2 ** ADAM_STEP)
        d_ref[...] = -ADAM_LR * (m_hat / (jnp.sqrt(v_hat) + ADAM_EPS) + ADAM_WD * w_ref[...])
        nm_ref[...] = nm
        nv_ref[...] = nv

    spec = pl.BlockSpec((tr, cols), lambda i: (i, 0))
    shp = jax.ShapeDtypeStruct((rows, cols), F32)
    return pl.pallas_call(body, grid=(rows // tr,), in_specs=[spec] * 4, out_specs=[spec] * 3,
                          out_shape=(shp,) * 3, compiler_params=_params(), name=name)(w, g, m, v)


def _place_small(dsmall, dproj, name):
    tp = dsmall.shape[0]

    def body(s_ref, _, o_ref):
        o_ref[...] = s_ref[...]

    return pl.pallas_call(
        body, grid=(tp // TM,), in_specs=[_row_spec(N_SMALL), ANY], out_specs=_row_spec(N_SMALL, N_BIG // N_SMALL),
        out_shape=jax.ShapeDtypeStruct(dproj.shape, dproj.dtype), input_output_aliases={1: 0},
        compiler_params=_params(), name=name)(dsmall, dproj)


def _row_tile(rows, cap=512):
    best = rows
    for cand in range(8, min(rows, cap) + 1, 8):
        if rows % cand == 0:
            best = cand
    return best


def _add2(a, b, out_dtype, name):
    rows, cols = a.shape
    tr = _row_tile(rows)

    def body(a_ref, b_ref, o_ref):
        o_ref[...] = (a_ref[...] + b_ref[...]).astype(o_ref.dtype)

    spec = pl.BlockSpec((tr, cols), lambda i: (i, 0))
    return pl.pallas_call(body, grid=(rows // tr,), in_specs=[spec] * 2, out_specs=spec,
                          out_shape=jax.ShapeDtypeStruct((rows, cols), out_dtype), compiler_params=_params(), name=name)(a, b)


def _add4(first, rest, name):
    rows, cols = first.shape
    tr = _row_tile(rows, 256)

    def body(f_ref, r_ref, o_ref):
        up = lambda v: v.astype(F32)
        o_ref[...] = ((up(f_ref[...]) + up(r_ref[0])) + up(r_ref[1])) + up(r_ref[2])

    return pl.pallas_call(body, grid=(rows // tr,),
                          in_specs=[pl.BlockSpec((tr, cols), lambda i: (i, 0)), pl.BlockSpec((3, tr, cols), lambda i: (0, i, 0))],
                          out_specs=pl.BlockSpec((tr, cols), lambda i: (i, 0)),
                          out_shape=jax.ShapeDtypeStruct((rows, cols), F32), compiler_params=_params(), name=name)(first, rest)


def _chunk_consts():
    r2 = lax.broadcasted_iota(jnp.int32, (CHUNK, CHUNK), 0)
    c2 = lax.broadcasted_iota(jnp.int32, (CHUNK, CHUNK), 1)
    tri = r2 >= c2
    return dict(tri=tri, tril_f=tri.astype(F32), triu_f=(r2 <= c2).astype(F32),
                lane=lax.broadcasted_iota(jnp.int32, (CHUNK, N_SMALL), 1),
                rowio=lax.broadcasted_iota(jnp.int32, (CHUNK, 1), 0),
                ones=jnp.ones((CHUNK, N_SMALL), F32))


def _valid_rows(block, c):
    row = block * TM + c * CHUNK + lax.broadcasted_iota(jnp.int32, (CHUNK, 1), 0)
    return row >= FIRST_VALID


def _col(x, lane, idx):
    return jnp.sum(jnp.where(lane == idx, x, 0.0), axis=1, keepdims=True)


def _last_row(x, rowio):
    return jnp.sum(jnp.where(rowio == CHUNK - 1, x, 0.0), axis=0, keepdims=True)


def _sum_all(x):
    return jnp.sum(jnp.sum(x, axis=1, keepdims=True), axis=0, keepdims=True)


def _headnorm_fwd(hm, gain, gate_act):
    rs = lax.rsqrt(jnp.mean(hm * hm, axis=1, keepdims=True) + EPS)
    return hm * rs * gain * gate_act


def _headnorm_bwd(dy, hm, gain, gate_act):
    rs = lax.rsqrt(jnp.mean(hm * hm, axis=1, keepdims=True) + EPS)
    xh = hm * rs
    dact = dy * xh * gain
    dgain = jnp.sum(dy * gate_act * xh, axis=0, keepdims=True)
    dxh = dy * gate_act * gain
    dhm = rs * (dxh - xh * jnp.mean(dxh * xh, axis=1, keepdims=True))
    return dhm, dact, dgain


def _mlstm_gates(sm, gbias, valid, k):
    pre = sm + gbias
    lf = jnp.where(valid, _logsig(pre), 0.0)
    b_all = _nn(k["tril_f"], lf, precision=HI)
    li_all = jnp.where(valid, pre, NEG)
    return pre, li_all, b_all


def _mlstm_open(h, qh, kh, c_st, li_all, b_all, k):
    lane = k["lane"]
    sel = jnp.where(lane == h, 1.0, 0.0) - jnp.where(lane == NH + h, 1.0, 0.0)
    x = jnp.where(lane < NH, li_all, jnp.where(lane < 2 * NH, b_all, 0.0))
    cb = c_st.astype(BF)
    return dict(ubc=_nt(sel, x, precision=HI), sim=_nt(qh, kh), cb=cb, cq=_nt(qh, cb))


def _mlstm_weights(h, f, qh, vh, li_all, b_all, n_row, m11, k):
    lane, tri, rowio = k["lane"], k["tri"], k["rowio"]
    b_col = _col(b_all, lane, NH + h)
    li_col = _col(li_all, lane, h)
    dmat = jnp.where(tri, b_col + f["ubc"], NEG)
    m_row = jnp.maximum(b_col + m11, jnp.max(dmat, axis=1, keepdims=True))
    e = jnp.exp(dmat - m_row)
    w_mat = e * f["sim"]
    a = jnp.exp(b_col + m11 - m_row)
    qf = qh.astype(F32)
    nq = jnp.sum(qf * n_row, axis=1, keepdims=True)
    g = _last_row(b_col, rowio)
    wlog = g - b_col + li_col
    m_new = jnp.maximum(g + m11, jnp.max(wlog, axis=0, keepdims=True))
    a_s = jnp.exp(g + m11 - m_new)
    w = jnp.exp(wlog - m_new)
    return dict(f, e=e, w_mat=w_mat, a=a, qf=qf, nq=nq, m_row=m_row, m_new=m_new, a_s=a_s, w=w,
                wv=_nn(w_mat.astype(BF), vh))


def _mlstm_out(f):
    num = f["a"] * f["cq"] + f["wv"]
    den = f["a"] * f["nq"] + jnp.sum(f["w_mat"], axis=1, keepdims=True)
    floor = jnp.exp(-f["m_row"])
    r = jnp.maximum(jnp.abs(den), floor)
    return dict(f, den=den, floor=floor, r=r, hm=num / r)


def _mlstm_fwd(qk, pbig, small, gbias, headg, name):
    tp = qk.shape[0]
    nb = tp // TM

    def body(qk_ref, v_ref, mo_ref, sm_ref, gb_ref, hg_ref, y_ref, cs_ref, ns_ref, c_scr, n_scr):
        blk = pl.program_id(0)

        @pl.when(blk == 0)
        def _():
            c_scr[...] = jnp.zeros_like(c_scr)
            n_scr[...] = jnp.zeros_like(n_scr)

        k = _chunk_consts()
        io8 = lax.broadcasted_iota(jnp.int32, (8, DQK), 0)

        def chunk(c, carry):
            r0 = pl.multiple_of(c * CHUNK, CHUNK)
            rows = pl.ds(r0, CHUNK)
            valid = _valid_rows(blk, c)
            _, li_all, b_all = _mlstm_gates(sm_ref[rows, :], gb_ref[...], valid, k)
            heads = range(NH)
            qs = [qk_ref[rows, h * DQK:(h + 1) * DQK] for h in heads]
            ks = [qk_ref[rows, NH * DQK + h * DQK:NH * DQK + (h + 1) * DQK] for h in heads]
            vs = [v_ref[rows, h * DV:(h + 1) * DV] for h in heads]
            cst = [c_scr[h] for h in heads]
            nrow = [n_scr[h, 0:1, :] for h in heads]
            m11 = [jnp.max(n_scr[h, 1:2, :], axis=1, keepdims=True) for h in heads]
            f = [_mlstm_open(h, qs[h], ks[h], cst[h], li_all, b_all, k) for h in heads]
            f = [_mlstm_weights(h, f[h], qs[h], vs[h], li_all, b_all, nrow[h], m11[h], k) for h in heads]
            wk = [f[h]["w"] * ks[h].astype(F32) for h in heads]
            kv = [_tn(vs[h], wk[h].astype(BF)) for h in heads]
            for h in heads:
                hm = _mlstm_out(f[h])["hm"]
                gate = _sigmoid(mo_ref[rows, h * DV:(h + 1) * DV].astype(F32))
                y_ref[rows, h * DV:(h + 1) * DV] = _headnorm_fwd(hm, hg_ref[:, h * DV:(h + 1) * DV], gate).astype(BF)
                cs_ref[c, h] = f[h]["cb"]
                ns_ref[c, h] = jnp.where(io8 == 0, nrow[h], jnp.where(io8 == 1, m11[h], 0.0))
                c_scr[h] = f[h]["a_s"] * cst[h] + kv[h]
                n_scr[h, 0:1, :] = f[h]["a_s"] * nrow[h] + jnp.sum(wk[h], axis=0, keepdims=True)
                n_scr[h, 1:2, :] = jnp.broadcast_to(f[h]["m_new"], (1, DQK))
            return carry

        lax.fori_loop(0, CPB, chunk, 0, unroll=2)

    return pl.pallas_call(
        body, grid=(nb,),
        in_specs=[_row_spec(D), _row_spec(D, CB_MV), _row_spec(D, CB_MO), _row_spec(N_SMALL), _full_spec((1, N_SMALL)), _full_spec((1, D))],
        out_specs=[_row_spec(D), pl.BlockSpec((CPB, NH, DV, DQK), lambda i: (i, 0, 0, 0)),
                   pl.BlockSpec((CPB, NH, 8, DQK), lambda i: (i, 0, 0, 0))],
        out_shape=(jax.ShapeDtypeStruct((tp, D), BF), jax.ShapeDtypeStruct((tp // CHUNK, NH, DV, DQK), BF),
                   jax.ShapeDtypeStruct((tp // CHUNK, NH, 8, DQK), F32)),
        scratch_shapes=[pltpu.VMEM((NH, DV, DQK), F32), pltpu.VMEM((NH, 8, DQK), F32)],
        compiler_params=_params(), name=name)(qk, pbig, pbig, small, gbias, headg)


def _mlstm_bwd(dy, qk, pbig, small, gbias, headg, cs, ns, dproj, name, ride=()):
    tp = qk.shape[0]
    nb = tp // TM
    nr = len(ride)

    def body(*refs):
        dy_ref, qk_ref, v_ref, mo_ref, sm_ref, gb_ref, hg_ref, cs_ref, ns_ref = refs[:9]
        ride_in = refs[10:10 + nr]
        dqk_ref, dproj_ref, dsm_ref, dgb_ref, dhg_ref = refs[10 + nr:15 + nr]
        ride_out = refs[15 + nr:15 + 2 * nr]
        dc_scr, dn_scr = refs[15 + 2 * nr:17 + 2 * nr]
        step = pl.program_id(0)
        blk = nb - 1 - step
        sent = _scatter_copies(ride_in, ride_out, *refs[17 + 2 * nr:]) if nr else []

        @pl.when(step == 0)
        def _():
            dc_scr[...] = jnp.zeros_like(dc_scr)
            dn_scr[...] = jnp.zeros_like(dn_scr)
            dgb_ref[...] = jnp.zeros_like(dgb_ref)
            dhg_ref[...] = jnp.zeros_like(dhg_ref)
            for cp in sent:
                cp.start()

        k = _chunk_consts()
        lane, rowio = k["lane"], k["rowio"]

        def chunk(cc, carry):
            c = CPB - 1 - cc
            r0 = pl.multiple_of(c * CHUNK, CHUNK)
            rows = pl.ds(r0, CHUNK)
            valid = _valid_rows(blk, c)
            pre, li_all, b_all = _mlstm_gates(sm_ref[rows, :], gb_ref[...], valid, k)
            dli_all = jnp.zeros((CHUNK, N_SMALL), F32)
            db_all = jnp.zeros((CHUNK, N_SMALL), F32)
            heads = range(NH)
            qs = [qk_ref[rows, h * DQK:(h + 1) * DQK] for h in heads]
            ks = [qk_ref[rows, NH * DQK + h * DQK:NH * DQK + (h + 1) * DQK] for h in heads]
            vs = [v_ref[rows, h * DV:(h + 1) * DV] for h in heads]
            cst = [cs_ref[c, h].astype(F32) for h in heads]
            nrow = [ns_ref[c, h, 0:1, :] for h in heads]
            m11 = [jnp.max(ns_ref[c, h, 1:2, :], axis=1, keepdims=True) for h in heads]
            f = [_mlstm_open(h, qs[h], ks[h], cst[h], li_all, b_all, k) for h in heads]
            f = [_mlstm_weights(h, f[h], qs[h], vs[h], li_all, b_all, nrow[h], m11[h], k) for h in heads]
            f = [_mlstm_out(f[h]) for h in heads]
            t = []
            for h in heads:
                gain = hg_ref[:, h * DV:(h + 1) * DV]
                gate = _sigmoid(mo_ref[rows, h * DV:(h + 1) * DV].astype(F32))
                dhm, dgate, dgain = _headnorm_bwd(dy_ref[rows, h * DV:(h + 1) * DV].astype(F32), f[h]["hm"], gain, gate)
                dproj_ref[rows, D + h * DV:D + (h + 1) * DV] = (dgate * gate * (1.0 - gate)).astype(BF)
                dhg_ref[:, h * DV:(h + 1) * DV] += dgain
                r, den = f[h]["r"], f[h]["den"]
                dnum = dhm / r
                dr = -jnp.sum(dhm * f[h]["hm"], axis=1, keepdims=True) / r
                dden = jnp.where(jnp.abs(den) > f[h]["floor"], dr * jnp.sign(den), 0.0)
                dnb = dnum.astype(BF)
                dc_new = dc_scr[h]
                dcb = dc_new.astype(BF)
                t.append(dict(dnum=dnum, dden=dden, dnb=dnb, dc_new=dc_new, dn_new=dn_scr[h],
                              dwm=_nt(dnb, vs[h]), vdc=_nn(vs[h], dcb), kdc=_nt(ks[h], dcb)))
            for h in heads:
                dw_mat = t[h]["dwm"] + t[h]["dden"]
                dsim = (f[h]["e"] * dw_mat).astype(BF)
                gm = f[h]["w_mat"] * dw_mat
                t[h].update(gm=gm, dv0=_tn(f[h]["w_mat"].astype(BF), t[h]["dnb"]), dq0=_nn(dsim, ks[h]),
                            dq1=_nn(t[h]["dnb"], f[h]["cb"]), dk0=_tn(dsim, qs[h]),
                            dcq=_tn((f[h]["a"] * t[h]["dnum"]).astype(BF), qs[h]), cs2=_tn(gm, k["ones"], precision=HI))
            for h in heads:
                a, w, a_s = f[h]["a"], f[h]["w"], f[h]["a_s"]
                dnum, dden, dc_new, dn_new, vdc, gm = (t[h][n] for n in ("dnum", "dden", "dc_new", "dn_new", "vdc", "gm"))
                kf = ks[h].astype(F32)
                dproj_ref[rows, h * DV:(h + 1) * DV] = (t[h]["dv0"] + w * t[h]["kdc"]).astype(BF)
                adden = a * dden
                dqk_ref[rows, h * DQK:(h + 1) * DQK] = t[h]["dq0"] + a * t[h]["dq1"] + adden * nrow[h]
                dqk_ref[rows, NH * DQK + h * DQK:NH * DQK + (h + 1) * DQK] = t[h]["dk0"] + w * vdc + w * dn_new
                da = jnp.sum(dnum * f[h]["cq"], axis=1, keepdims=True) + dden * f[h]["nq"]
                dw = jnp.sum(vdc * kf, axis=1, keepdims=True) + jnp.sum(kf * dn_new, axis=1, keepdims=True)
                da_s = _sum_all(dc_new * cst[h]) + jnp.sum(dn_new * nrow[h], axis=1, keepdims=True)
                wdw = w * dw
                rs = jnp.sum(gm, axis=1, keepdims=True)
                cs_col = _col(t[h]["cs2"], lane, 0)
                dg = a_s * da_s + jnp.sum(wdw, axis=0, keepdims=True)
                db = a * da + rs - cs_col - wdw + jnp.where(rowio == CHUNK - 1, dg, 0.0)
                dli_all = dli_all + jnp.where(lane == h, cs_col + wdw, 0.0)
                db_all = db_all + jnp.where(lane == NH + h, db, 0.0)
                dc_scr[h] = a_s * dc_new + t[h]["dcq"]
                dn_scr[h] = a_s * dn_new + jnp.sum(adden * f[h]["qf"], axis=0, keepdims=True)
            dlf_all = _nn(k["triu_f"], db_all, precision=HI)
            dsm = jnp.where(valid, dli_all + dlf_all * _sigmoid(-pre), 0.0)
            dsm = jnp.where(lane < 2 * NH, dsm, 0.0)
            dsm_ref[rows, :] = dsm
            dgb_ref[0:1, :] += jnp.sum(dsm, axis=0, keepdims=True)
            return carry

        lax.fori_loop(0, CPB, chunk, 0, unroll=2)

        if nr:
            @pl.when(step == nb - 1)
            def _():
                for cp in sent:
                    cp.wait_recv()
                for cp in sent:
                    cp.wait_send()

    rev = lambda col: (lambda i: (nb - 1 - i, col))
    rspec = lambda width, col=0: pl.BlockSpec((TM, width), rev(col))
    ride_shapes, ride_sems = _scatter_shapes(ride) if nr else ((), [])
    outs = pl.pallas_call(
        body, grid=(nb,),
        in_specs=[rspec(D), rspec(D), rspec(D, CB_MV), rspec(D, CB_MO), rspec(N_SMALL), _full_spec((1, N_SMALL)), _full_spec((1, D)),
                  pl.BlockSpec((CPB, NH, DV, DQK), lambda i: (nb - 1 - i, 0, 0, 0)),
                  pl.BlockSpec((CPB, NH, 8, DQK), lambda i: (nb - 1 - i, 0, 0, 0)), ANY] + [ANY] * nr,
        out_specs=[rspec(D), rspec(2 * D, CB_MV // 2), rspec(N_SMALL), _full_spec((8, N_SMALL)), _full_spec((1, D))] + [ANY] * nr,
        out_shape=(jax.ShapeDtypeStruct((tp, D), F32), jax.ShapeDtypeStruct(dproj.shape, BF),
                   jax.ShapeDtypeStruct((tp, N_SMALL), F32), jax.ShapeDtypeStruct((8, N_SMALL), F32),
                   jax.ShapeDtypeStruct((1, D), F32)) + tuple(ride_shapes),
        scratch_shapes=[pltpu.VMEM((NH, DV, DQK), F32), pltpu.VMEM((NH, 1, DQK), F32)] + ride_sems,
        input_output_aliases={9: 1}, compiler_params=_params(), name=name)(dy, qk, pbig, pbig, small, gbias, headg, cs, ns, dproj, *ride)
    return tuple(outs[:5]) + (list(outs[5:]),)


def _gla_loga(sm_ref, a2_ref, a2b_ref, blk):
    za = _nn(sm_ref[...].astype(BF), a2_ref[...]) + a2b_ref[...]
    row = blk * TM + lax.broadcasted_iota(jnp.int32, (TM, 1), 0)
    return za, jnp.where(row >= FIRST_VALID, _logsig(za) / G_TAU, 0.0)


def _gla_head(h, q_ref, k_ref, rows, bc, btot, k):
    sl = slice(h * DQK, (h + 1) * DQK)
    bch = bc[:, sl]
    bth = btot[:, sl]
    gq = q_ref[rows, h * DQK:(h + 1) * DQK].astype(F32)
    gk = k_ref[rows, NH * DQK + h * DQK:NH * DQK + (h + 1) * DQK].astype(F32)
    e_pos = jnp.exp(bch) * (DQK ** -0.5)
    e_neg = jnp.exp(-bch)
    e_end = jnp.exp(bth - bch)
    qd = gq * e_pos
    ki = gk * e_neg
    ke = gk * e_end
    att = jnp.where(k["tri"], _nt(qd.astype(BF), ki.astype(BF)), 0.0)
    return dict(e_pos=e_pos, e_neg=e_neg, e_end=e_end, qd=qd, ki=ki, ke=ke, att=att, decay=jnp.exp(bth))


def _gla_fwd(pbig, small, a2p, a2b, headg, name):
    tp = pbig.shape[0]
    nb = tp // TM

    def body(qk_ref, v_ref, gr_ref, sm_ref, a2_ref, a2b_ref, hg_ref, y_ref, ss_ref, s_scr, lg_scr):
        blk = pl.program_id(0)

        @pl.when(blk == 0)
        def _():
            s_scr[...] = jnp.zeros_like(s_scr)

        k = _chunk_consts()
        _, loga = _gla_loga(sm_ref, a2_ref, a2b_ref, blk)
        lg_scr[...] = loga

        def chunk(c, carry):
            r0 = pl.multiple_of(c * CHUNK, CHUNK)
            rows = pl.ds(r0, CHUNK)
            bc = _nn(k["tril_f"], lg_scr[rows, :], precision=HI)
            btot = _last_row(bc, k["rowio"])
            heads = range(NH)
            f = [_gla_head(h, qk_ref, qk_ref, rows, bc, btot, k) for h in heads]
            vs = [v_ref[rows, h * DV:(h + 1) * DV] for h in heads]
            sst = [s_scr[h] for h in heads]
            sbs = [s.astype(BF) for s in sst]
            inter = [_nt(f[h]["qd"].astype(BF), sbs[h]) for h in heads]
            intra = [_nn(f[h]["att"].astype(BF), vs[h]) for h in heads]
            kv = [_tn(vs[h], f[h]["ke"].astype(BF)) for h in heads]
            for h in heads:
                gr = gr_ref[rows, h * DV:(h + 1) * DV].astype(F32)
                y_ref[rows, h * DV:(h + 1) * DV] = _headnorm_fwd(intra[h] + inter[h], hg_ref[:, h * DV:(h + 1) * DV],
                                                                   gr * _sigmoid(gr)).astype(BF)
                ss_ref[c, h] = sbs[h]
                s_scr[h] = sst[h] * f[h]["decay"] + kv[h]
            return carry

        lax.fori_loop(0, CPB, chunk, 0, unroll=2)

    return pl.pallas_call(
        body, grid=(nb,),
        in_specs=[_row_spec(D, CB_GQK), _row_spec(D, CB_GV), _row_spec(D, CB_GR), _row_spec(N_SMALL),
                  _full_spec((N_SMALL, NH * DQK)), _full_spec((1, NH * DQK)), _full_spec((1, D))],
        out_specs=[_row_spec(D), pl.BlockSpec((CPB, NH, DV, DQK), lambda i: (i, 0, 0, 0))],
        out_shape=(jax.ShapeDtypeStruct((tp, D), BF), jax.ShapeDtypeStruct((tp // CHUNK, NH, DV, DQK), BF)),
        scratch_shapes=[pltpu.VMEM((NH, DV, DQK), F32), pltpu.VMEM((TM, NH * DQK), F32)],
        compiler_params=_params(), name=name)(pbig, pbig, pbig, small, a2p, a2b, headg)


def _gla_bwd(dy, pbig, small, a2p, a2b, headg, ss, dsm_m, dproj, name):
    tp = pbig.shape[0]
    nb = tp // TM
    nqk = NH * DQK

    def body(dy_ref, qk_ref, v_ref, gr_ref, sm_ref, a2_ref, a2b_ref, hg_ref, ss_ref, dsmm_ref, _,
             dproj_ref, dsm_ref, da2_ref, da2b_ref, dhg_ref, ds_scr, lg_scr, dza_scr):
        step = pl.program_id(0)
        blk = nb - 1 - step

        @pl.when(step == 0)
        def _():
            ds_scr[...] = jnp.zeros_like(ds_scr)
            da2_ref[...] = jnp.zeros_like(da2_ref)
            da2b_ref[...] = jnp.zeros_like(da2b_ref)
            dhg_ref[...] = jnp.zeros_like(dhg_ref)

        k = _chunk_consts()
        rowio = k["rowio"]
        za, loga = _gla_loga(sm_ref, a2_ref, a2b_ref, blk)
        lg_scr[...] = loga

        def chunk(cc, carry):
            c = CPB - 1 - cc
            r0 = pl.multiple_of(c * CHUNK, CHUNK)
            rows = pl.ds(r0, CHUNK)
            bc = _nn(k["tril_f"], lg_scr[rows, :], precision=HI)
            btot = _last_row(bc, rowio)
            heads = range(NH)
            f = [_gla_head(h, qk_ref, qk_ref, rows, bc, btot, k) for h in heads]
            vs = [v_ref[rows, h * DV:(h + 1) * DV] for h in heads]
            sbs = [ss_ref[c, h] for h in heads]
            qdb = [f[h]["qd"].astype(BF) for h in heads]
            attb = [f[h]["att"].astype(BF) for h in heads]
            inter = [_nt(qdb[h], sbs[h]) for h in heads]
            intra = [_nn(attb[h], vs[h]) for h in heads]
            dsn = [ds_scr[h] for h in heads]
            dsb = [d.astype(BF) for d in dsn]
            dke = [_nn(vs[h], dsb[h]) for h in heads]
            dv1 = [_nt(f[h]["ke"].astype(BF), dsb[h]) for h in heads]
            t = []
            for h in heads:
                gr = gr_ref[rows, h * DV:(h + 1) * DV].astype(F32)
                sg = _sigmoid(gr)
                gain = hg_ref[:, h * DV:(h + 1) * DV]
                do, dact, dgain = _headnorm_bwd(dy_ref[rows, h * DV:(h + 1) * DV].astype(F32), intra[h] + inter[h], gain, gr * sg)
                dproj_ref[rows, 2 * D + h * DV:2 * D + (h + 1) * DV] = (dact * sg * (1.0 + gr * (1.0 - sg))).astype(BF)
                dhg_ref[:, h * DV:(h + 1) * DV] += dgain
                dob = do.astype(BF)
                t.append(dict(dob=dob, datt=_nt(dob, vs[h]), dv0=_tn(attb[h], dob), dq1=_nn(dob, sbs[h]), dsq=_tn(dob, qdb[h])))
            for h in heads:
                datt = jnp.where(k["tri"], t[h]["datt"], 0.0).astype(BF)
                t[h].update(dq0=_nn(datt, f[h]["ki"].astype(BF)), dki=_tn(datt, qdb[h]))
            dbc_parts = []
            for h in heads:
                dqd = t[h]["dq0"] + t[h]["dq1"]
                dki = t[h]["dki"]
                dproj_ref[rows, D + h * DV:D + (h + 1) * DV] = (t[h]["dv0"] + dv1[h]).astype(BF)
                dproj_ref[rows, h * DQK:(h + 1) * DQK] = (dqd * f[h]["e_pos"]).astype(BF)
                dproj_ref[rows, nqk + h * DQK:nqk + (h + 1) * DQK] = (dki * f[h]["e_neg"] + dke[h] * f[h]["e_end"]).astype(BF)
                dke_ke = dke[h] * f[h]["ke"]
                dbtot = (jnp.sum(dke_ke, axis=0, keepdims=True)
                         + jnp.sum(dsn[h] * sbs[h].astype(F32), axis=0, keepdims=True) * f[h]["decay"])
                dbc_parts.append(dqd * f[h]["qd"] - dki * f[h]["ki"] - dke_ke + jnp.where(rowio == CHUNK - 1, dbtot, 0.0))
                ds_scr[h] = dsn[h] * f[h]["decay"] + t[h]["dsq"]
            dbc = jnp.concatenate(dbc_parts, axis=1)
            dza_scr[rows, :] = _nn(k["triu_f"], dbc, precision=HI)
            return carry

        lax.fori_loop(0, CPB, chunk, 0, unroll=2)
        row = blk * TM + lax.broadcasted_iota(jnp.int32, (TM, 1), 0)
        dza = jnp.where(row >= FIRST_VALID, dza_scr[...] * (_sigmoid(-za) / G_TAU), 0.0)
        dzb = dza.astype(BF)
        dsm_ref[...] = (_nt(dzb, a2_ref[...]) + dsmm_ref[...]).astype(BF)
        da2_ref[...] += _tn(sm_ref[...].astype(BF), dzb)
        da2b_ref[...] += jnp.sum(dza, axis=0, keepdims=True)

    rspec = lambda width, col=0: pl.BlockSpec((TM, width), lambda i: (nb - 1 - i, col))
    return pl.pallas_call(
        body, grid=(nb,),
        in_specs=[rspec(D), rspec(D, CB_GQK), rspec(D, CB_GV), rspec(D, CB_GR), rspec(N_SMALL),
                  _full_spec((N_SMALL, nqk)), _full_spec((1, nqk)), _full_spec((1, D)),
                  pl.BlockSpec((CPB, NH, DV, DQK), lambda i: (nb - 1 - i, 0, 0, 0)), rspec(N_SMALL), ANY],
        out_specs=[rspec(3 * D, 0), rspec(N_SMALL), _full_spec((N_SMALL, nqk)), _full_spec((1, nqk)), _full_spec((1, D))],
        out_shape=(jax.ShapeDtypeStruct(dproj.shape, BF),
                   jax.ShapeDtypeStruct((tp, N_SMALL), BF), jax.ShapeDtypeStruct((N_SMALL, nqk), F32),
                   jax.ShapeDtypeStruct((1, nqk), F32), jax.ShapeDtypeStruct((1, D), F32)),
        scratch_shapes=[pltpu.VMEM((NH, DV, DQK), F32), pltpu.VMEM((TM, nqk), F32), pltpu.VMEM((TM, nqk), F32)],
        input_output_aliases={10: 0}, compiler_params=_params(), name=name)(dy, pbig, pbig, pbig, small, a2p, a2b, headg, ss, dsm_m, dproj)


PIECE_BYTES = 1 << 20
MAX_PIECES = 32


def _place():
    return lax.axis_index("x"), lax.axis_index("y"), lax.axis_index("c")


def _piece_rows(rows, row_bytes, align):
    want = min(MAX_PIECES, max(1, -(-rows * row_bytes // PIECE_BYTES)))
    best = rows
    for k in range(1, want + 1):
        if rows % k == 0 and (rows // k) % align == 0:
            best = rows // k
    return best


def _remote(src, dst, send_sems, recv_sems, k, to):
    return pltpu.make_async_remote_copy(src_ref=src, dst_ref=dst, send_sem=send_sems.at[k], recv_sem=recv_sems.at[k],
                                        device_id=to, device_id_type=MESH)


def _all_gather_chips(p, name):
    r, n = p.shape
    rh = r // 2
    align = 32 // p.dtype.itemsize
    assert r % (2 * align) == 0
    cr = _piece_rows(rh, n * p.dtype.itemsize, align)

    def body(p_ref, o_ref, send_sems, recv_sems):
        x, y, c = _place()
        chips = [(1 - x, y), (x, 1 - y), (1 - x, 1 - y)]
        sib = (x, y, 1 - c)

        def half(hc, piece=None):
            if piece is None:
                return pl.ds(pl.multiple_of(hc * rh, align), rh)
            return pl.ds(pl.multiple_of(hc * rh + piece * cr, align), cr)

        first = [_remote(p_ref.at[half(c)], o_ref.at[j, half(c)], send_sems, recv_sems, j, (*chip, c))
                 for j, chip in enumerate(chips)]
        for cp in first:
            cp.start()
        for j, cp in enumerate(first):
            cp.wait_recv()
            for i in range(rh // cr):
                _remote(o_ref.at[j, half(c, i)], o_ref.at[j, half(c, i)], send_sems, recv_sems, 3 + j, sib).start()
        for j in range(3):
            block = _remote(o_ref.at[j, half(c)], o_ref.at[j, half(1 - c)], send_sems, recv_sems, 3 + j, sib)
            block.wait_send()
            block.wait_recv()
        for cp in first:
            cp.wait_send()

    return pl.pallas_call(
        body, in_specs=[ANY], out_specs=ANY, out_shape=jax.ShapeDtypeStruct((3, r, n), p.dtype),
        scratch_shapes=[pltpu.SemaphoreType.DMA((6,)), pltpu.SemaphoreType.DMA((6,))],
        name=name)(p)


def _by_chip(mine, others):
    me = 2 * lax.axis_index("x") + lax.axis_index("y")
    by_mask = jnp.stack([mine, others[1], others[0], others[2]])
    return [lax.dynamic_index_in_dim(by_mask, q ^ me, 0, keepdims=False) for q in range(4)]


def _swap_halves(items, name):
    k = len(items)

    def body(*refs):
        a_refs, got_refs = refs[:k], refs[k:2 * k]
        send_sems, recv_sems = refs[2 * k:]
        x, y, c = _place()
        sib = (x, y, 1 - c)
        for i, a in enumerate(items):
            _, r, n = a.shape
            rh = r // 2
            cr = _piece_rows(rh, n * a.dtype.itemsize, 8)
            for q in range(4):
                for t in range(rh // cr):
                    other = pl.ds(pl.multiple_of((1 - c) * rh + t * cr, 8), cr)
                    _remote(a_refs[i].at[q, other], got_refs[i].at[q, pl.ds(t * cr, cr)], send_sems, recv_sems, i, sib).start()
        for i, a in enumerate(items):
            block = _remote(a_refs[i].at[:, pl.ds(0, a.shape[1] // 2)], got_refs[i], send_sems, recv_sems, i, sib)
            block.wait_send()
            block.wait_recv()

    return pl.pallas_call(
        body, in_specs=[ANY] * k, out_specs=[ANY] * k,
        out_shape=tuple(jax.ShapeDtypeStruct((4, a.shape[1] // 2, a.shape[2]), a.dtype) for a in items),
        scratch_shapes=[pltpu.SemaphoreType.DMA((k,)), pltpu.SemaphoreType.DMA((k,))], name=name)(*items)


def _scatter_copies(s_refs, o_refs, send_sems, recv_sems):
    x, y, c = _place()
    chips = [(1 - x, y), (x, 1 - y), (1 - x, 1 - y)]
    return [_remote(s_refs[i].at[2 * cx + cy], o_refs[i].at[j], send_sems, recv_sems, 3 * i + j, (cx, cy, c))
            for i in range(len(s_refs)) for j, (cx, cy) in enumerate(chips)]


def _scatter_shapes(items):
    k = len(items)
    return (tuple(jax.ShapeDtypeStruct((3,) + s.shape[1:], s.dtype) for s in items),
            [pltpu.SemaphoreType.DMA((3 * k,)), pltpu.SemaphoreType.DMA((3 * k,))])


def _scatter_chips(items, name):
    k = len(items)

    def body(*refs):
        sent = _scatter_copies(refs[:k], refs[k:2 * k], *refs[2 * k:])
        for cp in sent:
            cp.start()
        for cp in sent:
            cp.wait_recv()
        for cp in sent:
            cp.wait_send()

    out_shape, scratch = _scatter_shapes(items)
    return pl.pallas_call(body, in_specs=[ANY] * k, out_specs=[ANY] * k, out_shape=out_shape, scratch_shapes=scratch,
                          name=name)(*items)


def _join_halves(items, name):
    k = len(items)

    def body(*refs):
        f_refs, o_refs = refs[:k], refs[k:2 * k]
        send_sems, recv_sems = refs[2 * k:]
        x, y, c = _place()
        sib = (x, y, 1 - c)
        for i, f in enumerate(items):
            rh, n = f.shape
            cr = _piece_rows(rh, n * f.dtype.itemsize, 8)
            for t in range(rh // cr):
                rows = pl.ds(t * cr, cr)
                _remote(f_refs[i].at[rows], o_refs[i].at[rows], send_sems, recv_sems, i, sib).start()
        for i in range(k):
            block = _remote(f_refs[i], o_refs[i], send_sems, recv_sems, i, sib)
            block.wait_send()
            block.wait_recv()

    return pl.pallas_call(
        body, in_specs=[ANY] * k, out_specs=[ANY] * k, out_shape=tuple(jax.ShapeDtypeStruct(f.shape, f.dtype) for f in items),
        scratch_shapes=[pltpu.SemaphoreType.DMA((k,)), pltpu.SemaphoreType.DMA((k,))], name=name)(*items)


SMALL_ROWS = 16
SMALL_SHARD_SHAPES = [(N_META, 256), (4, 256), (G_RANK, 128), (NH, 64), (NH, 64)]
REPL_SHAPES = [(1, D), (1, D), (1, 2, NH), (1, NH * DQK), (1, D), (D,)]
W_IN_SHARD = 2054


def _pack_small(parts):
    flat = jnp.concatenate([p.reshape(-1) for p in parts])
    return jnp.pad(flat, (0, SMALL_ROWS * D - flat.shape[0])).reshape(SMALL_ROWS, D)


def _unpack_small(block, shapes):
    flat, out, off = block.reshape(-1), [], 0
    for shp in shapes:
        n = 1
        for s in shp:
            n *= s
        out.append(flat[off:off + n].reshape(shp))
        off += n
    return out


def _proj_cols_from_w_in(w_in_f):
    w_big = jnp.concatenate([w_in_f[:, 3080:5128], w_in_f[:, 5144:6168], w_in_f[:, 0:1024], w_in_f[:, 6168:8216],
                             w_in_f[:, 1024:2048], w_in_f[:, 2056:3080]], axis=1)
    w_small = jnp.concatenate([w_in_f[:, 2048:2056], w_in_f[:, 5128:5144], jnp.zeros((D, N_SMALL - 24), w_in_f.dtype)], axis=1)
    return w_big, w_small


def _w_in_from_proj_cols(d_wall):
    big, small = d_wall[:, 0:N_BIG], d_wall[:, N_BIG:N_ALL]
    return jnp.concatenate([big[:, 3072:4096], big[:, 6144:7168], small[:, 0:8], big[:, 7168:8192], big[:, 0:2048],
                            small[:, 8:24], big[:, 2048:3072], big[:, 4096:6144]], axis=1)


def kernel(x, meta_tokens, norm1_g, w_in, conv_w, conv_b, m_gate_b, g_a2, g_a2_b, m_head_g, g_head_g, w_branch_m, w_branch_g, w_out, norm2_g, w_ff_gate, w_ff_up, w_ff_down, final_g, loss_target, m_meta_tokens, m_norm1_g, m_w_in, m_conv_w, m_conv_b, m_m_gate_b, m_g_a2, m_g_a2_b, m_m_head_g, m_g_head_g, m_w_branch_m, m_w_branch_g, m_w_out, m_norm2_g, m_w_ff_gate, m_w_ff_up, m_w_ff_down, m_final_g, v_meta_tokens, v_norm1_g, v_w_in, v_conv_w, v_conv_b, v_m_gate_b, v_g_a2, v_g_a2_b, v_m_head_g, v_g_head_g, v_w_branch_m, v_w_branch_g, v_w_out, v_norm2_g, v_w_ff_gate, v_w_ff_up, v_w_ff_down, v_final_g):
    w = _gather_weights(w_in, w_branch_m, w_branch_g, w_out, w_ff_gate, w_ff_up, w_ff_down, meta_tokens, conv_w, g_a2, m_head_g, g_head_g)
    loss_local, dx, grads = _local_step(x[0], loss_target[0], w, norm1_g, conv_b, m_gate_b, g_a2_b, norm2_g, final_g, _Reducer())

    weights = [w_in, w_branch_m, w_branch_g, w_out, w_ff_gate, w_ff_up, w_ff_down, meta_tokens, conv_w, g_a2, m_head_g, g_head_g,
               norm1_g, conv_b, m_gate_b, g_a2_b, norm2_g, final_g]
    moms = [m_w_in, m_w_branch_m, m_w_branch_g, m_w_out, m_w_ff_gate, m_w_ff_up, m_w_ff_down, m_meta_tokens, m_conv_w, m_g_a2,
            m_m_head_g, m_g_head_g, m_norm1_g, m_conv_b, m_m_gate_b, m_g_a2_b, m_norm2_g, m_final_g]
    vels = [v_w_in, v_w_branch_m, v_w_branch_g, v_w_out, v_w_ff_gate, v_w_ff_up, v_w_ff_down, v_meta_tokens, v_conv_w, v_g_a2,
            v_m_head_g, v_g_head_g, v_norm1_g, v_conv_b, v_m_gate_b, v_g_a2_b, v_norm2_g, v_final_g]
    res = {}
    for nm, wt, g, m, v in zip(PACK_ORDER, weights, grads, moms, vels):
        two_d = (wt.size // wt.shape[-1], wt.shape[-1])
        d, nm_, nv_ = _adamw(wt.reshape(two_d), g.reshape(two_d), m.reshape(two_d), v.reshape(two_d), "adamw_" + nm)
        res[nm] = (g.reshape(wt.shape), d.reshape(wt.shape), nm_.reshape(wt.shape), nv_.reshape(wt.shape))

    order = ["meta_tokens", "norm1_g", "w_in", "conv_w", "conv_b", "m_gate_b", "g_a2", "g_a2_b", "m_head_g", "g_head_g",
             "w_branch_m", "w_branch_g", "w_out", "norm2_g", "w_ff_gate", "w_ff_up", "w_ff_down", "final_g"]
    loss = lax.psum(loss_local[0, 0], ("x", "y", "c"))
    grad_x = dx.reshape(x.shape)
    return (loss, grad_x, *[res[n][0] for n in order], *[res[n][1] for n in order],
            *[res[n][2] for n in order], *[res[n][3] for n in order])


PACK_ORDER = ["w_in", "w_branch_m", "w_branch_g", "w_out", "w_ff_gate", "w_ff_up", "w_ff_down", "meta_tokens", "conv_w", "g_a2",
              "m_head_g", "g_head_g", "norm1_g", "conv_b", "m_gate_b", "g_a2_b", "norm2_g", "final_g"]


def _gather_weights(w_in, w_branch_m, w_branch_g, w_out, w_ff_gate, w_ff_up, w_ff_down, meta_tokens, conv_w, g_a2, m_head_g, g_head_g):
    bf = lambda a: a.astype(BF)
    rows_local = jnp.concatenate([bf(w_branch_m[0]), bf(w_branch_g[0]), bf(w_out[0]), bf(w_ff_down[0]),
                                  bf(w_ff_gate[0].T), bf(w_ff_up[0].T)], axis=0)
    win_local = bf(w_in[0])
    small_local = _pack_small([meta_tokens, conv_w[0], g_a2[0], m_head_g[0], g_head_g[0]])
    rows_all = jnp.stack(_by_chip(rows_local, _all_gather_chips(rows_local, "gather_rows")))
    win_all = _by_chip(win_local, _all_gather_chips(win_local, "gather_w_in"))
    small_all = _by_chip(small_local, _all_gather_chips(small_local, "gather_small"))
    cut = lambda lo, hi: rows_all[:, lo:hi].reshape(4 * (hi - lo), D)
    wbm, wbg, wout, wdown = cut(0, 256), cut(256, 512), cut(512, 768), cut(768, 1472)
    wgu_t = _ffn_weight_rows(cut(1472, 2176), cut(2176, 2880))
    w_in_f = jnp.concatenate([win_all[q] for q in range(4)], axis=1)
    small_sh = [_unpack_small(small_all[q], SMALL_SHARD_SHAPES) for q in range(4)]
    cat = lambda i: jnp.concatenate([s[i] for s in small_sh], axis=-1)
    return dict(w_in=w_in_f, wbm=wbm, wbg=wbg, wout=wout, wgu_t=wgu_t, wdown=wdown, meta=cat(0), convw=cat(1), ga2=cat(2),
                mhg=cat(3).reshape(1, D), ghg=cat(4).reshape(1, D))


def _local_step(x0, target, w, norm1_g, conv_b, m_gate_b, g_a2_b, norm2_g, final_g, reducer):
    w_in_f, wbm, wbg, wout, wgu_t, wdown = w["w_in"], w["wbm"], w["wbg"], w["wout"], w["wgu_t"], w["wdown"]
    meta_f, convw_f, ga2_f, mhg_f, ghg_f = w["meta"], w["convw"], w["ga2"], w["mhg"], w["ghg"]
    w_big, w_small = _proj_cols_from_w_in(w_in_f)
    w_all = jnp.concatenate([w_big, w_small], axis=1)
    gbias = jnp.concatenate([m_gate_b.reshape(1, 2 * NH), jnp.zeros((1, N_SMALL - 2 * NH), F32)], axis=1)
    a2p = jnp.concatenate([jnp.zeros((8, NH * DQK), F32), ga2_f, jnp.zeros((N_SMALL - 24, NH * DQK), F32)], axis=0).astype(BF)
    convb = conv_b.reshape(1, D)
    g1 = norm1_g.reshape(1, D)
    g2 = norm2_g.reshape(1, D)
    gf = final_g.reshape(1, D)
    h0 = jnp.concatenate([jnp.zeros((FIRST_VALID, D), F32), meta_f, x0], axis=0)

    xn1, rstd1 = _rms_fwd(h0, g1, "rms1")
    pbig = _mm(xn1, w_big, nt=False, out_dtype=BF, tn=1024, name="proj_big")
    small = _mm(xn1, w_small, nt=False, out_dtype=F32, tn=N_SMALL, name="proj_small")
    qk = _conv_fwd(pbig, convw_f, convb, "conv_fwd")
    y_m, m_cs, m_ns = _mlstm_fwd(qk, pbig, small, gbias, mhg_f, "mlstm_fwd")
    y_g, g_ss = _gla_fwd(pbig, small, a2p, g_a2_b, ghg_f, "gla_fwd")
    p_m, p_g, merged = _branch_merge(y_m, y_g, wbm, wbg, pbig, "branch_merge")
    h1, hn, rstd2 = _out_proj_norm(merged, wout, h0, g2, "out_proj")
    gu, ff = _ffn_in(hn, wgu_t, "ff_in")
    dh2, loss_local, d_final_g = _ffn_down_loss(ff, wdown, h1, target, gf, "ff_down_loss")

    d_wdown = _mm_tn(ff, dh2, tm=1408, tn=1024, name="dw_ff_down")
    dgu = _ffn_d_hidden(dh2, wdown, gu, "d_ff")
    d_wgu_t = _mm_tn(dgu, hn, tm=1408, tn=1024, name="dw_ff_in")
    dh1, d_g2 = _ffn_d_in(dgu, wgu_t, h1, rstd2, g2, dh2, "d_hn")
    d_wout = _mm_tn(merged, dh1, tm=1024, tn=1024, name="dw_out")
    dp_m, dp_g, dproj = _merge_d(dh1, wout, p_m, p_g, pbig, "d_merged")
    dy_m = _mm(dp_m, wbm, nt=True, out_dtype=BF, tn=1024, name="d_ym")
    dy_g = _mm(dp_g, wbg, nt=True, out_dtype=BF, tn=1024, name="d_yg")
    d_wbm = _mm_tn(y_m, dp_m, tm=1024, tn=1024, name="dw_branch_m")
    d_wbg = _mm_tn(y_g, dp_g, tm=1024, tn=1024, name="dw_branch_g")
    fq = D_FF // 4
    gu4 = jnp.transpose(d_wgu_t.reshape(2, 2, 2, fq, D), (0, 2, 1, 3, 4)).reshape(4, 2 * fq, D)
    sq4 = jnp.concatenate([d_wbm.reshape(4, 256, D), d_wbg.reshape(4, 256, D), d_wout.reshape(4, 256, D)], axis=1)
    sums_a = reducer.partial_sums([sq4, d_wdown.reshape(4, fq, D), gu4], BF, "a")
    dqk_m, dproj, dsm_m, d_gbias, d_mhg, recv_a = _mlstm_bwd(dy_m, qk, pbig, small, gbias, mhg_f, m_cs, m_ns, dproj,
                                                              "mlstm_bwd", ride=sums_a)
    dconv, d_convwb = _conv_bwd_pre(dqk_m, pbig, convw_f, convb, "conv_bwd_pre")
    dproj = _conv_bwd_in(dconv, convw_f, dproj, "conv_bwd_in")
    dproj, dsmall, d_a2p, d_a2b, d_ghg = _gla_bwd(dy_g, pbig, small, a2p, g_a2_b, ghg_f, g_ss, dsm_m, dproj, "gla_bwd")
    dproj = _place_small(dsmall, dproj, "dproj_small")
    d_win = _w_in_from_proj_cols(_mm_tn(xn1, dproj, tm=1024, tn=1664, name="dw_in"))
    win4 = jnp.stack([d_win[:, q * W_IN_SHARD:(q + 1) * W_IN_SHARD] for q in range(4)])
    sums_b = reducer.partial_sums([win4], BF, "b")
    dxn, recv_b = _mm(dproj, w_all, nt=True, out_dtype=F32, tn=1024, tk=1664, name="d_xn", ride=sums_b)
    dh_first, dx, d_g1 = _rms_bwd(dxn, h0, rstd1, g1, dh1, "rms1_bwd", split_first=True)

    small_sharded = [dh_first[FIRST_VALID:TM], d_convwb[0:4], d_a2p[8:24], d_mhg.reshape(NH, DV), d_ghg.reshape(NH, DV)]
    replicated = [d_g1, d_convwb[4:5], d_gbias[0:1, 0:2 * NH].reshape(1, 2, NH), d_a2b, d_g2, d_final_g.reshape(D)]
    small4 = jnp.stack([_pack_small([g[:, q * shp[1]:(q + 1) * shp[1]] for g, shp in zip(small_sharded, SMALL_SHARD_SHAPES)]
                                    + replicated) for q in range(4)])
    sums_c = reducer.partial_sums([small4], F32, "c")
    recv_c = reducer.scatter(sums_c, "c")
    sq, down, gu, win, smalls = reducer.finish(sums_a + sums_b + sums_c, recv_a + recv_b + recv_c, in_chip_order=[4])
    grads = ([win, sq[0:256], sq[256:512], sq[512:768], gu[0:fq].T, gu[fq:2 * fq].T, down]
             + _unpack_small(smalls, SMALL_SHARD_SHAPES + REPL_SHAPES))
    return loss_local, dx, grads


class _Reducer:
    def partial_sums(self, items, dtype, tag):
        c = lax.axis_index("c")
        got = _swap_halves(items, "reduce_siblings_" + tag)
        sums = []
        for i, (a, g) in enumerate(zip(items, got)):
            rh, n = g.shape[1], g.shape[2]
            own = lax.dynamic_slice_in_dim(a, c * rh, rh, axis=1)
            sums.append(_add2(own.reshape(-1, n), g.reshape(-1, n), dtype, f"reduce_add2_{tag}{i}").reshape(g.shape))
        return sums

    def scatter(self, sums, tag):
        return list(_scatter_chips(sums, "reduce_chips_" + tag))

    def finish(self, sums, from_chips, in_chip_order):
        c = lax.axis_index("c")
        me = 2 * lax.axis_index("x") + lax.axis_index("y")
        halves = []
        for i, (s, f) in enumerate(zip(sums, from_chips)):
            mine = lax.dynamic_index_in_dim(s, me, 0, keepdims=False)
            if i in in_chip_order:
                by_chip = _by_chip(mine, f)
                mine, f = by_chip[0], jnp.stack(by_chip[1:])
            halves.append(_add4(mine, f, f"reduce_add4_{i}"))
        got = _join_halves(halves, "reduce_join")
        return [jnp.where(c == 0, jnp.concatenate([h, g], axis=0), jnp.concatenate([g, h], axis=0)) for h, g in zip(halves, got)]
```

```python
import functools

import jax
import jax.numpy as jnp
from jax import lax
from jax.experimental import pallas as pl
from jax.experimental.pallas import tpu as pltpu

F32 = jnp.float32
BF = jnp.bfloat16
HI = lax.Precision.HIGHEST
MESH = pl.DeviceIdType.MESH

D = 1024
N_META = 16
CHUNK = 128
EPS = 1e-6
NH = 4
DV = 256
DQK = 128
G_RANK = 16
G_TAU = 16.0
D_FF = 2816
TM = 512
FIRST_VALID = TM - N_META
CPB = TM // CHUNK
NEG = -1e30
N_BIG = 8192
CB_GQK, CB_GV, CB_GR, CB_MQK, CB_GM, CB_GG, CB_MV, CB_MO = range(8)
N_SMALL = 128
N_ALL = N_BIG + N_SMALL
VMEM_LIMIT = 56 * 1024 * 1024

ADAM_LR, ADAM_B1, ADAM_B2, ADAM_EPS, ADAM_WD, ADAM_STEP = 0.001, 0.9, 0.999, 1e-08, 0.01, 10

NT_DIMS = (((1,), (1,)), ((), ()))
TN_DIMS = (((0,), (0,)), ((), ()))


def _nt(a, b, **kw):
    return lax.dot_general(a, b, NT_DIMS, preferred_element_type=F32, **kw)


def _tn(a, b, **kw):
    return lax.dot_general(a, b, TN_DIMS, preferred_element_type=F32, **kw)


def _nn(a, b, **kw):
    return jnp.dot(a, b, preferred_element_type=F32, **kw)


def _params(**kw):
    return pltpu.CompilerParams(vmem_limit_bytes=VMEM_LIMIT, **kw)


def _sigmoid(x):
    return 0.5 * jnp.tanh(0.5 * x) + 0.5


def _logsig(x):
    return jnp.minimum(x, 0.0) - jnp.log(1.0 + jnp.exp(-jnp.abs(x)))


def _mm_rows(rows):
    return 3 * TM if rows % (3 * TM) == 0 else TM


def _mm(a, b, *, nt, out_dtype, tn, tk=None, tm=None, name, rider=None):
    m, k = a.shape
    n = b.shape[0] if nt else b.shape[1]
    tk = k if tk is None else tk
    tm = _mm_rows(m) if tm is None else tm
    nk = k // tk
    nj, ni = n // tn, m // tm
    nr_in = len(rider["inputs"]) if rider else 0
    nr_out = len(rider["out_shapes"]) if rider else 0
    assert m % tm == 0 and n % tn == 0 and k % tk == 0
    dims = NT_DIMS if nt else (((1,), (0,)), ((), ()))

    def body(*refs):
        a_ref, b_ref = refs[:2]
        o_ref = refs[2 + nr_in]
        j, i, kk = pl.program_id(0), pl.program_id(1), pl.program_id(2)
        step = (j * ni + i) * nk + kk
        if rider:
            start, middle, finish = rider["make"](refs[2:2 + nr_in], refs[3 + nr_in:3 + nr_in + nr_out],
                                                  refs[3 + nr_in + nr_out:5 + nr_in + nr_out])
            pl.when(step == 0)(start)
            pl.when(step == (nj * ni * nk) // 2)(middle)

        part = lax.dot_general(a_ref[...].astype(BF), b_ref[...].astype(BF), dims, preferred_element_type=F32)
        if nk == 1:
            o_ref[...] = part.astype(o_ref.dtype)
        else:
            acc_ref = refs[-1]

            @pl.when(kk == 0)
            def _():
                acc_ref[...] = part

            @pl.when(jnp.logical_and(kk > 0, kk < nk - 1))
            def _():
                acc_ref[...] += part

            @pl.when(kk == nk - 1)
            def _():
                o_ref[...] = (acc_ref[...] + part).astype(o_ref.dtype)

        if rider:
            pl.when(step == nj * ni * nk - 1)(finish)

    outs = pl.pallas_call(
        body, grid=(nj, ni, nk),
        in_specs=[pl.BlockSpec((tm, tk), lambda j, i, kk: (i, kk)),
                  pl.BlockSpec((tn, tk), lambda j, i, kk: (j, kk)) if nt else pl.BlockSpec((tk, tn), lambda j, i, kk: (kk, j))]
                 + [ANY] * nr_in,
        out_specs=[pl.BlockSpec((tm, tn), lambda j, i, kk: (i, j))] + [ANY] * nr_out,
        out_shape=(jax.ShapeDtypeStruct((m, n), out_dtype),) + (tuple(rider["out_shapes"]) if rider else ()),
        scratch_shapes=(rider["sems"] if rider else []) + ([pltpu.VMEM((tm, tn), F32)] if nk > 1 else []),
        compiler_params=_params(), name=name)(a, b, *(rider["inputs"] if rider else []))
    return (outs[0], list(outs[1:])) if rider else outs[0]


def _mm_tn(a, b, *, tm, tn, tk=None, name):
    t, m = a.shape
    n = b.shape[1]
    tk = _mm_rows(t) if tk is None else tk
    assert t % tk == 0 and m % tm == 0 and n % tn == 0

    def body(a_ref, b_ref, o_ref):
        part = _tn(a_ref[...].astype(BF), b_ref[...].astype(BF))

        @pl.when(pl.program_id(2) == 0)
        def _():
            o_ref[...] = part

        @pl.when(pl.program_id(2) > 0)
        def _():
            o_ref[...] += part

    return pl.pallas_call(
        body, grid=(m // tm, n // tn, t // tk),
        in_specs=[pl.BlockSpec((tk, tm), lambda i, j, kk: (kk, i)), pl.BlockSpec((tk, tn), lambda i, j, kk: (kk, j))],
        out_specs=pl.BlockSpec((tm, tn), lambda i, j, kk: (i, j)),
        out_shape=jax.ShapeDtypeStruct((m, n), F32), compiler_params=_params(), name=name)(a, b)


ANY = pl.BlockSpec(memory_space=pl.ANY)


def _row_spec(width, col=0):
    return pl.BlockSpec((TM, width), lambda i: (i, col))


def _full_spec(shape):
    return pl.BlockSpec(shape, lambda i: (0,) * len(shape))


def _rms_fwd(h, g, name):
    tp = h.shape[0]

    def body(h_ref, g_ref, xn_ref, r_ref):
        x = h_ref[...]
        r = lax.rsqrt(jnp.mean(x * x, axis=1, keepdims=True) + EPS)
        xn_ref[...] = (x * r * g_ref[...]).astype(BF)
        r_ref[...] = r

    return pl.pallas_call(
        body, grid=(tp // TM,), in_specs=[_row_spec(D), _full_spec((1, D))],
        out_specs=[_row_spec(D), _row_spec(1)],
        out_shape=(jax.ShapeDtypeStruct((tp, D), BF), jax.ShapeDtypeStruct((tp, 1), F32)),
        compiler_params=_params(), name=name)(h, g)


def _rms_bwd(dxn, h, rstd, g, dres, name, split_first=False):
    tp = h.shape[0]

    def body(dxn_ref, h_ref, r_ref, g_ref, dres_ref, *outs):
        r = r_ref[...]
        xh = h_ref[...] * r
        dxn_v = dxn_ref[...].astype(F32)
        dxh = dxn_v * g_ref[...]
        dh = r * (dxh - xh * jnp.mean(dxh * xh, axis=1, keepdims=True)) + dres_ref[...]
        if split_first:
            first_ref, dh_ref, dg_ref = outs

            @pl.when(pl.program_id(0) == 0)
            def _():
                first_ref[...] = dh
        else:
            dh_ref, dg_ref = outs
        dh_ref[...] = dh
        part = jnp.sum(dxn_v * xh, axis=0, keepdims=True)

        @pl.when(pl.program_id(0) == 0)
        def _():
            dg_ref[...] = part

        @pl.when(pl.program_id(0) > 0)
        def _():
            dg_ref[...] += part

    if split_first:
        out_specs = [_full_spec((TM, D)), pl.BlockSpec((TM, D), lambda i: (jnp.maximum(i - 1, 0), 0)), _full_spec((1, D))]
        out_shape = (jax.ShapeDtypeStruct((TM, D), F32), jax.ShapeDtypeStruct((tp - TM, D), F32), jax.ShapeDtypeStruct((1, D), F32))
    else:
        out_specs = [_row_spec(D), _full_spec((1, D))]
        out_shape = (jax.ShapeDtypeStruct((tp, D), F32), jax.ShapeDtypeStruct((1, D), F32))
    return pl.pallas_call(
        body, grid=(tp // TM,),
        in_specs=[_row_spec(D), _row_spec(D), _row_spec(1), _full_spec((1, D)), _row_spec(D)],
        out_specs=out_specs, out_shape=out_shape, compiler_params=_params(), name=name)(dxn, h, rstd, g, dres)


def _shift_down(x, halo, k):
    rk = pltpu.roll(x, k, 0)
    io = lax.broadcasted_iota(jnp.int32, (8, x.shape[1]), 0)
    top = jnp.where(io < k, pltpu.roll(halo, k, 0), rk[0:8])
    return jnp.concatenate([top, rk[8:]], axis=0)


def _shift_up(x, nxt, k):
    n = x.shape[0]
    rk = pltpu.roll(x, n - k, 0)
    io = lax.broadcasted_iota(jnp.int32, (8, x.shape[1]), 0)
    bot = jnp.where(io >= 8 - k, pltpu.roll(nxt, 8 - k, 0), rk[n - 8:n])
    return jnp.concatenate([rk[:n - 8], bot], axis=0)


def _conv_pre(x, halo, w_ref, b_ref):
    c = x * w_ref[3:4, :] + b_ref[...]
    shifted = []
    for k in (1, 2, 3):
        s = _shift_down(x, halo, k)
        shifted.append(s)
        c = c + s * w_ref[3 - k:4 - k, :]
    return c, shifted


def _qk_scale():
    col = lax.broadcasted_iota(jnp.int32, (1, D), 1)
    return jnp.where(col < NH * DQK, DQK ** -0.5, 1.0).astype(F32)


def _halo_prev_spec():
    return pl.BlockSpec((8, D), lambda i: (jnp.maximum(i * (TM // 8) - 1, 0), CB_MQK))


def _conv_fwd(pbig, w, b, name):
    tp = pbig.shape[0]

    def body(x_ref, halo_ref, w_ref, b_ref, o_ref):
        x = x_ref[...].astype(F32)
        halo = jnp.where(pl.program_id(0) > 0, halo_ref[...].astype(F32), 0.0)
        c, _ = _conv_pre(x, halo, w_ref, b_ref)
        o_ref[...] = (c * _sigmoid(c) * _qk_scale()).astype(BF)

    return pl.pallas_call(
        body, grid=(tp // TM,),
        in_specs=[_row_spec(D, CB_MQK), _halo_prev_spec(), _full_spec((4, D)), _full_spec((1, D))],
        out_specs=_row_spec(D), out_shape=jax.ShapeDtypeStruct((tp, D), BF),
        compiler_params=_params(), name=name)(pbig, pbig, w, b)


def _conv_bwd_pre(dqk, pbig, w, b, name):
    tp = pbig.shape[0]

    def body(d_ref, x_ref, halo_ref, w_ref, b_ref, dc_ref, dwb_ref):
        x = x_ref[...].astype(F32)
        halo = jnp.where(pl.program_id(0) > 0, halo_ref[...].astype(F32), 0.0)
        c, shifted = _conv_pre(x, halo, w_ref, b_ref)
        sg = _sigmoid(c)
        dc = d_ref[...] * _qk_scale() * (sg * (1.0 + c * (1.0 - sg)))
        dc_ref[...] = dc
        taps = [shifted[2], shifted[1], shifted[0], x]
        rows = [jnp.sum(dc * t, axis=0, keepdims=True) for t in taps] + [jnp.sum(dc, axis=0, keepdims=True)]
        io = lax.broadcasted_iota(jnp.int32, (8, D), 0)
        part = jnp.zeros((8, D), F32)
        for r, v in enumerate(rows):
            part = jnp.where(io == r, v, part)

        @pl.when(pl.program_id(0) == 0)
        def _():
            dwb_ref[...] = part

        @pl.when(pl.program_id(0) > 0)
        def _():
            dwb_ref[...] += part

    return pl.pallas_call(
        body, grid=(tp // TM,),
        in_specs=[_row_spec(D), _row_spec(D, CB_MQK), _halo_prev_spec(), _full_spec((4, D)), _full_spec((1, D))],
        out_specs=[_row_spec(D), _full_spec((8, D))],
        out_shape=(jax.ShapeDtypeStruct((tp, D), F32), jax.ShapeDtypeStruct((8, D), F32)),
        compiler_params=_params(), name=name)(dqk, pbig, pbig, w, b)


def _conv_bwd_in(dc, w, dproj, name):
    tp = dc.shape[0]
    nb = tp // TM

    def body(d_ref, nxt_ref, w_ref, _, o_ref):
        d = d_ref[...]
        nxt = jnp.where(pl.program_id(0) < nb - 1, nxt_ref[...], 0.0)
        acc = d * w_ref[3:4, :]
        for k in (1, 2, 3):
            acc = acc + _shift_up(d, nxt, k) * w_ref[3 - k:4 - k, :]
        o_ref[...] = acc.astype(BF)

    return pl.pallas_call(
        body, grid=(nb,),
        in_specs=[_row_spec(D), pl.BlockSpec((8, D), lambda i: (jnp.minimum((i + 1) * (TM // 8), tp // 8 - 1), 0)),
                  _full_spec((4, D)), ANY],
        out_specs=_row_spec(D, CB_MQK), out_shape=jax.ShapeDtypeStruct(dproj.shape, BF),
        input_output_aliases={3: 0}, compiler_params=_params(), name=name)(dc, dc, w, dproj)


def _mm_fused(inputs, products, *, nt, m, n, tm, tn, outs, epilogue, name, nk=1, sub=None):
    dims = NT_DIMS if nt else (((1,), (0,)), ((), ()))
    nin = len(inputs)
    assert nk == 1 or (len(products) == 1 and sub is None)

    def body(*refs):
        in_refs, out_refs = refs[:nin], refs[nin:nin + len(outs)]
        i = pl.program_id(1)
        if sub is not None:
            lhs = {ia: in_refs[ia][...].astype(BF) for ia, _ in products}

            def dots(cols):
                return [lax.dot_general(lhs[ia], (in_refs[ib][cols, :] if nt else in_refs[ib][:, cols]).astype(BF),
                                        dims, preferred_element_type=F32) for ia, ib in products]

            slices = [slice(s, min(s + sub, tn)) for s in range(0, tn, sub)]
            prods = dots(slices[0])
            for idx, cols in enumerate(slices):
                nxt = dots(slices[idx + 1]) if idx + 1 < len(slices) else None
                epilogue(prods, in_refs, out_refs, i, cols)
                prods = nxt
            return
        prods = [lax.dot_general(in_refs[ia][...].astype(BF), in_refs[ib][...].astype(BF), dims, preferred_element_type=F32)
                 for ia, ib in products]
        if nk == 1:
            epilogue(prods, in_refs, out_refs, i, slice(None))
            return
        acc_ref = refs[-1]
        kk = pl.program_id(2)

        @pl.when(kk == 0)
        def _():
            acc_ref[...] = prods[0]

        @pl.when(jnp.logical_and(kk > 0, kk < nk - 1))
        def _():
            acc_ref[...] += prods[0]

        @pl.when(kk == nk - 1)
        def _():
            epilogue([acc_ref[...] + prods[0]], in_refs, out_refs, i, slice(None))

    return pl.pallas_call(
        body, grid=(n // tn, m // tm, nk), in_specs=[s for _, s in inputs], out_specs=[s for _, s in outs],
        out_shape=tuple(sh for sh, _ in outs), scratch_shapes=[pltpu.VMEM((tm, tn), F32)] if nk > 1 else [],
        compiler_params=_params(), name=name)(*[a for a, _ in inputs])


SUB_COLS = 256


def _cols_at(cols, offset):
    return slice(cols.start + offset, cols.stop + offset)


def _blk(rows, width, col=None, row=None):
    return pl.BlockSpec((rows, width), lambda j, i, kk: ((i if row is None else row(i)), (0 if col is None else col(j, kk))))


FF_TN = D_FF // 2


def _ffn_weight_rows(wg_t, wu_t):
    return jnp.concatenate([wg_t[0:FF_TN], wu_t[0:FF_TN], wg_t[FF_TN:], wu_t[FF_TN:]], axis=0)


def _ffn_in(hn, wgu_t, name):
    tp = hn.shape[0]
    tm = _mm_rows(tp)

    def epilogue(prods, in_refs, out_refs, i, cols):
        g, u = prods
        out_refs[0][:, cols] = g.astype(BF)
        out_refs[0][:, _cols_at(cols, FF_TN)] = u.astype(BF)
        out_refs[1][:, cols] = (g * _sigmoid(g) * u).astype(BF)

    wspec = lambda off: pl.BlockSpec((FF_TN, D), lambda j, i, kk: (2 * j + off, 0))
    return _mm_fused(
        [(hn, _blk(tm, D)), (wgu_t, wspec(0)), (wgu_t, wspec(1))], [(0, 1), (0, 2)], nt=True, m=tp, n=D_FF, tm=tm, tn=FF_TN,
        outs=[(jax.ShapeDtypeStruct((tp, 2 * D_FF), BF), _blk(tm, 2 * FF_TN, lambda j, kk: j)),
              (jax.ShapeDtypeStruct((tp, D_FF), BF), _blk(tm, FF_TN, lambda j, kk: j))],
        epilogue=epilogue, name=name, sub=SUB_COLS)


def _ffn_down_loss(ff, wdown, h1, target, gf, name):
    tp = ff.shape[0]

    def epilogue(prods, in_refs, out_refs, i, cols):
        live = (i > 0).astype(F32)
        g = in_refs[4][...]
        x = prods[0] + in_refs[2][...]
        r = lax.rsqrt(jnp.mean(x * x, axis=1, keepdims=True) + EPS)
        xh = x * r
        e = xh * g - in_refs[3][...]
        loss_part = 0.5 * live * jnp.sum(jnp.mean(e * e, axis=1, keepdims=True), axis=0, keepdims=True)
        dout = e * (live / D)
        dg_part = jnp.sum(dout * xh, axis=0, keepdims=True)
        dxh = dout * g
        out_refs[0][...] = r * (dxh - xh * jnp.mean(dxh * xh, axis=1, keepdims=True))

        @pl.when(i == 0)
        def _():
            out_refs[1][...] = loss_part
            out_refs[2][...] = dg_part

        @pl.when(i > 0)
        def _():
            out_refs[1][...] += loss_part
            out_refs[2][...] += dg_part

    const = lambda shape: pl.BlockSpec(shape, lambda j, i, kk: (0,) * len(shape))
    return _mm_fused(
        [(ff, _blk(TM, D_FF)), (wdown, const((D_FF, D))), (h1, _blk(TM, D)),
         (target, _blk(TM, D, row=lambda i: jnp.maximum(i - 1, 0))), (gf, const((1, D)))],
        [(0, 1)], nt=False, m=tp, n=D, tm=TM, tn=D,
        outs=[(jax.ShapeDtypeStruct((tp, D), F32), _blk(TM, D)), (jax.ShapeDtypeStruct((1, 1), F32), const((1, 1))),
              (jax.ShapeDtypeStruct((1, D), F32), const((1, D)))],
        epilogue=epilogue, name=name)


def _ffn_d_hidden(dh2, wdown, gu, name):
    tp = dh2.shape[0]

    def epilogue(prods, in_refs, out_refs, i, cols):
        d = prods[0]
        g = in_refs[2][:, cols].astype(F32)
        u = in_refs[2][:, _cols_at(cols, FF_TN)].astype(F32)
        sg = _sigmoid(g)
        out_refs[0][:, cols] = (d * u * sg * (1.0 + g * (1.0 - sg))).astype(BF)
        out_refs[0][:, _cols_at(cols, FF_TN)] = (d * g * sg).astype(BF)

    return _mm_fused(
        [(dh2, _blk(TM, D)), (wdown, pl.BlockSpec((FF_TN, D), lambda j, i, kk: (j, 0))), (gu, _blk(TM, 2 * FF_TN, lambda j, kk: j))],
        [(0, 1)], nt=True, m=tp, n=D_FF, tm=TM, tn=FF_TN,
        outs=[(jax.ShapeDtypeStruct((tp, 2 * D_FF), BF), _blk(TM, 2 * FF_TN, lambda j, kk: j))],
        epilogue=epilogue, name=name, sub=SUB_COLS)[0]


def _ffn_d_in(dgu, wgu_t, h1, rstd, g2, dh2, name):
    tp = dgu.shape[0]
    nk = 2

    def epilogue(prods, in_refs, out_refs, i, cols):
        r = in_refs[3][...]
        xh = in_refs[2][...] * r
        dxn = prods[0]
        dxh = dxn * in_refs[4][...]
        out_refs[0][...] = r * (dxh - xh * jnp.mean(dxh * xh, axis=1, keepdims=True)) + in_refs[5][...]
        part = jnp.sum(dxn * xh, axis=0, keepdims=True)

        @pl.when(i == 0)
        def _():
            out_refs[1][...] = part

        @pl.when(i > 0)
        def _():
            out_refs[1][...] += part

    const = lambda shape: pl.BlockSpec(shape, lambda j, i, kk: (0,) * len(shape))
    return _mm_fused(
        [(dgu, pl.BlockSpec((TM, D_FF), lambda j, i, kk: (i, kk))), (wgu_t, pl.BlockSpec((D_FF, D), lambda j, i, kk: (kk, 0))),
         (h1, _blk(TM, D)), (rstd, _blk(TM, 1)), (g2, const((1, D))), (dh2, _blk(TM, D))],
        [(0, 1)], nt=False, m=tp, n=D, tm=TM, tn=D, nk=nk,
        outs=[(jax.ShapeDtypeStruct((tp, D), F32), _blk(TM, D)), (jax.ShapeDtypeStruct((1, D), F32), const((1, D)))],
        epilogue=epilogue, name=name)


def _branch_merge(y_m, y_g, wbm, wbg, pbig, name):
    tp = y_m.shape[0]

    def epilogue(prods, in_refs, out_refs, i, cols):
        pm, pg = prods[0].astype(BF), prods[1].astype(BF)
        out_refs[0][:, cols] = pm
        out_refs[1][:, cols] = pg
        out_refs[2][:, cols] = (_sigmoid(in_refs[4][:, cols].astype(F32)) * pm.astype(F32)
                                + _sigmoid(in_refs[5][:, cols].astype(F32)) * pg.astype(F32)).astype(BF)

    const = lambda shape: pl.BlockSpec(shape, lambda j, i, kk: (0,) * len(shape))
    shp = jax.ShapeDtypeStruct((tp, D), BF)
    return _mm_fused(
        [(y_m, _blk(TM, D)), (wbm, const((D, D))), (y_g, _blk(TM, D)), (wbg, const((D, D))),
         (pbig, _blk(TM, D, lambda j, kk: CB_GM)), (pbig, _blk(TM, D, lambda j, kk: CB_GG))],
        [(0, 1), (2, 3)], nt=False, m=tp, n=D, tm=TM, tn=D,
        outs=[(shp, _blk(TM, D)), (shp, _blk(TM, D)), (shp, _blk(TM, D))], epilogue=epilogue, name=name, sub=SUB_COLS)


def _merge_d(dh1, wout, pm, pg, pbig, name):
    tp = dh1.shape[0]

    def epilogue(prods, in_refs, out_refs, i, cols):
        d = prods[0]
        sm = _sigmoid(in_refs[4][:, cols].astype(F32))
        sg = _sigmoid(in_refs[5][:, cols].astype(F32))
        out_refs[0][:, cols] = (d * sm).astype(BF)
        out_refs[1][:, cols] = (d * sg).astype(BF)
        out_refs[2][:, cols] = (d * in_refs[2][:, cols].astype(F32) * sm * (1.0 - sm)).astype(BF)
        out_refs[2][:, _cols_at(cols, D)] = (d * in_refs[3][:, cols].astype(F32) * sg * (1.0 - sg)).astype(BF)

    const = lambda shape: pl.BlockSpec(shape, lambda j, i, kk: (0,) * len(shape))
    shp = jax.ShapeDtypeStruct((tp, D), BF)
    return _mm_fused(
        [(dh1, _blk(TM, D)), (wout, const((D, D))), (pm, _blk(TM, D)), (pg, _blk(TM, D)),
         (pbig, _blk(TM, D, lambda j, kk: CB_GM)), (pbig, _blk(TM, D, lambda j, kk: CB_GG))],
        [(0, 1)], nt=True, m=tp, n=D, tm=TM, tn=D,
        outs=[(shp, _blk(TM, D)), (shp, _blk(TM, D)),
              (jax.ShapeDtypeStruct((tp, N_ALL), BF), _blk(TM, 2 * D, lambda j, kk: CB_GM // 2))],
        epilogue=epilogue, name=name, sub=SUB_COLS)


def _out_proj_norm(merged, wout, h0, g2, name):
    tp = merged.shape[0]
    tm = _mm_rows(tp)

    def epilogue(prods, in_refs, out_refs, i, cols):
        x = prods[0] + in_refs[2][...]
        r = lax.rsqrt(jnp.mean(x * x, axis=1, keepdims=True) + EPS)
        out_refs[0][...] = x
        out_refs[1][...] = (x * r * in_refs[3][...]).astype(BF)
        out_refs[2][...] = r

    const = lambda shape: pl.BlockSpec(shape, lambda j, i, kk: (0,) * len(shape))
    return _mm_fused(
        [(merged, _blk(tm, D)), (wout, const((D, D))), (h0, _blk(tm, D)), (g2, const((1, D)))],
        [(0, 1)], nt=False, m=tp, n=D, tm=tm, tn=D,
        outs=[(jax.ShapeDtypeStruct((tp, D), F32), _blk(tm, D)), (jax.ShapeDtypeStruct((tp, D), BF), _blk(tm, D)),
              (jax.ShapeDtypeStruct((tp, 1), F32), _blk(tm, 1))],
        epilogue=epilogue, name=name)


def _adamw(w, g, m, v, name):
    rows, cols = w.shape
    tr = 128 if rows % 128 == 0 else rows

    def body(w_ref, g_ref, m_ref, v_ref, d_ref, nm_ref, nv_ref):
        gv = g_ref[...]
        nm = ADAM_B1 * m_ref[...] + (1.0 - ADAM_B1) * gv
        nv = ADAM_B2 * v_ref[...] + (1.0 - ADAM_B2) * (gv * gv)
        m_hat = nm / (1.0 - ADAM_B1 ** ADAM_STEP)
        v_hat = nv / (1.0 - ADAM_B2 ** ADAM_STEP)
        d_ref[...] = -ADAM_LR * (m_hat / (jnp.sqrt(v_hat) + ADAM_EPS) + ADAM_WD * w_ref[...])
        nm_ref[...] = nm
        nv_ref[...] = nv

    spec = pl.BlockSpec((tr, cols), lambda i: (i, 0))
    shp = jax.ShapeDtypeStruct((rows, cols), F32)
    return pl.pallas_call(body, grid=(rows // tr,), in_specs=[spec] * 4, out_specs=[spec] * 3,
                          out_shape=(shp,) * 3, compiler_params=_params(), name=name)(w, g, m, v)


def _place_small(dsmall, dproj, name):
    tp = dsmall.shape[0]

    def body(s_ref, _, o_ref):
        o_ref[...] = s_ref[...]

    return pl.pallas_call(
        body, grid=(tp // TM,), in_specs=[_row_spec(N_SMALL), ANY], out_specs=_row_spec(N_SMALL, N_BIG // N_SMALL),
        out_shape=jax.ShapeDtypeStruct(dproj.shape, dproj.dtype), input_output_aliases={1: 0},
        compiler_params=_params(), name=name)(dsmall, dproj)


def _row_tile(rows, cap=512):
    best = rows
    for cand in range(8, min(rows, cap) + 1, 8):
        if rows % cand == 0:
            best = cand
    return best


def _add2(a, b, out_dtype, name):
    rows, cols = a.shape
    tr = _row_tile(rows)

    def body(a_ref, b_ref, o_ref):
        o_ref[...] = (a_ref[...] + b_ref[...]).astype(o_ref.dtype)

    spec = pl.BlockSpec((tr, cols), lambda i: (i, 0))
    return pl.pallas_call(body, grid=(rows // tr,), in_specs=[spec] * 2, out_specs=spec,
                          out_shape=jax.ShapeDtypeStruct((rows, cols), out_dtype), compiler_params=_params(), name=name)(a, b)


def _add4(first, rest, name):
    rows, cols = first.shape
    tr = _row_tile(rows, 256)

    def body(f_ref, r_ref, o_ref):
        up = lambda v: v.astype(F32)
        o_ref[...] = ((up(f_ref[...]) + up(r_ref[0])) + up(r_ref[1])) + up(r_ref[2])

    return pl.pallas_call(body, grid=(rows // tr,),
                          in_specs=[pl.BlockSpec((tr, cols), lambda i: (i, 0)), pl.BlockSpec((3, tr, cols), lambda i: (0, i, 0))],
                          out_specs=pl.BlockSpec((tr, cols), lambda i: (i, 0)),
                          out_shape=jax.ShapeDtypeStruct((rows, cols), F32), compiler_params=_params(), name=name)(first, rest)


def _chunk_consts():
    r2 = lax.broadcasted_iota(jnp.int32, (CHUNK, CHUNK), 0)
    c2 = lax.broadcasted_iota(jnp.int32, (CHUNK, CHUNK), 1)
    tri = r2 >= c2
    return dict(tri=tri, tril_f=tri.astype(F32), triu_f=(r2 <= c2).astype(F32),
                lane=lax.broadcasted_iota(jnp.int32, (CHUNK, N_SMALL), 1),
                rowio=lax.broadcasted_iota(jnp.int32, (CHUNK, 1), 0),
                ones=jnp.ones((CHUNK, N_SMALL), F32))


def _valid_rows(block, c):
    row = block * TM + c * CHUNK + lax.broadcasted_iota(jnp.int32, (CHUNK, 1), 0)
    return row >= FIRST_VALID


def _col(x, lane, idx):
    return jnp.sum(jnp.where(lane == idx, x, 0.0), axis=1, keepdims=True)


def _last_row(x, rowio):
    return jnp.sum(jnp.where(rowio == CHUNK - 1, x, 0.0), axis=0, keepdims=True)


def _sum_all(x):
    return jnp.sum(jnp.sum(x, axis=1, keepdims=True), axis=0, keepdims=True)


def _headnorm_fwd(hm, gain, gate_act):
    rs = lax.rsqrt(jnp.mean(hm * hm, axis=1, keepdims=True) + EPS)
    return hm * rs * gain * gate_act


def _headnorm_bwd(dy, hm, gain, gate_act):
    rs = lax.rsqrt(jnp.mean(hm * hm, axis=1, keepdims=True) + EPS)
    xh = hm * rs
    dact = dy * xh * gain
    dgain = jnp.sum(dy * gate_act * xh, axis=0, keepdims=True)
    dxh = dy * gate_act * gain
    dhm = rs * (dxh - xh * jnp.mean(dxh * xh, axis=1, keepdims=True))
    return dhm, dact, dgain


def _mlstm_gates(sm, gbias, valid, k):
    pre = sm + gbias
    lf = jnp.where(valid, _logsig(pre), 0.0)
    b_all = _nn(k["tril_f"], lf, precision=HI)
    li_all = jnp.where(valid, pre, NEG)
    return pre, li_all, b_all


def _mlstm_open(h, qh, kh, c_st, li_all, b_all, k):
    lane = k["lane"]
    sel = jnp.where(lane == h, 1.0, 0.0) - jnp.where(lane == NH + h, 1.0, 0.0)
    x = jnp.where(lane < NH, li_all, jnp.where(lane < 2 * NH, b_all, 0.0))
    cb = c_st.astype(BF)
    return dict(ubc=_nt(sel, x, precision=HI), sim=_nt(qh, kh), cb=cb, cq=_nt(qh, cb))


def _mlstm_weights(h, f, qh, vh, li_all, b_all, n_row, m11, k):
    lane, tri, rowio = k["lane"], k["tri"], k["rowio"]
    b_col = _col(b_all, lane, NH + h)
    li_col = _col(li_all, lane, h)
    dmat = jnp.where(tri, b_col + f["ubc"], NEG)
    m_row = jnp.maximum(b_col + m11, jnp.max(dmat, axis=1, keepdims=True))
    e = jnp.exp(dmat - m_row)
    w_mat = e * f["sim"]
    a = jnp.exp(b_col + m11 - m_row)
    qf = qh.astype(F32)
    nq = jnp.sum(qf * n_row, axis=1, keepdims=True)
    g = _last_row(b_col, rowio)
    wlog = g - b_col + li_col
    m_new = jnp.maximum(g + m11, jnp.max(wlog, axis=0, keepdims=True))
    a_s = jnp.exp(g + m11 - m_new)
    w = jnp.exp(wlog - m_new)
    return dict(f, e=e, w_mat=w_mat, a=a, qf=qf, nq=nq, m_row=m_row, m_new=m_new, a_s=a_s, w=w,
                wv=_nn(w_mat.astype(BF), vh))


def _mlstm_out(f):
    num = f["a"] * f["cq"] + f["wv"]
    den = f["a"] * f["nq"] + jnp.sum(f["w_mat"], axis=1, keepdims=True)
    floor = jnp.exp(-f["m_row"])
    r = jnp.maximum(jnp.abs(den), floor)
    return dict(f, den=den, floor=floor, r=r, hm=num / r)


def _mlstm_fwd(qk, pbig, small, gbias, headg, name):
    tp = qk.shape[0]
    nb = tp // TM

    def body(qk_ref, v_ref, mo_ref, sm_ref, gb_ref, hg_ref, y_ref, cs_ref, ns_ref, c_scr, n_scr):
        blk = pl.program_id(0)

        @pl.when(blk == 0)
        def _():
            c_scr[...] = jnp.zeros_like(c_scr)
            n_scr[...] = jnp.zeros_like(n_scr)

        k = _chunk_consts()
        io8 = lax.broadcasted_iota(jnp.int32, (8, DQK), 0)

        def chunk(c, carry):
            r0 = pl.multiple_of(c * CHUNK, CHUNK)
            rows = pl.ds(r0, CHUNK)
            valid = _valid_rows(blk, c)
            _, li_all, b_all = _mlstm_gates(sm_ref[rows, :], gb_ref[...], valid, k)
            heads = range(NH)
            qs = [qk_ref[rows, h * DQK:(h + 1) * DQK] for h in heads]
            ks = [qk_ref[rows, NH * DQK + h * DQK:NH * DQK + (h + 1) * DQK] for h in heads]
            vs = [v_ref[rows, h * DV:(h + 1) * DV] for h in heads]
            cst = [c_scr[h] for h in heads]
            nrow = [n_scr[h, 0:1, :] for h in heads]
            m11 = [jnp.max(n_scr[h, 1:2, :], axis=1, keepdims=True) for h in heads]
            f = [_mlstm_open(h, qs[h], ks[h], cst[h], li_all, b_all, k) for h in heads]
            f = [_mlstm_weights(h, f[h], qs[h], vs[h], li_all, b_all, nrow[h], m11[h], k) for h in heads]
            wk = [f[h]["w"] * ks[h].astype(F32) for h in heads]
            kv = [_tn(vs[h], wk[h].astype(BF)) for h in heads]
            for h in heads:
                hm = _mlstm_out(f[h])["hm"]
                gate = _sigmoid(mo_ref[rows, h * DV:(h + 1) * DV].astype(F32))
                y_ref[rows, h * DV:(h + 1) * DV] = _headnorm_fwd(hm, hg_ref[:, h * DV:(h + 1) * DV], gate).astype(BF)
                cs_ref[c, h] = f[h]["cb"]
                ns_ref[c, h] = jnp.where(io8 == 0, nrow[h], jnp.where(io8 == 1, m11[h], 0.0))
                c_scr[h] = f[h]["a_s"] * cst[h] + kv[h]
                n_scr[h, 0:1, :] = f[h]["a_s"] * nrow[h] + jnp.sum(wk[h], axis=0, keepdims=True)
                n_scr[h, 1:2, :] = jnp.broadcast_to(f[h]["m_new"], (1, DQK))
            return carry

        lax.fori_loop(0, CPB, chunk, 0, unroll=2)

    return pl.pallas_call(
        body, grid=(nb,),
        in_specs=[_row_spec(D), _row_spec(D, CB_MV), _row_spec(D, CB_MO), _row_spec(N_SMALL), _full_spec((1, N_SMALL)), _full_spec((1, D))],
        out_specs=[_row_spec(D), pl.BlockSpec((CPB, NH, DV, DQK), lambda i: (i, 0, 0, 0)),
                   pl.BlockSpec((CPB, NH, 8, DQK), lambda i: (i, 0, 0, 0))],
        out_shape=(jax.ShapeDtypeStruct((tp, D), BF), jax.ShapeDtypeStruct((tp // CHUNK, NH, DV, DQK), BF),
                   jax.ShapeDtypeStruct((tp // CHUNK, NH, 8, DQK), F32)),
        scratch_shapes=[pltpu.VMEM((NH, DV, DQK), F32), pltpu.VMEM((NH, 8, DQK), F32)],
        compiler_params=_params(), name=name)(qk, pbig, pbig, small, gbias, headg)


def _mlstm_bwd(dy, qk, pbig, small, gbias, headg, cs, ns, dproj, name, ride=()):
    tp = qk.shape[0]
    nb = tp // TM
    nr = len(ride)

    def body(*refs):
        dy_ref, qk_ref, v_ref, mo_ref, sm_ref, gb_ref, hg_ref, cs_ref, ns_ref = refs[:9]
        ride_in = refs[10:10 + nr]
        dqk_ref, dproj_ref, dsm_ref, dgb_ref, dhg_ref = refs[10 + nr:15 + nr]
        ride_out = refs[15 + nr:15 + 2 * nr]
        dc_scr, dn_scr = refs[15 + 2 * nr:17 + 2 * nr]
        step = pl.program_id(0)
        blk = nb - 1 - step
        sent = _scatter_copies(ride_in, ride_out, *refs[17 + 2 * nr:]) if nr else []

        @pl.when(step == 0)
        def _():
            dc_scr[...] = jnp.zeros_like(dc_scr)
            dn_scr[...] = jnp.zeros_like(dn_scr)
            dgb_ref[...] = jnp.zeros_like(dgb_ref)
            dhg_ref[...] = jnp.zeros_like(dhg_ref)
            for cp in sent:
                cp.start()

        k = _chunk_consts()
        lane, rowio = k["lane"], k["rowio"]

        def chunk(cc, carry):
            c = CPB - 1 - cc
            r0 = pl.multiple_of(c * CHUNK, CHUNK)
            rows = pl.ds(r0, CHUNK)
            valid = _valid_rows(blk, c)
            pre, li_all, b_all = _mlstm_gates(sm_ref[rows, :], gb_ref[...], valid, k)
            dli_all = jnp.zeros((CHUNK, N_SMALL), F32)
            db_all = jnp.zeros((CHUNK, N_SMALL), F32)
            heads = range(NH)
            qs = [qk_ref[rows, h * DQK:(h + 1) * DQK] for h in heads]
            ks = [qk_ref[rows, NH * DQK + h * DQK:NH * DQK + (h + 1) * DQK] for h in heads]
            vs = [v_ref[rows, h * DV:(h + 1) * DV] for h in heads]
            cst = [cs_ref[c, h].astype(F32) for h in heads]
            nrow = [ns_ref[c, h, 0:1, :] for h in heads]
            m11 = [jnp.max(ns_ref[c, h, 1:2, :], axis=1, keepdims=True) for h in heads]
            f = [_mlstm_open(h, qs[h], ks[h], cst[h], li_all, b_all, k) for h in heads]
            f = [_mlstm_weights(h, f[h], qs[h], vs[h], li_all, b_all, nrow[h], m11[h], k) for h in heads]
            f = [_mlstm_out(f[h]) for h in heads]
            t = []
            for h in heads:
                gain = hg_ref[:, h * DV:(h + 1) * DV]
                gate = _sigmoid(mo_ref[rows, h * DV:(h + 1) * DV].astype(F32))
                dhm, dgate, dgain = _headnorm_bwd(dy_ref[rows, h * DV:(h + 1) * DV].astype(F32), f[h]["hm"], gain, gate)
                dproj_ref[rows, D + h * DV:D + (h + 1) * DV] = (dgate * gate * (1.0 - gate)).astype(BF)
                dhg_ref[:, h * DV:(h + 1) * DV] += dgain
                r, den = f[h]["r"], f[h]["den"]
                dnum = dhm / r
                dr = -jnp.sum(dhm * f[h]["hm"], axis=1, keepdims=True) / r
                dden = jnp.where(jnp.abs(den) > f[h]["floor"], dr * jnp.sign(den), 0.0)
                dnb = dnum.astype(BF)
                dc_new = dc_scr[h]
                dcb = dc_new.astype(BF)
                t.append(dict(dnum=dnum, dden=dden, dnb=dnb, dc_new=dc_new, dn_new=dn_scr[h],
                              dwm=_nt(dnb, vs[h]), vdc=_nn(vs[h], dcb), kdc=_nt(ks[h], dcb)))
            for h in heads:
                dw_mat = t[h]["dwm"] + t[h]["dden"]
                dsim = (f[h]["e"] * dw_mat).astype(BF)
                gm = f[h]["w_mat"] * dw_mat
                t[h].update(gm=gm, dv0=_tn(f[h]["w_mat"].astype(BF), t[h]["dnb"]), dq0=_nn(dsim, ks[h]),
                            dq1=_nn(t[h]["dnb"], f[h]["cb"]), dk0=_tn(dsim, qs[h]),
                            dcq=_tn((f[h]["a"] * t[h]["dnum"]).astype(BF), qs[h]), cs2=_tn(gm, k["ones"], precision=HI))
            for h in heads:
                a, w, a_s = f[h]["a"], f[h]["w"], f[h]["a_s"]
                dnum, dden, dc_new, dn_new, vdc, gm = (t[h][n] for n in ("dnum", "dden", "dc_new", "dn_new", "vdc", "gm"))
                kf = ks[h].astype(F32)
                dproj_ref[rows, h * DV:(h + 1) * DV] = (t[h]["dv0"] + w * t[h]["kdc"]).astype(BF)
                adden = a * dden
                dqk_ref[rows, h * DQK:(h + 1) * DQK] = t[h]["dq0"] + a * t[h]["dq1"] + adden * nrow[h]
                dqk_ref[rows, NH * DQK + h * DQK:NH * DQK + (h + 1) * DQK] = t[h]["dk0"] + w * vdc + w * dn_new
                da = jnp.sum(dnum * f[h]["cq"], axis=1, keepdims=True) + dden * f[h]["nq"]
                dw = jnp.sum(vdc * kf, axis=1, keepdims=True) + jnp.sum(kf * dn_new, axis=1, keepdims=True)
                da_s = _sum_all(dc_new * cst[h]) + jnp.sum(dn_new * nrow[h], axis=1, keepdims=True)
                wdw = w * dw
                rs = jnp.sum(gm, axis=1, keepdims=True)
                cs_col = _col(t[h]["cs2"], lane, 0)
                dg = a_s * da_s + jnp.sum(wdw, axis=0, keepdims=True)
                db = a * da + rs - cs_col - wdw + jnp.where(rowio == CHUNK - 1, dg, 0.0)
                dli_all = dli_all + jnp.where(lane == h, cs_col + wdw, 0.0)
                db_all = db_all + jnp.where(lane == NH + h, db, 0.0)
                dc_scr[h] = a_s * dc_new + t[h]["dcq"]
                dn_scr[h] = a_s * dn_new + jnp.sum(adden * f[h]["qf"], axis=0, keepdims=True)
            dlf_all = _nn(k["triu_f"], db_all, precision=HI)
            dsm = jnp.where(valid, dli_all + dlf_all * _sigmoid(-pre), 0.0)
            dsm = jnp.where(lane < 2 * NH, dsm, 0.0)
            dsm_ref[rows, :] = dsm
            dgb_ref[0:1, :] += jnp.sum(dsm, axis=0, keepdims=True)
            return carry

        lax.fori_loop(0, CPB, chunk, 0, unroll=2)

        if nr:
            @pl.when(step == nb - 1)
            def _():
                for cp in sent:
                    cp.wait_recv()
                for cp in sent:
                    cp.wait_send()

    rev = lambda col: (lambda i: (nb - 1 - i, col))
    rspec = lambda width, col=0: pl.BlockSpec((TM, width), rev(col))
    ride_shapes, ride_sems = _scatter_shapes(ride) if nr else ((), [])
    outs = pl.pallas_call(
        body, grid=(nb,),
        in_specs=[rspec(D), rspec(D), rspec(D, CB_MV), rspec(D, CB_MO), rspec(N_SMALL), _full_spec((1, N_SMALL)), _full_spec((1, D)),
                  pl.BlockSpec((CPB, NH, DV, DQK), lambda i: (nb - 1 - i, 0, 0, 0)),
                  pl.BlockSpec((CPB, NH, 8, DQK), lambda i: (nb - 1 - i, 0, 0, 0)), ANY] + [ANY] * nr,
        out_specs=[rspec(D), rspec(2 * D, CB_MV // 2), rspec(N_SMALL), _full_spec((8, N_SMALL)), _full_spec((1, D))] + [ANY] * nr,
        out_shape=(jax.ShapeDtypeStruct((tp, D), F32), jax.ShapeDtypeStruct(dproj.shape, BF),
                   jax.ShapeDtypeStruct((tp, N_SMALL), F32), jax.ShapeDtypeStruct((8, N_SMALL), F32),
                   jax.ShapeDtypeStruct((1, D), F32)) + tuple(ride_shapes),
        scratch_shapes=[pltpu.VMEM((NH, DV, DQK), F32), pltpu.VMEM((NH, 1, DQK), F32)] + ride_sems,
        input_output_aliases={9: 1}, compiler_params=_params(), name=name)(dy, qk, pbig, pbig, small, gbias, headg, cs, ns, dproj, *ride)
    return tuple(outs[:5]) + (list(outs[5:]),)


def _gla_loga(sm_ref, a2_ref, a2b_ref, blk):
    za = _nn(sm_ref[...].astype(BF), a2_ref[...]) + a2b_ref[...]
    row = blk * TM + lax.broadcasted_iota(jnp.int32, (TM, 1), 0)
    return za, jnp.where(row >= FIRST_VALID, _logsig(za) / G_TAU, 0.0)


def _gla_head(h, q_ref, k_ref, rows, bc, btot, k):
    sl = slice(h * DQK, (h + 1) * DQK)
    bch = bc[:, sl]
    bth = btot[:, sl]
    gq = q_ref[rows, h * DQK:(h + 1) * DQK].astype(F32)
    gk = k_ref[rows, NH * DQK + h * DQK:NH * DQK + (h + 1) * DQK].astype(F32)
    e_pos = jnp.exp(bch) * (DQK ** -0.5)
    e_neg = jnp.exp(-bch)
    e_end = jnp.exp(bth - bch)
    qd = gq * e_pos
    ki = gk * e_neg
    ke = gk * e_end
    att = jnp.where(k["tri"], _nt(qd.astype(BF), ki.astype(BF)), 0.0)
    return dict(e_pos=e_pos, e_neg=e_neg, e_end=e_end, qd=qd, ki=ki, ke=ke, att=att, decay=jnp.exp(bth))


def _gla_fwd(pbig, small, a2p, a2b, headg, name):
    tp = pbig.shape[0]
    nb = tp // TM

    def body(qk_ref, v_ref, gr_ref, sm_ref, a2_ref, a2b_ref, hg_ref, y_ref, ss_ref, s_scr, lg_scr):
        blk = pl.program_id(0)

        @pl.when(blk == 0)
        def _():
            s_scr[...] = jnp.zeros_like(s_scr)

        k = _chunk_consts()
        _, loga = _gla_loga(sm_ref, a2_ref, a2b_ref, blk)
        lg_scr[...] = loga

        def chunk(c, carry):
            r0 = pl.multiple_of(c * CHUNK, CHUNK)
            rows = pl.ds(r0, CHUNK)
            bc = _nn(k["tril_f"], lg_scr[rows, :], precision=HI)
            btot = _last_row(bc, k["rowio"])
            heads = range(NH)
            f = [_gla_head(h, qk_ref, qk_ref, rows, bc, btot, k) for h in heads]
            vs = [v_ref[rows, h * DV:(h + 1) * DV] for h in heads]
            sst = [s_scr[h] for h in heads]
            sbs = [s.astype(BF) for s in sst]
            inter = [_nt(f[h]["qd"].astype(BF), sbs[h]) for h in heads]
            intra = [_nn(f[h]["att"].astype(BF), vs[h]) for h in heads]
            kv = [_tn(vs[h], f[h]["ke"].astype(BF)) for h in heads]
            for h in heads:
                gr = gr_ref[rows, h * DV:(h + 1) * DV].astype(F32)
                y_ref[rows, h * DV:(h + 1) * DV] = _headnorm_fwd(intra[h] + inter[h], hg_ref[:, h * DV:(h + 1) * DV],
                                                                   gr * _sigmoid(gr)).astype(BF)
                ss_ref[c, h] = sbs[h]
                s_scr[h] = sst[h] * f[h]["decay"] + kv[h]
            return carry

        lax.fori_loop(0, CPB, chunk, 0, unroll=2)

    return pl.pallas_call(
        body, grid=(nb,),
        in_specs=[_row_spec(D, CB_GQK), _row_spec(D, CB_GV), _row_spec(D, CB_GR), _row_spec(N_SMALL),
                  _full_spec((N_SMALL, NH * DQK)), _full_spec((1, NH * DQK)), _full_spec((1, D))],
        out_specs=[_row_spec(D), pl.BlockSpec((CPB, NH, DV, DQK), lambda i: (i, 0, 0, 0))],
        out_shape=(jax.ShapeDtypeStruct((tp, D), BF), jax.ShapeDtypeStruct((tp // CHUNK, NH, DV, DQK), BF)),
        scratch_shapes=[pltpu.VMEM((NH, DV, DQK), F32), pltpu.VMEM((TM, NH * DQK), F32)],
        compiler_params=_params(), name=name)(pbig, pbig, pbig, small, a2p, a2b, headg)


def _gla_bwd(dy, pbig, small, a2p, a2b, headg, ss, dsm_m, dproj, name):
    tp = pbig.shape[0]
    nb = tp // TM
    nqk = NH * DQK

    def body(dy_ref, qk_ref, v_ref, gr_ref, sm_ref, a2_ref, a2b_ref, hg_ref, ss_ref, dsmm_ref, _,
             dproj_ref, dsm_ref, da2_ref, da2b_ref, dhg_ref, ds_scr, lg_scr, dza_scr):
        step = pl.program_id(0)
        blk = nb - 1 - step

        @pl.when(step == 0)
        def _():
            ds_scr[...] = jnp.zeros_like(ds_scr)
            da2_ref[...] = jnp.zeros_like(da2_ref)
            da2b_ref[...] = jnp.zeros_like(da2b_ref)
            dhg_ref[...] = jnp.zeros_like(dhg_ref)

        k = _chunk_consts()
        rowio = k["rowio"]
        za, loga = _gla_loga(sm_ref, a2_ref, a2b_ref, blk)
        lg_scr[...] = loga

        def chunk(cc, carry):
            c = CPB - 1 - cc
            r0 = pl.multiple_of(c * CHUNK, CHUNK)
            rows = pl.ds(r0, CHUNK)
            bc = _nn(k["tril_f"], lg_scr[rows, :], precision=HI)
            btot = _last_row(bc, rowio)
            heads = range(NH)
            f = [_gla_head(h, qk_ref, qk_ref, rows, bc, btot, k) for h in heads]
            vs = [v_ref[rows, h * DV:(h + 1) * DV] for h in heads]
            sbs = [ss_ref[c, h] for h in heads]
            qdb = [f[h]["qd"].astype(BF) for h in heads]
            attb = [f[h]["att"].astype(BF) for h in heads]
            inter = [_nt(qdb[h], sbs[h]) for h in heads]
            intra = [_nn(attb[h], vs[h]) for h in heads]
            dsn = [ds_scr[h] for h in heads]
            dsb = [d.astype(BF) for d in dsn]
            dke = [_nn(vs[h], dsb[h]) for h in heads]
            dv1 = [_nt(f[h]["ke"].astype(BF), dsb[h]) for h in heads]
            t = []
            for h in heads:
                gr = gr_ref[rows, h * DV:(h + 1) * DV].astype(F32)
                sg = _sigmoid(gr)
                gain = hg_ref[:, h * DV:(h + 1) * DV]
                do, dact, dgain = _headnorm_bwd(dy_ref[rows, h * DV:(h + 1) * DV].astype(F32), intra[h] + inter[h], gain, gr * sg)
                dproj_ref[rows, 2 * D + h * DV:2 * D + (h + 1) * DV] = (dact * sg * (1.0 + gr * (1.0 - sg))).astype(BF)
                dhg_ref[:, h * DV:(h + 1) * DV] += dgain
                dob = do.astype(BF)
                t.append(dict(dob=dob, datt=_nt(dob, vs[h]), dv0=_tn(attb[h], dob), dq1=_nn(dob, sbs[h]), dsq=_tn(dob, qdb[h])))
            for h in heads:
                datt = jnp.where(k["tri"], t[h]["datt"], 0.0).astype(BF)
                t[h].update(dq0=_nn(datt, f[h]["ki"].astype(BF)), dki=_tn(datt, qdb[h]))
            dbc_parts = []
            for h in heads:
                dqd = t[h]["dq0"] + t[h]["dq1"]
                dki = t[h]["dki"]
                dproj_ref[rows, D + h * DV:D + (h + 1) * DV] = (t[h]["dv0"] + dv1[h]).astype(BF)
                dproj_ref[rows, h * DQK:(h + 1) * DQK] = (dqd * f[h]["e_pos"]).astype(BF)
                dproj_ref[rows, nqk + h * DQK:nqk + (h + 1) * DQK] = (dki * f[h]["e_neg"] + dke[h] * f[h]["e_end"]).astype(BF)
                dke_ke = dke[h] * f[h]["ke"]
                dbtot = (jnp.sum(dke_ke, axis=0, keepdims=True)
                         + jnp.sum(dsn[h] * sbs[h].astype(F32), axis=0, keepdims=True) * f[h]["decay"])
                dbc_parts.append(dqd * f[h]["qd"] - dki * f[h]["ki"] - dke_ke + jnp.where(rowio == CHUNK - 1, dbtot, 0.0))
                ds_scr[h] = dsn[h] * f[h]["decay"] + t[h]["dsq"]
            dbc = jnp.concatenate(dbc_parts, axis=1)
            dza_scr[rows, :] = _nn(k["triu_f"], dbc, precision=HI)
            return carry

        lax.fori_loop(0, CPB, chunk, 0, unroll=2)
        row = blk * TM + lax.broadcasted_iota(jnp.int32, (TM, 1), 0)
        dza = jnp.where(row >= FIRST_VALID, dza_scr[...] * (_sigmoid(-za) / G_TAU), 0.0)
        dzb = dza.astype(BF)
        dsm_ref[...] = (_nt(dzb, a2_ref[...]) + dsmm_ref[...]).astype(BF)
        da2_ref[...] += _tn(sm_ref[...].astype(BF), dzb)
        da2b_ref[...] += jnp.sum(dza, axis=0, keepdims=True)

    rspec = lambda width, col=0: pl.BlockSpec((TM, width), lambda i: (nb - 1 - i, col))
    return pl.pallas_call(
        body, grid=(nb,),
        in_specs=[rspec(D), rspec(D, CB_GQK), rspec(D, CB_GV), rspec(D, CB_GR), rspec(N_SMALL),
                  _full_spec((N_SMALL, nqk)), _full_spec((1, nqk)), _full_spec((1, D)),
                  pl.BlockSpec((CPB, NH, DV, DQK), lambda i: (nb - 1 - i, 0, 0, 0)), rspec(N_SMALL), ANY],
        out_specs=[rspec(3 * D, 0), rspec(N_SMALL), _full_spec((N_SMALL, nqk)), _full_spec((1, nqk)), _full_spec((1, D))],
        out_shape=(jax.ShapeDtypeStruct(dproj.shape, BF),
                   jax.ShapeDtypeStruct((tp, N_SMALL), BF), jax.ShapeDtypeStruct((N_SMALL, nqk), F32),
                   jax.ShapeDtypeStruct((1, nqk), F32), jax.ShapeDtypeStruct((1, D), F32)),
        scratch_shapes=[pltpu.VMEM((NH, DV, DQK), F32), pltpu.VMEM((TM, nqk), F32), pltpu.VMEM((TM, nqk), F32)],
        input_output_aliases={10: 0}, compiler_params=_params(), name=name)(dy, pbig, pbig, pbig, small, a2p, a2b, headg, ss, dsm_m, dproj)


PIECE_BYTES = 1 << 20
MAX_PIECES = 32


def _place():
    return lax.axis_index("x"), lax.axis_index("y"), lax.axis_index("c")


def _piece_rows(rows, row_bytes, align):
    want = min(MAX_PIECES, max(1, -(-rows * row_bytes // PIECE_BYTES)))
    best = rows
    for k in range(1, want + 1):
        if rows % k == 0 and (rows // k) % align == 0:
            best = rows // k
    return best


def _remote(src, dst, send_sems, recv_sems, k, to):
    return pltpu.make_async_remote_copy(src_ref=src, dst_ref=dst, send_sem=send_sems.at[k], recv_sem=recv_sems.at[k],
                                        device_id=to, device_id_type=MESH)


def _all_gather_chips(p, name):
    rd = _gather_rider(p)

    def body(*refs):
        start, middle, finish = rd["make"](refs[:1], refs[1:2], refs[2:])
        start()
        middle()
        finish()

    return pl.pallas_call(body, in_specs=[ANY], out_specs=[ANY], out_shape=rd["out_shapes"], scratch_shapes=rd["sems"],
                          name=name)(p)[0]


def _gather_rider(p):
    r, n = p.shape
    rh = r // 2
    align = 32 // p.dtype.itemsize
    assert r % (2 * align) == 0
    cr = _piece_rows(rh, n * p.dtype.itemsize, align)

    def make(in_refs, out_refs, sem_refs):
        p_ref, o_ref = in_refs[0], out_refs[0]
        send_sems, recv_sems = sem_refs
        x, y, c = _place()
        chips = [(1 - x, y), (x, 1 - y), (1 - x, 1 - y)]
        sib = (x, y, 1 - c)

        def half(hc, piece=None):
            if piece is None:
                return pl.ds(pl.multiple_of(hc * rh, align), rh)
            return pl.ds(pl.multiple_of(hc * rh + piece * cr, align), cr)

        first = [_remote(p_ref.at[half(c)], o_ref.at[j, half(c)], send_sems, recv_sems, j, (*chip, c))
                 for j, chip in enumerate(chips)]
        passed = [[_remote(o_ref.at[j, half(c, i)], o_ref.at[j, half(c, i)], send_sems, recv_sems, 3 + j, sib)
                   for i in range(rh // cr)] for j in range(3)]
        blocks = [_remote(o_ref.at[j, half(c)], o_ref.at[j, half(1 - c)], send_sems, recv_sems, 3 + j, sib) for j in range(3)]

        def start():
            for cp in first:
                cp.start()

        def middle():
            for j, cp in enumerate(first):
                cp.wait_recv()
                for piece in passed[j]:
                    piece.start()

        def finish():
            for block in blocks:
                block.wait_send()
                block.wait_recv()
            for cp in first:
                cp.wait_send()

        return start, middle, finish

    return dict(inputs=[p], out_shapes=(jax.ShapeDtypeStruct((3, r, n), p.dtype),),
                sems=[pltpu.SemaphoreType.DMA((6,)), pltpu.SemaphoreType.DMA((6,))], make=make)


def _scatter_rider(items):
    out_shapes, sems = _scatter_shapes(items)

    def make(in_refs, out_refs, sem_refs):
        sent = _scatter_copies(in_refs, out_refs, *sem_refs)

        def start():
            for cp in sent:
                cp.start()

        def finish():
            for cp in sent:
                cp.wait_recv()
            for cp in sent:
                cp.wait_send()

        return start, (lambda: None), finish

    return dict(inputs=list(items), out_shapes=out_shapes, sems=sems, make=make)


def _by_chip(mine, others):
    me = 2 * lax.axis_index("x") + lax.axis_index("y")
    by_mask = jnp.stack([mine, others[1], others[0], others[2]])
    return [lax.dynamic_index_in_dim(by_mask, q ^ me, 0, keepdims=False) for q in range(4)]


def _swap_halves(items, name):
    k = len(items)

    def body(*refs):
        a_refs, got_refs = refs[:k], refs[k:2 * k]
        send_sems, recv_sems = refs[2 * k:]
        x, y, c = _place()
        sib = (x, y, 1 - c)
        for i, a in enumerate(items):
            _, r, n = a.shape
            rh = r // 2
            cr = _piece_rows(rh, n * a.dtype.itemsize, 8)
            for q in range(4):
                for t in range(rh // cr):
                    other = pl.ds(pl.multiple_of((1 - c) * rh + t * cr, 8), cr)
                    _remote(a_refs[i].at[q, other], got_refs[i].at[q, pl.ds(t * cr, cr)], send_sems, recv_sems, i, sib).start()
        for i, a in enumerate(items):
            block = _remote(a_refs[i].at[:, pl.ds(0, a.shape[1] // 2)], got_refs[i], send_sems, recv_sems, i, sib)
            block.wait_send()
            block.wait_recv()

    return pl.pallas_call(
        body, in_specs=[ANY] * k, out_specs=[ANY] * k,
        out_shape=tuple(jax.ShapeDtypeStruct((4, a.shape[1] // 2, a.shape[2]), a.dtype) for a in items),
        scratch_shapes=[pltpu.SemaphoreType.DMA((k,)), pltpu.SemaphoreType.DMA((k,))], name=name)(*items)


def _scatter_copies(s_refs, o_refs, send_sems, recv_sems):
    x, y, c = _place()
    chips = [(1 - x, y), (x, 1 - y), (1 - x, 1 - y)]
    return [_remote(s_refs[i].at[2 * cx + cy], o_refs[i].at[j], send_sems, recv_sems, 3 * i + j, (cx, cy, c))
            for i in range(len(s_refs)) for j, (cx, cy) in enumerate(chips)]


def _scatter_shapes(items):
    k = len(items)
    return (tuple(jax.ShapeDtypeStruct((3,) + s.shape[1:], s.dtype) for s in items),
            [pltpu.SemaphoreType.DMA((3 * k,)), pltpu.SemaphoreType.DMA((3 * k,))])


def _scatter_chips(items, name):
    k = len(items)

    def body(*refs):
        sent = _scatter_copies(refs[:k], refs[k:2 * k], *refs[2 * k:])
        for cp in sent:
            cp.start()
        for cp in sent:
            cp.wait_recv()
        for cp in sent:
            cp.wait_send()

    out_shape, scratch = _scatter_shapes(items)
    return pl.pallas_call(body, in_specs=[ANY] * k, out_specs=[ANY] * k, out_shape=out_shape, scratch_shapes=scratch,
                          name=name)(*items)


def _join_halves(items, name):
    k = len(items)

    def body(*refs):
        f_refs, o_refs = refs[:k], refs[k:2 * k]
        send_sems, recv_sems = refs[2 * k:]
        x, y, c = _place()
        sib = (x, y, 1 - c)
        for i, f in enumerate(items):
            rh, n = f.shape
            cr = _piece_rows(rh, n * f.dtype.itemsize, 8)
            for t in range(rh // cr):
                rows = pl.ds(t * cr, cr)
                _remote(f_refs[i].at[rows], o_refs[i].at[rows], send_sems, recv_sems, i, sib).start()
        for i in range(k):
            block = _remote(f_refs[i], o_refs[i], send_sems, recv_sems, i, sib)
            block.wait_send()
            block.wait_recv()

    return pl.pallas_call(
        body, in_specs=[ANY] * k, out_specs=[ANY] * k, out_shape=tuple(jax.ShapeDtypeStruct(f.shape, f.dtype) for f in items),
        scratch_shapes=[pltpu.SemaphoreType.DMA((k,)), pltpu.SemaphoreType.DMA((k,))], name=name)(*items)


SMALL_ROWS = 16
SMALL_SHARD_SHAPES = [(N_META, 256), (4, 256), (G_RANK, 128), (NH, 64), (NH, 64)]
REPL_SHAPES = [(1, D), (1, D), (1, 2, NH), (1, NH * DQK), (1, D), (D,)]
W_IN_SHARD = 2054


def _pack_small(parts):
    flat = jnp.concatenate([p.reshape(-1) for p in parts])
    return jnp.pad(flat, (0, SMALL_ROWS * D - flat.shape[0])).reshape(SMALL_ROWS, D)


def _unpack_small(block, shapes):
    flat, out, off = block.reshape(-1), [], 0
    for shp in shapes:
        n = 1
        for s in shp:
            n *= s
        out.append(flat[off:off + n].reshape(shp))
        off += n
    return out


def _proj_cols_from_w_in(w_in_f):
    w_big = jnp.concatenate([w_in_f[:, 3080:5128], w_in_f[:, 5144:6168], w_in_f[:, 0:1024], w_in_f[:, 6168:8216],
                             w_in_f[:, 1024:2048], w_in_f[:, 2056:3080]], axis=1)
    w_small = jnp.concatenate([w_in_f[:, 2048:2056], w_in_f[:, 5128:5144], jnp.zeros((D, N_SMALL - 24), w_in_f.dtype)], axis=1)
    return w_big, w_small


def _w_in_from_proj_cols(d_wall):
    big, small = d_wall[:, 0:N_BIG], d_wall[:, N_BIG:N_ALL]
    return jnp.concatenate([big[:, 3072:4096], big[:, 6144:7168], small[:, 0:8], big[:, 7168:8192], big[:, 0:2048],
                            small[:, 8:24], big[:, 2048:3072], big[:, 4096:6144]], axis=1)


def kernel(x, meta_tokens, norm1_g, w_in, conv_w, conv_b, m_gate_b, g_a2, g_a2_b, m_head_g, g_head_g, w_branch_m, w_branch_g, w_out, norm2_g, w_ff_gate, w_ff_up, w_ff_down, final_g, loss_target, m_meta_tokens, m_norm1_g, m_w_in, m_conv_w, m_conv_b, m_m_gate_b, m_g_a2, m_g_a2_b, m_m_head_g, m_g_head_g, m_w_branch_m, m_w_branch_g, m_w_out, m_norm2_g, m_w_ff_gate, m_w_ff_up, m_w_ff_down, m_final_g, v_meta_tokens, v_norm1_g, v_w_in, v_conv_w, v_conv_b, v_m_gate_b, v_g_a2, v_g_a2_b, v_m_head_g, v_g_head_g, v_w_branch_m, v_w_branch_g, v_w_out, v_norm2_g, v_w_ff_gate, v_w_ff_up, v_w_ff_down, v_final_g):
    w = _gather_weights(w_in, w_branch_m, w_branch_g, w_out, w_ff_gate, w_ff_up, w_ff_down, meta_tokens, conv_w, g_a2, m_head_g, g_head_g)
    loss_local, dx, grads = _local_step(x[0], loss_target[0], w, norm1_g, conv_b, m_gate_b, g_a2_b, norm2_g, final_g, _Reducer())

    weights = [w_in, w_branch_m, w_branch_g, w_out, w_ff_gate, w_ff_up, w_ff_down, meta_tokens, conv_w, g_a2, m_head_g, g_head_g,
               norm1_g, conv_b, m_gate_b, g_a2_b, norm2_g, final_g]
    moms = [m_w_in, m_w_branch_m, m_w_branch_g, m_w_out, m_w_ff_gate, m_w_ff_up, m_w_ff_down, m_meta_tokens, m_conv_w, m_g_a2,
            m_m_head_g, m_g_head_g, m_norm1_g, m_conv_b, m_m_gate_b, m_g_a2_b, m_norm2_g, m_final_g]
    vels = [v_w_in, v_w_branch_m, v_w_branch_g, v_w_out, v_w_ff_gate, v_w_ff_up, v_w_ff_down, v_meta_tokens, v_conv_w, v_g_a2,
            v_m_head_g, v_g_head_g, v_norm1_g, v_conv_b, v_m_gate_b, v_g_a2_b, v_norm2_g, v_final_g]
    res = {}
    for nm, wt, g, m, v in zip(PACK_ORDER, weights, grads, moms, vels):
        two_d = (wt.size // wt.shape[-1], wt.shape[-1])
        d, nm_, nv_ = _adamw(wt.reshape(two_d), g.reshape(two_d), m.reshape(two_d), v.reshape(two_d), "adamw_" + nm)
        res[nm] = (g.reshape(wt.shape), d.reshape(wt.shape), nm_.reshape(wt.shape), nv_.reshape(wt.shape))

    order = ["meta_tokens", "norm1_g", "w_in", "conv_w", "conv_b", "m_gate_b", "g_a2", "g_a2_b", "m_head_g", "g_head_g",
             "w_branch_m", "w_branch_g", "w_out", "norm2_g", "w_ff_gate", "w_ff_up", "w_ff_down", "final_g"]
    loss = lax.psum(loss_local[0, 0], ("x", "y", "c"))
    grad_x = dx.reshape(x.shape)
    return (loss, grad_x, *[res[n][0] for n in order], *[res[n][1] for n in order],
            *[res[n][2] for n in order], *[res[n][3] for n in order])


PACK_ORDER = ["w_in", "w_branch_m", "w_branch_g", "w_out", "w_ff_gate", "w_ff_up", "w_ff_down", "meta_tokens", "conv_w", "g_a2",
              "m_head_g", "g_head_g", "norm1_g", "conv_b", "m_gate_b", "g_a2_b", "norm2_g", "final_g"]


def _gather_weights(w_in, w_branch_m, w_branch_g, w_out, w_ff_gate, w_ff_up, w_ff_down, meta_tokens, conv_w, g_a2, m_head_g, g_head_g):
    bf = lambda a: a.astype(BF)
    rows_local = jnp.concatenate([bf(w_branch_m[0]), bf(w_branch_g[0]), bf(w_out[0]), bf(w_ff_down[0]),
                                  bf(w_ff_gate[0].T), bf(w_ff_up[0].T)], axis=0)
    win_local = bf(w_in[0])
    small_local = _pack_small([meta_tokens, conv_w[0], g_a2[0], m_head_g[0], g_head_g[0]])
    win_all = _by_chip(win_local, _all_gather_chips(win_local, "gather_w_in"))
    small_all = _by_chip(small_local, _all_gather_chips(small_local, "gather_small"))
    w_in_f = jnp.concatenate([win_all[q] for q in range(4)], axis=1)
    small_sh = [_unpack_small(small_all[q], SMALL_SHARD_SHAPES) for q in range(4)]
    cat = lambda i: jnp.concatenate([s[i] for s in small_sh], axis=-1)
    return dict(w_in=w_in_f, rows_local=rows_local, meta=cat(0), convw=cat(1), ga2=cat(2),
                mhg=cat(3).reshape(1, D), ghg=cat(4).reshape(1, D))


def _row_weights(rows_local, gathered):
    rows_all = jnp.stack(_by_chip(rows_local, gathered))
    cut = lambda lo, hi: rows_all[:, lo:hi].reshape(4 * (hi - lo), D)
    return cut(0, 256), cut(256, 512), cut(512, 768), cut(768, 1472), _ffn_weight_rows(cut(1472, 2176), cut(2176, 2880))


def _local_step(x0, target, w, norm1_g, conv_b, m_gate_b, g_a2_b, norm2_g, final_g, reducer):
    w_in_f, meta_f, convw_f, ga2_f, mhg_f, ghg_f = w["w_in"], w["meta"], w["convw"], w["ga2"], w["mhg"], w["ghg"]
    w_big, w_small = _proj_cols_from_w_in(w_in_f)
    w_all = jnp.concatenate([w_big, w_small], axis=1)
    gbias = jnp.concatenate([m_gate_b.reshape(1, 2 * NH), jnp.zeros((1, N_SMALL - 2 * NH), F32)], axis=1)
    a2p = jnp.concatenate([jnp.zeros((8, NH * DQK), F32), ga2_f, jnp.zeros((N_SMALL - 24, NH * DQK), F32)], axis=0).astype(BF)
    convb = conv_b.reshape(1, D)
    g1 = norm1_g.reshape(1, D)
    g2 = norm2_g.reshape(1, D)
    gf = final_g.reshape(1, D)
    h0 = jnp.concatenate([jnp.zeros((FIRST_VALID, D), F32), meta_f, x0], axis=0)

    xn1, rstd1 = _rms_fwd(h0, g1, "rms1")
    pbig, rows_gathered = _mm(xn1, w_big, nt=False, out_dtype=BF, tn=1024, name="proj_big", rider=_gather_rider(w["rows_local"]))
    wbm, wbg, wout, wdown, wgu_t = _row_weights(w["rows_local"], rows_gathered[0])
    small = _mm(xn1, w_small, nt=False, out_dtype=F32, tn=N_SMALL, name="proj_small")
    qk = _conv_fwd(pbig, convw_f, convb, "conv_fwd")
    y_m, m_cs, m_ns = _mlstm_fwd(qk, pbig, small, gbias, mhg_f, "mlstm_fwd")
    y_g, g_ss = _gla_fwd(pbig, small, a2p, g_a2_b, ghg_f, "gla_fwd")
    p_m, p_g, merged = _branch_merge(y_m, y_g, wbm, wbg, pbig, "branch_merge")
    h1, hn, rstd2 = _out_proj_norm(merged, wout, h0, g2, "out_proj")
    gu, ff = _ffn_in(hn, wgu_t, "ff_in")
    dh2, loss_local, d_final_g = _ffn_down_loss(ff, wdown, h1, target, gf, "ff_down_loss")

    d_wdown = _mm_tn(ff, dh2, tm=1408, tn=1024, name="dw_ff_down")
    dgu = _ffn_d_hidden(dh2, wdown, gu, "d_ff")
    d_wgu_t = _mm_tn(dgu, hn, tm=1408, tn=1024, name="dw_ff_in")
    dh1, d_g2 = _ffn_d_in(dgu, wgu_t, h1, rstd2, g2, dh2, "d_hn")
    d_wout = _mm_tn(merged, dh1, tm=1024, tn=1024, name="dw_out")
    dp_m, dp_g, dproj = _merge_d(dh1, wout, p_m, p_g, pbig, "d_merged")
    dy_m = _mm(dp_m, wbm, nt=True, out_dtype=BF, tn=1024, name="d_ym")
    dy_g = _mm(dp_g, wbg, nt=True, out_dtype=BF, tn=1024, name="d_yg")
    d_wbm = _mm_tn(y_m, dp_m, tm=1024, tn=1024, name="dw_branch_m")
    d_wbg = _mm_tn(y_g, dp_g, tm=1024, tn=1024, name="dw_branch_g")
    fq = D_FF // 4
    gu4 = jnp.transpose(d_wgu_t.reshape(2, 2, 2, fq, D), (0, 2, 1, 3, 4)).reshape(4, 2 * fq, D)
    sq4 = jnp.concatenate([d_wbm.reshape(4, 256, D), d_wbg.reshape(4, 256, D), d_wout.reshape(4, 256, D)], axis=1)
    sums_a = reducer.partial_sums([sq4, d_wdown.reshape(4, fq, D), gu4], BF, "a")
    dqk_m, dproj, dsm_m, d_gbias, d_mhg, recv_a = _mlstm_bwd(dy_m, qk, pbig, small, gbias, mhg_f, m_cs, m_ns, dproj,
                                                              "mlstm_bwd", ride=sums_a)
    dconv, d_convwb = _conv_bwd_pre(dqk_m, pbig, convw_f, convb, "conv_bwd_pre")
    dproj = _conv_bwd_in(dconv, convw_f, dproj, "conv_bwd_in")
    dproj, dsmall, d_a2p, d_a2b, d_ghg = _gla_bwd(dy_g, pbig, small, a2p, g_a2_b, ghg_f, g_ss, dsm_m, dproj, "gla_bwd")
    dproj = _place_small(dsmall, dproj, "dproj_small")
    d_win = _w_in_from_proj_cols(_mm_tn(xn1, dproj, tm=1024, tn=1664, name="dw_in"))
    win4 = jnp.stack([d_win[:, q * W_IN_SHARD:(q + 1) * W_IN_SHARD] for q in range(4)])
    sums_b = reducer.partial_sums([win4], BF, "b")
    dxn, recv_b = _mm(dproj, w_all, nt=True, out_dtype=F32, tn=1024, tk=1664, name="d_xn", rider=_scatter_rider(sums_b))
    dh_first, dx, d_g1 = _rms_bwd(dxn, h0, rstd1, g1, dh1, "rms1_bwd", split_first=True)

    small_sharded = [dh_first[FIRST_VALID:TM], d_convwb[0:4], d_a2p[8:24], d_mhg.reshape(NH, DV), d_ghg.reshape(NH, DV)]
    replicated = [d_g1, d_convwb[4:5], d_gbias[0:1, 0:2 * NH].reshape(1, 2, NH), d_a2b, d_g2, d_final_g.reshape(D)]
    small4 = jnp.stack([_pack_small([g[:, q * shp[1]:(q + 1) * shp[1]] for g, shp in zip(small_sharded, SMALL_SHARD_SHAPES)]
                                    + replicated) for q in range(4)])
    sums_c = reducer.partial_sums([small4], F32, "c")
    recv_c = reducer.scatter(sums_c, "c")
    sq, down, gu, win, smalls = reducer.finish(sums_a + sums_b + sums_c, recv_a + recv_b + recv_c, in_chip_order=[4])
    grads = ([win, sq[0:256], sq[256:512], sq[512:768], gu[0:fq].T, gu[fq:2 * fq].T, down]
             + _unpack_small(smalls, SMALL_SHARD_SHAPES + REPL_SHAPES))
    return loss_local, dx, grads


class _Reducer:
    def partial_sums(self, items, dtype, tag):
        c = lax.axis_index("c")
        got = _swap_halves(items, "reduce_siblings_" + tag)
        sums = []
        for i, (a, g) in enumerate(zip(items, got)):
            rh, n = g.shape[1], g.shape[2]
            own = lax.dynamic_slice_in_dim(a, c * rh, rh, axis=1)
            sums.append(_add2(own.reshape(-1, n), g.reshape(-1, n), dtype, f"reduce_add2_{tag}{i}").reshape(g.shape))
        return sums

    def scatter(self, sums, tag):
        return list(_scatter_chips(sums, "reduce_chips_" + tag))

    def finish(self, sums, from_chips, in_chip_order):
        c = lax.axis_index("c")
        me = 2 * lax.axis_index("x") + lax.axis_index("y")
        halves = []
        for i, (s, f) in enumerate(zip(sums, from_chips)):
            mine = lax.dynamic_index_in_dim(s, me, 0, keepdims=False)
            if i in in_chip_order:
                by_chip = _by_chip(mine, f)
                mine, f = by_chip[0], jnp.stack(by_chip[1:])
            halves.append(_add4(mine, f, f"reduce_add4_{i}"))
        got = _join_halves(halves, "reduce_join")
        return [jnp.where(c == 0, jnp.concatenate([h, g], axis=0), jnp.concatenate([g, h], axis=0)) for h, g in zip(halves, got)]
```

```python
import functools

import jax
import jax.numpy as jnp
from jax import lax
from jax.experimental import pallas as pl
from jax.experimental.pallas import tpu as pltpu

F32 = jnp.float32
BF = jnp.bfloat16
HI = lax.Precision.HIGHEST
MESH = pl.DeviceIdType.MESH

D = 1024
N_META = 16
CHUNK = 128
EPS = 1e-6
NH = 4
DV = 256
DQK = 128
G_RANK = 16
G_TAU = 16.0
D_FF = 2816
TM = 512
FIRST_VALID = TM - N_META
CPB = TM // CHUNK
G_CHUNK = 256
G_CPB = TM // G_CHUNK
NEG = -1e30
N_BIG = 8192
CB_GQK, CB_GV, CB_GR, CB_MQK, CB_GM, CB_GG, CB_MV, CB_MO = range(8)
N_SMALL = 128
N_ALL = N_BIG + N_SMALL
VMEM_LIMIT = 56 * 1024 * 1024

ADAM_LR, ADAM_B1, ADAM_B2, ADAM_EPS, ADAM_WD, ADAM_STEP = 0.001, 0.9, 0.999, 1e-08, 0.01, 10

NT_DIMS = (((1,), (1,)), ((), ()))
TN_DIMS = (((0,), (0,)), ((), ()))


def _nt(a, b, **kw):
    return lax.dot_general(a, b, NT_DIMS, preferred_element_type=F32, **kw)


def _tn(a, b, **kw):
    return lax.dot_general(a, b, TN_DIMS, preferred_element_type=F32, **kw)


def _nn(a, b, **kw):
    return jnp.dot(a, b, preferred_element_type=F32, **kw)


def _params(**kw):
    return pltpu.CompilerParams(vmem_limit_bytes=VMEM_LIMIT, **kw)


def _sigmoid(x):
    return 0.5 * jnp.tanh(0.5 * x) + 0.5


def _logsig(x):
    return jnp.minimum(x, 0.0) - jnp.log(1.0 + jnp.exp(-jnp.abs(x)))


def _mm_rows(rows):
    return 3 * TM if rows % (3 * TM) == 0 else TM


def _mm(a, b, *, nt, out_dtype, tn, tk=None, tm=None, name, rider=None):
    m, k = a.shape
    n = b.shape[0] if nt else b.shape[1]
    tk = k if tk is None else tk
    tm = _mm_rows(m) if tm is None else tm
    nk = k // tk
    nj, ni = n // tn, m // tm
    nr_in = len(rider["inputs"]) if rider else 0
    nr_out = len(rider["out_shapes"]) if rider else 0
    assert m % tm == 0 and n % tn == 0 and k % tk == 0
    dims = NT_DIMS if nt else (((1,), (0,)), ((), ()))

    def body(*refs):
        a_ref, b_ref = refs[:2]
        o_ref = refs[2 + nr_in]
        j, i, kk = pl.program_id(0), pl.program_id(1), pl.program_id(2)
        step = (j * ni + i) * nk + kk
        if rider:
            start, middle, finish = rider["make"](refs[2:2 + nr_in], refs[3 + nr_in:3 + nr_in + nr_out],
                                                  refs[3 + nr_in + nr_out:5 + nr_in + nr_out])
            pl.when(step == 0)(start)
            pl.when(step == (nj * ni * nk) // 2)(middle)

        part = lax.dot_general(a_ref[...].astype(BF), b_ref[...].astype(BF), dims, preferred_element_type=F32)
        if nk == 1:
            o_ref[...] = part.astype(o_ref.dtype)
        else:
            acc_ref = refs[-1]

            @pl.when(kk == 0)
            def _():
                acc_ref[...] = part

            @pl.when(jnp.logical_and(kk > 0, kk < nk - 1))
            def _():
                acc_ref[...] += part

            @pl.when(kk == nk - 1)
            def _():
                o_ref[...] = (acc_ref[...] + part).astype(o_ref.dtype)

        if rider:
            pl.when(step == nj * ni * nk - 1)(finish)

    outs = pl.pallas_call(
        body, grid=(nj, ni, nk),
        in_specs=[pl.BlockSpec((tm, tk), lambda j, i, kk: (i, kk)),
                  pl.BlockSpec((tn, tk), lambda j, i, kk: (j, kk)) if nt else pl.BlockSpec((tk, tn), lambda j, i, kk: (kk, j))]
                 + [ANY] * nr_in,
        out_specs=[pl.BlockSpec((tm, tn), lambda j, i, kk: (i, j))] + [ANY] * nr_out,
        out_shape=(jax.ShapeDtypeStruct((m, n), out_dtype),) + (tuple(rider["out_shapes"]) if rider else ()),
        scratch_shapes=(rider["sems"] if rider else []) + ([pltpu.VMEM((tm, tn), F32)] if nk > 1 else []),
        compiler_params=_params(), name=name)(a, b, *(rider["inputs"] if rider else []))
    return (outs[0], list(outs[1:])) if rider else outs[0]


def _mm_tn(a, b, *, tm, tn, tk=None, name):
    t, m = a.shape
    n = b.shape[1]
    tk = _mm_rows(t) if tk is None else tk
    assert t % tk == 0 and m % tm == 0 and n % tn == 0

    def body(a_ref, b_ref, o_ref):
        part = _tn(a_ref[...].astype(BF), b_ref[...].astype(BF))

        @pl.when(pl.program_id(2) == 0)
        def _():
            o_ref[...] = part

        @pl.when(pl.program_id(2) > 0)
        def _():
            o_ref[...] += part

    return pl.pallas_call(
        body, grid=(m // tm, n // tn, t // tk),
        in_specs=[pl.BlockSpec((tk, tm), lambda i, j, kk: (kk, i)), pl.BlockSpec((tk, tn), lambda i, j, kk: (kk, j))],
        out_specs=pl.BlockSpec((tm, tn), lambda i, j, kk: (i, j)),
        out_shape=jax.ShapeDtypeStruct((m, n), F32), compiler_params=_params(), name=name)(a, b)


ANY = pl.BlockSpec(memory_space=pl.ANY)


def _row_spec(width, col=0):
    return pl.BlockSpec((TM, width), lambda i: (i, col))


def _full_spec(shape):
    return pl.BlockSpec(shape, lambda i: (0,) * len(shape))


def _rms_fwd(h, g, name):
    tp = h.shape[0]

    def body(h_ref, g_ref, xn_ref, r_ref):
        x = h_ref[...]
        r = lax.rsqrt(jnp.mean(x * x, axis=1, keepdims=True) + EPS)
        xn_ref[...] = (x * r * g_ref[...]).astype(BF)
        r_ref[...] = r

    return pl.pallas_call(
        body, grid=(tp // TM,), in_specs=[_row_spec(D), _full_spec((1, D))],
        out_specs=[_row_spec(D), _row_spec(1)],
        out_shape=(jax.ShapeDtypeStruct((tp, D), BF), jax.ShapeDtypeStruct((tp, 1), F32)),
        compiler_params=_params(), name=name)(h, g)


def _rms_bwd(dxn, h, rstd, g, dres, name, split_first=False):
    tp = h.shape[0]

    def body(dxn_ref, h_ref, r_ref, g_ref, dres_ref, *outs):
        r = r_ref[...]
        xh = h_ref[...] * r
        dxn_v = dxn_ref[...].astype(F32)
        dxh = dxn_v * g_ref[...]
        dh = r * (dxh - xh * jnp.mean(dxh * xh, axis=1, keepdims=True)) + dres_ref[...]
        if split_first:
            first_ref, dh_ref, dg_ref = outs

            @pl.when(pl.program_id(0) == 0)
            def _():
                first_ref[...] = dh
        else:
            dh_ref, dg_ref = outs
        dh_ref[...] = dh
        part = jnp.sum(dxn_v * xh, axis=0, keepdims=True)

        @pl.when(pl.program_id(0) == 0)
        def _():
            dg_ref[...] = part

        @pl.when(pl.program_id(0) > 0)
        def _():
            dg_ref[...] += part

    if split_first:
        out_specs = [_full_spec((TM, D)), pl.BlockSpec((TM, D), lambda i: (jnp.maximum(i - 1, 0), 0)), _full_spec((1, D))]
        out_shape = (jax.ShapeDtypeStruct((TM, D), F32), jax.ShapeDtypeStruct((tp - TM, D), F32), jax.ShapeDtypeStruct((1, D), F32))
    else:
        out_specs = [_row_spec(D), _full_spec((1, D))]
        out_shape = (jax.ShapeDtypeStruct((tp, D), F32), jax.ShapeDtypeStruct((1, D), F32))
    return pl.pallas_call(
        body, grid=(tp // TM,),
        in_specs=[_row_spec(D), _row_spec(D), _row_spec(1), _full_spec((1, D)), _row_spec(D)],
        out_specs=out_specs, out_shape=out_shape, compiler_params=_params(), name=name)(dxn, h, rstd, g, dres)


def _shift_down(x, halo, k):
    rk = pltpu.roll(x, k, 0)
    io = lax.broadcasted_iota(jnp.int32, (8, x.shape[1]), 0)
    top = jnp.where(io < k, pltpu.roll(halo, k, 0), rk[0:8])
    return jnp.concatenate([top, rk[8:]], axis=0)


def _shift_up(x, nxt, k):
    n = x.shape[0]
    rk = pltpu.roll(x, n - k, 0)
    io = lax.broadcasted_iota(jnp.int32, (8, x.shape[1]), 0)
    bot = jnp.where(io >= 8 - k, pltpu.roll(nxt, 8 - k, 0), rk[n - 8:n])
    return jnp.concatenate([rk[:n - 8], bot], axis=0)


def _conv_pre(x, halo, w_ref, b_ref):
    c = x * w_ref[3:4, :] + b_ref[...]
    shifted = []
    for k in (1, 2, 3):
        s = _shift_down(x, halo, k)
        shifted.append(s)
        c = c + s * w_ref[3 - k:4 - k, :]
    return c, shifted


def _qk_scale():
    col = lax.broadcasted_iota(jnp.int32, (1, D), 1)
    return jnp.where(col < NH * DQK, DQK ** -0.5, 1.0).astype(F32)


def _halo_prev_spec():
    return pl.BlockSpec((8, D), lambda i: (jnp.maximum(i * (TM // 8) - 1, 0), CB_MQK))


def _conv_fwd(pbig, w, b, name):
    tp = pbig.shape[0]

    def body(x_ref, halo_ref, w_ref, b_ref, o_ref):
        x = x_ref[...].astype(F32)
        halo = jnp.where(pl.program_id(0) > 0, halo_ref[...].astype(F32), 0.0)
        c, _ = _conv_pre(x, halo, w_ref, b_ref)
        o_ref[...] = (c * _sigmoid(c) * _qk_scale()).astype(BF)

    return pl.pallas_call(
        body, grid=(tp // TM,),
        in_specs=[_row_spec(D, CB_MQK), _halo_prev_spec(), _full_spec((4, D)), _full_spec((1, D))],
        out_specs=_row_spec(D), out_shape=jax.ShapeDtypeStruct((tp, D), BF),
        compiler_params=_params(), name=name)(pbig, pbig, w, b)


def _conv_bwd_pre(dqk, pbig, w, b, name):
    tp = pbig.shape[0]

    def body(d_ref, x_ref, halo_ref, w_ref, b_ref, dc_ref, dwb_ref):
        x = x_ref[...].astype(F32)
        halo = jnp.where(pl.program_id(0) > 0, halo_ref[...].astype(F32), 0.0)
        c, shifted = _conv_pre(x, halo, w_ref, b_ref)
        sg = _sigmoid(c)
        dc = d_ref[...] * _qk_scale() * (sg * (1.0 + c * (1.0 - sg)))
        dc_ref[...] = dc
        taps = [shifted[2], shifted[1], shifted[0], x]
        rows = [jnp.sum(dc * t, axis=0, keepdims=True) for t in taps] + [jnp.sum(dc, axis=0, keepdims=True)]
        io = lax.broadcasted_iota(jnp.int32, (8, D), 0)
        part = jnp.zeros((8, D), F32)
        for r, v in enumerate(rows):
            part = jnp.where(io == r, v, part)

        @pl.when(pl.program_id(0) == 0)
        def _():
            dwb_ref[...] = part

        @pl.when(pl.program_id(0) > 0)
        def _():
            dwb_ref[...] += part

    return pl.pallas_call(
        body, grid=(tp // TM,),
        in_specs=[_row_spec(D), _row_spec(D, CB_MQK), _halo_prev_spec(), _full_spec((4, D)), _full_spec((1, D))],
        out_specs=[_row_spec(D), _full_spec((8, D))],
        out_shape=(jax.ShapeDtypeStruct((tp, D), F32), jax.ShapeDtypeStruct((8, D), F32)),
        compiler_params=_params(), name=name)(dqk, pbig, pbig, w, b)


def _conv_bwd_in(dc, w, dproj, name):
    tp = dc.shape[0]
    nb = tp // TM

    def body(d_ref, nxt_ref, w_ref, _, o_ref):
        d = d_ref[...]
        nxt = jnp.where(pl.program_id(0) < nb - 1, nxt_ref[...], 0.0)
        acc = d * w_ref[3:4, :]
        for k in (1, 2, 3):
            acc = acc + _shift_up(d, nxt, k) * w_ref[3 - k:4 - k, :]
        o_ref[...] = acc.astype(BF)

    return pl.pallas_call(
        body, grid=(nb,),
        in_specs=[_row_spec(D), pl.BlockSpec((8, D), lambda i: (jnp.minimum((i + 1) * (TM // 8), tp // 8 - 1), 0)),
                  _full_spec((4, D)), ANY],
        out_specs=_row_spec(D, CB_MQK), out_shape=jax.ShapeDtypeStruct(dproj.shape, BF),
        input_output_aliases={3: 0}, compiler_params=_params(), name=name)(dc, dc, w, dproj)


def _mm_fused(inputs, products, *, nt, m, n, tm, tn, outs, epilogue, name, nk=1, sub=None):
    dims = NT_DIMS if nt else (((1,), (0,)), ((), ()))
    nin = len(inputs)
    assert nk == 1 or (len(products) == 1 and sub is None)

    def body(*refs):
        in_refs, out_refs = refs[:nin], refs[nin:nin + len(outs)]
        i = pl.program_id(1)
        if sub is not None:
            lhs = {ia: in_refs[ia][...].astype(BF) for ia, _ in products}

            def dots(cols):
                return [lax.dot_general(lhs[ia], (in_refs[ib][cols, :] if nt else in_refs[ib][:, cols]).astype(BF),
                                        dims, preferred_element_type=F32) for ia, ib in products]

            slices = [slice(s, min(s + sub, tn)) for s in range(0, tn, sub)]
            prods = dots(slices[0])
            for idx, cols in enumerate(slices):
                nxt = dots(slices[idx + 1]) if idx + 1 < len(slices) else None
                epilogue(prods, in_refs, out_refs, i, cols)
                prods = nxt
            return
        prods = [lax.dot_general(in_refs[ia][...].astype(BF), in_refs[ib][...].astype(BF), dims, preferred_element_type=F32)
                 for ia, ib in products]
        if nk == 1:
            epilogue(prods, in_refs, out_refs, i, slice(None))
            return
        acc_ref = refs[-1]
        kk = pl.program_id(2)

        @pl.when(kk == 0)
        def _():
            acc_ref[...] = prods[0]

        @pl.when(jnp.logical_and(kk > 0, kk < nk - 1))
        def _():
            acc_ref[...] += prods[0]

        @pl.when(kk == nk - 1)
        def _():
            epilogue([acc_ref[...] + prods[0]], in_refs, out_refs, i, slice(None))

    return pl.pallas_call(
        body, grid=(n // tn, m // tm, nk), in_specs=[s for _, s in inputs], out_specs=[s for _, s in outs],
        out_shape=tuple(sh for sh, _ in outs), scratch_shapes=[pltpu.VMEM((tm, tn), F32)] if nk > 1 else [],
        compiler_params=_params(), name=name)(*[a for a, _ in inputs])


SUB_COLS = 256


def _cols_at(cols, offset):
    return slice(cols.start + offset, cols.stop + offset)


def _blk(rows, width, col=None, row=None):
    return pl.BlockSpec((rows, width), lambda j, i, kk: ((i if row is None else row(i)), (0 if col is None else col(j, kk))))


FF_TN = D_FF // 2


def _ffn_weight_rows(wg_t, wu_t):
    return jnp.concatenate([wg_t[0:FF_TN], wu_t[0:FF_TN], wg_t[FF_TN:], wu_t[FF_TN:]], axis=0)


def _ffn_in(hn, wgu_t, name):
    tp = hn.shape[0]
    tm = _mm_rows(tp)

    def epilogue(prods, in_refs, out_refs, i, cols):
        g, u = prods
        out_refs[0][:, cols] = g.astype(BF)
        out_refs[0][:, _cols_at(cols, FF_TN)] = u.astype(BF)
        out_refs[1][:, cols] = (g * _sigmoid(g) * u).astype(BF)

    wspec = lambda off: pl.BlockSpec((FF_TN, D), lambda j, i, kk: (2 * j + off, 0))
    return _mm_fused(
        [(hn, _blk(tm, D)), (wgu_t, wspec(0)), (wgu_t, wspec(1))], [(0, 1), (0, 2)], nt=True, m=tp, n=D_FF, tm=tm, tn=FF_TN,
        outs=[(jax.ShapeDtypeStruct((tp, 2 * D_FF), BF), _blk(tm, 2 * FF_TN, lambda j, kk: j)),
              (jax.ShapeDtypeStruct((tp, D_FF), BF), _blk(tm, FF_TN, lambda j, kk: j))],
        epilogue=epilogue, name=name, sub=SUB_COLS)


def _ffn_down_loss(ff, wdown, h1, target, gf, name):
    tp = ff.shape[0]

    def epilogue(prods, in_refs, out_refs, i, cols):
        live = (i > 0).astype(F32)
        g = in_refs[4][...]
        x = prods[0] + in_refs[2][...]
        r = lax.rsqrt(jnp.mean(x * x, axis=1, keepdims=True) + EPS)
        xh = x * r
        e = xh * g - in_refs[3][...]
        loss_part = 0.5 * live * jnp.sum(jnp.mean(e * e, axis=1, keepdims=True), axis=0, keepdims=True)
        dout = e * (live / D)
        dg_part = jnp.sum(dout * xh, axis=0, keepdims=True)
        dxh = dout * g
        out_refs[0][...] = r * (dxh - xh * jnp.mean(dxh * xh, axis=1, keepdims=True))

        @pl.when(i == 0)
        def _():
            out_refs[1][...] = loss_part
            out_refs[2][...] = dg_part

        @pl.when(i > 0)
        def _():
            out_refs[1][...] += loss_part
            out_refs[2][...] += dg_part

    const = lambda shape: pl.BlockSpec(shape, lambda j, i, kk: (0,) * len(shape))
    return _mm_fused(
        [(ff, _blk(TM, D_FF)), (wdown, const((D_FF, D))), (h1, _blk(TM, D)),
         (target, _blk(TM, D, row=lambda i: jnp.maximum(i - 1, 0))), (gf, const((1, D)))],
        [(0, 1)], nt=False, m=tp, n=D, tm=TM, tn=D,
        outs=[(jax.ShapeDtypeStruct((tp, D), F32), _blk(TM, D)), (jax.ShapeDtypeStruct((1, 1), F32), const((1, 1))),
              (jax.ShapeDtypeStruct((1, D), F32), const((1, D)))],
        epilogue=epilogue, name=name)


def _ffn_d_hidden(dh2, wdown, gu, name):
    tp = dh2.shape[0]

    def epilogue(prods, in_refs, out_refs, i, cols):
        d = prods[0]
        g = in_refs[2][:, cols].astype(F32)
        u = in_refs[2][:, _cols_at(cols, FF_TN)].astype(F32)
        sg = _sigmoid(g)
        out_refs[0][:, cols] = (d * u * sg * (1.0 + g * (1.0 - sg))).astype(BF)
        out_refs[0][:, _cols_at(cols, FF_TN)] = (d * g * sg).astype(BF)

    return _mm_fused(
        [(dh2, _blk(TM, D)), (wdown, pl.BlockSpec((FF_TN, D), lambda j, i, kk: (j, 0))), (gu, _blk(TM, 2 * FF_TN, lambda j, kk: j))],
        [(0, 1)], nt=True, m=tp, n=D_FF, tm=TM, tn=FF_TN,
        outs=[(jax.ShapeDtypeStruct((tp, 2 * D_FF), BF), _blk(TM, 2 * FF_TN, lambda j, kk: j))],
        epilogue=epilogue, name=name, sub=SUB_COLS)[0]


def _ffn_d_in(dgu, wgu_t, h1, rstd, g2, dh2, name):
    tp = dgu.shape[0]
    nk = 2

    def epilogue(prods, in_refs, out_refs, i, cols):
        r = in_refs[3][...]
        xh = in_refs[2][...] * r
        dxn = prods[0]
        dxh = dxn * in_refs[4][...]
        out_refs[0][...] = r * (dxh - xh * jnp.mean(dxh * xh, axis=1, keepdims=True)) + in_refs[5][...]
        part = jnp.sum(dxn * xh, axis=0, keepdims=True)

        @pl.when(i == 0)
        def _():
            out_refs[1][...] = part

        @pl.when(i > 0)
        def _():
            out_refs[1][...] += part

    const = lambda shape: pl.BlockSpec(shape, lambda j, i, kk: (0,) * len(shape))
    return _mm_fused(
        [(dgu, pl.BlockSpec((TM, D_FF), lambda j, i, kk: (i, kk))), (wgu_t, pl.BlockSpec((D_FF, D), lambda j, i, kk: (kk, 0))),
         (h1, _blk(TM, D)), (rstd, _blk(TM, 1)), (g2, const((1, D))), (dh2, _blk(TM, D))],
        [(0, 1)], nt=False, m=tp, n=D, tm=TM, tn=D, nk=nk,
        outs=[(jax.ShapeDtypeStruct((tp, D), F32), _blk(TM, D)), (jax.ShapeDtypeStruct((1, D), F32), const((1, D)))],
        epilogue=epilogue, name=name)


def _branch_merge(y_m, y_g, wbm, wbg, pbig, name):
    tp = y_m.shape[0]

    def epilogue(prods, in_refs, out_refs, i, cols):
        pm, pg = prods[0].astype(BF), prods[1].astype(BF)
        out_refs[0][:, cols] = pm
        out_refs[1][:, cols] = pg
        out_refs[2][:, cols] = (_sigmoid(in_refs[4][:, cols].astype(F32)) * pm.astype(F32)
                                + _sigmoid(in_refs[5][:, cols].astype(F32)) * pg.astype(F32)).astype(BF)

    const = lambda shape: pl.BlockSpec(shape, lambda j, i, kk: (0,) * len(shape))
    shp = jax.ShapeDtypeStruct((tp, D), BF)
    return _mm_fused(
        [(y_m, _blk(TM, D)), (wbm, const((D, D))), (y_g, _blk(TM, D)), (wbg, const((D, D))),
         (pbig, _blk(TM, D, lambda j, kk: CB_GM)), (pbig, _blk(TM, D, lambda j, kk: CB_GG))],
        [(0, 1), (2, 3)], nt=False, m=tp, n=D, tm=TM, tn=D,
        outs=[(shp, _blk(TM, D)), (shp, _blk(TM, D)), (shp, _blk(TM, D))], epilogue=epilogue, name=name, sub=SUB_COLS)


def _merge_d(dh1, wout, pm, pg, pbig, name):
    tp = dh1.shape[0]

    def epilogue(prods, in_refs, out_refs, i, cols):
        d = prods[0]
        sm = _sigmoid(in_refs[4][:, cols].astype(F32))
        sg = _sigmoid(in_refs[5][:, cols].astype(F32))
        out_refs[0][:, cols] = (d * sm).astype(BF)
        out_refs[1][:, cols] = (d * sg).astype(BF)
        out_refs[2][:, cols] = (d * in_refs[2][:, cols].astype(F32) * sm * (1.0 - sm)).astype(BF)
        out_refs[2][:, _cols_at(cols, D)] = (d * in_refs[3][:, cols].astype(F32) * sg * (1.0 - sg)).astype(BF)

    const = lambda shape: pl.BlockSpec(shape, lambda j, i, kk: (0,) * len(shape))
    shp = jax.ShapeDtypeStruct((tp, D), BF)
    return _mm_fused(
        [(dh1, _blk(TM, D)), (wout, const((D, D))), (pm, _blk(TM, D)), (pg, _blk(TM, D)),
         (pbig, _blk(TM, D, lambda j, kk: CB_GM)), (pbig, _blk(TM, D, lambda j, kk: CB_GG))],
        [(0, 1)], nt=True, m=tp, n=D, tm=TM, tn=D,
        outs=[(shp, _blk(TM, D)), (shp, _blk(TM, D)),
              (jax.ShapeDtypeStruct((tp, N_ALL), BF), _blk(TM, 2 * D, lambda j, kk: CB_GM // 2))],
        epilogue=epilogue, name=name, sub=SUB_COLS)


def _out_proj_norm(merged, wout, h0, g2, name):
    tp = merged.shape[0]
    tm = _mm_rows(tp)

    def epilogue(prods, in_refs, out_refs, i, cols):
        x = prods[0] + in_refs[2][...]
        r = lax.rsqrt(jnp.mean(x * x, axis=1, keepdims=True) + EPS)
        out_refs[0][...] = x
        out_refs[1][...] = (x * r * in_refs[3][...]).astype(BF)
        out_refs[2][...] = r

    const = lambda shape: pl.BlockSpec(shape, lambda j, i, kk: (0,) * len(shape))
    return _mm_fused(
        [(merged, _blk(tm, D)), (wout, const((D, D))), (h0, _blk(tm, D)), (g2, const((1, D)))],
        [(0, 1)], nt=False, m=tp, n=D, tm=tm, tn=D,
        outs=[(jax.ShapeDtypeStruct((tp, D), F32), _blk(tm, D)), (jax.ShapeDtypeStruct((tp, D), BF), _blk(tm, D)),
              (jax.ShapeDtypeStruct((tp, 1), F32), _blk(tm, 1))],
        epilogue=epilogue, name=name)


def _adamw(w, g, m, v, name):
    rows, cols = w.shape
    tr = 128 if rows % 128 == 0 else rows

    def body(w_ref, g_ref, m_ref, v_ref, d_ref, nm_ref, nv_ref):
        gv = g_ref[...]
        nm = ADAM_B1 * m_ref[...] + (1.0 - ADAM_B1) * gv
        nv = ADAM_B2 * v_ref[...] + (1.0 - ADAM_B2) * (gv * gv)
        m_hat = nm / (1.0 - ADAM_B1 ** ADAM_STEP)
        v_hat = nv / (1.0 - ADAM_B2 ** ADAM_STEP)
        d_ref[...] = -ADAM_LR * (m_hat / (jnp.sqrt(v_hat) + ADAM_EPS) + ADAM_WD * w_ref[...])
        nm_ref[...] = nm
        nv_ref[...] = nv

    spec = pl.BlockSpec((tr, cols), lambda i: (i, 0))
    shp = jax.ShapeDtypeStruct((rows, cols), F32)
    return pl.pallas_call(body, grid=(rows // tr,), in_specs=[spec] * 4, out_specs=[spec] * 3,
                          out_shape=(shp,) * 3, compiler_params=_params(), name=name)(w, g, m, v)


def _place_small(dsmall, dproj, name):
    tp = dsmall.shape[0]

    def body(s_ref, _, o_ref):
        o_ref[...] = s_ref[...]

    return pl.pallas_call(
        body, grid=(tp // TM,), in_specs=[_row_spec(N_SMALL), ANY], out_specs=_row_spec(N_SMALL, N_BIG // N_SMALL),
        out_shape=jax.ShapeDtypeStruct(dproj.shape, dproj.dtype), input_output_aliases={1: 0},
        compiler_params=_params(), name=name)(dsmall, dproj)


def _row_tile(rows, cap=512):
    best = rows
    for cand in range(8, min(rows, cap) + 1, 8):
        if rows % cand == 0:
            best = cand
    return best


def _add2(a, b, out_dtype, name):
    rows, cols = a.shape
    tr = _row_tile(rows)

    def body(a_ref, b_ref, o_ref):
        o_ref[...] = (a_ref[...] + b_ref[...]).astype(o_ref.dtype)

    spec = pl.BlockSpec((tr, cols), lambda i: (i, 0))
    return pl.pallas_call(body, grid=(rows // tr,), in_specs=[spec] * 2, out_specs=spec,
                          out_shape=jax.ShapeDtypeStruct((rows, cols), out_dtype), compiler_params=_params(), name=name)(a, b)


def _add4(first, rest, name):
    rows, cols = first.shape
    tr = _row_tile(rows, 256)

    def body(f_ref, r_ref, o_ref):
        up = lambda v: v.astype(F32)
        o_ref[...] = ((up(f_ref[...]) + up(r_ref[0])) + up(r_ref[1])) + up(r_ref[2])

    return pl.pallas_call(body, grid=(rows // tr,),
                          in_specs=[pl.BlockSpec((tr, cols), lambda i: (i, 0)), pl.BlockSpec((3, tr, cols), lambda i: (0, i, 0))],
                          out_specs=pl.BlockSpec((tr, cols), lambda i: (i, 0)),
                          out_shape=jax.ShapeDtypeStruct((rows, cols), F32), compiler_params=_params(), name=name)(first, rest)


def _chunk_consts(length=CHUNK):
    r2 = lax.broadcasted_iota(jnp.int32, (length, length), 0)
    c2 = lax.broadcasted_iota(jnp.int32, (length, length), 1)
    tri = r2 >= c2
    return dict(tri=tri, tril_f=tri.astype(F32), triu_f=(r2 <= c2).astype(F32),
                lane=lax.broadcasted_iota(jnp.int32, (length, N_SMALL), 1),
                rowio=lax.broadcasted_iota(jnp.int32, (length, 1), 0),
                ones=jnp.ones((length, N_SMALL), F32))


def _valid_rows(block, c):
    row = block * TM + c * CHUNK + lax.broadcasted_iota(jnp.int32, (CHUNK, 1), 0)
    return row >= FIRST_VALID


def _col(x, lane, idx):
    return jnp.sum(jnp.where(lane == idx, x, 0.0), axis=1, keepdims=True)


def _last_row(x, rowio):
    return jnp.sum(jnp.where(rowio == rowio.shape[0] - 1, x, 0.0), axis=0, keepdims=True)


def _sum_all(x):
    return jnp.sum(jnp.sum(x, axis=1, keepdims=True), axis=0, keepdims=True)


def _headnorm_fwd(hm, gain, gate_act):
    rs = lax.rsqrt(jnp.mean(hm * hm, axis=1, keepdims=True) + EPS)
    return hm * rs * gain * gate_act


def _headnorm_bwd(dy, hm, gain, gate_act):
    rs = lax.rsqrt(jnp.mean(hm * hm, axis=1, keepdims=True) + EPS)
    xh = hm * rs
    dact = dy * xh * gain
    dgain = jnp.sum(dy * gate_act * xh, axis=0, keepdims=True)
    dxh = dy * gate_act * gain
    dhm = rs * (dxh - xh * jnp.mean(dxh * xh, axis=1, keepdims=True))
    return dhm, dact, dgain


def _mlstm_gates(sm, gbias, valid, k):
    pre = sm + gbias
    lf = jnp.where(valid, _logsig(pre), 0.0)
    b_all = _nn(k["tril_f"], lf, precision=HI)
    li_all = jnp.where(valid, pre, NEG)
    return pre, li_all, b_all


def _mlstm_open(h, qh, kh, c_st, li_all, b_all, k):
    lane = k["lane"]
    sel = jnp.where(lane == h, 1.0, 0.0) - jnp.where(lane == NH + h, 1.0, 0.0)
    x = jnp.where(lane < NH, li_all, jnp.where(lane < 2 * NH, b_all, 0.0))
    cb = c_st.astype(BF)
    return dict(ubc=_nt(sel, x, precision=HI), sim=_nt(qh, kh), cb=cb, cq=_nt(qh, cb))


def _mlstm_weights(h, f, qh, vh, li_all, b_all, n_row, m11, k):
    lane, tri, rowio = k["lane"], k["tri"], k["rowio"]
    b_col = _col(b_all, lane, NH + h)
    li_col = _col(li_all, lane, h)
    dmat = jnp.where(tri, b_col + f["ubc"], NEG)
    m_row = jnp.maximum(b_col + m11, jnp.max(dmat, axis=1, keepdims=True))
    e = jnp.exp(dmat - m_row)
    w_mat = e * f["sim"]
    a = jnp.exp(b_col + m11 - m_row)
    qf = qh.astype(F32)
    nq = jnp.sum(qf * n_row, axis=1, keepdims=True)
    g = _last_row(b_col, rowio)
    wlog = g - b_col + li_col
    m_new = jnp.maximum(g + m11, jnp.max(wlog, axis=0, keepdims=True))
    a_s = jnp.exp(g + m11 - m_new)
    w = jnp.exp(wlog - m_new)
    return dict(f, e=e, w_mat=w_mat, a=a, qf=qf, nq=nq, m_row=m_row, m_new=m_new, a_s=a_s, w=w,
                wv=_nn(w_mat.astype(BF), vh))


def _mlstm_out(f):
    num = f["a"] * f["cq"] + f["wv"]
    den = f["a"] * f["nq"] + jnp.sum(f["w_mat"], axis=1, keepdims=True)
    floor = jnp.exp(-f["m_row"])
    r = jnp.maximum(jnp.abs(den), floor)
    return dict(f, den=den, floor=floor, r=r, hm=num / r)


def _mlstm_fwd(qk, pbig, small, gbias, headg, name):
    tp = qk.shape[0]
    nb = tp // TM

    def body(qk_ref, v_ref, mo_ref, sm_ref, gb_ref, hg_ref, y_ref, cs_ref, ns_ref, c_scr, n_scr):
        blk = pl.program_id(0)

        @pl.when(blk == 0)
        def _():
            c_scr[...] = jnp.zeros_like(c_scr)
            n_scr[...] = jnp.zeros_like(n_scr)

        k = _chunk_consts()
        io8 = lax.broadcasted_iota(jnp.int32, (8, DQK), 0)

        def chunk(c, carry):
            r0 = pl.multiple_of(c * CHUNK, CHUNK)
            rows = pl.ds(r0, CHUNK)
            valid = _valid_rows(blk, c)
            _, li_all, b_all = _mlstm_gates(sm_ref[rows, :], gb_ref[...], valid, k)
            heads = range(NH)
            qs = [qk_ref[rows, h * DQK:(h + 1) * DQK] for h in heads]
            ks = [qk_ref[rows, NH * DQK + h * DQK:NH * DQK + (h + 1) * DQK] for h in heads]
            vs = [v_ref[rows, h * DV:(h + 1) * DV] for h in heads]
            cst = [c_scr[h] for h in heads]
            nrow = [n_scr[h, 0:1, :] for h in heads]
            m11 = [jnp.max(n_scr[h, 1:2, :], axis=1, keepdims=True) for h in heads]
            f = [_mlstm_open(h, qs[h], ks[h], cst[h], li_all, b_all, k) for h in heads]
            f = [_mlstm_weights(h, f[h], qs[h], vs[h], li_all, b_all, nrow[h], m11[h], k) for h in heads]
            wk = [f[h]["w"] * ks[h].astype(F32) for h in heads]
            kv = [_tn(vs[h], wk[h].astype(BF)) for h in heads]
            for h in heads:
                hm = _mlstm_out(f[h])["hm"]
                gate = _sigmoid(mo_ref[rows, h * DV:(h + 1) * DV].astype(F32))
                y_ref[rows, h * DV:(h + 1) * DV] = _headnorm_fwd(hm, hg_ref[:, h * DV:(h + 1) * DV], gate).astype(BF)
                cs_ref[c, h] = f[h]["cb"]
                ns_ref[c, h] = jnp.where(io8 == 0, nrow[h], jnp.where(io8 == 1, m11[h], 0.0))
                c_scr[h] = f[h]["a_s"] * cst[h] + kv[h]
                n_scr[h, 0:1, :] = f[h]["a_s"] * nrow[h] + jnp.sum(wk[h], axis=0, keepdims=True)
                n_scr[h, 1:2, :] = jnp.broadcast_to(f[h]["m_new"], (1, DQK))
            return carry

        lax.fori_loop(0, CPB, chunk, 0, unroll=2)

    return pl.pallas_call(
        body, grid=(nb,),
        in_specs=[_row_spec(D), _row_spec(D, CB_MV), _row_spec(D, CB_MO), _row_spec(N_SMALL), _full_spec((1, N_SMALL)), _full_spec((1, D))],
        out_specs=[_row_spec(D), pl.BlockSpec((CPB, NH, DV, DQK), lambda i: (i, 0, 0, 0)),
                   pl.BlockSpec((CPB, NH, 8, DQK), lambda i: (i, 0, 0, 0))],
        out_shape=(jax.ShapeDtypeStruct((tp, D), BF), jax.ShapeDtypeStruct((tp // CHUNK, NH, DV, DQK), BF),
                   jax.ShapeDtypeStruct((tp // CHUNK, NH, 8, DQK), F32)),
        scratch_shapes=[pltpu.VMEM((NH, DV, DQK), F32), pltpu.VMEM((NH, 8, DQK), F32)],
        compiler_params=_params(), name=name)(qk, pbig, pbig, small, gbias, headg)


def _mlstm_bwd(dy, qk, pbig, small, gbias, headg, cs, ns, dproj, name, ride=()):
    tp = qk.shape[0]
    nb = tp // TM
    nr = len(ride)

    def body(*refs):
        dy_ref, qk_ref, v_ref, mo_ref, sm_ref, gb_ref, hg_ref, cs_ref, ns_ref = refs[:9]
        ride_in = refs[10:10 + nr]
        dqk_ref, dproj_ref, dsm_ref, dgb_ref, dhg_ref = refs[10 + nr:15 + nr]
        ride_out = refs[15 + nr:15 + 2 * nr]
        dc_scr, dn_scr = refs[15 + 2 * nr:17 + 2 * nr]
        step = pl.program_id(0)
        blk = nb - 1 - step
        sent = _scatter_copies(ride_in, ride_out, *refs[17 + 2 * nr:]) if nr else []

        @pl.when(step == 0)
        def _():
            dc_scr[...] = jnp.zeros_like(dc_scr)
            dn_scr[...] = jnp.zeros_like(dn_scr)
            dgb_ref[...] = jnp.zeros_like(dgb_ref)
            dhg_ref[...] = jnp.zeros_like(dhg_ref)
            for cp in sent:
                cp.start()

        k = _chunk_consts()
        lane, rowio = k["lane"], k["rowio"]

        def chunk(cc, carry):
            c = CPB - 1 - cc
            r0 = pl.multiple_of(c * CHUNK, CHUNK)
            rows = pl.ds(r0, CHUNK)
            valid = _valid_rows(blk, c)
            pre, li_all, b_all = _mlstm_gates(sm_ref[rows, :], gb_ref[...], valid, k)
            dli_all = jnp.zeros((CHUNK, N_SMALL), F32)
            db_all = jnp.zeros((CHUNK, N_SMALL), F32)
            heads = range(NH)
            qs = [qk_ref[rows, h * DQK:(h + 1) * DQK] for h in heads]
            ks = [qk_ref[rows, NH * DQK + h * DQK:NH * DQK + (h + 1) * DQK] for h in heads]
            vs = [v_ref[rows, h * DV:(h + 1) * DV] for h in heads]
            cst = [cs_ref[c, h].astype(F32) for h in heads]
            nrow = [ns_ref[c, h, 0:1, :] for h in heads]
            m11 = [jnp.max(ns_ref[c, h, 1:2, :], axis=1, keepdims=True) for h in heads]
            f = [_mlstm_open(h, qs[h], ks[h], cst[h], li_all, b_all, k) for h in heads]
            f = [_mlstm_weights(h, f[h], qs[h], vs[h], li_all, b_all, nrow[h], m11[h], k) for h in heads]
            f = [_mlstm_out(f[h]) for h in heads]
            t = []
            for h in heads:
                gain = hg_ref[:, h * DV:(h + 1) * DV]
                gate = _sigmoid(mo_ref[rows, h * DV:(h + 1) * DV].astype(F32))
                dhm, dgate, dgain = _headnorm_bwd(dy_ref[rows, h * DV:(h + 1) * DV].astype(F32), f[h]["hm"], gain, gate)
                dproj_ref[rows, D + h * DV:D + (h + 1) * DV] = (dgate * gate * (1.0 - gate)).astype(BF)
                dhg_ref[:, h * DV:(h + 1) * DV] += dgain
                r, den = f[h]["r"], f[h]["den"]
                dnum = dhm / r
                dr = -jnp.sum(dhm * f[h]["hm"], axis=1, keepdims=True) / r
                dden = jnp.where(jnp.abs(den) > f[h]["floor"], dr * jnp.sign(den), 0.0)
                dnb = dnum.astype(BF)
                dc_new = dc_scr[h]
                dcb = dc_new.astype(BF)
                t.append(dict(dnum=dnum, dden=dden, dnb=dnb, dc_new=dc_new, dn_new=dn_scr[h],
                              dwm=_nt(dnb, vs[h]), vdc=_nn(vs[h], dcb), kdc=_nt(ks[h], dcb)))
            for h in heads:
                dw_mat = t[h]["dwm"] + t[h]["dden"]
                dsim = (f[h]["e"] * dw_mat).astype(BF)
                gm = f[h]["w_mat"] * dw_mat
                t[h].update(gm=gm, dv0=_tn(f[h]["w_mat"].astype(BF), t[h]["dnb"]), dq0=_nn(dsim, ks[h]),
                            dq1=_nn(t[h]["dnb"], f[h]["cb"]), dk0=_tn(dsim, qs[h]),
                            dcq=_tn((f[h]["a"] * t[h]["dnum"]).astype(BF), qs[h]), cs2=_tn(gm, k["ones"], precision=HI))
            for h in heads:
                a, w, a_s = f[h]["a"], f[h]["w"], f[h]["a_s"]
                dnum, dden, dc_new, dn_new, vdc, gm = (t[h][n] for n in ("dnum", "dden", "dc_new", "dn_new", "vdc", "gm"))
                kf = ks[h].astype(F32)
                dproj_ref[rows, h * DV:(h + 1) * DV] = (t[h]["dv0"] + w * t[h]["kdc"]).astype(BF)
                adden = a * dden
                dqk_ref[rows, h * DQK:(h + 1) * DQK] = t[h]["dq0"] + a * t[h]["dq1"] + adden * nrow[h]
                dqk_ref[rows, NH * DQK + h * DQK:NH * DQK + (h + 1) * DQK] = t[h]["dk0"] + w * vdc + w * dn_new
                da = jnp.sum(dnum * f[h]["cq"], axis=1, keepdims=True) + dden * f[h]["nq"]
                dw = jnp.sum(vdc * kf, axis=1, keepdims=True) + jnp.sum(kf * dn_new, axis=1, keepdims=True)
                da_s = _sum_all(dc_new * cst[h]) + jnp.sum(dn_new * nrow[h], axis=1, keepdims=True)
                wdw = w * dw
                rs = jnp.sum(gm, axis=1, keepdims=True)
                cs_col = _col(t[h]["cs2"], lane, 0)
                dg = a_s * da_s + jnp.sum(wdw, axis=0, keepdims=True)
                db = a * da + rs - cs_col - wdw + jnp.where(rowio == CHUNK - 1, dg, 0.0)
                dli_all = dli_all + jnp.where(lane == h, cs_col + wdw, 0.0)
                db_all = db_all + jnp.where(lane == NH + h, db, 0.0)
                dc_scr[h] = a_s * dc_new + t[h]["dcq"]
                dn_scr[h] = a_s * dn_new + jnp.sum(adden * f[h]["qf"], axis=0, keepdims=True)
            dlf_all = _nn(k["triu_f"], db_all, precision=HI)
            dsm = jnp.where(valid, dli_all + dlf_all * _sigmoid(-pre), 0.0)
            dsm = jnp.where(lane < 2 * NH, dsm, 0.0)
            dsm_ref[rows, :] = dsm
            dgb_ref[0:1, :] += jnp.sum(dsm, axis=0, keepdims=True)
            return carry

        lax.fori_loop(0, CPB, chunk, 0, unroll=2)

        if nr:
            @pl.when(step == nb - 1)
            def _():
                for cp in sent:
                    cp.wait_recv()
                for cp in sent:
                    cp.wait_send()

    rev = lambda col: (lambda i: (nb - 1 - i, col))
    rspec = lambda width, col=0: pl.BlockSpec((TM, width), rev(col))
    ride_shapes, ride_sems = _scatter_shapes(ride) if nr else ((), [])
    outs = pl.pallas_call(
        body, grid=(nb,),
        in_specs=[rspec(D), rspec(D), rspec(D, CB_MV), rspec(D, CB_MO), rspec(N_SMALL), _full_spec((1, N_SMALL)), _full_spec((1, D)),
                  pl.BlockSpec((CPB, NH, DV, DQK), lambda i: (nb - 1 - i, 0, 0, 0)),
                  pl.BlockSpec((CPB, NH, 8, DQK), lambda i: (nb - 1 - i, 0, 0, 0)), ANY] + [ANY] * nr,
        out_specs=[rspec(D), rspec(2 * D, CB_MV // 2), rspec(N_SMALL), _full_spec((8, N_SMALL)), _full_spec((1, D))] + [ANY] * nr,
        out_shape=(jax.ShapeDtypeStruct((tp, D), F32), jax.ShapeDtypeStruct(dproj.shape, BF),
                   jax.ShapeDtypeStruct((tp, N_SMALL), F32), jax.ShapeDtypeStruct((8, N_SMALL), F32),
                   jax.ShapeDtypeStruct((1, D), F32)) + tuple(ride_shapes),
        scratch_shapes=[pltpu.VMEM((NH, DV, DQK), F32), pltpu.VMEM((NH, 1, DQK), F32)] + ride_sems,
        input_output_aliases={9: 1}, compiler_params=_params(), name=name)(dy, qk, pbig, pbig, small, gbias, headg, cs, ns, dproj, *ride)
    return tuple(outs[:5]) + (list(outs[5:]),)


def _gla_loga(sm_ref, a2_ref, a2b_ref, blk):
    za = _nn(sm_ref[...].astype(BF), a2_ref[...]) + a2b_ref[...]
    row = blk * TM + lax.broadcasted_iota(jnp.int32, (TM, 1), 0)
    return za, jnp.where(row >= FIRST_VALID, _logsig(za) / G_TAU, 0.0)


def _gla_head(h, q_ref, k_ref, rows, bc, btot, k):
    sl = slice(h * DQK, (h + 1) * DQK)
    bch = bc[:, sl]
    bth = btot[:, sl]
    gq = q_ref[rows, h * DQK:(h + 1) * DQK].astype(F32)
    gk = k_ref[rows, NH * DQK + h * DQK:NH * DQK + (h + 1) * DQK].astype(F32)
    e_pos = jnp.exp(bch) * (DQK ** -0.5)
    e_neg = jnp.exp(-bch)
    e_end = jnp.exp(bth - bch)
    qd = gq * e_pos
    ki = gk * e_neg
    ke = gk * e_end
    att = jnp.where(k["tri"], _nt(qd.astype(BF), ki.astype(BF)), 0.0)
    return dict(e_pos=e_pos, e_neg=e_neg, e_end=e_end, qd=qd, ki=ki, ke=ke, att=att, decay=jnp.exp(bth))


def _gla_fwd(pbig, small, a2p, a2b, headg, name):
    tp = pbig.shape[0]
    nb = tp // TM

    def body(qk_ref, v_ref, gr_ref, sm_ref, a2_ref, a2b_ref, hg_ref, y_ref, ss_ref, s_scr, lg_scr):
        blk = pl.program_id(0)

        @pl.when(blk == 0)
        def _():
            s_scr[...] = jnp.zeros_like(s_scr)

        k = _chunk_consts(G_CHUNK)
        _, loga = _gla_loga(sm_ref, a2_ref, a2b_ref, blk)
        lg_scr[...] = loga

        def chunk(c, carry):
            r0 = pl.multiple_of(c * G_CHUNK, G_CHUNK)
            rows = pl.ds(r0, G_CHUNK)
            bc = _nn(k["tril_f"], lg_scr[rows, :], precision=HI)
            btot = _last_row(bc, k["rowio"])
            heads = range(NH)
            f = [_gla_head(h, qk_ref, qk_ref, rows, bc, btot, k) for h in heads]
            vs = [v_ref[rows, h * DV:(h + 1) * DV] for h in heads]
            sst = [s_scr[h] for h in heads]
            sbs = [s.astype(BF) for s in sst]
            inter = [_nt(f[h]["qd"].astype(BF), sbs[h]) for h in heads]
            intra = [_nn(f[h]["att"].astype(BF), vs[h]) for h in heads]
            kv = [_tn(vs[h], f[h]["ke"].astype(BF)) for h in heads]
            for h in heads:
                gr = gr_ref[rows, h * DV:(h + 1) * DV].astype(F32)
                y_ref[rows, h * DV:(h + 1) * DV] = _headnorm_fwd(intra[h] + inter[h], hg_ref[:, h * DV:(h + 1) * DV],
                                                                   gr * _sigmoid(gr)).astype(BF)
                ss_ref[c, h] = sbs[h]
                s_scr[h] = sst[h] * f[h]["decay"] + kv[h]
            return carry

        lax.fori_loop(0, G_CPB, chunk, 0, unroll=2)

    return pl.pallas_call(
        body, grid=(nb,),
        in_specs=[_row_spec(D, CB_GQK), _row_spec(D, CB_GV), _row_spec(D, CB_GR), _row_spec(N_SMALL),
                  _full_spec((N_SMALL, NH * DQK)), _full_spec((1, NH * DQK)), _full_spec((1, D))],
        out_specs=[_row_spec(D), pl.BlockSpec((G_CPB, NH, DV, DQK), lambda i: (i, 0, 0, 0))],
        out_shape=(jax.ShapeDtypeStruct((tp, D), BF), jax.ShapeDtypeStruct((tp // G_CHUNK, NH, DV, DQK), BF)),
        scratch_shapes=[pltpu.VMEM((NH, DV, DQK), F32), pltpu.VMEM((TM, NH * DQK), F32)],
        compiler_params=_params(), name=name)(pbig, pbig, pbig, small, a2p, a2b, headg)


def _gla_bwd(dy, pbig, small, a2p, a2b, headg, ss, dsm_m, dproj, name):
    tp = pbig.shape[0]
    nb = tp // TM
    nqk = NH * DQK

    def body(dy_ref, qk_ref, v_ref, gr_ref, sm_ref, a2_ref, a2b_ref, hg_ref, ss_ref, dsmm_ref, _,
             dproj_ref, dsm_ref, da2_ref, da2b_ref, dhg_ref, ds_scr, lg_scr, dza_scr):
        step = pl.program_id(0)
        blk = nb - 1 - step

        @pl.when(step == 0)
        def _():
            ds_scr[...] = jnp.zeros_like(ds_scr)
            da2_ref[...] = jnp.zeros_like(da2_ref)
            da2b_ref[...] = jnp.zeros_like(da2b_ref)
            dhg_ref[...] = jnp.zeros_like(dhg_ref)

        k = _chunk_consts(G_CHUNK)
        rowio = k["rowio"]
        za, loga = _gla_loga(sm_ref, a2_ref, a2b_ref, blk)
        lg_scr[...] = loga

        def chunk(cc, carry):
            c = G_CPB - 1 - cc
            r0 = pl.multiple_of(c * G_CHUNK, G_CHUNK)
            rows = pl.ds(r0, G_CHUNK)
            bc = _nn(k["tril_f"], lg_scr[rows, :], precision=HI)
            btot = _last_row(bc, rowio)
            heads = range(NH)
            f = [_gla_head(h, qk_ref, qk_ref, rows, bc, btot, k) for h in heads]
            vs = [v_ref[rows, h * DV:(h + 1) * DV] for h in heads]
            sbs = [ss_ref[c, h] for h in heads]
            qdb = [f[h]["qd"].astype(BF) for h in heads]
            attb = [f[h]["att"].astype(BF) for h in heads]
            inter = [_nt(qdb[h], sbs[h]) for h in heads]
            intra = [_nn(attb[h], vs[h]) for h in heads]
            dsn = [ds_scr[h] for h in heads]
            dsb = [d.astype(BF) for d in dsn]
            dke = [_nn(vs[h], dsb[h]) for h in heads]
            dv1 = [_nt(f[h]["ke"].astype(BF), dsb[h]) for h in heads]
            t = []
            for h in heads:
                gr = gr_ref[rows, h * DV:(h + 1) * DV].astype(F32)
                sg = _sigmoid(gr)
                gain = hg_ref[:, h * DV:(h + 1) * DV]
                do, dact, dgain = _headnorm_bwd(dy_ref[rows, h * DV:(h + 1) * DV].astype(F32), intra[h] + inter[h], gain, gr * sg)
                dproj_ref[rows, 2 * D + h * DV:2 * D + (h + 1) * DV] = (dact * sg * (1.0 + gr * (1.0 - sg))).astype(BF)
                dhg_ref[:, h * DV:(h + 1) * DV] += dgain
                dob = do.astype(BF)
                t.append(dict(dob=dob, datt=_nt(dob, vs[h]), dv0=_tn(attb[h], dob), dq1=_nn(dob, sbs[h]), dsq=_tn(dob, qdb[h])))
            for h in heads:
                datt = jnp.where(k["tri"], t[h]["datt"], 0.0).astype(BF)
                t[h].update(dq0=_nn(datt, f[h]["ki"].astype(BF)), dki=_tn(datt, qdb[h]))
            dbc_parts = []
            for h in heads:
                dqd = t[h]["dq0"] + t[h]["dq1"]
                dki = t[h]["dki"]
                dproj_ref[rows, D + h * DV:D + (h + 1) * DV] = (t[h]["dv0"] + dv1[h]).astype(BF)
                dproj_ref[rows, h * DQK:(h + 1) * DQK] = (dqd * f[h]["e_pos"]).astype(BF)
                dproj_ref[rows, nqk + h * DQK:nqk + (h + 1) * DQK] = (dki * f[h]["e_neg"] + dke[h] * f[h]["e_end"]).astype(BF)
                dke_ke = dke[h] * f[h]["ke"]
                dbtot = (jnp.sum(dke_ke, axis=0, keepdims=True)
                         + jnp.sum(dsn[h] * sbs[h].astype(F32), axis=0, keepdims=True) * f[h]["decay"])
                dbc_parts.append(dqd * f[h]["qd"] - dki * f[h]["ki"] - dke_ke + jnp.where(rowio == G_CHUNK - 1, dbtot, 0.0))
                ds_scr[h] = dsn[h] * f[h]["decay"] + t[h]["dsq"]
            dbc = jnp.concatenate(dbc_parts, axis=1)
            dza_scr[rows, :] = _nn(k["triu_f"], dbc, precision=HI)
            return carry

        lax.fori_loop(0, G_CPB, chunk, 0, unroll=2)
        row = blk * TM + lax.broadcasted_iota(jnp.int32, (TM, 1), 0)
        dza = jnp.where(row >= FIRST_VALID, dza_scr[...] * (_sigmoid(-za) / G_TAU), 0.0)
        dzb = dza.astype(BF)
        dsm_ref[...] = (_nt(dzb, a2_ref[...]) + dsmm_ref[...]).astype(BF)
        da2_ref[...] += _tn(sm_ref[...].astype(BF), dzb)
        da2b_ref[...] += jnp.sum(dza, axis=0, keepdims=True)

    rspec = lambda width, col=0: pl.BlockSpec((TM, width), lambda i: (nb - 1 - i, col))
    return pl.pallas_call(
        body, grid=(nb,),
        in_specs=[rspec(D), rspec(D, CB_GQK), rspec(D, CB_GV), rspec(D, CB_GR), rspec(N_SMALL),
                  _full_spec((N_SMALL, nqk)), _full_spec((1, nqk)), _full_spec((1, D)),
                  pl.BlockSpec((G_CPB, NH, DV, DQK), lambda i: (nb - 1 - i, 0, 0, 0)), rspec(N_SMALL), ANY],
        out_specs=[rspec(3 * D, 0), rspec(N_SMALL), _full_spec((N_SMALL, nqk)), _full_spec((1, nqk)), _full_spec((1, D))],
        out_shape=(jax.ShapeDtypeStruct(dproj.shape, BF),
                   jax.ShapeDtypeStruct((tp, N_SMALL), BF), jax.ShapeDtypeStruct((N_SMALL, nqk), F32),
                   jax.ShapeDtypeStruct((1, nqk), F32), jax.ShapeDtypeStruct((1, D), F32)),
        scratch_shapes=[pltpu.VMEM((NH, DV, DQK), F32), pltpu.VMEM((TM, nqk), F32), pltpu.VMEM((TM, nqk), F32)],
        input_output_aliases={10: 0}, compiler_params=_params(), name=name)(dy, pbig, pbig, pbig, small, a2p, a2b, headg, ss, dsm_m, dproj)


PIECE_BYTES = 1 << 20
MAX_PIECES = 32


def _place():
    return lax.axis_index("x"), lax.axis_index("y"), lax.axis_index("c")


def _piece_rows(rows, row_bytes, align):
    want = min(MAX_PIECES, max(1, -(-rows * row_bytes // PIECE_BYTES)))
    best = rows
    for k in range(1, want + 1):
        if rows % k == 0 and (rows // k) % align == 0:
            best = rows // k
    return best


def _remote(src, dst, send_sems, recv_sems, k, to):
    return pltpu.make_async_remote_copy(src_ref=src, dst_ref=dst, send_sem=send_sems.at[k], recv_sem=recv_sems.at[k],
                                        device_id=to, device_id_type=MESH)


def _all_gather_chips(p, name):
    rd = _gather_rider(p)

    def body(*refs):
        start, middle, finish = rd["make"](refs[:1], refs[1:2], refs[2:])
        start()
        middle()
        finish()

    return pl.pallas_call(body, in_specs=[ANY], out_specs=[ANY], out_shape=rd["out_shapes"], scratch_shapes=rd["sems"],
                          name=name)(p)[0]


def _gather_rider(p):
    r, n = p.shape
    rh = r // 2
    align = 32 // p.dtype.itemsize
    assert r % (2 * align) == 0
    cr = _piece_rows(rh, n * p.dtype.itemsize, align)

    def make(in_refs, out_refs, sem_refs):
        p_ref, o_ref = in_refs[0], out_refs[0]
        send_sems, recv_sems = sem_refs
        x, y, c = _place()
        chips = [(1 - x, y), (x, 1 - y), (1 - x, 1 - y)]
        sib = (x, y, 1 - c)

        def half(hc, piece=None):
            if piece is None:
                return pl.ds(pl.multiple_of(hc * rh, align), rh)
            return pl.ds(pl.multiple_of(hc * rh + piece * cr, align), cr)

        first = [_remote(p_ref.at[half(c)], o_ref.at[j, half(c)], send_sems, recv_sems, j, (*chip, c))
                 for j, chip in enumerate(chips)]
        passed = [[_remote(o_ref.at[j, half(c, i)], o_ref.at[j, half(c, i)], send_sems, recv_sems, 3 + j, sib)
                   for i in range(rh // cr)] for j in range(3)]
        blocks = [_remote(o_ref.at[j, half(c)], o_ref.at[j, half(1 - c)], send_sems, recv_sems, 3 + j, sib) for j in range(3)]

        def start():
            for cp in first:
                cp.start()

        def middle():
            for j, cp in enumerate(first):
                cp.wait_recv()
                for piece in passed[j]:
                    piece.start()

        def finish():
            for block in blocks:
                block.wait_send()
                block.wait_recv()
            for cp in first:
                cp.wait_send()

        return start, middle, finish

    return dict(inputs=[p], out_shapes=(jax.ShapeDtypeStruct((3, r, n), p.dtype),),
                sems=[pltpu.SemaphoreType.DMA((6,)), pltpu.SemaphoreType.DMA((6,))], make=make)


def _scatter_rider(items):
    out_shapes, sems = _scatter_shapes(items)

    def make(in_refs, out_refs, sem_refs):
        sent = _scatter_copies(in_refs, out_refs, *sem_refs)

        def start():
            for cp in sent:
                cp.start()

        def finish():
            for cp in sent:
                cp.wait_recv()
            for cp in sent:
                cp.wait_send()

        return start, (lambda: None), finish

    return dict(inputs=list(items), out_shapes=out_shapes, sems=sems, make=make)


def _by_chip(mine, others):
    me = 2 * lax.axis_index("x") + lax.axis_index("y")
    by_mask = jnp.stack([mine, others[1], others[0], others[2]])
    return [lax.dynamic_index_in_dim(by_mask, q ^ me, 0, keepdims=False) for q in range(4)]


def _swap_halves(items, name):
    k = len(items)

    def body(*refs):
        a_refs, got_refs = refs[:k], refs[k:2 * k]
        send_sems, recv_sems = refs[2 * k:]
        x, y, c = _place()
        sib = (x, y, 1 - c)
        for i, a in enumerate(items):
            _, r, n = a.shape
            rh = r // 2
            cr = _piece_rows(rh, n * a.dtype.itemsize, 8)
            for q in range(4):
                for t in range(rh // cr):
                    other = pl.ds(pl.multiple_of((1 - c) * rh + t * cr, 8), cr)
                    _remote(a_refs[i].at[q, other], got_refs[i].at[q, pl.ds(t * cr, cr)], send_sems, recv_sems, i, sib).start()
        for i, a in enumerate(items):
            block = _remote(a_refs[i].at[:, pl.ds(0, a.shape[1] // 2)], got_refs[i], send_sems, recv_sems, i, sib)
            block.wait_send()
            block.wait_recv()

    return pl.pallas_call(
        body, in_specs=[ANY] * k, out_specs=[ANY] * k,
        out_shape=tuple(jax.ShapeDtypeStruct((4, a.shape[1] // 2, a.shape[2]), a.dtype) for a in items),
        scratch_shapes=[pltpu.SemaphoreType.DMA((k,)), pltpu.SemaphoreType.DMA((k,))], name=name)(*items)


def _scatter_copies(s_refs, o_refs, send_sems, recv_sems):
    x, y, c = _place()
    chips = [(1 - x, y), (x, 1 - y), (1 - x, 1 - y)]
    return [_remote(s_refs[i].at[2 * cx + cy], o_refs[i].at[j], send_sems, recv_sems, 3 * i + j, (cx, cy, c))
            for i in range(len(s_refs)) for j, (cx, cy) in enumerate(chips)]


def _scatter_shapes(items):
    k = len(items)
    return (tuple(jax.ShapeDtypeStruct((3,) + s.shape[1:], s.dtype) for s in items),
            [pltpu.SemaphoreType.DMA((3 * k,)), pltpu.SemaphoreType.DMA((3 * k,))])


def _scatter_chips(items, name):
    k = len(items)

    def body(*refs):
        sent = _scatter_copies(refs[:k], refs[k:2 * k], *refs[2 * k:])
        for cp in sent:
            cp.start()
        for cp in sent:
            cp.wait_recv()
        for cp in sent:
            cp.wait_send()

    out_shape, scratch = _scatter_shapes(items)
    return pl.pallas_call(body, in_specs=[ANY] * k, out_specs=[ANY] * k, out_shape=out_shape, scratch_shapes=scratch,
                          name=name)(*items)


def _join_halves(items, name):
    k = len(items)

    def body(*refs):
        f_refs, o_refs = refs[:k], refs[k:2 * k]
        send_sems, recv_sems = refs[2 * k:]
        x, y, c = _place()
        sib = (x, y, 1 - c)
        for i, f in enumerate(items):
            rh, n = f.shape
            cr = _piece_rows(rh, n * f.dtype.itemsize, 8)
            for t in range(rh // cr):
                rows = pl.ds(t * cr, cr)
                _remote(f_refs[i].at[rows], o_refs[i].at[rows], send_sems, recv_sems, i, sib).start()
        for i in range(k):
            block = _remote(f_refs[i], o_refs[i], send_sems, recv_sems, i, sib)
            block.wait_send()
            block.wait_recv()

    return pl.pallas_call(
        body, in_specs=[ANY] * k, out_specs=[ANY] * k, out_shape=tuple(jax.ShapeDtypeStruct(f.shape, f.dtype) for f in items),
        scratch_shapes=[pltpu.SemaphoreType.DMA((k,)), pltpu.SemaphoreType.DMA((k,))], name=name)(*items)


SMALL_ROWS = 16
SMALL_SHARD_SHAPES = [(N_META, 256), (4, 256), (G_RANK, 128), (NH, 64), (NH, 64)]
REPL_SHAPES = [(1, D), (1, D), (1, 2, NH), (1, NH * DQK), (1, D), (D,)]
W_IN_SHARD = 2054


def _pack_small(parts):
    flat = jnp.concatenate([p.reshape(-1) for p in parts])
    return jnp.pad(flat, (0, SMALL_ROWS * D - flat.shape[0])).reshape(SMALL_ROWS, D)


def _unpack_small(block, shapes):
    flat, out, off = block.reshape(-1), [], 0
    for shp in shapes:
        n = 1
        for s in shp:
            n *= s
        out.append(flat[off:off + n].reshape(shp))
        off += n
    return out


def _proj_cols_from_w_in(w_in_f):
    w_big = jnp.concatenate([w_in_f[:, 3080:5128], w_in_f[:, 5144:6168], w_in_f[:, 0:1024], w_in_f[:, 6168:8216],
                             w_in_f[:, 1024:2048], w_in_f[:, 2056:3080]], axis=1)
    w_small = jnp.concatenate([w_in_f[:, 2048:2056], w_in_f[:, 5128:5144], jnp.zeros((D, N_SMALL - 24), w_in_f.dtype)], axis=1)
    return w_big, w_small


def _w_in_from_proj_cols(d_wall):
    big, small = d_wall[:, 0:N_BIG], d_wall[:, N_BIG:N_ALL]
    return jnp.concatenate([big[:, 3072:4096], big[:, 6144:7168], small[:, 0:8], big[:, 7168:8192], big[:, 0:2048],
                            small[:, 8:24], big[:, 2048:3072], big[:, 4096:6144]], axis=1)


def kernel(x, meta_tokens, norm1_g, w_in, conv_w, conv_b, m_gate_b, g_a2, g_a2_b, m_head_g, g_head_g, w_branch_m, w_branch_g, w_out, norm2_g, w_ff_gate, w_ff_up, w_ff_down, final_g, loss_target, m_meta_tokens, m_norm1_g, m_w_in, m_conv_w, m_conv_b, m_m_gate_b, m_g_a2, m_g_a2_b, m_m_head_g, m_g_head_g, m_w_branch_m, m_w_branch_g, m_w_out, m_norm2_g, m_w_ff_gate, m_w_ff_up, m_w_ff_down, m_final_g, v_meta_tokens, v_norm1_g, v_w_in, v_conv_w, v_conv_b, v_m_gate_b, v_g_a2, v_g_a2_b, v_m_head_g, v_g_head_g, v_w_branch_m, v_w_branch_g, v_w_out, v_norm2_g, v_w_ff_gate, v_w_ff_up, v_w_ff_down, v_final_g):
    w = _gather_weights(w_in, w_branch_m, w_branch_g, w_out, w_ff_gate, w_ff_up, w_ff_down, meta_tokens, conv_w, g_a2, m_head_g, g_head_g)
    loss_local, dx, grads = _local_step(x[0], loss_target[0], w, norm1_g, conv_b, m_gate_b, g_a2_b, norm2_g, final_g, _Reducer())

    weights = [w_in, w_branch_m, w_branch_g, w_out, w_ff_gate, w_ff_up, w_ff_down, meta_tokens, conv_w, g_a2, m_head_g, g_head_g,
               norm1_g, conv_b, m_gate_b, g_a2_b, norm2_g, final_g]
    moms = [m_w_in, m_w_branch_m, m_w_branch_g, m_w_out, m_w_ff_gate, m_w_ff_up, m_w_ff_down, m_meta_tokens, m_conv_w, m_g_a2,
            m_m_head_g, m_g_head_g, m_norm1_g, m_conv_b, m_m_gate_b, m_g_a2_b, m_norm2_g, m_final_g]
    vels = [v_w_in, v_w_branch_m, v_w_branch_g, v_w_out, v_w_ff_gate, v_w_ff_up, v_w_ff_down, v_meta_tokens, v_conv_w, v_g_a2,
            v_m_head_g, v_g_head_g, v_norm1_g, v_conv_b, v_m_gate_b, v_g_a2_b, v_norm2_g, v_final_g]
    res = {}
    for nm, wt, g, m, v in zip(PACK_ORDER, weights, grads, moms, vels):
        two_d = (wt.size // wt.shape[-1], wt.shape[-1])
        d, nm_, nv_ = _adamw(wt.reshape(two_d), g.reshape(two_d), m.reshape(two_d), v.reshape(two_d), "adamw_" + nm)
        res[nm] = (g.reshape(wt.shape), d.reshape(wt.shape), nm_.reshape(wt.shape), nv_.reshape(wt.shape))

    order = ["meta_tokens", "norm1_g", "w_in", "conv_w", "conv_b", "m_gate_b", "g_a2", "g_a2_b", "m_head_g", "g_head_g",
             "w_branch_m", "w_branch_g", "w_out", "norm2_g", "w_ff_gate", "w_ff_up", "w_ff_down", "final_g"]
    loss = lax.psum(loss_local[0, 0], ("x", "y", "c"))
    grad_x = dx.reshape(x.shape)
    return (loss, grad_x, *[res[n][0] for n in order], *[res[n][1] for n in order],
            *[res[n][2] for n in order], *[res[n][3] for n in order])


PACK_ORDER = ["w_in", "w_branch_m", "w_branch_g", "w_out", "w_ff_gate", "w_ff_up", "w_ff_down", "meta_tokens", "conv_w", "g_a2",
              "m_head_g", "g_head_g", "norm1_g", "conv_b", "m_gate_b", "g_a2_b", "norm2_g", "final_g"]


def _gather_weights(w_in, w_branch_m, w_branch_g, w_out, w_ff_gate, w_ff_up, w_ff_down, meta_tokens, conv_w, g_a2, m_head_g, g_head_g):
    bf = lambda a: a.astype(BF)
    rows_local = jnp.concatenate([bf(w_branch_m[0]), bf(w_branch_g[0]), bf(w_out[0]), bf(w_ff_down[0]),
                                  bf(w_ff_gate[0].T), bf(w_ff_up[0].T)], axis=0)
    win_local = bf(w_in[0])
    small_local = _pack_small([meta_tokens, conv_w[0], g_a2[0], m_head_g[0], g_head_g[0]])
    win_all = _by_chip(win_local, _all_gather_chips(win_local, "gather_w_in"))
    small_all = _by_chip(small_local, _all_gather_chips(small_local, "gather_small"))
    w_in_f = jnp.concatenate([win_all[q] for q in range(4)], axis=1)
    small_sh = [_unpack_small(small_all[q], SMALL_SHARD_SHAPES) for q in range(4)]
    cat = lambda i: jnp.concatenate([s[i] for s in small_sh], axis=-1)
    return dict(w_in=w_in_f, rows_local=rows_local, meta=cat(0), convw=cat(1), ga2=cat(2),
                mhg=cat(3).reshape(1, D), ghg=cat(4).reshape(1, D))


def _row_weights(rows_local, gathered):
    rows_all = jnp.stack(_by_chip(rows_local, gathered))
    cut = lambda lo, hi: rows_all[:, lo:hi].reshape(4 * (hi - lo), D)
    return cut(0, 256), cut(256, 512), cut(512, 768), cut(768, 1472), _ffn_weight_rows(cut(1472, 2176), cut(2176, 2880))


def _local_step(x0, target, w, norm1_g, conv_b, m_gate_b, g_a2_b, norm2_g, final_g, reducer):
    w_in_f, meta_f, convw_f, ga2_f, mhg_f, ghg_f = w["w_in"], w["meta"], w["convw"], w["ga2"], w["mhg"], w["ghg"]
    w_big, w_small = _proj_cols_from_w_in(w_in_f)
    w_all = jnp.concatenate([w_big, w_small], axis=1)
    gbias = jnp.concatenate([m_gate_b.reshape(1, 2 * NH), jnp.zeros((1, N_SMALL - 2 * NH), F32)], axis=1)
    a2p = jnp.concatenate([jnp.zeros((8, NH * DQK), F32), ga2_f, jnp.zeros((N_SMALL - 24, NH * DQK), F32)], axis=0).astype(BF)
    convb = conv_b.reshape(1, D)
    g1 = norm1_g.reshape(1, D)
    g2 = norm2_g.reshape(1, D)
    gf = final_g.reshape(1, D)
    h0 = jnp.concatenate([jnp.zeros((FIRST_VALID, D), F32), meta_f, x0], axis=0)

    xn1, rstd1 = _rms_fwd(h0, g1, "rms1")
    pbig, rows_gathered = _mm(xn1, w_big, nt=False, out_dtype=BF, tn=1024, name="proj_big", rider=_gather_rider(w["rows_local"]))
    wbm, wbg, wout, wdown, wgu_t = _row_weights(w["rows_local"], rows_gathered[0])
    small = _mm(xn1, w_small, nt=False, out_dtype=F32, tn=N_SMALL, name="proj_small")
    qk = _conv_fwd(pbig, convw_f, convb, "conv_fwd")
    y_m, m_cs, m_ns = _mlstm_fwd(qk, pbig, small, gbias, mhg_f, "mlstm_fwd")
    y_g, g_ss = _gla_fwd(pbig, small, a2p, g_a2_b, ghg_f, "gla_fwd")
    p_m, p_g, merged = _branch_merge(y_m, y_g, wbm, wbg, pbig, "branch_merge")
    h1, hn, rstd2 = _out_proj_norm(merged, wout, h0, g2, "out_proj")
    gu, ff = _ffn_in(hn, wgu_t, "ff_in")
    dh2, loss_local, d_final_g = _ffn_down_loss(ff, wdown, h1, target, gf, "ff_down_loss")

    d_wdown = _mm_tn(ff, dh2, tm=1408, tn=1024, name="dw_ff_down")
    dgu = _ffn_d_hidden(dh2, wdown, gu, "d_ff")
    d_wgu_t = _mm_tn(dgu, hn, tm=1408, tn=1024, name="dw_ff_in")
    dh1, d_g2 = _ffn_d_in(dgu, wgu_t, h1, rstd2, g2, dh2, "d_hn")
    d_wout = _mm_tn(merged, dh1, tm=1024, tn=1024, name="dw_out")
    dp_m, dp_g, dproj = _merge_d(dh1, wout, p_m, p_g, pbig, "d_merged")
    dy_m = _mm(dp_m, wbm, nt=True, out_dtype=BF, tn=1024, name="d_ym")
    dy_g = _mm(dp_g, wbg, nt=True, out_dtype=BF, tn=1024, name="d_yg")
    d_wbm = _mm_tn(y_m, dp_m, tm=1024, tn=1024, name="dw_branch_m")
    d_wbg = _mm_tn(y_g, dp_g, tm=1024, tn=1024, name="dw_branch_g")
    fq = D_FF // 4
    gu4 = jnp.transpose(d_wgu_t.reshape(2, 2, 2, fq, D), (0, 2, 1, 3, 4)).reshape(4, 2 * fq, D)
    sq4 = jnp.concatenate([d_wbm.reshape(4, 256, D), d_wbg.reshape(4, 256, D), d_wout.reshape(4, 256, D)], axis=1)
    sums_a = reducer.partial_sums([sq4, d_wdown.reshape(4, fq, D), gu4], BF, "a")
    dqk_m, dproj, dsm_m, d_gbias, d_mhg, recv_a = _mlstm_bwd(dy_m, qk, pbig, small, gbias, mhg_f, m_cs, m_ns, dproj,
                                                              "mlstm_bwd", ride=sums_a)
    dconv, d_convwb = _conv_bwd_pre(dqk_m, pbig, convw_f, convb, "conv_bwd_pre")
    dproj = _conv_bwd_in(dconv, convw_f, dproj, "conv_bwd_in")
    dproj, dsmall, d_a2p, d_a2b, d_ghg = _gla_bwd(dy_g, pbig, small, a2p, g_a2_b, ghg_f, g_ss, dsm_m, dproj, "gla_bwd")
    dproj = _place_small(dsmall, dproj, "dproj_small")
    d_win = _w_in_from_proj_cols(_mm_tn(xn1, dproj, tm=1024, tn=1664, name="dw_in"))
    win4 = jnp.stack([d_win[:, q * W_IN_SHARD:(q + 1) * W_IN_SHARD] for q in range(4)])
    sums_b = reducer.partial_sums([win4], BF, "b")
    dxn, recv_b = _mm(dproj, w_all, nt=True, out_dtype=F32, tn=1024, tk=1664, name="d_xn", rider=_scatter_rider(sums_b))
    dh_first, dx, d_g1 = _rms_bwd(dxn, h0, rstd1, g1, dh1, "rms1_bwd", split_first=True)

    small_sharded = [dh_first[FIRST_VALID:TM], d_convwb[0:4], d_a2p[8:24], d_mhg.reshape(NH, DV), d_ghg.reshape(NH, DV)]
    replicated = [d_g1, d_convwb[4:5], d_gbias[0:1, 0:2 * NH].reshape(1, 2, NH), d_a2b, d_g2, d_final_g.reshape(D)]
    small4 = jnp.stack([_pack_small([g[:, q * shp[1]:(q + 1) * shp[1]] for g, shp in zip(small_sharded, SMALL_SHARD_SHAPES)]
                                    + replicated) for q in range(4)])
    sums_c = reducer.partial_sums([small4], F32, "c")
    recv_c = reducer.scatter(sums_c, "c")
    sq, down, gu, win, smalls = reducer.finish(sums_a + sums_b + sums_c, recv_a + recv_b + recv_c, in_chip_order=[4])
    grads = ([win, sq[0:256], sq[256:512], sq[512:768], gu[0:fq].T, gu[fq:2 * fq].T, down]
             + _unpack_small(smalls, SMALL_SHARD_SHAPES + REPL_SHAPES))
    return loss_local, dx, grads


class _Reducer:
    def partial_sums(self, items, dtype, tag):
        c = lax.axis_index("c")
        got = _swap_halves(items, "reduce_siblings_" + tag)
        sums = []
        for i, (a, g) in enumerate(zip(items, got)):
            rh, n = g.shape[1], g.shape[2]
            own = lax.dynamic_slice_in_dim(a, c * rh, rh, axis=1)
            sums.append(_add2(own.reshape(-1, n), g.reshape(-1, n), dtype, f"reduce_add2_{tag}{i}").reshape(g.shape))
        return sums

    def scatter(self, sums, tag):
        return list(_scatter_chips(sums, "reduce_chips_" + tag))

    def finish(self, sums, from_chips, in_chip_order):
        c = lax.axis_index("c")
        me = 2 * lax.axis_index("x") + lax.axis_index("y")
        halves = []
        for i, (s, f) in enumerate(zip(sums, from_chips)):
            mine = lax.dynamic_index_in_dim(s, me, 0, keepdims=False)
            if i in in_chip_order:
                by_chip = _by_chip(mine, f)
                mine, f = by_chip[0], jnp.stack(by_chip[1:])
            halves.append(_add4(mine, f, f"reduce_add4_{i}"))
        got = _join_halves(halves, "reduce_join")
        return [jnp.where(c == 0, jnp.concatenate([h, g], axis=0), jnp.concatenate([g, h], axis=0)) for h, g in zip(halves, got)]
```

```python
import functools

import jax
import jax.numpy as jnp
from jax import lax
from jax.experimental import pallas as pl
from jax.experimental.pallas import tpu as pltpu

F32 = jnp.float32
BF = jnp.bfloat16
HI = lax.Precision.HIGHEST
MESH = pl.DeviceIdType.MESH

D = 1024
N_META = 16
CHUNK = 128
EPS = 1e-6
NH = 4
DV = 256
DQK = 128
G_RANK = 16
G_TAU = 16.0
D_FF = 2816
TM = 512
FIRST_VALID = TM - N_META
CPB = TM // CHUNK
G_CHUNK = 256
G_CPB = TM // G_CHUNK
NEG = -1e30
N_BIG = 8192
CB_GQK, CB_GV, CB_GR, CB_MQK, CB_GM, CB_GG, CB_MV, CB_MO = range(8)
N_SMALL = 128
N_ALL = N_BIG + N_SMALL
VMEM_LIMIT = 56 * 1024 * 1024

ADAM_LR, ADAM_B1, ADAM_B2, ADAM_EPS, ADAM_WD, ADAM_STEP = 0.001, 0.9, 0.999, 1e-08, 0.01, 10

NT_DIMS = (((1,), (1,)), ((), ()))
TN_DIMS = (((0,), (0,)), ((), ()))


def _nt(a, b, **kw):
    return lax.dot_general(a, b, NT_DIMS, preferred_element_type=F32, **kw)


def _tn(a, b, **kw):
    return lax.dot_general(a, b, TN_DIMS, preferred_element_type=F32, **kw)


def _nn(a, b, **kw):
    return jnp.dot(a, b, preferred_element_type=F32, **kw)


def _params(**kw):
    return pltpu.CompilerParams(vmem_limit_bytes=VMEM_LIMIT, **kw)


def _sigmoid(x):
    return 0.5 * jnp.tanh(0.5 * x) + 0.5


def _logsig(x):
    return jnp.minimum(x, 0.0) - jnp.log(1.0 + jnp.exp(-jnp.abs(x)))


def _mm_rows(rows):
    return 3 * TM if rows % (3 * TM) == 0 else TM


def _mm(a, b, *, nt, out_dtype, tn, tk=None, tm=None, name, rider=None):
    m, k = a.shape
    n = b.shape[0] if nt else b.shape[1]
    tk = k if tk is None else tk
    tm = _mm_rows(m) if tm is None else tm
    nk = k // tk
    nj, ni = n // tn, m // tm
    nr_in = len(rider["inputs"]) if rider else 0
    nr_out = len(rider["out_shapes"]) if rider else 0
    assert m % tm == 0 and n % tn == 0 and k % tk == 0
    dims = NT_DIMS if nt else (((1,), (0,)), ((), ()))

    def body(*refs):
        a_ref, b_ref = refs[:2]
        o_ref = refs[2 + nr_in]
        j, i, kk = pl.program_id(0), pl.program_id(1), pl.program_id(2)
        step = (j * ni + i) * nk + kk
        if rider:
            start, middle, finish = rider["make"](refs[2:2 + nr_in], refs[3 + nr_in:3 + nr_in + nr_out],
                                                  refs[3 + nr_in + nr_out:5 + nr_in + nr_out])
            pl.when(step == 0)(start)
            pl.when(step == (nj * ni * nk) // 2)(middle)

        part = lax.dot_general(a_ref[...].astype(BF), b_ref[...].astype(BF), dims, preferred_element_type=F32)
        if nk == 1:
            o_ref[...] = part.astype(o_ref.dtype)
        else:
            acc_ref = refs[-1]

            @pl.when(kk == 0)
            def _():
                acc_ref[...] = part

            @pl.when(jnp.logical_and(kk > 0, kk < nk - 1))
            def _():
                acc_ref[...] += part

            @pl.when(kk == nk - 1)
            def _():
                o_ref[...] = (acc_ref[...] + part).astype(o_ref.dtype)

        if rider:
            pl.when(step == nj * ni * nk - 1)(finish)

    outs = pl.pallas_call(
        body, grid=(nj, ni, nk),
        in_specs=[pl.BlockSpec((tm, tk), lambda j, i, kk: (i, kk)),
                  pl.BlockSpec((tn, tk), lambda j, i, kk: (j, kk)) if nt else pl.BlockSpec((tk, tn), lambda j, i, kk: (kk, j))]
                 + [ANY] * nr_in,
        out_specs=[pl.BlockSpec((tm, tn), lambda j, i, kk: (i, j))] + [ANY] * nr_out,
        out_shape=(jax.ShapeDtypeStruct((m, n), out_dtype),) + (tuple(rider["out_shapes"]) if rider else ()),
        scratch_shapes=(rider["sems"] if rider else []) + ([pltpu.VMEM((tm, tn), F32)] if nk > 1 else []),
        compiler_params=_params(), name=name)(a, b, *(rider["inputs"] if rider else []))
    return (outs[0], list(outs[1:])) if rider else outs[0]


def _mm_tn(a, b, *, tm, tn, tk=None, name):
    t, m = a.shape
    n = b.shape[1]
    tk = _mm_rows(t) if tk is None else tk
    assert t % tk == 0 and m % tm == 0 and n % tn == 0

    def body(a_ref, b_ref, o_ref):
        part = _tn(a_ref[...].astype(BF), b_ref[...].astype(BF))

        @pl.when(pl.program_id(2) == 0)
        def _():
            o_ref[...] = part

        @pl.when(pl.program_id(2) > 0)
        def _():
            o_ref[...] += part

    return pl.pallas_call(
        body, grid=(m // tm, n // tn, t // tk),
        in_specs=[pl.BlockSpec((tk, tm), lambda i, j, kk: (kk, i)), pl.BlockSpec((tk, tn), lambda i, j, kk: (kk, j))],
        out_specs=pl.BlockSpec((tm, tn), lambda i, j, kk: (i, j)),
        out_shape=jax.ShapeDtypeStruct((m, n), F32), compiler_params=_params(), name=name)(a, b)


ANY = pl.BlockSpec(memory_space=pl.ANY)


def _row_spec(width, col=0):
    return pl.BlockSpec((TM, width), lambda i: (i, col))


def _full_spec(shape):
    return pl.BlockSpec(shape, lambda i: (0,) * len(shape))


def _rms_fwd(h, g, name):
    tp = h.shape[0]

    def body(h_ref, g_ref, xn_ref, r_ref):
        x = h_ref[...]
        r = lax.rsqrt(jnp.mean(x * x, axis=1, keepdims=True) + EPS)
        xn_ref[...] = (x * r * g_ref[...]).astype(BF)
        r_ref[...] = r

    return pl.pallas_call(
        body, grid=(tp // TM,), in_specs=[_row_spec(D), _full_spec((1, D))],
        out_specs=[_row_spec(D), _row_spec(1)],
        out_shape=(jax.ShapeDtypeStruct((tp, D), BF), jax.ShapeDtypeStruct((tp, 1), F32)),
        compiler_params=_params(), name=name)(h, g)


def _rms_bwd(dxn, h, rstd, g, dres, name, split_first=False):
    tp = h.shape[0]

    def body(dxn_ref, h_ref, r_ref, g_ref, dres_ref, *outs):
        r = r_ref[...]
        xh = h_ref[...] * r
        dxn_v = dxn_ref[...].astype(F32)
        dxh = dxn_v * g_ref[...]
        dh = r * (dxh - xh * jnp.mean(dxh * xh, axis=1, keepdims=True)) + dres_ref[...]
        if split_first:
            first_ref, dh_ref, dg_ref = outs

            @pl.when(pl.program_id(0) == 0)
            def _():
                first_ref[...] = dh
        else:
            dh_ref, dg_ref = outs
        dh_ref[...] = dh
        part = jnp.sum(dxn_v * xh, axis=0, keepdims=True)

        @pl.when(pl.program_id(0) == 0)
        def _():
            dg_ref[...] = part

        @pl.when(pl.program_id(0) > 0)
        def _():
            dg_ref[...] += part

    if split_first:
        out_specs = [_full_spec((TM, D)), pl.BlockSpec((TM, D), lambda i: (jnp.maximum(i - 1, 0), 0)), _full_spec((1, D))]
        out_shape = (jax.ShapeDtypeStruct((TM, D), F32), jax.ShapeDtypeStruct((tp - TM, D), F32), jax.ShapeDtypeStruct((1, D), F32))
    else:
        out_specs = [_row_spec(D), _full_spec((1, D))]
        out_shape = (jax.ShapeDtypeStruct((tp, D), F32), jax.ShapeDtypeStruct((1, D), F32))
    return pl.pallas_call(
        body, grid=(tp // TM,),
        in_specs=[_row_spec(D), _row_spec(D), _row_spec(1), _full_spec((1, D)), _row_spec(D)],
        out_specs=out_specs, out_shape=out_shape, compiler_params=_params(), name=name)(dxn, h, rstd, g, dres)


def _shift_down(x, halo, k):
    rk = pltpu.roll(x, k, 0)
    io = lax.broadcasted_iota(jnp.int32, (8, x.shape[1]), 0)
    top = jnp.where(io < k, pltpu.roll(halo, k, 0), rk[0:8])
    return jnp.concatenate([top, rk[8:]], axis=0)


def _shift_up(x, nxt, k):
    n = x.shape[0]
    rk = pltpu.roll(x, n - k, 0)
    io = lax.broadcasted_iota(jnp.int32, (8, x.shape[1]), 0)
    bot = jnp.where(io >= 8 - k, pltpu.roll(nxt, 8 - k, 0), rk[n - 8:n])
    return jnp.concatenate([rk[:n - 8], bot], axis=0)


def _conv_pre(x, halo, w_ref, b_ref):
    c = x * w_ref[3:4, :] + b_ref[...]
    shifted = []
    for k in (1, 2, 3):
        s = _shift_down(x, halo, k)
        shifted.append(s)
        c = c + s * w_ref[3 - k:4 - k, :]
    return c, shifted


def _qk_scale():
    col = lax.broadcasted_iota(jnp.int32, (1, D), 1)
    return jnp.where(col < NH * DQK, DQK ** -0.5, 1.0).astype(F32)


def _halo_prev_spec():
    return pl.BlockSpec((8, D), lambda i: (jnp.maximum(i * (TM // 8) - 1, 0), CB_MQK))


def _conv_fwd(pbig, w, b, name):
    tp = pbig.shape[0]

    def body(x_ref, halo_ref, w_ref, b_ref, o_ref):
        x = x_ref[...].astype(F32)
        halo = jnp.where(pl.program_id(0) > 0, halo_ref[...].astype(F32), 0.0)
        c, _ = _conv_pre(x, halo, w_ref, b_ref)
        o_ref[...] = (c * _sigmoid(c) * _qk_scale()).astype(BF)

    return pl.pallas_call(
        body, grid=(tp // TM,),
        in_specs=[_row_spec(D, CB_MQK), _halo_prev_spec(), _full_spec((4, D)), _full_spec((1, D))],
        out_specs=_row_spec(D), out_shape=jax.ShapeDtypeStruct((tp, D), BF),
        compiler_params=_params(), name=name)(pbig, pbig, w, b)


def _conv_bwd_pre(dqk, pbig, w, b, name):
    tp = pbig.shape[0]

    def body(d_ref, x_ref, halo_ref, w_ref, b_ref, dc_ref, dwb_ref):
        x = x_ref[...].astype(F32)
        halo = jnp.where(pl.program_id(0) > 0, halo_ref[...].astype(F32), 0.0)
        c, shifted = _conv_pre(x, halo, w_ref, b_ref)
        sg = _sigmoid(c)
        dc = d_ref[...] * _qk_scale() * (sg * (1.0 + c * (1.0 - sg)))
        dc_ref[...] = dc
        taps = [shifted[2], shifted[1], shifted[0], x]
        rows = [jnp.sum(dc * t, axis=0, keepdims=True) for t in taps] + [jnp.sum(dc, axis=0, keepdims=True)]
        io = lax.broadcasted_iota(jnp.int32, (8, D), 0)
        part = jnp.zeros((8, D), F32)
        for r, v in enumerate(rows):
            part = jnp.where(io == r, v, part)

        @pl.when(pl.program_id(0) == 0)
        def _():
            dwb_ref[...] = part

        @pl.when(pl.program_id(0) > 0)
        def _():
            dwb_ref[...] += part

    return pl.pallas_call(
        body, grid=(tp // TM,),
        in_specs=[_row_spec(D), _row_spec(D, CB_MQK), _halo_prev_spec(), _full_spec((4, D)), _full_spec((1, D))],
        out_specs=[_row_spec(D), _full_spec((8, D))],
        out_shape=(jax.ShapeDtypeStruct((tp, D), F32), jax.ShapeDtypeStruct((8, D), F32)),
        compiler_params=_params(), name=name)(dqk, pbig, pbig, w, b)


def _conv_bwd_in(dc, w, dproj, name):
    tp = dc.shape[0]
    nb = tp // TM

    def body(d_ref, nxt_ref, w_ref, _, o_ref):
        d = d_ref[...]
        nxt = jnp.where(pl.program_id(0) < nb - 1, nxt_ref[...], 0.0)
        acc = d * w_ref[3:4, :]
        for k in (1, 2, 3):
            acc = acc + _shift_up(d, nxt, k) * w_ref[3 - k:4 - k, :]
        o_ref[...] = acc.astype(BF)

    return pl.pallas_call(
        body, grid=(nb,),
        in_specs=[_row_spec(D), pl.BlockSpec((8, D), lambda i: (jnp.minimum((i + 1) * (TM // 8), tp // 8 - 1), 0)),
                  _full_spec((4, D)), ANY],
        out_specs=_row_spec(D, CB_MQK), out_shape=jax.ShapeDtypeStruct(dproj.shape, BF),
        input_output_aliases={3: 0}, compiler_params=_params(), name=name)(dc, dc, w, dproj)


def _mm_fused(inputs, products, *, nt, m, n, tm, tn, outs, epilogue, name, nk=1, sub=None):
    dims = NT_DIMS if nt else (((1,), (0,)), ((), ()))
    nin = len(inputs)
    assert nk == 1 or (len(products) == 1 and sub is None)

    def body(*refs):
        in_refs, out_refs = refs[:nin], refs[nin:nin + len(outs)]
        i = pl.program_id(1)
        if sub is not None:
            lhs = {ia: in_refs[ia][...].astype(BF) for ia, _ in products}

            def dots(cols):
                return [lax.dot_general(lhs[ia], (in_refs[ib][cols, :] if nt else in_refs[ib][:, cols]).astype(BF),
                                        dims, preferred_element_type=F32) for ia, ib in products]

            slices = [slice(s, min(s + sub, tn)) for s in range(0, tn, sub)]
            prods = dots(slices[0])
            for idx, cols in enumerate(slices):
                nxt = dots(slices[idx + 1]) if idx + 1 < len(slices) else None
                epilogue(prods, in_refs, out_refs, i, cols)
                prods = nxt
            return
        prods = [lax.dot_general(in_refs[ia][...].astype(BF), in_refs[ib][...].astype(BF), dims, preferred_element_type=F32)
                 for ia, ib in products]
        if nk == 1:
            epilogue(prods, in_refs, out_refs, i, slice(None))
            return
        acc_ref = refs[-1]
        kk = pl.program_id(2)

        @pl.when(kk == 0)
        def _():
            acc_ref[...] = prods[0]

        @pl.when(jnp.logical_and(kk > 0, kk < nk - 1))
        def _():
            acc_ref[...] += prods[0]

        @pl.when(kk == nk - 1)
        def _():
            epilogue([acc_ref[...] + prods[0]], in_refs, out_refs, i, slice(None))

    return pl.pallas_call(
        body, grid=(n // tn, m // tm, nk), in_specs=[s for _, s in inputs], out_specs=[s for _, s in outs],
        out_shape=tuple(sh for sh, _ in outs), scratch_shapes=[pltpu.VMEM((tm, tn), F32)] if nk > 1 else [],
        compiler_params=_params(), name=name)(*[a for a, _ in inputs])


SUB_COLS = 256


def _cols_at(cols, offset):
    return slice(cols.start + offset, cols.stop + offset)


def _blk(rows, width, col=None, row=None):
    return pl.BlockSpec((rows, width), lambda j, i, kk: ((i if row is None else row(i)), (0 if col is None else col(j, kk))))


FF_TN = D_FF // 2


def _ffn_weight_rows(wg_t, wu_t):
    return jnp.concatenate([wg_t[0:FF_TN], wu_t[0:FF_TN], wg_t[FF_TN:], wu_t[FF_TN:]], axis=0)


def _ffn_in(hn, wgu_t, name):
    tp = hn.shape[0]
    tm = _mm_rows(tp)

    def epilogue(prods, in_refs, out_refs, i, cols):
        g, u = prods
        out_refs[0][:, cols] = g.astype(BF)
        out_refs[0][:, _cols_at(cols, FF_TN)] = u.astype(BF)
        out_refs[1][:, cols] = (g * _sigmoid(g) * u).astype(BF)

    wspec = lambda off: pl.BlockSpec((FF_TN, D), lambda j, i, kk: (2 * j + off, 0))
    return _mm_fused(
        [(hn, _blk(tm, D)), (wgu_t, wspec(0)), (wgu_t, wspec(1))], [(0, 1), (0, 2)], nt=True, m=tp, n=D_FF, tm=tm, tn=FF_TN,
        outs=[(jax.ShapeDtypeStruct((tp, 2 * D_FF), BF), _blk(tm, 2 * FF_TN, lambda j, kk: j)),
              (jax.ShapeDtypeStruct((tp, D_FF), BF), _blk(tm, FF_TN, lambda j, kk: j))],
        epilogue=epilogue, name=name, sub=SUB_COLS)


def _ffn_down_loss(ff, wdown, h1, target, gf, name):
    tp = ff.shape[0]

    def epilogue(prods, in_refs, out_refs, i, cols):
        live = (i > 0).astype(F32)
        g = in_refs[4][...]
        x = prods[0] + in_refs[2][...]
        r = lax.rsqrt(jnp.mean(x * x, axis=1, keepdims=True) + EPS)
        xh = x * r
        e = xh * g - in_refs[3][...]
        loss_part = 0.5 * live * jnp.sum(jnp.mean(e * e, axis=1, keepdims=True), axis=0, keepdims=True)
        dout = e * (live / D)
        dg_part = jnp.sum(dout * xh, axis=0, keepdims=True)
        dxh = dout * g
        out_refs[0][...] = r * (dxh - xh * jnp.mean(dxh * xh, axis=1, keepdims=True))

        @pl.when(i == 0)
        def _():
            out_refs[1][...] = loss_part
            out_refs[2][...] = dg_part

        @pl.when(i > 0)
        def _():
            out_refs[1][...] += loss_part
            out_refs[2][...] += dg_part

    const = lambda shape: pl.BlockSpec(shape, lambda j, i, kk: (0,) * len(shape))
    return _mm_fused(
        [(ff, _blk(TM, D_FF)), (wdown, const((D_FF, D))), (h1, _blk(TM, D)),
         (target, _blk(TM, D, row=lambda i: jnp.maximum(i - 1, 0))), (gf, const((1, D)))],
        [(0, 1)], nt=False, m=tp, n=D, tm=TM, tn=D,
        outs=[(jax.ShapeDtypeStruct((tp, D), F32), _blk(TM, D)), (jax.ShapeDtypeStruct((1, 1), F32), const((1, 1))),
              (jax.ShapeDtypeStruct((1, D), F32), const((1, D)))],
        epilogue=epilogue, name=name)


def _ffn_d_hidden(dh2, wdown, gu, name):
    tp = dh2.shape[0]

    def epilogue(prods, in_refs, out_refs, i, cols):
        d = prods[0]
        g = in_refs[2][:, cols].astype(F32)
        u = in_refs[2][:, _cols_at(cols, FF_TN)].astype(F32)
        sg = _sigmoid(g)
        out_refs[0][:, cols] = (d * u * sg * (1.0 + g * (1.0 - sg))).astype(BF)
        out_refs[0][:, _cols_at(cols, FF_TN)] = (d * g * sg).astype(BF)

    return _mm_fused(
        [(dh2, _blk(TM, D)), (wdown, pl.BlockSpec((FF_TN, D), lambda j, i, kk: (j, 0))), (gu, _blk(TM, 2 * FF_TN, lambda j, kk: j))],
        [(0, 1)], nt=True, m=tp, n=D_FF, tm=TM, tn=FF_TN,
        outs=[(jax.ShapeDtypeStruct((tp, 2 * D_FF), BF), _blk(TM, 2 * FF_TN, lambda j, kk: j))],
        epilogue=epilogue, name=name, sub=SUB_COLS)[0]


def _ffn_d_in(dgu, wgu_t, h1, rstd, g2, dh2, name):
    tp = dgu.shape[0]
    nk = 2

    def epilogue(prods, in_refs, out_refs, i, cols):
        r = in_refs[3][...]
        xh = in_refs[2][...] * r
        dxn = prods[0]
        dxh = dxn * in_refs[4][...]
        out_refs[0][...] = r * (dxh - xh * jnp.mean(dxh * xh, axis=1, keepdims=True)) + in_refs[5][...]
        part = jnp.sum(dxn * xh, axis=0, keepdims=True)

        @pl.when(i == 0)
        def _():
            out_refs[1][...] = part

        @pl.when(i > 0)
        def _():
            out_refs[1][...] += part

    const = lambda shape: pl.BlockSpec(shape, lambda j, i, kk: (0,) * len(shape))
    return _mm_fused(
        [(dgu, pl.BlockSpec((TM, D_FF), lambda j, i, kk: (i, kk))), (wgu_t, pl.BlockSpec((D_FF, D), lambda j, i, kk: (kk, 0))),
         (h1, _blk(TM, D)), (rstd, _blk(TM, 1)), (g2, const((1, D))), (dh2, _blk(TM, D))],
        [(0, 1)], nt=False, m=tp, n=D, tm=TM, tn=D, nk=nk,
        outs=[(jax.ShapeDtypeStruct((tp, D), F32), _blk(TM, D)), (jax.ShapeDtypeStruct((1, D), F32), const((1, D)))],
        epilogue=epilogue, name=name)


def _branch_merge(y_m, y_g, wbm, wbg, pbig, name):
    tp = y_m.shape[0]

    def epilogue(prods, in_refs, out_refs, i, cols):
        pm, pg = prods[0].astype(BF), prods[1].astype(BF)
        out_refs[0][:, cols] = pm
        out_refs[1][:, cols] = pg
        out_refs[2][:, cols] = (_sigmoid(in_refs[4][:, cols].astype(F32)) * pm.astype(F32)
                                + _sigmoid(in_refs[5][:, cols].astype(F32)) * pg.astype(F32)).astype(BF)

    const = lambda shape: pl.BlockSpec(shape, lambda j, i, kk: (0,) * len(shape))
    shp = jax.ShapeDtypeStruct((tp, D), BF)
    return _mm_fused(
        [(y_m, _blk(TM, D)), (wbm, const((D, D))), (y_g, _blk(TM, D)), (wbg, const((D, D))),
         (pbig, _blk(TM, D, lambda j, kk: CB_GM)), (pbig, _blk(TM, D, lambda j, kk: CB_GG))],
        [(0, 1), (2, 3)], nt=False, m=tp, n=D, tm=TM, tn=D,
        outs=[(shp, _blk(TM, D)), (shp, _blk(TM, D)), (shp, _blk(TM, D))], epilogue=epilogue, name=name, sub=SUB_COLS)


def _merge_d(dh1, wout, pm, pg, pbig, name):
    tp = dh1.shape[0]

    def epilogue(prods, in_refs, out_refs, i, cols):
        d = prods[0]
        sm = _sigmoid(in_refs[4][:, cols].astype(F32))
        sg = _sigmoid(in_refs[5][:, cols].astype(F32))
        out_refs[0][:, cols] = (d * sm).astype(BF)
        out_refs[1][:, cols] = (d * sg).astype(BF)
        out_refs[2][:, cols] = (d * in_refs[2][:, cols].astype(F32) * sm * (1.0 - sm)).astype(BF)
        out_refs[2][:, _cols_at(cols, D)] = (d * in_refs[3][:, cols].astype(F32) * sg * (1.0 - sg)).astype(BF)

    const = lambda shape: pl.BlockSpec(shape, lambda j, i, kk: (0,) * len(shape))
    shp = jax.ShapeDtypeStruct((tp, D), BF)
    return _mm_fused(
        [(dh1, _blk(TM, D)), (wout, const((D, D))), (pm, _blk(TM, D)), (pg, _blk(TM, D)),
         (pbig, _blk(TM, D, lambda j, kk: CB_GM)), (pbig, _blk(TM, D, lambda j, kk: CB_GG))],
        [(0, 1)], nt=True, m=tp, n=D, tm=TM, tn=D,
        outs=[(shp, _blk(TM, D)), (shp, _blk(TM, D)),
              (jax.ShapeDtypeStruct((tp, N_ALL), BF), _blk(TM, 2 * D, lambda j, kk: CB_GM // 2))],
        epilogue=epilogue, name=name, sub=SUB_COLS)


def _out_proj_norm(merged, wout, h0, g2, name):
    tp = merged.shape[0]
    tm = _mm_rows(tp)

    def epilogue(prods, in_refs, out_refs, i, cols):
        x = prods[0] + in_refs[2][...]
        r = lax.rsqrt(jnp.mean(x * x, axis=1, keepdims=True) + EPS)
        out_refs[0][...] = x
        out_refs[1][...] = (x * r * in_refs[3][...]).astype(BF)
        out_refs[2][...] = r

    const = lambda shape: pl.BlockSpec(shape, lambda j, i, kk: (0,) * len(shape))
    return _mm_fused(
        [(merged, _blk(tm, D)), (wout, const((D, D))), (h0, _blk(tm, D)), (g2, const((1, D)))],
        [(0, 1)], nt=False, m=tp, n=D, tm=tm, tn=D,
        outs=[(jax.ShapeDtypeStruct((tp, D), F32), _blk(tm, D)), (jax.ShapeDtypeStruct((tp, D), BF), _blk(tm, D)),
              (jax.ShapeDtypeStruct((tp, 1), F32), _blk(tm, 1))],
        epilogue=epilogue, name=name)


def _adamw(w, g, m, v, name):
    rows, cols = w.shape
    by_cols = rows % 128 != 0 and cols % 128 == 0 and rows * cols > 128 * 1024
    tr = rows if (by_cols or rows % 128 != 0) else 128
    tc = 128 if by_cols else cols

    def body(w_ref, g_ref, m_ref, v_ref, d_ref, nm_ref, nv_ref):
        gv = g_ref[...]
        nm = ADAM_B1 * m_ref[...] + (1.0 - ADAM_B1) * gv
        nv = ADAM_B2 * v_ref[...] + (1.0 - ADAM_B2) * (gv * gv)
        m_hat = nm / (1.0 - ADAM_B1 ** ADAM_STEP)
        v_hat = nv / (1.0 - ADAM_B2 ** ADAM_STEP)
        d_ref[...] = -ADAM_LR * (m_hat / (jnp.sqrt(v_hat) + ADAM_EPS) + ADAM_WD * w_ref[...])
        nm_ref[...] = nm
        nv_ref[...] = nv

    spec = pl.BlockSpec((tr, tc), (lambda i: (0, i)) if by_cols else (lambda i: (i, 0)))
    shp = jax.ShapeDtypeStruct((rows, cols), F32)
    return pl.pallas_call(body, grid=(cols // tc if by_cols else rows // tr,), in_specs=[spec] * 4, out_specs=[spec] * 3,
                          out_shape=(shp,) * 3, compiler_params=_params(), name=name)(w, g, m, v)


def _place_small(dsmall, dproj, name):
    tp = dsmall.shape[0]

    def body(s_ref, _, o_ref):
        o_ref[...] = s_ref[...]

    return pl.pallas_call(
        body, grid=(tp // TM,), in_specs=[_row_spec(N_SMALL), ANY], out_specs=_row_spec(N_SMALL, N_BIG // N_SMALL),
        out_shape=jax.ShapeDtypeStruct(dproj.shape, dproj.dtype), input_output_aliases={1: 0},
        compiler_params=_params(), name=name)(dsmall, dproj)


def _row_tile(rows, cap=512):
    best = rows
    for cand in range(8, min(rows, cap) + 1, 8):
        if rows % cand == 0:
            best = cand
    return best


def _add2(a, b, out_dtype, name):
    rows, cols = a.shape
    tr = _row_tile(rows)

    def body(a_ref, b_ref, o_ref):
        o_ref[...] = (a_ref[...] + b_ref[...]).astype(o_ref.dtype)

    spec = pl.BlockSpec((tr, cols), lambda i: (i, 0))
    return pl.pallas_call(body, grid=(rows // tr,), in_specs=[spec] * 2, out_specs=spec,
                          out_shape=jax.ShapeDtypeStruct((rows, cols), out_dtype), compiler_params=_params(), name=name)(a, b)


def _add4(first, rest, name):
    rows, cols = first.shape
    tr = _row_tile(rows, 256)

    def body(f_ref, r_ref, o_ref):
        up = lambda v: v.astype(F32)
        o_ref[...] = ((up(f_ref[...]) + up(r_ref[0])) + up(r_ref[1])) + up(r_ref[2])

    return pl.pallas_call(body, grid=(rows // tr,),
                          in_specs=[pl.BlockSpec((tr, cols), lambda i: (i, 0)), pl.BlockSpec((3, tr, cols), lambda i: (0, i, 0))],
                          out_specs=pl.BlockSpec((tr, cols), lambda i: (i, 0)),
                          out_shape=jax.ShapeDtypeStruct((rows, cols), F32), compiler_params=_params(), name=name)(first, rest)


def _chunk_consts(length=CHUNK):
    r2 = lax.broadcasted_iota(jnp.int32, (length, length), 0)
    c2 = lax.broadcasted_iota(jnp.int32, (length, length), 1)
    tri = r2 >= c2
    return dict(tri=tri, tril_f=tri.astype(F32), triu_f=(r2 <= c2).astype(F32),
                lane=lax.broadcasted_iota(jnp.int32, (length, N_SMALL), 1),
                rowio=lax.broadcasted_iota(jnp.int32, (length, 1), 0),
                ones=jnp.ones((length, N_SMALL), F32))


def _valid_rows(block, c):
    row = block * TM + c * CHUNK + lax.broadcasted_iota(jnp.int32, (CHUNK, 1), 0)
    return row >= FIRST_VALID


def _col(x, lane, idx):
    return jnp.sum(jnp.where(lane == idx, x, 0.0), axis=1, keepdims=True)


def _last_row(x, rowio):
    return jnp.sum(jnp.where(rowio == rowio.shape[0] - 1, x, 0.0), axis=0, keepdims=True)


def _sum_all(x):
    return jnp.sum(jnp.sum(x, axis=1, keepdims=True), axis=0, keepdims=True)


def _headnorm_fwd(hm, gain, gate_act):
    rs = lax.rsqrt(jnp.mean(hm * hm, axis=1, keepdims=True) + EPS)
    return hm * rs * gain * gate_act


def _headnorm_bwd(dy, hm, gain, gate_act):
    rs = lax.rsqrt(jnp.mean(hm * hm, axis=1, keepdims=True) + EPS)
    xh = hm * rs
    dact = dy * xh * gain
    dgain = jnp.sum(dy * gate_act * xh, axis=0, keepdims=True)
    dxh = dy * gate_act * gain
    dhm = rs * (dxh - xh * jnp.mean(dxh * xh, axis=1, keepdims=True))
    return dhm, dact, dgain


def _mlstm_gates(sm, gbias, valid, k):
    pre = sm + gbias
    lf = jnp.where(valid, _logsig(pre), 0.0)
    b_all = _nn(k["tril_f"], lf, precision=HI)
    li_all = jnp.where(valid, pre, NEG)
    return pre, li_all, b_all


def _mlstm_open(h, qh, kh, c_st, li_all, b_all, k):
    lane = k["lane"]
    sel = jnp.where(lane == h, 1.0, 0.0) - jnp.where(lane == NH + h, 1.0, 0.0)
    x = jnp.where(lane < NH, li_all, jnp.where(lane < 2 * NH, b_all, 0.0))
    cb = c_st.astype(BF)
    return dict(ubc=_nt(sel, x, precision=HI), sim=_nt(qh, kh), cb=cb, cq=_nt(qh, cb))


def _mlstm_weights(h, f, qh, vh, li_all, b_all, n_row, m11, k):
    lane, tri, rowio = k["lane"], k["tri"], k["rowio"]
    b_col = _col(b_all, lane, NH + h)
    li_col = _col(li_all, lane, h)
    dmat = jnp.where(tri, b_col + f["ubc"], NEG)
    m_row = jnp.maximum(b_col + m11, jnp.max(dmat, axis=1, keepdims=True))
    e = jnp.exp(dmat - m_row)
    w_mat = e * f["sim"]
    a = jnp.exp(b_col + m11 - m_row)
    qf = qh.astype(F32)
    nq = jnp.sum(qf * n_row, axis=1, keepdims=True)
    g = _last_row(b_col, rowio)
    wlog = g - b_col + li_col
    m_new = jnp.maximum(g + m11, jnp.max(wlog, axis=0, keepdims=True))
    a_s = jnp.exp(g + m11 - m_new)
    w = jnp.exp(wlog - m_new)
    return dict(f, e=e, w_mat=w_mat, a=a, qf=qf, nq=nq, m_row=m_row, m_new=m_new, a_s=a_s, w=w,
                wv=_nn(w_mat.astype(BF), vh))


def _mlstm_out(f):
    num = f["a"] * f["cq"] + f["wv"]
    den = f["a"] * f["nq"] + jnp.sum(f["w_mat"], axis=1, keepdims=True)
    floor = jnp.exp(-f["m_row"])
    r = jnp.maximum(jnp.abs(den), floor)
    return dict(f, den=den, floor=floor, r=r, hm=num / r)


def _mlstm_fwd(qk, pbig, small, gbias, headg, name):
    tp = qk.shape[0]
    nb = tp // TM

    def body(qk_ref, v_ref, mo_ref, sm_ref, gb_ref, hg_ref, y_ref, cs_ref, ns_ref, c_scr, n_scr):
        blk = pl.program_id(0)

        @pl.when(blk == 0)
        def _():
            c_scr[...] = jnp.zeros_like(c_scr)
            n_scr[...] = jnp.zeros_like(n_scr)

        k = _chunk_consts()
        io8 = lax.broadcasted_iota(jnp.int32, (8, DQK), 0)

        def chunk(c, carry):
            r0 = pl.multiple_of(c * CHUNK, CHUNK)
            rows = pl.ds(r0, CHUNK)
            valid = _valid_rows(blk, c)
            _, li_all, b_all = _mlstm_gates(sm_ref[rows, :], gb_ref[...], valid, k)
            heads = range(NH)
            qs = [qk_ref[rows, h * DQK:(h + 1) * DQK] for h in heads]
            ks = [qk_ref[rows, NH * DQK + h * DQK:NH * DQK + (h + 1) * DQK] for h in heads]
            vs = [v_ref[rows, h * DV:(h + 1) * DV] for h in heads]
            cst = [c_scr[h] for h in heads]
            nrow = [n_scr[h, 0:1, :] for h in heads]
            m11 = [jnp.max(n_scr[h, 1:2, :], axis=1, keepdims=True) for h in heads]
            f = [_mlstm_open(h, qs[h], ks[h], cst[h], li_all, b_all, k) for h in heads]
            f = [_mlstm_weights(h, f[h], qs[h], vs[h], li_all, b_all, nrow[h], m11[h], k) for h in heads]
            wk = [f[h]["w"] * ks[h].astype(F32) for h in heads]
            kv = [_tn(vs[h], wk[h].astype(BF)) for h in heads]
            for h in heads:
                hm = _mlstm_out(f[h])["hm"]
                gate = _sigmoid(mo_ref[rows, h * DV:(h + 1) * DV].astype(F32))
                y_ref[rows, h * DV:(h + 1) * DV] = _headnorm_fwd(hm, hg_ref[:, h * DV:(h + 1) * DV], gate).astype(BF)
                cs_ref[c, h] = f[h]["cb"]
                ns_ref[c, h] = jnp.where(io8 == 0, nrow[h], jnp.where(io8 == 1, m11[h], 0.0))
                c_scr[h] = f[h]["a_s"] * cst[h] + kv[h]
                n_scr[h, 0:1, :] = f[h]["a_s"] * nrow[h] + jnp.sum(wk[h], axis=0, keepdims=True)
                n_scr[h, 1:2, :] = jnp.broadcast_to(f[h]["m_new"], (1, DQK))
            return carry

        lax.fori_loop(0, CPB, chunk, 0, unroll=2)

    return pl.pallas_call(
        body, grid=(nb,),
        in_specs=[_row_spec(D), _row_spec(D, CB_MV), _row_spec(D, CB_MO), _row_spec(N_SMALL), _full_spec((1, N_SMALL)), _full_spec((1, D))],
        out_specs=[_row_spec(D), pl.BlockSpec((CPB, NH, DV, DQK), lambda i: (i, 0, 0, 0)),
                   pl.BlockSpec((CPB, NH, 8, DQK), lambda i: (i, 0, 0, 0))],
        out_shape=(jax.ShapeDtypeStruct((tp, D), BF), jax.ShapeDtypeStruct((tp // CHUNK, NH, DV, DQK), BF),
                   jax.ShapeDtypeStruct((tp // CHUNK, NH, 8, DQK), F32)),
        scratch_shapes=[pltpu.VMEM((NH, DV, DQK), F32), pltpu.VMEM((NH, 8, DQK), F32)],
        compiler_params=_params(), name=name)(qk, pbig, pbig, small, gbias, headg)


def _mlstm_bwd(dy, qk, pbig, small, gbias, headg, cs, ns, dproj, name, ride=()):
    tp = qk.shape[0]
    nb = tp // TM
    nr = len(ride)

    def body(*refs):
        dy_ref, qk_ref, v_ref, mo_ref, sm_ref, gb_ref, hg_ref, cs_ref, ns_ref = refs[:9]
        ride_in = refs[10:10 + nr]
        dqk_ref, dproj_ref, dsm_ref, dgb_ref, dhg_ref = refs[10 + nr:15 + nr]
        ride_out = refs[15 + nr:15 + 2 * nr]
        dc_scr, dn_scr = refs[15 + 2 * nr:17 + 2 * nr]
        step = pl.program_id(0)
        blk = nb - 1 - step
        sent = _scatter_copies(ride_in, ride_out, *refs[17 + 2 * nr:]) if nr else []

        @pl.when(step == 0)
        def _():
            dc_scr[...] = jnp.zeros_like(dc_scr)
            dn_scr[...] = jnp.zeros_like(dn_scr)
            dgb_ref[...] = jnp.zeros_like(dgb_ref)
            dhg_ref[...] = jnp.zeros_like(dhg_ref)
            for cp in sent:
                cp.start()

        k = _chunk_consts()
        lane, rowio = k["lane"], k["rowio"]

        def chunk(cc, carry):
            c = CPB - 1 - cc
            r0 = pl.multiple_of(c * CHUNK, CHUNK)
            rows = pl.ds(r0, CHUNK)
            valid = _valid_rows(blk, c)
            pre, li_all, b_all = _mlstm_gates(sm_ref[rows, :], gb_ref[...], valid, k)
            dli_all = jnp.zeros((CHUNK, N_SMALL), F32)
            db_all = jnp.zeros((CHUNK, N_SMALL), F32)
            heads = range(NH)
            qs = [qk_ref[rows, h * DQK:(h + 1) * DQK] for h in heads]
            ks = [qk_ref[rows, NH * DQK + h * DQK:NH * DQK + (h + 1) * DQK] for h in heads]
            vs = [v_ref[rows, h * DV:(h + 1) * DV] for h in heads]
            cst = [cs_ref[c, h].astype(F32) for h in heads]
            nrow = [ns_ref[c, h, 0:1, :] for h in heads]
            m11 = [jnp.max(ns_ref[c, h, 1:2, :], axis=1, keepdims=True) for h in heads]
            f = [_mlstm_open(h, qs[h], ks[h], cst[h], li_all, b_all, k) for h in heads]
            f = [_mlstm_weights(h, f[h], qs[h], vs[h], li_all, b_all, nrow[h], m11[h], k) for h in heads]
            f = [_mlstm_out(f[h]) for h in heads]
            t = []
            for h in heads:
                gain = hg_ref[:, h * DV:(h + 1) * DV]
                gate = _sigmoid(mo_ref[rows, h * DV:(h + 1) * DV].astype(F32))
                dhm, dgate, dgain = _headnorm_bwd(dy_ref[rows, h * DV:(h + 1) * DV].astype(F32), f[h]["hm"], gain, gate)
                dproj_ref[rows, D + h * DV:D + (h + 1) * DV] = (dgate * gate * (1.0 - gate)).astype(BF)
                dhg_ref[:, h * DV:(h + 1) * DV] += dgain
                r, den = f[h]["r"], f[h]["den"]
                dnum = dhm / r
                dr = -jnp.sum(dhm * f[h]["hm"], axis=1, keepdims=True) / r
                dden = jnp.where(jnp.abs(den) > f[h]["floor"], dr * jnp.sign(den), 0.0)
                dnb = dnum.astype(BF)
                dc_new = dc_scr[h]
                dcb = dc_new.astype(BF)
                t.append(dict(dnum=dnum, dden=dden, dnb=dnb, dc_new=dc_new, dn_new=dn_scr[h],
                              dwm=_nt(dnb, vs[h]), vdc=_nn(vs[h], dcb), kdc=_nt(ks[h], dcb)))
            for h in heads:
                dw_mat = t[h]["dwm"] + t[h]["dden"]
                dsim = (f[h]["e"] * dw_mat).astype(BF)
                gm = f[h]["w_mat"] * dw_mat
                t[h].update(gm=gm, dv0=_tn(f[h]["w_mat"].astype(BF), t[h]["dnb"]), dq0=_nn(dsim, ks[h]),
                            dq1=_nn(t[h]["dnb"], f[h]["cb"]), dk0=_tn(dsim, qs[h]),
                            dcq=_tn((f[h]["a"] * t[h]["dnum"]).astype(BF), qs[h]), cs2=_tn(gm, k["ones"], precision=HI))
            for h in heads:
                a, w, a_s = f[h]["a"], f[h]["w"], f[h]["a_s"]
                dnum, dden, dc_new, dn_new, vdc, gm = (t[h][n] for n in ("dnum", "dden", "dc_new", "dn_new", "vdc", "gm"))
                kf = ks[h].astype(F32)
                dproj_ref[rows, h * DV:(h + 1) * DV] = (t[h]["dv0"] + w * t[h]["kdc"]).astype(BF)
                adden = a * dden
                dqk_ref[rows, h * DQK:(h + 1) * DQK] = t[h]["dq0"] + a * t[h]["dq1"] + adden * nrow[h]
                dqk_ref[rows, NH * DQK + h * DQK:NH * DQK + (h + 1) * DQK] = t[h]["dk0"] + w * vdc + w * dn_new
                da = jnp.sum(dnum * f[h]["cq"], axis=1, keepdims=True) + dden * f[h]["nq"]
                dw = jnp.sum(vdc * kf, axis=1, keepdims=True) + jnp.sum(kf * dn_new, axis=1, keepdims=True)
                da_s = _sum_all(dc_new * cst[h]) + jnp.sum(dn_new * nrow[h], axis=1, keepdims=True)
                wdw = w * dw
                rs = jnp.sum(gm, axis=1, keepdims=True)
                cs_col = _col(t[h]["cs2"], lane, 0)
                dg = a_s * da_s + jnp.sum(wdw, axis=0, keepdims=True)
                db = a * da + rs - cs_col - wdw + jnp.where(rowio == CHUNK - 1, dg, 0.0)
                dli_all = dli_all + jnp.where(lane == h, cs_col + wdw, 0.0)
                db_all = db_all + jnp.where(lane == NH + h, db, 0.0)
                dc_scr[h] = a_s * dc_new + t[h]["dcq"]
                dn_scr[h] = a_s * dn_new + jnp.sum(adden * f[h]["qf"], axis=0, keepdims=True)
            dlf_all = _nn(k["triu_f"], db_all, precision=HI)
            dsm = jnp.where(valid, dli_all + dlf_all * _sigmoid(-pre), 0.0)
            dsm = jnp.where(lane < 2 * NH, dsm, 0.0)
            dsm_ref[rows, :] = dsm
            dgb_ref[0:1, :] += jnp.sum(dsm, axis=0, keepdims=True)
            return carry

        lax.fori_loop(0, CPB, chunk, 0, unroll=2)

        if nr:
            @pl.when(step == nb - 1)
            def _():
                for cp in sent:
                    cp.wait_recv()
                for cp in sent:
                    cp.wait_send()

    rev = lambda col: (lambda i: (nb - 1 - i, col))
    rspec = lambda width, col=0: pl.BlockSpec((TM, width), rev(col))
    ride_shapes, ride_sems = _scatter_shapes(ride) if nr else ((), [])
    outs = pl.pallas_call(
        body, grid=(nb,),
        in_specs=[rspec(D), rspec(D), rspec(D, CB_MV), rspec(D, CB_MO), rspec(N_SMALL), _full_spec((1, N_SMALL)), _full_spec((1, D)),
                  pl.BlockSpec((CPB, NH, DV, DQK), lambda i: (nb - 1 - i, 0, 0, 0)),
                  pl.BlockSpec((CPB, NH, 8, DQK), lambda i: (nb - 1 - i, 0, 0, 0)), ANY] + [ANY] * nr,
        out_specs=[rspec(D), rspec(2 * D, CB_MV // 2), rspec(N_SMALL), _full_spec((8, N_SMALL)), _full_spec((1, D))] + [ANY] * nr,
        out_shape=(jax.ShapeDtypeStruct((tp, D), F32), jax.ShapeDtypeStruct(dproj.shape, BF),
                   jax.ShapeDtypeStruct((tp, N_SMALL), F32), jax.ShapeDtypeStruct((8, N_SMALL), F32),
                   jax.ShapeDtypeStruct((1, D), F32)) + tuple(ride_shapes),
        scratch_shapes=[pltpu.VMEM((NH, DV, DQK), F32), pltpu.VMEM((NH, 1, DQK), F32)] + ride_sems,
        input_output_aliases={9: 1}, compiler_params=_params(), name=name)(dy, qk, pbig, pbig, small, gbias, headg, cs, ns, dproj, *ride)
    return tuple(outs[:5]) + (list(outs[5:]),)


def _gla_loga(sm_ref, a2_ref, a2b_ref, blk):
    za = _nn(sm_ref[...].astype(BF), a2_ref[...]) + a2b_ref[...]
    row = blk * TM + lax.broadcasted_iota(jnp.int32, (TM, 1), 0)
    return za, jnp.where(row >= FIRST_VALID, _logsig(za) / G_TAU, 0.0)


def _gla_head(h, q_ref, k_ref, rows, bc, btot, k):
    sl = slice(h * DQK, (h + 1) * DQK)
    bch = bc[:, sl]
    bth = btot[:, sl]
    gq = q_ref[rows, h * DQK:(h + 1) * DQK].astype(F32)
    gk = k_ref[rows, NH * DQK + h * DQK:NH * DQK + (h + 1) * DQK].astype(F32)
    e_pos = jnp.exp(bch) * (DQK ** -0.5)
    e_neg = jnp.exp(-bch)
    e_end = jnp.exp(bth - bch)
    qd = gq * e_pos
    ki = gk * e_neg
    ke = gk * e_end
    att = jnp.where(k["tri"], _nt(qd.astype(BF), ki.astype(BF)), 0.0)
    return dict(e_pos=e_pos, e_neg=e_neg, e_end=e_end, qd=qd, ki=ki, ke=ke, att=att, decay=jnp.exp(bth))


def _gla_fwd(pbig, small, a2p, a2b, headg, name):
    tp = pbig.shape[0]
    nb = tp // TM

    def body(qk_ref, v_ref, gr_ref, sm_ref, a2_ref, a2b_ref, hg_ref, y_ref, ss_ref, s_scr, lg_scr):
        blk = pl.program_id(0)

        @pl.when(blk == 0)
        def _():
            s_scr[...] = jnp.zeros_like(s_scr)

        k = _chunk_consts(G_CHUNK)
        _, loga = _gla_loga(sm_ref, a2_ref, a2b_ref, blk)
        lg_scr[...] = loga

        def chunk(c, carry):
            r0 = pl.multiple_of(c * G_CHUNK, G_CHUNK)
            rows = pl.ds(r0, G_CHUNK)
            bc = _nn(k["tril_f"], lg_scr[rows, :], precision=HI)
            btot = _last_row(bc, k["rowio"])
            heads = range(NH)
            f = [_gla_head(h, qk_ref, qk_ref, rows, bc, btot, k) for h in heads]
            vs = [v_ref[rows, h * DV:(h + 1) * DV] for h in heads]
            sst = [s_scr[h] for h in heads]
            sbs = [s.astype(BF) for s in sst]
            inter = [_nt(f[h]["qd"].astype(BF), sbs[h]) for h in heads]
            intra = [_nn(f[h]["att"].astype(BF), vs[h]) for h in heads]
            kv = [_tn(vs[h], f[h]["ke"].astype(BF)) for h in heads]
            for h in heads:
                gr = gr_ref[rows, h * DV:(h + 1) * DV].astype(F32)
                y_ref[rows, h * DV:(h + 1) * DV] = _headnorm_fwd(intra[h] + inter[h], hg_ref[:, h * DV:(h + 1) * DV],
                                                                   gr * _sigmoid(gr)).astype(BF)
                ss_ref[c, h] = sbs[h]
                s_scr[h] = sst[h] * f[h]["decay"] + kv[h]
            return carry

        lax.fori_loop(0, G_CPB, chunk, 0, unroll=2)

    return pl.pallas_call(
        body, grid=(nb,),
        in_specs=[_row_spec(D, CB_GQK), _row_spec(D, CB_GV), _row_spec(D, CB_GR), _row_spec(N_SMALL),
                  _full_spec((N_SMALL, NH * DQK)), _full_spec((1, NH * DQK)), _full_spec((1, D))],
        out_specs=[_row_spec(D), pl.BlockSpec((G_CPB, NH, DV, DQK), lambda i: (i, 0, 0, 0))],
        out_shape=(jax.ShapeDtypeStruct((tp, D), BF), jax.ShapeDtypeStruct((tp // G_CHUNK, NH, DV, DQK), BF)),
        scratch_shapes=[pltpu.VMEM((NH, DV, DQK), F32), pltpu.VMEM((TM, NH * DQK), F32)],
        compiler_params=_params(), name=name)(pbig, pbig, pbig, small, a2p, a2b, headg)


def _gla_bwd(dy, pbig, small, a2p, a2b, headg, ss, dsm_m, dproj, name):
    tp = pbig.shape[0]
    nb = tp // TM
    nqk = NH * DQK

    def body(dy_ref, qk_ref, v_ref, gr_ref, sm_ref, a2_ref, a2b_ref, hg_ref, ss_ref, dsmm_ref, _,
             dproj_ref, dsm_ref, da2_ref, da2b_ref, dhg_ref, ds_scr, lg_scr, dza_scr):
        step = pl.program_id(0)
        blk = nb - 1 - step

        @pl.when(step == 0)
        def _():
            ds_scr[...] = jnp.zeros_like(ds_scr)
            da2_ref[...] = jnp.zeros_like(da2_ref)
            da2b_ref[...] = jnp.zeros_like(da2b_ref)
            dhg_ref[...] = jnp.zeros_like(dhg_ref)

        k = _chunk_consts(G_CHUNK)
        rowio = k["rowio"]
        za, loga = _gla_loga(sm_ref, a2_ref, a2b_ref, blk)
        lg_scr[...] = loga

        def chunk(cc, carry):
            c = G_CPB - 1 - cc
            r0 = pl.multiple_of(c * G_CHUNK, G_CHUNK)
            rows = pl.ds(r0, G_CHUNK)
            bc = _nn(k["tril_f"], lg_scr[rows, :], precision=HI)
            btot = _last_row(bc, rowio)
            heads = range(NH)
            f = [_gla_head(h, qk_ref, qk_ref, rows, bc, btot, k) for h in heads]
            vs = [v_ref[rows, h * DV:(h + 1) * DV] for h in heads]
            sbs = [ss_ref[c, h] for h in heads]
            qdb = [f[h]["qd"].astype(BF) for h in heads]
            attb = [f[h]["att"].astype(BF) for h in heads]
            inter = [_nt(qdb[h], sbs[h]) for h in heads]
            intra = [_nn(attb[h], vs[h]) for h in heads]
            dsn = [ds_scr[h] for h in heads]
            dsb = [d.astype(BF) for d in dsn]
            dke = [_nn(vs[h], dsb[h]) for h in heads]
            dv1 = [_nt(f[h]["ke"].astype(BF), dsb[h]) for h in heads]
            t = []
            for h in heads:
                gr = gr_ref[rows, h * DV:(h + 1) * DV].astype(F32)
                sg = _sigmoid(gr)
                gain = hg_ref[:, h * DV:(h + 1) * DV]
                do, dact, dgain = _headnorm_bwd(dy_ref[rows, h * DV:(h + 1) * DV].astype(F32), intra[h] + inter[h], gain, gr * sg)
                dproj_ref[rows, 2 * D + h * DV:2 * D + (h + 1) * DV] = (dact * sg * (1.0 + gr * (1.0 - sg))).astype(BF)
                dhg_ref[:, h * DV:(h + 1) * DV] += dgain
                dob = do.astype(BF)
                t.append(dict(dob=dob, datt=_nt(dob, vs[h]), dv0=_tn(attb[h], dob), dq1=_nn(dob, sbs[h]), dsq=_tn(dob, qdb[h])))
            for h in heads:
                datt = jnp.where(k["tri"], t[h]["datt"], 0.0).astype(BF)
                t[h].update(dq0=_nn(datt, f[h]["ki"].astype(BF)), dki=_tn(datt, qdb[h]))
            dbc_parts = []
            for h in heads:
                dqd = t[h]["dq0"] + t[h]["dq1"]
                dki = t[h]["dki"]
                dproj_ref[rows, D + h * DV:D + (h + 1) * DV] = (t[h]["dv0"] + dv1[h]).astype(BF)
                dproj_ref[rows, h * DQK:(h + 1) * DQK] = (dqd * f[h]["e_pos"]).astype(BF)
                dproj_ref[rows, nqk + h * DQK:nqk + (h + 1) * DQK] = (dki * f[h]["e_neg"] + dke[h] * f[h]["e_end"]).astype(BF)
                dke_ke = dke[h] * f[h]["ke"]
                dbtot = (jnp.sum(dke_ke, axis=0, keepdims=True)
                         + jnp.sum(dsn[h] * sbs[h].astype(F32), axis=0, keepdims=True) * f[h]["decay"])
                dbc_parts.append(dqd * f[h]["qd"] - dki * f[h]["ki"] - dke_ke + jnp.where(rowio == G_CHUNK - 1, dbtot, 0.0))
                ds_scr[h] = dsn[h] * f[h]["decay"] + t[h]["dsq"]
            dbc = jnp.concatenate(dbc_parts, axis=1)
            dza_scr[rows, :] = _nn(k["triu_f"], dbc, precision=HI)
            return carry

        lax.fori_loop(0, G_CPB, chunk, 0, unroll=2)
        row = blk * TM + lax.broadcasted_iota(jnp.int32, (TM, 1), 0)
        dza = jnp.where(row >= FIRST_VALID, dza_scr[...] * (_sigmoid(-za) / G_TAU), 0.0)
        dzb = dza.astype(BF)
        dsm_ref[...] = (_nt(dzb, a2_ref[...]) + dsmm_ref[...]).astype(BF)
        da2_ref[...] += _tn(sm_ref[...].astype(BF), dzb)
        da2b_ref[...] += jnp.sum(dza, axis=0, keepdims=True)

    rspec = lambda width, col=0: pl.BlockSpec((TM, width), lambda i: (nb - 1 - i, col))
    return pl.pallas_call(
        body, grid=(nb,),
        in_specs=[rspec(D), rspec(D, CB_GQK), rspec(D, CB_GV), rspec(D, CB_GR), rspec(N_SMALL),
                  _full_spec((N_SMALL, nqk)), _full_spec((1, nqk)), _full_spec((1, D)),
                  pl.BlockSpec((G_CPB, NH, DV, DQK), lambda i: (nb - 1 - i, 0, 0, 0)), rspec(N_SMALL), ANY],
        out_specs=[rspec(3 * D, 0), rspec(N_SMALL), _full_spec((N_SMALL, nqk)), _full_spec((1, nqk)), _full_spec((1, D))],
        out_shape=(jax.ShapeDtypeStruct(dproj.shape, BF),
                   jax.ShapeDtypeStruct((tp, N_SMALL), BF), jax.ShapeDtypeStruct((N_SMALL, nqk), F32),
                   jax.ShapeDtypeStruct((1, nqk), F32), jax.ShapeDtypeStruct((1, D), F32)),
        scratch_shapes=[pltpu.VMEM((NH, DV, DQK), F32), pltpu.VMEM((TM, nqk), F32), pltpu.VMEM((TM, nqk), F32)],
        input_output_aliases={10: 0}, compiler_params=_params(), name=name)(dy, pbig, pbig, pbig, small, a2p, a2b, headg, ss, dsm_m, dproj)


PIECE_BYTES = 1 << 20
MAX_PIECES = 32


def _place():
    return lax.axis_index("x"), lax.axis_index("y"), lax.axis_index("c")


def _piece_rows(rows, row_bytes, align):
    want = min(MAX_PIECES, max(1, -(-rows * row_bytes // PIECE_BYTES)))
    best = rows
    for k in range(1, want + 1):
        if rows % k == 0 and (rows // k) % align == 0:
            best = rows // k
    return best


def _remote(src, dst, send_sems, recv_sems, k, to):
    return pltpu.make_async_remote_copy(src_ref=src, dst_ref=dst, send_sem=send_sems.at[k], recv_sem=recv_sems.at[k],
                                        device_id=to, device_id_type=MESH)


def _all_gather_chips(p, name):
    rd = _gather_rider(p)

    def body(*refs):
        start, middle, finish = rd["make"](refs[:1], refs[1:2], refs[2:])
        start()
        middle()
        finish()

    return pl.pallas_call(body, in_specs=[ANY], out_specs=[ANY], out_shape=rd["out_shapes"], scratch_shapes=rd["sems"],
                          name=name)(p)[0]


def _gather_rider(p):
    r, n = p.shape
    rh = r // 2
    align = 32 // p.dtype.itemsize
    assert r % (2 * align) == 0
    cr = _piece_rows(rh, n * p.dtype.itemsize, align)

    def make(in_refs, out_refs, sem_refs):
        p_ref, o_ref = in_refs[0], out_refs[0]
        send_sems, recv_sems = sem_refs
        x, y, c = _place()
        chips = [(1 - x, y), (x, 1 - y), (1 - x, 1 - y)]
        sib = (x, y, 1 - c)

        def half(hc, piece=None):
            if piece is None:
                return pl.ds(pl.multiple_of(hc * rh, align), rh)
            return pl.ds(pl.multiple_of(hc * rh + piece * cr, align), cr)

        first = [_remote(p_ref.at[half(c)], o_ref.at[j, half(c)], send_sems, recv_sems, j, (*chip, c))
                 for j, chip in enumerate(chips)]
        passed = [[_remote(o_ref.at[j, half(c, i)], o_ref.at[j, half(c, i)], send_sems, recv_sems, 3 + j, sib)
                   for i in range(rh // cr)] for j in range(3)]
        blocks = [_remote(o_ref.at[j, half(c)], o_ref.at[j, half(1 - c)], send_sems, recv_sems, 3 + j, sib) for j in range(3)]

        def start():
            for cp in first:
                cp.start()

        def middle():
            for j, cp in enumerate(first):
                cp.wait_recv()
                for piece in passed[j]:
                    piece.start()

        def finish():
            for block in blocks:
                block.wait_send()
                block.wait_recv()
            for cp in first:
                cp.wait_send()

        return start, middle, finish

    return dict(inputs=[p], out_shapes=(jax.ShapeDtypeStruct((3, r, n), p.dtype),),
                sems=[pltpu.SemaphoreType.DMA((6,)), pltpu.SemaphoreType.DMA((6,))], make=make)


def _scatter_rider(items):
    out_shapes, sems = _scatter_shapes(items)

    def make(in_refs, out_refs, sem_refs):
        sent = _scatter_copies(in_refs, out_refs, *sem_refs)

        def start():
            for cp in sent:
                cp.start()

        def finish():
            for cp in sent:
                cp.wait_recv()
            for cp in sent:
                cp.wait_send()

        return start, (lambda: None), finish

    return dict(inputs=list(items), out_shapes=out_shapes, sems=sems, make=make)


def _by_chip(mine, others):
    me = 2 * lax.axis_index("x") + lax.axis_index("y")
    by_mask = jnp.stack([mine, others[1], others[0], others[2]])
    return [lax.dynamic_index_in_dim(by_mask, q ^ me, 0, keepdims=False) for q in range(4)]


def _swap_halves(items, name):
    k = len(items)

    def body(*refs):
        a_refs, got_refs = refs[:k], refs[k:2 * k]
        send_sems, recv_sems = refs[2 * k:]
        x, y, c = _place()
        sib = (x, y, 1 - c)
        for i, a in enumerate(items):
            _, r, n = a.shape
            rh = r // 2
            cr = _piece_rows(rh, n * a.dtype.itemsize, 8)
            for q in range(4):
                for t in range(rh // cr):
                    other = pl.ds(pl.multiple_of((1 - c) * rh + t * cr, 8), cr)
                    _remote(a_refs[i].at[q, other], got_refs[i].at[q, pl.ds(t * cr, cr)], send_sems, recv_sems, i, sib).start()
        for i, a in enumerate(items):
            block = _remote(a_refs[i].at[:, pl.ds(0, a.shape[1] // 2)], got_refs[i], send_sems, recv_sems, i, sib)
            block.wait_send()
            block.wait_recv()

    return pl.pallas_call(
        body, in_specs=[ANY] * k, out_specs=[ANY] * k,
        out_shape=tuple(jax.ShapeDtypeStruct((4, a.shape[1] // 2, a.shape[2]), a.dtype) for a in items),
        scratch_shapes=[pltpu.SemaphoreType.DMA((k,)), pltpu.SemaphoreType.DMA((k,))], name=name)(*items)


def _scatter_copies(s_refs, o_refs, send_sems, recv_sems):
    x, y, c = _place()
    chips = [(1 - x, y), (x, 1 - y), (1 - x, 1 - y)]
    return [_remote(s_refs[i].at[2 * cx + cy], o_refs[i].at[j], send_sems, recv_sems, 3 * i + j, (cx, cy, c))
            for i in range(len(s_refs)) for j, (cx, cy) in enumerate(chips)]


def _scatter_shapes(items):
    k = len(items)
    return (tuple(jax.ShapeDtypeStruct((3,) + s.shape[1:], s.dtype) for s in items),
            [pltpu.SemaphoreType.DMA((3 * k,)), pltpu.SemaphoreType.DMA((3 * k,))])


def _scatter_chips(items, name):
    k = len(items)

    def body(*refs):
        sent = _scatter_copies(refs[:k], refs[k:2 * k], *refs[2 * k:])
        for cp in sent:
            cp.start()
        for cp in sent:
            cp.wait_recv()
        for cp in sent:
            cp.wait_send()

    out_shape, scratch = _scatter_shapes(items)
    return pl.pallas_call(body, in_specs=[ANY] * k, out_specs=[ANY] * k, out_shape=out_shape, scratch_shapes=scratch,
                          name=name)(*items)


def _join_halves(items, name):
    k = len(items)

    def body(*refs):
        f_refs, o_refs = refs[:k], refs[k:2 * k]
        send_sems, recv_sems = refs[2 * k:]
        x, y, c = _place()
        sib = (x, y, 1 - c)
        for i, f in enumerate(items):
            rh, n = f.shape
            cr = _piece_rows(rh, n * f.dtype.itemsize, 8)
            for t in range(rh // cr):
                rows = pl.ds(t * cr, cr)
                _remote(f_refs[i].at[rows], o_refs[i].at[rows], send_sems, recv_sems, i, sib).start()
        for i in range(k):
            block = _remote(f_refs[i], o_refs[i], send_sems, recv_sems, i, sib)
            block.wait_send()
            block.wait_recv()

    return pl.pallas_call(
        body, in_specs=[ANY] * k, out_specs=[ANY] * k, out_shape=tuple(jax.ShapeDtypeStruct(f.shape, f.dtype) for f in items),
        scratch_shapes=[pltpu.SemaphoreType.DMA((k,)), pltpu.SemaphoreType.DMA((k,))], name=name)(*items)


SMALL_ROWS = 16
SMALL_SHARD_SHAPES = [(N_META, 256), (4, 256), (G_RANK, 128), (NH, 64), (NH, 64)]
REPL_SHAPES = [(1, D), (1, D), (1, 2, NH), (1, NH * DQK), (1, D), (D,)]
W_IN_SHARD = 2054
W_IN_BLOCK = 2080


def _pack_small(parts):
    flat = jnp.concatenate([p.reshape(-1) for p in parts])
    return jnp.pad(flat, (0, SMALL_ROWS * D - flat.shape[0])).reshape(SMALL_ROWS, D)


def _unpack_small(block, shapes):
    flat, out, off = block.reshape(-1), [], 0
    for shp in shapes:
        n = 1
        for s in shp:
            n *= s
        out.append(flat[off:off + n].reshape(shp))
        off += n
    return out


def _proj_rows_from_w_in(w_in_t):
    w_big = jnp.concatenate([w_in_t[3080:5128], w_in_t[5144:6168], w_in_t[0:1024], w_in_t[6168:8216],
                             w_in_t[1024:2048], w_in_t[2056:3080]], axis=0)
    w_small = jnp.concatenate([w_in_t[2048:2056], w_in_t[5128:5144], jnp.zeros((N_SMALL - 24, D), w_in_t.dtype)], axis=0)
    return w_big, w_small


def _w_in_from_proj_rows(d_wall_t):
    big, small = d_wall_t[0:N_BIG], d_wall_t[N_BIG:N_ALL]
    return jnp.concatenate([big[3072:4096], big[6144:7168], small[0:8], big[7168:8192], big[0:2048],
                            small[8:24], big[2048:3072], big[4096:6144]], axis=0)


def kernel(x, meta_tokens, norm1_g, w_in, conv_w, conv_b, m_gate_b, g_a2, g_a2_b, m_head_g, g_head_g, w_branch_m, w_branch_g, w_out, norm2_g, w_ff_gate, w_ff_up, w_ff_down, final_g, loss_target, m_meta_tokens, m_norm1_g, m_w_in, m_conv_w, m_conv_b, m_m_gate_b, m_g_a2, m_g_a2_b, m_m_head_g, m_g_head_g, m_w_branch_m, m_w_branch_g, m_w_out, m_norm2_g, m_w_ff_gate, m_w_ff_up, m_w_ff_down, m_final_g, v_meta_tokens, v_norm1_g, v_w_in, v_conv_w, v_conv_b, v_m_gate_b, v_g_a2, v_g_a2_b, v_m_head_g, v_g_head_g, v_w_branch_m, v_w_branch_g, v_w_out, v_norm2_g, v_w_ff_gate, v_w_ff_up, v_w_ff_down, v_final_g):
    w = _gather_weights(w_in, w_branch_m, w_branch_g, w_out, w_ff_gate, w_ff_up, w_ff_down, meta_tokens, conv_w, g_a2, m_head_g, g_head_g)
    loss_local, dx, grads = _local_step(x[0], loss_target[0], w, norm1_g, conv_b, m_gate_b, g_a2_b, norm2_g, final_g, _Reducer())

    weights = [w_in, w_branch_m, w_branch_g, w_out, w_ff_gate, w_ff_up, w_ff_down, meta_tokens, conv_w, g_a2, m_head_g, g_head_g,
               norm1_g, conv_b, m_gate_b, g_a2_b, norm2_g, final_g]
    moms = [m_w_in, m_w_branch_m, m_w_branch_g, m_w_out, m_w_ff_gate, m_w_ff_up, m_w_ff_down, m_meta_tokens, m_conv_w, m_g_a2,
            m_m_head_g, m_g_head_g, m_norm1_g, m_conv_b, m_m_gate_b, m_g_a2_b, m_norm2_g, m_final_g]
    vels = [v_w_in, v_w_branch_m, v_w_branch_g, v_w_out, v_w_ff_gate, v_w_ff_up, v_w_ff_down, v_meta_tokens, v_conv_w, v_g_a2,
            v_m_head_g, v_g_head_g, v_norm1_g, v_conv_b, v_m_gate_b, v_g_a2_b, v_norm2_g, v_final_g]
    res = {}
    for nm, wt, g, m, v in zip(PACK_ORDER, weights, grads, moms, vels):
        if nm in TRANSPOSED_GRADS:
            to2d = lambda a: jnp.swapaxes(a, -1, -2).reshape(a.shape[-1], a.shape[-2])
            back = lambda a: jnp.swapaxes(a, 0, 1).reshape(wt.shape)
        else:
            to2d = lambda a: a.reshape(wt.size // wt.shape[-1], wt.shape[-1])
            back = lambda a: a.reshape(wt.shape)
        d, nm_, nv_ = _adamw(to2d(wt), g, to2d(m), to2d(v), "adamw_" + nm)
        res[nm] = (back(g), back(d), back(nm_), back(nv_))

    order = ["meta_tokens", "norm1_g", "w_in", "conv_w", "conv_b", "m_gate_b", "g_a2", "g_a2_b", "m_head_g", "g_head_g",
             "w_branch_m", "w_branch_g", "w_out", "norm2_g", "w_ff_gate", "w_ff_up", "w_ff_down", "final_g"]
    loss = lax.psum(loss_local[0, 0], ("x", "y", "c"))
    grad_x = dx.reshape(x.shape)
    return (loss, grad_x, *[res[n][0] for n in order], *[res[n][1] for n in order],
            *[res[n][2] for n in order], *[res[n][3] for n in order])


TRANSPOSED_GRADS = ("w_in", "w_ff_gate", "w_ff_up")
PACK_ORDER = ["w_in", "w_branch_m", "w_branch_g", "w_out", "w_ff_gate", "w_ff_up", "w_ff_down", "meta_tokens", "conv_w", "g_a2",
              "m_head_g", "g_head_g", "norm1_g", "conv_b", "m_gate_b", "g_a2_b", "norm2_g", "final_g"]


def _gather_weights(w_in, w_branch_m, w_branch_g, w_out, w_ff_gate, w_ff_up, w_ff_down, meta_tokens, conv_w, g_a2, m_head_g, g_head_g):
    bf = lambda a: a.astype(BF)
    rows_local = jnp.concatenate([bf(w_branch_m[0]), bf(w_branch_g[0]), bf(w_out[0]), bf(w_ff_down[0]),
                                  bf(w_ff_gate[0].T), bf(w_ff_up[0].T)], axis=0)
    win_local = jnp.pad(bf(w_in[0].T), ((0, W_IN_BLOCK - W_IN_SHARD), (0, 0)))
    small_local = _pack_small([meta_tokens, conv_w[0], g_a2[0], m_head_g[0], g_head_g[0]])
    win_all = _by_chip(win_local, _all_gather_chips(win_local, "gather_w_in"))
    small_all = _by_chip(small_local, _all_gather_chips(small_local, "gather_small"))
    w_in_f = jnp.concatenate([win_all[q][0:W_IN_SHARD] for q in range(4)], axis=0)
    small_sh = [_unpack_small(small_all[q], SMALL_SHARD_SHAPES) for q in range(4)]
    cat = lambda i: jnp.concatenate([s[i] for s in small_sh], axis=-1)
    return dict(w_in=w_in_f, rows_local=rows_local, meta=cat(0), convw=cat(1), ga2=cat(2),
                mhg=cat(3).reshape(1, D), ghg=cat(4).reshape(1, D))


def _row_weights(rows_local, gathered):
    rows_all = jnp.stack(_by_chip(rows_local, gathered))
    cut = lambda lo, hi: rows_all[:, lo:hi].reshape(4 * (hi - lo), D)
    return cut(0, 256), cut(256, 512), cut(512, 768), cut(768, 1472), _ffn_weight_rows(cut(1472, 2176), cut(2176, 2880))


def _local_step(x0, target, w, norm1_g, conv_b, m_gate_b, g_a2_b, norm2_g, final_g, reducer):
    w_in_f, meta_f, convw_f, ga2_f, mhg_f, ghg_f = w["w_in"], w["meta"], w["convw"], w["ga2"], w["mhg"], w["ghg"]
    w_big, w_small = _proj_rows_from_w_in(w_in_f)
    w_all = jnp.concatenate([w_big, w_small], axis=0)
    gbias = jnp.concatenate([m_gate_b.reshape(1, 2 * NH), jnp.zeros((1, N_SMALL - 2 * NH), F32)], axis=1)
    a2p = jnp.concatenate([jnp.zeros((8, NH * DQK), F32), ga2_f, jnp.zeros((N_SMALL - 24, NH * DQK), F32)], axis=0).astype(BF)
    convb = conv_b.reshape(1, D)
    g1 = norm1_g.reshape(1, D)
    g2 = norm2_g.reshape(1, D)
    gf = final_g.reshape(1, D)
    h0 = jnp.concatenate([jnp.zeros((FIRST_VALID, D), F32), meta_f, x0], axis=0)

    xn1, rstd1 = _rms_fwd(h0, g1, "rms1")
    pbig, rows_gathered = _mm(xn1, w_big, nt=True, out_dtype=BF, tn=1024, name="proj_big", rider=_gather_rider(w["rows_local"]))
    wbm, wbg, wout, wdown, wgu_t = _row_weights(w["rows_local"], rows_gathered[0])
    small = _mm(xn1, w_small, nt=True, out_dtype=F32, tn=N_SMALL, name="proj_small")
    qk = _conv_fwd(pbig, convw_f, convb, "conv_fwd")
    y_m, m_cs, m_ns = _mlstm_fwd(qk, pbig, small, gbias, mhg_f, "mlstm_fwd")
    y_g, g_ss = _gla_fwd(pbig, small, a2p, g_a2_b, ghg_f, "gla_fwd")
    p_m, p_g, merged = _branch_merge(y_m, y_g, wbm, wbg, pbig, "branch_merge")
    h1, hn, rstd2 = _out_proj_norm(merged, wout, h0, g2, "out_proj")
    gu, ff = _ffn_in(hn, wgu_t, "ff_in")
    dh2, loss_local, d_final_g = _ffn_down_loss(ff, wdown, h1, target, gf, "ff_down_loss")

    d_wdown = _mm_tn(ff, dh2, tm=1408, tn=1024, name="dw_ff_down")
    dgu = _ffn_d_hidden(dh2, wdown, gu, "d_ff")
    d_wgu_t = _mm_tn(dgu, hn, tm=1408, tn=1024, name="dw_ff_in")
    dh1, d_g2 = _ffn_d_in(dgu, wgu_t, h1, rstd2, g2, dh2, "d_hn")
    d_wout = _mm_tn(merged, dh1, tm=1024, tn=1024, name="dw_out")
    dp_m, dp_g, dproj = _merge_d(dh1, wout, p_m, p_g, pbig, "d_merged")
    dy_m = _mm(dp_m, wbm, nt=True, out_dtype=BF, tn=1024, name="d_ym")
    dy_g = _mm(dp_g, wbg, nt=True, out_dtype=BF, tn=1024, name="d_yg")
    d_wbm = _mm_tn(y_m, dp_m, tm=1024, tn=1024, name="dw_branch_m")
    d_wbg = _mm_tn(y_g, dp_g, tm=1024, tn=1024, name="dw_branch_g")
    fq = D_FF // 4
    gu4 = jnp.transpose(d_wgu_t.reshape(2, 2, 2, fq, D), (0, 2, 1, 3, 4)).reshape(4, 2 * fq, D)
    sq4 = jnp.concatenate([d_wbm.reshape(4, 256, D), d_wbg.reshape(4, 256, D), d_wout.reshape(4, 256, D)], axis=1)
    sums_a = reducer.partial_sums([sq4, d_wdown.reshape(4, fq, D), gu4], BF, "a")
    dqk_m, dproj, dsm_m, d_gbias, d_mhg, recv_a = _mlstm_bwd(dy_m, qk, pbig, small, gbias, mhg_f, m_cs, m_ns, dproj,
                                                              "mlstm_bwd", ride=sums_a)
    dconv, d_convwb = _conv_bwd_pre(dqk_m, pbig, convw_f, convb, "conv_bwd_pre")
    dproj = _conv_bwd_in(dconv, convw_f, dproj, "conv_bwd_in")
    dproj, dsmall, d_a2p, d_a2b, d_ghg = _gla_bwd(dy_g, pbig, small, a2p, g_a2_b, ghg_f, g_ss, dsm_m, dproj, "gla_bwd")
    dproj = _place_small(dsmall, dproj, "dproj_small")
    d_win = _w_in_from_proj_rows(_mm_tn(dproj, xn1, tm=1664, tn=1024, name="dw_in"))
    pad = jnp.zeros((W_IN_BLOCK - W_IN_SHARD, D), F32)
    win4 = jnp.stack([jnp.concatenate([d_win[q * W_IN_SHARD:(q + 1) * W_IN_SHARD], pad], axis=0) for q in range(4)])
    sums_b = reducer.partial_sums([win4], BF, "b")
    dxn, recv_b = _mm(dproj, w_all, nt=False, out_dtype=F32, tn=1024, tk=1664, name="d_xn", rider=_scatter_rider(sums_b))
    dh_first, dx, d_g1 = _rms_bwd(dxn, h0, rstd1, g1, dh1, "rms1_bwd", split_first=True)

    small_sharded = [dh_first[FIRST_VALID:TM], d_convwb[0:4], d_a2p[8:24], d_mhg.reshape(NH, DV), d_ghg.reshape(NH, DV)]
    replicated = [d_g1, d_convwb[4:5], d_gbias[0:1, 0:2 * NH].reshape(1, 2, NH), d_a2b, d_g2, d_final_g.reshape(D)]
    small4 = jnp.stack([_pack_small([g[:, q * shp[1]:(q + 1) * shp[1]] for g, shp in zip(small_sharded, SMALL_SHARD_SHAPES)]
                                    + replicated) for q in range(4)])
    sums_c = reducer.partial_sums([small4], F32, "c")
    recv_c = reducer.scatter(sums_c, "c")
    sq, down, gu, win, smalls = reducer.finish(sums_a + sums_b + sums_c, recv_a + recv_b + recv_c, in_chip_order=[4])
    smalls = _unpack_small(smalls, SMALL_SHARD_SHAPES + REPL_SHAPES)
    grads = ([win[0:W_IN_SHARD], sq[0:256], sq[256:512], sq[512:768], gu[0:fq], gu[fq:2 * fq], down]
             + [g.reshape(g.size // g.shape[-1], g.shape[-1]) for g in smalls])
    return loss_local, dx, grads


class _Reducer:
    def partial_sums(self, items, dtype, tag):
        c = lax.axis_index("c")
        got = _swap_halves(items, "reduce_siblings_" + tag)
        sums = []
        for i, (a, g) in enumerate(zip(items, got)):
            rh, n = g.shape[1], g.shape[2]
            own = lax.dynamic_slice_in_dim(a, c * rh, rh, axis=1)
            sums.append(_add2(own.reshape(-1, n), g.reshape(-1, n), dtype, f"reduce_add2_{tag}{i}").reshape(g.shape))
        return sums

    def scatter(self, sums, tag):
        return list(_scatter_chips(sums, "reduce_chips_" + tag))

    def finish(self, sums, from_chips, in_chip_order):
        c = lax.axis_index("c")
        me = 2 * lax.axis_index("x") + lax.axis_index("y")
        halves = []
        for i, (s, f) in enumerate(zip(sums, from_chips)):
            mine = lax.dynamic_index_in_dim(s, me, 0, keepdims=False)
            if i in in_chip_order:
                by_chip = _by_chip(mine, f)
                mine, f = by_chip[0], jnp.stack(by_chip[1:])
            halves.append(_add4(mine, f, f"reduce_add4_{i}"))
        got = _join_halves(halves, "reduce_join")
        return [jnp.where(c == 0, jnp.concatenate([h, g], axis=0), jnp.concatenate([g, h], axis=0)) for h, g in zip(halves, got)]
```

```python
import functools

import jax
import jax.numpy as jnp
from jax import lax
from jax.experimental import pallas as pl
from jax.experimental.pallas import tpu as pltpu

F32 = jnp.float32
BF = jnp.bfloat16
HI = lax.Precision.HIGHEST
MESH = pl.DeviceIdType.MESH

D = 1024
N_META = 16
CHUNK = 128
EPS = 1e-6
NH = 4
DV = 256
DQK = 128
G_RANK = 16
G_TAU = 16.0
D_FF = 2816
TM = 512
FIRST_VALID = TM - N_META
CPB = TM // CHUNK
G_CHUNK = 256
G_CPB = TM // G_CHUNK
NEG = -1e30
N_BIG = 8192
CB_GQK, CB_GV, CB_GR, CB_MQK, CB_GM, CB_GG, CB_MV, CB_MO = range(8)
N_SMALL = 128
N_ALL = N_BIG + N_SMALL
VMEM_LIMIT = 56 * 1024 * 1024

ADAM_LR, ADAM_B1, ADAM_B2, ADAM_EPS, ADAM_WD, ADAM_STEP = 0.001, 0.9, 0.999, 1e-08, 0.01, 10

NT_DIMS = (((1,), (1,)), ((), ()))
TN_DIMS = (((0,), (0,)), ((), ()))


def _nt(a, b, **kw):
    return lax.dot_general(a, b, NT_DIMS, preferred_element_type=F32, **kw)


def _tn(a, b, **kw):
    return lax.dot_general(a, b, TN_DIMS, preferred_element_type=F32, **kw)


def _nn(a, b, **kw):
    return jnp.dot(a, b, preferred_element_type=F32, **kw)


def _params(**kw):
    return pltpu.CompilerParams(vmem_limit_bytes=VMEM_LIMIT, **kw)


def _sigmoid(x):
    return 0.5 * jnp.tanh(0.5 * x) + 0.5


def _logsig(x):
    return jnp.minimum(x, 0.0) - jnp.log(1.0 + jnp.exp(-jnp.abs(x)))


def _mm_rows(rows):
    return 3 * TM if rows % (3 * TM) == 0 else TM


def _mm(a, b, *, nt, out_dtype, tn, tk=None, tm=None, name, rider=None):
    m, k = a.shape
    n = b.shape[0] if nt else b.shape[1]
    tk = k if tk is None else tk
    tm = _mm_rows(m) if tm is None else tm
    nk = k // tk
    nj, ni = n // tn, m // tm
    nr_in = len(rider["inputs"]) if rider else 0
    nr_out = len(rider["out_shapes"]) if rider else 0
    assert m % tm == 0 and n % tn == 0 and k % tk == 0
    dims = NT_DIMS if nt else (((1,), (0,)), ((), ()))

    def body(*refs):
        a_ref, b_ref = refs[:2]
        o_ref = refs[2 + nr_in]
        j, i, kk = pl.program_id(0), pl.program_id(1), pl.program_id(2)
        step = (j * ni + i) * nk + kk
        if rider:
            start, middle, finish = rider["make"](refs[2:2 + nr_in], refs[3 + nr_in:3 + nr_in + nr_out],
                                                  refs[3 + nr_in + nr_out:5 + nr_in + nr_out])
            pl.when(step == 0)(start)
            pl.when(step == (nj * ni * nk) // 2)(middle)

        part = lax.dot_general(a_ref[...].astype(BF), b_ref[...].astype(BF), dims, preferred_element_type=F32)
        if nk == 1:
            o_ref[...] = part.astype(o_ref.dtype)
        else:
            acc_ref = refs[-1]

            @pl.when(kk == 0)
            def _():
                acc_ref[...] = part

            @pl.when(jnp.logical_and(kk > 0, kk < nk - 1))
            def _():
                acc_ref[...] += part

            @pl.when(kk == nk - 1)
            def _():
                o_ref[...] = (acc_ref[...] + part).astype(o_ref.dtype)

        if rider:
            pl.when(step == nj * ni * nk - 1)(finish)

    outs = pl.pallas_call(
        body, grid=(nj, ni, nk),
        in_specs=[pl.BlockSpec((tm, tk), lambda j, i, kk: (i, kk)),
                  pl.BlockSpec((tn, tk), lambda j, i, kk: (j, kk)) if nt else pl.BlockSpec((tk, tn), lambda j, i, kk: (kk, j))]
                 + [ANY] * nr_in,
        out_specs=[pl.BlockSpec((tm, tn), lambda j, i, kk: (i, j))] + [ANY] * nr_out,
        out_shape=(jax.ShapeDtypeStruct((m, n), out_dtype),) + (tuple(rider["out_shapes"]) if rider else ()),
        scratch_shapes=(rider["sems"] if rider else []) + ([pltpu.VMEM((tm, tn), F32)] if nk > 1 else []),
        compiler_params=_params(), name=name)(a, b, *(rider["inputs"] if rider else []))
    return (outs[0], list(outs[1:])) if rider else outs[0]


def _mm_tn(a, b, *, tm, tn, tk=None, name):
    t, m = a.shape
    n = b.shape[1]
    tk = _mm_rows(t) if tk is None else tk
    assert t % tk == 0 and m % tm == 0 and n % tn == 0

    def body(a_ref, b_ref, o_ref):
        part = _tn(a_ref[...].astype(BF), b_ref[...].astype(BF))

        @pl.when(pl.program_id(2) == 0)
        def _():
            o_ref[...] = part

        @pl.when(pl.program_id(2) > 0)
        def _():
            o_ref[...] += part

    return pl.pallas_call(
        body, grid=(m // tm, n // tn, t // tk),
        in_specs=[pl.BlockSpec((tk, tm), lambda i, j, kk: (kk, i)), pl.BlockSpec((tk, tn), lambda i, j, kk: (kk, j))],
        out_specs=pl.BlockSpec((tm, tn), lambda i, j, kk: (i, j)),
        out_shape=jax.ShapeDtypeStruct((m, n), F32), compiler_params=_params(), name=name)(a, b)


ANY = pl.BlockSpec(memory_space=pl.ANY)


def _row_spec(width, col=0):
    return pl.BlockSpec((TM, width), lambda i: (i, col))


def _full_spec(shape):
    return pl.BlockSpec(shape, lambda i: (0,) * len(shape))


def _embed_norm(x0, first, g, rider, name):
    tp = x0.shape[0] + TM
    nb = tp // TM
    nri, nro = len(rider["inputs"]), len(rider["out_shapes"])

    def body(*refs):
        x_ref, f_ref, g_ref = refs[:3]
        h_ref, xn_ref, r_ref = refs[3 + nri:6 + nri]
        i = pl.program_id(0)
        start, middle, finish = rider["make"](refs[3:3 + nri], refs[6 + nri:6 + nri + nro], refs[6 + nri + nro:])
        pl.when(i == 0)(start)
        pl.when(i == nb // 2)(middle)
        x = jnp.where(i == 0, f_ref[...], x_ref[...])
        r = lax.rsqrt(jnp.mean(x * x, axis=1, keepdims=True) + EPS)
        h_ref[...] = x
        xn_ref[...] = (x * r * g_ref[...]).astype(BF)
        r_ref[...] = r
        pl.when(i == nb - 1)(finish)

    outs = pl.pallas_call(
        body, grid=(nb,),
        in_specs=[pl.BlockSpec((TM, D), lambda i: (jnp.maximum(i - 1, 0), 0)), _full_spec((TM, D)), _full_spec((1, D))] + [ANY] * nri,
        out_specs=[_row_spec(D), _row_spec(D), _row_spec(1)] + [ANY] * nro,
        out_shape=(jax.ShapeDtypeStruct((tp, D), F32), jax.ShapeDtypeStruct((tp, D), BF), jax.ShapeDtypeStruct((tp, 1), F32))
                  + tuple(rider["out_shapes"]),
        scratch_shapes=rider["sems"], compiler_params=_params(), name=name)(x0, first, g, *rider["inputs"])
    return outs[0], outs[1], outs[2], list(outs[3:])


def _rms_bwd(dxn, h, rstd, g, dres, name, split_first=False):
    tp = h.shape[0]

    def body(dxn_ref, h_ref, r_ref, g_ref, dres_ref, *outs):
        r = r_ref[...]
        xh = h_ref[...] * r
        dxn_v = dxn_ref[...].astype(F32)
        dxh = dxn_v * g_ref[...]
        dh = r * (dxh - xh * jnp.mean(dxh * xh, axis=1, keepdims=True)) + dres_ref[...]
        if split_first:
            first_ref, dh_ref, dg_ref = outs

            @pl.when(pl.program_id(0) == 0)
            def _():
                first_ref[...] = dh
        else:
            dh_ref, dg_ref = outs
        dh_ref[...] = dh
        part = jnp.sum(dxn_v * xh, axis=0, keepdims=True)

        @pl.when(pl.program_id(0) == 0)
        def _():
            dg_ref[...] = part

        @pl.when(pl.program_id(0) > 0)
        def _():
            dg_ref[...] += part

    if split_first:
        out_specs = [_full_spec((TM, D)), pl.BlockSpec((TM, D), lambda i: (jnp.maximum(i - 1, 0), 0)), _full_spec((1, D))]
        out_shape = (jax.ShapeDtypeStruct((TM, D), F32), jax.ShapeDtypeStruct((tp - TM, D), F32), jax.ShapeDtypeStruct((1, D), F32))
    else:
        out_specs = [_row_spec(D), _full_spec((1, D))]
        out_shape = (jax.ShapeDtypeStruct((tp, D), F32), jax.ShapeDtypeStruct((1, D), F32))
    return pl.pallas_call(
        body, grid=(tp // TM,),
        in_specs=[_row_spec(D), _row_spec(D), _row_spec(1), _full_spec((1, D)), _row_spec(D)],
        out_specs=out_specs, out_shape=out_shape, compiler_params=_params(), name=name)(dxn, h, rstd, g, dres)


def _shift_down(x, halo, k):
    rk = pltpu.roll(x, k, 0)
    io = lax.broadcasted_iota(jnp.int32, (8, x.shape[1]), 0)
    top = jnp.where(io < k, pltpu.roll(halo, k, 0), rk[0:8])
    return jnp.concatenate([top, rk[8:]], axis=0)


def _shift_up(x, nxt, k):
    n = x.shape[0]
    rk = pltpu.roll(x, n - k, 0)
    io = lax.broadcasted_iota(jnp.int32, (8, x.shape[1]), 0)
    bot = jnp.where(io >= 8 - k, pltpu.roll(nxt, 8 - k, 0), rk[n - 8:n])
    return jnp.concatenate([rk[:n - 8], bot], axis=0)


def _conv_pre(x, halo, w_ref, b_ref):
    c = x * w_ref[3:4, :] + b_ref[...]
    shifted = []
    for k in (1, 2, 3):
        s = _shift_down(x, halo, k)
        shifted.append(s)
        c = c + s * w_ref[3 - k:4 - k, :]
    return c, shifted


def _qk_scale():
    col = lax.broadcasted_iota(jnp.int32, (1, D), 1)
    return jnp.where(col < NH * DQK, DQK ** -0.5, 1.0).astype(F32)


def _halo_prev_spec():
    return pl.BlockSpec((8, D), lambda i: (jnp.maximum(i * (TM // 8) - 1, 0), CB_MQK))


def _conv_fwd(pbig, w, b, name):
    tp = pbig.shape[0]

    def body(x_ref, halo_ref, w_ref, b_ref, o_ref):
        x = x_ref[...].astype(F32)
        halo = jnp.where(pl.program_id(0) > 0, halo_ref[...].astype(F32), 0.0)
        c, _ = _conv_pre(x, halo, w_ref, b_ref)
        o_ref[...] = (c * _sigmoid(c) * _qk_scale()).astype(BF)

    return pl.pallas_call(
        body, grid=(tp // TM,),
        in_specs=[_row_spec(D, CB_MQK), _halo_prev_spec(), _full_spec((4, D)), _full_spec((1, D))],
        out_specs=_row_spec(D), out_shape=jax.ShapeDtypeStruct((tp, D), BF),
        compiler_params=_params(), name=name)(pbig, pbig, w, b)


def _conv_bwd_pre(dqk, pbig, w, b, name):
    tp = pbig.shape[0]

    def body(d_ref, x_ref, halo_ref, w_ref, b_ref, dc_ref, dwb_ref):
        x = x_ref[...].astype(F32)
        halo = jnp.where(pl.program_id(0) > 0, halo_ref[...].astype(F32), 0.0)
        c, shifted = _conv_pre(x, halo, w_ref, b_ref)
        sg = _sigmoid(c)
        dc = d_ref[...] * _qk_scale() * (sg * (1.0 + c * (1.0 - sg)))
        dc_ref[...] = dc
        taps = [shifted[2], shifted[1], shifted[0], x]
        rows = [jnp.sum(dc * t, axis=0, keepdims=True) for t in taps] + [jnp.sum(dc, axis=0, keepdims=True)]
        io = lax.broadcasted_iota(jnp.int32, (8, D), 0)
        part = jnp.zeros((8, D), F32)
        for r, v in enumerate(rows):
            part = jnp.where(io == r, v, part)

        @pl.when(pl.program_id(0) == 0)
        def _():
            dwb_ref[...] = part

        @pl.when(pl.program_id(0) > 0)
        def _():
            dwb_ref[...] += part

    return pl.pallas_call(
        body, grid=(tp // TM,),
        in_specs=[_row_spec(D), _row_spec(D, CB_MQK), _halo_prev_spec(), _full_spec((4, D)), _full_spec((1, D))],
        out_specs=[_row_spec(D), _full_spec((8, D))],
        out_shape=(jax.ShapeDtypeStruct((tp, D), F32), jax.ShapeDtypeStruct((8, D), F32)),
        compiler_params=_params(), name=name)(dqk, pbig, pbig, w, b)


def _conv_bwd_in(dc, w, dproj, name):
    tp = dc.shape[0]
    nb = tp // TM

    def body(d_ref, nxt_ref, w_ref, _, o_ref):
        d = d_ref[...]
        nxt = jnp.where(pl.program_id(0) < nb - 1, nxt_ref[...], 0.0)
        acc = d * w_ref[3:4, :]
        for k in (1, 2, 3):
            acc = acc + _shift_up(d, nxt, k) * w_ref[3 - k:4 - k, :]
        o_ref[...] = acc.astype(BF)

    return pl.pallas_call(
        body, grid=(nb,),
        in_specs=[_row_spec(D), pl.BlockSpec((8, D), lambda i: (jnp.minimum((i + 1) * (TM // 8), tp // 8 - 1), 0)),
                  _full_spec((4, D)), ANY],
        out_specs=_row_spec(D, CB_MQK), out_shape=jax.ShapeDtypeStruct(dproj.shape, BF),
        input_output_aliases={3: 0}, compiler_params=_params(), name=name)(dc, dc, w, dproj)


def _mm_fused(inputs, products, *, nt, m, n, tm, tn, outs, epilogue, name, nk=1, sub=None):
    dims = NT_DIMS if nt else (((1,), (0,)), ((), ()))
    nin = len(inputs)
    assert nk == 1 or (len(products) == 1 and sub is None)

    def body(*refs):
        in_refs, out_refs = refs[:nin], refs[nin:nin + len(outs)]
        i = pl.program_id(1)
        if sub is not None:
            lhs = {ia: in_refs[ia][...].astype(BF) for ia, _ in products}

            def dots(cols):
                return [lax.dot_general(lhs[ia], (in_refs[ib][cols, :] if nt else in_refs[ib][:, cols]).astype(BF),
                                        dims, preferred_element_type=F32) for ia, ib in products]

            slices = [slice(s, min(s + sub, tn)) for s in range(0, tn, sub)]
            prods = dots(slices[0])
            for idx, cols in enumerate(slices):
                nxt = dots(slices[idx + 1]) if idx + 1 < len(slices) else None
                epilogue(prods, in_refs, out_refs, i, cols)
                prods = nxt
            return
        prods = [lax.dot_general(in_refs[ia][...].astype(BF), in_refs[ib][...].astype(BF), dims, preferred_element_type=F32)
                 for ia, ib in products]
        if nk == 1:
            epilogue(prods, in_refs, out_refs, i, slice(None))
            return
        acc_ref = refs[-1]
        kk = pl.program_id(2)

        @pl.when(kk == 0)
        def _():
            acc_ref[...] = prods[0]

        @pl.when(jnp.logical_and(kk > 0, kk < nk - 1))
        def _():
            acc_ref[...] += prods[0]

        @pl.when(kk == nk - 1)
        def _():
            epilogue([acc_ref[...] + prods[0]], in_refs, out_refs, i, slice(None))

    return pl.pallas_call(
        body, grid=(n // tn, m // tm, nk), in_specs=[s for _, s in inputs], out_specs=[s for _, s in outs],
        out_shape=tuple(sh for sh, _ in outs), scratch_shapes=[pltpu.VMEM((tm, tn), F32)] if nk > 1 else [],
        compiler_params=_params(), name=name)(*[a for a, _ in inputs])


SUB_COLS = 256


def _cols_at(cols, offset):
    return slice(cols.start + offset, cols.stop + offset)


def _blk(rows, width, col=None, row=None):
    return pl.BlockSpec((rows, width), lambda j, i, kk: ((i if row is None else row(i)), (0 if col is None else col(j, kk))))


FF_TN = D_FF // 2


def _ffn_weight_rows(wg_t, wu_t):
    return jnp.concatenate([wg_t[0:FF_TN], wu_t[0:FF_TN], wg_t[FF_TN:], wu_t[FF_TN:]], axis=0)


def _ffn_in(hn, wgu_t, name):
    tp = hn.shape[0]
    tm = _mm_rows(tp)

    def epilogue(prods, in_refs, out_refs, i, cols):
        g, u = prods
        out_refs[0][:, cols] = g.astype(BF)
        out_refs[0][:, _cols_at(cols, FF_TN)] = u.astype(BF)
        out_refs[1][:, cols] = (g * _sigmoid(g) * u).astype(BF)

    wspec = lambda off: pl.BlockSpec((FF_TN, D), lambda j, i, kk: (2 * j + off, 0))
    return _mm_fused(
        [(hn, _blk(tm, D)), (wgu_t, wspec(0)), (wgu_t, wspec(1))], [(0, 1), (0, 2)], nt=True, m=tp, n=D_FF, tm=tm, tn=FF_TN,
        outs=[(jax.ShapeDtypeStruct((tp, 2 * D_FF), BF), _blk(tm, 2 * FF_TN, lambda j, kk: j)),
              (jax.ShapeDtypeStruct((tp, D_FF), BF), _blk(tm, FF_TN, lambda j, kk: j))],
        epilogue=epilogue, name=name, sub=SUB_COLS)


def _ffn_down_loss(ff, wdown, h1, target, gf, name):
    tp = ff.shape[0]

    def epilogue(prods, in_refs, out_refs, i, cols):
        live = (i > 0).astype(F32)
        g = in_refs[4][...]
        x = prods[0] + in_refs[2][...]
        r = lax.rsqrt(jnp.mean(x * x, axis=1, keepdims=True) + EPS)
        xh = x * r
        e = xh * g - in_refs[3][...]
        loss_part = 0.5 * live * jnp.sum(jnp.mean(e * e, axis=1, keepdims=True), axis=0, keepdims=True)
        dout = e * (live / D)
        dg_part = jnp.sum(dout * xh, axis=0, keepdims=True)
        dxh = dout * g
        out_refs[0][...] = r * (dxh - xh * jnp.mean(dxh * xh, axis=1, keepdims=True))

        @pl.when(i == 0)
        def _():
            out_refs[1][...] = loss_part
            out_refs[2][...] = dg_part

        @pl.when(i > 0)
        def _():
            out_refs[1][...] += loss_part
            out_refs[2][...] += dg_part

    const = lambda shape: pl.BlockSpec(shape, lambda j, i, kk: (0,) * len(shape))
    return _mm_fused(
        [(ff, _blk(TM, D_FF)), (wdown, const((D_FF, D))), (h1, _blk(TM, D)),
         (target, _blk(TM, D, row=lambda i: jnp.maximum(i - 1, 0))), (gf, const((1, D)))],
        [(0, 1)], nt=False, m=tp, n=D, tm=TM, tn=D,
        outs=[(jax.ShapeDtypeStruct((tp, D), F32), _blk(TM, D)), (jax.ShapeDtypeStruct((1, 1), F32), const((1, 1))),
              (jax.ShapeDtypeStruct((1, D), F32), const((1, D)))],
        epilogue=epilogue, name=name)


def _ffn_d_hidden(dh2, wdown, gu, name):
    tp = dh2.shape[0]

    def epilogue(prods, in_refs, out_refs, i, cols):
        d = prods[0]
        g = in_refs[2][:, cols].astype(F32)
        u = in_refs[2][:, _cols_at(cols, FF_TN)].astype(F32)
        sg = _sigmoid(g)
        out_refs[0][:, cols] = (d * u * sg * (1.0 + g * (1.0 - sg))).astype(BF)
        out_refs[0][:, _cols_at(cols, FF_TN)] = (d * g * sg).astype(BF)

    return _mm_fused(
        [(dh2, _blk(TM, D)), (wdown, pl.BlockSpec((FF_TN, D), lambda j, i, kk: (j, 0))), (gu, _blk(TM, 2 * FF_TN, lambda j, kk: j))],
        [(0, 1)], nt=True, m=tp, n=D_FF, tm=TM, tn=FF_TN,
        outs=[(jax.ShapeDtypeStruct((tp, 2 * D_FF), BF), _blk(TM, 2 * FF_TN, lambda j, kk: j))],
        epilogue=epilogue, name=name, sub=SUB_COLS)[0]


def _ffn_d_in(dgu, wgu_t, h1, rstd, g2, dh2, name):
    tp = dgu.shape[0]
    nk = 2

    def epilogue(prods, in_refs, out_refs, i, cols):
        r = in_refs[3][...]
        xh = in_refs[2][...] * r
        dxn = prods[0]
        dxh = dxn * in_refs[4][...]
        out_refs[0][...] = r * (dxh - xh * jnp.mean(dxh * xh, axis=1, keepdims=True)) + in_refs[5][...]
        part = jnp.sum(dxn * xh, axis=0, keepdims=True)

        @pl.when(i == 0)
        def _():
            out_refs[1][...] = part

        @pl.when(i > 0)
        def _():
            out_refs[1][...] += part

    const = lambda shape: pl.BlockSpec(shape, lambda j, i, kk: (0,) * len(shape))
    return _mm_fused(
        [(dgu, pl.BlockSpec((TM, D_FF), lambda j, i, kk: (i, kk))), (wgu_t, pl.BlockSpec((D_FF, D), lambda j, i, kk: (kk, 0))),
         (h1, _blk(TM, D)), (rstd, _blk(TM, 1)), (g2, const((1, D))), (dh2, _blk(TM, D))],
        [(0, 1)], nt=False, m=tp, n=D, tm=TM, tn=D, nk=nk,
        outs=[(jax.ShapeDtypeStruct((tp, D), F32), _blk(TM, D)), (jax.ShapeDtypeStruct((1, D), F32), const((1, D)))],
        epilogue=epilogue, name=name)


def _branch_merge(y_m, y_g, wbm, wbg, pbig, name):
    tp = y_m.shape[0]

    def epilogue(prods, in_refs, out_refs, i, cols):
        pm, pg = prods[0].astype(BF), prods[1].astype(BF)
        out_refs[0][:, cols] = pm
        out_refs[1][:, cols] = pg
        out_refs[2][:, cols] = (_sigmoid(in_refs[4][:, cols].astype(F32)) * pm.astype(F32)
                                + _sigmoid(in_refs[5][:, cols].astype(F32)) * pg.astype(F32)).astype(BF)

    const = lambda shape: pl.BlockSpec(shape, lambda j, i, kk: (0,) * len(shape))
    shp = jax.ShapeDtypeStruct((tp, D), BF)
    return _mm_fused(
        [(y_m, _blk(TM, D)), (wbm, const((D, D))), (y_g, _blk(TM, D)), (wbg, const((D, D))),
         (pbig, _blk(TM, D, lambda j, kk: CB_GM)), (pbig, _blk(TM, D, lambda j, kk: CB_GG))],
        [(0, 1), (2, 3)], nt=False, m=tp, n=D, tm=TM, tn=D,
        outs=[(shp, _blk(TM, D)), (shp, _blk(TM, D)), (shp, _blk(TM, D))], epilogue=epilogue, name=name, sub=SUB_COLS)


def _merge_d(dh1, wout, pm, pg, pbig, name):
    tp = dh1.shape[0]

    def epilogue(prods, in_refs, out_refs, i, cols):
        d = prods[0]
        sm = _sigmoid(in_refs[4][:, cols].astype(F32))
        sg = _sigmoid(in_refs[5][:, cols].astype(F32))
        out_refs[0][:, cols] = (d * sm).astype(BF)
        out_refs[1][:, cols] = (d * sg).astype(BF)
        out_refs[2][:, cols] = (d * in_refs[2][:, cols].astype(F32) * sm * (1.0 - sm)).astype(BF)
        out_refs[2][:, _cols_at(cols, D)] = (d * in_refs[3][:, cols].astype(F32) * sg * (1.0 - sg)).astype(BF)

    const = lambda shape: pl.BlockSpec(shape, lambda j, i, kk: (0,) * len(shape))
    shp = jax.ShapeDtypeStruct((tp, D), BF)
    return _mm_fused(
        [(dh1, _blk(TM, D)), (wout, const((D, D))), (pm, _blk(TM, D)), (pg, _blk(TM, D)),
         (pbig, _blk(TM, D, lambda j, kk: CB_GM)), (pbig, _blk(TM, D, lambda j, kk: CB_GG))],
        [(0, 1)], nt=True, m=tp, n=D, tm=TM, tn=D,
        outs=[(shp, _blk(TM, D)), (shp, _blk(TM, D)),
              (jax.ShapeDtypeStruct((tp, N_ALL), BF), _blk(TM, 2 * D, lambda j, kk: CB_GM // 2))],
        epilogue=epilogue, name=name, sub=SUB_COLS)


def _out_proj_norm(merged, wout, h0, g2, name):
    tp = merged.shape[0]
    tm = _mm_rows(tp)

    def epilogue(prods, in_refs, out_refs, i, cols):
        x = prods[0] + in_refs[2][...]
        r = lax.rsqrt(jnp.mean(x * x, axis=1, keepdims=True) + EPS)
        out_refs[0][...] = x
        out_refs[1][...] = (x * r * in_refs[3][...]).astype(BF)
        out_refs[2][...] = r

    const = lambda shape: pl.BlockSpec(shape, lambda j, i, kk: (0,) * len(shape))
    return _mm_fused(
        [(merged, _blk(tm, D)), (wout, const((D, D))), (h0, _blk(tm, D)), (g2, const((1, D)))],
        [(0, 1)], nt=False, m=tp, n=D, tm=tm, tn=D,
        outs=[(jax.ShapeDtypeStruct((tp, D), F32), _blk(tm, D)), (jax.ShapeDtypeStruct((tp, D), BF), _blk(tm, D)),
              (jax.ShapeDtypeStruct((tp, 1), F32), _blk(tm, 1))],
        epilogue=epilogue, name=name)


def _adamw(w, g, m, v, name):
    rows, cols = w.shape
    by_cols = rows % 128 != 0 and cols % 128 == 0 and rows * cols > 128 * 1024
    tr = rows if (by_cols or rows % 128 != 0) else 128
    tc = 128 if by_cols else cols

    def body(w_ref, g_ref, m_ref, v_ref, d_ref, nm_ref, nv_ref):
        gv = g_ref[...]
        nm = ADAM_B1 * m_ref[...] + (1.0 - ADAM_B1) * gv
        nv = ADAM_B2 * v_ref[...] + (1.0 - ADAM_B2) * (gv * gv)
        m_hat = nm / (1.0 - ADAM_B1 ** ADAM_STEP)
        v_hat = nv / (1.0 - ADAM_B2 ** ADAM_STEP)
        d_ref[...] = -ADAM_LR * (m_hat / (jnp.sqrt(v_hat) + ADAM_EPS) + ADAM_WD * w_ref[...])
        nm_ref[...] = nm
        nv_ref[...] = nv

    spec = pl.BlockSpec((tr, tc), (lambda i: (0, i)) if by_cols else (lambda i: (i, 0)))
    shp = jax.ShapeDtypeStruct((rows, cols), F32)
    return pl.pallas_call(body, grid=(cols // tc if by_cols else rows // tr,), in_specs=[spec] * 4, out_specs=[spec] * 3,
                          out_shape=(shp,) * 3, compiler_params=_params(), name=name)(w, g, m, v)


def _place_small(dsmall, dproj, name):
    tp = dsmall.shape[0]

    def body(s_ref, _, o_ref):
        o_ref[...] = s_ref[...]

    return pl.pallas_call(
        body, grid=(tp // TM,), in_specs=[_row_spec(N_SMALL), ANY], out_specs=_row_spec(N_SMALL, N_BIG // N_SMALL),
        out_shape=jax.ShapeDtypeStruct(dproj.shape, dproj.dtype), input_output_aliases={1: 0},
        compiler_params=_params(), name=name)(dsmall, dproj)


def _row_tile(rows, cap=512):
    best = rows
    for cand in range(8, min(rows, cap) + 1, 8):
        if rows % cand == 0:
            best = cand
    return best


def _add2(a, b, out_dtype, name):
    rows, cols = a.shape
    tr = _row_tile(rows)

    def body(a_ref, b_ref, o_ref):
        o_ref[...] = (a_ref[...] + b_ref[...]).astype(o_ref.dtype)

    spec = pl.BlockSpec((tr, cols), lambda i: (i, 0))
    return pl.pallas_call(body, grid=(rows // tr,), in_specs=[spec] * 2, out_specs=spec,
                          out_shape=jax.ShapeDtypeStruct((rows, cols), out_dtype), compiler_params=_params(), name=name)(a, b)


def _add4(first, rest, name):
    rows, cols = first.shape
    tr = _row_tile(rows, 256)

    def body(f_ref, r_ref, o_ref):
        up = lambda v: v.astype(F32)
        o_ref[...] = ((up(f_ref[...]) + up(r_ref[0])) + up(r_ref[1])) + up(r_ref[2])

    return pl.pallas_call(body, grid=(rows // tr,),
                          in_specs=[pl.BlockSpec((tr, cols), lambda i: (i, 0)), pl.BlockSpec((3, tr, cols), lambda i: (0, i, 0))],
                          out_specs=pl.BlockSpec((tr, cols), lambda i: (i, 0)),
                          out_shape=jax.ShapeDtypeStruct((rows, cols), F32), compiler_params=_params(), name=name)(first, rest)


def _chunk_consts(length=CHUNK):
    r2 = lax.broadcasted_iota(jnp.int32, (length, length), 0)
    c2 = lax.broadcasted_iota(jnp.int32, (length, length), 1)
    tri = r2 >= c2
    return dict(tri=tri, tril_f=tri.astype(F32), triu_f=(r2 <= c2).astype(F32),
                lane=lax.broadcasted_iota(jnp.int32, (length, N_SMALL), 1),
                rowio=lax.broadcasted_iota(jnp.int32, (length, 1), 0),
                ones=jnp.ones((length, N_SMALL), F32))


def _valid_rows(block, c):
    row = block * TM + c * CHUNK + lax.broadcasted_iota(jnp.int32, (CHUNK, 1), 0)
    return row >= FIRST_VALID


def _col(x, lane, idx):
    return jnp.sum(jnp.where(lane == idx, x, 0.0), axis=1, keepdims=True)


def _last_row(x, rowio):
    return jnp.sum(jnp.where(rowio == rowio.shape[0] - 1, x, 0.0), axis=0, keepdims=True)


def _sum_all(x):
    return jnp.sum(jnp.sum(x, axis=1, keepdims=True), axis=0, keepdims=True)


def _headnorm_fwd(hm, gain, gate_act):
    rs = lax.rsqrt(jnp.mean(hm * hm, axis=1, keepdims=True) + EPS)
    return hm * rs * gain * gate_act


def _headnorm_bwd(dy, hm, gain, gate_act):
    rs = lax.rsqrt(jnp.mean(hm * hm, axis=1, keepdims=True) + EPS)
    xh = hm * rs
    dact = dy * xh * gain
    dgain = jnp.sum(dy * gate_act * xh, axis=0, keepdims=True)
    dxh = dy * gate_act * gain
    dhm = rs * (dxh - xh * jnp.mean(dxh * xh, axis=1, keepdims=True))
    return dhm, dact, dgain


def _mlstm_gates(sm, gbias, valid, k):
    pre = sm + gbias
    lf = jnp.where(valid, _logsig(pre), 0.0)
    b_all = _nn(k["tril_f"], lf, precision=HI)
    li_all = jnp.where(valid, pre, NEG)
    return pre, li_all, b_all


def _mlstm_open(h, qh, kh, c_st, li_all, b_all, k):
    lane = k["lane"]
    sel = jnp.where(lane == h, 1.0, 0.0) - jnp.where(lane == NH + h, 1.0, 0.0)
    x = jnp.where(lane < NH, li_all, jnp.where(lane < 2 * NH, b_all, 0.0))
    cb = c_st.astype(BF)
    return dict(ubc=_nt(sel, x, precision=HI), sim=_nt(qh, kh), cb=cb, cq=_nt(qh, cb))


def _mlstm_weights(h, f, qh, vh, li_all, b_all, n_row, m11, k):
    lane, tri, rowio = k["lane"], k["tri"], k["rowio"]
    b_col = _col(b_all, lane, NH + h)
    li_col = _col(li_all, lane, h)
    dmat = jnp.where(tri, b_col + f["ubc"], NEG)
    m_row = jnp.maximum(b_col + m11, jnp.max(dmat, axis=1, keepdims=True))
    e = jnp.exp(dmat - m_row)
    w_mat = e * f["sim"]
    a = jnp.exp(b_col + m11 - m_row)
    qf = qh.astype(F32)
    nq = jnp.sum(qf * n_row, axis=1, keepdims=True)
    g = _last_row(b_col, rowio)
    wlog = g - b_col + li_col
    m_new = jnp.maximum(g + m11, jnp.max(wlog, axis=0, keepdims=True))
    a_s = jnp.exp(g + m11 - m_new)
    w = jnp.exp(wlog - m_new)
    return dict(f, e=e, w_mat=w_mat, a=a, qf=qf, nq=nq, m_row=m_row, m_new=m_new, a_s=a_s, w=w,
                wv=_nn(w_mat.astype(BF), vh))


def _mlstm_out(f):
    num = f["a"] * f["cq"] + f["wv"]
    den = f["a"] * f["nq"] + jnp.sum(f["w_mat"], axis=1, keepdims=True)
    floor = jnp.exp(-f["m_row"])
    r = jnp.maximum(jnp.abs(den), floor)
    return dict(f, den=den, floor=floor, r=r, hm=num / r)


def _mlstm_fwd(qk, pbig, small, gbias, headg, name):
    tp = qk.shape[0]
    nb = tp // TM

    def body(qk_ref, v_ref, mo_ref, sm_ref, gb_ref, hg_ref, y_ref, cs_ref, ns_ref, c_scr, n_scr):
        blk = pl.program_id(0)

        @pl.when(blk == 0)
        def _():
            c_scr[...] = jnp.zeros_like(c_scr)
            n_scr[...] = jnp.zeros_like(n_scr)

        k = _chunk_consts()
        io8 = lax.broadcasted_iota(jnp.int32, (8, DQK), 0)

        def chunk(c, carry):
            r0 = pl.multiple_of(c * CHUNK, CHUNK)
            rows = pl.ds(r0, CHUNK)
            valid = _valid_rows(blk, c)
            _, li_all, b_all = _mlstm_gates(sm_ref[rows, :], gb_ref[...], valid, k)
            heads = range(NH)
            qs = [qk_ref[rows, h * DQK:(h + 1) * DQK] for h in heads]
            ks = [qk_ref[rows, NH * DQK + h * DQK:NH * DQK + (h + 1) * DQK] for h in heads]
            vs = [v_ref[rows, h * DV:(h + 1) * DV] for h in heads]
            cst = [c_scr[h] for h in heads]
            nrow = [n_scr[h, 0:1, :] for h in heads]
            m11 = [jnp.max(n_scr[h, 1:2, :], axis=1, keepdims=True) for h in heads]
            f = [_mlstm_open(h, qs[h], ks[h], cst[h], li_all, b_all, k) for h in heads]
            f = [_mlstm_weights(h, f[h], qs[h], vs[h], li_all, b_all, nrow[h], m11[h], k) for h in heads]
            wk = [f[h]["w"] * ks[h].astype(F32) for h in heads]
            kv = [_tn(vs[h], wk[h].astype(BF)) for h in heads]
            for h in heads:
                hm = _mlstm_out(f[h])["hm"]
                gate = _sigmoid(mo_ref[rows, h * DV:(h + 1) * DV].astype(F32))
                y_ref[rows, h * DV:(h + 1) * DV] = _headnorm_fwd(hm, hg_ref[:, h * DV:(h + 1) * DV], gate).astype(BF)
                cs_ref[c, h] = f[h]["cb"]
                ns_ref[c, h] = jnp.where(io8 == 0, nrow[h], jnp.where(io8 == 1, m11[h], 0.0))
                c_scr[h] = f[h]["a_s"] * cst[h] + kv[h]
                n_scr[h, 0:1, :] = f[h]["a_s"] * nrow[h] + jnp.sum(wk[h], axis=0, keepdims=True)
                n_scr[h, 1:2, :] = jnp.broadcast_to(f[h]["m_new"], (1, DQK))
            return carry

        lax.fori_loop(0, CPB, chunk, 0, unroll=2)

    return pl.pallas_call(
        body, grid=(nb,),
        in_specs=[_row_spec(D), _row_spec(D, CB_MV), _row_spec(D, CB_MO), _row_spec(N_SMALL), _full_spec((1, N_SMALL)), _full_spec((1, D))],
        out_specs=[_row_spec(D), pl.BlockSpec((CPB, NH, DV, DQK), lambda i: (i, 0, 0, 0)),
                   pl.BlockSpec((CPB, NH, 8, DQK), lambda i: (i, 0, 0, 0))],
        out_shape=(jax.ShapeDtypeStruct((tp, D), BF), jax.ShapeDtypeStruct((tp // CHUNK, NH, DV, DQK), BF),
                   jax.ShapeDtypeStruct((tp // CHUNK, NH, 8, DQK), F32)),
        scratch_shapes=[pltpu.VMEM((NH, DV, DQK), F32), pltpu.VMEM((NH, 8, DQK), F32)],
        compiler_params=_params(), name=name)(qk, pbig, pbig, small, gbias, headg)


def _mlstm_bwd(dy, qk, pbig, small, gbias, headg, cs, ns, dproj, name, ride=()):
    tp = qk.shape[0]
    nb = tp // TM
    nr = len(ride)

    def body(*refs):
        dy_ref, qk_ref, v_ref, mo_ref, sm_ref, gb_ref, hg_ref, cs_ref, ns_ref = refs[:9]
        ride_in = refs[10:10 + nr]
        dqk_ref, dproj_ref, dsm_ref, dgb_ref, dhg_ref = refs[10 + nr:15 + nr]
        ride_out = refs[15 + nr:15 + 2 * nr]
        dc_scr, dn_scr = refs[15 + 2 * nr:17 + 2 * nr]
        step = pl.program_id(0)
        blk = nb - 1 - step
        sent = _scatter_copies(ride_in, ride_out, *refs[17 + 2 * nr:]) if nr else []

        @pl.when(step == 0)
        def _():
            dc_scr[...] = jnp.zeros_like(dc_scr)
            dn_scr[...] = jnp.zeros_like(dn_scr)
            dgb_ref[...] = jnp.zeros_like(dgb_ref)
            dhg_ref[...] = jnp.zeros_like(dhg_ref)
            for cp in sent:
                cp.start()

        k = _chunk_consts()
        lane, rowio = k["lane"], k["rowio"]

        def chunk(cc, carry):
            c = CPB - 1 - cc
            r0 = pl.multiple_of(c * CHUNK, CHUNK)
            rows = pl.ds(r0, CHUNK)
            valid = _valid_rows(blk, c)
            pre, li_all, b_all = _mlstm_gates(sm_ref[rows, :], gb_ref[...], valid, k)
            dli_all = jnp.zeros((CHUNK, N_SMALL), F32)
            db_all = jnp.zeros((CHUNK, N_SMALL), F32)
            heads = range(NH)
            qs = [qk_ref[rows, h * DQK:(h + 1) * DQK] for h in heads]
            ks = [qk_ref[rows, NH * DQK + h * DQK:NH * DQK + (h + 1) * DQK] for h in heads]
            vs = [v_ref[rows, h * DV:(h + 1) * DV] for h in heads]
            cst = [cs_ref[c, h].astype(F32) for h in heads]
            nrow = [ns_ref[c, h, 0:1, :] for h in heads]
            m11 = [jnp.max(ns_ref[c, h, 1:2, :], axis=1, keepdims=True) for h in heads]
            f = [_mlstm_open(h, qs[h], ks[h], cst[h], li_all, b_all, k) for h in heads]
            f = [_mlstm_weights(h, f[h], qs[h], vs[h], li_all, b_all, nrow[h], m11[h], k) for h in heads]
            f = [_mlstm_out(f[h]) for h in heads]
            t = []
            for h in heads:
                gain = hg_ref[:, h * DV:(h + 1) * DV]
                gate = _sigmoid(mo_ref[rows, h * DV:(h + 1) * DV].astype(F32))
                dhm, dgate, dgain = _headnorm_bwd(dy_ref[rows, h * DV:(h + 1) * DV].astype(F32), f[h]["hm"], gain, gate)
                dproj_ref[rows, D + h * DV:D + (h + 1) * DV] = (dgate * gate * (1.0 - gate)).astype(BF)
                dhg_ref[:, h * DV:(h + 1) * DV] += dgain
                r, den = f[h]["r"], f[h]["den"]
                dnum = dhm / r
                dr = -jnp.sum(dhm * f[h]["hm"], axis=1, keepdims=True) / r
                dden = jnp.where(jnp.abs(den) > f[h]["floor"], dr * jnp.sign(den), 0.0)
                dnb = dnum.astype(BF)
                dc_new = dc_scr[h]
                dcb = dc_new.astype(BF)
                t.append(dict(dnum=dnum, dden=dden, dnb=dnb, dc_new=dc_new, dn_new=dn_scr[h],
                              dwm=_nt(dnb, vs[h]), vdc=_nn(vs[h], dcb), kdc=_nt(ks[h], dcb)))
            for h in heads:
                dw_mat = t[h]["dwm"] + t[h]["dden"]
                dsim = (f[h]["e"] * dw_mat).astype(BF)
                gm = f[h]["w_mat"] * dw_mat
                t[h].update(gm=gm, dv0=_tn(f[h]["w_mat"].astype(BF), t[h]["dnb"]), dq0=_nn(dsim, ks[h]),
                            dq1=_nn(t[h]["dnb"], f[h]["cb"]), dk0=_tn(dsim, qs[h]),
                            dcq=_tn((f[h]["a"] * t[h]["dnum"]).astype(BF), qs[h]), cs2=_tn(gm, k["ones"], precision=HI))
            for h in heads:
                a, w, a_s = f[h]["a"], f[h]["w"], f[h]["a_s"]
                dnum, dden, dc_new, dn_new, vdc, gm = (t[h][n] for n in ("dnum", "dden", "dc_new", "dn_new", "vdc", "gm"))
                kf = ks[h].astype(F32)
                dproj_ref[rows, h * DV:(h + 1) * DV] = (t[h]["dv0"] + w * t[h]["kdc"]).astype(BF)
                adden = a * dden
                dqk_ref[rows, h * DQK:(h + 1) * DQK] = t[h]["dq0"] + a * t[h]["dq1"] + adden * nrow[h]
                dqk_ref[rows, NH * DQK + h * DQK:NH * DQK + (h + 1) * DQK] = t[h]["dk0"] + w * vdc + w * dn_new
                da = jnp.sum(dnum * f[h]["cq"], axis=1, keepdims=True) + dden * f[h]["nq"]
                dw = jnp.sum(vdc * kf, axis=1, keepdims=True) + jnp.sum(kf * dn_new, axis=1, keepdims=True)
                da_s = _sum_all(dc_new * cst[h]) + jnp.sum(dn_new * nrow[h], axis=1, keepdims=True)
                wdw = w * dw
                rs = jnp.sum(gm, axis=1, keepdims=True)
                cs_col = _col(t[h]["cs2"], lane, 0)
                dg = a_s * da_s + jnp.sum(wdw, axis=0, keepdims=True)
                db = a * da + rs - cs_col - wdw + jnp.where(rowio == CHUNK - 1, dg, 0.0)
                dli_all = dli_all + jnp.where(lane == h, cs_col + wdw, 0.0)
                db_all = db_all + jnp.where(lane == NH + h, db, 0.0)
                dc_scr[h] = a_s * dc_new + t[h]["dcq"]
                dn_scr[h] = a_s * dn_new + jnp.sum(adden * f[h]["qf"], axis=0, keepdims=True)
            dlf_all = _nn(k["triu_f"], db_all, precision=HI)
            dsm = jnp.where(valid, dli_all + dlf_all * _sigmoid(-pre), 0.0)
            dsm = jnp.where(lane < 2 * NH, dsm, 0.0)
            dsm_ref[rows, :] = dsm
            dgb_ref[0:1, :] += jnp.sum(dsm, axis=0, keepdims=True)
            return carry

        lax.fori_loop(0, CPB, chunk, 0, unroll=2)

        if nr:
            @pl.when(step == nb - 1)
            def _():
                for cp in sent:
                    cp.wait_recv()
                for cp in sent:
                    cp.wait_send()

    rev = lambda col: (lambda i: (nb - 1 - i, col))
    rspec = lambda width, col=0: pl.BlockSpec((TM, width), rev(col))
    ride_shapes, ride_sems = _scatter_shapes(ride) if nr else ((), [])
    outs = pl.pallas_call(
        body, grid=(nb,),
        in_specs=[rspec(D), rspec(D), rspec(D, CB_MV), rspec(D, CB_MO), rspec(N_SMALL), _full_spec((1, N_SMALL)), _full_spec((1, D)),
                  pl.BlockSpec((CPB, NH, DV, DQK), lambda i: (nb - 1 - i, 0, 0, 0)),
                  pl.BlockSpec((CPB, NH, 8, DQK), lambda i: (nb - 1 - i, 0, 0, 0)), ANY] + [ANY] * nr,
        out_specs=[rspec(D), rspec(2 * D, CB_MV // 2), rspec(N_SMALL), _full_spec((8, N_SMALL)), _full_spec((1, D))] + [ANY] * nr,
        out_shape=(jax.ShapeDtypeStruct((tp, D), F32), jax.ShapeDtypeStruct(dproj.shape, BF),
                   jax.ShapeDtypeStruct((tp, N_SMALL), F32), jax.ShapeDtypeStruct((8, N_SMALL), F32),
                   jax.ShapeDtypeStruct((1, D), F32)) + tuple(ride_shapes),
        scratch_shapes=[pltpu.VMEM((NH, DV, DQK), F32), pltpu.VMEM((NH, 1, DQK), F32)] + ride_sems,
        input_output_aliases={9: 1}, compiler_params=_params(), name=name)(dy, qk, pbig, pbig, small, gbias, headg, cs, ns, dproj, *ride)
    return tuple(outs[:5]) + (list(outs[5:]),)


def _gla_loga(sm_ref, a2_ref, a2b_ref, blk):
    za = _nn(sm_ref[...].astype(BF), a2_ref[...]) + a2b_ref[...]
    row = blk * TM + lax.broadcasted_iota(jnp.int32, (TM, 1), 0)
    return za, jnp.where(row >= FIRST_VALID, _logsig(za) / G_TAU, 0.0)


def _gla_head(h, q_ref, k_ref, rows, bc, btot, k):
    sl = slice(h * DQK, (h + 1) * DQK)
    bch = bc[:, sl]
    bth = btot[:, sl]
    gq = q_ref[rows, h * DQK:(h + 1) * DQK].astype(F32)
    gk = k_ref[rows, NH * DQK + h * DQK:NH * DQK + (h + 1) * DQK].astype(F32)
    e_pos = jnp.exp(bch) * (DQK ** -0.5)
    e_neg = jnp.exp(-bch)
    e_end = jnp.exp(bth - bch)
    qd = gq * e_pos
    ki = gk * e_neg
    ke = gk * e_end
    att = jnp.where(k["tri"], _nt(qd.astype(BF), ki.astype(BF)), 0.0)
    return dict(e_pos=e_pos, e_neg=e_neg, e_end=e_end, qd=qd, ki=ki, ke=ke, att=att, decay=jnp.exp(bth))


def _gla_fwd(pbig, small, a2p, a2b, headg, name):
    tp = pbig.shape[0]
    nb = tp // TM

    def body(qk_ref, v_ref, gr_ref, sm_ref, a2_ref, a2b_ref, hg_ref, y_ref, ss_ref, s_scr, lg_scr):
        blk = pl.program_id(0)

        @pl.when(blk == 0)
        def _():
            s_scr[...] = jnp.zeros_like(s_scr)

        k = _chunk_consts(G_CHUNK)
        _, loga = _gla_loga(sm_ref, a2_ref, a2b_ref, blk)
        lg_scr[...] = loga

        def chunk(c, carry):
            r0 = pl.multiple_of(c * G_CHUNK, G_CHUNK)
            rows = pl.ds(r0, G_CHUNK)
            bc = _nn(k["tril_f"], lg_scr[rows, :], precision=HI)
            btot = _last_row(bc, k["rowio"])
            heads = range(NH)
            f = [_gla_head(h, qk_ref, qk_ref, rows, bc, btot, k) for h in heads]
            vs = [v_ref[rows, h * DV:(h + 1) * DV] for h in heads]
            sst = [s_scr[h] for h in heads]
            sbs = [s.astype(BF) for s in sst]
            inter = [_nt(f[h]["qd"].astype(BF), sbs[h]) for h in heads]
            intra = [_nn(f[h]["att"].astype(BF), vs[h]) for h in heads]
            kv = [_tn(vs[h], f[h]["ke"].astype(BF)) for h in heads]
            for h in heads:
                gr = gr_ref[rows, h * DV:(h + 1) * DV].astype(F32)
                y_ref[rows, h * DV:(h + 1) * DV] = _headnorm_fwd(intra[h] + inter[h], hg_ref[:, h * DV:(h + 1) * DV],
                                                                   gr * _sigmoid(gr)).astype(BF)
                ss_ref[c, h] = sbs[h]
                s_scr[h] = sst[h] * f[h]["decay"] + kv[h]
            return carry

        lax.fori_loop(0, G_CPB, chunk, 0, unroll=2)

    return pl.pallas_call(
        body, grid=(nb,),
        in_specs=[_row_spec(D, CB_GQK), _row_spec(D, CB_GV), _row_spec(D, CB_GR), _row_spec(N_SMALL),
                  _full_spec((N_SMALL, NH * DQK)), _full_spec((1, NH * DQK)), _full_spec((1, D))],
        out_specs=[_row_spec(D), pl.BlockSpec((G_CPB, NH, DV, DQK), lambda i: (i, 0, 0, 0))],
        out_shape=(jax.ShapeDtypeStruct((tp, D), BF), jax.ShapeDtypeStruct((tp // G_CHUNK, NH, DV, DQK), BF)),
        scratch_shapes=[pltpu.VMEM((NH, DV, DQK), F32), pltpu.VMEM((TM, NH * DQK), F32)],
        compiler_params=_params(), name=name)(pbig, pbig, pbig, small, a2p, a2b, headg)


def _gla_bwd(dy, pbig, small, a2p, a2b, headg, ss, dsm_m, dproj, name):
    tp = pbig.shape[0]
    nb = tp // TM
    nqk = NH * DQK

    def body(dy_ref, qk_ref, v_ref, gr_ref, sm_ref, a2_ref, a2b_ref, hg_ref, ss_ref, dsmm_ref, _,
             dproj_ref, dsm_ref, da2_ref, da2b_ref, dhg_ref, ds_scr, lg_scr, dza_scr):
        step = pl.program_id(0)
        blk = nb - 1 - step

        @pl.when(step == 0)
        def _():
            ds_scr[...] = jnp.zeros_like(ds_scr)
            da2_ref[...] = jnp.zeros_like(da2_ref)
            da2b_ref[...] = jnp.zeros_like(da2b_ref)
            dhg_ref[...] = jnp.zeros_like(dhg_ref)

        k = _chunk_consts(G_CHUNK)
        rowio = k["rowio"]
        za, loga = _gla_loga(sm_ref, a2_ref, a2b_ref, blk)
        lg_scr[...] = loga

        def chunk(cc, carry):
            c = G_CPB - 1 - cc
            r0 = pl.multiple_of(c * G_CHUNK, G_CHUNK)
            rows = pl.ds(r0, G_CHUNK)
            bc = _nn(k["tril_f"], lg_scr[rows, :], precision=HI)
            btot = _last_row(bc, rowio)
            heads = range(NH)
            f = [_gla_head(h, qk_ref, qk_ref, rows, bc, btot, k) for h in heads]
            vs = [v_ref[rows, h * DV:(h + 1) * DV] for h in heads]
            sbs = [ss_ref[c, h] for h in heads]
            qdb = [f[h]["qd"].astype(BF) for h in heads]
            attb = [f[h]["att"].astype(BF) for h in heads]
            inter = [_nt(qdb[h], sbs[h]) for h in heads]
            intra = [_nn(attb[h], vs[h]) for h in heads]
            dsn = [ds_scr[h] for h in heads]
            dsb = [d.astype(BF) for d in dsn]
            dke = [_nn(vs[h], dsb[h]) for h in heads]
            dv1 = [_nt(f[h]["ke"].astype(BF), dsb[h]) for h in heads]
            t = []
            for h in heads:
                gr = gr_ref[rows, h * DV:(h + 1) * DV].astype(F32)
                sg = _sigmoid(gr)
                gain = hg_ref[:, h * DV:(h + 1) * DV]
                do, dact, dgain = _headnorm_bwd(dy_ref[rows, h * DV:(h + 1) * DV].astype(F32), intra[h] + inter[h], gain, gr * sg)
                dproj_ref[rows, 2 * D + h * DV:2 * D + (h + 1) * DV] = (dact * sg * (1.0 + gr * (1.0 - sg))).astype(BF)
                dhg_ref[:, h * DV:(h + 1) * DV] += dgain
                dob = do.astype(BF)
                t.append(dict(dob=dob, datt=_nt(dob, vs[h]), dv0=_tn(attb[h], dob), dq1=_nn(dob, sbs[h]), dsq=_tn(dob, qdb[h])))
            for h in heads:
                datt = jnp.where(k["tri"], t[h]["datt"], 0.0).astype(BF)
                t[h].update(dq0=_nn(datt, f[h]["ki"].astype(BF)), dki=_tn(datt, qdb[h]))
            dbc_parts = []
            for h in heads:
                dqd = t[h]["dq0"] + t[h]["dq1"]
                dki = t[h]["dki"]
                dproj_ref[rows, D + h * DV:D + (h + 1) * DV] = (t[h]["dv0"] + dv1[h]).astype(BF)
                dproj_ref[rows, h * DQK:(h + 1) * DQK] = (dqd * f[h]["e_pos"]).astype(BF)
                dproj_ref[rows, nqk + h * DQK:nqk + (h + 1) * DQK] = (dki * f[h]["e_neg"] + dke[h] * f[h]["e_end"]).astype(BF)
                dke_ke = dke[h] * f[h]["ke"]
                dbtot = (jnp.sum(dke_ke, axis=0, keepdims=True)
                         + jnp.sum(dsn[h] * sbs[h].astype(F32), axis=0, keepdims=True) * f[h]["decay"])
                dbc_parts.append(dqd * f[h]["qd"] - dki * f[h]["ki"] - dke_ke + jnp.where(rowio == G_CHUNK - 1, dbtot, 0.0))
                ds_scr[h] = dsn[h] * f[h]["decay"] + t[h]["dsq"]
            dbc = jnp.concatenate(dbc_parts, axis=1)
            dza_scr[rows, :] = _nn(k["triu_f"], dbc, precision=HI)
            return carry

        lax.fori_loop(0, G_CPB, chunk, 0, unroll=2)
        row = blk * TM + lax.broadcasted_iota(jnp.int32, (TM, 1), 0)
        dza = jnp.where(row >= FIRST_VALID, dza_scr[...] * (_sigmoid(-za) / G_TAU), 0.0)
        dzb = dza.astype(BF)
        dsm_ref[...] = (_nt(dzb, a2_ref[...]) + dsmm_ref[...]).astype(BF)
        da2_ref[...] += _tn(sm_ref[...].astype(BF), dzb)
        da2b_ref[...] += jnp.sum(dza, axis=0, keepdims=True)

    rspec = lambda width, col=0: pl.BlockSpec((TM, width), lambda i: (nb - 1 - i, col))
    return pl.pallas_call(
        body, grid=(nb,),
        in_specs=[rspec(D), rspec(D, CB_GQK), rspec(D, CB_GV), rspec(D, CB_GR), rspec(N_SMALL),
                  _full_spec((N_SMALL, nqk)), _full_spec((1, nqk)), _full_spec((1, D)),
                  pl.BlockSpec((G_CPB, NH, DV, DQK), lambda i: (nb - 1 - i, 0, 0, 0)), rspec(N_SMALL), ANY],
        out_specs=[rspec(3 * D, 0), rspec(N_SMALL), _full_spec((N_SMALL, nqk)), _full_spec((1, nqk)), _full_spec((1, D))],
        out_shape=(jax.ShapeDtypeStruct(dproj.shape, BF),
                   jax.ShapeDtypeStruct((tp, N_SMALL), BF), jax.ShapeDtypeStruct((N_SMALL, nqk), F32),
                   jax.ShapeDtypeStruct((1, nqk), F32), jax.ShapeDtypeStruct((1, D), F32)),
        scratch_shapes=[pltpu.VMEM((NH, DV, DQK), F32), pltpu.VMEM((TM, nqk), F32), pltpu.VMEM((TM, nqk), F32)],
        input_output_aliases={10: 0}, compiler_params=_params(), name=name)(dy, pbig, pbig, pbig, small, a2p, a2b, headg, ss, dsm_m, dproj)


PIECE_BYTES = 1 << 20
MAX_PIECES = 32


def _place():
    return lax.axis_index("x"), lax.axis_index("y"), lax.axis_index("c")


def _piece_rows(rows, row_bytes, align):
    want = min(MAX_PIECES, max(1, -(-rows * row_bytes // PIECE_BYTES)))
    best = rows
    for k in range(1, want + 1):
        if rows % k == 0 and (rows // k) % align == 0:
            best = rows // k
    return best


def _remote(src, dst, send_sems, recv_sems, k, to):
    return pltpu.make_async_remote_copy(src_ref=src, dst_ref=dst, send_sem=send_sems.at[k], recv_sem=recv_sems.at[k],
                                        device_id=to, device_id_type=MESH)


def _all_gather_chips(p, name):
    rd = _gather_rider(p)

    def body(*refs):
        start, middle, finish = rd["make"](refs[:1], refs[1:2], refs[2:])
        start()
        middle()
        finish()

    return pl.pallas_call(body, in_specs=[ANY], out_specs=[ANY], out_shape=rd["out_shapes"], scratch_shapes=rd["sems"],
                          name=name)(p)[0]


def _gather_rider(p):
    r, n = p.shape
    rh = r // 2
    align = 32 // p.dtype.itemsize
    assert r % (2 * align) == 0
    cr = _piece_rows(rh, n * p.dtype.itemsize, align)

    def make(in_refs, out_refs, sem_refs):
        p_ref, o_ref = in_refs[0], out_refs[0]
        send_sems, recv_sems = sem_refs
        x, y, c = _place()
        chips = [(1 - x, y), (x, 1 - y), (1 - x, 1 - y)]
        sib = (x, y, 1 - c)

        def half(hc, piece=None):
            if piece is None:
                return pl.ds(pl.multiple_of(hc * rh, align), rh)
            return pl.ds(pl.multiple_of(hc * rh + piece * cr, align), cr)

        first = [_remote(p_ref.at[half(c)], o_ref.at[j, half(c)], send_sems, recv_sems, j, (*chip, c))
                 for j, chip in enumerate(chips)]
        passed = [[_remote(o_ref.at[j, half(c, i)], o_ref.at[j, half(c, i)], send_sems, recv_sems, 3 + j, sib)
                   for i in range(rh // cr)] for j in range(3)]
        blocks = [_remote(o_ref.at[j, half(c)], o_ref.at[j, half(1 - c)], send_sems, recv_sems, 3 + j, sib) for j in range(3)]

        def start():
            for cp in first:
                cp.start()

        def middle():
            for j, cp in enumerate(first):
                cp.wait_recv()
                for piece in passed[j]:
                    piece.start()

        def finish():
            for block in blocks:
                block.wait_send()
                block.wait_recv()
            for cp in first:
                cp.wait_send()

        return start, middle, finish

    return dict(inputs=[p], out_shapes=(jax.ShapeDtypeStruct((3, r, n), p.dtype),),
                sems=[pltpu.SemaphoreType.DMA((6,)), pltpu.SemaphoreType.DMA((6,))], make=make)


def _scatter_rider(items):
    out_shapes, sems = _scatter_shapes(items)

    def make(in_refs, out_refs, sem_refs):
        sent = _scatter_copies(in_refs, out_refs, *sem_refs)

        def start():
            for cp in sent:
                cp.start()

        def finish():
            for cp in sent:
                cp.wait_recv()
            for cp in sent:
                cp.wait_send()

        return start, (lambda: None), finish

    return dict(inputs=list(items), out_shapes=out_shapes, sems=sems, make=make)


def _by_chip(mine, others):
    me = 2 * lax.axis_index("x") + lax.axis_index("y")
    by_mask = jnp.stack([mine, others[1], others[0], others[2]])
    return [lax.dynamic_index_in_dim(by_mask, q ^ me, 0, keepdims=False) for q in range(4)]


def _swap_halves(items, name):
    k = len(items)

    def body(*refs):
        a_refs, got_refs = refs[:k], refs[k:2 * k]
        send_sems, recv_sems = refs[2 * k:]
        x, y, c = _place()
        sib = (x, y, 1 - c)
        for i, a in enumerate(items):
            _, r, n = a.shape
            rh = r // 2
            cr = _piece_rows(rh, n * a.dtype.itemsize, 8)
            for q in range(4):
                for t in range(rh // cr):
                    other = pl.ds(pl.multiple_of((1 - c) * rh + t * cr, 8), cr)
                    _remote(a_refs[i].at[q, other], got_refs[i].at[q, pl.ds(t * cr, cr)], send_sems, recv_sems, i, sib).start()
        for i, a in enumerate(items):
            block = _remote(a_refs[i].at[:, pl.ds(0, a.shape[1] // 2)], got_refs[i], send_sems, recv_sems, i, sib)
            block.wait_send()
            block.wait_recv()

    return pl.pallas_call(
        body, in_specs=[ANY] * k, out_specs=[ANY] * k,
        out_shape=tuple(jax.ShapeDtypeStruct((4, a.shape[1] // 2, a.shape[2]), a.dtype) for a in items),
        scratch_shapes=[pltpu.SemaphoreType.DMA((k,)), pltpu.SemaphoreType.DMA((k,))], name=name)(*items)


def _scatter_copies(s_refs, o_refs, send_sems, recv_sems):
    x, y, c = _place()
    chips = [(1 - x, y), (x, 1 - y), (1 - x, 1 - y)]
    return [_remote(s_refs[i].at[2 * cx + cy], o_refs[i].at[j], send_sems, recv_sems, 3 * i + j, (cx, cy, c))
            for i in range(len(s_refs)) for j, (cx, cy) in enumerate(chips)]


def _scatter_shapes(items):
    k = len(items)
    return (tuple(jax.ShapeDtypeStruct((3,) + s.shape[1:], s.dtype) for s in items),
            [pltpu.SemaphoreType.DMA((3 * k,)), pltpu.SemaphoreType.DMA((3 * k,))])


def _scatter_chips(items, name):
    k = len(items)

    def body(*refs):
        sent = _scatter_copies(refs[:k], refs[k:2 * k], *refs[2 * k:])
        for cp in sent:
            cp.start()
        for cp in sent:
            cp.wait_recv()
        for cp in sent:
            cp.wait_send()

    out_shape, scratch = _scatter_shapes(items)
    return pl.pallas_call(body, in_specs=[ANY] * k, out_specs=[ANY] * k, out_shape=out_shape, scratch_shapes=scratch,
                          name=name)(*items)


def _join_halves(items, name):
    k = len(items)

    def body(*refs):
        f_refs, o_refs = refs[:k], refs[k:2 * k]
        send_sems, recv_sems = refs[2 * k:]
        x, y, c = _place()
        sib = (x, y, 1 - c)
        for i, f in enumerate(items):
            rh, n = f.shape
            cr = _piece_rows(rh, n * f.dtype.itemsize, 8)
            for t in range(rh // cr):
                rows = pl.ds(t * cr, cr)
                _remote(f_refs[i].at[rows], o_refs[i].at[rows], send_sems, recv_sems, i, sib).start()
        for i in range(k):
            block = _remote(f_refs[i], o_refs[i], send_sems, recv_sems, i, sib)
            block.wait_send()
            block.wait_recv()

    return pl.pallas_call(
        body, in_specs=[ANY] * k, out_specs=[ANY] * k, out_shape=tuple(jax.ShapeDtypeStruct(f.shape, f.dtype) for f in items),
        scratch_shapes=[pltpu.SemaphoreType.DMA((k,)), pltpu.SemaphoreType.DMA((k,))], name=name)(*items)


SMALL_ROWS = 16
SMALL_SHARD_SHAPES = [(N_META, 256), (4, 256), (G_RANK, 128), (NH, 64), (NH, 64)]
REPL_SHAPES = [(1, D), (1, D), (1, 2, NH), (1, NH * DQK), (1, D), (D,)]
W_IN_SHARD = 2054
W_IN_BLOCK = 2080


def _pack_small(parts):
    flat = jnp.concatenate([p.reshape(-1) for p in parts])
    return jnp.pad(flat, (0, SMALL_ROWS * D - flat.shape[0])).reshape(SMALL_ROWS, D)


def _unpack_small(block, shapes):
    flat, out, off = block.reshape(-1), [], 0
    for shp in shapes:
        n = 1
        for s in shp:
            n *= s
        out.append(flat[off:off + n].reshape(shp))
        off += n
    return out


def _proj_rows_from_w_in(w_in_t):
    w_big = jnp.concatenate([w_in_t[3080:5128], w_in_t[5144:6168], w_in_t[0:1024], w_in_t[6168:8216],
                             w_in_t[1024:2048], w_in_t[2056:3080]], axis=0)
    w_small = jnp.concatenate([w_in_t[2048:2056], w_in_t[5128:5144], jnp.zeros((N_SMALL - 24, D), w_in_t.dtype)], axis=0)
    return w_big, w_small


def _w_in_from_proj_rows(d_wall_t):
    big, small = d_wall_t[0:N_BIG], d_wall_t[N_BIG:N_ALL]
    return jnp.concatenate([big[3072:4096], big[6144:7168], small[0:8], big[7168:8192], big[0:2048],
                            small[8:24], big[2048:3072], big[4096:6144]], axis=0)


def kernel(x, meta_tokens, norm1_g, w_in, conv_w, conv_b, m_gate_b, g_a2, g_a2_b, m_head_g, g_head_g, w_branch_m, w_branch_g, w_out, norm2_g, w_ff_gate, w_ff_up, w_ff_down, final_g, loss_target, m_meta_tokens, m_norm1_g, m_w_in, m_conv_w, m_conv_b, m_m_gate_b, m_g_a2, m_g_a2_b, m_m_head_g, m_g_head_g, m_w_branch_m, m_w_branch_g, m_w_out, m_norm2_g, m_w_ff_gate, m_w_ff_up, m_w_ff_down, m_final_g, v_meta_tokens, v_norm1_g, v_w_in, v_conv_w, v_conv_b, v_m_gate_b, v_g_a2, v_g_a2_b, v_m_head_g, v_g_head_g, v_w_branch_m, v_w_branch_g, v_w_out, v_norm2_g, v_w_ff_gate, v_w_ff_up, v_w_ff_down, v_final_g):
    w = _gather_weights(w_in, w_branch_m, w_branch_g, w_out, w_ff_gate, w_ff_up, w_ff_down, meta_tokens, conv_w, g_a2, m_head_g, g_head_g)
    loss_local, dx, grads = _local_step(x[0], loss_target[0], w, norm1_g, conv_b, m_gate_b, g_a2_b, norm2_g, final_g, _Reducer())

    weights = [w_in, w_branch_m, w_branch_g, w_out, w_ff_gate, w_ff_up, w_ff_down, meta_tokens, conv_w, g_a2, m_head_g, g_head_g,
               norm1_g, conv_b, m_gate_b, g_a2_b, norm2_g, final_g]
    moms = [m_w_in, m_w_branch_m, m_w_branch_g, m_w_out, m_w_ff_gate, m_w_ff_up, m_w_ff_down, m_meta_tokens, m_conv_w, m_g_a2,
            m_m_head_g, m_g_head_g, m_norm1_g, m_conv_b, m_m_gate_b, m_g_a2_b, m_norm2_g, m_final_g]
    vels = [v_w_in, v_w_branch_m, v_w_branch_g, v_w_out, v_w_ff_gate, v_w_ff_up, v_w_ff_down, v_meta_tokens, v_conv_w, v_g_a2,
            v_m_head_g, v_g_head_g, v_norm1_g, v_conv_b, v_m_gate_b, v_g_a2_b, v_norm2_g, v_final_g]
    res = {}
    for nm, wt, g, m, v in zip(PACK_ORDER, weights, grads, moms, vels):
        if nm in TRANSPOSED_GRADS:
            to2d = lambda a: jnp.swapaxes(a, -1, -2).reshape(a.shape[-1], a.shape[-2])
            back = lambda a: jnp.swapaxes(a, 0, 1).reshape(wt.shape)
        else:
            to2d = lambda a: a.reshape(wt.size // wt.shape[-1], wt.shape[-1])
            back = lambda a: a.reshape(wt.shape)
        d, nm_, nv_ = _adamw(to2d(wt), g, to2d(m), to2d(v), "adamw_" + nm)
        res[nm] = (back(g), back(d), back(nm_), back(nv_))

    order = ["meta_tokens", "norm1_g", "w_in", "conv_w", "conv_b", "m_gate_b", "g_a2", "g_a2_b", "m_head_g", "g_head_g",
             "w_branch_m", "w_branch_g", "w_out", "norm2_g", "w_ff_gate", "w_ff_up", "w_ff_down", "final_g"]
    loss = lax.psum(loss_local[0, 0], ("x", "y", "c"))
    grad_x = dx.reshape(x.shape)
    return (loss, grad_x, *[res[n][0] for n in order], *[res[n][1] for n in order],
            *[res[n][2] for n in order], *[res[n][3] for n in order])


TRANSPOSED_GRADS = ("w_in", "w_ff_gate", "w_ff_up")
PACK_ORDER = ["w_in", "w_branch_m", "w_branch_g", "w_out", "w_ff_gate", "w_ff_up", "w_ff_down", "meta_tokens", "conv_w", "g_a2",
              "m_head_g", "g_head_g", "norm1_g", "conv_b", "m_gate_b", "g_a2_b", "norm2_g", "final_g"]


def _gather_weights(w_in, w_branch_m, w_branch_g, w_out, w_ff_gate, w_ff_up, w_ff_down, meta_tokens, conv_w, g_a2, m_head_g, g_head_g):
    bf = lambda a: a.astype(BF)
    rows_local = jnp.concatenate([bf(w_branch_m[0]), bf(w_branch_g[0]), bf(w_out[0]), bf(w_ff_down[0]),
                                  bf(w_ff_gate[0].T), bf(w_ff_up[0].T)], axis=0)
    win_local = jnp.pad(bf(w_in[0].T), ((0, W_IN_BLOCK - W_IN_SHARD), (0, 0)))
    small_local = _pack_small([meta_tokens, conv_w[0], g_a2[0], m_head_g[0], g_head_g[0]])
    small_all = _by_chip(small_local, _all_gather_chips(small_local, "gather_small"))
    small_sh = [_unpack_small(small_all[q], SMALL_SHARD_SHAPES) for q in range(4)]
    cat = lambda i: jnp.concatenate([s[i] for s in small_sh], axis=-1)
    return dict(win_local=win_local, rows_local=rows_local, meta=cat(0), convw=cat(1), ga2=cat(2),
                mhg=cat(3).reshape(1, D), ghg=cat(4).reshape(1, D))


def _row_weights(rows_local, gathered):
    rows_all = jnp.stack(_by_chip(rows_local, gathered))
    cut = lambda lo, hi: rows_all[:, lo:hi].reshape(4 * (hi - lo), D)
    return cut(0, 256), cut(256, 512), cut(512, 768), cut(768, 1472), _ffn_weight_rows(cut(1472, 2176), cut(2176, 2880))


def _local_step(x0, target, w, norm1_g, conv_b, m_gate_b, g_a2_b, norm2_g, final_g, reducer):
    meta_f, convw_f, ga2_f, mhg_f, ghg_f = w["meta"], w["convw"], w["ga2"], w["mhg"], w["ghg"]
    gbias =jnp.concatenate([m_gate_b.reshape(1, 2 * NH), jnp.zeros((1, N_SMALL - 2 * NH), F32)], axis=1)
    a2p = jnp.concatenate([jnp.zeros((8, NH * DQK), F32), ga2_f, jnp.zeros((N_SMALL - 24, NH * DQK), F32)], axis=0).astype(BF)
    convb = conv_b.reshape(1, D)
    g1 = norm1_g.reshape(1, D)
    g2 = norm2_g.reshape(1, D)
    gf = final_g.reshape(1, D)
    first = jnp.concatenate([jnp.zeros((FIRST_VALID, D), F32), meta_f], axis=0)

    h0, xn1, rstd1, win_gathered = _embed_norm(x0, first, g1, _gather_rider(w["win_local"]), "rms1")
    win_all = _by_chip(w["win_local"], win_gathered[0])
    w_in_f = jnp.concatenate([win_all[q][0:W_IN_SHARD] for q in range(4)], axis=0)
    w_big, w_small = _proj_rows_from_w_in(w_in_f)
    w_all = jnp.concatenate([w_big, w_small], axis=0)
    pbig, rows_gathered = _mm(xn1, w_big, nt=True, out_dtype=BF, tn=1024, name="proj_big", rider=_gather_rider(w["rows_local"]))
    wbm, wbg, wout, wdown, wgu_t = _row_weights(w["rows_local"], rows_gathered[0])
    small = _mm(xn1, w_small, nt=True, out_dtype=F32, tn=N_SMALL, name="proj_small")
    qk = _conv_fwd(pbig, convw_f, convb, "conv_fwd")
    y_m, m_cs, m_ns = _mlstm_fwd(qk, pbig, small, gbias, mhg_f, "mlstm_fwd")
    y_g, g_ss = _gla_fwd(pbig, small, a2p, g_a2_b, ghg_f, "gla_fwd")
    p_m, p_g, merged = _branch_merge(y_m, y_g, wbm, wbg, pbig, "branch_merge")
    h1, hn, rstd2 = _out_proj_norm(merged, wout, h0, g2, "out_proj")
    gu, ff = _ffn_in(hn, wgu_t, "ff_in")
    dh2, loss_local, d_final_g = _ffn_down_loss(ff, wdown, h1, target, gf, "ff_down_loss")

    d_wdown = _mm_tn(ff, dh2, tm=1408, tn=1024, name="dw_ff_down")
    dgu = _ffn_d_hidden(dh2, wdown, gu, "d_ff")
    d_wgu_t = _mm_tn(dgu, hn, tm=1408, tn=1024, name="dw_ff_in")
    dh1, d_g2 = _ffn_d_in(dgu, wgu_t, h1, rstd2, g2, dh2, "d_hn")
    d_wout = _mm_tn(merged, dh1, tm=1024, tn=1024, name="dw_out")
    dp_m, dp_g, dproj = _merge_d(dh1, wout, p_m, p_g, pbig, "d_merged")
    dy_m = _mm(dp_m, wbm, nt=True, out_dtype=BF, tn=1024, name="d_ym")
    dy_g = _mm(dp_g, wbg, nt=True, out_dtype=BF, tn=1024, name="d_yg")
    d_wbm = _mm_tn(y_m, dp_m, tm=1024, tn=1024, name="dw_branch_m")
    d_wbg = _mm_tn(y_g, dp_g, tm=1024, tn=1024, name="dw_branch_g")
    fq = D_FF // 4
    gu4 = jnp.transpose(d_wgu_t.reshape(2, 2, 2, fq, D), (0, 2, 1, 3, 4)).reshape(4, 2 * fq, D)
    sq4 = jnp.concatenate([d_wbm.reshape(4, 256, D), d_wbg.reshape(4, 256, D), d_wout.reshape(4, 256, D)], axis=1)
    sums_a = reducer.partial_sums([sq4, d_wdown.reshape(4, fq, D), gu4], BF, "a")
    dqk_m, dproj, dsm_m, d_gbias, d_mhg, recv_a = _mlstm_bwd(dy_m, qk, pbig, small, gbias, mhg_f, m_cs, m_ns, dproj,
                                                              "mlstm_bwd", ride=sums_a)
    dconv, d_convwb = _conv_bwd_pre(dqk_m, pbig, convw_f, convb, "conv_bwd_pre")
    dproj = _conv_bwd_in(dconv, convw_f, dproj, "conv_bwd_in")
    dproj, dsmall, d_a2p, d_a2b, d_ghg = _gla_bwd(dy_g, pbig, small, a2p, g_a2_b, ghg_f, g_ss, dsm_m, dproj, "gla_bwd")
    dproj = _place_small(dsmall, dproj, "dproj_small")
    d_win = _w_in_from_proj_rows(_mm_tn(dproj, xn1, tm=1664, tn=1024, name="dw_in"))
    pad = jnp.zeros((W_IN_BLOCK - W_IN_SHARD, D), F32)
    win4 = jnp.stack([jnp.concatenate([d_win[q * W_IN_SHARD:(q + 1) * W_IN_SHARD], pad], axis=0) for q in range(4)])
    sums_b = reducer.partial_sums([win4], BF, "b")
    dxn, recv_b = _mm(dproj, w_all, nt=False, out_dtype=F32, tn=1024, tk=1664, name="d_xn", rider=_scatter_rider(sums_b))
    dh_first, dx, d_g1 = _rms_bwd(dxn, h0, rstd1, g1, dh1, "rms1_bwd", split_first=True)

    small_sharded = [dh_first[FIRST_VALID:TM], d_convwb[0:4], d_a2p[8:24], d_mhg.reshape(NH, DV), d_ghg.reshape(NH, DV)]
    replicated = [d_g1, d_convwb[4:5], d_gbias[0:1, 0:2 * NH].reshape(1, 2, NH), d_a2b, d_g2, d_final_g.reshape(D)]
    small4 = jnp.stack([_pack_small([g[:, q * shp[1]:(q + 1) * shp[1]] for g, shp in zip(small_sharded, SMALL_SHARD_SHAPES)]
                                    + replicated) for q in range(4)])
    sums_c = reducer.partial_sums([small4], F32, "c")
    recv_c = reducer.scatter(sums_c, "c")
    sq, down, gu, win, smalls = reducer.finish(sums_a + sums_b + sums_c, recv_a + recv_b + recv_c, in_chip_order=[4])
    smalls = _unpack_small(smalls, SMALL_SHARD_SHAPES + REPL_SHAPES)
    grads = ([win[0:W_IN_SHARD], sq[0:256], sq[256:512], sq[512:768], gu[0:fq], gu[fq:2 * fq], down]
             + [g.reshape(g.size // g.shape[-1], g.shape[-1]) for g in smalls])
    return loss_local, dx, grads


class _Reducer:
    def partial_sums(self, items, dtype, tag):
        c = lax.axis_index("c")
        got = _swap_halves(items, "reduce_siblings_" + tag)
        sums = []
        for i, (a, g) in enumerate(zip(items, got)):
            rh, n = g.shape[1], g.shape[2]
            own = lax.dynamic_slice_in_dim(a, c * rh, rh, axis=1)
            sums.append(_add2(own.reshape(-1, n), g.reshape(-1, n), dtype, f"reduce_add2_{tag}{i}").reshape(g.shape))
        return sums

    def scatter(self, sums, tag):
        return list(_scatter_chips(sums, "reduce_chips_" + tag))

    def finish(self, sums, from_chips, in_chip_order):
        c = lax.axis_index("c")
        me = 2 * lax.axis_index("x") + lax.axis_index("y")
        halves = []
        for i, (s, f) in enumerate(zip(sums, from_chips)):
            mine = lax.dynamic_index_in_dim(s, me, 0, keepdims=False)
            if i in in_chip_order:
                by_chip = _by_chip(mine, f)
                mine, f = by_chip[0], jnp.stack(by_chip[1:])
            halves.append(_add4(mine, f, f"reduce_add4_{i}"))
        got = _join_halves(halves, "reduce_join")
        return [jnp.where(c == 0, jnp.concatenate([h, g], axis=0), jnp.concatenate([g, h], axis=0)) for h, g in zip(halves, got)]
```

```python
import functools

import jax
import jax.numpy as jnp
from jax import lax
from jax.experimental import pallas as pl
from jax.experimental.pallas import tpu as pltpu

F32 = jnp.float32
BF = jnp.bfloat16
HI = lax.Precision.HIGHEST
MESH = pl.DeviceIdType.MESH

D = 1024
N_META = 16
CHUNK = 128
EPS = 1e-6
NH = 4
DV = 256
DQK = 128
G_RANK = 16
G_TAU = 16.0
D_FF = 2816
TM = 512
FIRST_VALID = TM - N_META
CPB = TM // CHUNK
G_CHUNK = 256
G_CPB = TM // G_CHUNK
NEG = -1e30
N_BIG = 8192
CB_GQK, CB_GV, CB_GR, CB_MQK, CB_GM, CB_GG, CB_MV, CB_MO = range(8)
N_SMALL = 128
N_ALL = N_BIG + N_SMALL
VMEM_LIMIT = 56 * 1024 * 1024

ADAM_LR, ADAM_B1, ADAM_B2, ADAM_EPS, ADAM_WD, ADAM_STEP = 0.001, 0.9, 0.999, 1e-08, 0.01, 10

NT_DIMS = (((1,), (1,)), ((), ()))
TN_DIMS = (((0,), (0,)), ((), ()))


def _nt(a, b, **kw):
    return lax.dot_general(a, b, NT_DIMS, preferred_element_type=F32, **kw)


def _tn(a, b, **kw):
    return lax.dot_general(a, b, TN_DIMS, preferred_element_type=F32, **kw)


def _nn(a, b, **kw):
    return jnp.dot(a, b, preferred_element_type=F32, **kw)


def _params(**kw):
    return pltpu.CompilerParams(vmem_limit_bytes=VMEM_LIMIT, **kw)


def _sigmoid(x):
    return 0.5 * jnp.tanh(0.5 * x) + 0.5


def _logsig(x):
    return jnp.minimum(x, 0.0) - jnp.log(1.0 + jnp.exp(-jnp.abs(x)))


def _mm_rows(rows):
    return 3 * TM if rows % (3 * TM) == 0 else TM


def _mm(a, b, *, nt, out_dtype, tn, tk=None, tm=None, name, rider=None):
    m, k = a.shape
    n = b.shape[0] if nt else b.shape[1]
    tk = k if tk is None else tk
    tm = _mm_rows(m) if tm is None else tm
    nk = k // tk
    nj, ni = n // tn, m // tm
    nr_in = len(rider["inputs"]) if rider else 0
    nr_out = len(rider["out_shapes"]) if rider else 0
    assert m % tm == 0 and n % tn == 0 and k % tk == 0
    dims = NT_DIMS if nt else (((1,), (0,)), ((), ()))

    def body(*refs):
        a_ref, b_ref = refs[:2]
        o_ref = refs[2 + nr_in]
        j, i, kk = pl.program_id(0), pl.program_id(1), pl.program_id(2)
        step = (j * ni + i) * nk + kk
        if rider:
            start, middle, finish = rider["make"](refs[2:2 + nr_in], refs[3 + nr_in:3 + nr_in + nr_out],
                                                  refs[3 + nr_in + nr_out:5 + nr_in + nr_out])
            pl.when(step == 0)(start)
            pl.when(step == (nj * ni * nk) // 2)(middle)

        part = lax.dot_general(a_ref[...].astype(BF), b_ref[...].astype(BF), dims, preferred_element_type=F32)
        if nk == 1:
            o_ref[...] = part.astype(o_ref.dtype)
        else:
            acc_ref = refs[-1]

            @pl.when(kk == 0)
            def _():
                acc_ref[...] = part

            @pl.when(jnp.logical_and(kk > 0, kk < nk - 1))
            def _():
                acc_ref[...] += part

            @pl.when(kk == nk - 1)
            def _():
                o_ref[...] = (acc_ref[...] + part).astype(o_ref.dtype)

        if rider:
            pl.when(step == nj * ni * nk - 1)(finish)

    outs = pl.pallas_call(
        body, grid=(nj, ni, nk),
        in_specs=[pl.BlockSpec((tm, tk), lambda j, i, kk: (i, kk)),
                  pl.BlockSpec((tn, tk), lambda j, i, kk: (j, kk)) if nt else pl.BlockSpec((tk, tn), lambda j, i, kk: (kk, j))]
                 + [ANY] * nr_in,
        out_specs=[pl.BlockSpec((tm, tn), lambda j, i, kk: (i, j))] + [ANY] * nr_out,
        out_shape=(jax.ShapeDtypeStruct((m, n), out_dtype),) + (tuple(rider["out_shapes"]) if rider else ()),
        scratch_shapes=(rider["sems"] if rider else []) + ([pltpu.VMEM((tm, tn), F32)] if nk > 1 else []),
        compiler_params=_params(), name=name)(a, b, *(rider["inputs"] if rider else []))
    return (outs[0], list(outs[1:])) if rider else outs[0]


def _mm_tn(a, b, *, tm, tn, tk=None, name):
    t, m = a.shape
    n = b.shape[1]
    tk = _mm_rows(t) if tk is None else tk
    assert t % tk == 0 and m % tm == 0 and n % tn == 0

    def body(a_ref, b_ref, o_ref):
        part = _tn(a_ref[...].astype(BF), b_ref[...].astype(BF))

        @pl.when(pl.program_id(2) == 0)
        def _():
            o_ref[...] = part

        @pl.when(pl.program_id(2) > 0)
        def _():
            o_ref[...] += part

    return pl.pallas_call(
        body, grid=(m // tm, n // tn, t // tk),
        in_specs=[pl.BlockSpec((tk, tm), lambda i, j, kk: (kk, i)), pl.BlockSpec((tk, tn), lambda i, j, kk: (kk, j))],
        out_specs=pl.BlockSpec((tm, tn), lambda i, j, kk: (i, j)),
        out_shape=jax.ShapeDtypeStruct((m, n), F32), compiler_params=_params(), name=name)(a, b)


ANY = pl.BlockSpec(memory_space=pl.ANY)


def _row_spec(width, col=0):
    return pl.BlockSpec((TM, width), lambda i: (i, col))


def _full_spec(shape):
    return pl.BlockSpec(shape, lambda i: (0,) * len(shape))


def _embed_norm(x0, first, g, rider, name):
    tp = x0.shape[0] + TM
    nb = tp // TM
    nri, nro = len(rider["inputs"]), len(rider["out_shapes"])

    def body(*refs):
        x_ref, f_ref, g_ref = refs[:3]
        h_ref, xn_ref, r_ref = refs[3 + nri:6 + nri]
        i = pl.program_id(0)
        start, middle, finish = rider["make"](refs[3:3 + nri], refs[6 + nri:6 + nri + nro], refs[6 + nri + nro:])
        pl.when(i == 0)(start)
        pl.when(i == nb // 2)(middle)
        x = jnp.where(i == 0, f_ref[...], x_ref[...])
        r = lax.rsqrt(jnp.mean(x * x, axis=1, keepdims=True) + EPS)
        h_ref[...] = x
        xn_ref[...] = (x * r * g_ref[...]).astype(BF)
        r_ref[...] = r
        pl.when(i == nb - 1)(finish)

    outs = pl.pallas_call(
        body, grid=(nb,),
        in_specs=[pl.BlockSpec((TM, D), lambda i: (jnp.maximum(i - 1, 0), 0)), _full_spec((TM, D)), _full_spec((1, D))] + [ANY] * nri,
        out_specs=[_row_spec(D), _row_spec(D), _row_spec(1)] + [ANY] * nro,
        out_shape=(jax.ShapeDtypeStruct((tp, D), F32), jax.ShapeDtypeStruct((tp, D), BF), jax.ShapeDtypeStruct((tp, 1), F32))
                  + tuple(rider["out_shapes"]),
        scratch_shapes=rider["sems"], compiler_params=_params(), name=name)(x0, first, g, *rider["inputs"])
    return outs[0], outs[1], outs[2], list(outs[3:])


def _rms_bwd(dxn, h, rstd, g, dres, name, split_first=False):
    tp = h.shape[0]

    def body(dxn_ref, h_ref, r_ref, g_ref, dres_ref, *outs):
        r = r_ref[...]
        xh = h_ref[...] * r
        dxn_v = dxn_ref[...].astype(F32)
        dxh = dxn_v * g_ref[...]
        dh = r * (dxh - xh * jnp.mean(dxh * xh, axis=1, keepdims=True)) + dres_ref[...]
        if split_first:
            first_ref, dh_ref, dg_ref = outs

            @pl.when(pl.program_id(0) == 0)
            def _():
                first_ref[...] = dh
        else:
            dh_ref, dg_ref = outs
        dh_ref[...] = dh
        part = jnp.sum(dxn_v * xh, axis=0, keepdims=True)

        @pl.when(pl.program_id(0) == 0)
        def _():
            dg_ref[...] = part

        @pl.when(pl.program_id(0) > 0)
        def _():
            dg_ref[...] += part

    if split_first:
        out_specs = [_full_spec((TM, D)), pl.BlockSpec((TM, D), lambda i: (jnp.maximum(i - 1, 0), 0)), _full_spec((1, D))]
        out_shape = (jax.ShapeDtypeStruct((TM, D), F32), jax.ShapeDtypeStruct((tp - TM, D), F32), jax.ShapeDtypeStruct((1, D), F32))
    else:
        out_specs = [_row_spec(D), _full_spec((1, D))]
        out_shape = (jax.ShapeDtypeStruct((tp, D), F32), jax.ShapeDtypeStruct((1, D), F32))
    return pl.pallas_call(
        body, grid=(tp // TM,),
        in_specs=[_row_spec(D), _row_spec(D), _row_spec(1), _full_spec((1, D)), _row_spec(D)],
        out_specs=out_specs, out_shape=out_shape, compiler_params=_params(), name=name)(dxn, h, rstd, g, dres)


def _shift_down(x, halo, k):
    rk = pltpu.roll(x, k, 0)
    io = lax.broadcasted_iota(jnp.int32, (8, x.shape[1]), 0)
    top = jnp.where(io < k, pltpu.roll(halo, k, 0), rk[0:8])
    return top if x.shape[0] == 8 else jnp.concatenate([top, rk[8:]], axis=0)


def _shift_up(x, nxt, k):
    n = x.shape[0]
    rk = pltpu.roll(x, n - k, 0)
    io = lax.broadcasted_iota(jnp.int32, (8, x.shape[1]), 0)
    bot = jnp.where(io >= 8 - k, pltpu.roll(nxt, 8 - k, 0), rk[n - 8:n])
    return jnp.concatenate([rk[:n - 8], bot], axis=0)


def _conv_pre(x, halo, w_ref, b_ref):
    c = x * w_ref[3:4, :] + b_ref[...]
    shifted = []
    for k in (1, 2, 3):
        s = _shift_down(x, halo, k)
        shifted.append(s)
        c = c + s * w_ref[3 - k:4 - k, :]
    return c, shifted


def _qk_scale():
    col = lax.broadcasted_iota(jnp.int32, (1, D), 1)
    return jnp.where(col < NH * DQK, DQK ** -0.5, 1.0).astype(F32)


def _halo_prev_spec():
    return pl.BlockSpec((8, D), lambda i: (jnp.maximum(i * (TM // 8) - 1, 0), CB_MQK))


def _conv_fwd(pbig, w, b, name):
    tp = pbig.shape[0]

    def body(x_ref, halo_ref, w_ref, b_ref, o_ref):
        x = x_ref[...].astype(F32)
        halo = jnp.where(pl.program_id(0) > 0, halo_ref[...].astype(F32), 0.0)
        c, _ = _conv_pre(x, halo, w_ref, b_ref)
        o_ref[...] = (c * _sigmoid(c) * _qk_scale()).astype(BF)

    return pl.pallas_call(
        body, grid=(tp // TM,),
        in_specs=[_row_spec(D, CB_MQK), _halo_prev_spec(), _full_spec((4, D)), _full_spec((1, D))],
        out_specs=_row_spec(D), out_shape=jax.ShapeDtypeStruct((tp, D), BF),
        compiler_params=_params(), name=name)(pbig, pbig, w, b)


def _conv_bwd(dqk, pbig, w, b, dproj, name):
    tp = pbig.shape[0]
    nb = tp // TM

    def d_conv_out(d, x, halo, w_ref, b_ref):
        c, shifted = _conv_pre(x, halo, w_ref, b_ref)
        sg = _sigmoid(c)
        return d * _qk_scale() * (sg * (1.0 + c * (1.0 - sg))), shifted

    def body(d_ref, dn_ref, x_ref, halo_ref, xn_ref, w_ref, b_ref, _, o_ref, dwb_ref):
        i = pl.program_id(0)
        x = x_ref[...].astype(F32)
        halo = jnp.where(i > 0, halo_ref[...].astype(F32), 0.0)
        dc, shifted = d_conv_out(d_ref[...], x, halo, w_ref, b_ref)
        dc_next, _ = d_conv_out(dn_ref[...], xn_ref[...].astype(F32), x[TM - 8:TM], w_ref, b_ref)
        nxt = jnp.where(i < nb - 1, dc_next, 0.0)
        acc = dc * w_ref[3:4, :]
        for k in (1, 2, 3):
            acc = acc + _shift_up(dc, nxt, k) * w_ref[3 - k:4 - k, :]
        o_ref[...] = acc.astype(BF)
        taps = [shifted[2], shifted[1], shifted[0], x]
        rows = [jnp.sum(dc * t, axis=0, keepdims=True) for t in taps] + [jnp.sum(dc, axis=0, keepdims=True)]
        io = lax.broadcasted_iota(jnp.int32, (8, D), 0)
        part = jnp.zeros((8, D), F32)
        for r, v in enumerate(rows):
            part = jnp.where(io == r, v, part)

        @pl.when(pl.program_id(0) == 0)
        def _():
            dwb_ref[...] = part

        @pl.when(pl.program_id(0) > 0)
        def _():
            dwb_ref[...] += part

    next8 = lambda col: pl.BlockSpec((8, D), lambda i: (jnp.minimum((i + 1) * (TM // 8), tp // 8 - 1), col))
    return pl.pallas_call(
        body, grid=(nb,),
        in_specs=[_row_spec(D), next8(0), _row_spec(D, CB_MQK), _halo_prev_spec(), next8(CB_MQK),
                  _full_spec((4, D)), _full_spec((1, D)), ANY],
        out_specs=[_row_spec(D, CB_MQK), _full_spec((8, D))],
        out_shape=(jax.ShapeDtypeStruct(dproj.shape, BF), jax.ShapeDtypeStruct((8, D), F32)),
        input_output_aliases={7: 0}, compiler_params=_params(), name=name)(dqk, dqk, pbig, pbig, pbig, w, b, dproj)


def _mm_fused(inputs, products, *, nt, m, n, tm, tn, outs, epilogue, name, nk=1, sub=None):
    dims = NT_DIMS if nt else (((1,), (0,)), ((), ()))
    nin = len(inputs)
    assert nk == 1 or (len(products) == 1 and sub is None)

    def body(*refs):
        in_refs, out_refs = refs[:nin], refs[nin:nin + len(outs)]
        i = pl.program_id(1)
        if sub is not None:
            lhs = {ia: in_refs[ia][...].astype(BF) for ia, _ in products}

            def dots(cols):
                return [lax.dot_general(lhs[ia], (in_refs[ib][cols, :] if nt else in_refs[ib][:, cols]).astype(BF),
                                        dims, preferred_element_type=F32) for ia, ib in products]

            slices = [slice(s, min(s + sub, tn)) for s in range(0, tn, sub)]
            prods = dots(slices[0])
            for idx, cols in enumerate(slices):
                nxt = dots(slices[idx + 1]) if idx + 1 < len(slices) else None
                epilogue(prods, in_refs, out_refs, i, cols)
                prods = nxt
            return
        prods = [lax.dot_general(in_refs[ia][...].astype(BF), in_refs[ib][...].astype(BF), dims, preferred_element_type=F32)
                 for ia, ib in products]
        if nk == 1:
            epilogue(prods, in_refs, out_refs, i, slice(None))
            return
        acc_ref = refs[-1]
        kk = pl.program_id(2)

        @pl.when(kk == 0)
        def _():
            acc_ref[...] = prods[0]

        @pl.when(jnp.logical_and(kk > 0, kk < nk - 1))
        def _():
            acc_ref[...] += prods[0]

        @pl.when(kk == nk - 1)
        def _():
            epilogue([acc_ref[...] + prods[0]], in_refs, out_refs, i, slice(None))

    return pl.pallas_call(
        body, grid=(n // tn, m // tm, nk), in_specs=[s for _, s in inputs], out_specs=[s for _, s in outs],
        out_shape=tuple(sh for sh, _ in outs), scratch_shapes=[pltpu.VMEM((tm, tn), F32)] if nk > 1 else [],
        compiler_params=_params(), name=name)(*[a for a, _ in inputs])


SUB_COLS = 256


def _cols_at(cols, offset):
    return slice(cols.start + offset, cols.stop + offset)


def _blk(rows, width, col=None, row=None):
    return pl.BlockSpec((rows, width), lambda j, i, kk: ((i if row is None else row(i)), (0 if col is None else col(j, kk))))


FF_TN = D_FF // 2


def _ffn_weight_rows(wg_t, wu_t):
    return jnp.concatenate([wg_t[0:FF_TN], wu_t[0:FF_TN], wg_t[FF_TN:], wu_t[FF_TN:]], axis=0)


def _ffn_in(hn, wgu_t, name):
    tp = hn.shape[0]
    tm = _mm_rows(tp)

    def epilogue(prods, in_refs, out_refs, i, cols):
        g, u = prods
        out_refs[0][:, cols] = g.astype(BF)
        out_refs[0][:, _cols_at(cols, FF_TN)] = u.astype(BF)
        out_refs[1][:, cols] = (g * _sigmoid(g) * u).astype(BF)

    wspec = lambda off: pl.BlockSpec((FF_TN, D), lambda j, i, kk: (2 * j + off, 0))
    return _mm_fused(
        [(hn, _blk(tm, D)), (wgu_t, wspec(0)), (wgu_t, wspec(1))], [(0, 1), (0, 2)], nt=True, m=tp, n=D_FF, tm=tm, tn=FF_TN,
        outs=[(jax.ShapeDtypeStruct((tp, 2 * D_FF), BF), _blk(tm, 2 * FF_TN, lambda j, kk: j)),
              (jax.ShapeDtypeStruct((tp, D_FF), BF), _blk(tm, FF_TN, lambda j, kk: j))],
        epilogue=epilogue, name=name, sub=SUB_COLS)


def _ffn_down_loss(ff, wdown, h1, target, gf, name):
    tp = ff.shape[0]

    def epilogue(prods, in_refs, out_refs, i, cols):
        live = (i > 0).astype(F32)
        g = in_refs[4][...]
        x = prods[0] + in_refs[2][...]
        r = lax.rsqrt(jnp.mean(x * x, axis=1, keepdims=True) + EPS)
        xh = x * r
        e = xh * g - in_refs[3][...]
        loss_part = 0.5 * live * jnp.sum(jnp.mean(e * e, axis=1, keepdims=True), axis=0, keepdims=True)
        dout = e * (live / D)
        dg_part = jnp.sum(dout * xh, axis=0, keepdims=True)
        dxh = dout * g
        out_refs[0][...] = r * (dxh - xh * jnp.mean(dxh * xh, axis=1, keepdims=True))

        @pl.when(i == 0)
        def _():
            out_refs[1][...] = loss_part
            out_refs[2][...] = dg_part

        @pl.when(i > 0)
        def _():
            out_refs[1][...] += loss_part
            out_refs[2][...] += dg_part

    const = lambda shape: pl.BlockSpec(shape, lambda j, i, kk: (0,) * len(shape))
    return _mm_fused(
        [(ff, _blk(TM, D_FF)), (wdown, const((D_FF, D))), (h1, _blk(TM, D)),
         (target, _blk(TM, D, row=lambda i: jnp.maximum(i - 1, 0))), (gf, const((1, D)))],
        [(0, 1)], nt=False, m=tp, n=D, tm=TM, tn=D,
        outs=[(jax.ShapeDtypeStruct((tp, D), F32), _blk(TM, D)), (jax.ShapeDtypeStruct((1, 1), F32), const((1, 1))),
              (jax.ShapeDtypeStruct((1, D), F32), const((1, D)))],
        epilogue=epilogue, name=name)


def _ffn_d_hidden(dh2, wdown, gu, name):
    tp = dh2.shape[0]

    def epilogue(prods, in_refs, out_refs, i, cols):
        d = prods[0]
        g = in_refs[2][:, cols].astype(F32)
        u = in_refs[2][:, _cols_at(cols, FF_TN)].astype(F32)
        sg = _sigmoid(g)
        out_refs[0][:, cols] = (d * u * sg * (1.0 + g * (1.0 - sg))).astype(BF)
        out_refs[0][:, _cols_at(cols, FF_TN)] = (d * g * sg).astype(BF)

    return _mm_fused(
        [(dh2, _blk(TM, D)), (wdown, pl.BlockSpec((FF_TN, D), lambda j, i, kk: (j, 0))), (gu, _blk(TM, 2 * FF_TN, lambda j, kk: j))],
        [(0, 1)], nt=True, m=tp, n=D_FF, tm=TM, tn=FF_TN,
        outs=[(jax.ShapeDtypeStruct((tp, 2 * D_FF), BF), _blk(TM, 2 * FF_TN, lambda j, kk: j))],
        epilogue=epilogue, name=name, sub=SUB_COLS)[0]


def _ffn_d_in(dgu, wgu_t, h1, rstd, g2, dh2, name):
    tp = dgu.shape[0]
    nk = 2

    def epilogue(prods, in_refs, out_refs, i, cols):
        r = in_refs[3][...]
        xh = in_refs[2][...] * r
        dxn = prods[0]
        dxh = dxn * in_refs[4][...]
        out_refs[0][...] = r * (dxh - xh * jnp.mean(dxh * xh, axis=1, keepdims=True)) + in_refs[5][...]
        part = jnp.sum(dxn * xh, axis=0, keepdims=True)

        @pl.when(i == 0)
        def _():
            out_refs[1][...] = part

        @pl.when(i > 0)
        def _():
            out_refs[1][...] += part

    const = lambda shape: pl.BlockSpec(shape, lambda j, i, kk: (0,) * len(shape))
    return _mm_fused(
        [(dgu, pl.BlockSpec((TM, D_FF), lambda j, i, kk: (i, kk))), (wgu_t, pl.BlockSpec((D_FF, D), lambda j, i, kk: (kk, 0))),
         (h1, _blk(TM, D)), (rstd, _blk(TM, 1)), (g2, const((1, D))), (dh2, _blk(TM, D))],
        [(0, 1)], nt=False, m=tp, n=D, tm=TM, tn=D, nk=nk,
        outs=[(jax.ShapeDtypeStruct((tp, D), F32), _blk(TM, D)), (jax.ShapeDtypeStruct((1, D), F32), const((1, D)))],
        epilogue=epilogue, name=name)


def _branch_merge(y_m, y_g, wbm, wbg, pbig, name):
    tp = y_m.shape[0]

    def epilogue(prods, in_refs, out_refs, i, cols):
        pm, pg = prods[0].astype(BF), prods[1].astype(BF)
        out_refs[0][:, cols] = pm
        out_refs[1][:, cols] = pg
        out_refs[2][:, cols] = (_sigmoid(in_refs[4][:, cols].astype(F32)) * pm.astype(F32)
                                + _sigmoid(in_refs[5][:, cols].astype(F32)) * pg.astype(F32)).astype(BF)

    const = lambda shape: pl.BlockSpec(shape, lambda j, i, kk: (0,) * len(shape))
    shp = jax.ShapeDtypeStruct((tp, D), BF)
    return _mm_fused(
        [(y_m, _blk(TM, D)), (wbm, const((D, D))), (y_g, _blk(TM, D)), (wbg, const((D, D))),
         (pbig, _blk(TM, D, lambda j, kk: CB_GM)), (pbig, _blk(TM, D, lambda j, kk: CB_GG))],
        [(0, 1), (2, 3)], nt=False, m=tp, n=D, tm=TM, tn=D,
        outs=[(shp, _blk(TM, D)), (shp, _blk(TM, D)), (shp, _blk(TM, D))], epilogue=epilogue, name=name, sub=SUB_COLS)


def _merge_d(dh1, wout, pm, pg, pbig, name):
    tp = dh1.shape[0]

    def epilogue(prods, in_refs, out_refs, i, cols):
        d = prods[0]
        sm = _sigmoid(in_refs[4][:, cols].astype(F32))
        sg = _sigmoid(in_refs[5][:, cols].astype(F32))
        out_refs[0][:, cols] = (d * sm).astype(BF)
        out_refs[1][:, cols] = (d * sg).astype(BF)
        out_refs[2][:, cols] = (d * in_refs[2][:, cols].astype(F32) * sm * (1.0 - sm)).astype(BF)
        out_refs[2][:, _cols_at(cols, D)] = (d * in_refs[3][:, cols].astype(F32) * sg * (1.0 - sg)).astype(BF)

    const = lambda shape: pl.BlockSpec(shape, lambda j, i, kk: (0,) * len(shape))
    shp = jax.ShapeDtypeStruct((tp, D), BF)
    return _mm_fused(
        [(dh1, _blk(TM, D)), (wout, const((D, D))), (pm, _blk(TM, D)), (pg, _blk(TM, D)),
         (pbig, _blk(TM, D, lambda j, kk: CB_GM)), (pbig, _blk(TM, D, lambda j, kk: CB_GG))],
        [(0, 1)], nt=True, m=tp, n=D, tm=TM, tn=D,
        outs=[(shp, _blk(TM, D)), (shp, _blk(TM, D)),
              (jax.ShapeDtypeStruct((tp, N_ALL), BF), _blk(TM, 2 * D, lambda j, kk: CB_GM // 2))],
        epilogue=epilogue, name=name, sub=SUB_COLS)


def _out_proj_norm(merged, wout, h0, g2, name):
    tp = merged.shape[0]
    tm = _mm_rows(tp)

    def epilogue(prods, in_refs, out_refs, i, cols):
        x = prods[0] + in_refs[2][...]
        r = lax.rsqrt(jnp.mean(x * x, axis=1, keepdims=True) + EPS)
        out_refs[0][...] = x
        out_refs[1][...] = (x * r * in_refs[3][...]).astype(BF)
        out_refs[2][...] = r

    const = lambda shape: pl.BlockSpec(shape, lambda j, i, kk: (0,) * len(shape))
    return _mm_fused(
        [(merged, _blk(tm, D)), (wout, const((D, D))), (h0, _blk(tm, D)), (g2, const((1, D)))],
        [(0, 1)], nt=False, m=tp, n=D, tm=tm, tn=D,
        outs=[(jax.ShapeDtypeStruct((tp, D), F32), _blk(tm, D)), (jax.ShapeDtypeStruct((tp, D), BF), _blk(tm, D)),
              (jax.ShapeDtypeStruct((tp, 1), F32), _blk(tm, 1))],
        epilogue=epilogue, name=name)


def _adamw(w, g, m, v, name):
    rows, cols = w.shape
    by_cols = rows % 128 != 0 and cols % 128 == 0 and rows * cols > 128 * 1024
    tr = rows if (by_cols or rows % 128 != 0) else 128
    tc = 128 if by_cols else cols

    def body(w_ref, g_ref, m_ref, v_ref, d_ref, nm_ref, nv_ref):
        gv = g_ref[...]
        nm = ADAM_B1 * m_ref[...] + (1.0 - ADAM_B1) * gv
        nv = ADAM_B2 * v_ref[...] + (1.0 - ADAM_B2) * (gv * gv)
        m_hat = nm / (1.0 - ADAM_B1 ** ADAM_STEP)
        v_hat = nv / (1.0 - ADAM_B2 ** ADAM_STEP)
        d_ref[...] = -ADAM_LR * (m_hat / (jnp.sqrt(v_hat) + ADAM_EPS) + ADAM_WD * w_ref[...])
        nm_ref[...] = nm
        nv_ref[...] = nv

    spec = pl.BlockSpec((tr, tc), (lambda i: (0, i)) if by_cols else (lambda i: (i, 0)))
    shp = jax.ShapeDtypeStruct((rows, cols), F32)
    return pl.pallas_call(body, grid=(cols // tc if by_cols else rows // tr,), in_specs=[spec] * 4, out_specs=[spec] * 3,
                          out_shape=(shp,) * 3, compiler_params=_params(), name=name)(w, g, m, v)


def _place_small(dsmall, dproj, name):
    tp = dsmall.shape[0]

    def body(s_ref, _, o_ref):
        o_ref[...] = s_ref[...]

    return pl.pallas_call(
        body, grid=(tp // TM,), in_specs=[_row_spec(N_SMALL), ANY], out_specs=_row_spec(N_SMALL, N_BIG // N_SMALL),
        out_shape=jax.ShapeDtypeStruct(dproj.shape, dproj.dtype), input_output_aliases={1: 0},
        compiler_params=_params(), name=name)(dsmall, dproj)


def _row_tile(rows, cap=512):
    best = rows
    for cand in range(8, min(rows, cap) + 1, 8):
        if rows % cand == 0:
            best = cand
    return best


def _add2(a, b, out_dtype, name):
    rows, cols = a.shape
    tr = _row_tile(rows)

    def body(a_ref, b_ref, o_ref):
        o_ref[...] = (a_ref[...] + b_ref[...]).astype(o_ref.dtype)

    spec = pl.BlockSpec((tr, cols), lambda i: (i, 0))
    return pl.pallas_call(body, grid=(rows // tr,), in_specs=[spec] * 2, out_specs=spec,
                          out_shape=jax.ShapeDtypeStruct((rows, cols), out_dtype), compiler_params=_params(), name=name)(a, b)


def _add4(first, rest, name):
    rows, cols = first.shape
    tr = _row_tile(rows, 256)

    def body(f_ref, r_ref, o_ref):
        up = lambda v: v.astype(F32)
        o_ref[...] = ((up(f_ref[...]) + up(r_ref[0])) + up(r_ref[1])) + up(r_ref[2])

    return pl.pallas_call(body, grid=(rows // tr,),
                          in_specs=[pl.BlockSpec((tr, cols), lambda i: (i, 0)), pl.BlockSpec((3, tr, cols), lambda i: (0, i, 0))],
                          out_specs=pl.BlockSpec((tr, cols), lambda i: (i, 0)),
                          out_shape=jax.ShapeDtypeStruct((rows, cols), F32), compiler_params=_params(), name=name)(first, rest)


def _chunk_consts(length=CHUNK):
    r2 = lax.broadcasted_iota(jnp.int32, (length, length), 0)
    c2 = lax.broadcasted_iota(jnp.int32, (length, length), 1)
    tri = r2 >= c2
    return dict(tri=tri, tril_f=tri.astype(F32), triu_f=(r2 <= c2).astype(F32),
                lane=lax.broadcasted_iota(jnp.int32, (length, N_SMALL), 1),
                rowio=lax.broadcasted_iota(jnp.int32, (length, 1), 0),
                ones=jnp.ones((length, N_SMALL), F32))


def _valid_rows(block, c):
    row = block * TM + c * CHUNK + lax.broadcasted_iota(jnp.int32, (CHUNK, 1), 0)
    return row >= FIRST_VALID


def _col(x, lane, idx):
    return jnp.sum(jnp.where(lane == idx, x, 0.0), axis=1, keepdims=True)


def _last_row(x, rowio):
    return jnp.sum(jnp.where(rowio == rowio.shape[0] - 1, x, 0.0), axis=0, keepdims=True)


def _sum_all(x):
    return jnp.sum(jnp.sum(x, axis=1, keepdims=True), axis=0, keepdims=True)


def _headnorm_fwd(hm, gain, gate_act):
    rs = lax.rsqrt(jnp.mean(hm * hm, axis=1, keepdims=True) + EPS)
    return hm * rs * gain * gate_act


def _headnorm_bwd(dy, hm, gain, gate_act):
    rs = lax.rsqrt(jnp.mean(hm * hm, axis=1, keepdims=True) + EPS)
    xh = hm * rs
    dact = dy * xh * gain
    dgain = jnp.sum(dy * gate_act * xh, axis=0, keepdims=True)
    dxh = dy * gate_act * gain
    dhm = rs * (dxh - xh * jnp.mean(dxh * xh, axis=1, keepdims=True))
    return dhm, dact, dgain


def _mlstm_gates(sm, gbias, valid, k):
    pre = sm + gbias
    lf = jnp.where(valid, _logsig(pre), 0.0)
    b_all = _nn(k["tril_f"], lf, precision=HI)
    li_all = jnp.where(valid, pre, NEG)
    return pre, li_all, b_all


def _mlstm_open(h, qh, kh, c_st, li_all, b_all, k):
    lane = k["lane"]
    sel = jnp.where(lane == h, 1.0, 0.0) - jnp.where(lane == NH + h, 1.0, 0.0)
    x = jnp.where(lane < NH, li_all, jnp.where(lane < 2 * NH, b_all, 0.0))
    cb = c_st.astype(BF)
    return dict(ubc=_nt(sel, x, precision=HI), sim=_nt(qh, kh), cb=cb, cq=_nt(qh, cb))


def _mlstm_weights(h, f, qh, vh, li_all, b_all, n_row, m11, k):
    lane, tri, rowio = k["lane"], k["tri"], k["rowio"]
    b_col = _col(b_all, lane, NH + h)
    li_col = _col(li_all, lane, h)
    dmat = jnp.where(tri, b_col + f["ubc"], NEG)
    m_row = jnp.maximum(b_col + m11, jnp.max(dmat, axis=1, keepdims=True))
    e = jnp.exp(dmat - m_row)
    w_mat = e * f["sim"]
    a = jnp.exp(b_col + m11 - m_row)
    qf = qh.astype(F32)
    nq = jnp.sum(qf * n_row, axis=1, keepdims=True)
    g = _last_row(b_col, rowio)
    wlog = g - b_col + li_col
    m_new = jnp.maximum(g + m11, jnp.max(wlog, axis=0, keepdims=True))
    a_s = jnp.exp(g + m11 - m_new)
    w = jnp.exp(wlog - m_new)
    return dict(f, e=e, w_mat=w_mat, a=a, qf=qf, nq=nq, m_row=m_row, m_new=m_new, a_s=a_s, w=w,
                wv=_nn(w_mat.astype(BF), vh))


def _mlstm_out(f):
    num = f["a"] * f["cq"] + f["wv"]
    den = f["a"] * f["nq"] + jnp.sum(f["w_mat"], axis=1, keepdims=True)
    floor = jnp.exp(-f["m_row"])
    r = jnp.maximum(jnp.abs(den), floor)
    return dict(f, den=den, floor=floor, r=r, hm=num / r)


def _mlstm_fwd(qk, pbig, small, gbias, headg, name):
    tp = qk.shape[0]
    nb = tp // TM

    def body(qk_ref, v_ref, mo_ref, sm_ref, gb_ref, hg_ref, y_ref, cs_ref, ns_ref, c_scr, n_scr):
        blk = pl.program_id(0)

        @pl.when(blk == 0)
        def _():
            c_scr[...] = jnp.zeros_like(c_scr)
            n_scr[...] = jnp.zeros_like(n_scr)

        k = _chunk_consts()
        io8 = lax.broadcasted_iota(jnp.int32, (8, DQK), 0)

        def chunk(c, carry):
            r0 = pl.multiple_of(c * CHUNK, CHUNK)
            rows = pl.ds(r0, CHUNK)
            valid = _valid_rows(blk, c)
            _, li_all, b_all = _mlstm_gates(sm_ref[rows, :], gb_ref[...], valid, k)
            heads = range(NH)
            qs = [qk_ref[rows, h * DQK:(h + 1) * DQK] for h in heads]
            ks = [qk_ref[rows, NH * DQK + h * DQK:NH * DQK + (h + 1) * DQK] for h in heads]
            vs = [v_ref[rows, h * DV:(h + 1) * DV] for h in heads]
            cst = [c_scr[h] for h in heads]
            nrow = [n_scr[h, 0:1, :] for h in heads]
            m11 = [jnp.max(n_scr[h, 1:2, :], axis=1, keepdims=True) for h in heads]
            f = [_mlstm_open(h, qs[h], ks[h], cst[h], li_all, b_all, k) for h in heads]
            f = [_mlstm_weights(h, f[h], qs[h], vs[h], li_all, b_all, nrow[h], m11[h], k) for h in heads]
            wk = [f[h]["w"] * ks[h].astype(F32) for h in heads]
            kv = [_tn(vs[h], wk[h].astype(BF)) for h in heads]
            for h in heads:
                hm = _mlstm_out(f[h])["hm"]
                gate = _sigmoid(mo_ref[rows, h * DV:(h + 1) * DV].astype(F32))
                y_ref[rows, h * DV:(h + 1) * DV] = _headnorm_fwd(hm, hg_ref[:, h * DV:(h + 1) * DV], gate).astype(BF)
                cs_ref[c, h] = f[h]["cb"]
                ns_ref[c, h] = jnp.where(io8 == 0, nrow[h], jnp.where(io8 == 1, m11[h], 0.0))
                c_scr[h] = f[h]["a_s"] * cst[h] + kv[h]
                n_scr[h, 0:1, :] = f[h]["a_s"] * nrow[h] + jnp.sum(wk[h], axis=0, keepdims=True)
                n_scr[h, 1:2, :] = jnp.broadcast_to(f[h]["m_new"], (1, DQK))
            return carry

        lax.fori_loop(0, CPB, chunk, 0, unroll=2)

    return pl.pallas_call(
        body, grid=(nb,),
        in_specs=[_row_spec(D), _row_spec(D, CB_MV), _row_spec(D, CB_MO), _row_spec(N_SMALL), _full_spec((1, N_SMALL)), _full_spec((1, D))],
        out_specs=[_row_spec(D), pl.BlockSpec((CPB, NH, DV, DQK), lambda i: (i, 0, 0, 0)),
                   pl.BlockSpec((CPB, NH, 8, DQK), lambda i: (i, 0, 0, 0))],
        out_shape=(jax.ShapeDtypeStruct((tp, D), BF), jax.ShapeDtypeStruct((tp // CHUNK, NH, DV, DQK), BF),
                   jax.ShapeDtypeStruct((tp // CHUNK, NH, 8, DQK), F32)),
        scratch_shapes=[pltpu.VMEM((NH, DV, DQK), F32), pltpu.VMEM((NH, 8, DQK), F32)],
        compiler_params=_params(), name=name)(qk, pbig, pbig, small, gbias, headg)


def _mlstm_bwd(dy, qk, pbig, small, gbias, headg, cs, ns, dproj, name, ride=()):
    tp = qk.shape[0]
    nb = tp // TM
    nr = len(ride)

    def body(*refs):
        dy_ref, qk_ref, v_ref, mo_ref, sm_ref, gb_ref, hg_ref, cs_ref, ns_ref = refs[:9]
        ride_in = refs[10:10 + nr]
        dqk_ref, dproj_ref, dsm_ref, dgb_ref, dhg_ref = refs[10 + nr:15 + nr]
        ride_out = refs[15 + nr:15 + 2 * nr]
        dc_scr, dn_scr = refs[15 + 2 * nr:17 + 2 * nr]
        step = pl.program_id(0)
        blk = nb - 1 - step
        sent = _scatter_copies(ride_in, ride_out, *refs[17 + 2 * nr:]) if nr else []

        @pl.when(step == 0)
        def _():
            dc_scr[...] = jnp.zeros_like(dc_scr)
            dn_scr[...] = jnp.zeros_like(dn_scr)
            dgb_ref[...] = jnp.zeros_like(dgb_ref)
            dhg_ref[...] = jnp.zeros_like(dhg_ref)
            for cp in sent:
                cp.start()

        k = _chunk_consts()
        lane, rowio = k["lane"], k["rowio"]

        def chunk(cc, carry):
            c = CPB - 1 - cc
            r0 = pl.multiple_of(c * CHUNK, CHUNK)
            rows = pl.ds(r0, CHUNK)
            valid = _valid_rows(blk, c)
            pre, li_all, b_all = _mlstm_gates(sm_ref[rows, :], gb_ref[...], valid, k)
            dli_all = jnp.zeros((CHUNK, N_SMALL), F32)
            db_all = jnp.zeros((CHUNK, N_SMALL), F32)
            heads = range(NH)
            qs = [qk_ref[rows, h * DQK:(h + 1) * DQK] for h in heads]
            ks = [qk_ref[rows, NH * DQK + h * DQK:NH * DQK + (h + 1) * DQK] for h in heads]
            vs = [v_ref[rows, h * DV:(h + 1) * DV] for h in heads]
            cst = [cs_ref[c, h].astype(F32) for h in heads]
            nrow = [ns_ref[c, h, 0:1, :] for h in heads]
            m11 = [jnp.max(ns_ref[c, h, 1:2, :], axis=1, keepdims=True) for h in heads]
            f = [_mlstm_open(h, qs[h], ks[h], cst[h], li_all, b_all, k) for h in heads]
            f = [_mlstm_weights(h, f[h], qs[h], vs[h], li_all, b_all, nrow[h], m11[h], k) for h in heads]
            f = [_mlstm_out(f[h]) for h in heads]
            t = []
            for h in heads:
                gain = hg_ref[:, h * DV:(h + 1) * DV]
                gate = _sigmoid(mo_ref[rows, h * DV:(h + 1) * DV].astype(F32))
                dhm, dgate, dgain = _headnorm_bwd(dy_ref[rows, h * DV:(h + 1) * DV].astype(F32), f[h]["hm"], gain, gate)
                dproj_ref[rows, D + h * DV:D + (h + 1) * DV] = (dgate * gate * (1.0 - gate)).astype(BF)
                dhg_ref[:, h * DV:(h + 1) * DV] += dgain
                r, den = f[h]["r"], f[h]["den"]
                dnum = dhm / r
                dr = -jnp.sum(dhm * f[h]["hm"], axis=1, keepdims=True) / r
                dden = jnp.where(jnp.abs(den) > f[h]["floor"], dr * jnp.sign(den), 0.0)
                dnb = dnum.astype(BF)
                dc_new = dc_scr[h]
                dcb = dc_new.astype(BF)
                t.append(dict(dnum=dnum, dden=dden, dnb=dnb, dc_new=dc_new, dn_new=dn_scr[h],
                              dwm=_nt(dnb, vs[h]), vdc=_nn(vs[h], dcb), kdc=_nt(ks[h], dcb)))
            for h in heads:
                dw_mat = t[h]["dwm"] + t[h]["dden"]
                dsim = (f[h]["e"] * dw_mat).astype(BF)
                gm = f[h]["w_mat"] * dw_mat
                t[h].update(gm=gm, dv0=_tn(f[h]["w_mat"].astype(BF), t[h]["dnb"]), dq0=_nn(dsim, ks[h]),
                            dq1=_nn(t[h]["dnb"], f[h]["cb"]), dk0=_tn(dsim, qs[h]),
                            dcq=_tn((f[h]["a"] * t[h]["dnum"]).astype(BF), qs[h]), cs2=_tn(gm, k["ones"], precision=HI))
            for h in heads:
                a, w, a_s = f[h]["a"], f[h]["w"], f[h]["a_s"]
                dnum, dden, dc_new, dn_new, vdc, gm = (t[h][n] for n in ("dnum", "dden", "dc_new", "dn_new", "vdc", "gm"))
                kf = ks[h].astype(F32)
                dproj_ref[rows, h * DV:(h + 1) * DV] = (t[h]["dv0"] + w * t[h]["kdc"]).astype(BF)
                adden = a * dden
                dqk_ref[rows, h * DQK:(h + 1) * DQK] = t[h]["dq0"] + a * t[h]["dq1"] + adden * nrow[h]
                dqk_ref[rows, NH * DQK + h * DQK:NH * DQK + (h + 1) * DQK] = t[h]["dk0"] + w * vdc + w * dn_new
                da = jnp.sum(dnum * f[h]["cq"], axis=1, keepdims=True) + dden * f[h]["nq"]
                dw = jnp.sum(vdc * kf, axis=1, keepdims=True) + jnp.sum(kf * dn_new, axis=1, keepdims=True)
                da_s = _sum_all(dc_new * cst[h]) + jnp.sum(dn_new * nrow[h], axis=1, keepdims=True)
                wdw = w * dw
                rs = jnp.sum(gm, axis=1, keepdims=True)
                cs_col = _col(t[h]["cs2"], lane, 0)
                dg = a_s * da_s + jnp.sum(wdw, axis=0, keepdims=True)
                db = a * da + rs - cs_col - wdw + jnp.where(rowio == CHUNK - 1, dg, 0.0)
                dli_all = dli_all + jnp.where(lane == h, cs_col + wdw, 0.0)
                db_all = db_all + jnp.where(lane == NH + h, db, 0.0)
                dc_scr[h] = a_s * dc_new + t[h]["dcq"]
                dn_scr[h] = a_s * dn_new + jnp.sum(adden * f[h]["qf"], axis=0, keepdims=True)
            dlf_all = _nn(k["triu_f"], db_all, precision=HI)
            dsm = jnp.where(valid, dli_all + dlf_all * _sigmoid(-pre), 0.0)
            dsm = jnp.where(lane < 2 * NH, dsm, 0.0)
            dsm_ref[rows, :] = dsm
            dgb_ref[0:1, :] += jnp.sum(dsm, axis=0, keepdims=True)
            return carry

        lax.fori_loop(0, CPB, chunk, 0, unroll=2)

        if nr:
            @pl.when(step == nb - 1)
            def _():
                for cp in sent:
                    cp.wait_recv()
                for cp in sent:
                    cp.wait_send()

    rev = lambda col: (lambda i: (nb - 1 - i, col))
    rspec = lambda width, col=0: pl.BlockSpec((TM, width), rev(col))
    ride_shapes, ride_sems = _scatter_shapes(ride) if nr else ((), [])
    outs = pl.pallas_call(
        body, grid=(nb,),
        in_specs=[rspec(D), rspec(D), rspec(D, CB_MV), rspec(D, CB_MO), rspec(N_SMALL), _full_spec((1, N_SMALL)), _full_spec((1, D)),
                  pl.BlockSpec((CPB, NH, DV, DQK), lambda i: (nb - 1 - i, 0, 0, 0)),
                  pl.BlockSpec((CPB, NH, 8, DQK), lambda i: (nb - 1 - i, 0, 0, 0)), ANY] + [ANY] * nr,
        out_specs=[rspec(D), rspec(2 * D, CB_MV // 2), rspec(N_SMALL), _full_spec((8, N_SMALL)), _full_spec((1, D))] + [ANY] * nr,
        out_shape=(jax.ShapeDtypeStruct((tp, D), F32), jax.ShapeDtypeStruct(dproj.shape, BF),
                   jax.ShapeDtypeStruct((tp, N_SMALL), F32), jax.ShapeDtypeStruct((8, N_SMALL), F32),
                   jax.ShapeDtypeStruct((1, D), F32)) + tuple(ride_shapes),
        scratch_shapes=[pltpu.VMEM((NH, DV, DQK), F32), pltpu.VMEM((NH, 1, DQK), F32)] + ride_sems,
        input_output_aliases={9: 1}, compiler_params=_params(), name=name)(dy, qk, pbig, pbig, small, gbias, headg, cs, ns, dproj, *ride)
    return tuple(outs[:5]) + (list(outs[5:]),)


def _gla_loga(sm_ref, a2_ref, a2b_ref, blk):
    za = _nn(sm_ref[...].astype(BF), a2_ref[...]) + a2b_ref[...]
    row = blk * TM + lax.broadcasted_iota(jnp.int32, (TM, 1), 0)
    return za, jnp.where(row >= FIRST_VALID, _logsig(za) / G_TAU, 0.0)


def _gla_head(h, q_ref, k_ref, rows, bc, btot, k):
    sl = slice(h * DQK, (h + 1) * DQK)
    bch = bc[:, sl]
    bth = btot[:, sl]
    gq = q_ref[rows, h * DQK:(h + 1) * DQK].astype(F32)
    gk = k_ref[rows, NH * DQK + h * DQK:NH * DQK + (h + 1) * DQK].astype(F32)
    e_pos = jnp.exp(bch) * (DQK ** -0.5)
    e_neg = jnp.exp(-bch)
    e_end = jnp.exp(bth - bch)
    qd = gq * e_pos
    ki = gk * e_neg
    ke = gk * e_end
    att = jnp.where(k["tri"], _nt(qd.astype(BF), ki.astype(BF)), 0.0)
    return dict(e_pos=e_pos, e_neg=e_neg, e_end=e_end, qd=qd, ki=ki, ke=ke, att=att, decay=jnp.exp(bth))


def _gla_fwd(pbig, small, a2p, a2b, headg, name):
    tp = pbig.shape[0]
    nb = tp // TM

    def body(qk_ref, v_ref, gr_ref, sm_ref, a2_ref, a2b_ref, hg_ref, y_ref, ss_ref, s_scr, lg_scr):
        blk = pl.program_id(0)

        @pl.when(blk == 0)
        def _():
            s_scr[...] = jnp.zeros_like(s_scr)

        k = _chunk_consts(G_CHUNK)
        _, loga = _gla_loga(sm_ref, a2_ref, a2b_ref, blk)
        lg_scr[...] = loga

        def chunk(c, carry):
            r0 = pl.multiple_of(c * G_CHUNK, G_CHUNK)
            rows = pl.ds(r0, G_CHUNK)
            bc = _nn(k["tril_f"], lg_scr[rows, :], precision=HI)
            btot = _last_row(bc, k["rowio"])
            heads = range(NH)
            f = [_gla_head(h, qk_ref, qk_ref, rows, bc, btot, k) for h in heads]
            vs = [v_ref[rows, h * DV:(h + 1) * DV] for h in heads]
            sst = [s_scr[h] for h in heads]
            sbs = [s.astype(BF) for s in sst]
            inter = [_nt(f[h]["qd"].astype(BF), sbs[h]) for h in heads]
            intra = [_nn(f[h]["att"].astype(BF), vs[h]) for h in heads]
            kv = [_tn(vs[h], f[h]["ke"].astype(BF)) for h in heads]
            for h in heads:
                gr = gr_ref[rows, h * DV:(h + 1) * DV].astype(F32)
                y_ref[rows, h * DV:(h + 1) * DV] = _headnorm_fwd(intra[h] + inter[h], hg_ref[:, h * DV:(h + 1) * DV],
                                                                   gr * _sigmoid(gr)).astype(BF)
                ss_ref[c, h] = sbs[h]
                s_scr[h] = sst[h] * f[h]["decay"] + kv[h]
            return carry

        lax.fori_loop(0, G_CPB, chunk, 0, unroll=2)

    return pl.pallas_call(
        body, grid=(nb,),
        in_specs=[_row_spec(D, CB_GQK), _row_spec(D, CB_GV), _row_spec(D, CB_GR), _row_spec(N_SMALL),
                  _full_spec((N_SMALL, NH * DQK)), _full_spec((1, NH * DQK)), _full_spec((1, D))],
        out_specs=[_row_spec(D), pl.BlockSpec((G_CPB, NH, DV, DQK), lambda i: (i, 0, 0, 0))],
        out_shape=(jax.ShapeDtypeStruct((tp, D), BF), jax.ShapeDtypeStruct((tp // G_CHUNK, NH, DV, DQK), BF)),
        scratch_shapes=[pltpu.VMEM((NH, DV, DQK), F32), pltpu.VMEM((TM, NH * DQK), F32)],
        compiler_params=_params(), name=name)(pbig, pbig, pbig, small, a2p, a2b, headg)


def _gla_bwd(dy, pbig, small, a2p, a2b, headg, ss, dsm_m, dproj, name):
    tp = pbig.shape[0]
    nb = tp // TM
    nqk = NH * DQK

    def body(dy_ref, qk_ref, v_ref, gr_ref, sm_ref, a2_ref, a2b_ref, hg_ref, ss_ref, dsmm_ref, _,
             dproj_ref, dsm_ref, da2_ref, da2b_ref, dhg_ref, ds_scr, lg_scr, dza_scr):
        step = pl.program_id(0)
        blk = nb - 1 - step

        @pl.when(step == 0)
        def _():
            ds_scr[...] = jnp.zeros_like(ds_scr)
            da2_ref[...] = jnp.zeros_like(da2_ref)
            da2b_ref[...] = jnp.zeros_like(da2b_ref)
            dhg_ref[...] = jnp.zeros_like(dhg_ref)

        k = _chunk_consts(G_CHUNK)
        rowio = k["rowio"]
        za, loga = _gla_loga(sm_ref, a2_ref, a2b_ref, blk)
        lg_scr[...] = loga

        def chunk(cc, carry):
            c = G_CPB - 1 - cc
            r0 = pl.multiple_of(c * G_CHUNK, G_CHUNK)
            rows = pl.ds(r0, G_CHUNK)
            bc = _nn(k["tril_f"], lg_scr[rows, :], precision=HI)
            btot = _last_row(bc, rowio)
            heads = range(NH)
            f = [_gla_head(h, qk_ref, qk_ref, rows, bc, btot, k) for h in heads]
            vs = [v_ref[rows, h * DV:(h + 1) * DV] for h in heads]
            sbs = [ss_ref[c, h] for h in heads]
            qdb = [f[h]["qd"].astype(BF) for h in heads]
            attb = [f[h]["att"].astype(BF) for h in heads]
            inter = [_nt(qdb[h], sbs[h]) for h in heads]
            intra = [_nn(attb[h], vs[h]) for h in heads]
            dsn = [ds_scr[h] for h in heads]
            dsb = [d.astype(BF) for d in dsn]
            dke = [_nn(vs[h], dsb[h]) for h in heads]
            dv1 = [_nt(f[h]["ke"].astype(BF), dsb[h]) for h in heads]
            t = []
            for h in heads:
                gr = gr_ref[rows, h * DV:(h + 1) * DV].astype(F32)
                sg = _sigmoid(gr)
                gain = hg_ref[:, h * DV:(h + 1) * DV]
                do, dact, dgain = _headnorm_bwd(dy_ref[rows, h * DV:(h + 1) * DV].astype(F32), intra[h] + inter[h], gain, gr * sg)
                dproj_ref[rows, 2 * D + h * DV:2 * D + (h + 1) * DV] = (dact * sg * (1.0 + gr * (1.0 - sg))).astype(BF)
                dhg_ref[:, h * DV:(h + 1) * DV] += dgain
                dob = do.astype(BF)
                t.append(dict(dob=dob, datt=_nt(dob, vs[h]), dv0=_tn(attb[h], dob), dq1=_nn(dob, sbs[h]), dsq=_tn(dob, qdb[h])))
            for h in heads:
                datt = jnp.where(k["tri"], t[h]["datt"], 0.0).astype(BF)
                t[h].update(dq0=_nn(datt, f[h]["ki"].astype(BF)), dki=_tn(datt, qdb[h]))
            dbc_parts = []
            for h in heads:
                dqd = t[h]["dq0"] + t[h]["dq1"]
                dki = t[h]["dki"]
                dproj_ref[rows, D + h * DV:D + (h + 1) * DV] = (t[h]["dv0"] + dv1[h]).astype(BF)
                dproj_ref[rows, h * DQK:(h + 1) * DQK] = (dqd * f[h]["e_pos"]).astype(BF)
                dproj_ref[rows, nqk + h * DQK:nqk + (h + 1) * DQK] = (dki * f[h]["e_neg"] + dke[h] * f[h]["e_end"]).astype(BF)
                dke_ke = dke[h] * f[h]["ke"]
                dbtot = (jnp.sum(dke_ke, axis=0, keepdims=True)
                         + jnp.sum(dsn[h] * sbs[h].astype(F32), axis=0, keepdims=True) * f[h]["decay"])
                dbc_parts.append(dqd * f[h]["qd"] - dki * f[h]["ki"] - dke_ke + jnp.where(rowio == G_CHUNK - 1, dbtot, 0.0))
                ds_scr[h] = dsn[h] * f[h]["decay"] + t[h]["dsq"]
            dbc = jnp.concatenate(dbc_parts, axis=1)
            dza_scr[rows, :] = _nn(k["triu_f"], dbc, precision=HI)
            return carry

        lax.fori_loop(0, G_CPB, chunk, 0, unroll=2)
        row = blk * TM + lax.broadcasted_iota(jnp.int32, (TM, 1), 0)
        dza = jnp.where(row >= FIRST_VALID, dza_scr[...] * (_sigmoid(-za) / G_TAU), 0.0)
        dzb = dza.astype(BF)
        dsm_ref[...] = (_nt(dzb, a2_ref[...]) + dsmm_ref[...]).astype(BF)
        da2_ref[...] += _tn(sm_ref[...].astype(BF), dzb)
        da2b_ref[...] += jnp.sum(dza, axis=0, keepdims=True)

    rspec = lambda width, col=0: pl.BlockSpec((TM, width), lambda i: (nb - 1 - i, col))
    return pl.pallas_call(
        body, grid=(nb,),
        in_specs=[rspec(D), rspec(D, CB_GQK), rspec(D, CB_GV), rspec(D, CB_GR), rspec(N_SMALL),
                  _full_spec((N_SMALL, nqk)), _full_spec((1, nqk)), _full_spec((1, D)),
                  pl.BlockSpec((G_CPB, NH, DV, DQK), lambda i: (nb - 1 - i, 0, 0, 0)), rspec(N_SMALL), ANY],
        out_specs=[rspec(3 * D, 0), rspec(N_SMALL), _full_spec((N_SMALL, nqk)), _full_spec((1, nqk)), _full_spec((1, D))],
        out_shape=(jax.ShapeDtypeStruct(dproj.shape, BF),
                   jax.ShapeDtypeStruct((tp, N_SMALL), BF), jax.ShapeDtypeStruct((N_SMALL, nqk), F32),
                   jax.ShapeDtypeStruct((1, nqk), F32), jax.ShapeDtypeStruct((1, D), F32)),
        scratch_shapes=[pltpu.VMEM((NH, DV, DQK), F32), pltpu.VMEM((TM, nqk), F32), pltpu.VMEM((TM, nqk), F32)],
        input_output_aliases={10: 0}, compiler_params=_params(), name=name)(dy, pbig, pbig, pbig, small, a2p, a2b, headg, ss, dsm_m, dproj)


PIECE_BYTES = 1 << 20
MAX_PIECES = 32


def _place():
    return lax.axis_index("x"), lax.axis_index("y"), lax.axis_index("c")


def _piece_rows(rows, row_bytes, align):
    want = min(MAX_PIECES, max(1, -(-rows * row_bytes // PIECE_BYTES)))
    best = rows
    for k in range(1, want + 1):
        if rows % k == 0 and (rows // k) % align == 0:
            best = rows // k
    return best


def _remote(src, dst, send_sems, recv_sems, k, to):
    return pltpu.make_async_remote_copy(src_ref=src, dst_ref=dst, send_sem=send_sems.at[k], recv_sem=recv_sems.at[k],
                                        device_id=to, device_id_type=MESH)


def _all_gather_chips(p, name):
    rd = _gather_rider(p)

    def body(*refs):
        start, middle, finish = rd["make"](refs[:1], refs[1:2], refs[2:])
        start()
        middle()
        finish()

    return pl.pallas_call(body, in_specs=[ANY], out_specs=[ANY], out_shape=rd["out_shapes"], scratch_shapes=rd["sems"],
                          name=name)(p)[0]


def _gather_rider(p):
    r, n = p.shape
    rh = r // 2
    align = 32 // p.dtype.itemsize
    assert r % (2 * align) == 0
    cr = _piece_rows(rh, n * p.dtype.itemsize, align)

    def make(in_refs, out_refs, sem_refs):
        p_ref, o_ref = in_refs[0], out_refs[0]
        send_sems, recv_sems = sem_refs
        x, y, c = _place()
        chips = [(1 - x, y), (x, 1 - y), (1 - x, 1 - y)]
        sib = (x, y, 1 - c)

        def half(hc, piece=None):
            if piece is None:
                return pl.ds(pl.multiple_of(hc * rh, align), rh)
            return pl.ds(pl.multiple_of(hc * rh + piece * cr, align), cr)

        first = [_remote(p_ref.at[half(c)], o_ref.at[j, half(c)], send_sems, recv_sems, j, (*chip, c))
                 for j, chip in enumerate(chips)]
        passed = [[_remote(o_ref.at[j, half(c, i)], o_ref.at[j, half(c, i)], send_sems, recv_sems, 3 + j, sib)
                   for i in range(rh // cr)] for j in range(3)]
        blocks = [_remote(o_ref.at[j, half(c)], o_ref.at[j, half(1 - c)], send_sems, recv_sems, 3 + j, sib) for j in range(3)]

        def start():
            for cp in first:
                cp.start()

        def middle():
            for j, cp in enumerate(first):
                cp.wait_recv()
                for piece in passed[j]:
                    piece.start()

        def finish():
            for block in blocks:
                block.wait_send()
                block.wait_recv()
            for cp in first:
                cp.wait_send()

        return start, middle, finish

    return dict(inputs=[p], out_shapes=(jax.ShapeDtypeStruct((3, r, n), p.dtype),),
                sems=[pltpu.SemaphoreType.DMA((6,)), pltpu.SemaphoreType.DMA((6,))], make=make)


def _scatter_rider(items):
    out_shapes, sems = _scatter_shapes(items)

    def make(in_refs, out_refs, sem_refs):
        sent = _scatter_copies(in_refs, out_refs, *sem_refs)

        def start():
            for cp in sent:
                cp.start()

        def finish():
            for cp in sent:
                cp.wait_recv()
            for cp in sent:
                cp.wait_send()

        return start, (lambda: None), finish

    return dict(inputs=list(items), out_shapes=out_shapes, sems=sems, make=make)


def _by_chip(mine, others):
    me = 2 * lax.axis_index("x") + lax.axis_index("y")
    by_mask = jnp.stack([mine, others[1], others[0], others[2]])
    return [lax.dynamic_index_in_dim(by_mask, q ^ me, 0, keepdims=False) for q in range(4)]


def _swap_halves(items, name):
    k = len(items)

    def body(*refs):
        a_refs, got_refs = refs[:k], refs[k:2 * k]
        send_sems, recv_sems = refs[2 * k:]
        x, y, c = _place()
        sib = (x, y, 1 - c)
        for i, a in enumerate(items):
            _, r, n = a.shape
            rh = r // 2
            cr = _piece_rows(rh, n * a.dtype.itemsize, 8)
            for q in range(4):
                for t in range(rh // cr):
                    other = pl.ds(pl.multiple_of((1 - c) * rh + t * cr, 8), cr)
                    _remote(a_refs[i].at[q, other], got_refs[i].at[q, pl.ds(t * cr, cr)], send_sems, recv_sems, i, sib).start()
        for i, a in enumerate(items):
            block = _remote(a_refs[i].at[:, pl.ds(0, a.shape[1] // 2)], got_refs[i], send_sems, recv_sems, i, sib)
            block.wait_send()
            block.wait_recv()

    return pl.pallas_call(
        body, in_specs=[ANY] * k, out_specs=[ANY] * k,
        out_shape=tuple(jax.ShapeDtypeStruct((4, a.shape[1] // 2, a.shape[2]), a.dtype) for a in items),
        scratch_shapes=[pltpu.SemaphoreType.DMA((k,)), pltpu.SemaphoreType.DMA((k,))], name=name)(*items)


def _scatter_copies(s_refs, o_refs, send_sems, recv_sems):
    x, y, c = _place()
    chips = [(1 - x, y), (x, 1 - y), (1 - x, 1 - y)]
    return [_remote(s_refs[i].at[2 * cx + cy], o_refs[i].at[j], send_sems, recv_sems, 3 * i + j, (cx, cy, c))
            for i in range(len(s_refs)) for j, (cx, cy) in enumerate(chips)]


def _scatter_shapes(items):
    k = len(items)
    return (tuple(jax.ShapeDtypeStruct((3,) + s.shape[1:], s.dtype) for s in items),
            [pltpu.SemaphoreType.DMA((3 * k,)), pltpu.SemaphoreType.DMA((3 * k,))])


def _scatter_chips(items, name):
    k = len(items)

    def body(*refs):
        sent = _scatter_copies(refs[:k], refs[k:2 * k], *refs[2 * k:])
        for cp in sent:
            cp.start()
        for cp in sent:
            cp.wait_recv()
        for cp in sent:
            cp.wait_send()

    out_shape, scratch = _scatter_shapes(items)
    return pl.pallas_call(body, in_specs=[ANY] * k, out_specs=[ANY] * k, out_shape=out_shape, scratch_shapes=scratch,
                          name=name)(*items)


def _join_halves(items, name):
    k = len(items)

    def body(*refs):
        f_refs, o_refs = refs[:k], refs[k:2 * k]
        send_sems, recv_sems = refs[2 * k:]
        x, y, c = _place()
        sib = (x, y, 1 - c)
        for i, f in enumerate(items):
            rh, n = f.shape
            cr = _piece_rows(rh, n * f.dtype.itemsize, 8)
            for t in range(rh // cr):
                rows = pl.ds(t * cr, cr)
                _remote(f_refs[i].at[rows], o_refs[i].at[rows], send_sems, recv_sems, i, sib).start()
        for i in range(k):
            block = _remote(f_refs[i], o_refs[i], send_sems, recv_sems, i, sib)
            block.wait_send()
            block.wait_recv()

    return pl.pallas_call(
        body, in_specs=[ANY] * k, out_specs=[ANY] * k, out_shape=tuple(jax.ShapeDtypeStruct(f.shape, f.dtype) for f in items),
        scratch_shapes=[pltpu.SemaphoreType.DMA((k,)), pltpu.SemaphoreType.DMA((k,))], name=name)(*items)


SMALL_ROWS = 16
SMALL_GRAD_ROWS = 48
SMALL_SHARD_SHAPES = [(N_META, 256), (4, 256), (G_RANK, 128), (NH, 64), (NH, 64)]
REPL_SHAPES = [(1, D), (1, D), (1, 2, NH), (1, NH * DQK), (1, D), (D,)]
W_IN_SHARD = 2054
W_IN_BLOCK = 2080


def _pack_small(parts, rows=SMALL_ROWS):
    flat = jnp.concatenate([p.reshape(-1) for p in parts])
    return jnp.pad(flat, (0, rows * D - flat.shape[0])).reshape(rows, D)


def _unpack_small(block, shapes):
    flat, out, off = block.reshape(-1), [], 0
    for shp in shapes:
        n = 1
        for s in shp:
            n *= s
        out.append(flat[off:off + n].reshape(shp))
        off += n
    return out


def _proj_rows_from_w_in(w_in_t):
    w_big = jnp.concatenate([w_in_t[3080:5128], w_in_t[5144:6168], w_in_t[0:1024], w_in_t[6168:8216],
                             w_in_t[1024:2048], w_in_t[2056:3080]], axis=0)
    w_small = jnp.concatenate([w_in_t[2048:2056], w_in_t[5128:5144], jnp.zeros((N_SMALL - 24, D), w_in_t.dtype)], axis=0)
    return w_big, w_small


def _w_in_from_proj_rows(d_wall_t):
    big, small = d_wall_t[0:N_BIG], d_wall_t[N_BIG:N_ALL]
    return jnp.concatenate([big[3072:4096], big[6144:7168], small[0:8], big[7168:8192], big[0:2048],
                            small[8:24], big[2048:3072], big[4096:6144]], axis=0)


def kernel(x, meta_tokens, norm1_g, w_in, conv_w, conv_b, m_gate_b, g_a2, g_a2_b, m_head_g, g_head_g, w_branch_m, w_branch_g, w_out, norm2_g, w_ff_gate, w_ff_up, w_ff_down, final_g, loss_target, m_meta_tokens, m_norm1_g, m_w_in, m_conv_w, m_conv_b, m_m_gate_b, m_g_a2, m_g_a2_b, m_m_head_g, m_g_head_g, m_w_branch_m, m_w_branch_g, m_w_out, m_norm2_g, m_w_ff_gate, m_w_ff_up, m_w_ff_down, m_final_g, v_meta_tokens, v_norm1_g, v_w_in, v_conv_w, v_conv_b, v_m_gate_b, v_g_a2, v_g_a2_b, v_m_head_g, v_g_head_g, v_w_branch_m, v_w_branch_g, v_w_out, v_norm2_g, v_w_ff_gate, v_w_ff_up, v_w_ff_down, v_final_g):
    w = _gather_weights(w_in, w_branch_m, w_branch_g, w_out, w_ff_gate, w_ff_up, w_ff_down, meta_tokens, conv_w, g_a2, m_head_g, g_head_g)
    loss_local, dx, grads = _local_step(x[0], loss_target[0], w, norm1_g, conv_b, m_gate_b, g_a2_b, norm2_g, final_g, _Reducer())

    weights = [w_in, w_branch_m, w_branch_g, w_out, w_ff_gate, w_ff_up, w_ff_down, meta_tokens, conv_w, g_a2, m_head_g, g_head_g,
               norm1_g, conv_b, m_gate_b, g_a2_b, norm2_g, final_g]
    moms = [m_w_in, m_w_branch_m, m_w_branch_g, m_w_out, m_w_ff_gate, m_w_ff_up, m_w_ff_down, m_meta_tokens, m_conv_w, m_g_a2,
            m_m_head_g, m_g_head_g, m_norm1_g, m_conv_b, m_m_gate_b, m_g_a2_b, m_norm2_g, m_final_g]
    vels = [v_w_in, v_w_branch_m, v_w_branch_g, v_w_out, v_w_ff_gate, v_w_ff_up, v_w_ff_down, v_meta_tokens, v_conv_w, v_g_a2,
            v_m_head_g, v_g_head_g, v_norm1_g, v_conv_b, v_m_gate_b, v_g_a2_b, v_norm2_g, v_final_g]
    res = {}
    for nm, wt, g, m, v in zip(PACK_ORDER, weights, grads, moms, vels):
        if nm in TRANSPOSED_GRADS:
            to2d = lambda a: jnp.swapaxes(a, -1, -2).reshape(a.shape[-1], a.shape[-2])
            back = lambda a: jnp.swapaxes(a, 0, 1).reshape(wt.shape)
        else:
            to2d = lambda a: a.reshape(wt.size // wt.shape[-1], wt.shape[-1])
            back = lambda a: a.reshape(wt.shape)
        d, nm_, nv_ = _adamw(to2d(wt), g, to2d(m), to2d(v), "adamw_" + nm)
        res[nm] = (back(g), back(d), back(nm_), back(nv_))

    order = ["meta_tokens", "norm1_g", "w_in", "conv_w", "conv_b", "m_gate_b", "g_a2", "g_a2_b", "m_head_g", "g_head_g",
             "w_branch_m", "w_branch_g", "w_out", "norm2_g", "w_ff_gate", "w_ff_up", "w_ff_down", "final_g"]
    loss = lax.psum(loss_local[0, 0], ("x", "y", "c"))
    grad_x = dx.reshape(x.shape)
    return (loss, grad_x, *[res[n][0] for n in order], *[res[n][1] for n in order],
            *[res[n][2] for n in order], *[res[n][3] for n in order])


TRANSPOSED_GRADS = ("w_in", "w_ff_gate", "w_ff_up")
PACK_ORDER = ["w_in", "w_branch_m", "w_branch_g", "w_out", "w_ff_gate", "w_ff_up", "w_ff_down", "meta_tokens", "conv_w", "g_a2",
              "m_head_g", "g_head_g", "norm1_g", "conv_b", "m_gate_b", "g_a2_b", "norm2_g", "final_g"]


def _gather_weights(w_in, w_branch_m, w_branch_g, w_out, w_ff_gate, w_ff_up, w_ff_down, meta_tokens, conv_w, g_a2, m_head_g, g_head_g):
    bf = lambda a: a.astype(BF)
    rows_local = jnp.concatenate([bf(w_branch_m[0]), bf(w_branch_g[0]), bf(w_out[0]), bf(w_ff_down[0]),
                                  bf(w_ff_gate[0].T), bf(w_ff_up[0].T)], axis=0)
    win_local = jnp.pad(bf(w_in[0].T), ((0, W_IN_BLOCK - W_IN_SHARD), (0, 0)))
    small_local = _pack_small([meta_tokens, conv_w[0], g_a2[0], m_head_g[0], g_head_g[0]])
    small_all = _by_chip(small_local, _all_gather_chips(small_local, "gather_small"))
    small_sh = [_unpack_small(small_all[q], SMALL_SHARD_SHAPES) for q in range(4)]
    cat = lambda i: jnp.concatenate([s[i] for s in small_sh], axis=-1)
    return dict(win_local=win_local, rows_local=rows_local, meta=cat(0), convw=cat(1), ga2=cat(2),
                mhg=cat(3).reshape(1, D), ghg=cat(4).reshape(1, D))


def _row_weights(rows_local, gathered):
    rows_all = jnp.stack(_by_chip(rows_local, gathered))
    cut = lambda lo, hi: rows_all[:, lo:hi].reshape(4 * (hi - lo), D)
    return cut(0, 256), cut(256, 512), cut(512, 768), cut(768, 1472), _ffn_weight_rows(cut(1472, 2176), cut(2176, 2880))


def _local_step(x0, target, w, norm1_g, conv_b, m_gate_b, g_a2_b, norm2_g, final_g, reducer):
    meta_f, convw_f, ga2_f, mhg_f, ghg_f = w["meta"], w["convw"], w["ga2"], w["mhg"], w["ghg"]
    gbias =jnp.concatenate([m_gate_b.reshape(1, 2 * NH), jnp.zeros((1, N_SMALL - 2 * NH), F32)], axis=1)
    a2p = jnp.concatenate([jnp.zeros((8, NH * DQK), F32), ga2_f, jnp.zeros((N_SMALL - 24, NH * DQK), F32)], axis=0).astype(BF)
    convb = conv_b.reshape(1, D)
    g1 = norm1_g.reshape(1, D)
    g2 = norm2_g.reshape(1, D)
    gf = final_g.reshape(1, D)
    first = jnp.concatenate([jnp.zeros((FIRST_VALID, D), F32), meta_f], axis=0)

    h0, xn1, rstd1, win_gathered = _embed_norm(x0, first, g1, _gather_rider(w["win_local"]), "rms1")
    win_all = _by_chip(w["win_local"], win_gathered[0])
    w_in_f = jnp.concatenate([win_all[q][0:W_IN_SHARD] for q in range(4)], axis=0)
    w_big, w_small = _proj_rows_from_w_in(w_in_f)
    w_all = jnp.concatenate([w_big, w_small], axis=0)
    pbig, rows_gathered = _mm(xn1, w_big, nt=True, out_dtype=BF, tn=1024, name="proj_big", rider=_gather_rider(w["rows_local"]))
    wbm, wbg, wout, wdown, wgu_t = _row_weights(w["rows_local"], rows_gathered[0])
    small = _mm(xn1, w_small, nt=True, out_dtype=F32, tn=N_SMALL, name="proj_small")
    qk = _conv_fwd(pbig, convw_f, convb, "conv_fwd")
    y_m, m_cs, m_ns = _mlstm_fwd(qk, pbig, small, gbias, mhg_f, "mlstm_fwd")
    y_g, g_ss = _gla_fwd(pbig, small, a2p, g_a2_b, ghg_f, "gla_fwd")
    p_m, p_g, merged = _branch_merge(y_m, y_g, wbm, wbg, pbig, "branch_merge")
    h1, hn, rstd2 = _out_proj_norm(merged, wout, h0, g2, "out_proj")
    gu, ff = _ffn_in(hn, wgu_t, "ff_in")
    dh2, loss_local, d_final_g = _ffn_down_loss(ff, wdown, h1, target, gf, "ff_down_loss")

    d_wdown = _mm_tn(ff, dh2, tm=1408, tn=1024, name="dw_ff_down")
    dgu = _ffn_d_hidden(dh2, wdown, gu, "d_ff")
    d_wgu_t = _mm_tn(dgu, hn, tm=1408, tn=1024, name="dw_ff_in")
    dh1, d_g2 = _ffn_d_in(dgu, wgu_t, h1, rstd2, g2, dh2, "d_hn")
    d_wout = _mm_tn(merged, dh1, tm=1024, tn=1024, name="dw_out")
    dp_m, dp_g, dproj = _merge_d(dh1, wout, p_m, p_g, pbig, "d_merged")
    dy_m = _mm(dp_m, wbm, nt=True, out_dtype=BF, tn=1024, name="d_ym")
    dy_g = _mm(dp_g, wbg, nt=True, out_dtype=BF, tn=1024, name="d_yg")
    d_wbm = _mm_tn(y_m, dp_m, tm=1024, tn=1024, name="dw_branch_m")
    d_wbg = _mm_tn(y_g, dp_g, tm=1024, tn=1024, name="dw_branch_g")
    fq = D_FF // 4
    gu4 = jnp.transpose(d_wgu_t.reshape(2, 2, 2, fq, D), (0, 2, 1, 3, 4)).reshape(4, 2 * fq, D)
    sq4 = jnp.concatenate([d_wbm.reshape(4, 256, D), d_wbg.reshape(4, 256, D), d_wout.reshape(4, 256, D)], axis=1)
    sums_a = reducer.partial_sums([sq4, d_wdown.reshape(4, fq, D), gu4], BF, "a")
    dqk_m, dproj, dsm_m, d_gbias, d_mhg, recv_a = _mlstm_bwd(dy_m, qk, pbig, small, gbias, mhg_f, m_cs, m_ns, dproj,
                                                              "mlstm_bwd", ride=sums_a)
    dproj, d_convwb = _conv_bwd(dqk_m, pbig, convw_f, convb, dproj, "conv_bwd")
    dproj, dsmall, d_a2p, d_a2b, d_ghg = _gla_bwd(dy_g, pbig, small, a2p, g_a2_b, ghg_f, g_ss, dsm_m, dproj, "gla_bwd")
    dproj = _place_small(dsmall, dproj, "dproj_small")
    d_win = _w_in_from_proj_rows(_mm_tn(dproj, xn1, tm=1664, tn=1024, name="dw_in"))
    pad = jnp.zeros((W_IN_BLOCK - W_IN_SHARD, D), F32)
    win4 = jnp.stack([jnp.concatenate([d_win[q * W_IN_SHARD:(q + 1) * W_IN_SHARD], pad], axis=0) for q in range(4)])
    sums_b = reducer.partial_sums([win4], BF, "b")
    dxn, recv_b = _mm(dproj, w_all, nt=False, out_dtype=F32, tn=1024, tk=1664, name="d_xn", rider=_scatter_rider(sums_b))
    dh_first, dx, d_g1 = _rms_bwd(dxn, h0, rstd1, g1, dh1, "rms1_bwd", split_first=True)

    small_sharded = [dh_first[FIRST_VALID:TM], d_convwb[0:4], d_a2p[8:24], d_mhg.reshape(NH, DV), d_ghg.reshape(NH, DV)]
    replicated = [d_g1, d_convwb[4:5], d_gbias[0:1, 0:2 * NH].reshape(1, 2, NH), d_a2b, d_g2, d_final_g.reshape(D)]
    small4 = jnp.broadcast_to(_pack_small(small_sharded + replicated, SMALL_GRAD_ROWS)[None], (4, SMALL_GRAD_ROWS, D))
    sums_c = reducer.partial_sums([small4], F32, "c")
    recv_c = reducer.scatter(sums_c, "c")
    sq, down, gu, win, smalls = reducer.finish(sums_a + sums_b + sums_c, recv_a + recv_b + recv_c, in_chip_order=[4])
    smalls = _unpack_small(smalls, [g.shape for g in small_sharded + replicated])
    me = 2 * lax.axis_index("x") + lax.axis_index("y")
    smalls = ([lax.dynamic_slice_in_dim(g, me * shp[1], shp[1], axis=1) for g, shp in zip(smalls, SMALL_SHARD_SHAPES)]
              + smalls[len(SMALL_SHARD_SHAPES):])
    grads = ([win[0:W_IN_SHARD], sq[0:256], sq[256:512], sq[512:768], gu[0:fq], gu[fq:2 * fq], down]
             + [g.reshape(g.size // g.shape[-1], g.shape[-1]) for g in smalls])
    return loss_local, dx, grads


class _Reducer:
    def partial_sums(self, items, dtype, tag):
        c = lax.axis_index("c")
        got = _swap_halves(items, "reduce_siblings_" + tag)
        sums = []
        for i, (a, g) in enumerate(zip(items, got)):
            rh, n = g.shape[1], g.shape[2]
            own = lax.dynamic_slice_in_dim(a, c * rh, rh, axis=1)
            sums.append(_add2(own.reshape(-1, n), g.reshape(-1, n), dtype, f"reduce_add2_{tag}{i}").reshape(g.shape))
        return sums

    def scatter(self, sums, tag):
        return list(_scatter_chips(sums, "reduce_chips_" + tag))

    def finish(self, sums, from_chips, in_chip_order):
        c = lax.axis_index("c")
        me = 2 * lax.axis_index("x") + lax.axis_index("y")
        halves = []
        for i, (s, f) in enumerate(zip(sums, from_chips)):
            mine = lax.dynamic_index_in_dim(s, me, 0, keepdims=False)
            if i in in_chip_order:
                by_chip = _by_chip(mine, f)
                mine, f = by_chip[0], jnp.stack(by_chip[1:])
            halves.append(_add4(mine, f, f"reduce_add4_{i}"))
        got = _join_halves(halves, "reduce_join")
        return [jnp.where(c == 0, jnp.concatenate([h, g], axis=0), jnp.concatenate([g, h], axis=0)) for h, g in zip(halves, got)]
```

```python
import functools

import jax
import jax.numpy as jnp
from jax import lax
from jax.experimental import pallas as pl
from jax.experimental.pallas import tpu as pltpu

F32 = jnp.float32
BF = jnp.bfloat16
HI = lax.Precision.HIGHEST
MESH = pl.DeviceIdType.MESH

D = 1024
N_META = 16
CHUNK = 128
EPS = 1e-6
NH = 4
DV = 256
DQK = 128
G_RANK = 16
G_TAU = 16.0
D_FF = 2816
TM = 512
FIRST_VALID = TM - N_META
CPB = TM // CHUNK
G_CHUNK = 256
G_CPB = TM // G_CHUNK
NEG = -1e30
N_BIG = 8192
CB_GQK, CB_GV, CB_GR, CB_MQK, CB_GM, CB_GG, CB_MV, CB_MO = range(8)
N_SMALL = 128
N_ALL = N_BIG + N_SMALL
VMEM_LIMIT = 56 * 1024 * 1024

ADAM_LR, ADAM_B1, ADAM_B2, ADAM_EPS, ADAM_WD, ADAM_STEP = 0.001, 0.9, 0.999, 1e-08, 0.01, 10

NT_DIMS = (((1,), (1,)), ((), ()))
TN_DIMS = (((0,), (0,)), ((), ()))


def _nt(a, b, **kw):
    return lax.dot_general(a, b, NT_DIMS, preferred_element_type=F32, **kw)


def _tn(a, b, **kw):
    return lax.dot_general(a, b, TN_DIMS, preferred_element_type=F32, **kw)


def _nn(a, b, **kw):
    return jnp.dot(a, b, preferred_element_type=F32, **kw)


def _params(**kw):
    return pltpu.CompilerParams(vmem_limit_bytes=VMEM_LIMIT, **kw)


def _sigmoid(x):
    return 0.5 * jnp.tanh(0.5 * x) + 0.5


def _logsig(x):
    return jnp.minimum(x, 0.0) - jnp.log(1.0 + jnp.exp(-jnp.abs(x)))


def _mm_rows(rows):
    return 3 * TM if rows % (3 * TM) == 0 else TM


def _mm(a, b, *, nt, out_dtype, tn, tk=None, tm=None, name, rider=None):
    m, k = a.shape
    n = b.shape[0] if nt else b.shape[1]
    tk = k if tk is None else tk
    tm = _mm_rows(m) if tm is None else tm
    nk = k // tk
    nj, ni = n // tn, m // tm
    nr_in = len(rider["inputs"]) if rider else 0
    nr_out = len(rider["out_shapes"]) if rider else 0
    assert m % tm == 0 and n % tn == 0 and k % tk == 0
    dims = NT_DIMS if nt else (((1,), (0,)), ((), ()))

    def body(*refs):
        a_ref, b_ref = refs[:2]
        o_ref = refs[2 + nr_in]
        j, i, kk = pl.program_id(0), pl.program_id(1), pl.program_id(2)
        step = (j * ni + i) * nk + kk
        if rider:
            start, middle, finish = rider["make"](refs[2:2 + nr_in], refs[3 + nr_in:3 + nr_in + nr_out],
                                                  refs[3 + nr_in + nr_out:5 + nr_in + nr_out])
            pl.when(step == 0)(start)
            pl.when(step == (nj * ni * nk) // 2)(middle)

        part = lax.dot_general(a_ref[...].astype(BF), b_ref[...].astype(BF), dims, preferred_element_type=F32)
        if nk == 1:
            o_ref[...] = part.astype(o_ref.dtype)
        else:
            acc_ref = refs[-1]

            @pl.when(kk == 0)
            def _():
                acc_ref[...] = part

            @pl.when(jnp.logical_and(kk > 0, kk < nk - 1))
            def _():
                acc_ref[...] += part

            @pl.when(kk == nk - 1)
            def _():
                o_ref[...] = (acc_ref[...] + part).astype(o_ref.dtype)

        if rider:
            pl.when(step == nj * ni * nk - 1)(finish)

    outs = pl.pallas_call(
        body, grid=(nj, ni, nk),
        in_specs=[pl.BlockSpec((tm, tk), lambda j, i, kk: (i, kk)),
                  pl.BlockSpec((tn, tk), lambda j, i, kk: (j, kk)) if nt else pl.BlockSpec((tk, tn), lambda j, i, kk: (kk, j))]
                 + [ANY] * nr_in,
        out_specs=[pl.BlockSpec((tm, tn), lambda j, i, kk: (i, j))] + [ANY] * nr_out,
        out_shape=(jax.ShapeDtypeStruct((m, n), out_dtype),) + (tuple(rider["out_shapes"]) if rider else ()),
        scratch_shapes=(rider["sems"] if rider else []) + ([pltpu.VMEM((tm, tn), F32)] if nk > 1 else []),
        compiler_params=_params(), name=name)(a, b, *(rider["inputs"] if rider else []))
    return (outs[0], list(outs[1:])) if rider else outs[0]


def _mm_tn(a, b, *, tm, tn, tk=None, name):
    t, m = a.shape
    n = b.shape[1]
    tk = _mm_rows(t) if tk is None else tk
    assert t % tk == 0 and m % tm == 0 and n % tn == 0

    def body(a_ref, b_ref, o_ref):
        part = _tn(a_ref[...].astype(BF), b_ref[...].astype(BF))

        @pl.when(pl.program_id(2) == 0)
        def _():
            o_ref[...] = part

        @pl.when(pl.program_id(2) > 0)
        def _():
            o_ref[...] += part

    return pl.pallas_call(
        body, grid=(m // tm, n // tn, t // tk),
        in_specs=[pl.BlockSpec((tk, tm), lambda i, j, kk: (kk, i)), pl.BlockSpec((tk, tn), lambda i, j, kk: (kk, j))],
        out_specs=pl.BlockSpec((tm, tn), lambda i, j, kk: (i, j)),
        out_shape=jax.ShapeDtypeStruct((m, n), F32), compiler_params=_params(), name=name)(a, b)


ANY = pl.BlockSpec(memory_space=pl.ANY)


def _row_spec(width, col=0):
    return pl.BlockSpec((TM, width), lambda i: (i, col))


def _full_spec(shape):
    return pl.BlockSpec(shape, lambda i: (0,) * len(shape))


def _embed_norm(x0, first, g, rider, name):
    tp = x0.shape[0] + TM
    nb = tp // TM
    nri, nro = len(rider["inputs"]), len(rider["out_shapes"])

    def body(*refs):
        x_ref, f_ref, g_ref = refs[:3]
        h_ref, xn_ref, r_ref = refs[3 + nri:6 + nri]
        i = pl.program_id(0)
        start, middle, finish = rider["make"](refs[3:3 + nri], refs[6 + nri:6 + nri + nro], refs[6 + nri + nro:])
        pl.when(i == 0)(start)
        pl.when(i == nb // 2)(middle)
        x = jnp.where(i == 0, f_ref[...], x_ref[...])
        r = lax.rsqrt(jnp.mean(x * x, axis=1, keepdims=True) + EPS)
        h_ref[...] = x
        xn_ref[...] = (x * r * g_ref[...]).astype(BF)
        r_ref[...] = r
        pl.when(i == nb - 1)(finish)

    outs = pl.pallas_call(
        body, grid=(nb,),
        in_specs=[pl.BlockSpec((TM, D), lambda i: (jnp.maximum(i - 1, 0), 0)), _full_spec((TM, D)), _full_spec((1, D))] + [ANY] * nri,
        out_specs=[_row_spec(D), _row_spec(D), _row_spec(1)] + [ANY] * nro,
        out_shape=(jax.ShapeDtypeStruct((tp, D), F32), jax.ShapeDtypeStruct((tp, D), BF), jax.ShapeDtypeStruct((tp, 1), F32))
                  + tuple(rider["out_shapes"]),
        scratch_shapes=rider["sems"], compiler_params=_params(), name=name)(x0, first, g, *rider["inputs"])
    return outs[0], outs[1], outs[2], list(outs[3:])


def _rms_bwd(dxn, h, rstd, g, dres, name, split_first=False):
    tp = h.shape[0]

    def body(dxn_ref, h_ref, r_ref, g_ref, dres_ref, *outs):
        r = r_ref[...]
        xh = h_ref[...] * r
        dxn_v = dxn_ref[...].astype(F32)
        dxh = dxn_v * g_ref[...]
        dh = r * (dxh - xh * jnp.mean(dxh * xh, axis=1, keepdims=True)) + dres_ref[...]
        if split_first:
            first_ref, dh_ref, dg_ref = outs

            @pl.when(pl.program_id(0) == 0)
            def _():
                first_ref[...] = dh
        else:
            dh_ref, dg_ref = outs
        dh_ref[...] = dh
        part = jnp.sum(dxn_v * xh, axis=0, keepdims=True)

        @pl.when(pl.program_id(0) == 0)
        def _():
            dg_ref[...] = part

        @pl.when(pl.program_id(0) > 0)
        def _():
            dg_ref[...] += part

    if split_first:
        out_specs = [_full_spec((TM, D)), pl.BlockSpec((TM, D), lambda i: (jnp.maximum(i - 1, 0), 0)), _full_spec((1, D))]
        out_shape = (jax.ShapeDtypeStruct((TM, D), F32), jax.ShapeDtypeStruct((tp - TM, D), F32), jax.ShapeDtypeStruct((1, D), F32))
    else:
        out_specs = [_row_spec(D), _full_spec((1, D))]
        out_shape = (jax.ShapeDtypeStruct((tp, D), F32), jax.ShapeDtypeStruct((1, D), F32))
    return pl.pallas_call(
        body, grid=(tp // TM,),
        in_specs=[_row_spec(D), _row_spec(D), _row_spec(1), _full_spec((1, D)), _row_spec(D)],
        out_specs=out_specs, out_shape=out_shape, compiler_params=_params(), name=name)(dxn, h, rstd, g, dres)


def _shift_down(x, halo, k):
    rk = pltpu.roll(x, k, 0)
    io = lax.broadcasted_iota(jnp.int32, (8, x.shape[1]), 0)
    top = jnp.where(io < k, pltpu.roll(halo, k, 0), rk[0:8])
    return top if x.shape[0] == 8 else jnp.concatenate([top, rk[8:]], axis=0)


def _shift_up(x, nxt, k):
    n = x.shape[0]
    rk = pltpu.roll(x, n - k, 0)
    io = lax.broadcasted_iota(jnp.int32, (8, x.shape[1]), 0)
    bot = jnp.where(io >= 8 - k, pltpu.roll(nxt, 8 - k, 0), rk[n - 8:n])
    return jnp.concatenate([rk[:n - 8], bot], axis=0)


def _conv_pre(x, halo, w_ref, b_ref):
    c = x * w_ref[3:4, :] + b_ref[...]
    shifted = []
    for k in (1, 2, 3):
        s = _shift_down(x, halo, k)
        shifted.append(s)
        c = c + s * w_ref[3 - k:4 - k, :]
    return c, shifted


def _qk_scale():
    col = lax.broadcasted_iota(jnp.int32, (1, D), 1)
    return jnp.where(col < NH * DQK, DQK ** -0.5, 1.0).astype(F32)


def _halo_prev_spec():
    return pl.BlockSpec((8, D), lambda i: (jnp.maximum(i * (TM // 8) - 1, 0), CB_MQK))


def _conv_fwd(pbig, w, b, name):
    tp = pbig.shape[0]

    def body(x_ref, halo_ref, w_ref, b_ref, o_ref):
        x = x_ref[...].astype(F32)
        halo = jnp.where(pl.program_id(0) > 0, halo_ref[...].astype(F32), 0.0)
        c, _ = _conv_pre(x, halo, w_ref, b_ref)
        o_ref[...] = (c * _sigmoid(c) * _qk_scale()).astype(BF)

    return pl.pallas_call(
        body, grid=(tp // TM,),
        in_specs=[_row_spec(D, CB_MQK), _halo_prev_spec(), _full_spec((4, D)), _full_spec((1, D))],
        out_specs=_row_spec(D), out_shape=jax.ShapeDtypeStruct((tp, D), BF),
        compiler_params=_params(), name=name)(pbig, pbig, w, b)


def _conv_bwd(dqk, pbig, w, b, dproj, name):
    tp = pbig.shape[0]
    nb = tp // TM

    def d_conv_out(d, x, halo, w_ref, b_ref):
        c, shifted = _conv_pre(x, halo, w_ref, b_ref)
        sg = _sigmoid(c)
        return d * _qk_scale() * (sg * (1.0 + c * (1.0 - sg))), shifted

    def body(d_ref, dn_ref, x_ref, halo_ref, xn_ref, w_ref, b_ref, _, o_ref, dwb_ref):
        i = pl.program_id(0)
        x = x_ref[...].astype(F32)
        halo = jnp.where(i > 0, halo_ref[...].astype(F32), 0.0)
        dc, shifted = d_conv_out(d_ref[...], x, halo, w_ref, b_ref)
        dc_next, _ = d_conv_out(dn_ref[...], xn_ref[...].astype(F32), x[TM - 8:TM], w_ref, b_ref)
        nxt = jnp.where(i < nb - 1, dc_next, 0.0)
        acc = dc * w_ref[3:4, :]
        for k in (1, 2, 3):
            acc = acc + _shift_up(dc, nxt, k) * w_ref[3 - k:4 - k, :]
        o_ref[...] = acc.astype(BF)
        taps = [shifted[2], shifted[1], shifted[0], x]
        rows = [jnp.sum(dc * t, axis=0, keepdims=True) for t in taps] + [jnp.sum(dc, axis=0, keepdims=True)]
        io = lax.broadcasted_iota(jnp.int32, (8, D), 0)
        part = jnp.zeros((8, D), F32)
        for r, v in enumerate(rows):
            part = jnp.where(io == r, v, part)

        @pl.when(pl.program_id(0) == 0)
        def _():
            dwb_ref[...] = part

        @pl.when(pl.program_id(0) > 0)
        def _():
            dwb_ref[...] += part

    next8 = lambda col: pl.BlockSpec((8, D), lambda i: (jnp.minimum((i + 1) * (TM // 8), tp // 8 - 1), col))
    return pl.pallas_call(
        body, grid=(nb,),
        in_specs=[_row_spec(D), next8(0), _row_spec(D, CB_MQK), _halo_prev_spec(), next8(CB_MQK),
                  _full_spec((4, D)), _full_spec((1, D)), ANY],
        out_specs=[_row_spec(D, CB_MQK), _full_spec((8, D))],
        out_shape=(jax.ShapeDtypeStruct(dproj.shape, BF), jax.ShapeDtypeStruct((8, D), F32)),
        input_output_aliases={7: 0}, compiler_params=_params(), name=name)(dqk, dqk, pbig, pbig, pbig, w, b, dproj)


def _mm_fused(inputs, products, *, nt, m, n, tm, tn, outs, epilogue, name, nk=1, sub=None):
    dims = NT_DIMS if nt else (((1,), (0,)), ((), ()))
    nin = len(inputs)
    assert nk == 1 or (len(products) == 1 and sub is None)

    def body(*refs):
        in_refs, out_refs = refs[:nin], refs[nin:nin + len(outs)]
        i = pl.program_id(1)
        if sub is not None:
            lhs = {ia: in_refs[ia][...].astype(BF) for ia, _ in products}

            def dots(cols):
                return [lax.dot_general(lhs[ia], (in_refs[ib][cols, :] if nt else in_refs[ib][:, cols]).astype(BF),
                                        dims, preferred_element_type=F32) for ia, ib in products]

            slices = [slice(s, min(s + sub, tn)) for s in range(0, tn, sub)]
            prods = dots(slices[0])
            for idx, cols in enumerate(slices):
                nxt = dots(slices[idx + 1]) if idx + 1 < len(slices) else None
                epilogue(prods, in_refs, out_refs, i, cols)
                prods = nxt
            return
        prods = [lax.dot_general(in_refs[ia][...].astype(BF), in_refs[ib][...].astype(BF), dims, preferred_element_type=F32)
                 for ia, ib in products]
        if nk == 1:
            epilogue(prods, in_refs, out_refs, i, slice(None))
            return
        acc_ref = refs[-1]
        kk = pl.program_id(2)

        @pl.when(kk == 0)
        def _():
            acc_ref[...] = prods[0]

        @pl.when(jnp.logical_and(kk > 0, kk < nk - 1))
        def _():
            acc_ref[...] += prods[0]

        @pl.when(kk == nk - 1)
        def _():
            epilogue([acc_ref[...] + prods[0]], in_refs, out_refs, i, slice(None))

    return pl.pallas_call(
        body, grid=(n // tn, m // tm, nk), in_specs=[s for _, s in inputs], out_specs=[s for _, s in outs],
        out_shape=tuple(sh for sh, _ in outs), scratch_shapes=[pltpu.VMEM((tm, tn), F32)] if nk > 1 else [],
        compiler_params=_params(), name=name)(*[a for a, _ in inputs])


SUB_COLS = 256


def _cols_at(cols, offset):
    return slice(cols.start + offset, cols.stop + offset)


def _blk(rows, width, col=None, row=None):
    return pl.BlockSpec((rows, width), lambda j, i, kk: ((i if row is None else row(i)), (0 if col is None else col(j, kk))))


FF_TN = D_FF // 2


def _ffn_weight_rows(wg_t, wu_t):
    return jnp.concatenate([wg_t[0:FF_TN], wu_t[0:FF_TN], wg_t[FF_TN:], wu_t[FF_TN:]], axis=0)


def _ffn_in(hn, wgu_t, name):
    tp = hn.shape[0]
    tm = _mm_rows(tp)

    def epilogue(prods, in_refs, out_refs, i, cols):
        g, u = prods
        out_refs[0][:, cols] = g.astype(BF)
        out_refs[0][:, _cols_at(cols, FF_TN)] = u.astype(BF)
        out_refs[1][:, cols] = (g * _sigmoid(g) * u).astype(BF)

    wspec = lambda off: pl.BlockSpec((FF_TN, D), lambda j, i, kk: (2 * j + off, 0))
    return _mm_fused(
        [(hn, _blk(tm, D)), (wgu_t, wspec(0)), (wgu_t, wspec(1))], [(0, 1), (0, 2)], nt=True, m=tp, n=D_FF, tm=tm, tn=FF_TN,
        outs=[(jax.ShapeDtypeStruct((tp, 2 * D_FF), BF), _blk(tm, 2 * FF_TN, lambda j, kk: j)),
              (jax.ShapeDtypeStruct((tp, D_FF), BF), _blk(tm, FF_TN, lambda j, kk: j))],
        epilogue=epilogue, name=name, sub=SUB_COLS)


def _ffn_down_loss(ff, wdown, h1, target, gf, name):
    tp = ff.shape[0]

    def epilogue(prods, in_refs, out_refs, i, cols):
        live = (i > 0).astype(F32)
        g = in_refs[4][...]
        x = prods[0] + in_refs[2][...]
        r = lax.rsqrt(jnp.mean(x * x, axis=1, keepdims=True) + EPS)
        xh = x * r
        e = xh * g - in_refs[3][...]
        loss_part = 0.5 * live * jnp.sum(jnp.mean(e * e, axis=1, keepdims=True), axis=0, keepdims=True)
        dout = e * (live / D)
        dg_part = jnp.sum(dout * xh, axis=0, keepdims=True)
        dxh = dout * g
        out_refs[0][...] = r * (dxh - xh * jnp.mean(dxh * xh, axis=1, keepdims=True))

        @pl.when(i == 0)
        def _():
            out_refs[1][...] = loss_part
            out_refs[2][...] = dg_part

        @pl.when(i > 0)
        def _():
            out_refs[1][...] += loss_part
            out_refs[2][...] += dg_part

    const = lambda shape: pl.BlockSpec(shape, lambda j, i, kk: (0,) * len(shape))
    return _mm_fused(
        [(ff, _blk(TM, D_FF)), (wdown, const((D_FF, D))), (h1, _blk(TM, D)),
         (target, _blk(TM, D, row=lambda i: jnp.maximum(i - 1, 0))), (gf, const((1, D)))],
        [(0, 1)], nt=False, m=tp, n=D, tm=TM, tn=D,
        outs=[(jax.ShapeDtypeStruct((tp, D), F32), _blk(TM, D)), (jax.ShapeDtypeStruct((1, 1), F32), const((1, 1))),
              (jax.ShapeDtypeStruct((1, D), F32), const((1, D)))],
        epilogue=epilogue, name=name)


def _ffn_d_hidden(dh2, wdown, gu, name):
    tp = dh2.shape[0]

    def epilogue(prods, in_refs, out_refs, i, cols):
        d = prods[0]
        g = in_refs[2][:, cols].astype(F32)
        u = in_refs[2][:, _cols_at(cols, FF_TN)].astype(F32)
        sg = _sigmoid(g)
        out_refs[0][:, cols] = (d * u * sg * (1.0 + g * (1.0 - sg))).astype(BF)
        out_refs[0][:, _cols_at(cols, FF_TN)] = (d * g * sg).astype(BF)

    return _mm_fused(
        [(dh2, _blk(TM, D)), (wdown, pl.BlockSpec((FF_TN, D), lambda j, i, kk: (j, 0))), (gu, _blk(TM, 2 * FF_TN, lambda j, kk: j))],
        [(0, 1)], nt=True, m=tp, n=D_FF, tm=TM, tn=FF_TN,
        outs=[(jax.ShapeDtypeStruct((tp, 2 * D_FF), BF), _blk(TM, 2 * FF_TN, lambda j, kk: j))],
        epilogue=epilogue, name=name, sub=SUB_COLS)[0]


def _ffn_d_in(dgu, wgu_t, h1, rstd, g2, dh2, name):
    tp = dgu.shape[0]
    nk = 2

    def epilogue(prods, in_refs, out_refs, i, cols):
        r = in_refs[3][...]
        xh = in_refs[2][...] * r
        dxn = prods[0]
        dxh = dxn * in_refs[4][...]
        out_refs[0][...] = r * (dxh - xh * jnp.mean(dxh * xh, axis=1, keepdims=True)) + in_refs[5][...]
        part = jnp.sum(dxn * xh, axis=0, keepdims=True)

        @pl.when(i == 0)
        def _():
            out_refs[1][...] = part

        @pl.when(i > 0)
        def _():
            out_refs[1][...] += part

    const = lambda shape: pl.BlockSpec(shape, lambda j, i, kk: (0,) * len(shape))
    return _mm_fused(
        [(dgu, pl.BlockSpec((TM, D_FF), lambda j, i, kk: (i, kk))), (wgu_t, pl.BlockSpec((D_FF, D), lambda j, i, kk: (kk, 0))),
         (h1, _blk(TM, D)), (rstd, _blk(TM, 1)), (g2, const((1, D))), (dh2, _blk(TM, D))],
        [(0, 1)], nt=False, m=tp, n=D, tm=TM, tn=D, nk=nk,
        outs=[(jax.ShapeDtypeStruct((tp, D), F32), _blk(TM, D)), (jax.ShapeDtypeStruct((1, D), F32), const((1, D)))],
        epilogue=epilogue, name=name)


def _branch_merge(y_m, y_g, wbm, wbg, pbig, name):
    tp = y_m.shape[0]

    def epilogue(prods, in_refs, out_refs, i, cols):
        pm, pg = prods[0].astype(BF), prods[1].astype(BF)
        out_refs[0][:, cols] = pm
        out_refs[1][:, cols] = pg
        out_refs[2][:, cols] = (_sigmoid(in_refs[4][:, cols].astype(F32)) * pm.astype(F32)
                                + _sigmoid(in_refs[5][:, cols].astype(F32)) * pg.astype(F32)).astype(BF)

    const = lambda shape: pl.BlockSpec(shape, lambda j, i, kk: (0,) * len(shape))
    shp = jax.ShapeDtypeStruct((tp, D), BF)
    return _mm_fused(
        [(y_m, _blk(TM, D)), (wbm, const((D, D))), (y_g, _blk(TM, D)), (wbg, const((D, D))),
         (pbig, _blk(TM, D, lambda j, kk: CB_GM)), (pbig, _blk(TM, D, lambda j, kk: CB_GG))],
        [(0, 1), (2, 3)], nt=False, m=tp, n=D, tm=TM, tn=D,
        outs=[(shp, _blk(TM, D)), (shp, _blk(TM, D)), (shp, _blk(TM, D))], epilogue=epilogue, name=name, sub=SUB_COLS)


def _merge_d(dh1, wout, pm, pg, pbig, name):
    tp = dh1.shape[0]

    def epilogue(prods, in_refs, out_refs, i, cols):
        d = prods[0]
        sm = _sigmoid(in_refs[4][:, cols].astype(F32))
        sg = _sigmoid(in_refs[5][:, cols].astype(F32))
        out_refs[0][:, cols] = (d * sm).astype(BF)
        out_refs[1][:, cols] = (d * sg).astype(BF)
        out_refs[2][:, cols] = (d * in_refs[2][:, cols].astype(F32) * sm * (1.0 - sm)).astype(BF)
        out_refs[2][:, _cols_at(cols, D)] = (d * in_refs[3][:, cols].astype(F32) * sg * (1.0 - sg)).astype(BF)

    const = lambda shape: pl.BlockSpec(shape, lambda j, i, kk: (0,) * len(shape))
    shp = jax.ShapeDtypeStruct((tp, D), BF)
    return _mm_fused(
        [(dh1, _blk(TM, D)), (wout, const((D, D))), (pm, _blk(TM, D)), (pg, _blk(TM, D)),
         (pbig, _blk(TM, D, lambda j, kk: CB_GM)), (pbig, _blk(TM, D, lambda j, kk: CB_GG))],
        [(0, 1)], nt=True, m=tp, n=D, tm=TM, tn=D,
        outs=[(shp, _blk(TM, D)), (shp, _blk(TM, D)),
              (jax.ShapeDtypeStruct((tp, N_ALL), BF), _blk(TM, 2 * D, lambda j, kk: CB_GM // 2))],
        epilogue=epilogue, name=name, sub=SUB_COLS)


def _out_proj_norm(merged, wout, h0, g2, name):
    tp = merged.shape[0]
    tm = _mm_rows(tp)

    def epilogue(prods, in_refs, out_refs, i, cols):
        x = prods[0] + in_refs[2][...]
        r = lax.rsqrt(jnp.mean(x * x, axis=1, keepdims=True) + EPS)
        out_refs[0][...] = x
        out_refs[1][...] = (x * r * in_refs[3][...]).astype(BF)
        out_refs[2][...] = r

    const = lambda shape: pl.BlockSpec(shape, lambda j, i, kk: (0,) * len(shape))
    return _mm_fused(
        [(merged, _blk(tm, D)), (wout, const((D, D))), (h0, _blk(tm, D)), (g2, const((1, D)))],
        [(0, 1)], nt=False, m=tp, n=D, tm=tm, tn=D,
        outs=[(jax.ShapeDtypeStruct((tp, D), F32), _blk(tm, D)), (jax.ShapeDtypeStruct((tp, D), BF), _blk(tm, D)),
              (jax.ShapeDtypeStruct((tp, 1), F32), _blk(tm, 1))],
        epilogue=epilogue, name=name)


def _adamw(w, g, m, v, name):
    rows, cols = w.shape
    by_cols = rows % 128 != 0 and cols % 128 == 0 and rows * cols > 128 * 1024
    tr = rows if (by_cols or rows % 128 != 0) else 128
    tc = 128 if by_cols else cols

    def body(w_ref, g_ref, m_ref, v_ref, d_ref, nm_ref, nv_ref):
        gv = g_ref[...]
        nm = ADAM_B1 * m_ref[...] + (1.0 - ADAM_B1) * gv
        nv = ADAM_B2 * v_ref[...] + (1.0 - ADAM_B2) * (gv * gv)
        m_hat = nm / (1.0 - ADAM_B1 ** ADAM_STEP)
        v_hat = nv / (1.0 - ADAM_B2 ** ADAM_STEP)
        d_ref[...] = -ADAM_LR * (m_hat / (jnp.sqrt(v_hat) + ADAM_EPS) + ADAM_WD * w_ref[...])
        nm_ref[...] = nm
        nv_ref[...] = nv

    spec = pl.BlockSpec((tr, tc), (lambda i: (0, i)) if by_cols else (lambda i: (i, 0)))
    shp = jax.ShapeDtypeStruct((rows, cols), F32)
    return pl.pallas_call(body, grid=(cols // tc if by_cols else rows // tr,), in_specs=[spec] * 4, out_specs=[spec] * 3,
                          out_shape=(shp,) * 3, compiler_params=_params(), name=name)(w, g, m, v)


def _place_small(dsmall, dproj, name):
    tp = dsmall.shape[0]

    def body(s_ref, _, o_ref):
        o_ref[...] = s_ref[...]

    return pl.pallas_call(
        body, grid=(tp // TM,), in_specs=[_row_spec(N_SMALL), ANY], out_specs=_row_spec(N_SMALL, N_BIG // N_SMALL),
        out_shape=jax.ShapeDtypeStruct(dproj.shape, dproj.dtype), input_output_aliases={1: 0},
        compiler_params=_params(), name=name)(dsmall, dproj)


def _row_tile(rows, cap=512):
    best = rows
    for cand in range(8, min(rows, cap) + 1, 8):
        if rows % cand == 0:
            best = cand
    return best


def _add2(a, b, out_dtype, name):
    rows, cols = a.shape
    tr = _row_tile(rows)

    def body(a_ref, b_ref, o_ref):
        o_ref[...] = (a_ref[...] + b_ref[...]).astype(o_ref.dtype)

    spec = pl.BlockSpec((tr, cols), lambda i: (i, 0))
    return pl.pallas_call(body, grid=(rows // tr,), in_specs=[spec] * 2, out_specs=spec,
                          out_shape=jax.ShapeDtypeStruct((rows, cols), out_dtype), compiler_params=_params(), name=name)(a, b)


def _add4(first, rest, name):
    rows, cols = first.shape
    tr = _row_tile(rows, 256)

    def body(f_ref, r_ref, o_ref):
        up = lambda v: v.astype(F32)
        o_ref[...] = ((up(f_ref[...]) + up(r_ref[0])) + up(r_ref[1])) + up(r_ref[2])

    return pl.pallas_call(body, grid=(rows // tr,),
                          in_specs=[pl.BlockSpec((tr, cols), lambda i: (i, 0)), pl.BlockSpec((3, tr, cols), lambda i: (0, i, 0))],
                          out_specs=pl.BlockSpec((tr, cols), lambda i: (i, 0)),
                          out_shape=jax.ShapeDtypeStruct((rows, cols), F32), compiler_params=_params(), name=name)(first, rest)


def _chunk_consts(length=CHUNK):
    r2 = lax.broadcasted_iota(jnp.int32, (length, length), 0)
    c2 = lax.broadcasted_iota(jnp.int32, (length, length), 1)
    tri = r2 >= c2
    return dict(tri=tri, tril_f=tri.astype(F32), triu_f=(r2 <= c2).astype(F32),
                lane=lax.broadcasted_iota(jnp.int32, (length, N_SMALL), 1),
                rowio=lax.broadcasted_iota(jnp.int32, (length, 1), 0),
                ones=jnp.ones((length, N_SMALL), F32))


def _valid_rows(block, c):
    row = block * TM + c * CHUNK + lax.broadcasted_iota(jnp.int32, (CHUNK, 1), 0)
    return row >= FIRST_VALID


def _col(x, lane, idx):
    return jnp.sum(jnp.where(lane == idx, x, 0.0), axis=1, keepdims=True)


def _last_row(x, rowio):
    return jnp.sum(jnp.where(rowio == rowio.shape[0] - 1, x, 0.0), axis=0, keepdims=True)


def _sum_all(x):
    return jnp.sum(jnp.sum(x, axis=1, keepdims=True), axis=0, keepdims=True)


def _headnorm_fwd(hm, gain, gate_act):
    rs = lax.rsqrt(jnp.mean(hm * hm, axis=1, keepdims=True) + EPS)
    return hm * rs * gain * gate_act


def _headnorm_bwd(dy, hm, gain, gate_act):
    rs = lax.rsqrt(jnp.mean(hm * hm, axis=1, keepdims=True) + EPS)
    xh = hm * rs
    dact = dy * xh * gain
    dgain = jnp.sum(dy * gate_act * xh, axis=0, keepdims=True)
    dxh = dy * gate_act * gain
    dhm = rs * (dxh - xh * jnp.mean(dxh * xh, axis=1, keepdims=True))
    return dhm, dact, dgain


def _mlstm_gates(sm, gbias, valid, k):
    pre = sm + gbias
    lf = jnp.where(valid, _logsig(pre), 0.0)
    b_all = _nn(k["tril_f"], lf, precision=HI)
    li_all = jnp.where(valid, pre, NEG)
    return pre, li_all, b_all


def _mlstm_open(h, qh, kh, c_st, li_all, b_all, k):
    lane = k["lane"]
    sel = jnp.where(lane == h, 1.0, 0.0) - jnp.where(lane == NH + h, 1.0, 0.0)
    x = jnp.where(lane < NH, li_all, jnp.where(lane < 2 * NH, b_all, 0.0))
    cb = c_st.astype(BF)
    return dict(ubc=_nt(sel, x, precision=HI), sim=_nt(qh, kh), cb=cb, cq=_nt(qh, cb))


def _mlstm_weights(h, f, qh, vh, li_all, b_all, n_row, m11, k):
    lane, tri, rowio = k["lane"], k["tri"], k["rowio"]
    b_col = _col(b_all, lane, NH + h)
    li_col = _col(li_all, lane, h)
    dmat = jnp.where(tri, b_col + f["ubc"], NEG)
    m_row = jnp.maximum(b_col + m11, jnp.max(dmat, axis=1, keepdims=True))
    e = jnp.exp(dmat - m_row)
    w_mat = e * f["sim"]
    a = jnp.exp(b_col + m11 - m_row)
    qf = qh.astype(F32)
    nq = jnp.sum(qf * n_row, axis=1, keepdims=True)
    g = _last_row(b_col, rowio)
    wlog = g - b_col + li_col
    m_new = jnp.maximum(g + m11, jnp.max(wlog, axis=0, keepdims=True))
    a_s = jnp.exp(g + m11 - m_new)
    w = jnp.exp(wlog - m_new)
    return dict(f, e=e, w_mat=w_mat, a=a, qf=qf, nq=nq, m_row=m_row, m_new=m_new, a_s=a_s, w=w,
                wv=_nn(w_mat.astype(BF), vh))


def _mlstm_out(f):
    num = f["a"] * f["cq"] + f["wv"]
    den = f["a"] * f["nq"] + jnp.sum(f["w_mat"], axis=1, keepdims=True)
    floor = jnp.exp(-f["m_row"])
    r = jnp.maximum(jnp.abs(den), floor)
    return dict(f, den=den, floor=floor, r=r, hm=num / r)


def _mlstm_fwd(qk, pbig, small, gbias, headg, name):
    tp = qk.shape[0]
    nb = tp // TM

    def body(qk_ref, v_ref, mo_ref, sm_ref, gb_ref, hg_ref, y_ref, cs_ref, ns_ref, c_scr, n_scr):
        blk = pl.program_id(0)

        @pl.when(blk == 0)
        def _():
            c_scr[...] = jnp.zeros_like(c_scr)
            n_scr[...] = jnp.zeros_like(n_scr)

        k = _chunk_consts()
        io8 = lax.broadcasted_iota(jnp.int32, (8, DQK), 0)

        def chunk(c, carry):
            r0 = pl.multiple_of(c * CHUNK, CHUNK)
            rows = pl.ds(r0, CHUNK)
            valid = _valid_rows(blk, c)
            _, li_all, b_all = _mlstm_gates(sm_ref[rows, :], gb_ref[...], valid, k)
            heads = range(NH)
            qs = [qk_ref[rows, h * DQK:(h + 1) * DQK] for h in heads]
            ks = [qk_ref[rows, NH * DQK + h * DQK:NH * DQK + (h + 1) * DQK] for h in heads]
            vs = [v_ref[rows, h * DV:(h + 1) * DV] for h in heads]
            cst = [c_scr[h] for h in heads]
            nrow = [n_scr[h, 0:1, :] for h in heads]
            m11 = [jnp.max(n_scr[h, 1:2, :], axis=1, keepdims=True) for h in heads]
            f = [_mlstm_open(h, qs[h], ks[h], cst[h], li_all, b_all, k) for h in heads]
            f = [_mlstm_weights(h, f[h], qs[h], vs[h], li_all, b_all, nrow[h], m11[h], k) for h in heads]
            wk = [f[h]["w"] * ks[h].astype(F32) for h in heads]
            kv = [_tn(vs[h], wk[h].astype(BF)) for h in heads]
            for h in heads:
                hm = _mlstm_out(f[h])["hm"]
                gate = _sigmoid(mo_ref[rows, h * DV:(h + 1) * DV].astype(F32))
                y_ref[rows, h * DV:(h + 1) * DV] = _headnorm_fwd(hm, hg_ref[:, h * DV:(h + 1) * DV], gate).astype(BF)
                cs_ref[c, h] = f[h]["cb"]
                ns_ref[c, h] = jnp.where(io8 == 0, nrow[h], jnp.where(io8 == 1, m11[h], 0.0))
                c_scr[h] = f[h]["a_s"] * cst[h] + kv[h]
                n_scr[h, 0:1, :] = f[h]["a_s"] * nrow[h] + jnp.sum(wk[h], axis=0, keepdims=True)
                n_scr[h, 1:2, :] = jnp.broadcast_to(f[h]["m_new"], (1, DQK))
            return carry

        lax.fori_loop(0, CPB, chunk, 0, unroll=2)

    return pl.pallas_call(
        body, grid=(nb,),
        in_specs=[_row_spec(D), _row_spec(D, CB_MV), _row_spec(D, CB_MO), _row_spec(N_SMALL), _full_spec((1, N_SMALL)), _full_spec((1, D))],
        out_specs=[_row_spec(D), pl.BlockSpec((CPB, NH, DV, DQK), lambda i: (i, 0, 0, 0)),
                   pl.BlockSpec((CPB, NH, 8, DQK), lambda i: (i, 0, 0, 0))],
        out_shape=(jax.ShapeDtypeStruct((tp, D), BF), jax.ShapeDtypeStruct((tp // CHUNK, NH, DV, DQK), BF),
                   jax.ShapeDtypeStruct((tp // CHUNK, NH, 8, DQK), F32)),
        scratch_shapes=[pltpu.VMEM((NH, DV, DQK), F32), pltpu.VMEM((NH, 8, DQK), F32)],
        compiler_params=_params(), name=name)(qk, pbig, pbig, small, gbias, headg)


def _mlstm_bwd(dy, qk, pbig, small, gbias, headg, cs, ns, dproj, name, ride=()):
    tp = qk.shape[0]
    nb = tp // TM
    nr = len(ride)

    def body(*refs):
        dy_ref, qk_ref, v_ref, mo_ref, sm_ref, gb_ref, hg_ref, cs_ref, ns_ref = refs[:9]
        ride_in = refs[10:10 + nr]
        dqk_ref, dproj_ref, dsm_ref, dgb_ref, dhg_ref = refs[10 + nr:15 + nr]
        ride_out = refs[15 + nr:15 + 2 * nr]
        dc_scr, dn_scr = refs[15 + 2 * nr:17 + 2 * nr]
        step = pl.program_id(0)
        blk = nb - 1 - step
        sent = _scatter_copies(ride_in, ride_out, *refs[17 + 2 * nr:]) if nr else []

        @pl.when(step == 0)
        def _():
            dc_scr[...] = jnp.zeros_like(dc_scr)
            dn_scr[...] = jnp.zeros_like(dn_scr)
            dgb_ref[...] = jnp.zeros_like(dgb_ref)
            dhg_ref[...] = jnp.zeros_like(dhg_ref)
            for cp in sent:
                cp.start()

        k = _chunk_consts()
        lane, rowio = k["lane"], k["rowio"]

        def chunk(cc, carry):
            c = CPB - 1 - cc
            r0 = pl.multiple_of(c * CHUNK, CHUNK)
            rows = pl.ds(r0, CHUNK)
            valid = _valid_rows(blk, c)
            pre, li_all, b_all = _mlstm_gates(sm_ref[rows, :], gb_ref[...], valid, k)
            dli_all = jnp.zeros((CHUNK, N_SMALL), F32)
            db_all = jnp.zeros((CHUNK, N_SMALL), F32)
            heads = range(NH)
            qs = [qk_ref[rows, h * DQK:(h + 1) * DQK] for h in heads]
            ks = [qk_ref[rows, NH * DQK + h * DQK:NH * DQK + (h + 1) * DQK] for h in heads]
            vs = [v_ref[rows, h * DV:(h + 1) * DV] for h in heads]
            cst = [cs_ref[c, h].astype(F32) for h in heads]
            nrow = [ns_ref[c, h, 0:1, :] for h in heads]
            m11 = [jnp.max(ns_ref[c, h, 1:2, :], axis=1, keepdims=True) for h in heads]
            f = [_mlstm_open(h, qs[h], ks[h], cst[h], li_all, b_all, k) for h in heads]
            f = [_mlstm_weights(h, f[h], qs[h], vs[h], li_all, b_all, nrow[h], m11[h], k) for h in heads]
            f = [_mlstm_out(f[h]) for h in heads]
            t = []
            for h in heads:
                gain = hg_ref[:, h * DV:(h + 1) * DV]
                gate = _sigmoid(mo_ref[rows, h * DV:(h + 1) * DV].astype(F32))
                dhm, dgate, dgain = _headnorm_bwd(dy_ref[rows, h * DV:(h + 1) * DV].astype(F32), f[h]["hm"], gain, gate)
                dproj_ref[rows, D + h * DV:D + (h + 1) * DV] = (dgate * gate * (1.0 - gate)).astype(BF)
                dhg_ref[:, h * DV:(h + 1) * DV] += dgain
                r, den = f[h]["r"], f[h]["den"]
                dnum = dhm / r
                dr = -jnp.sum(dhm * f[h]["hm"], axis=1, keepdims=True) / r
                dden = jnp.where(jnp.abs(den) > f[h]["floor"], dr * jnp.sign(den), 0.0)
                dnb = dnum.astype(BF)
                dc_new = dc_scr[h]
                dcb = dc_new.astype(BF)
                t.append(dict(dnum=dnum, dden=dden, dnb=dnb, dc_new=dc_new, dn_new=dn_scr[h],
                              dwm=_nt(dnb, vs[h]), vdc=_nn(vs[h], dcb), kdc=_nt(ks[h], dcb)))
            for h in heads:
                dw_mat = t[h]["dwm"] + t[h]["dden"]
                dsim = (f[h]["e"] * dw_mat).astype(BF)
                gm = f[h]["w_mat"] * dw_mat
                t[h].update(gm=gm, dv0=_tn(f[h]["w_mat"].astype(BF), t[h]["dnb"]), dq0=_nn(dsim, ks[h]),
                            dq1=_nn(t[h]["dnb"], f[h]["cb"]), dk0=_tn(dsim, qs[h]),
                            dcq=_tn((f[h]["a"] * t[h]["dnum"]).astype(BF), qs[h]), cs2=_tn(gm, k["ones"], precision=HI))
            for h in heads:
                a, w, a_s = f[h]["a"], f[h]["w"], f[h]["a_s"]
                dnum, dden, dc_new, dn_new, vdc, gm = (t[h][n] for n in ("dnum", "dden", "dc_new", "dn_new", "vdc", "gm"))
                kf = ks[h].astype(F32)
                dproj_ref[rows, h * DV:(h + 1) * DV] = (t[h]["dv0"] + w * t[h]["kdc"]).astype(BF)
                adden = a * dden
                dqk_ref[rows, h * DQK:(h + 1) * DQK] = t[h]["dq0"] + a * t[h]["dq1"] + adden * nrow[h]
                dqk_ref[rows, NH * DQK + h * DQK:NH * DQK + (h + 1) * DQK] = t[h]["dk0"] + w * vdc + w * dn_new
                da = jnp.sum(dnum * f[h]["cq"], axis=1, keepdims=True) + dden * f[h]["nq"]
                dw = jnp.sum(vdc * kf, axis=1, keepdims=True) + jnp.sum(kf * dn_new, axis=1, keepdims=True)
                da_s = _sum_all(dc_new * cst[h]) + jnp.sum(dn_new * nrow[h], axis=1, keepdims=True)
                wdw = w * dw
                rs = jnp.sum(gm, axis=1, keepdims=True)
                cs_col = _col(t[h]["cs2"], lane, 0)
                dg = a_s * da_s + jnp.sum(wdw, axis=0, keepdims=True)
                db = a * da + rs - cs_col - wdw + jnp.where(rowio == CHUNK - 1, dg, 0.0)
                dli_all = dli_all + jnp.where(lane == h, cs_col + wdw, 0.0)
                db_all = db_all + jnp.where(lane == NH + h, db, 0.0)
                dc_scr[h] = a_s * dc_new + t[h]["dcq"]
                dn_scr[h] = a_s * dn_new + jnp.sum(adden * f[h]["qf"], axis=0, keepdims=True)
            dlf_all = _nn(k["triu_f"], db_all, precision=HI)
            dsm = jnp.where(valid, dli_all + dlf_all * _sigmoid(-pre), 0.0)
            dsm = jnp.where(lane < 2 * NH, dsm, 0.0)
            dsm_ref[rows, :] = dsm
            dgb_ref[0:1, :] += jnp.sum(dsm, axis=0, keepdims=True)
            return carry

        lax.fori_loop(0, CPB, chunk, 0, unroll=2)

        if nr:
            @pl.when(step == nb - 1)
            def _():
                for cp in sent:
                    cp.wait_recv()
                for cp in sent:
                    cp.wait_send()

    rev = lambda col: (lambda i: (nb - 1 - i, col))
    rspec = lambda width, col=0: pl.BlockSpec((TM, width), rev(col))
    ride_shapes, ride_sems = _scatter_shapes(ride) if nr else ((), [])
    outs = pl.pallas_call(
        body, grid=(nb,),
        in_specs=[rspec(D), rspec(D), rspec(D, CB_MV), rspec(D, CB_MO), rspec(N_SMALL), _full_spec((1, N_SMALL)), _full_spec((1, D)),
                  pl.BlockSpec((CPB, NH, DV, DQK), lambda i: (nb - 1 - i, 0, 0, 0)),
                  pl.BlockSpec((CPB, NH, 8, DQK), lambda i: (nb - 1 - i, 0, 0, 0)), ANY] + [ANY] * nr,
        out_specs=[rspec(D), rspec(2 * D, CB_MV // 2), rspec(N_SMALL), _full_spec((8, N_SMALL)), _full_spec((1, D))] + [ANY] * nr,
        out_shape=(jax.ShapeDtypeStruct((tp, D), F32), jax.ShapeDtypeStruct(dproj.shape, BF),
                   jax.ShapeDtypeStruct((tp, N_SMALL), F32), jax.ShapeDtypeStruct((8, N_SMALL), F32),
                   jax.ShapeDtypeStruct((1, D), F32)) + tuple(ride_shapes),
        scratch_shapes=[pltpu.VMEM((NH, DV, DQK), F32), pltpu.VMEM((NH, 1, DQK), F32)] + ride_sems,
        input_output_aliases={9: 1}, compiler_params=_params(), name=name)(dy, qk, pbig, pbig, small, gbias, headg, cs, ns, dproj, *ride)
    return tuple(outs[:5]) + (list(outs[5:]),)


def _gla_loga(sm_ref, a2_ref, a2b_ref, blk):
    za = _nn(sm_ref[...].astype(BF), a2_ref[...]) + a2b_ref[...]
    row = blk * TM + lax.broadcasted_iota(jnp.int32, (TM, 1), 0)
    return za, jnp.where(row >= FIRST_VALID, _logsig(za) / G_TAU, 0.0)


def _gla_head(h, q_ref, k_ref, rows, bc, btot, k):
    sl = slice(h * DQK, (h + 1) * DQK)
    bch = bc[:, sl]
    bth = btot[:, sl]
    gq = q_ref[rows, h * DQK:(h + 1) * DQK].astype(F32)
    gk = k_ref[rows, NH * DQK + h * DQK:NH * DQK + (h + 1) * DQK].astype(F32)
    e_pos = jnp.exp(bch) * (DQK ** -0.5)
    e_neg = jnp.exp(-bch)
    e_end = jnp.exp(bth - bch)
    qd = gq * e_pos
    ki = gk * e_neg
    ke = gk * e_end
    att = jnp.where(k["tri"], _nt(qd.astype(BF), ki.astype(BF)), 0.0)
    return dict(e_pos=e_pos, e_neg=e_neg, e_end=e_end, qd=qd, ki=ki, ke=ke, att=att, decay=jnp.exp(bth))


def _gla_fwd(pbig, small, a2p, a2b, headg, name):
    tp = pbig.shape[0]
    nb = tp // TM

    def body(qk_ref, v_ref, gr_ref, sm_ref, a2_ref, a2b_ref, hg_ref, y_ref, ss_ref, s_scr, lg_scr):
        blk = pl.program_id(0)

        @pl.when(blk == 0)
        def _():
            s_scr[...] = jnp.zeros_like(s_scr)

        k = _chunk_consts(G_CHUNK)
        _, loga = _gla_loga(sm_ref, a2_ref, a2b_ref, blk)
        lg_scr[...] = loga

        def chunk(c, carry):
            r0 = pl.multiple_of(c * G_CHUNK, G_CHUNK)
            rows = pl.ds(r0, G_CHUNK)
            bc = _nn(k["tril_f"], lg_scr[rows, :], precision=HI)
            btot = _last_row(bc, k["rowio"])
            heads = range(NH)
            f = [_gla_head(h, qk_ref, qk_ref, rows, bc, btot, k) for h in heads]
            vs = [v_ref[rows, h * DV:(h + 1) * DV] for h in heads]
            sst = [s_scr[h] for h in heads]
            sbs = [s.astype(BF) for s in sst]
            inter = [_nt(f[h]["qd"].astype(BF), sbs[h]) for h in heads]
            intra = [_nn(f[h]["att"].astype(BF), vs[h]) for h in heads]
            kv = [_tn(vs[h], f[h]["ke"].astype(BF)) for h in heads]
            for h in heads:
                gr = gr_ref[rows, h * DV:(h + 1) * DV].astype(F32)
                y_ref[rows, h * DV:(h + 1) * DV] = _headnorm_fwd(intra[h] + inter[h], hg_ref[:, h * DV:(h + 1) * DV],
                                                                   gr * _sigmoid(gr)).astype(BF)
                ss_ref[c, h] = sbs[h]
                s_scr[h] = sst[h] * f[h]["decay"] + kv[h]
            return carry

        lax.fori_loop(0, G_CPB, chunk, 0, unroll=2)

    return pl.pallas_call(
        body, grid=(nb,),
        in_specs=[_row_spec(D, CB_GQK), _row_spec(D, CB_GV), _row_spec(D, CB_GR), _row_spec(N_SMALL),
                  _full_spec((N_SMALL, NH * DQK)), _full_spec((1, NH * DQK)), _full_spec((1, D))],
        out_specs=[_row_spec(D), pl.BlockSpec((G_CPB, NH, DV, DQK), lambda i: (i, 0, 0, 0))],
        out_shape=(jax.ShapeDtypeStruct((tp, D), BF), jax.ShapeDtypeStruct((tp // G_CHUNK, NH, DV, DQK), BF)),
        scratch_shapes=[pltpu.VMEM((NH, DV, DQK), F32), pltpu.VMEM((TM, NH * DQK), F32)],
        compiler_params=_params(), name=name)(pbig, pbig, pbig, small, a2p, a2b, headg)


def _gla_bwd(dy, pbig, small, a2p, a2b, headg, ss, dsm_m, dproj, name):
    tp = pbig.shape[0]
    nb = tp // TM
    nqk = NH * DQK

    def body(dy_ref, qk_ref, v_ref, gr_ref, sm_ref, a2_ref, a2b_ref, hg_ref, ss_ref, dsmm_ref, _,
             dproj_ref, dsm_ref, da2_ref, da2b_ref, dhg_ref, ds_scr, lg_scr, dza_scr):
        step = pl.program_id(0)
        blk = nb - 1 - step

        @pl.when(step == 0)
        def _():
            ds_scr[...] = jnp.zeros_like(ds_scr)
            da2_ref[...] = jnp.zeros_like(da2_ref)
            da2b_ref[...] = jnp.zeros_like(da2b_ref)
            dhg_ref[...] = jnp.zeros_like(dhg_ref)

        k = _chunk_consts(G_CHUNK)
        rowio = k["rowio"]
        za, loga = _gla_loga(sm_ref, a2_ref, a2b_ref, blk)
        lg_scr[...] = loga

        def chunk(cc, carry):
            c = G_CPB - 1 - cc
            r0 = pl.multiple_of(c * G_CHUNK, G_CHUNK)
            rows = pl.ds(r0, G_CHUNK)
            bc = _nn(k["tril_f"], lg_scr[rows, :], precision=HI)
            btot = _last_row(bc, rowio)
            heads = range(NH)
            f = [_gla_head(h, qk_ref, qk_ref, rows, bc, btot, k) for h in heads]
            vs = [v_ref[rows, h * DV:(h + 1) * DV] for h in heads]
            sbs = [ss_ref[c, h] for h in heads]
            qdb = [f[h]["qd"].astype(BF) for h in heads]
            attb = [f[h]["att"].astype(BF) for h in heads]
            inter = [_nt(qdb[h], sbs[h]) for h in heads]
            intra = [_nn(attb[h], vs[h]) for h in heads]
            dsn = [ds_scr[h] for h in heads]
            dsb = [d.astype(BF) for d in dsn]
            dke = [_nn(vs[h], dsb[h]) for h in heads]
            dv1 = [_nt(f[h]["ke"].astype(BF), dsb[h]) for h in heads]
            t = []
            for h in heads:
                gr = gr_ref[rows, h * DV:(h + 1) * DV].astype(F32)
                sg = _sigmoid(gr)
                gain = hg_ref[:, h * DV:(h + 1) * DV]
                do, dact, dgain = _headnorm_bwd(dy_ref[rows, h * DV:(h + 1) * DV].astype(F32), intra[h] + inter[h], gain, gr * sg)
                dproj_ref[rows, 2 * D + h * DV:2 * D + (h + 1) * DV] = (dact * sg * (1.0 + gr * (1.0 - sg))).astype(BF)
                dhg_ref[:, h * DV:(h + 1) * DV] += dgain
                dob = do.astype(BF)
                t.append(dict(dob=dob, datt=_nt(dob, vs[h]), dv0=_tn(attb[h], dob), dq1=_nn(dob, sbs[h]), dsq=_tn(dob, qdb[h])))
            for h in heads:
                datt = jnp.where(k["tri"], t[h]["datt"], 0.0).astype(BF)
                t[h].update(dq0=_nn(datt, f[h]["ki"].astype(BF)), dki=_tn(datt, qdb[h]))
            dbc_parts = []
            for h in heads:
                dqd = t[h]["dq0"] + t[h]["dq1"]
                dki = t[h]["dki"]
                dproj_ref[rows, D + h * DV:D + (h + 1) * DV] = (t[h]["dv0"] + dv1[h]).astype(BF)
                dproj_ref[rows, h * DQK:(h + 1) * DQK] = (dqd * f[h]["e_pos"]).astype(BF)
                dproj_ref[rows, nqk + h * DQK:nqk + (h + 1) * DQK] = (dki * f[h]["e_neg"] + dke[h] * f[h]["e_end"]).astype(BF)
                dke_ke = dke[h] * f[h]["ke"]
                dbtot = (jnp.sum(dke_ke, axis=0, keepdims=True)
                         + jnp.sum(dsn[h] * sbs[h].astype(F32), axis=0, keepdims=True) * f[h]["decay"])
                dbc_parts.append(dqd * f[h]["qd"] - dki * f[h]["ki"] - dke_ke + jnp.where(rowio == G_CHUNK - 1, dbtot, 0.0))
                ds_scr[h] = dsn[h] * f[h]["decay"] + t[h]["dsq"]
            dbc = jnp.concatenate(dbc_parts, axis=1)
            dza_scr[rows, :] = _nn(k["triu_f"], dbc, precision=HI)
            return carry

        lax.fori_loop(0, G_CPB, chunk, 0, unroll=2)
        row = blk * TM + lax.broadcasted_iota(jnp.int32, (TM, 1), 0)
        dza = jnp.where(row >= FIRST_VALID, dza_scr[...] * (_sigmoid(-za) / G_TAU), 0.0)
        dzb = dza.astype(BF)
        dsm_ref[...] = (_nt(dzb, a2_ref[...]) + dsmm_ref[...]).astype(BF)
        da2_ref[...] += _tn(sm_ref[...].astype(BF), dzb)
        da2b_ref[...] += jnp.sum(dza, axis=0, keepdims=True)

    rspec = lambda width, col=0: pl.BlockSpec((TM, width), lambda i: (nb - 1 - i, col))
    return pl.pallas_call(
        body, grid=(nb,),
        in_specs=[rspec(D), rspec(D, CB_GQK), rspec(D, CB_GV), rspec(D, CB_GR), rspec(N_SMALL),
                  _full_spec((N_SMALL, nqk)), _full_spec((1, nqk)), _full_spec((1, D)),
                  pl.BlockSpec((G_CPB, NH, DV, DQK), lambda i: (nb - 1 - i, 0, 0, 0)), rspec(N_SMALL), ANY],
        out_specs=[rspec(3 * D, 0), rspec(N_SMALL), _full_spec((N_SMALL, nqk)), _full_spec((1, nqk)), _full_spec((1, D))],
        out_shape=(jax.ShapeDtypeStruct(dproj.shape, BF),
                   jax.ShapeDtypeStruct((tp, N_SMALL), BF), jax.ShapeDtypeStruct((N_SMALL, nqk), F32),
                   jax.ShapeDtypeStruct((1, nqk), F32), jax.ShapeDtypeStruct((1, D), F32)),
        scratch_shapes=[pltpu.VMEM((NH, DV, DQK), F32), pltpu.VMEM((TM, nqk), F32), pltpu.VMEM((TM, nqk), F32)],
        input_output_aliases={10: 0}, compiler_params=_params(), name=name)(dy, pbig, pbig, pbig, small, a2p, a2b, headg, ss, dsm_m, dproj)


PIECE_BYTES = 1 << 20
MAX_PIECES = 32


def _place():
    return lax.axis_index("x"), lax.axis_index("y"), lax.axis_index("c")


def _piece_rows(rows, row_bytes, align):
    want = min(MAX_PIECES, max(1, -(-rows * row_bytes // PIECE_BYTES)))
    best = rows
    for k in range(1, want + 1):
        if rows % k == 0 and (rows // k) % align == 0:
            best = rows // k
    return best


def _remote(src, dst, send_sems, recv_sems, k, to):
    return pltpu.make_async_remote_copy(src_ref=src, dst_ref=dst, send_sem=send_sems.at[k], recv_sem=recv_sems.at[k],
                                        device_id=to, device_id_type=MESH)


def _all_gather_chips(p, name):
    rd = _gather_rider(p)

    def body(*refs):
        start, middle, finish = rd["make"](refs[:1], refs[1:2], refs[2:])
        start()
        middle()
        finish()

    return pl.pallas_call(body, in_specs=[ANY], out_specs=[ANY], out_shape=rd["out_shapes"], scratch_shapes=rd["sems"],
                          name=name)(p)[0]


def _gather_rider(p):
    r, n = p.shape
    rh = r // 2
    align = 32 // p.dtype.itemsize
    assert r % (2 * align) == 0
    cr = _piece_rows(rh, n * p.dtype.itemsize, align)

    def make(in_refs, out_refs, sem_refs):
        p_ref, o_ref = in_refs[0], out_refs[0]
        send_sems, recv_sems = sem_refs
        x, y, c = _place()
        chips = [(1 - x, y), (x, 1 - y), (1 - x, 1 - y)]
        sib = (x, y, 1 - c)

        def half(hc, piece=None):
            if piece is None:
                return pl.ds(pl.multiple_of(hc * rh, align), rh)
            return pl.ds(pl.multiple_of(hc * rh + piece * cr, align), cr)

        first = [_remote(p_ref.at[half(c)], o_ref.at[j, half(c)], send_sems, recv_sems, j, (*chip, c))
                 for j, chip in enumerate(chips)]
        passed = [[_remote(o_ref.at[j, half(c, i)], o_ref.at[j, half(c, i)], send_sems, recv_sems, 3 + j, sib)
                   for i in range(rh // cr)] for j in range(3)]
        blocks = [_remote(o_ref.at[j, half(c)], o_ref.at[j, half(1 - c)], send_sems, recv_sems, 3 + j, sib) for j in range(3)]

        def start():
            for cp in first:
                cp.start()

        def middle():
            for j, cp in enumerate(first):
                cp.wait_recv()
                for piece in passed[j]:
                    piece.start()

        def finish():
            for block in blocks:
                block.wait_send()
                block.wait_recv()
            for cp in first:
                cp.wait_send()

        return start, middle, finish

    return dict(inputs=[p], out_shapes=(jax.ShapeDtypeStruct((3, r, n), p.dtype),),
                sems=[pltpu.SemaphoreType.DMA((6,)), pltpu.SemaphoreType.DMA((6,))], make=make)


def _scatter_rider(items):
    out_shapes, sems = _scatter_shapes(items)

    def make(in_refs, out_refs, sem_refs):
        sent = _scatter_copies(in_refs, out_refs, *sem_refs)

        def start():
            for cp in sent:
                cp.start()

        def finish():
            for cp in sent:
                cp.wait_recv()
            for cp in sent:
                cp.wait_send()

        return start, (lambda: None), finish

    return dict(inputs=list(items), out_shapes=out_shapes, sems=sems, make=make)


def _by_chip(mine, others):
    me = 2 * lax.axis_index("x") + lax.axis_index("y")
    by_mask = jnp.stack([mine, others[1], others[0], others[2]])
    return [lax.dynamic_index_in_dim(by_mask, q ^ me, 0, keepdims=False) for q in range(4)]


def _swap_halves(items, name):
    k = len(items)

    def body(*refs):
        a_refs, got_refs = refs[:k], refs[k:2 * k]
        send_sems, recv_sems = refs[2 * k:]
        x, y, c = _place()
        sib = (x, y, 1 - c)
        for i, a in enumerate(items):
            _, r, n = a.shape
            rh = r // 2
            cr = _piece_rows(rh, n * a.dtype.itemsize, 8)
            for q in range(4):
                for t in range(rh // cr):
                    other = pl.ds(pl.multiple_of((1 - c) * rh + t * cr, 8), cr)
                    _remote(a_refs[i].at[q, other], got_refs[i].at[q, pl.ds(t * cr, cr)], send_sems, recv_sems, i, sib).start()
        for i, a in enumerate(items):
            block = _remote(a_refs[i].at[:, pl.ds(0, a.shape[1] // 2)], got_refs[i], send_sems, recv_sems, i, sib)
            block.wait_send()
            block.wait_recv()

    return pl.pallas_call(
        body, in_specs=[ANY] * k, out_specs=[ANY] * k,
        out_shape=tuple(jax.ShapeDtypeStruct((4, a.shape[1] // 2, a.shape[2]), a.dtype) for a in items),
        scratch_shapes=[pltpu.SemaphoreType.DMA((k,)), pltpu.SemaphoreType.DMA((k,))], name=name)(*items)


def _scatter_copies(s_refs, o_refs, send_sems, recv_sems):
    x, y, c = _place()
    chips = [(1 - x, y), (x, 1 - y), (1 - x, 1 - y)]
    return [_remote(s_refs[i].at[2 * cx + cy], o_refs[i].at[j], send_sems, recv_sems, 3 * i + j, (cx, cy, c))
            for i in range(len(s_refs)) for j, (cx, cy) in enumerate(chips)]


def _scatter_shapes(items):
    k = len(items)
    return (tuple(jax.ShapeDtypeStruct((3,) + s.shape[1:], s.dtype) for s in items),
            [pltpu.SemaphoreType.DMA((3 * k,)), pltpu.SemaphoreType.DMA((3 * k,))])


def _scatter_chips(items, name):
    k = len(items)

    def body(*refs):
        sent = _scatter_copies(refs[:k], refs[k:2 * k], *refs[2 * k:])
        for cp in sent:
            cp.start()
        for cp in sent:
            cp.wait_recv()
        for cp in sent:
            cp.wait_send()

    out_shape, scratch = _scatter_shapes(items)
    return pl.pallas_call(body, in_specs=[ANY] * k, out_specs=[ANY] * k, out_shape=out_shape, scratch_shapes=scratch,
                          name=name)(*items)


def _join_halves(items, name):
    k = len(items)

    def body(*refs):
        f_refs, o_refs = refs[:k], refs[k:2 * k]
        send_sems, recv_sems = refs[2 * k:]
        x, y, c = _place()
        sib = (x, y, 1 - c)
        for i, f in enumerate(items):
            rh, n = f.shape
            cr = _piece_rows(rh, n * f.dtype.itemsize, 8)
            for t in range(rh // cr):
                rows = pl.ds(t * cr, cr)
                _remote(f_refs[i].at[rows], o_refs[i].at[rows], send_sems, recv_sems, i, sib).start()
        for i in range(k):
            block = _remote(f_refs[i], o_refs[i], send_sems, recv_sems, i, sib)
            block.wait_send()
            block.wait_recv()

    return pl.pallas_call(
        body, in_specs=[ANY] * k, out_specs=[ANY] * k, out_shape=tuple(jax.ShapeDtypeStruct(f.shape, f.dtype) for f in items),
        scratch_shapes=[pltpu.SemaphoreType.DMA((k,)), pltpu.SemaphoreType.DMA((k,))], name=name)(*items)


SMALL_ROWS = 16
SMALL_GRAD_ROWS = 48
SMALL_SHARD_SHAPES = [(N_META, 256), (4, 256), (G_RANK, 128), (NH, 64), (NH, 64)]
REPL_SHAPES = [(1, D), (1, D), (1, 2, NH), (1, NH * DQK), (1, D), (D,)]
W_IN_SHARD = 2054
W_IN_BLOCK = 2080


def _pack_small(parts, rows=SMALL_ROWS):
    flat = jnp.concatenate([p.reshape(-1) for p in parts])
    return jnp.pad(flat, (0, rows * D - flat.shape[0])).reshape(rows, D)


def _unpack_small(block, shapes):
    flat, out, off = block.reshape(-1), [], 0
    for shp in shapes:
        n = 1
        for s in shp:
            n *= s
        out.append(flat[off:off + n].reshape(shp))
        off += n
    return out


def _proj_rows_from_w_in(w_in_t):
    w_big = jnp.concatenate([w_in_t[3080:5128], w_in_t[5144:6168], w_in_t[0:1024], w_in_t[6168:8216],
                             w_in_t[1024:2048], w_in_t[2056:3080]], axis=0)
    w_small = jnp.concatenate([w_in_t[2048:2056], w_in_t[5128:5144], jnp.zeros((N_SMALL - 24, D), w_in_t.dtype)], axis=0)
    return w_big, w_small


def _w_in_from_proj_rows(d_wall_t):
    big, small = d_wall_t[0:N_BIG], d_wall_t[N_BIG:N_ALL]
    return jnp.concatenate([big[3072:4096], big[6144:7168], small[0:8], big[7168:8192], big[0:2048],
                            small[8:24], big[2048:3072], big[4096:6144]], axis=0)


def kernel(x, meta_tokens, norm1_g, w_in, conv_w, conv_b, m_gate_b, g_a2, g_a2_b, m_head_g, g_head_g, w_branch_m, w_branch_g, w_out, norm2_g, w_ff_gate, w_ff_up, w_ff_down, final_g, loss_target, m_meta_tokens, m_norm1_g, m_w_in, m_conv_w, m_conv_b, m_m_gate_b, m_g_a2, m_g_a2_b, m_m_head_g, m_g_head_g, m_w_branch_m, m_w_branch_g, m_w_out, m_norm2_g, m_w_ff_gate, m_w_ff_up, m_w_ff_down, m_final_g, v_meta_tokens, v_norm1_g, v_w_in, v_conv_w, v_conv_b, v_m_gate_b, v_g_a2, v_g_a2_b, v_m_head_g, v_g_head_g, v_w_branch_m, v_w_branch_g, v_w_out, v_norm2_g, v_w_ff_gate, v_w_ff_up, v_w_ff_down, v_final_g):
    w = _gather_weights(w_in, w_branch_m, w_branch_g, w_out, w_ff_gate, w_ff_up, w_ff_down, meta_tokens, conv_w, g_a2, m_head_g, g_head_g)
    loss_local, dx, grads = _local_step(x[0], loss_target[0], w, norm1_g, conv_b, m_gate_b, g_a2_b, norm2_g, final_g, _Reducer())

    weights = [w_in, w_branch_m, w_branch_g, w_out, w_ff_gate, w_ff_up, w_ff_down, meta_tokens, conv_w, g_a2, m_head_g, g_head_g,
               norm1_g, conv_b, m_gate_b, g_a2_b, norm2_g, final_g]
    moms = [m_w_in, m_w_branch_m, m_w_branch_g, m_w_out, m_w_ff_gate, m_w_ff_up, m_w_ff_down, m_meta_tokens, m_conv_w, m_g_a2,
            m_m_head_g, m_g_head_g, m_norm1_g, m_conv_b, m_m_gate_b, m_g_a2_b, m_norm2_g, m_final_g]
    vels = [v_w_in, v_w_branch_m, v_w_branch_g, v_w_out, v_w_ff_gate, v_w_ff_up, v_w_ff_down, v_meta_tokens, v_conv_w, v_g_a2,
            v_m_head_g, v_g_head_g, v_norm1_g, v_conv_b, v_m_gate_b, v_g_a2_b, v_norm2_g, v_final_g]
    res = {}
    for nm, wt, g, m, v in zip(PACK_ORDER, weights, grads, moms, vels):
        if nm in TRANSPOSED_GRADS:
            to2d = lambda a: jnp.swapaxes(a, -1, -2).reshape(a.shape[-1], a.shape[-2])
            back = lambda a: jnp.swapaxes(a, 0, 1).reshape(wt.shape)
        else:
            to2d = lambda a: a.reshape(wt.size // wt.shape[-1], wt.shape[-1])
            back = lambda a: a.reshape(wt.shape)
        d, nm_, nv_ = _adamw(to2d(wt), g, to2d(m), to2d(v), "adamw_" + nm)
        res[nm] = (back(g), back(d), back(nm_), back(nv_))

    order = ["meta_tokens", "norm1_g", "w_in", "conv_w", "conv_b", "m_gate_b", "g_a2", "g_a2_b", "m_head_g", "g_head_g",
             "w_branch_m", "w_branch_g", "w_out", "norm2_g", "w_ff_gate", "w_ff_up", "w_ff_down", "final_g"]
    loss = lax.psum(loss_local[0, 0], ("x", "y", "c"))
    grad_x = dx.reshape(x.shape)
    return (loss, grad_x, *[res[n][0] for n in order], *[res[n][1] for n in order],
            *[res[n][2] for n in order], *[res[n][3] for n in order])


TRANSPOSED_GRADS = ("w_in", "w_ff_gate", "w_ff_up")
PACK_ORDER = ["w_in", "w_branch_m", "w_branch_g", "w_out", "w_ff_gate", "w_ff_up", "w_ff_down", "meta_tokens", "conv_w", "g_a2",
              "m_head_g", "g_head_g", "norm1_g", "conv_b", "m_gate_b", "g_a2_b", "norm2_g", "final_g"]


def _gather_weights(w_in, w_branch_m, w_branch_g, w_out, w_ff_gate, w_ff_up, w_ff_down, meta_tokens, conv_w, g_a2, m_head_g, g_head_g):
    bf = lambda a: a.astype(BF)
    rows_local = jnp.concatenate([bf(w_branch_m[0]), bf(w_branch_g[0]), bf(w_out[0]), bf(w_ff_down[0]),
                                  bf(w_ff_gate[0].T), bf(w_ff_up[0].T)], axis=0)
    win_local = jnp.pad(bf(w_in[0].T), ((0, W_IN_BLOCK - W_IN_SHARD), (0, 0)))
    small_local = _pack_small([meta_tokens, conv_w[0], g_a2[0], m_head_g[0], g_head_g[0]])
    small_all = _by_chip(small_local, _all_gather_chips(small_local, "gather_small"))
    small_sh = [_unpack_small(small_all[q], SMALL_SHARD_SHAPES) for q in range(4)]
    cat = lambda i: jnp.concatenate([s[i] for s in small_sh], axis=-1)
    return dict(win_local=win_local, rows_local=rows_local, meta=cat(0), convw=cat(1), ga2=cat(2),
                mhg=cat(3).reshape(1, D), ghg=cat(4).reshape(1, D))


def _row_weights(rows_local, gathered):
    rows_all = jnp.stack(_by_chip(rows_local, gathered))
    cut = lambda lo, hi: rows_all[:, lo:hi].reshape(4 * (hi - lo), D)
    return cut(0, 256), cut(256, 512), cut(512, 768), cut(768, 1472), _ffn_weight_rows(cut(1472, 2176), cut(2176, 2880))


def _local_step(x0, target, w, norm1_g, conv_b, m_gate_b, g_a2_b, norm2_g, final_g, reducer):
    meta_f, convw_f, ga2_f, mhg_f, ghg_f = w["meta"], w["convw"], w["ga2"], w["mhg"], w["ghg"]
    gbias =jnp.concatenate([m_gate_b.reshape(1, 2 * NH), jnp.zeros((1, N_SMALL - 2 * NH), F32)], axis=1)
    a2p = jnp.concatenate([jnp.zeros((8, NH * DQK), F32), ga2_f, jnp.zeros((N_SMALL - 24, NH * DQK), F32)], axis=0).astype(BF)
    convb = conv_b.reshape(1, D)
    g1 = norm1_g.reshape(1, D)
    g2 = norm2_g.reshape(1, D)
    gf = final_g.reshape(1, D)
    first = jnp.concatenate([jnp.zeros((FIRST_VALID, D), F32), meta_f], axis=0)

    h0, xn1, rstd1, win_gathered = _embed_norm(x0, first, g1, _gather_rider(w["win_local"]), "rms1")
    win_all = _by_chip(w["win_local"], win_gathered[0])
    w_in_f = jnp.concatenate([win_all[q][0:W_IN_SHARD] for q in range(4)], axis=0)
    w_big, w_small = _proj_rows_from_w_in(w_in_f)
    w_all = jnp.concatenate([w_big, w_small], axis=0)
    pbig, rows_gathered = _mm(xn1, w_big, nt=True, out_dtype=BF, tn=2048, name="proj_big", rider=_gather_rider(w["rows_local"]))
    wbm, wbg, wout, wdown, wgu_t = _row_weights(w["rows_local"], rows_gathered[0])
    small = _mm(xn1, w_small, nt=True, out_dtype=F32, tn=N_SMALL, name="proj_small")
    qk = _conv_fwd(pbig, convw_f, convb, "conv_fwd")
    y_m, m_cs, m_ns = _mlstm_fwd(qk, pbig, small, gbias, mhg_f, "mlstm_fwd")
    y_g, g_ss = _gla_fwd(pbig, small, a2p, g_a2_b, ghg_f, "gla_fwd")
    p_m, p_g, merged = _branch_merge(y_m, y_g, wbm, wbg, pbig, "branch_merge")
    h1, hn, rstd2 = _out_proj_norm(merged, wout, h0, g2, "out_proj")
    gu, ff = _ffn_in(hn, wgu_t, "ff_in")
    dh2, loss_local, d_final_g = _ffn_down_loss(ff, wdown, h1, target, gf, "ff_down_loss")

    d_wdown = _mm_tn(ff, dh2, tm=1408, tn=1024, name="dw_ff_down")
    dgu = _ffn_d_hidden(dh2, wdown, gu, "d_ff")
    d_wgu_t = _mm_tn(dgu, hn, tm=1408, tn=1024, name="dw_ff_in")
    dh1, d_g2 = _ffn_d_in(dgu, wgu_t, h1, rstd2, g2, dh2, "d_hn")
    d_wout = _mm_tn(merged, dh1, tm=1024, tn=1024, name="dw_out")
    dp_m, dp_g, dproj = _merge_d(dh1, wout, p_m, p_g, pbig, "d_merged")
    dy_m = _mm(dp_m, wbm, nt=True, out_dtype=BF, tn=1024, name="d_ym")
    dy_g = _mm(dp_g, wbg, nt=True, out_dtype=BF, tn=1024, name="d_yg")
    d_wbm = _mm_tn(y_m, dp_m, tm=1024, tn=1024, name="dw_branch_m")
    d_wbg = _mm_tn(y_g, dp_g, tm=1024, tn=1024, name="dw_branch_g")
    fq = D_FF // 4
    gu4 = jnp.transpose(d_wgu_t.reshape(2, 2, 2, fq, D), (0, 2, 1, 3, 4)).reshape(4, 2 * fq, D)
    sq4 = jnp.concatenate([d_wbm.reshape(4, 256, D), d_wbg.reshape(4, 256, D), d_wout.reshape(4, 256, D)], axis=1)
    sums_a = reducer.partial_sums([sq4, d_wdown.reshape(4, fq, D), gu4], BF, "a")
    dqk_m, dproj, dsm_m, d_gbias, d_mhg, recv_a = _mlstm_bwd(dy_m, qk, pbig, small, gbias, mhg_f, m_cs, m_ns, dproj,
                                                              "mlstm_bwd", ride=sums_a)
    dproj, d_convwb = _conv_bwd(dqk_m, pbig, convw_f, convb, dproj, "conv_bwd")
    dproj, dsmall, d_a2p, d_a2b, d_ghg = _gla_bwd(dy_g, pbig, small, a2p, g_a2_b, ghg_f, g_ss, dsm_m, dproj, "gla_bwd")
    dproj = _place_small(dsmall, dproj, "dproj_small")
    d_win = _w_in_from_proj_rows(_mm_tn(dproj, xn1, tm=1664, tn=1024, name="dw_in"))
    pad = jnp.zeros((W_IN_BLOCK - W_IN_SHARD, D), F32)
    win4 = jnp.stack([jnp.concatenate([d_win[q * W_IN_SHARD:(q + 1) * W_IN_SHARD], pad], axis=0) for q in range(4)])
    sums_b = reducer.partial_sums([win4], BF, "b")
    dxn, recv_b = _mm(dproj, w_all, nt=False, out_dtype=F32, tn=1024, tk=1664, name="d_xn", rider=_scatter_rider(sums_b))
    dh_first, dx, d_g1 = _rms_bwd(dxn, h0, rstd1, g1, dh1, "rms1_bwd", split_first=True)

    small_sharded = [dh_first[FIRST_VALID:TM], d_convwb[0:4], d_a2p[8:24], d_mhg.reshape(NH, DV), d_ghg.reshape(NH, DV)]
    replicated = [d_g1, d_convwb[4:5], d_gbias[0:1, 0:2 * NH].reshape(1, 2, NH), d_a2b, d_g2, d_final_g.reshape(D)]
    small4 = jnp.broadcast_to(_pack_small(small_sharded + replicated, SMALL_GRAD_ROWS)[None], (4, SMALL_GRAD_ROWS, D))
    sums_c = reducer.partial_sums([small4], F32, "c")
    recv_c = reducer.scatter(sums_c, "c")
    sq, down, gu, win, smalls = reducer.finish(sums_a + sums_b + sums_c, recv_a + recv_b + recv_c, in_chip_order=[4])
    smalls = _unpack_small(smalls, [g.shape for g in small_sharded + replicated])
    me = 2 * lax.axis_index("x") + lax.axis_index("y")
    smalls = ([lax.dynamic_slice_in_dim(g, me * shp[1], shp[1], axis=1) for g, shp in zip(smalls, SMALL_SHARD_SHAPES)]
              + smalls[len(SMALL_SHARD_SHAPES):])
    grads = ([win[0:W_IN_SHARD], sq[0:256], sq[256:512], sq[512:768], gu[0:fq], gu[fq:2 * fq], down]
             + [g.reshape(g.size // g.shape[-1], g.shape[-1]) for g in smalls])
    return loss_local, dx, grads


class _Reducer:
    def partial_sums(self, items, dtype, tag):
        c = lax.axis_index("c")
        got = _swap_halves(items, "reduce_siblings_" + tag)
        sums = []
        for i, (a, g) in enumerate(zip(items, got)):
            rh, n = g.shape[1], g.shape[2]
            own = lax.dynamic_slice_in_dim(a, c * rh, rh, axis=1)
            sums.append(_add2(own.reshape(-1, n), g.reshape(-1, n), dtype, f"reduce_add2_{tag}{i}").reshape(g.shape))
        return sums

    def scatter(self, sums, tag):
        return list(_scatter_chips(sums, "reduce_chips_" + tag))

    def finish(self, sums, from_chips, in_chip_order):
        c = lax.axis_index("c")
        me = 2 * lax.axis_index("x") + lax.axis_index("y")
        halves = []
        for i, (s, f) in enumerate(zip(sums, from_chips)):
            mine = lax.dynamic_index_in_dim(s, me, 0, keepdims=False)
            if i in in_chip_order:
                by_chip = _by_chip(mine, f)
                mine, f = by_chip[0], jnp.stack(by_chip[1:])
            halves.append(_add4(mine, f, f"reduce_add4_{i}"))
        got = _join_halves(halves, "reduce_join")
        return [jnp.concatenate([jnp.where(c == 0, h, g), jnp.where(c == 0, g, h)], axis=0) for h, g in zip(halves, got)]
```

```python
import jax
import jax.numpy as jnp
from jax import lax
from jax.experimental import pallas as pl
from jax.experimental.pallas import tpu as pltpu

F32 = jnp.float32
BF = jnp.bfloat16
HI = lax.Precision.HIGHEST
MESH = pl.DeviceIdType.MESH

D = 1024
N_META = 16
CHUNK = 128
EPS = 1e-6
NH = 4
DV = 256
DQK = 128
G_RANK = 16
G_TAU = 16.0
D_FF = 2816
TM = 512
FIRST_VALID = TM - N_META
CPB = TM // CHUNK
G_CHUNK = 256
G_CPB = TM // G_CHUNK
NEG = -1e30
N_BIG = 8192
CB_GQK, CB_GV, CB_GR, CB_MQK, CB_GM, CB_GG, CB_MV, CB_MO = range(8)
N_SMALL = 128
N_ALL = N_BIG + N_SMALL
PROJ_TK = N_ALL // 5
VMEM_LIMIT = 56 * 1024 * 1024

ADAM_LR, ADAM_B1, ADAM_B2, ADAM_EPS, ADAM_WD, ADAM_STEP = 0.001, 0.9, 0.999, 1e-08, 0.01, 10

NT_DIMS = (((1,), (1,)), ((), ()))
TN_DIMS = (((0,), (0,)), ((), ()))


def _nt(a, b, **kw):
    return lax.dot_general(a, b, NT_DIMS, preferred_element_type=F32, **kw)


def _tn(a, b, **kw):
    return lax.dot_general(a, b, TN_DIMS, preferred_element_type=F32, **kw)


def _nn(a, b, **kw):
    return jnp.dot(a, b, preferred_element_type=F32, **kw)


def _params(**kw):
    return pltpu.CompilerParams(vmem_limit_bytes=VMEM_LIMIT, **kw)


def _sigmoid(x):
    return 0.5 * jnp.tanh(0.5 * x) + 0.5


def _logsig(x):
    return jnp.minimum(x, 0.0) - jnp.log(1.0 + jnp.exp(-jnp.abs(x)))


def _mm_rows(rows):
    return 3 * TM if rows % (3 * TM) == 0 else TM


def _mm(a, b, *, nt, out_dtype, tn, tk=None, tm=None, name, rider=None):
    m, k = a.shape
    n = b.shape[0] if nt else b.shape[1]
    tk = k if tk is None else tk
    tm = _mm_rows(m) if tm is None else tm
    nk = k // tk
    nj, ni = n // tn, m // tm
    nr_in = len(rider["inputs"]) if rider else 0
    nr_out = len(rider["out_shapes"]) if rider else 0
    assert m % tm == 0 and n % tn == 0 and k % tk == 0
    dims = NT_DIMS if nt else (((1,), (0,)), ((), ()))

    def body(*refs):
        a_ref, b_ref = refs[:2]
        o_ref = refs[2 + nr_in]
        j, i, kk = pl.program_id(0), pl.program_id(1), pl.program_id(2)
        step = (j * ni + i) * nk + kk
        if rider:
            start, middle, finish = rider["make"](refs[2:2 + nr_in], refs[3 + nr_in:3 + nr_in + nr_out],
                                                  refs[3 + nr_in + nr_out:5 + nr_in + nr_out])
            pl.when(step == 0)(start)
            pl.when(step == (nj * ni * nk) // 2)(middle)

        part = lax.dot_general(a_ref[...].astype(BF), b_ref[...].astype(BF), dims, preferred_element_type=F32)
        if nk == 1:
            o_ref[...] = part.astype(o_ref.dtype)
        else:
            acc_ref = refs[-1]

            @pl.when(kk == 0)
            def _():
                acc_ref[...] = part

            @pl.when(jnp.logical_and(kk > 0, kk < nk - 1))
            def _():
                acc_ref[...] += part

            @pl.when(kk == nk - 1)
            def _():
                o_ref[...] = (acc_ref[...] + part).astype(o_ref.dtype)

        if rider:
            pl.when(step == nj * ni * nk - 1)(finish)

    outs = pl.pallas_call(
        body, grid=(nj, ni, nk),
        in_specs=[pl.BlockSpec((tm, tk), lambda j, i, kk: (i, kk)),
                  pl.BlockSpec((tn, tk), lambda j, i, kk: (j, kk)) if nt else pl.BlockSpec((tk, tn), lambda j, i, kk: (kk, j))]
                 + [ANY] * nr_in,
        out_specs=[pl.BlockSpec((tm, tn), lambda j, i, kk: (i, j))] + [ANY] * nr_out,
        out_shape=(jax.ShapeDtypeStruct((m, n), out_dtype),) + (tuple(rider["out_shapes"]) if rider else ()),
        scratch_shapes=(rider["sems"] if rider else []) + ([pltpu.VMEM((tm, tn), F32)] if nk > 1 else []),
        compiler_params=_params(), name=name)(a, b, *(rider["inputs"] if rider else []))
    return (outs[0], list(outs[1:])) if rider else outs[0]


def _mm_tn(a, b, *, tm, tn, tk=None, name):
    t, m = a.shape
    n = b.shape[1]
    tk = _mm_rows(t) if tk is None else tk
    assert t % tk == 0 and m % tm == 0 and n % tn == 0

    def body(a_ref, b_ref, o_ref):
        part = _tn(a_ref[...].astype(BF), b_ref[...].astype(BF))

        @pl.when(pl.program_id(2) == 0)
        def _():
            o_ref[...] = part

        @pl.when(pl.program_id(2) > 0)
        def _():
            o_ref[...] += part

    return pl.pallas_call(
        body, grid=(m // tm, n // tn, t // tk),
        in_specs=[pl.BlockSpec((tk, tm), lambda i, j, kk: (kk, i)), pl.BlockSpec((tk, tn), lambda i, j, kk: (kk, j))],
        out_specs=pl.BlockSpec((tm, tn), lambda i, j, kk: (i, j)),
        out_shape=jax.ShapeDtypeStruct((m, n), F32), compiler_params=_params(), name=name)(a, b)


ANY = pl.BlockSpec(memory_space=pl.ANY)


def _row_spec(width, col=0):
    return pl.BlockSpec((TM, width), lambda i: (i, col))


def _full_spec(shape):
    return pl.BlockSpec(shape, lambda i: (0,) * len(shape))


def _embed_norm(x0, first, g, rider, name):
    tp = x0.shape[0] + TM
    nb = tp // TM
    nri, nro = len(rider["inputs"]), len(rider["out_shapes"])

    def body(*refs):
        x_ref, f_ref, g_ref = refs[:3]
        h_ref, xn_ref, r_ref = refs[3 + nri:6 + nri]
        i = pl.program_id(0)
        start, middle, finish = rider["make"](refs[3:3 + nri], refs[6 + nri:6 + nri + nro], refs[6 + nri + nro:])
        pl.when(i == 0)(start)
        pl.when(i == nb // 2)(middle)
        x = jnp.where(i == 0, f_ref[...], x_ref[...])
        r = lax.rsqrt(jnp.mean(x * x, axis=1, keepdims=True) + EPS)
        h_ref[...] = x
        xn_ref[...] = (x * r * g_ref[...]).astype(BF)
        r_ref[...] = r
        pl.when(i == nb - 1)(finish)

    outs = pl.pallas_call(
        body, grid=(nb,),
        in_specs=[pl.BlockSpec((TM, D), lambda i: (jnp.maximum(i - 1, 0), 0)), _full_spec((TM, D)), _full_spec((1, D))] + [ANY] * nri,
        out_specs=[_row_spec(D), _row_spec(D), _row_spec(1)] + [ANY] * nro,
        out_shape=(jax.ShapeDtypeStruct((tp, D), F32), jax.ShapeDtypeStruct((tp, D), BF), jax.ShapeDtypeStruct((tp, 1), F32))
                  + tuple(rider["out_shapes"]),
        scratch_shapes=rider["sems"], compiler_params=_params(), name=name)(x0, first, g, *rider["inputs"])
    return outs[0], outs[1], outs[2], list(outs[3:])


def _rms_bwd(dxn, h, rstd, g, dres, name, split_first=False):
    tp = h.shape[0]

    def body(dxn_ref, h_ref, r_ref, g_ref, dres_ref, *outs):
        r = r_ref[...]
        xh = h_ref[...] * r
        dxn_v = dxn_ref[...].astype(F32)
        dxh = dxn_v * g_ref[...]
        dh = r * (dxh - xh * jnp.mean(dxh * xh, axis=1, keepdims=True)) + dres_ref[...]
        if split_first:
            first_ref, dh_ref, dg_ref = outs

            @pl.when(pl.program_id(0) == 0)
            def _():
                first_ref[...] = dh
        else:
            dh_ref, dg_ref = outs
        dh_ref[...] = dh
        part = jnp.sum(dxn_v * xh, axis=0, keepdims=True)

        @pl.when(pl.program_id(0) == 0)
        def _():
            dg_ref[...] = part

        @pl.when(pl.program_id(0) > 0)
        def _():
            dg_ref[...] += part

    if split_first:
        out_specs = [_full_spec((TM, D)), pl.BlockSpec((TM, D), lambda i: (jnp.maximum(i - 1, 0), 0)), _full_spec((1, D))]
        out_shape = (jax.ShapeDtypeStruct((TM, D), F32), jax.ShapeDtypeStruct((tp - TM, D), F32), jax.ShapeDtypeStruct((1, D), F32))
    else:
        out_specs = [_row_spec(D), _full_spec((1, D))]
        out_shape = (jax.ShapeDtypeStruct((tp, D), F32), jax.ShapeDtypeStruct((1, D), F32))
    return pl.pallas_call(
        body, grid=(tp // TM,),
        in_specs=[_row_spec(D), _row_spec(D), _row_spec(1), _full_spec((1, D)), _row_spec(D)],
        out_specs=out_specs, out_shape=out_shape, compiler_params=_params(), name=name)(dxn, h, rstd, g, dres)


def _shift_down(x, halo, k):
    rk = pltpu.roll(x, k, 0)
    io = lax.broadcasted_iota(jnp.int32, (8, x.shape[1]), 0)
    top = jnp.where(io < k, pltpu.roll(halo, k, 0), rk[0:8])
    return top if x.shape[0] == 8 else jnp.concatenate([top, rk[8:]], axis=0)


def _shift_up(x, nxt, k):
    n = x.shape[0]
    rk = pltpu.roll(x, n - k, 0)
    io = lax.broadcasted_iota(jnp.int32, (8, x.shape[1]), 0)
    bot = jnp.where(io >= 8 - k, pltpu.roll(nxt, 8 - k, 0), rk[n - 8:n])
    return jnp.concatenate([rk[:n - 8], bot], axis=0)


def _conv_pre(x, halo, w_ref, b_ref):
    c = x * w_ref[3:4, :] + b_ref[...]
    shifted = []
    for k in (1, 2, 3):
        s = _shift_down(x, halo, k)
        shifted.append(s)
        c = c + s * w_ref[3 - k:4 - k, :]
    return c, shifted


def _qk_scale():
    col = lax.broadcasted_iota(jnp.int32, (1, D), 1)
    return jnp.where(col < NH * DQK, DQK ** -0.5, 1.0).astype(F32)


def _halo_prev_spec():
    return pl.BlockSpec((8, D), lambda i: (jnp.maximum(i * (TM // 8) - 1, 0), CB_MQK))


def _conv_fwd(pbig, w, b, name):
    tp = pbig.shape[0]

    def body(x_ref, halo_ref, w_ref, b_ref, o_ref):
        x = x_ref[...].astype(F32)
        halo = jnp.where(pl.program_id(0) > 0, halo_ref[...].astype(F32), 0.0)
        c, _ = _conv_pre(x, halo, w_ref, b_ref)
        o_ref[...] = (c * _sigmoid(c) * _qk_scale()).astype(BF)

    return pl.pallas_call(
        body, grid=(tp // TM,),
        in_specs=[_row_spec(D, CB_MQK), _halo_prev_spec(), _full_spec((4, D)), _full_spec((1, D))],
        out_specs=_row_spec(D), out_shape=jax.ShapeDtypeStruct((tp, D), BF),
        compiler_params=_params(), name=name)(pbig, pbig, w, b)


def _conv_bwd(dqk, pbig, w, b, dproj, name):
    tp = pbig.shape[0]
    nb = tp // TM

    def d_conv_out(d, x, halo, w_ref, b_ref):
        c, shifted = _conv_pre(x, halo, w_ref, b_ref)
        sg = _sigmoid(c)
        return d * _qk_scale() * (sg * (1.0 + c * (1.0 - sg))), shifted

    def body(d_ref, dn_ref, x_ref, halo_ref, xn_ref, w_ref, b_ref, _, o_ref, dwb_ref):
        i = pl.program_id(0)
        x = x_ref[...].astype(F32)
        halo = jnp.where(i > 0, halo_ref[...].astype(F32), 0.0)
        dc, shifted = d_conv_out(d_ref[...], x, halo, w_ref, b_ref)
        dc_next, _ = d_conv_out(dn_ref[...], xn_ref[...].astype(F32), x[TM - 8:TM], w_ref, b_ref)
        nxt = jnp.where(i < nb - 1, dc_next, 0.0)
        acc = dc * w_ref[3:4, :]
        for k in (1, 2, 3):
            acc = acc + _shift_up(dc, nxt, k) * w_ref[3 - k:4 - k, :]
        o_ref[...] = acc.astype(BF)
        taps = [shifted[2], shifted[1], shifted[0], x]
        rows = [jnp.sum(dc * t, axis=0, keepdims=True) for t in taps] + [jnp.sum(dc, axis=0, keepdims=True)]
        io = lax.broadcasted_iota(jnp.int32, (8, D), 0)
        part = jnp.zeros((8, D), F32)
        for r, v in enumerate(rows):
            part = jnp.where(io == r, v, part)

        @pl.when(pl.program_id(0) == 0)
        def _():
            dwb_ref[...] = part

        @pl.when(pl.program_id(0) > 0)
        def _():
            dwb_ref[...] += part

    next8 = lambda col: pl.BlockSpec((8, D), lambda i: (jnp.minimum((i + 1) * (TM // 8), tp // 8 - 1), col))
    return pl.pallas_call(
        body, grid=(nb,),
        in_specs=[_row_spec(D), next8(0), _row_spec(D, CB_MQK), _halo_prev_spec(), next8(CB_MQK),
                  _full_spec((4, D)), _full_spec((1, D)), ANY],
        out_specs=[_row_spec(D, CB_MQK), _full_spec((8, D))],
        out_shape=(jax.ShapeDtypeStruct(dproj.shape, BF), jax.ShapeDtypeStruct((8, D), F32)),
        input_output_aliases={7: 0}, compiler_params=_params(), name=name)(dqk, dqk, pbig, pbig, pbig, w, b, dproj)


def _mm_fused(inputs, products, *, nt, m, n, tm, tn, outs, epilogue, name, nk=1, sub=None):
    dims = NT_DIMS if nt else (((1,), (0,)), ((), ()))
    nin = len(inputs)
    assert nk == 1 or (len(products) == 1 and sub is None)

    def body(*refs):
        in_refs, out_refs = refs[:nin], refs[nin:nin + len(outs)]
        i = pl.program_id(1)
        if sub is not None:
            lhs = {ia: in_refs[ia][...].astype(BF) for ia, _ in products}

            def dots(cols):
                return [lax.dot_general(lhs[ia], (in_refs[ib][cols, :] if nt else in_refs[ib][:, cols]).astype(BF),
                                        dims, preferred_element_type=F32) for ia, ib in products]

            slices = [slice(s, min(s + sub, tn)) for s in range(0, tn, sub)]
            prods = dots(slices[0])
            for idx, cols in enumerate(slices):
                nxt = dots(slices[idx + 1]) if idx + 1 < len(slices) else None
                epilogue(prods, in_refs, out_refs, i, cols)
                prods = nxt
            return
        prods = [lax.dot_general(in_refs[ia][...].astype(BF), in_refs[ib][...].astype(BF), dims, preferred_element_type=F32)
                 for ia, ib in products]
        if nk == 1:
            epilogue(prods, in_refs, out_refs, i, slice(None))
            return
        acc_ref = refs[-1]
        kk = pl.program_id(2)

        @pl.when(kk == 0)
        def _():
            acc_ref[...] = prods[0]

        @pl.when(jnp.logical_and(kk > 0, kk < nk - 1))
        def _():
            acc_ref[...] += prods[0]

        @pl.when(kk == nk - 1)
        def _():
            epilogue([acc_ref[...] + prods[0]], in_refs, out_refs, i, slice(None))

    return pl.pallas_call(
        body, grid=(n // tn, m // tm, nk), in_specs=[s for _, s in inputs], out_specs=[s for _, s in outs],
        out_shape=tuple(sh for sh, _ in outs), scratch_shapes=[pltpu.VMEM((tm, tn), F32)] if nk > 1 else [],
        compiler_params=_params(), name=name)(*[a for a, _ in inputs])


SUB_COLS = 256


def _cols_at(cols, offset):
    return slice(cols.start + offset, cols.stop + offset)


def _blk(rows, width, col=None, row=None):
    return pl.BlockSpec((rows, width), lambda j, i, kk: ((i if row is None else row(i)), (0 if col is None else col(j, kk))))


FF_TN = D_FF // 2


def _ffn_weight_rows(wg_t, wu_t):
    return jnp.concatenate([wg_t[0:FF_TN], wu_t[0:FF_TN], wg_t[FF_TN:], wu_t[FF_TN:]], axis=0)


def _ffn_in(hn, wgu_t, name):
    tp = hn.shape[0]
    tm = _mm_rows(tp)

    def epilogue(prods, in_refs, out_refs, i, cols):
        g, u = prods
        out_refs[0][:, cols] = g.astype(BF)
        out_refs[0][:, _cols_at(cols, FF_TN)] = u.astype(BF)
        out_refs[1][:, cols] = (g * _sigmoid(g) * u).astype(BF)

    wspec = lambda off: pl.BlockSpec((FF_TN, D), lambda j, i, kk: (2 * j + off, 0))
    return _mm_fused(
        [(hn, _blk(tm, D)), (wgu_t, wspec(0)), (wgu_t, wspec(1))], [(0, 1), (0, 2)], nt=True, m=tp, n=D_FF, tm=tm, tn=FF_TN,
        outs=[(jax.ShapeDtypeStruct((tp, 2 * D_FF), BF), _blk(tm, 2 * FF_TN, lambda j, kk: j)),
              (jax.ShapeDtypeStruct((tp, D_FF), BF), _blk(tm, FF_TN, lambda j, kk: j))],
        epilogue=epilogue, name=name, sub=SUB_COLS)


def _ffn_down_loss(ff, wdown, h1, target, gf, name):
    tp = ff.shape[0]

    def epilogue(prods, in_refs, out_refs, i, cols):
        live = (i > 0).astype(F32)
        g = in_refs[4][...]
        x = prods[0] + in_refs[2][...]
        r = lax.rsqrt(jnp.mean(x * x, axis=1, keepdims=True) + EPS)
        xh = x * r
        e = xh * g - in_refs[3][...]
        loss_part = 0.5 * live * jnp.sum(jnp.mean(e * e, axis=1, keepdims=True), axis=0, keepdims=True)
        dout = e * (live / D)
        dg_part = jnp.sum(dout * xh, axis=0, keepdims=True)
        dxh = dout * g
        out_refs[0][...] = r * (dxh - xh * jnp.mean(dxh * xh, axis=1, keepdims=True))

        @pl.when(i == 0)
        def _():
            out_refs[1][...] = loss_part
            out_refs[2][...] = dg_part

        @pl.when(i > 0)
        def _():
            out_refs[1][...] += loss_part
            out_refs[2][...] += dg_part

    const = lambda shape: pl.BlockSpec(shape, lambda j, i, kk: (0,) * len(shape))
    return _mm_fused(
        [(ff, _blk(TM, D_FF)), (wdown, const((D_FF, D))), (h1, _blk(TM, D)),
         (target, _blk(TM, D, row=lambda i: jnp.maximum(i - 1, 0))), (gf, const((1, D)))],
        [(0, 1)], nt=False, m=tp, n=D, tm=TM, tn=D,
        outs=[(jax.ShapeDtypeStruct((tp, D), F32), _blk(TM, D)), (jax.ShapeDtypeStruct((1, 1), F32), const((1, 1))),
              (jax.ShapeDtypeStruct((1, D), F32), const((1, D)))],
        epilogue=epilogue, name=name)


def _ffn_d_hidden(dh2, wdown, gu, name):
    tp = dh2.shape[0]

    def epilogue(prods, in_refs, out_refs, i, cols):
        d = prods[0]
        g = in_refs[2][:, cols].astype(F32)
        u = in_refs[2][:, _cols_at(cols, FF_TN)].astype(F32)
        sg = _sigmoid(g)
        out_refs[0][:, cols] = (d * u * sg * (1.0 + g * (1.0 - sg))).astype(BF)
        out_refs[0][:, _cols_at(cols, FF_TN)] = (d * g * sg).astype(BF)

    return _mm_fused(
        [(dh2, _blk(TM, D)), (wdown, pl.BlockSpec((FF_TN, D), lambda j, i, kk: (j, 0))), (gu, _blk(TM, 2 * FF_TN, lambda j, kk: j))],
        [(0, 1)], nt=True, m=tp, n=D_FF, tm=TM, tn=FF_TN,
        outs=[(jax.ShapeDtypeStruct((tp, 2 * D_FF), BF), _blk(TM, 2 * FF_TN, lambda j, kk: j))],
        epilogue=epilogue, name=name, sub=SUB_COLS)[0]


def _ffn_d_in(dgu, wgu_t, h1, rstd, g2, dh2, name):
    tp = dgu.shape[0]

    def epilogue(prods, in_refs, out_refs, i, cols):
        r = in_refs[3][...]
        xh = in_refs[2][...] * r
        dxn = prods[0]
        dxh = dxn * in_refs[4][...]
        out_refs[0][...] = r * (dxh - xh * jnp.mean(dxh * xh, axis=1, keepdims=True)) + in_refs[5][...]
        part = jnp.sum(dxn * xh, axis=0, keepdims=True)

        @pl.when(i == 0)
        def _():
            out_refs[1][...] = part

        @pl.when(i > 0)
        def _():
            out_refs[1][...] += part

    const = lambda shape: pl.BlockSpec(shape, lambda j, i, kk: (0,) * len(shape))
    return _mm_fused(
        [(dgu, _blk(TM, 2 * D_FF)), (wgu_t, const((2 * D_FF, D))),
         (h1, _blk(TM, D)), (rstd, _blk(TM, 1)), (g2, const((1, D))), (dh2, _blk(TM, D))],
        [(0, 1)], nt=False, m=tp, n=D, tm=TM, tn=D,
        outs=[(jax.ShapeDtypeStruct((tp, D), F32), _blk(TM, D)), (jax.ShapeDtypeStruct((1, D), F32), const((1, D)))],
        epilogue=epilogue, name=name)


def _branch_merge(y_m, y_g, wbm, wbg, pbig, name):
    tp = y_m.shape[0]

    def epilogue(prods, in_refs, out_refs, i, cols):
        pm, pg = prods[0].astype(BF), prods[1].astype(BF)
        out_refs[0][:, cols] = pm
        out_refs[1][:, cols] = pg
        out_refs[2][:, cols] = (_sigmoid(in_refs[4][:, cols].astype(F32)) * pm.astype(F32)
                                + _sigmoid(in_refs[5][:, cols].astype(F32)) * pg.astype(F32)).astype(BF)

    const = lambda shape: pl.BlockSpec(shape, lambda j, i, kk: (0,) * len(shape))
    shp = jax.ShapeDtypeStruct((tp, D), BF)
    return _mm_fused(
        [(y_m, _blk(TM, D)), (wbm, const((D, D))), (y_g, _blk(TM, D)), (wbg, const((D, D))),
         (pbig, _blk(TM, D, lambda j, kk: CB_GM)), (pbig, _blk(TM, D, lambda j, kk: CB_GG))],
        [(0, 1), (2, 3)], nt=False, m=tp, n=D, tm=TM, tn=D,
        outs=[(shp, _blk(TM, D)), (shp, _blk(TM, D)), (shp, _blk(TM, D))], epilogue=epilogue, name=name, sub=SUB_COLS)


def _merge_d(dh1, wout, pm, pg, pbig, name):
    tp = dh1.shape[0]

    def epilogue(prods, in_refs, out_refs, i, cols):
        d = prods[0]
        sm = _sigmoid(in_refs[4][:, cols].astype(F32))
        sg = _sigmoid(in_refs[5][:, cols].astype(F32))
        out_refs[0][:, cols] = (d * sm).astype(BF)
        out_refs[1][:, cols] = (d * sg).astype(BF)
        out_refs[2][:, cols] = (d * in_refs[2][:, cols].astype(F32) * sm * (1.0 - sm)).astype(BF)
        out_refs[2][:, _cols_at(cols, D)] = (d * in_refs[3][:, cols].astype(F32) * sg * (1.0 - sg)).astype(BF)

    const = lambda shape: pl.BlockSpec(shape, lambda j, i, kk: (0,) * len(shape))
    shp = jax.ShapeDtypeStruct((tp, D), BF)
    return _mm_fused(
        [(dh1, _blk(TM, D)), (wout, const((D, D))), (pm, _blk(TM, D)), (pg, _blk(TM, D)),
         (pbig, _blk(TM, D, lambda j, kk: CB_GM)), (pbig, _blk(TM, D, lambda j, kk: CB_GG))],
        [(0, 1)], nt=True, m=tp, n=D, tm=TM, tn=D,
        outs=[(shp, _blk(TM, D)), (shp, _blk(TM, D)),
              (jax.ShapeDtypeStruct((tp, N_ALL), BF), _blk(TM, 2 * D, lambda j, kk: CB_GM // 2))],
        epilogue=epilogue, name=name, sub=SUB_COLS)


def _out_proj_norm(merged, wout, h0, g2, name):
    tp = merged.shape[0]
    tm = _mm_rows(tp)

    def epilogue(prods, in_refs, out_refs, i, cols):
        x = prods[0] + in_refs[2][...]
        r = lax.rsqrt(jnp.mean(x * x, axis=1, keepdims=True) + EPS)
        out_refs[0][...] = x
        out_refs[1][...] = (x * r * in_refs[3][...]).astype(BF)
        out_refs[2][...] = r

    const = lambda shape: pl.BlockSpec(shape, lambda j, i, kk: (0,) * len(shape))
    return _mm_fused(
        [(merged, _blk(tm, D)), (wout, const((D, D))), (h0, _blk(tm, D)), (g2, const((1, D)))],
        [(0, 1)], nt=False, m=tp, n=D, tm=tm, tn=D,
        outs=[(jax.ShapeDtypeStruct((tp, D), F32), _blk(tm, D)), (jax.ShapeDtypeStruct((tp, D), BF), _blk(tm, D)),
              (jax.ShapeDtypeStruct((tp, 1), F32), _blk(tm, 1))],
        epilogue=epilogue, name=name)


def _adamw(w, g, m, v, name):
    rows, cols = w.shape
    by_cols = rows % 128 != 0 and cols % 128 == 0 and rows * cols > 128 * 1024
    tr = rows if (by_cols or rows % 128 != 0) else 128
    tc = 128 if by_cols else cols

    def body(w_ref, g_ref, m_ref, v_ref, d_ref, nm_ref, nv_ref):
        gv = g_ref[...]
        nm = ADAM_B1 * m_ref[...] + (1.0 - ADAM_B1) * gv
        nv = ADAM_B2 * v_ref[...] + (1.0 - ADAM_B2) * (gv * gv)
        m_hat = nm / (1.0 - ADAM_B1 ** ADAM_STEP)
        v_hat = nv / (1.0 - ADAM_B2 ** ADAM_STEP)
        d_ref[...] = -ADAM_LR * (m_hat / (jnp.sqrt(v_hat) + ADAM_EPS) + ADAM_WD * w_ref[...])
        nm_ref[...] = nm
        nv_ref[...] = nv

    spec = pl.BlockSpec((tr, tc), (lambda i: (0, i)) if by_cols else (lambda i: (i, 0)))
    shp = jax.ShapeDtypeStruct((rows, cols), F32)
    return pl.pallas_call(body, grid=(cols // tc if by_cols else rows // tr,), in_specs=[spec] * 4, out_specs=[spec] * 3,
                          out_shape=(shp,) * 3, compiler_params=_params(), name=name)(w, g, m, v)


def _place_small(dsmall, dproj, name):
    tp = dsmall.shape[0]

    def body(s_ref, _, o_ref):
        o_ref[...] = s_ref[...]

    return pl.pallas_call(
        body, grid=(tp // TM,), in_specs=[_row_spec(N_SMALL), ANY], out_specs=_row_spec(N_SMALL, N_BIG // N_SMALL),
        out_shape=jax.ShapeDtypeStruct(dproj.shape, dproj.dtype), input_output_aliases={1: 0},
        compiler_params=_params(), name=name)(dsmall, dproj)


def _row_tile(rows, cap=512):
    best = rows
    for cand in range(8, min(rows, cap) + 1, 8):
        if rows % cand == 0:
            best = cand
    return best


def _add2(a, b, out_dtype, name):
    rows, cols = a.shape
    tr = _row_tile(rows)

    def body(a_ref, b_ref, o_ref):
        o_ref[...] = (a_ref[...] + b_ref[...]).astype(o_ref.dtype)

    spec = pl.BlockSpec((tr, cols), lambda i: (i, 0))
    return pl.pallas_call(body, grid=(rows // tr,), in_specs=[spec] * 2, out_specs=spec,
                          out_shape=jax.ShapeDtypeStruct((rows, cols), out_dtype), compiler_params=_params(), name=name)(a, b)


def _add4(first, rest, name):
    rows, cols = first.shape
    tr = _row_tile(rows, 256)

    def body(f_ref, r_ref, o_ref):
        up = lambda v: v.astype(F32)
        o_ref[...] = ((up(f_ref[...]) + up(r_ref[0])) + up(r_ref[1])) + up(r_ref[2])

    return pl.pallas_call(body, grid=(rows // tr,),
                          in_specs=[pl.BlockSpec((tr, cols), lambda i: (i, 0)), pl.BlockSpec((3, tr, cols), lambda i: (0, i, 0))],
                          out_specs=pl.BlockSpec((tr, cols), lambda i: (i, 0)),
                          out_shape=jax.ShapeDtypeStruct((rows, cols), F32), compiler_params=_params(), name=name)(first, rest)


def _chunk_consts(length=CHUNK):
    r2 = lax.broadcasted_iota(jnp.int32, (length, length), 0)
    c2 = lax.broadcasted_iota(jnp.int32, (length, length), 1)
    tri = r2 >= c2
    return dict(tri=tri, tril_f=tri.astype(F32), triu_f=(r2 <= c2).astype(F32),
                lane=lax.broadcasted_iota(jnp.int32, (length, N_SMALL), 1),
                rowio=lax.broadcasted_iota(jnp.int32, (length, 1), 0),
                ones=jnp.ones((length, N_SMALL), F32))


def _valid_rows(block, c):
    row = block * TM + c * CHUNK + lax.broadcasted_iota(jnp.int32, (CHUNK, 1), 0)
    return row >= FIRST_VALID


def _col(x, lane, idx):
    return jnp.sum(jnp.where(lane == idx, x, 0.0), axis=1, keepdims=True)


def _last_row(x, rowio):
    return jnp.sum(jnp.where(rowio == rowio.shape[0] - 1, x, 0.0), axis=0, keepdims=True)


def _sum_all(x):
    return jnp.sum(jnp.sum(x, axis=1, keepdims=True), axis=0, keepdims=True)


def _headnorm_fwd(hm, gain, gate_act):
    rs = lax.rsqrt(jnp.mean(hm * hm, axis=1, keepdims=True) + EPS)
    return hm * rs * gain * gate_act


def _headnorm_bwd(dy, hm, gain, gate_act):
    rs = lax.rsqrt(jnp.mean(hm * hm, axis=1, keepdims=True) + EPS)
    xh = hm * rs
    dact = dy * xh * gain
    dgain = jnp.sum(dy * gate_act * xh, axis=0, keepdims=True)
    dxh = dy * gate_act * gain
    dhm = rs * (dxh - xh * jnp.mean(dxh * xh, axis=1, keepdims=True))
    return dhm, dact, dgain


def _mlstm_gates(sm, gbias, valid, k):
    pre = sm + gbias
    lf = jnp.where(valid, _logsig(pre), 0.0)
    b_all = _nn(k["tril_f"], lf, precision=HI)
    li_all = jnp.where(valid, pre, NEG)
    return pre, li_all, b_all


def _mlstm_open(h, qh, kh, c_st, li_all, b_all, k):
    lane = k["lane"]
    sel = jnp.where(lane == h, 1.0, 0.0) - jnp.where(lane == NH + h, 1.0, 0.0)
    x = jnp.where(lane < NH, li_all, jnp.where(lane < 2 * NH, b_all, 0.0))
    cb = c_st.astype(BF)
    return dict(ubc=_nt(sel, x, precision=HI), sim=_nt(qh, kh), cb=cb, cq=_nt(qh, cb))


def _mlstm_weights(h, f, qh, vh, li_all, b_all, n_row, m11, k):
    lane, tri, rowio = k["lane"], k["tri"], k["rowio"]
    b_col = _col(b_all, lane, NH + h)
    li_col = _col(li_all, lane, h)
    dmat = jnp.where(tri, b_col + f["ubc"], NEG)
    m_row = jnp.maximum(b_col + m11, jnp.max(dmat, axis=1, keepdims=True))
    e = jnp.exp(dmat - m_row)
    w_mat = e * f["sim"]
    a = jnp.exp(b_col + m11 - m_row)
    qf = qh.astype(F32)
    nq = jnp.sum(qf * n_row, axis=1, keepdims=True)
    g = _last_row(b_col, rowio)
    wlog = g - b_col + li_col
    m_new = jnp.maximum(g + m11, jnp.max(wlog, axis=0, keepdims=True))
    a_s = jnp.exp(g + m11 - m_new)
    w = jnp.exp(wlog - m_new)
    return dict(f, e=e, w_mat=w_mat, a=a, qf=qf, nq=nq, m_row=m_row, m_new=m_new, a_s=a_s, w=w,
                wv=_nn(w_mat.astype(BF), vh))


def _mlstm_out(f):
    num = f["a"] * f["cq"] + f["wv"]
    den = f["a"] * f["nq"] + jnp.sum(f["w_mat"], axis=1, keepdims=True)
    floor = jnp.exp(-f["m_row"])
    r = jnp.maximum(jnp.abs(den), floor)
    return dict(f, den=den, floor=floor, r=r, hm=num / r)


def _mlstm_fwd(qk, pbig, small, gbias, headg, name):
    tp = qk.shape[0]
    nb = tp // TM

    def body(qk_ref, v_ref, mo_ref, sm_ref, gb_ref, hg_ref, y_ref, cs_ref, ns_ref, c_scr, n_scr):
        blk = pl.program_id(0)

        @pl.when(blk == 0)
        def _():
            c_scr[...] = jnp.zeros_like(c_scr)
            n_scr[...] = jnp.zeros_like(n_scr)

        k = _chunk_consts()
        io8 = lax.broadcasted_iota(jnp.int32, (8, DQK), 0)

        def chunk(c, carry):
            r0 = pl.multiple_of(c * CHUNK, CHUNK)
            rows = pl.ds(r0, CHUNK)
            valid = _valid_rows(blk, c)
            _, li_all, b_all = _mlstm_gates(sm_ref[rows, :], gb_ref[...], valid, k)
            heads = range(NH)
            qs = [qk_ref[rows, h * DQK:(h + 1) * DQK] for h in heads]
            ks = [qk_ref[rows, NH * DQK + h * DQK:NH * DQK + (h + 1) * DQK] for h in heads]
            vs = [v_ref[rows, h * DV:(h + 1) * DV] for h in heads]
            cst = [c_scr[h] for h in heads]
            nrow = [n_scr[h, 0:1, :] for h in heads]
            m11 = [jnp.max(n_scr[h, 1:2, :], axis=1, keepdims=True) for h in heads]
            f = [_mlstm_open(h, qs[h], ks[h], cst[h], li_all, b_all, k) for h in heads]
            f = [_mlstm_weights(h, f[h], qs[h], vs[h], li_all, b_all, nrow[h], m11[h], k) for h in heads]
            wk = [f[h]["w"] * ks[h].astype(F32) for h in heads]
            kv = [_tn(vs[h], wk[h].astype(BF)) for h in heads]
            for h in heads:
                hm = _mlstm_out(f[h])["hm"]
                gate = _sigmoid(mo_ref[rows, h * DV:(h + 1) * DV].astype(F32))
                y_ref[rows, h * DV:(h + 1) * DV] = _headnorm_fwd(hm, hg_ref[:, h * DV:(h + 1) * DV], gate).astype(BF)
                cs_ref[c, h] = f[h]["cb"]
                ns_ref[c, h] = jnp.where(io8 == 0, nrow[h], jnp.where(io8 == 1, m11[h], 0.0))
                c_scr[h] = f[h]["a_s"] * cst[h] + kv[h]
                n_scr[h, 0:1, :] = f[h]["a_s"] * nrow[h] + jnp.sum(wk[h], axis=0, keepdims=True)
                n_scr[h, 1:2, :] = jnp.broadcast_to(f[h]["m_new"], (1, DQK))
            return carry

        lax.fori_loop(0, CPB, chunk, 0, unroll=2)

    return pl.pallas_call(
        body, grid=(nb,),
        in_specs=[_row_spec(D), _row_spec(D, CB_MV), _row_spec(D, CB_MO), _row_spec(N_SMALL), _full_spec((1, N_SMALL)), _full_spec((1, D))],
        out_specs=[_row_spec(D), pl.BlockSpec((CPB, NH, DV, DQK), lambda i: (i, 0, 0, 0)),
                   pl.BlockSpec((CPB, NH, 8, DQK), lambda i: (i, 0, 0, 0))],
        out_shape=(jax.ShapeDtypeStruct((tp, D), BF), jax.ShapeDtypeStruct((tp // CHUNK, NH, DV, DQK), BF),
                   jax.ShapeDtypeStruct((tp // CHUNK, NH, 8, DQK), F32)),
        scratch_shapes=[pltpu.VMEM((NH, DV, DQK), F32), pltpu.VMEM((NH, 8, DQK), F32)],
        compiler_params=_params(), name=name)(qk, pbig, pbig, small, gbias, headg)


def _mlstm_bwd(dy, qk, pbig, small, gbias, headg, cs, ns, dproj, name, ride=()):
    tp = qk.shape[0]
    nb = tp // TM
    nr = len(ride)

    def body(*refs):
        dy_ref, qk_ref, v_ref, mo_ref, sm_ref, gb_ref, hg_ref, cs_ref, ns_ref = refs[:9]
        ride_in = refs[10:10 + nr]
        dqk_ref, dproj_ref, dsm_ref, dgb_ref, dhg_ref = refs[10 + nr:15 + nr]
        ride_out = refs[15 + nr:15 + 2 * nr]
        dc_scr, dn_scr = refs[15 + 2 * nr:17 + 2 * nr]
        step = pl.program_id(0)
        blk = nb - 1 - step
        sent = _scatter_copies(ride_in, ride_out, *refs[17 + 2 * nr:]) if nr else []

        @pl.when(step == 0)
        def _():
            dc_scr[...] = jnp.zeros_like(dc_scr)
            dn_scr[...] = jnp.zeros_like(dn_scr)
            dgb_ref[...] = jnp.zeros_like(dgb_ref)
            dhg_ref[...] = jnp.zeros_like(dhg_ref)
            for cp in sent:
                cp.start()

        k = _chunk_consts()
        lane, rowio = k["lane"], k["rowio"]

        def chunk(cc, carry):
            c = CPB - 1 - cc
            r0 = pl.multiple_of(c * CHUNK, CHUNK)
            rows = pl.ds(r0, CHUNK)
            valid = _valid_rows(blk, c)
            pre, li_all, b_all = _mlstm_gates(sm_ref[rows, :], gb_ref[...], valid, k)
            dli_all = jnp.zeros((CHUNK, N_SMALL), F32)
            db_all = jnp.zeros((CHUNK, N_SMALL), F32)
            heads = range(NH)
            qs = [qk_ref[rows, h * DQK:(h + 1) * DQK] for h in heads]
            ks = [qk_ref[rows, NH * DQK + h * DQK:NH * DQK + (h + 1) * DQK] for h in heads]
            vs = [v_ref[rows, h * DV:(h + 1) * DV] for h in heads]
            cst = [cs_ref[c, h].astype(F32) for h in heads]
            nrow = [ns_ref[c, h, 0:1, :] for h in heads]
            m11 = [jnp.max(ns_ref[c, h, 1:2, :], axis=1, keepdims=True) for h in heads]
            f = [_mlstm_open(h, qs[h], ks[h], cst[h], li_all, b_all, k) for h in heads]
            f = [_mlstm_weights(h, f[h], qs[h], vs[h], li_all, b_all, nrow[h], m11[h], k) for h in heads]
            f = [_mlstm_out(f[h]) for h in heads]
            t = []
            for h in heads:
                gain = hg_ref[:, h * DV:(h + 1) * DV]
                gate = _sigmoid(mo_ref[rows, h * DV:(h + 1) * DV].astype(F32))
                dhm, dgate, dgain = _headnorm_bwd(dy_ref[rows, h * DV:(h + 1) * DV].astype(F32), f[h]["hm"], gain, gate)
                dproj_ref[rows, D + h * DV:D + (h + 1) * DV] = (dgate * gate * (1.0 - gate)).astype(BF)
                dhg_ref[:, h * DV:(h + 1) * DV] += dgain
                r, den = f[h]["r"], f[h]["den"]
                dnum = dhm / r
                dr = -jnp.sum(dhm * f[h]["hm"], axis=1, keepdims=True) / r
                dden = jnp.where(jnp.abs(den) > f[h]["floor"], dr * jnp.sign(den), 0.0)
                dnb = dnum.astype(BF)
                dc_new = dc_scr[h]
                dcb = dc_new.astype(BF)
                t.append(dict(dnum=dnum, dden=dden, dnb=dnb, dc_new=dc_new, dn_new=dn_scr[h],
                              dwm=_nt(dnb, vs[h]), vdc=_nn(vs[h], dcb), kdc=_nt(ks[h], dcb)))
            for h in heads:
                dw_mat = t[h]["dwm"] + t[h]["dden"]
                dsim = (f[h]["e"] * dw_mat).astype(BF)
                gm = f[h]["w_mat"] * dw_mat
                t[h].update(gm=gm, dv0=_tn(f[h]["w_mat"].astype(BF), t[h]["dnb"]), dq0=_nn(dsim, ks[h]),
                            dq1=_nn(t[h]["dnb"], f[h]["cb"]), dk0=_tn(dsim, qs[h]),
                            dcq=_tn((f[h]["a"] * t[h]["dnum"]).astype(BF), qs[h]), cs2=_tn(gm, k["ones"], precision=HI))
            for h in heads:
                a, w, a_s = f[h]["a"], f[h]["w"], f[h]["a_s"]
                dnum, dden, dc_new, dn_new, vdc, gm = (t[h][n] for n in ("dnum", "dden", "dc_new", "dn_new", "vdc", "gm"))
                kf = ks[h].astype(F32)
                dproj_ref[rows, h * DV:(h + 1) * DV] = (t[h]["dv0"] + w * t[h]["kdc"]).astype(BF)
                adden = a * dden
                dqk_ref[rows, h * DQK:(h + 1) * DQK] = t[h]["dq0"] + a * t[h]["dq1"] + adden * nrow[h]
                dqk_ref[rows, NH * DQK + h * DQK:NH * DQK + (h + 1) * DQK] = t[h]["dk0"] + w * vdc + w * dn_new
                da = jnp.sum(dnum * f[h]["cq"], axis=1, keepdims=True) + dden * f[h]["nq"]
                dw = jnp.sum(vdc * kf, axis=1, keepdims=True) + jnp.sum(kf * dn_new, axis=1, keepdims=True)
                da_s = _sum_all(dc_new * cst[h]) + jnp.sum(dn_new * nrow[h], axis=1, keepdims=True)
                wdw = w * dw
                rs = jnp.sum(gm, axis=1, keepdims=True)
                cs_col = _col(t[h]["cs2"], lane, 0)
                dg = a_s * da_s + jnp.sum(wdw, axis=0, keepdims=True)
                db = a * da + rs - cs_col - wdw + jnp.where(rowio == CHUNK - 1, dg, 0.0)
                dli_all = dli_all + jnp.where(lane == h, cs_col + wdw, 0.0)
                db_all = db_all + jnp.where(lane == NH + h, db, 0.0)
                dc_scr[h] = a_s * dc_new + t[h]["dcq"]
                dn_scr[h] = a_s * dn_new + jnp.sum(adden * f[h]["qf"], axis=0, keepdims=True)
            dlf_all = _nn(k["triu_f"], db_all, precision=HI)
            dsm = jnp.where(valid, dli_all + dlf_all * _sigmoid(-pre), 0.0)
            dsm = jnp.where(lane < 2 * NH, dsm, 0.0)
            dsm_ref[rows, :] = dsm
            dgb_ref[0:1, :] += jnp.sum(dsm, axis=0, keepdims=True)
            return carry

        lax.fori_loop(0, CPB, chunk, 0, unroll=2)

        if nr:
            @pl.when(step == nb - 1)
            def _():
                for cp in sent:
                    cp.wait_recv()
                for cp in sent:
                    cp.wait_send()

    rev = lambda col: (lambda i: (nb - 1 - i, col))
    rspec = lambda width, col=0: pl.BlockSpec((TM, width), rev(col))
    ride_shapes, ride_sems = _scatter_shapes(ride) if nr else ((), [])
    outs = pl.pallas_call(
        body, grid=(nb,),
        in_specs=[rspec(D), rspec(D), rspec(D, CB_MV), rspec(D, CB_MO), rspec(N_SMALL), _full_spec((1, N_SMALL)), _full_spec((1, D)),
                  pl.BlockSpec((CPB, NH, DV, DQK), lambda i: (nb - 1 - i, 0, 0, 0)),
                  pl.BlockSpec((CPB, NH, 8, DQK), lambda i: (nb - 1 - i, 0, 0, 0)), ANY] + [ANY] * nr,
        out_specs=[rspec(D), rspec(2 * D, CB_MV // 2), rspec(N_SMALL), _full_spec((8, N_SMALL)), _full_spec((1, D))] + [ANY] * nr,
        out_shape=(jax.ShapeDtypeStruct((tp, D), F32), jax.ShapeDtypeStruct(dproj.shape, BF),
                   jax.ShapeDtypeStruct((tp, N_SMALL), F32), jax.ShapeDtypeStruct((8, N_SMALL), F32),
                   jax.ShapeDtypeStruct((1, D), F32)) + tuple(ride_shapes),
        scratch_shapes=[pltpu.VMEM((NH, DV, DQK), F32), pltpu.VMEM((NH, 1, DQK), F32)] + ride_sems,
        input_output_aliases={9: 1}, compiler_params=_params(), name=name)(dy, qk, pbig, pbig, small, gbias, headg, cs, ns, dproj, *ride)
    return tuple(outs[:5]) + (list(outs[5:]),)


def _gla_loga(sm_ref, a2_ref, a2b_ref, blk):
    za = _nn(sm_ref[...].astype(BF), a2_ref[...]) + a2b_ref[...]
    row = blk * TM + lax.broadcasted_iota(jnp.int32, (TM, 1), 0)
    return za, jnp.where(row >= FIRST_VALID, _logsig(za) / G_TAU, 0.0)


def _gla_head(h, q_ref, k_ref, rows, bc, btot, k):
    sl = slice(h * DQK, (h + 1) * DQK)
    bch = bc[:, sl]
    bth = btot[:, sl]
    gq = q_ref[rows, h * DQK:(h + 1) * DQK].astype(F32)
    gk = k_ref[rows, NH * DQK + h * DQK:NH * DQK + (h + 1) * DQK].astype(F32)
    e_pos = jnp.exp(bch) * (DQK ** -0.5)
    e_neg = jnp.exp(-bch)
    e_end = jnp.exp(bth - bch)
    qd = gq * e_pos
    ki = gk * e_neg
    ke = gk * e_end
    att = jnp.where(k["tri"], _nt(qd.astype(BF), ki.astype(BF)), 0.0)
    return dict(e_pos=e_pos, e_neg=e_neg, e_end=e_end, qd=qd, ki=ki, ke=ke, att=att, decay=jnp.exp(bth))


def _gla_fwd(pbig, small, a2p, a2b, headg, name):
    tp = pbig.shape[0]
    nb = tp // TM

    def body(qk_ref, v_ref, gr_ref, sm_ref, a2_ref, a2b_ref, hg_ref, y_ref, ss_ref, s_scr, lg_scr):
        blk = pl.program_id(0)

        @pl.when(blk == 0)
        def _():
            s_scr[...] = jnp.zeros_like(s_scr)

        k = _chunk_consts(G_CHUNK)
        _, loga = _gla_loga(sm_ref, a2_ref, a2b_ref, blk)
        lg_scr[...] = loga

        def chunk(c, carry):
            r0 = pl.multiple_of(c * G_CHUNK, G_CHUNK)
            rows = pl.ds(r0, G_CHUNK)
            bc = _nn(k["tril_f"], lg_scr[rows, :], precision=HI)
            btot = _last_row(bc, k["rowio"])
            heads = range(NH)
            f = [_gla_head(h, qk_ref, qk_ref, rows, bc, btot, k) for h in heads]
            vs = [v_ref[rows, h * DV:(h + 1) * DV] for h in heads]
            sst = [s_scr[h] for h in heads]
            sbs = [s.astype(BF) for s in sst]
            inter = [_nt(f[h]["qd"].astype(BF), sbs[h]) for h in heads]
            intra = [_nn(f[h]["att"].astype(BF), vs[h]) for h in heads]
            kv = [_tn(vs[h], f[h]["ke"].astype(BF)) for h in heads]
            for h in heads:
                gr = gr_ref[rows, h * DV:(h + 1) * DV].astype(F32)
                y_ref[rows, h * DV:(h + 1) * DV] = _headnorm_fwd(intra[h] + inter[h], hg_ref[:, h * DV:(h + 1) * DV],
                                                                   gr * _sigmoid(gr)).astype(BF)
                ss_ref[c, h] = sbs[h]
                s_scr[h] = sst[h] * f[h]["decay"] + kv[h]
            return carry

        lax.fori_loop(0, G_CPB, chunk, 0, unroll=2)

    return pl.pallas_call(
        body, grid=(nb,),
        in_specs=[_row_spec(D, CB_GQK), _row_spec(D, CB_GV), _row_spec(D, CB_GR), _row_spec(N_SMALL),
                  _full_spec((N_SMALL, NH * DQK)), _full_spec((1, NH * DQK)), _full_spec((1, D))],
        out_specs=[_row_spec(D), pl.BlockSpec((G_CPB, NH, DV, DQK), lambda i: (i, 0, 0, 0))],
        out_shape=(jax.ShapeDtypeStruct((tp, D), BF), jax.ShapeDtypeStruct((tp // G_CHUNK, NH, DV, DQK), BF)),
        scratch_shapes=[pltpu.VMEM((NH, DV, DQK), F32), pltpu.VMEM((TM, NH * DQK), F32)],
        compiler_params=_params(), name=name)(pbig, pbig, pbig, small, a2p, a2b, headg)


def _gla_bwd(dy, pbig, small, a2p, a2b, headg, ss, dsm_m, dproj, name):
    tp = pbig.shape[0]
    nb = tp // TM
    nqk = NH * DQK

    def body(dy_ref, qk_ref, v_ref, gr_ref, sm_ref, a2_ref, a2b_ref, hg_ref, ss_ref, dsmm_ref, _,
             dproj_ref, dsm_ref, da2_ref, da2b_ref, dhg_ref, ds_scr, lg_scr, dza_scr):
        step = pl.program_id(0)
        blk = nb - 1 - step

        @pl.when(step == 0)
        def _():
            ds_scr[...] = jnp.zeros_like(ds_scr)
            da2_ref[...] = jnp.zeros_like(da2_ref)
            da2b_ref[...] = jnp.zeros_like(da2b_ref)
            dhg_ref[...] = jnp.zeros_like(dhg_ref)

        k = _chunk_consts(G_CHUNK)
        rowio = k["rowio"]
        za, loga = _gla_loga(sm_ref, a2_ref, a2b_ref, blk)
        lg_scr[...] = loga

        def chunk(cc, carry):
            c = G_CPB - 1 - cc
            r0 = pl.multiple_of(c * G_CHUNK, G_CHUNK)
            rows = pl.ds(r0, G_CHUNK)
            bc = _nn(k["tril_f"], lg_scr[rows, :], precision=HI)
            btot = _last_row(bc, rowio)
            heads = range(NH)
            f = [_gla_head(h, qk_ref, qk_ref, rows, bc, btot, k) for h in heads]
            vs = [v_ref[rows, h * DV:(h + 1) * DV] for h in heads]
            sbs = [ss_ref[c, h] for h in heads]
            qdb = [f[h]["qd"].astype(BF) for h in heads]
            attb = [f[h]["att"].astype(BF) for h in heads]
            inter = [_nt(qdb[h], sbs[h]) for h in heads]
            intra = [_nn(attb[h], vs[h]) for h in heads]
            dsn = [ds_scr[h] for h in heads]
            dsb = [d.astype(BF) for d in dsn]
            dke = [_nn(vs[h], dsb[h]) for h in heads]
            dv1 = [_nt(f[h]["ke"].astype(BF), dsb[h]) for h in heads]
            t = []
            for h in heads:
                gr = gr_ref[rows, h * DV:(h + 1) * DV].astype(F32)
                sg = _sigmoid(gr)
                gain = hg_ref[:, h * DV:(h + 1) * DV]
                do, dact, dgain = _headnorm_bwd(dy_ref[rows, h * DV:(h + 1) * DV].astype(F32), intra[h] + inter[h], gain, gr * sg)
                dproj_ref[rows, 2 * D + h * DV:2 * D + (h + 1) * DV] = (dact * sg * (1.0 + gr * (1.0 - sg))).astype(BF)
                dhg_ref[:, h * DV:(h + 1) * DV] += dgain
                dob = do.astype(BF)
                t.append(dict(dob=dob, datt=_nt(dob, vs[h]), dv0=_tn(attb[h], dob), dq1=_nn(dob, sbs[h]), dsq=_tn(dob, qdb[h])))
            for h in heads:
                datt = jnp.where(k["tri"], t[h]["datt"], 0.0).astype(BF)
                t[h].update(dq0=_nn(datt, f[h]["ki"].astype(BF)), dki=_tn(datt, qdb[h]))
            dbc_parts = []
            for h in heads:
                dqd = t[h]["dq0"] + t[h]["dq1"]
                dki = t[h]["dki"]
                dproj_ref[rows, D + h * DV:D + (h + 1) * DV] = (t[h]["dv0"] + dv1[h]).astype(BF)
                dproj_ref[rows, h * DQK:(h + 1) * DQK] = (dqd * f[h]["e_pos"]).astype(BF)
                dproj_ref[rows, nqk + h * DQK:nqk + (h + 1) * DQK] = (dki * f[h]["e_neg"] + dke[h] * f[h]["e_end"]).astype(BF)
                dke_ke = dke[h] * f[h]["ke"]
                dbtot = (jnp.sum(dke_ke, axis=0, keepdims=True)
                         + jnp.sum(dsn[h] * sbs[h].astype(F32), axis=0, keepdims=True) * f[h]["decay"])
                dbc_parts.append(dqd * f[h]["qd"] - dki * f[h]["ki"] - dke_ke + jnp.where(rowio == G_CHUNK - 1, dbtot, 0.0))
                ds_scr[h] = dsn[h] * f[h]["decay"] + t[h]["dsq"]
            dbc = jnp.concatenate(dbc_parts, axis=1)
            dza_scr[rows, :] = _nn(k["triu_f"], dbc, precision=HI)
            return carry

        lax.fori_loop(0, G_CPB, chunk, 0, unroll=2)
        row = blk * TM + lax.broadcasted_iota(jnp.int32, (TM, 1), 0)
        dza = jnp.where(row >= FIRST_VALID, dza_scr[...] * (_sigmoid(-za) / G_TAU), 0.0)
        dzb = dza.astype(BF)
        dsm_ref[...] = (_nt(dzb, a2_ref[...]) + dsmm_ref[...]).astype(BF)
        da2_ref[...] += _tn(sm_ref[...].astype(BF), dzb)
        da2b_ref[...] += jnp.sum(dza, axis=0, keepdims=True)

    rspec = lambda width, col=0: pl.BlockSpec((TM, width), lambda i: (nb - 1 - i, col))
    return pl.pallas_call(
        body, grid=(nb,),
        in_specs=[rspec(D), rspec(D, CB_GQK), rspec(D, CB_GV), rspec(D, CB_GR), rspec(N_SMALL),
                  _full_spec((N_SMALL, nqk)), _full_spec((1, nqk)), _full_spec((1, D)),
                  pl.BlockSpec((G_CPB, NH, DV, DQK), lambda i: (nb - 1 - i, 0, 0, 0)), rspec(N_SMALL), ANY],
        out_specs=[rspec(3 * D, 0), rspec(N_SMALL), _full_spec((N_SMALL, nqk)), _full_spec((1, nqk)), _full_spec((1, D))],
        out_shape=(jax.ShapeDtypeStruct(dproj.shape, BF),
                   jax.ShapeDtypeStruct((tp, N_SMALL), BF), jax.ShapeDtypeStruct((N_SMALL, nqk), F32),
                   jax.ShapeDtypeStruct((1, nqk), F32), jax.ShapeDtypeStruct((1, D), F32)),
        scratch_shapes=[pltpu.VMEM((NH, DV, DQK), F32), pltpu.VMEM((TM, nqk), F32), pltpu.VMEM((TM, nqk), F32)],
        input_output_aliases={10: 0}, compiler_params=_params(), name=name)(dy, pbig, pbig, pbig, small, a2p, a2b, headg, ss, dsm_m, dproj)


PIECE_BYTES = 1 << 20
MAX_PIECES = 32


def _place():
    return lax.axis_index("x"), lax.axis_index("y"), lax.axis_index("c")


def _piece_rows(rows, row_bytes, align):
    want = min(MAX_PIECES, max(1, -(-rows * row_bytes // PIECE_BYTES)))
    best = rows
    for k in range(1, want + 1):
        if rows % k == 0 and (rows // k) % align == 0:
            best = rows // k
    return best


def _remote(src, dst, send_sems, recv_sems, k, to):
    return pltpu.make_async_remote_copy(src_ref=src, dst_ref=dst, send_sem=send_sems.at[k], recv_sem=recv_sems.at[k],
                                        device_id=to, device_id_type=MESH)


def _all_gather_chips(p, name):
    rd = _gather_rider(p)

    def body(*refs):
        start, middle, finish = rd["make"](refs[:1], refs[1:2], refs[2:])
        start()
        middle()
        finish()

    return pl.pallas_call(body, in_specs=[ANY], out_specs=[ANY], out_shape=rd["out_shapes"], scratch_shapes=rd["sems"],
                          name=name)(p)[0]


def _gather_rider(p):
    r, n = p.shape
    rh = r // 2
    align = 32 // p.dtype.itemsize
    assert r % (2 * align) == 0
    cr = _piece_rows(rh, n * p.dtype.itemsize, align)

    def make(in_refs, out_refs, sem_refs):
        p_ref, o_ref = in_refs[0], out_refs[0]
        send_sems, recv_sems = sem_refs
        x, y, c = _place()
        chips = [(1 - x, y), (x, 1 - y), (1 - x, 1 - y)]
        sib = (x, y, 1 - c)

        def half(hc, piece=None):
            if piece is None:
                return pl.ds(pl.multiple_of(hc * rh, align), rh)
            return pl.ds(pl.multiple_of(hc * rh + piece * cr, align), cr)

        first = [_remote(p_ref.at[half(c)], o_ref.at[j, half(c)], send_sems, recv_sems, j, (*chip, c))
                 for j, chip in enumerate(chips)]
        passed = [[_remote(o_ref.at[j, half(c, i)], o_ref.at[j, half(c, i)], send_sems, recv_sems, 3 + j, sib)
                   for i in range(rh // cr)] for j in range(3)]
        blocks = [_remote(o_ref.at[j, half(c)], o_ref.at[j, half(1 - c)], send_sems, recv_sems, 3 + j, sib) for j in range(3)]

        def start():
            for cp in first:
                cp.start()

        def middle():
            for j, cp in enumerate(first):
                cp.wait_recv()
                for piece in passed[j]:
                    piece.start()

        def finish():
            for block in blocks:
                block.wait_send()
                block.wait_recv()
            for cp in first:
                cp.wait_send()

        return start, middle, finish

    return dict(inputs=[p], out_shapes=(jax.ShapeDtypeStruct((3, r, n), p.dtype),),
                sems=[pltpu.SemaphoreType.DMA((6,)), pltpu.SemaphoreType.DMA((6,))], make=make)


def _scatter_rider(items):
    out_shapes, sems = _scatter_shapes(items)

    def make(in_refs, out_refs, sem_refs):
        sent = _scatter_copies(in_refs, out_refs, *sem_refs)

        def start():
            for cp in sent:
                cp.start()

        def finish():
            for cp in sent:
                cp.wait_recv()
            for cp in sent:
                cp.wait_send()

        return start, (lambda: None), finish

    return dict(inputs=list(items), out_shapes=out_shapes, sems=sems, make=make)


def _by_chip(mine, others):
    me = 2 * lax.axis_index("x") + lax.axis_index("y")
    by_mask = jnp.stack([mine, others[1], others[0], others[2]])
    return [lax.dynamic_index_in_dim(by_mask, q ^ me, 0, keepdims=False) for q in range(4)]


def _swap_halves(items, name):
    k = len(items)

    def body(*refs):
        a_refs, got_refs = refs[:k], refs[k:2 * k]
        send_sems, recv_sems = refs[2 * k:]
        x, y, c = _place()
        sib = (x, y, 1 - c)
        for i, a in enumerate(items):
            _, r, n = a.shape
            rh = r // 2
            cr = _piece_rows(rh, n * a.dtype.itemsize, 8)
            for q in range(4):
                for t in range(rh // cr):
                    other = pl.ds(pl.multiple_of((1 - c) * rh + t * cr, 8), cr)
                    _remote(a_refs[i].at[q, other], got_refs[i].at[q, pl.ds(t * cr, cr)], send_sems, recv_sems, i, sib).start()
        for i, a in enumerate(items):
            block = _remote(a_refs[i].at[:, pl.ds(0, a.shape[1] // 2)], got_refs[i], send_sems, recv_sems, i, sib)
            block.wait_send()
            block.wait_recv()

    return pl.pallas_call(
        body, in_specs=[ANY] * k, out_specs=[ANY] * k,
        out_shape=tuple(jax.ShapeDtypeStruct((4, a.shape[1] // 2, a.shape[2]), a.dtype) for a in items),
        scratch_shapes=[pltpu.SemaphoreType.DMA((k,)), pltpu.SemaphoreType.DMA((k,))], name=name)(*items)


def _scatter_copies(s_refs, o_refs, send_sems, recv_sems):
    x, y, c = _place()
    chips = [(1 - x, y), (x, 1 - y), (1 - x, 1 - y)]
    return [_remote(s_refs[i].at[2 * cx + cy], o_refs[i].at[j], send_sems, recv_sems, 3 * i + j, (cx, cy, c))
            for i in range(len(s_refs)) for j, (cx, cy) in enumerate(chips)]


def _scatter_shapes(items):
    k = len(items)
    return (tuple(jax.ShapeDtypeStruct((3,) + s.shape[1:], s.dtype) for s in items),
            [pltpu.SemaphoreType.DMA((3 * k,)), pltpu.SemaphoreType.DMA((3 * k,))])


def _scatter_chips(items, name):
    k = len(items)

    def body(*refs):
        sent = _scatter_copies(refs[:k], refs[k:2 * k], *refs[2 * k:])
        for cp in sent:
            cp.start()
        for cp in sent:
            cp.wait_recv()
        for cp in sent:
            cp.wait_send()

    out_shape, scratch = _scatter_shapes(items)
    return pl.pallas_call(body, in_specs=[ANY] * k, out_specs=[ANY] * k, out_shape=out_shape, scratch_shapes=scratch,
                          name=name)(*items)


def _join_halves(items, name):
    k = len(items)

    def body(*refs):
        f_refs, o_refs = refs[:k], refs[k:2 * k]
        send_sems, recv_sems = refs[2 * k:]
        x, y, c = _place()
        sib = (x, y, 1 - c)
        for i, f in enumerate(items):
            rh, n = f.shape
            cr = _piece_rows(rh, n * f.dtype.itemsize, 8)
            for t in range(rh // cr):
                rows = pl.ds(t * cr, cr)
                _remote(f_refs[i].at[rows], o_refs[i].at[rows], send_sems, recv_sems, i, sib).start()
        for i in range(k):
            block = _remote(f_refs[i], o_refs[i], send_sems, recv_sems, i, sib)
            block.wait_send()
            block.wait_recv()

    return pl.pallas_call(
        body, in_specs=[ANY] * k, out_specs=[ANY] * k, out_shape=tuple(jax.ShapeDtypeStruct(f.shape, f.dtype) for f in items),
        scratch_shapes=[pltpu.SemaphoreType.DMA((k,)), pltpu.SemaphoreType.DMA((k,))], name=name)(*items)


SMALL_ROWS = 16
SMALL_GRAD_ROWS = 48
SMALL_SHARD_SHAPES = [(N_META, 256), (4, 256), (G_RANK, 128), (NH, 64), (NH, 64)]
REPL_SHAPES = [(1, D), (1, D), (1, 2, NH), (1, NH * DQK), (1, D), (D,)]
W_IN_SHARD = 2054
W_IN_BLOCK = 2080


def _pack_small(parts, rows=SMALL_ROWS):
    flat = jnp.concatenate([p.reshape(-1) for p in parts])
    return jnp.pad(flat, (0, rows * D - flat.shape[0])).reshape(rows, D)


def _unpack_small(block, shapes):
    flat, out, off = block.reshape(-1), [], 0
    for shp in shapes:
        n = 1
        for s in shp:
            n *= s
        out.append(flat[off:off + n].reshape(shp))
        off += n
    return out


def _proj_rows_from_w_in(w_in_t):
    w_big = jnp.concatenate([w_in_t[3080:5128], w_in_t[5144:6168], w_in_t[0:1024], w_in_t[6168:8216],
                             w_in_t[1024:2048], w_in_t[2056:3080]], axis=0)
    w_small = jnp.concatenate([w_in_t[2048:2056], w_in_t[5128:5144], jnp.zeros((N_SMALL - 24, D), w_in_t.dtype)], axis=0)
    return w_big, w_small


def _w_in_from_proj_rows(d_wall_t):
    big, small = d_wall_t[0:N_BIG], d_wall_t[N_BIG:N_ALL]
    return jnp.concatenate([big[3072:4096], big[6144:7168], small[0:8], big[7168:8192], big[0:2048],
                            small[8:24], big[2048:3072], big[4096:6144]], axis=0)


def kernel(x, meta_tokens, norm1_g, w_in, conv_w, conv_b, m_gate_b, g_a2, g_a2_b, m_head_g, g_head_g, w_branch_m, w_branch_g, w_out, norm2_g, w_ff_gate, w_ff_up, w_ff_down, final_g, loss_target, m_meta_tokens, m_norm1_g, m_w_in, m_conv_w, m_conv_b, m_m_gate_b, m_g_a2, m_g_a2_b, m_m_head_g, m_g_head_g, m_w_branch_m, m_w_branch_g, m_w_out, m_norm2_g, m_w_ff_gate, m_w_ff_up, m_w_ff_down, m_final_g, v_meta_tokens, v_norm1_g, v_w_in, v_conv_w, v_conv_b, v_m_gate_b, v_g_a2, v_g_a2_b, v_m_head_g, v_g_head_g, v_w_branch_m, v_w_branch_g, v_w_out, v_norm2_g, v_w_ff_gate, v_w_ff_up, v_w_ff_down, v_final_g):
    w = _gather_weights(w_in, w_branch_m, w_branch_g, w_out, w_ff_gate, w_ff_up, w_ff_down, meta_tokens, conv_w, g_a2, m_head_g, g_head_g)
    loss_local, dx, grads = _local_step(x[0], loss_target[0], w, norm1_g, conv_b, m_gate_b, g_a2_b, norm2_g, final_g, _Reducer())

    weights = [w_in, w_branch_m, w_branch_g, w_out, w_ff_gate, w_ff_up, w_ff_down, meta_tokens, conv_w, g_a2, m_head_g, g_head_g,
               norm1_g, conv_b, m_gate_b, g_a2_b, norm2_g, final_g]
    moms = [m_w_in, m_w_branch_m, m_w_branch_g, m_w_out, m_w_ff_gate, m_w_ff_up, m_w_ff_down, m_meta_tokens, m_conv_w, m_g_a2,
            m_m_head_g, m_g_head_g, m_norm1_g, m_conv_b, m_m_gate_b, m_g_a2_b, m_norm2_g, m_final_g]
    vels = [v_w_in, v_w_branch_m, v_w_branch_g, v_w_out, v_w_ff_gate, v_w_ff_up, v_w_ff_down, v_meta_tokens, v_conv_w, v_g_a2,
            v_m_head_g, v_g_head_g, v_norm1_g, v_conv_b, v_m_gate_b, v_g_a2_b, v_norm2_g, v_final_g]
    res = {}
    for nm, wt, g, m, v in zip(PACK_ORDER, weights, grads, moms, vels):
        if nm in TRANSPOSED_GRADS:
            to2d = lambda a: jnp.swapaxes(a, -1, -2).reshape(a.shape[-1], a.shape[-2])
            back = lambda a: jnp.swapaxes(a, 0, 1).reshape(wt.shape)
        else:
            to2d = lambda a: a.reshape(wt.size // wt.shape[-1], wt.shape[-1])
            back = lambda a: a.reshape(wt.shape)
        d, nm_, nv_ = _adamw(to2d(wt), g, to2d(m), to2d(v), "adamw_" + nm)
        res[nm] = (back(g), back(d), back(nm_), back(nv_))

    order = ["meta_tokens", "norm1_g", "w_in", "conv_w", "conv_b", "m_gate_b", "g_a2", "g_a2_b", "m_head_g", "g_head_g",
             "w_branch_m", "w_branch_g", "w_out", "norm2_g", "w_ff_gate", "w_ff_up", "w_ff_down", "final_g"]
    loss = lax.psum(loss_local[0, 0], ("x", "y", "c"))
    grad_x = dx.reshape(x.shape)
    return (loss, grad_x, *[res[n][0] for n in order], *[res[n][1] for n in order],
            *[res[n][2] for n in order], *[res[n][3] for n in order])


TRANSPOSED_GRADS = ("w_in", "w_ff_gate", "w_ff_up")
PACK_ORDER = ["w_in", "w_branch_m", "w_branch_g", "w_out", "w_ff_gate", "w_ff_up", "w_ff_down", "meta_tokens", "conv_w", "g_a2",
              "m_head_g", "g_head_g", "norm1_g", "conv_b", "m_gate_b", "g_a2_b", "norm2_g", "final_g"]


def _gather_weights(w_in, w_branch_m, w_branch_g, w_out, w_ff_gate, w_ff_up, w_ff_down, meta_tokens, conv_w, g_a2, m_head_g, g_head_g):
    bf = lambda a: a.astype(BF)
    rows_local = jnp.concatenate([bf(w_branch_m[0]), bf(w_branch_g[0]), bf(w_out[0]), bf(w_ff_down[0]),
                                  bf(w_ff_gate[0].T), bf(w_ff_up[0].T)], axis=0)
    win_local = jnp.pad(bf(w_in[0].T), ((0, W_IN_BLOCK - W_IN_SHARD), (0, 0)))
    small_local = _pack_small([meta_tokens, conv_w[0], g_a2[0], m_head_g[0], g_head_g[0]])
    small_all = _by_chip(small_local, _all_gather_chips(small_local, "gather_small"))
    small_sh = [_unpack_small(small_all[q], SMALL_SHARD_SHAPES) for q in range(4)]
    cat = lambda i: jnp.concatenate([s[i] for s in small_sh], axis=-1)
    return dict(win_local=win_local, rows_local=rows_local, meta=cat(0), convw=cat(1), ga2=cat(2),
                mhg=cat(3).reshape(1, D), ghg=cat(4).reshape(1, D))


def _row_weights(rows_local, gathered):
    rows_all = jnp.stack(_by_chip(rows_local, gathered))
    cut = lambda lo, hi: rows_all[:, lo:hi].reshape(4 * (hi - lo), D)
    return cut(0, 256), cut(256, 512), cut(512, 768), cut(768, 1472), _ffn_weight_rows(cut(1472, 2176), cut(2176, 2880))


def _local_step(x0, target, w, norm1_g, conv_b, m_gate_b, g_a2_b, norm2_g, final_g, reducer):
    meta_f, convw_f, ga2_f, mhg_f, ghg_f = w["meta"], w["convw"], w["ga2"], w["mhg"], w["ghg"]
    gbias =jnp.concatenate([m_gate_b.reshape(1, 2 * NH), jnp.zeros((1, N_SMALL - 2 * NH), F32)], axis=1)
    a2p = jnp.concatenate([jnp.zeros((8, NH * DQK), F32), ga2_f, jnp.zeros((N_SMALL - 24, NH * DQK), F32)], axis=0).astype(BF)
    convb = conv_b.reshape(1, D)
    g1 = norm1_g.reshape(1, D)
    g2 = norm2_g.reshape(1, D)
    gf = final_g.reshape(1, D)
    first = jnp.concatenate([jnp.zeros((FIRST_VALID, D), F32), meta_f], axis=0)

    h0, xn1, rstd1, win_gathered = _embed_norm(x0, first, g1, _gather_rider(w["win_local"]), "rms1")
    win_all = _by_chip(w["win_local"], win_gathered[0])
    w_in_f = jnp.concatenate([win_all[q][0:W_IN_SHARD] for q in range(4)], axis=0)
    w_big, w_small = _proj_rows_from_w_in(w_in_f)
    w_all = jnp.concatenate([w_big, w_small], axis=0)
    pbig, rows_gathered = _mm(xn1, w_big, nt=True, out_dtype=BF, tn=2 * D, name="proj_big", rider=_gather_rider(w["rows_local"]))
    wbm, wbg, wout, wdown, wgu_t = _row_weights(w["rows_local"], rows_gathered[0])
    small = _mm(xn1, w_small, nt=True, out_dtype=F32, tn=N_SMALL, name="proj_small")
    qk = _conv_fwd(pbig, convw_f, convb, "conv_fwd")
    y_m, m_cs, m_ns = _mlstm_fwd(qk, pbig, small, gbias, mhg_f, "mlstm_fwd")
    y_g, g_ss = _gla_fwd(pbig, small, a2p, g_a2_b, ghg_f, "gla_fwd")
    p_m, p_g, merged = _branch_merge(y_m, y_g, wbm, wbg, pbig, "branch_merge")
    h1, hn, rstd2 = _out_proj_norm(merged, wout, h0, g2, "out_proj")
    gu, ff = _ffn_in(hn, wgu_t, "ff_in")
    dh2, loss_local, d_final_g = _ffn_down_loss(ff, wdown, h1, target, gf, "ff_down_loss")

    d_wdown = _mm_tn(ff, dh2, tm=FF_TN, tn=D, name="dw_ff_down")
    dgu = _ffn_d_hidden(dh2, wdown, gu, "d_ff")
    d_wgu_t = _mm_tn(dgu, hn, tm=FF_TN, tn=D, name="dw_ff_in")
    dh1, d_g2 = _ffn_d_in(dgu, wgu_t, h1, rstd2, g2, dh2, "d_hn")
    d_wout = _mm_tn(merged, dh1, tm=D, tn=D, name="dw_out")
    dp_m, dp_g, dproj = _merge_d(dh1, wout, p_m, p_g, pbig, "d_merged")
    dy_m = _mm(dp_m, wbm, nt=True, out_dtype=BF, tn=D, name="d_ym")
    dy_g = _mm(dp_g, wbg, nt=True, out_dtype=BF, tn=D, name="d_yg")
    d_wbm = _mm_tn(y_m, dp_m, tm=D, tn=D, name="dw_branch_m")
    d_wbg = _mm_tn(y_g, dp_g, tm=D, tn=D, name="dw_branch_g")
    fq = D_FF // 4
    gu4 = jnp.transpose(d_wgu_t.reshape(2, 2, 2, fq, D), (0, 2, 1, 3, 4)).reshape(4, 2 * fq, D)
    sq4 = jnp.concatenate([d_wbm.reshape(4, 256, D), d_wbg.reshape(4, 256, D), d_wout.reshape(4, 256, D)], axis=1)
    sums_a = reducer.partial_sums([sq4, d_wdown.reshape(4, fq, D), gu4], BF, "a")
    dqk_m, dproj, dsm_m, d_gbias, d_mhg, recv_a = _mlstm_bwd(dy_m, qk, pbig, small, gbias, mhg_f, m_cs, m_ns, dproj,
                                                              "mlstm_bwd", ride=sums_a)
    dproj, d_convwb = _conv_bwd(dqk_m, pbig, convw_f, convb, dproj, "conv_bwd")
    dproj, dsmall, d_a2p, d_a2b, d_ghg = _gla_bwd(dy_g, pbig, small, a2p, g_a2_b, ghg_f, g_ss, dsm_m, dproj, "gla_bwd")
    dproj = _place_small(dsmall, dproj, "dproj_small")
    d_win = _w_in_from_proj_rows(_mm_tn(dproj, xn1, tm=PROJ_TK, tn=D, name="dw_in"))
    pad = jnp.zeros((W_IN_BLOCK - W_IN_SHARD, D), F32)
    win4 = jnp.stack([jnp.concatenate([d_win[q * W_IN_SHARD:(q + 1) * W_IN_SHARD], pad], axis=0) for q in range(4)])
    sums_b = reducer.partial_sums([win4], BF, "b")
    dxn, recv_b = _mm(dproj, w_all, nt=False, out_dtype=F32, tn=D, tk=PROJ_TK, name="d_xn", rider=_scatter_rider(sums_b))
    dh_first, dx, d_g1 = _rms_bwd(dxn, h0, rstd1, g1, dh1, "rms1_bwd", split_first=True)

    small_sharded = [dh_first[FIRST_VALID:TM], d_convwb[0:4], d_a2p[8:24], d_mhg.reshape(NH, DV), d_ghg.reshape(NH, DV)]
    replicated = [d_g1, d_convwb[4:5], d_gbias[0:1, 0:2 * NH].reshape(1, 2, NH), d_a2b, d_g2, d_final_g.reshape(D)]
    small4 = jnp.broadcast_to(_pack_small(small_sharded + replicated, SMALL_GRAD_ROWS)[None], (4, SMALL_GRAD_ROWS, D))
    sums_c = reducer.partial_sums([small4], F32, "c")
    recv_c = reducer.scatter(sums_c, "c")
    sq, down, gu, win, smalls = reducer.finish(sums_a + sums_b + sums_c, recv_a + recv_b + recv_c, in_chip_order=[4])
    smalls = _unpack_small(smalls, [g.shape for g in small_sharded + replicated])
    me = 2 * lax.axis_index("x") + lax.axis_index("y")
    smalls = ([lax.dynamic_slice_in_dim(g, me * shp[1], shp[1], axis=1) for g, shp in zip(smalls, SMALL_SHARD_SHAPES)]
              + smalls[len(SMALL_SHARD_SHAPES):])
    grads = ([win[0:W_IN_SHARD], sq[0:256], sq[256:512], sq[512:768], gu[0:fq], gu[fq:2 * fq], down]
             + [g.reshape(g.size // g.shape[-1], g.shape[-1]) for g in smalls])
    return loss_local, dx, grads


class _Reducer:
    def partial_sums(self, items, dtype, tag):
        c = lax.axis_index("c")
        got = _swap_halves(items, "reduce_siblings_" + tag)
        sums = []
        for i, (a, g) in enumerate(zip(items, got)):
            rh, n = g.shape[1], g.shape[2]
            own = lax.dynamic_slice_in_dim(a, c * rh, rh, axis=1)
            sums.append(_add2(own.reshape(-1, n), g.reshape(-1, n), dtype, f"reduce_add2_{tag}{i}").reshape(g.shape))
        return sums

    def scatter(self, sums, tag):
        return list(_scatter_chips(sums, "reduce_chips_" + tag))

    def finish(self, sums, from_chips, in_chip_order):
        c = lax.axis_index("c")
        me = 2 * lax.axis_index("x") + lax.axis_index("y")
        halves = []
        for i, (s, f) in enumerate(zip(sums, from_chips)):
            mine = lax.dynamic_index_in_dim(s, me, 0, keepdims=False)
            if i in in_chip_order:
                by_chip = _by_chip(mine, f)
                mine, f = by_chip[0], jnp.stack(by_chip[1:])
            halves.append(_add4(mine, f, f"reduce_add4_{i}"))
        got = _join_halves(halves, "reduce_join")
        return [jnp.concatenate([jnp.where(c == 0, h, g), jnp.where(c == 0, g, h)], axis=0) for h, g in zip(halves, got)]
```

```python
import jax
import jax.numpy as jnp
from jax import lax
from jax.experimental import pallas as pl
from jax.experimental.pallas import tpu as pltpu

F32 = jnp.float32
BF = jnp.bfloat16
HI = lax.Precision.HIGHEST
MESH = pl.DeviceIdType.MESH

D = 1024
N_META = 16
CHUNK = 128
EPS = 1e-6
NH = 4
DV = 256
DQK = 128
G_RANK = 16
G_TAU = 16.0
D_FF = 2816
TM = 512
FIRST_VALID = TM - N_META
CPB = TM // CHUNK
G_CHUNK = 256
G_CPB = TM // G_CHUNK
NEG = -1e30
N_BIG = 8192
CB_GQK, CB_GV, CB_GR, CB_MQK, CB_GM, CB_GG, CB_MV, CB_MO = range(8)
N_SMALL = 128
N_ALL = N_BIG + N_SMALL
PROJ_TK = N_ALL // 5
VMEM_LIMIT = 56 * 1024 * 1024

ADAM_LR, ADAM_B1, ADAM_B2, ADAM_EPS, ADAM_WD, ADAM_STEP = 0.001, 0.9, 0.999, 1e-08, 0.01, 10

NT_DIMS = (((1,), (1,)), ((), ()))
TN_DIMS = (((0,), (0,)), ((), ()))


def _nt(a, b, **kw):
    return lax.dot_general(a, b, NT_DIMS, preferred_element_type=F32, **kw)


def _tn(a, b, **kw):
    return lax.dot_general(a, b, TN_DIMS, preferred_element_type=F32, **kw)


def _nn(a, b, **kw):
    return jnp.dot(a, b, preferred_element_type=F32, **kw)


def _params(**kw):
    return pltpu.CompilerParams(vmem_limit_bytes=VMEM_LIMIT, **kw)


def _sigmoid(x):
    return 0.5 * jnp.tanh(0.5 * x) + 0.5


def _logsig(x):
    return jnp.minimum(x, 0.0) - jnp.log(1.0 + jnp.exp(-jnp.abs(x)))


def _mm_rows(rows):
    return 3 * TM if rows % (3 * TM) == 0 else TM


def _mm(a, b, *, nt, out_dtype, tn, tk=None, tm=None, name, rider=None, single_buffer_b=False):
    m, k = a.shape
    n = b.shape[0] if nt else b.shape[1]
    tk = k if tk is None else tk
    tm = _mm_rows(m) if tm is None else tm
    nk = k // tk
    nj, ni = n // tn, m // tm
    nr_in = len(rider["inputs"]) if rider else 0
    nr_out = len(rider["out_shapes"]) if rider else 0
    assert m % tm == 0 and n % tn == 0 and k % tk == 0
    dims = NT_DIMS if nt else (((1,), (0,)), ((), ()))

    def body(*refs):
        a_ref, b_ref = refs[:2]
        o_ref = refs[2 + nr_in]
        j, i, kk = pl.program_id(0), pl.program_id(1), pl.program_id(2)
        step = (j * ni + i) * nk + kk
        if rider:
            start, middle, finish = rider["make"](refs[2:2 + nr_in], refs[3 + nr_in:3 + nr_in + nr_out],
                                                  refs[3 + nr_in + nr_out:5 + nr_in + nr_out])
            pl.when(step == 0)(start)
            pl.when(step == (nj * ni * nk) // 2)(middle)

        part = lax.dot_general(a_ref[...].astype(BF), b_ref[...].astype(BF), dims, preferred_element_type=F32)
        if nk == 1:
            o_ref[...] = part.astype(o_ref.dtype)
        else:
            acc_ref = refs[-1]

            @pl.when(kk == 0)
            def _():
                acc_ref[...] = part

            @pl.when(jnp.logical_and(kk > 0, kk < nk - 1))
            def _():
                acc_ref[...] += part

            @pl.when(kk == nk - 1)
            def _():
                o_ref[...] = (acc_ref[...] + part).astype(o_ref.dtype)

        if rider:
            pl.when(step == nj * ni * nk - 1)(finish)

    outs = pl.pallas_call(
        body, grid=(nj, ni, nk),
        in_specs=[pl.BlockSpec((tm, tk), lambda j, i, kk: (i, kk)),
                  pl.BlockSpec((tn, tk) if nt else (tk, tn), (lambda j, i, kk: (j, kk)) if nt else (lambda j, i, kk: (kk, j)),
                               pipeline_mode=pl.Buffered(1) if single_buffer_b else None)]
                 + [ANY] * nr_in,
        out_specs=[pl.BlockSpec((tm, tn), lambda j, i, kk: (i, j))] + [ANY] * nr_out,
        out_shape=(jax.ShapeDtypeStruct((m, n), out_dtype),) + (tuple(rider["out_shapes"]) if rider else ()),
        scratch_shapes=(rider["sems"] if rider else []) + ([pltpu.VMEM((tm, tn), F32)] if nk > 1 else []),
        compiler_params=_params(), name=name)(a, b, *(rider["inputs"] if rider else []))
    return (outs[0], list(outs[1:])) if rider else outs[0]


def _mm_tn(a, b, *, tm, tn, tk=None, name):
    t, m = a.shape
    n = b.shape[1]
    tk = _mm_rows(t) if tk is None else tk
    assert t % tk == 0 and m % tm == 0 and n % tn == 0

    def body(a_ref, b_ref, o_ref):
        part = _tn(a_ref[...].astype(BF), b_ref[...].astype(BF))

        @pl.when(pl.program_id(2) == 0)
        def _():
            o_ref[...] = part

        @pl.when(pl.program_id(2) > 0)
        def _():
            o_ref[...] += part

    return pl.pallas_call(
        body, grid=(m // tm, n // tn, t // tk),
        in_specs=[pl.BlockSpec((tk, tm), lambda i, j, kk: (kk, i)), pl.BlockSpec((tk, tn), lambda i, j, kk: (kk, j))],
        out_specs=pl.BlockSpec((tm, tn), lambda i, j, kk: (i, j)),
        out_shape=jax.ShapeDtypeStruct((m, n), F32), compiler_params=_params(), name=name)(a, b)


ANY = pl.BlockSpec(memory_space=pl.ANY)


def _row_spec(width, col=0):
    return pl.BlockSpec((TM, width), lambda i: (i, col))


def _full_spec(shape):
    return pl.BlockSpec(shape, lambda i: (0,) * len(shape))


def _embed_norm(x0, first, g, rider, name):
    tp = x0.shape[0] + TM
    nb = tp // TM
    nri, nro = len(rider["inputs"]), len(rider["out_shapes"])

    def body(*refs):
        x_ref, f_ref, g_ref = refs[:3]
        h_ref, xn_ref, r_ref = refs[3 + nri:6 + nri]
        i = pl.program_id(0)
        start, middle, finish = rider["make"](refs[3:3 + nri], refs[6 + nri:6 + nri + nro], refs[6 + nri + nro:])
        pl.when(i == 0)(start)
        pl.when(i == nb // 2)(middle)
        x = jnp.where(i == 0, f_ref[...], x_ref[...])
        r = lax.rsqrt(jnp.mean(x * x, axis=1, keepdims=True) + EPS)
        h_ref[...] = x
        xn_ref[...] = (x * r * g_ref[...]).astype(BF)
        r_ref[...] = r
        pl.when(i == nb - 1)(finish)

    outs = pl.pallas_call(
        body, grid=(nb,),
        in_specs=[pl.BlockSpec((TM, D), lambda i: (jnp.maximum(i - 1, 0), 0)), _full_spec((TM, D)), _full_spec((1, D))] + [ANY] * nri,
        out_specs=[_row_spec(D), _row_spec(D), _row_spec(1)] + [ANY] * nro,
        out_shape=(jax.ShapeDtypeStruct((tp, D), F32), jax.ShapeDtypeStruct((tp, D), BF), jax.ShapeDtypeStruct((tp, 1), F32))
                  + tuple(rider["out_shapes"]),
        scratch_shapes=rider["sems"], compiler_params=_params(), name=name)(x0, first, g, *rider["inputs"])
    return outs[0], outs[1], outs[2], list(outs[3:])


def _rms_bwd(dxn, h, rstd, g, dres, name, split_first=False):
    tp = h.shape[0]

    def body(dxn_ref, h_ref, r_ref, g_ref, dres_ref, *outs):
        r = r_ref[...]
        xh = h_ref[...] * r
        dxn_v = dxn_ref[...].astype(F32)
        dxh = dxn_v * g_ref[...]
        dh = r * (dxh - xh * jnp.mean(dxh * xh, axis=1, keepdims=True)) + dres_ref[...]
        if split_first:
            first_ref, dh_ref, dg_ref = outs

            @pl.when(pl.program_id(0) == 0)
            def _():
                first_ref[...] = dh
        else:
            dh_ref, dg_ref = outs
        dh_ref[...] = dh
        part = jnp.sum(dxn_v * xh, axis=0, keepdims=True)

        @pl.when(pl.program_id(0) == 0)
        def _():
            dg_ref[...] = part

        @pl.when(pl.program_id(0) > 0)
        def _():
            dg_ref[...] += part

    if split_first:
        out_specs = [_full_spec((TM, D)), pl.BlockSpec((TM, D), lambda i: (jnp.maximum(i - 1, 0), 0)), _full_spec((1, D))]
        out_shape = (jax.ShapeDtypeStruct((TM, D), F32), jax.ShapeDtypeStruct((tp - TM, D), F32), jax.ShapeDtypeStruct((1, D), F32))
    else:
        out_specs = [_row_spec(D), _full_spec((1, D))]
        out_shape = (jax.ShapeDtypeStruct((tp, D), F32), jax.ShapeDtypeStruct((1, D), F32))
    return pl.pallas_call(
        body, grid=(tp // TM,),
        in_specs=[_row_spec(D), _row_spec(D), _row_spec(1), _full_spec((1, D)), _row_spec(D)],
        out_specs=out_specs, out_shape=out_shape, compiler_params=_params(), name=name)(dxn, h, rstd, g, dres)


def _shift_down(x, halo, k):
    rk = pltpu.roll(x, k, 0)
    io = lax.broadcasted_iota(jnp.int32, (8, x.shape[1]), 0)
    top = jnp.where(io < k, pltpu.roll(halo, k, 0), rk[0:8])
    return top if x.shape[0] == 8 else jnp.concatenate([top, rk[8:]], axis=0)


def _shift_up(x, nxt, k):
    n = x.shape[0]
    rk = pltpu.roll(x, n - k, 0)
    io = lax.broadcasted_iota(jnp.int32, (8, x.shape[1]), 0)
    bot = jnp.where(io >= 8 - k, pltpu.roll(nxt, 8 - k, 0), rk[n - 8:n])
    return jnp.concatenate([rk[:n - 8], bot], axis=0)


def _conv_pre(x, halo, w_ref, b_ref):
    c = x * w_ref[3:4, :] + b_ref[...]
    shifted = []
    for k in (1, 2, 3):
        s = _shift_down(x, halo, k)
        shifted.append(s)
        c = c + s * w_ref[3 - k:4 - k, :]
    return c, shifted


def _qk_scale():
    col = lax.broadcasted_iota(jnp.int32, (1, D), 1)
    return jnp.where(col < NH * DQK, DQK ** -0.5, 1.0).astype(F32)


def _halo_prev_spec():
    return pl.BlockSpec((8, D), lambda i: (jnp.maximum(i * (TM // 8) - 1, 0), CB_MQK))


def _conv_fwd(pbig, w, b, name):
    tp = pbig.shape[0]

    def body(x_ref, halo_ref, w_ref, b_ref, o_ref):
        x = x_ref[...].astype(F32)
        halo = jnp.where(pl.program_id(0) > 0, halo_ref[...].astype(F32), 0.0)
        c, _ = _conv_pre(x, halo, w_ref, b_ref)
        o_ref[...] = (c * _sigmoid(c) * _qk_scale()).astype(BF)

    return pl.pallas_call(
        body, grid=(tp // TM,),
        in_specs=[_row_spec(D, CB_MQK), _halo_prev_spec(), _full_spec((4, D)), _full_spec((1, D))],
        out_specs=_row_spec(D), out_shape=jax.ShapeDtypeStruct((tp, D), BF),
        compiler_params=_params(), name=name)(pbig, pbig, w, b)


def _conv_bwd(dqk, pbig, w, b, dproj, name):
    tp = pbig.shape[0]
    nb = tp // TM

    def d_conv_out(d, x, halo, w_ref, b_ref):
        c, shifted = _conv_pre(x, halo, w_ref, b_ref)
        sg = _sigmoid(c)
        return d * _qk_scale() * (sg * (1.0 + c * (1.0 - sg))), shifted

    def body(d_ref, dn_ref, x_ref, halo_ref, xn_ref, w_ref, b_ref, _, o_ref, dwb_ref):
        i = pl.program_id(0)
        x = x_ref[...].astype(F32)
        halo = jnp.where(i > 0, halo_ref[...].astype(F32), 0.0)
        dc, shifted = d_conv_out(d_ref[...], x, halo, w_ref, b_ref)
        dc_next, _ = d_conv_out(dn_ref[...], xn_ref[...].astype(F32), x[TM - 8:TM], w_ref, b_ref)
        nxt = jnp.where(i < nb - 1, dc_next, 0.0)
        acc = dc * w_ref[3:4, :]
        for k in (1, 2, 3):
            acc = acc + _shift_up(dc, nxt, k) * w_ref[3 - k:4 - k, :]
        o_ref[...] = acc.astype(BF)
        taps = [shifted[2], shifted[1], shifted[0], x]
        rows = [jnp.sum(dc * t, axis=0, keepdims=True) for t in taps] + [jnp.sum(dc, axis=0, keepdims=True)]
        io = lax.broadcasted_iota(jnp.int32, (8, D), 0)
        part = jnp.zeros((8, D), F32)
        for r, v in enumerate(rows):
            part = jnp.where(io == r, v, part)

        @pl.when(pl.program_id(0) == 0)
        def _():
            dwb_ref[...] = part

        @pl.when(pl.program_id(0) > 0)
        def _():
            dwb_ref[...] += part

    next8 = lambda col: pl.BlockSpec((8, D), lambda i: (jnp.minimum((i + 1) * (TM // 8), tp // 8 - 1), col))
    return pl.pallas_call(
        body, grid=(nb,),
        in_specs=[_row_spec(D), next8(0), _row_spec(D, CB_MQK), _halo_prev_spec(), next8(CB_MQK),
                  _full_spec((4, D)), _full_spec((1, D)), ANY],
        out_specs=[_row_spec(D, CB_MQK), _full_spec((8, D))],
        out_shape=(jax.ShapeDtypeStruct(dproj.shape, BF), jax.ShapeDtypeStruct((8, D), F32)),
        input_output_aliases={7: 0}, compiler_params=_params(), name=name)(dqk, dqk, pbig, pbig, pbig, w, b, dproj)


def _mm_fused(inputs, products, *, nt, m, n, tm, tn, outs, epilogue, name, nk=1, sub=None):
    dims = NT_DIMS if nt else (((1,), (0,)), ((), ()))
    nin = len(inputs)
    assert nk == 1 or (len(products) == 1 and sub is None)

    def body(*refs):
        in_refs, out_refs = refs[:nin], refs[nin:nin + len(outs)]
        i = pl.program_id(1)
        if sub is not None:
            lhs = {ia: in_refs[ia][...].astype(BF) for ia, _ in products}

            def dots(cols):
                return [lax.dot_general(lhs[ia], (in_refs[ib][cols, :] if nt else in_refs[ib][:, cols]).astype(BF),
                                        dims, preferred_element_type=F32) for ia, ib in products]

            slices = [slice(s, min(s + sub, tn)) for s in range(0, tn, sub)]
            prods = dots(slices[0])
            for idx, cols in enumerate(slices):
                nxt = dots(slices[idx + 1]) if idx + 1 < len(slices) else None
                epilogue(prods, in_refs, out_refs, i, cols)
                prods = nxt
            return
        prods = [lax.dot_general(in_refs[ia][...].astype(BF), in_refs[ib][...].astype(BF), dims, preferred_element_type=F32)
                 for ia, ib in products]
        if nk == 1:
            epilogue(prods, in_refs, out_refs, i, slice(None))
            return
        acc_ref = refs[-1]
        kk = pl.program_id(2)

        @pl.when(kk == 0)
        def _():
            acc_ref[...] = prods[0]

        @pl.when(jnp.logical_and(kk > 0, kk < nk - 1))
        def _():
            acc_ref[...] += prods[0]

        @pl.when(kk == nk - 1)
        def _():
            epilogue([acc_ref[...] + prods[0]], in_refs, out_refs, i, slice(None))

    return pl.pallas_call(
        body, grid=(n // tn, m // tm, nk), in_specs=[s for _, s in inputs], out_specs=[s for _, s in outs],
        out_shape=tuple(sh for sh, _ in outs), scratch_shapes=[pltpu.VMEM((tm, tn), F32)] if nk > 1 else [],
        compiler_params=_params(), name=name)(*[a for a, _ in inputs])


SUB_COLS = 256


def _cols_at(cols, offset):
    return slice(cols.start + offset, cols.stop + offset)


def _blk(rows, width, col=None, row=None):
    return pl.BlockSpec((rows, width), lambda j, i, kk: ((i if row is None else row(i)), (0 if col is None else col(j, kk))))


FF_TN = D_FF // 2


def _ffn_weight_rows(wg_t, wu_t):
    return jnp.concatenate([wg_t[0:FF_TN], wu_t[0:FF_TN], wg_t[FF_TN:], wu_t[FF_TN:]], axis=0)


def _ffn_in(hn, wgu_t, name):
    tp = hn.shape[0]
    tm = _mm_rows(tp)

    def epilogue(prods, in_refs, out_refs, i, cols):
        g, u = prods
        out_refs[0][:, cols] = g.astype(BF)
        out_refs[0][:, _cols_at(cols, FF_TN)] = u.astype(BF)
        out_refs[1][:, cols] = (g * _sigmoid(g) * u).astype(BF)

    wspec = lambda off: pl.BlockSpec((FF_TN, D), lambda j, i, kk: (2 * j + off, 0))
    return _mm_fused(
        [(hn, _blk(tm, D)), (wgu_t, wspec(0)), (wgu_t, wspec(1))], [(0, 1), (0, 2)], nt=True, m=tp, n=D_FF, tm=tm, tn=FF_TN,
        outs=[(jax.ShapeDtypeStruct((tp, 2 * D_FF), BF), _blk(tm, 2 * FF_TN, lambda j, kk: j)),
              (jax.ShapeDtypeStruct((tp, D_FF), BF), _blk(tm, FF_TN, lambda j, kk: j))],
        epilogue=epilogue, name=name, sub=SUB_COLS)


def _ffn_down_loss(ff, wdown, h1, target, gf, name):
    tp = ff.shape[0]

    def epilogue(prods, in_refs, out_refs, i, cols):
        live = (i > 0).astype(F32)
        g = in_refs[4][...]
        x = prods[0] + in_refs[2][...]
        r = lax.rsqrt(jnp.mean(x * x, axis=1, keepdims=True) + EPS)
        xh = x * r
        e = xh * g - in_refs[3][...]
        loss_part = 0.5 * live * jnp.sum(jnp.mean(e * e, axis=1, keepdims=True), axis=0, keepdims=True)
        dout = e * (live / D)
        dg_part = jnp.sum(dout * xh, axis=0, keepdims=True)
        dxh = dout * g
        out_refs[0][...] = r * (dxh - xh * jnp.mean(dxh * xh, axis=1, keepdims=True))

        @pl.when(i == 0)
        def _():
            out_refs[1][...] = loss_part
            out_refs[2][...] = dg_part

        @pl.when(i > 0)
        def _():
            out_refs[1][...] += loss_part
            out_refs[2][...] += dg_part

    const = lambda shape: pl.BlockSpec(shape, lambda j, i, kk: (0,) * len(shape))
    return _mm_fused(
        [(ff, _blk(TM, D_FF)), (wdown, const((D_FF, D))), (h1, _blk(TM, D)),
         (target, _blk(TM, D, row=lambda i: jnp.maximum(i - 1, 0))), (gf, const((1, D)))],
        [(0, 1)], nt=False, m=tp, n=D, tm=TM, tn=D,
        outs=[(jax.ShapeDtypeStruct((tp, D), F32), _blk(TM, D)), (jax.ShapeDtypeStruct((1, 1), F32), const((1, 1))),
              (jax.ShapeDtypeStruct((1, D), F32), const((1, D)))],
        epilogue=epilogue, name=name)


def _ffn_d_hidden(dh2, wdown, gu, name):
    tp = dh2.shape[0]

    def epilogue(prods, in_refs, out_refs, i, cols):
        d = prods[0]
        g = in_refs[2][:, cols].astype(F32)
        u = in_refs[2][:, _cols_at(cols, FF_TN)].astype(F32)
        sg = _sigmoid(g)
        out_refs[0][:, cols] = (d * u * sg * (1.0 + g * (1.0 - sg))).astype(BF)
        out_refs[0][:, _cols_at(cols, FF_TN)] = (d * g * sg).astype(BF)

    return _mm_fused(
        [(dh2, _blk(TM, D)), (wdown, pl.BlockSpec((FF_TN, D), lambda j, i, kk: (j, 0))), (gu, _blk(TM, 2 * FF_TN, lambda j, kk: j))],
        [(0, 1)], nt=True, m=tp, n=D_FF, tm=TM, tn=FF_TN,
        outs=[(jax.ShapeDtypeStruct((tp, 2 * D_FF), BF), _blk(TM, 2 * FF_TN, lambda j, kk: j))],
        epilogue=epilogue, name=name, sub=SUB_COLS)[0]


def _ffn_d_in(dgu, wgu_t, h1, rstd, g2, dh2, name):
    tp = dgu.shape[0]

    def epilogue(prods, in_refs, out_refs, i, cols):
        r = in_refs[3][...]
        xh = in_refs[2][...] * r
        dxn = prods[0]
        dxh = dxn * in_refs[4][...]
        out_refs[0][...] = r * (dxh - xh * jnp.mean(dxh * xh, axis=1, keepdims=True)) + in_refs[5][...]
        part = jnp.sum(dxn * xh, axis=0, keepdims=True)

        @pl.when(i == 0)
        def _():
            out_refs[1][...] = part

        @pl.when(i > 0)
        def _():
            out_refs[1][...] += part

    const = lambda shape: pl.BlockSpec(shape, lambda j, i, kk: (0,) * len(shape))
    return _mm_fused(
        [(dgu, _blk(TM, 2 * D_FF)), (wgu_t, const((2 * D_FF, D))),
         (h1, _blk(TM, D)), (rstd, _blk(TM, 1)), (g2, const((1, D))), (dh2, _blk(TM, D))],
        [(0, 1)], nt=False, m=tp, n=D, tm=TM, tn=D,
        outs=[(jax.ShapeDtypeStruct((tp, D), F32), _blk(TM, D)), (jax.ShapeDtypeStruct((1, D), F32), const((1, D)))],
        epilogue=epilogue, name=name)


def _branch_merge(y_m, y_g, wbm, wbg, pbig, name):
    tp = y_m.shape[0]

    def epilogue(prods, in_refs, out_refs, i, cols):
        pm, pg = prods[0].astype(BF), prods[1].astype(BF)
        out_refs[0][:, cols] = pm
        out_refs[1][:, cols] = pg
        out_refs[2][:, cols] = (_sigmoid(in_refs[4][:, cols].astype(F32)) * pm.astype(F32)
                                + _sigmoid(in_refs[5][:, cols].astype(F32)) * pg.astype(F32)).astype(BF)

    const = lambda shape: pl.BlockSpec(shape, lambda j, i, kk: (0,) * len(shape))
    shp = jax.ShapeDtypeStruct((tp, D), BF)
    return _mm_fused(
        [(y_m, _blk(TM, D)), (wbm, const((D, D))), (y_g, _blk(TM, D)), (wbg, const((D, D))),
         (pbig, _blk(TM, D, lambda j, kk: CB_GM)), (pbig, _blk(TM, D, lambda j, kk: CB_GG))],
        [(0, 1), (2, 3)], nt=False, m=tp, n=D, tm=TM, tn=D,
        outs=[(shp, _blk(TM, D)), (shp, _blk(TM, D)), (shp, _blk(TM, D))], epilogue=epilogue, name=name, sub=SUB_COLS)


def _merge_d(dh1, wout, pm, pg, pbig, name):
    tp = dh1.shape[0]

    def epilogue(prods, in_refs, out_refs, i, cols):
        d = prods[0]
        sm = _sigmoid(in_refs[4][:, cols].astype(F32))
        sg = _sigmoid(in_refs[5][:, cols].astype(F32))
        out_refs[0][:, cols] = (d * sm).astype(BF)
        out_refs[1][:, cols] = (d * sg).astype(BF)
        out_refs[2][:, cols] = (d * in_refs[2][:, cols].astype(F32) * sm * (1.0 - sm)).astype(BF)
        out_refs[2][:, _cols_at(cols, D)] = (d * in_refs[3][:, cols].astype(F32) * sg * (1.0 - sg)).astype(BF)

    const = lambda shape: pl.BlockSpec(shape, lambda j, i, kk: (0,) * len(shape))
    shp = jax.ShapeDtypeStruct((tp, D), BF)
    return _mm_fused(
        [(dh1, _blk(TM, D)), (wout, const((D, D))), (pm, _blk(TM, D)), (pg, _blk(TM, D)),
         (pbig, _blk(TM, D, lambda j, kk: CB_GM)), (pbig, _blk(TM, D, lambda j, kk: CB_GG))],
        [(0, 1)], nt=True, m=tp, n=D, tm=TM, tn=D,
        outs=[(shp, _blk(TM, D)), (shp, _blk(TM, D)),
              (jax.ShapeDtypeStruct((tp, N_ALL), BF), _blk(TM, 2 * D, lambda j, kk: CB_GM // 2))],
        epilogue=epilogue, name=name, sub=SUB_COLS)


def _out_proj_norm(merged, wout, h0, g2, name):
    tp = merged.shape[0]
    tm = _mm_rows(tp)

    def epilogue(prods, in_refs, out_refs, i, cols):
        x = prods[0] + in_refs[2][...]
        r = lax.rsqrt(jnp.mean(x * x, axis=1, keepdims=True) + EPS)
        out_refs[0][...] = x
        out_refs[1][...] = (x * r * in_refs[3][...]).astype(BF)
        out_refs[2][...] = r

    const = lambda shape: pl.BlockSpec(shape, lambda j, i, kk: (0,) * len(shape))
    return _mm_fused(
        [(merged, _blk(tm, D)), (wout, const((D, D))), (h0, _blk(tm, D)), (g2, const((1, D)))],
        [(0, 1)], nt=False, m=tp, n=D, tm=tm, tn=D,
        outs=[(jax.ShapeDtypeStruct((tp, D), F32), _blk(tm, D)), (jax.ShapeDtypeStruct((tp, D), BF), _blk(tm, D)),
              (jax.ShapeDtypeStruct((tp, 1), F32), _blk(tm, 1))],
        epilogue=epilogue, name=name)


def _adamw(w, g, m, v, name):
    rows, cols = w.shape
    by_cols = rows % 128 != 0 and cols % 128 == 0 and rows * cols > 128 * 1024
    tr = rows if (by_cols or rows % 128 != 0) else 128
    tc = 128 if by_cols else cols

    def body(w_ref, g_ref, m_ref, v_ref, d_ref, nm_ref, nv_ref):
        gv = g_ref[...]
        nm = ADAM_B1 * m_ref[...] + (1.0 - ADAM_B1) * gv
        nv = ADAM_B2 * v_ref[...] + (1.0 - ADAM_B2) * (gv * gv)
        m_hat = nm / (1.0 - ADAM_B1 ** ADAM_STEP)
        v_hat = nv / (1.0 - ADAM_B2 ** ADAM_STEP)
        d_ref[...] = -ADAM_LR * (m_hat / (jnp.sqrt(v_hat) + ADAM_EPS) + ADAM_WD * w_ref[...])
        nm_ref[...] = nm
        nv_ref[...] = nv

    spec = pl.BlockSpec((tr, tc), (lambda i: (0, i)) if by_cols else (lambda i: (i, 0)))
    shp = jax.ShapeDtypeStruct((rows, cols), F32)
    return pl.pallas_call(body, grid=(cols // tc if by_cols else rows // tr,), in_specs=[spec] * 4, out_specs=[spec] * 3,
                          out_shape=(shp,) * 3, compiler_params=_params(), name=name)(w, g, m, v)


def _place_small(dsmall, dproj, name):
    tp = dsmall.shape[0]

    def body(s_ref, _, o_ref):
        o_ref[...] = s_ref[...]

    return pl.pallas_call(
        body, grid=(tp // TM,), in_specs=[_row_spec(N_SMALL), ANY], out_specs=_row_spec(N_SMALL, N_BIG // N_SMALL),
        out_shape=jax.ShapeDtypeStruct(dproj.shape, dproj.dtype), input_output_aliases={1: 0},
        compiler_params=_params(), name=name)(dsmall, dproj)


def _row_tile(rows, cap=512):
    best = rows
    for cand in range(8, min(rows, cap) + 1, 8):
        if rows % cand == 0:
            best = cand
    return best


def _add2(a, b, out_dtype, name):
    rows, cols = a.shape
    tr = _row_tile(rows)

    def body(a_ref, b_ref, o_ref):
        o_ref[...] = (a_ref[...] + b_ref[...]).astype(o_ref.dtype)

    spec = pl.BlockSpec((tr, cols), lambda i: (i, 0))
    return pl.pallas_call(body, grid=(rows // tr,), in_specs=[spec] * 2, out_specs=spec,
                          out_shape=jax.ShapeDtypeStruct((rows, cols), out_dtype), compiler_params=_params(), name=name)(a, b)


def _add4(first, rest, name):
    rows, cols = first.shape
    tr = _row_tile(rows, 256)

    def body(f_ref, r_ref, o_ref):
        up = lambda v: v.astype(F32)
        o_ref[...] = ((up(f_ref[...]) + up(r_ref[0])) + up(r_ref[1])) + up(r_ref[2])

    return pl.pallas_call(body, grid=(rows // tr,),
                          in_specs=[pl.BlockSpec((tr, cols), lambda i: (i, 0)), pl.BlockSpec((3, tr, cols), lambda i: (0, i, 0))],
                          out_specs=pl.BlockSpec((tr, cols), lambda i: (i, 0)),
                          out_shape=jax.ShapeDtypeStruct((rows, cols), F32), compiler_params=_params(), name=name)(first, rest)


def _chunk_consts(length=CHUNK):
    r2 = lax.broadcasted_iota(jnp.int32, (length, length), 0)
    c2 = lax.broadcasted_iota(jnp.int32, (length, length), 1)
    tri = r2 >= c2
    return dict(tri=tri, tril_f=tri.astype(F32), triu_f=(r2 <= c2).astype(F32),
                lane=lax.broadcasted_iota(jnp.int32, (length, N_SMALL), 1),
                rowio=lax.broadcasted_iota(jnp.int32, (length, 1), 0),
                ones=jnp.ones((length, N_SMALL), F32))


def _valid_rows(block, c):
    row = block * TM + c * CHUNK + lax.broadcasted_iota(jnp.int32, (CHUNK, 1), 0)
    return row >= FIRST_VALID


def _col(x, lane, idx):
    return jnp.sum(jnp.where(lane == idx, x, 0.0), axis=1, keepdims=True)


def _last_row(x, rowio):
    return jnp.sum(jnp.where(rowio == rowio.shape[0] - 1, x, 0.0), axis=0, keepdims=True)


def _sum_all(x):
    return jnp.sum(jnp.sum(x, axis=1, keepdims=True), axis=0, keepdims=True)


def _headnorm_fwd(hm, gain, gate_act):
    rs = lax.rsqrt(jnp.mean(hm * hm, axis=1, keepdims=True) + EPS)
    return hm * rs * gain * gate_act


def _headnorm_bwd(dy, hm, gain, gate_act):
    rs = lax.rsqrt(jnp.mean(hm * hm, axis=1, keepdims=True) + EPS)
    xh = hm * rs
    dact = dy * xh * gain
    dgain = jnp.sum(dy * gate_act * xh, axis=0, keepdims=True)
    dxh = dy * gate_act * gain
    dhm = rs * (dxh - xh * jnp.mean(dxh * xh, axis=1, keepdims=True))
    return dhm, dact, dgain


def _mlstm_gates(sm, gbias, valid, k):
    pre = sm + gbias
    lf = jnp.where(valid, _logsig(pre), 0.0)
    b_all = _nn(k["tril_f"], lf, precision=HI)
    li_all = jnp.where(valid, pre, NEG)
    return pre, li_all, b_all


def _mlstm_open(h, qh, kh, c_st, li_all, b_all, k):
    lane = k["lane"]
    sel = jnp.where(lane == h, 1.0, 0.0) - jnp.where(lane == NH + h, 1.0, 0.0)
    x = jnp.where(lane < NH, li_all, jnp.where(lane < 2 * NH, b_all, 0.0))
    cb = c_st.astype(BF)
    return dict(ubc=_nt(sel, x, precision=HI), sim=_nt(qh, kh), cb=cb, cq=_nt(qh, cb))


def _mlstm_weights(h, f, qh, vh, li_all, b_all, n_row, m11, k):
    lane, tri, rowio = k["lane"], k["tri"], k["rowio"]
    b_col = _col(b_all, lane, NH + h)
    li_col = _col(li_all, lane, h)
    dmat = jnp.where(tri, b_col + f["ubc"], NEG)
    m_row = jnp.maximum(b_col + m11, jnp.max(dmat, axis=1, keepdims=True))
    e = jnp.exp(dmat - m_row)
    w_mat = e * f["sim"]
    a = jnp.exp(b_col + m11 - m_row)
    qf = qh.astype(F32)
    nq = jnp.sum(qf * n_row, axis=1, keepdims=True)
    g = _last_row(b_col, rowio)
    wlog = g - b_col + li_col
    m_new = jnp.maximum(g + m11, jnp.max(wlog, axis=0, keepdims=True))
    a_s = jnp.exp(g + m11 - m_new)
    w = jnp.exp(wlog - m_new)
    return dict(f, e=e, w_mat=w_mat, a=a, qf=qf, nq=nq, m_row=m_row, m_new=m_new, a_s=a_s, w=w,
                wv=_nn(w_mat.astype(BF), vh))


def _mlstm_out(f):
    num = f["a"] * f["cq"] + f["wv"]
    den = f["a"] * f["nq"] + jnp.sum(f["w_mat"], axis=1, keepdims=True)
    floor = jnp.exp(-f["m_row"])
    r = jnp.maximum(jnp.abs(den), floor)
    return dict(f, den=den, floor=floor, r=r, hm=num / r)


def _mlstm_fwd(qk, pbig, small, gbias, headg, name):
    tp = qk.shape[0]
    nb = tp // TM

    def body(qk_ref, v_ref, mo_ref, sm_ref, gb_ref, hg_ref, y_ref, cs_ref, ns_ref, c_scr, n_scr):
        blk = pl.program_id(0)

        @pl.when(blk == 0)
        def _():
            c_scr[...] = jnp.zeros_like(c_scr)
            n_scr[...] = jnp.zeros_like(n_scr)

        k = _chunk_consts()
        io8 = lax.broadcasted_iota(jnp.int32, (8, DQK), 0)

        def chunk(c, carry):
            r0 = pl.multiple_of(c * CHUNK, CHUNK)
            rows = pl.ds(r0, CHUNK)
            valid = _valid_rows(blk, c)
            _, li_all, b_all = _mlstm_gates(sm_ref[rows, :], gb_ref[...], valid, k)
            heads = range(NH)
            qs = [qk_ref[rows, h * DQK:(h + 1) * DQK] for h in heads]
            ks = [qk_ref[rows, NH * DQK + h * DQK:NH * DQK + (h + 1) * DQK] for h in heads]
            vs = [v_ref[rows, h * DV:(h + 1) * DV] for h in heads]
            cst = [c_scr[h] for h in heads]
            nrow = [n_scr[h, 0:1, :] for h in heads]
            m11 = [jnp.max(n_scr[h, 1:2, :], axis=1, keepdims=True) for h in heads]
            f = [_mlstm_open(h, qs[h], ks[h], cst[h], li_all, b_all, k) for h in heads]
            f = [_mlstm_weights(h, f[h], qs[h], vs[h], li_all, b_all, nrow[h], m11[h], k) for h in heads]
            wk = [f[h]["w"] * ks[h].astype(F32) for h in heads]
            kv = [_tn(vs[h], wk[h].astype(BF)) for h in heads]
            for h in heads:
                hm = _mlstm_out(f[h])["hm"]
                gate = _sigmoid(mo_ref[rows, h * DV:(h + 1) * DV].astype(F32))
                y_ref[rows, h * DV:(h + 1) * DV] = _headnorm_fwd(hm, hg_ref[:, h * DV:(h + 1) * DV], gate).astype(BF)
                cs_ref[c, h] = f[h]["cb"]
                ns_ref[c, h] = jnp.where(io8 == 0, nrow[h], jnp.where(io8 == 1, m11[h], 0.0))
                c_scr[h] = f[h]["a_s"] * cst[h] + kv[h]
                n_scr[h, 0:1, :] = f[h]["a_s"] * nrow[h] + jnp.sum(wk[h], axis=0, keepdims=True)
                n_scr[h, 1:2, :] = jnp.broadcast_to(f[h]["m_new"], (1, DQK))
            return carry

        lax.fori_loop(0, CPB, chunk, 0, unroll=2)

    return pl.pallas_call(
        body, grid=(nb,),
        in_specs=[_row_spec(D), _row_spec(D, CB_MV), _row_spec(D, CB_MO), _row_spec(N_SMALL), _full_spec((1, N_SMALL)), _full_spec((1, D))],
        out_specs=[_row_spec(D), pl.BlockSpec((CPB, NH, DV, DQK), lambda i: (i, 0, 0, 0)),
                   pl.BlockSpec((CPB, NH, 8, DQK), lambda i: (i, 0, 0, 0))],
        out_shape=(jax.ShapeDtypeStruct((tp, D), BF), jax.ShapeDtypeStruct((tp // CHUNK, NH, DV, DQK), BF),
                   jax.ShapeDtypeStruct((tp // CHUNK, NH, 8, DQK), F32)),
        scratch_shapes=[pltpu.VMEM((NH, DV, DQK), F32), pltpu.VMEM((NH, 8, DQK), F32)],
        compiler_params=_params(), name=name)(qk, pbig, pbig, small, gbias, headg)


def _mlstm_bwd(dy, qk, pbig, small, gbias, headg, cs, ns, dproj, name, ride=()):
    tp = qk.shape[0]
    nb = tp // TM
    nr = len(ride)

    def body(*refs):
        dy_ref, qk_ref, v_ref, mo_ref, sm_ref, gb_ref, hg_ref, cs_ref, ns_ref = refs[:9]
        ride_in = refs[10:10 + nr]
        dqk_ref, dproj_ref, dsm_ref, dgb_ref, dhg_ref = refs[10 + nr:15 + nr]
        ride_out = refs[15 + nr:15 + 2 * nr]
        dc_scr, dn_scr = refs[15 + 2 * nr:17 + 2 * nr]
        step = pl.program_id(0)
        blk = nb - 1 - step
        sent = _scatter_copies(ride_in, ride_out, *refs[17 + 2 * nr:]) if nr else []

        @pl.when(step == 0)
        def _():
            dc_scr[...] = jnp.zeros_like(dc_scr)
            dn_scr[...] = jnp.zeros_like(dn_scr)
            dgb_ref[...] = jnp.zeros_like(dgb_ref)
            dhg_ref[...] = jnp.zeros_like(dhg_ref)
            for cp in sent:
                cp.start()

        k = _chunk_consts()
        lane, rowio = k["lane"], k["rowio"]

        def chunk(cc, carry):
            c = CPB - 1 - cc
            r0 = pl.multiple_of(c * CHUNK, CHUNK)
            rows = pl.ds(r0, CHUNK)
            valid = _valid_rows(blk, c)
            pre, li_all, b_all = _mlstm_gates(sm_ref[rows, :], gb_ref[...], valid, k)
            dli_all = jnp.zeros((CHUNK, N_SMALL), F32)
            db_all = jnp.zeros((CHUNK, N_SMALL), F32)
            heads = range(NH)
            qs = [qk_ref[rows, h * DQK:(h + 1) * DQK] for h in heads]
            ks = [qk_ref[rows, NH * DQK + h * DQK:NH * DQK + (h + 1) * DQK] for h in heads]
            vs = [v_ref[rows, h * DV:(h + 1) * DV] for h in heads]
            cst = [cs_ref[c, h].astype(F32) for h in heads]
            nrow = [ns_ref[c, h, 0:1, :] for h in heads]
            m11 = [jnp.max(ns_ref[c, h, 1:2, :], axis=1, keepdims=True) for h in heads]
            f = [_mlstm_open(h, qs[h], ks[h], cst[h], li_all, b_all, k) for h in heads]
            f = [_mlstm_weights(h, f[h], qs[h], vs[h], li_all, b_all, nrow[h], m11[h], k) for h in heads]
            f = [_mlstm_out(f[h]) for h in heads]
            t = []
            for h in heads:
                gain = hg_ref[:, h * DV:(h + 1) * DV]
                gate = _sigmoid(mo_ref[rows, h * DV:(h + 1) * DV].astype(F32))
                dhm, dgate, dgain = _headnorm_bwd(dy_ref[rows, h * DV:(h + 1) * DV].astype(F32), f[h]["hm"], gain, gate)
                dproj_ref[rows, D + h * DV:D + (h + 1) * DV] = (dgate * gate * (1.0 - gate)).astype(BF)
                dhg_ref[:, h * DV:(h + 1) * DV] += dgain
                r, den = f[h]["r"], f[h]["den"]
                dnum = dhm / r
                dr = -jnp.sum(dhm * f[h]["hm"], axis=1, keepdims=True) / r
                dden = jnp.where(jnp.abs(den) > f[h]["floor"], dr * jnp.sign(den), 0.0)
                dnb = dnum.astype(BF)
                dc_new = dc_scr[h]
                dcb = dc_new.astype(BF)
                t.append(dict(dnum=dnum, dden=dden, dnb=dnb, dc_new=dc_new, dn_new=dn_scr[h],
                              dwm=_nt(dnb, vs[h]), vdc=_nn(vs[h], dcb), kdc=_nt(ks[h], dcb)))
            for h in heads:
                dw_mat = t[h]["dwm"] + t[h]["dden"]
                dsim = (f[h]["e"] * dw_mat).astype(BF)
                gm = f[h]["w_mat"] * dw_mat
                t[h].update(gm=gm, dv0=_tn(f[h]["w_mat"].astype(BF), t[h]["dnb"]), dq0=_nn(dsim, ks[h]),
                            dq1=_nn(t[h]["dnb"], f[h]["cb"]), dk0=_tn(dsim, qs[h]),
                            dcq=_tn((f[h]["a"] * t[h]["dnum"]).astype(BF), qs[h]), cs2=_tn(gm, k["ones"], precision=HI))
            for h in heads:
                a, w, a_s = f[h]["a"], f[h]["w"], f[h]["a_s"]
                dnum, dden, dc_new, dn_new, vdc, gm = (t[h][n] for n in ("dnum", "dden", "dc_new", "dn_new", "vdc", "gm"))
                kf = ks[h].astype(F32)
                dproj_ref[rows, h * DV:(h + 1) * DV] = (t[h]["dv0"] + w * t[h]["kdc"]).astype(BF)
                adden = a * dden
                dqk_ref[rows, h * DQK:(h + 1) * DQK] = t[h]["dq0"] + a * t[h]["dq1"] + adden * nrow[h]
                dqk_ref[rows, NH * DQK + h * DQK:NH * DQK + (h + 1) * DQK] = t[h]["dk0"] + w * vdc + w * dn_new
                da = jnp.sum(dnum * f[h]["cq"], axis=1, keepdims=True) + dden * f[h]["nq"]
                dw = jnp.sum(vdc * kf, axis=1, keepdims=True) + jnp.sum(kf * dn_new, axis=1, keepdims=True)
                da_s = _sum_all(dc_new * cst[h]) + jnp.sum(dn_new * nrow[h], axis=1, keepdims=True)
                wdw = w * dw
                rs = jnp.sum(gm, axis=1, keepdims=True)
                cs_col = _col(t[h]["cs2"], lane, 0)
                dg = a_s * da_s + jnp.sum(wdw, axis=0, keepdims=True)
                db = a * da + rs - cs_col - wdw + jnp.where(rowio == CHUNK - 1, dg, 0.0)
                dli_all = dli_all + jnp.where(lane == h, cs_col + wdw, 0.0)
                db_all = db_all + jnp.where(lane == NH + h, db, 0.0)
                dc_scr[h] = a_s * dc_new + t[h]["dcq"]
                dn_scr[h] = a_s * dn_new + jnp.sum(adden * f[h]["qf"], axis=0, keepdims=True)
            dlf_all = _nn(k["triu_f"], db_all, precision=HI)
            dsm = jnp.where(valid, dli_all + dlf_all * _sigmoid(-pre), 0.0)
            dsm = jnp.where(lane < 2 * NH, dsm, 0.0)
            dsm_ref[rows, :] = dsm
            dgb_ref[0:1, :] += jnp.sum(dsm, axis=0, keepdims=True)
            return carry

        lax.fori_loop(0, CPB, chunk, 0, unroll=2)

        if nr:
            @pl.when(step == nb - 1)
            def _():
                for cp in sent:
                    cp.wait_recv()
                for cp in sent:
                    cp.wait_send()

    rev = lambda col: (lambda i: (nb - 1 - i, col))
    rspec = lambda width, col=0: pl.BlockSpec((TM, width), rev(col))
    ride_shapes, ride_sems = _scatter_shapes(ride) if nr else ((), [])
    outs = pl.pallas_call(
        body, grid=(nb,),
        in_specs=[rspec(D), rspec(D), rspec(D, CB_MV), rspec(D, CB_MO), rspec(N_SMALL), _full_spec((1, N_SMALL)), _full_spec((1, D)),
                  pl.BlockSpec((CPB, NH, DV, DQK), lambda i: (nb - 1 - i, 0, 0, 0)),
                  pl.BlockSpec((CPB, NH, 8, DQK), lambda i: (nb - 1 - i, 0, 0, 0)), ANY] + [ANY] * nr,
        out_specs=[rspec(D), rspec(2 * D, CB_MV // 2), rspec(N_SMALL), _full_spec((8, N_SMALL)), _full_spec((1, D))] + [ANY] * nr,
        out_shape=(jax.ShapeDtypeStruct((tp, D), F32), jax.ShapeDtypeStruct(dproj.shape, BF),
                   jax.ShapeDtypeStruct((tp, N_SMALL), F32), jax.ShapeDtypeStruct((8, N_SMALL), F32),
                   jax.ShapeDtypeStruct((1, D), F32)) + tuple(ride_shapes),
        scratch_shapes=[pltpu.VMEM((NH, DV, DQK), F32), pltpu.VMEM((NH, 1, DQK), F32)] + ride_sems,
        input_output_aliases={9: 1}, compiler_params=_params(), name=name)(dy, qk, pbig, pbig, small, gbias, headg, cs, ns, dproj, *ride)
    return tuple(outs[:5]) + (list(outs[5:]),)


def _gla_loga(sm_ref, a2_ref, a2b_ref, blk):
    za = _nn(sm_ref[...].astype(BF), a2_ref[...]) + a2b_ref[...]
    row = blk * TM + lax.broadcasted_iota(jnp.int32, (TM, 1), 0)
    return za, jnp.where(row >= FIRST_VALID, _logsig(za) / G_TAU, 0.0)


def _gla_head(h, q_ref, k_ref, rows, bc, btot, k):
    sl = slice(h * DQK, (h + 1) * DQK)
    bch = bc[:, sl]
    bth = btot[:, sl]
    gq = q_ref[rows, h * DQK:(h + 1) * DQK].astype(F32)
    gk = k_ref[rows, NH * DQK + h * DQK:NH * DQK + (h + 1) * DQK].astype(F32)
    e_pos = jnp.exp(bch) * (DQK ** -0.5)
    e_neg = jnp.exp(-bch)
    e_end = jnp.exp(bth - bch)
    qd = gq * e_pos
    ki = gk * e_neg
    ke = gk * e_end
    att = jnp.where(k["tri"], _nt(qd.astype(BF), ki.astype(BF)), 0.0)
    return dict(e_pos=e_pos, e_neg=e_neg, e_end=e_end, qd=qd, ki=ki, ke=ke, att=att, decay=jnp.exp(bth))


def _gla_fwd(pbig, small, a2p, a2b, headg, name):
    tp = pbig.shape[0]
    nb = tp // TM

    def body(qk_ref, v_ref, gr_ref, sm_ref, a2_ref, a2b_ref, hg_ref, y_ref, ss_ref, s_scr, lg_scr):
        blk = pl.program_id(0)

        @pl.when(blk == 0)
        def _():
            s_scr[...] = jnp.zeros_like(s_scr)

        k = _chunk_consts(G_CHUNK)
        _, loga = _gla_loga(sm_ref, a2_ref, a2b_ref, blk)
        lg_scr[...] = loga

        def chunk(c, carry):
            r0 = pl.multiple_of(c * G_CHUNK, G_CHUNK)
            rows = pl.ds(r0, G_CHUNK)
            bc = _nn(k["tril_f"], lg_scr[rows, :], precision=HI)
            btot = _last_row(bc, k["rowio"])
            heads = range(NH)
            f = [_gla_head(h, qk_ref, qk_ref, rows, bc, btot, k) for h in heads]
            vs = [v_ref[rows, h * DV:(h + 1) * DV] for h in heads]
            sst = [s_scr[h] for h in heads]
            sbs = [s.astype(BF) for s in sst]
            inter = [_nt(f[h]["qd"].astype(BF), sbs[h]) for h in heads]
            intra = [_nn(f[h]["att"].astype(BF), vs[h]) for h in heads]
            kv = [_tn(vs[h], f[h]["ke"].astype(BF)) for h in heads]
            for h in heads:
                gr = gr_ref[rows, h * DV:(h + 1) * DV].astype(F32)
                y_ref[rows, h * DV:(h + 1) * DV] = _headnorm_fwd(intra[h] + inter[h], hg_ref[:, h * DV:(h + 1) * DV],
                                                                   gr * _sigmoid(gr)).astype(BF)
                ss_ref[c, h] = sbs[h]
                s_scr[h] = sst[h] * f[h]["decay"] + kv[h]
            return carry

        lax.fori_loop(0, G_CPB, chunk, 0, unroll=2)

    return pl.pallas_call(
        body, grid=(nb,),
        in_specs=[_row_spec(D, CB_GQK), _row_spec(D, CB_GV), _row_spec(D, CB_GR), _row_spec(N_SMALL),
                  _full_spec((N_SMALL, NH * DQK)), _full_spec((1, NH * DQK)), _full_spec((1, D))],
        out_specs=[_row_spec(D), pl.BlockSpec((G_CPB, NH, DV, DQK), lambda i: (i, 0, 0, 0))],
        out_shape=(jax.ShapeDtypeStruct((tp, D), BF), jax.ShapeDtypeStruct((tp // G_CHUNK, NH, DV, DQK), BF)),
        scratch_shapes=[pltpu.VMEM((NH, DV, DQK), F32), pltpu.VMEM((TM, NH * DQK), F32)],
        compiler_params=_params(), name=name)(pbig, pbig, pbig, small, a2p, a2b, headg)


def _gla_bwd(dy, pbig, small, a2p, a2b, headg, ss, dsm_m, dproj, name):
    tp = pbig.shape[0]
    nb = tp // TM
    nqk = NH * DQK

    def body(dy_ref, qk_ref, v_ref, gr_ref, sm_ref, a2_ref, a2b_ref, hg_ref, ss_ref, dsmm_ref, _,
             dproj_ref, dsm_ref, da2_ref, da2b_ref, dhg_ref, ds_scr, lg_scr, dza_scr):
        step = pl.program_id(0)
        blk = nb - 1 - step

        @pl.when(step == 0)
        def _():
            ds_scr[...] = jnp.zeros_like(ds_scr)
            da2_ref[...] = jnp.zeros_like(da2_ref)
            da2b_ref[...] = jnp.zeros_like(da2b_ref)
            dhg_ref[...] = jnp.zeros_like(dhg_ref)

        k = _chunk_consts(G_CHUNK)
        rowio = k["rowio"]
        za, loga = _gla_loga(sm_ref, a2_ref, a2b_ref, blk)
        lg_scr[...] = loga

        def chunk(cc, carry):
            c = G_CPB - 1 - cc
            r0 = pl.multiple_of(c * G_CHUNK, G_CHUNK)
            rows = pl.ds(r0, G_CHUNK)
            bc = _nn(k["tril_f"], lg_scr[rows, :], precision=HI)
            btot = _last_row(bc, rowio)
            heads = range(NH)
            f = [_gla_head(h, qk_ref, qk_ref, rows, bc, btot, k) for h in heads]
            vs = [v_ref[rows, h * DV:(h + 1) * DV] for h in heads]
            sbs = [ss_ref[c, h] for h in heads]
            qdb = [f[h]["qd"].astype(BF) for h in heads]
            attb = [f[h]["att"].astype(BF) for h in heads]
            inter = [_nt(qdb[h], sbs[h]) for h in heads]
            intra = [_nn(attb[h], vs[h]) for h in heads]
            dsn = [ds_scr[h] for h in heads]
            dsb = [d.astype(BF) for d in dsn]
            dke = [_nn(vs[h], dsb[h]) for h in heads]
            dv1 = [_nt(f[h]["ke"].astype(BF), dsb[h]) for h in heads]
            t = []
            for h in heads:
                gr = gr_ref[rows, h * DV:(h + 1) * DV].astype(F32)
                sg = _sigmoid(gr)
                gain = hg_ref[:, h * DV:(h + 1) * DV]
                do, dact, dgain = _headnorm_bwd(dy_ref[rows, h * DV:(h + 1) * DV].astype(F32), intra[h] + inter[h], gain, gr * sg)
                dproj_ref[rows, 2 * D + h * DV:2 * D + (h + 1) * DV] = (dact * sg * (1.0 + gr * (1.0 - sg))).astype(BF)
                dhg_ref[:, h * DV:(h + 1) * DV] += dgain
                dob = do.astype(BF)
                t.append(dict(dob=dob, datt=_nt(dob, vs[h]), dv0=_tn(attb[h], dob), dq1=_nn(dob, sbs[h]), dsq=_tn(dob, qdb[h])))
            for h in heads:
                datt = jnp.where(k["tri"], t[h]["datt"], 0.0).astype(BF)
                t[h].update(dq0=_nn(datt, f[h]["ki"].astype(BF)), dki=_tn(datt, qdb[h]))
            dbc_parts = []
            for h in heads:
                dqd = t[h]["dq0"] + t[h]["dq1"]
                dki = t[h]["dki"]
                dproj_ref[rows, D + h * DV:D + (h + 1) * DV] = (t[h]["dv0"] + dv1[h]).astype(BF)
                dproj_ref[rows, h * DQK:(h + 1) * DQK] = (dqd * f[h]["e_pos"]).astype(BF)
                dproj_ref[rows, nqk + h * DQK:nqk + (h + 1) * DQK] = (dki * f[h]["e_neg"] + dke[h] * f[h]["e_end"]).astype(BF)
                dke_ke = dke[h] * f[h]["ke"]
                dbtot = (jnp.sum(dke_ke, axis=0, keepdims=True)
                         + jnp.sum(dsn[h] * sbs[h].astype(F32), axis=0, keepdims=True) * f[h]["decay"])
                dbc_parts.append(dqd * f[h]["qd"] - dki * f[h]["ki"] - dke_ke + jnp.where(rowio == G_CHUNK - 1, dbtot, 0.0))
                ds_scr[h] = dsn[h] * f[h]["decay"] + t[h]["dsq"]
            dbc = jnp.concatenate(dbc_parts, axis=1)
            dza_scr[rows, :] = _nn(k["triu_f"], dbc, precision=HI)
            return carry

        lax.fori_loop(0, G_CPB, chunk, 0, unroll=2)
        row = blk * TM + lax.broadcasted_iota(jnp.int32, (TM, 1), 0)
        dza = jnp.where(row >= FIRST_VALID, dza_scr[...] * (_sigmoid(-za) / G_TAU), 0.0)
        dzb = dza.astype(BF)
        dsm_ref[...] = (_nt(dzb, a2_ref[...]) + dsmm_ref[...]).astype(BF)
        da2_ref[...] += _tn(sm_ref[...].astype(BF), dzb)
        da2b_ref[...] += jnp.sum(dza, axis=0, keepdims=True)

    rspec = lambda width, col=0: pl.BlockSpec((TM, width), lambda i: (nb - 1 - i, col))
    return pl.pallas_call(
        body, grid=(nb,),
        in_specs=[rspec(D), rspec(D, CB_GQK), rspec(D, CB_GV), rspec(D, CB_GR), rspec(N_SMALL),
                  _full_spec((N_SMALL, nqk)), _full_spec((1, nqk)), _full_spec((1, D)),
                  pl.BlockSpec((G_CPB, NH, DV, DQK), lambda i: (nb - 1 - i, 0, 0, 0)), rspec(N_SMALL), ANY],
        out_specs=[rspec(3 * D, 0), rspec(N_SMALL), _full_spec((N_SMALL, nqk)), _full_spec((1, nqk)), _full_spec((1, D))],
        out_shape=(jax.ShapeDtypeStruct(dproj.shape, BF),
                   jax.ShapeDtypeStruct((tp, N_SMALL), BF), jax.ShapeDtypeStruct((N_SMALL, nqk), F32),
                   jax.ShapeDtypeStruct((1, nqk), F32), jax.ShapeDtypeStruct((1, D), F32)),
        scratch_shapes=[pltpu.VMEM((NH, DV, DQK), F32), pltpu.VMEM((TM, nqk), F32), pltpu.VMEM((TM, nqk), F32)],
        input_output_aliases={10: 0}, compiler_params=_params(), name=name)(dy, pbig, pbig, pbig, small, a2p, a2b, headg, ss, dsm_m, dproj)


PIECE_BYTES = 1 << 20
MAX_PIECES = 32


def _place():
    return lax.axis_index("x"), lax.axis_index("y"), lax.axis_index("c")


def _piece_rows(rows, row_bytes, align):
    want = min(MAX_PIECES, max(1, -(-rows * row_bytes // PIECE_BYTES)))
    best = rows
    for k in range(1, want + 1):
        if rows % k == 0 and (rows // k) % align == 0:
            best = rows // k
    return best


def _remote(src, dst, send_sems, recv_sems, k, to):
    return pltpu.make_async_remote_copy(src_ref=src, dst_ref=dst, send_sem=send_sems.at[k], recv_sem=recv_sems.at[k],
                                        device_id=to, device_id_type=MESH)


def _all_gather_chips(p, name):
    rd = _gather_rider(p)

    def body(*refs):
        start, middle, finish = rd["make"](refs[:1], refs[1:2], refs[2:])
        start()
        middle()
        finish()

    return pl.pallas_call(body, in_specs=[ANY], out_specs=[ANY], out_shape=rd["out_shapes"], scratch_shapes=rd["sems"],
                          name=name)(p)[0]


def _gather_rider(p):
    r, n = p.shape
    rh = r // 2
    align = 32 // p.dtype.itemsize
    assert r % (2 * align) == 0
    cr = _piece_rows(rh, n * p.dtype.itemsize, align)

    def make(in_refs, out_refs, sem_refs):
        p_ref, o_ref = in_refs[0], out_refs[0]
        send_sems, recv_sems = sem_refs
        x, y, c = _place()
        chips = [(1 - x, y), (x, 1 - y), (1 - x, 1 - y)]
        sib = (x, y, 1 - c)

        def half(hc, piece=None):
            if piece is None:
                return pl.ds(pl.multiple_of(hc * rh, align), rh)
            return pl.ds(pl.multiple_of(hc * rh + piece * cr, align), cr)

        first = [_remote(p_ref.at[half(c)], o_ref.at[j, half(c)], send_sems, recv_sems, j, (*chip, c))
                 for j, chip in enumerate(chips)]
        passed = [[_remote(o_ref.at[j, half(c, i)], o_ref.at[j, half(c, i)], send_sems, recv_sems, 3 + j, sib)
                   for i in range(rh // cr)] for j in range(3)]
        blocks = [_remote(o_ref.at[j, half(c)], o_ref.at[j, half(1 - c)], send_sems, recv_sems, 3 + j, sib) for j in range(3)]

        def start():
            for cp in first:
                cp.start()

        def middle():
            for j, cp in enumerate(first):
                cp.wait_recv()
                for piece in passed[j]:
                    piece.start()

        def finish():
            for block in blocks:
                block.wait_send()
                block.wait_recv()
            for cp in first:
                cp.wait_send()

        return start, middle, finish

    return dict(inputs=[p], out_shapes=(jax.ShapeDtypeStruct((3, r, n), p.dtype),),
                sems=[pltpu.SemaphoreType.DMA((6,)), pltpu.SemaphoreType.DMA((6,))], make=make)


def _scatter_rider(items):
    out_shapes, sems = _scatter_shapes(items)

    def make(in_refs, out_refs, sem_refs):
        sent = _scatter_copies(in_refs, out_refs, *sem_refs)

        def start():
            for cp in sent:
                cp.start()

        def finish():
            for cp in sent:
                cp.wait_recv()
            for cp in sent:
                cp.wait_send()

        return start, (lambda: None), finish

    return dict(inputs=list(items), out_shapes=out_shapes, sems=sems, make=make)


def _by_chip(mine, others):
    me = 2 * lax.axis_index("x") + lax.axis_index("y")
    by_mask = jnp.stack([mine, others[1], others[0], others[2]])
    return [lax.dynamic_index_in_dim(by_mask, q ^ me, 0, keepdims=False) for q in range(4)]


def _swap_halves(items, name):
    k = len(items)

    def body(*refs):
        a_refs, got_refs = refs[:k], refs[k:2 * k]
        send_sems, recv_sems = refs[2 * k:]
        x, y, c = _place()
        sib = (x, y, 1 - c)
        for i, a in enumerate(items):
            _, r, n = a.shape
            rh = r // 2
            cr = _piece_rows(rh, n * a.dtype.itemsize, 8)
            for q in range(4):
                for t in range(rh // cr):
                    other = pl.ds(pl.multiple_of((1 - c) * rh + t * cr, 8), cr)
                    _remote(a_refs[i].at[q, other], got_refs[i].at[q, pl.ds(t * cr, cr)], send_sems, recv_sems, i, sib).start()
        for i, a in enumerate(items):
            block = _remote(a_refs[i].at[:, pl.ds(0, a.shape[1] // 2)], got_refs[i], send_sems, recv_sems, i, sib)
            block.wait_send()
            block.wait_recv()

    return pl.pallas_call(
        body, in_specs=[ANY] * k, out_specs=[ANY] * k,
        out_shape=tuple(jax.ShapeDtypeStruct((4, a.shape[1] // 2, a.shape[2]), a.dtype) for a in items),
        scratch_shapes=[pltpu.SemaphoreType.DMA((k,)), pltpu.SemaphoreType.DMA((k,))], name=name)(*items)


def _scatter_copies(s_refs, o_refs, send_sems, recv_sems):
    x, y, c = _place()
    chips = [(1 - x, y), (x, 1 - y), (1 - x, 1 - y)]
    return [_remote(s_refs[i].at[2 * cx + cy], o_refs[i].at[j], send_sems, recv_sems, 3 * i + j, (cx, cy, c))
            for i in range(len(s_refs)) for j, (cx, cy) in enumerate(chips)]


def _scatter_shapes(items):
    k = len(items)
    return (tuple(jax.ShapeDtypeStruct((3,) + s.shape[1:], s.dtype) for s in items),
            [pltpu.SemaphoreType.DMA((3 * k,)), pltpu.SemaphoreType.DMA((3 * k,))])


def _scatter_chips(items, name):
    k = len(items)

    def body(*refs):
        sent = _scatter_copies(refs[:k], refs[k:2 * k], *refs[2 * k:])
        for cp in sent:
            cp.start()
        for cp in sent:
            cp.wait_recv()
        for cp in sent:
            cp.wait_send()

    out_shape, scratch = _scatter_shapes(items)
    return pl.pallas_call(body, in_specs=[ANY] * k, out_specs=[ANY] * k, out_shape=out_shape, scratch_shapes=scratch,
                          name=name)(*items)


def _join_halves(items, name):
    k = len(items)

    def body(*refs):
        f_refs, o_refs = refs[:k], refs[k:2 * k]
        send_sems, recv_sems = refs[2 * k:]
        x, y, c = _place()
        sib = (x, y, 1 - c)
        for i, f in enumerate(items):
            rh, n = f.shape
            cr = _piece_rows(rh, n * f.dtype.itemsize, 8)
            for t in range(rh // cr):
                rows = pl.ds(t * cr, cr)
                _remote(f_refs[i].at[rows], o_refs[i].at[rows], send_sems, recv_sems, i, sib).start()
        for i in range(k):
            block = _remote(f_refs[i], o_refs[i], send_sems, recv_sems, i, sib)
            block.wait_send()
            block.wait_recv()

    return pl.pallas_call(
        body, in_specs=[ANY] * k, out_specs=[ANY] * k, out_shape=tuple(jax.ShapeDtypeStruct(f.shape, f.dtype) for f in items),
        scratch_shapes=[pltpu.SemaphoreType.DMA((k,)), pltpu.SemaphoreType.DMA((k,))], name=name)(*items)


SMALL_ROWS = 16
SMALL_GRAD_ROWS = 48
SMALL_SHARD_SHAPES = [(N_META, 256), (4, 256), (G_RANK, 128), (NH, 64), (NH, 64)]
REPL_SHAPES = [(1, D), (1, D), (1, 2, NH), (1, NH * DQK), (1, D), (D,)]
W_IN_SHARD = 2054
W_IN_BLOCK = 2080


def _pack_small(parts, rows=SMALL_ROWS):
    flat = jnp.concatenate([p.reshape(-1) for p in parts])
    return jnp.pad(flat, (0, rows * D - flat.shape[0])).reshape(rows, D)


def _unpack_small(block, shapes):
    flat, out, off = block.reshape(-1), [], 0
    for shp in shapes:
        n = 1
        for s in shp:
            n *= s
        out.append(flat[off:off + n].reshape(shp))
        off += n
    return out


def _proj_rows_from_w_in(w_in_t):
    w_big = jnp.concatenate([w_in_t[3080:5128], w_in_t[5144:6168], w_in_t[0:1024], w_in_t[6168:8216],
                             w_in_t[1024:2048], w_in_t[2056:3080]], axis=0)
    w_small = jnp.concatenate([w_in_t[2048:2056], w_in_t[5128:5144], jnp.zeros((N_SMALL - 24, D), w_in_t.dtype)], axis=0)
    return w_big, w_small


def _w_in_from_proj_rows(d_wall_t):
    big, small = d_wall_t[0:N_BIG], d_wall_t[N_BIG:N_ALL]
    return jnp.concatenate([big[3072:4096], big[6144:7168], small[0:8], big[7168:8192], big[0:2048],
                            small[8:24], big[2048:3072], big[4096:6144]], axis=0)


def kernel(x, meta_tokens, norm1_g, w_in, conv_w, conv_b, m_gate_b, g_a2, g_a2_b, m_head_g, g_head_g, w_branch_m, w_branch_g, w_out, norm2_g, w_ff_gate, w_ff_up, w_ff_down, final_g, loss_target, m_meta_tokens, m_norm1_g, m_w_in, m_conv_w, m_conv_b, m_m_gate_b, m_g_a2, m_g_a2_b, m_m_head_g, m_g_head_g, m_w_branch_m, m_w_branch_g, m_w_out, m_norm2_g, m_w_ff_gate, m_w_ff_up, m_w_ff_down, m_final_g, v_meta_tokens, v_norm1_g, v_w_in, v_conv_w, v_conv_b, v_m_gate_b, v_g_a2, v_g_a2_b, v_m_head_g, v_g_head_g, v_w_branch_m, v_w_branch_g, v_w_out, v_norm2_g, v_w_ff_gate, v_w_ff_up, v_w_ff_down, v_final_g):
    w = _gather_weights(w_in, w_branch_m, w_branch_g, w_out, w_ff_gate, w_ff_up, w_ff_down, meta_tokens, conv_w, g_a2, m_head_g, g_head_g)
    loss_local, dx, grads = _local_step(x[0], loss_target[0], w, norm1_g, conv_b, m_gate_b, g_a2_b, norm2_g, final_g, _Reducer())

    weights = [w_in, w_branch_m, w_branch_g, w_out, w_ff_gate, w_ff_up, w_ff_down, meta_tokens, conv_w, g_a2, m_head_g, g_head_g,
               norm1_g, conv_b, m_gate_b, g_a2_b, norm2_g, final_g]
    moms = [m_w_in, m_w_branch_m, m_w_branch_g, m_w_out, m_w_ff_gate, m_w_ff_up, m_w_ff_down, m_meta_tokens, m_conv_w, m_g_a2,
            m_m_head_g, m_g_head_g, m_norm1_g, m_conv_b, m_m_gate_b, m_g_a2_b, m_norm2_g, m_final_g]
    vels = [v_w_in, v_w_branch_m, v_w_branch_g, v_w_out, v_w_ff_gate, v_w_ff_up, v_w_ff_down, v_meta_tokens, v_conv_w, v_g_a2,
            v_m_head_g, v_g_head_g, v_norm1_g, v_conv_b, v_m_gate_b, v_g_a2_b, v_norm2_g, v_final_g]
    res = {}
    for nm, wt, g, m, v in zip(PACK_ORDER, weights, grads, moms, vels):
        if nm in TRANSPOSED_GRADS:
            to2d = lambda a: jnp.swapaxes(a, -1, -2).reshape(a.shape[-1], a.shape[-2])
            back = lambda a: jnp.swapaxes(a, 0, 1).reshape(wt.shape)
        else:
            to2d = lambda a: a.reshape(wt.size // wt.shape[-1], wt.shape[-1])
            back = lambda a: a.reshape(wt.shape)
        d, nm_, nv_ = _adamw(to2d(wt), g, to2d(m), to2d(v), "adamw_" + nm)
        res[nm] = (back(g), back(d), back(nm_), back(nv_))

    order = ["meta_tokens", "norm1_g", "w_in", "conv_w", "conv_b", "m_gate_b", "g_a2", "g_a2_b", "m_head_g", "g_head_g",
             "w_branch_m", "w_branch_g", "w_out", "norm2_g", "w_ff_gate", "w_ff_up", "w_ff_down", "final_g"]
    loss = lax.psum(loss_local[0, 0], ("x", "y", "c"))
    grad_x = dx.reshape(x.shape)
    return (loss, grad_x, *[res[n][0] for n in order], *[res[n][1] for n in order],
            *[res[n][2] for n in order], *[res[n][3] for n in order])


TRANSPOSED_GRADS = ("w_in", "w_ff_gate", "w_ff_up")
PACK_ORDER = ["w_in", "w_branch_m", "w_branch_g", "w_out", "w_ff_gate", "w_ff_up", "w_ff_down", "meta_tokens", "conv_w", "g_a2",
              "m_head_g", "g_head_g", "norm1_g", "conv_b", "m_gate_b", "g_a2_b", "norm2_g", "final_g"]


def _gather_weights(w_in, w_branch_m, w_branch_g, w_out, w_ff_gate, w_ff_up, w_ff_down, meta_tokens, conv_w, g_a2, m_head_g, g_head_g):
    bf = lambda a: a.astype(BF)
    rows_local = jnp.concatenate([bf(w_branch_m[0]), bf(w_branch_g[0]), bf(w_out[0]), bf(w_ff_down[0]),
                                  bf(w_ff_gate[0].T), bf(w_ff_up[0].T)], axis=0)
    win_local = jnp.pad(bf(w_in[0].T), ((0, W_IN_BLOCK - W_IN_SHARD), (0, 0)))
    small_local = _pack_small([meta_tokens, conv_w[0], g_a2[0], m_head_g[0], g_head_g[0]])
    small_all = _by_chip(small_local, _all_gather_chips(small_local, "gather_small"))
    small_sh = [_unpack_small(small_all[q], SMALL_SHARD_SHAPES) for q in range(4)]
    cat = lambda i: jnp.concatenate([s[i] for s in small_sh], axis=-1)
    return dict(win_local=win_local, rows_local=rows_local, meta=cat(0), convw=cat(1), ga2=cat(2),
                mhg=cat(3).reshape(1, D), ghg=cat(4).reshape(1, D))


def _row_weights(rows_local, gathered):
    rows_all = jnp.stack(_by_chip(rows_local, gathered))
    cut = lambda lo, hi: rows_all[:, lo:hi].reshape(4 * (hi - lo), D)
    return cut(0, 256), cut(256, 512), cut(512, 768), cut(768, 1472), _ffn_weight_rows(cut(1472, 2176), cut(2176, 2880))


def _local_step(x0, target, w, norm1_g, conv_b, m_gate_b, g_a2_b, norm2_g, final_g, reducer):
    meta_f, convw_f, ga2_f, mhg_f, ghg_f = w["meta"], w["convw"], w["ga2"], w["mhg"], w["ghg"]
    gbias =jnp.concatenate([m_gate_b.reshape(1, 2 * NH), jnp.zeros((1, N_SMALL - 2 * NH), F32)], axis=1)
    a2p = jnp.concatenate([jnp.zeros((8, NH * DQK), F32), ga2_f, jnp.zeros((N_SMALL - 24, NH * DQK), F32)], axis=0).astype(BF)
    convb = conv_b.reshape(1, D)
    g1 = norm1_g.reshape(1, D)
    g2 = norm2_g.reshape(1, D)
    gf = final_g.reshape(1, D)
    first = jnp.concatenate([jnp.zeros((FIRST_VALID, D), F32), meta_f], axis=0)

    h0, xn1, rstd1, win_gathered = _embed_norm(x0, first, g1, _gather_rider(w["win_local"]), "rms1")
    win_all = _by_chip(w["win_local"], win_gathered[0])
    w_in_f = jnp.concatenate([win_all[q][0:W_IN_SHARD] for q in range(4)], axis=0)
    w_big, w_small = _proj_rows_from_w_in(w_in_f)
    w_all = jnp.concatenate([w_big, w_small], axis=0)
    pbig, rows_gathered = _mm(xn1, w_big, nt=True, out_dtype=BF, tn=2 * D, name="proj_big", rider=_gather_rider(w["rows_local"]))
    wbm, wbg, wout, wdown, wgu_t = _row_weights(w["rows_local"], rows_gathered[0])
    small = _mm(xn1, w_small, nt=True, out_dtype=F32, tn=N_SMALL, name="proj_small")
    qk = _conv_fwd(pbig, convw_f, convb, "conv_fwd")
    y_m, m_cs, m_ns = _mlstm_fwd(qk, pbig, small, gbias, mhg_f, "mlstm_fwd")
    y_g, g_ss = _gla_fwd(pbig, small, a2p, g_a2_b, ghg_f, "gla_fwd")
    p_m, p_g, merged = _branch_merge(y_m, y_g, wbm, wbg, pbig, "branch_merge")
    h1, hn, rstd2 = _out_proj_norm(merged, wout, h0, g2, "out_proj")
    gu, ff = _ffn_in(hn, wgu_t, "ff_in")
    dh2, loss_local, d_final_g = _ffn_down_loss(ff, wdown, h1, target, gf, "ff_down_loss")

    d_wdown = _mm_tn(ff, dh2, tm=FF_TN, tn=D, name="dw_ff_down")
    dgu = _ffn_d_hidden(dh2, wdown, gu, "d_ff")
    d_wgu_t = _mm_tn(dgu, hn, tm=FF_TN, tn=D, name="dw_ff_in")
    dh1, d_g2 = _ffn_d_in(dgu, wgu_t, h1, rstd2, g2, dh2, "d_hn")
    d_wout = _mm_tn(merged, dh1, tm=D, tn=D, name="dw_out")
    dp_m, dp_g, dproj = _merge_d(dh1, wout, p_m, p_g, pbig, "d_merged")
    dy_m = _mm(dp_m, wbm, nt=True, out_dtype=BF, tn=D, name="d_ym")
    dy_g = _mm(dp_g, wbg, nt=True, out_dtype=BF, tn=D, name="d_yg")
    d_wbm = _mm_tn(y_m, dp_m, tm=D, tn=D, name="dw_branch_m")
    d_wbg = _mm_tn(y_g, dp_g, tm=D, tn=D, name="dw_branch_g")
    fq = D_FF // 4
    gu4 = jnp.transpose(d_wgu_t.reshape(2, 2, 2, fq, D), (0, 2, 1, 3, 4)).reshape(4, 2 * fq, D)
    sq4 = jnp.concatenate([d_wbm.reshape(4, 256, D), d_wbg.reshape(4, 256, D), d_wout.reshape(4, 256, D)], axis=1)
    sums_a = reducer.partial_sums([sq4, d_wdown.reshape(4, fq, D), gu4], BF, "a")
    dqk_m, dproj, dsm_m, d_gbias, d_mhg, recv_a = _mlstm_bwd(dy_m, qk, pbig, small, gbias, mhg_f, m_cs, m_ns, dproj,
                                                              "mlstm_bwd", ride=sums_a)
    dproj, d_convwb = _conv_bwd(dqk_m, pbig, convw_f, convb, dproj, "conv_bwd")
    dproj, dsmall, d_a2p, d_a2b, d_ghg = _gla_bwd(dy_g, pbig, small, a2p, g_a2_b, ghg_f, g_ss, dsm_m, dproj, "gla_bwd")
    dproj = _place_small(dsmall, dproj, "dproj_small")
    d_win = _w_in_from_proj_rows(_mm_tn(dproj, xn1, tm=PROJ_TK, tn=D, name="dw_in"))
    pad = jnp.zeros((W_IN_BLOCK - W_IN_SHARD, D), F32)
    win4 = jnp.stack([jnp.concatenate([d_win[q * W_IN_SHARD:(q + 1) * W_IN_SHARD], pad], axis=0) for q in range(4)])
    sums_b = reducer.partial_sums([win4], BF, "b")
    dxn, recv_b = _mm(dproj, w_all, nt=False, out_dtype=F32, tn=D, tm=TM, name="d_xn", rider=_scatter_rider(sums_b),
                      single_buffer_b=True)
    dh_first, dx, d_g1 = _rms_bwd(dxn, h0, rstd1, g1, dh1, "rms1_bwd", split_first=True)

    small_sharded = [dh_first[FIRST_VALID:TM], d_convwb[0:4], d_a2p[8:24], d_mhg.reshape(NH, DV), d_ghg.reshape(NH, DV)]
    replicated = [d_g1, d_convwb[4:5], d_gbias[0:1, 0:2 * NH].reshape(1, 2, NH), d_a2b, d_g2, d_final_g.reshape(D)]
    small4 = jnp.broadcast_to(_pack_small(small_sharded + replicated, SMALL_GRAD_ROWS)[None], (4, SMALL_GRAD_ROWS, D))
    sums_c = reducer.partial_sums([small4], F32, "c")
    recv_c = reducer.scatter(sums_c, "c")
    sq, down, gu, win, smalls = reducer.finish(sums_a + sums_b + sums_c, recv_a + recv_b + recv_c, in_chip_order=[4])
    smalls = _unpack_small(smalls, [g.shape for g in small_sharded + replicated])
    me = 2 * lax.axis_index("x") + lax.axis_index("y")
    smalls = ([lax.dynamic_slice_in_dim(g, me * shp[1], shp[1], axis=1) for g, shp in zip(smalls, SMALL_SHARD_SHAPES)]
              + smalls[len(SMALL_SHARD_SHAPES):])
    grads = ([win[0:W_IN_SHARD], sq[0:256], sq[256:512], sq[512:768], gu[0:fq], gu[fq:2 * fq], down]
             + [g.reshape(g.size // g.shape[-1], g.shape[-1]) for g in smalls])
    return loss_local, dx, grads


class _Reducer:
    def partial_sums(self, items, dtype, tag):
        c = lax.axis_index("c")
        got = _swap_halves(items, "reduce_siblings_" + tag)
        sums = []
        for i, (a, g) in enumerate(zip(items, got)):
            rh, n = g.shape[1], g.shape[2]
            own = lax.dynamic_slice_in_dim(a, c * rh, rh, axis=1)
            sums.append(_add2(own.reshape(-1, n), g.reshape(-1, n), dtype, f"reduce_add2_{tag}{i}").reshape(g.shape))
        return sums

    def scatter(self, sums, tag):
        return list(_scatter_chips(sums, "reduce_chips_" + tag))

    def finish(self, sums, from_chips, in_chip_order):
        c = lax.axis_index("c")
        me = 2 * lax.axis_index("x") + lax.axis_index("y")
        halves = []
        for i, (s, f) in enumerate(zip(sums, from_chips)):
            mine = lax.dynamic_index_in_dim(s, me, 0, keepdims=False)
            if i in in_chip_order:
                by_chip = _by_chip(mine, f)
                mine, f = by_chip[0], jnp.stack(by_chip[1:])
            halves.append(_add4(mine, f, f"reduce_add4_{i}"))
        got = _join_halves(halves, "reduce_join")
        return [jnp.concatenate([jnp.where(c == 0, h, g), jnp.where(c == 0, g, h)], axis=0) for h, g in zip(halves, got)]
```

```python
import jax
import jax.numpy as jnp
from jax import lax
from jax.experimental import pallas as pl
from jax.experimental.pallas import tpu as pltpu

F32 = jnp.float32
BF = jnp.bfloat16
HI = lax.Precision.HIGHEST
MESH = pl.DeviceIdType.MESH

D = 1024
N_META = 16
CHUNK = 128
EPS = 1e-6
NH = 4
DV = 256
DQK = 128
G_RANK = 16
G_TAU = 16.0
D_FF = 2816
TM = 512
FIRST_VALID = TM - N_META
CPB = TM // CHUNK
G_CHUNK = 256
G_CPB = TM // G_CHUNK
NEG = -1e30
N_BIG = 8192
CB_GQK, CB_GV, CB_GR, CB_MQK, CB_GM, CB_GG, CB_MV, CB_MO = range(8)
N_SMALL = 128
N_ALL = N_BIG + N_SMALL
PROJ_TK = N_ALL // 5
VMEM_LIMIT = 56 * 1024 * 1024

ADAM_LR, ADAM_B1, ADAM_B2, ADAM_EPS, ADAM_WD, ADAM_STEP = 0.001, 0.9, 0.999, 1e-08, 0.01, 10

NT_DIMS = (((1,), (1,)), ((), ()))
TN_DIMS = (((0,), (0,)), ((), ()))


def _nt(a, b, **kw):
    return lax.dot_general(a, b, NT_DIMS, preferred_element_type=F32, **kw)


def _tn(a, b, **kw):
    return lax.dot_general(a, b, TN_DIMS, preferred_element_type=F32, **kw)


def _nn(a, b, **kw):
    return jnp.dot(a, b, preferred_element_type=F32, **kw)


def _params(**kw):
    return pltpu.CompilerParams(vmem_limit_bytes=VMEM_LIMIT, **kw)


def _sigmoid(x):
    return 0.5 * jnp.tanh(0.5 * x) + 0.5


def _logsig(x):
    return jnp.minimum(x, 0.0) - jnp.log(1.0 + jnp.exp(-jnp.abs(x)))


def _mm_rows(rows):
    return 3 * TM if rows % (3 * TM) == 0 else TM


def _mm(a, b, *, nt, out_dtype, tn, tk=None, tm=None, name, rider=None, single_buffer_b=False):
    m, k = a.shape
    n = b.shape[0] if nt else b.shape[1]
    tk = k if tk is None else tk
    tm = _mm_rows(m) if tm is None else tm
    nk = k // tk
    nj, ni = n // tn, m // tm
    nr_in = len(rider["inputs"]) if rider else 0
    nr_out = len(rider["out_shapes"]) if rider else 0
    assert m % tm == 0 and n % tn == 0 and k % tk == 0
    dims = NT_DIMS if nt else (((1,), (0,)), ((), ()))

    def body(*refs):
        a_ref, b_ref = refs[:2]
        o_ref = refs[2 + nr_in]
        j, i, kk = pl.program_id(0), pl.program_id(1), pl.program_id(2)
        step = (j * ni + i) * nk + kk
        if rider:
            start, middle, finish = rider["make"](refs[2:2 + nr_in], refs[3 + nr_in:3 + nr_in + nr_out],
                                                  refs[3 + nr_in + nr_out:5 + nr_in + nr_out])
            pl.when(step == 0)(start)
            pl.when(step == (nj * ni * nk) // 2)(middle)

        part = lax.dot_general(a_ref[...].astype(BF), b_ref[...].astype(BF), dims, preferred_element_type=F32)
        if nk == 1:
            o_ref[...] = part.astype(o_ref.dtype)
        else:
            acc_ref = refs[-1]

            @pl.when(kk == 0)
            def _():
                acc_ref[...] = part

            @pl.when(jnp.logical_and(kk > 0, kk < nk - 1))
            def _():
                acc_ref[...] += part

            @pl.when(kk == nk - 1)
            def _():
                o_ref[...] = (acc_ref[...] + part).astype(o_ref.dtype)

        if rider:
            pl.when(step == nj * ni * nk - 1)(finish)

    outs = pl.pallas_call(
        body, grid=(nj, ni, nk),
        in_specs=[pl.BlockSpec((tm, tk), lambda j, i, kk: (i, kk)),
                  pl.BlockSpec((tn, tk) if nt else (tk, tn), (lambda j, i, kk: (j, kk)) if nt else (lambda j, i, kk: (kk, j)),
                               pipeline_mode=pl.Buffered(1) if single_buffer_b else None)]
                 + [ANY] * nr_in,
        out_specs=[pl.BlockSpec((tm, tn), lambda j, i, kk: (i, j))] + [ANY] * nr_out,
        out_shape=(jax.ShapeDtypeStruct((m, n), out_dtype),) + (tuple(rider["out_shapes"]) if rider else ()),
        scratch_shapes=(rider["sems"] if rider else []) + ([pltpu.VMEM((tm, tn), F32)] if nk > 1 else []),
        compiler_params=_params(), name=name)(a, b, *(rider["inputs"] if rider else []))
    return (outs[0], list(outs[1:])) if rider else outs[0]


def _mm_tn(a, b, *, tm, tn, tk=None, name):
    t, m = a.shape
    n = b.shape[1]
    tk = _mm_rows(t) if tk is None else tk
    assert t % tk == 0 and m % tm == 0 and n % tn == 0

    def body(a_ref, b_ref, o_ref):
        part = _tn(a_ref[...].astype(BF), b_ref[...].astype(BF))

        @pl.when(pl.program_id(2) == 0)
        def _():
            o_ref[...] = part

        @pl.when(pl.program_id(2) > 0)
        def _():
            o_ref[...] += part

    return pl.pallas_call(
        body, grid=(m // tm, n // tn, t // tk),
        in_specs=[pl.BlockSpec((tk, tm), lambda i, j, kk: (kk, i)), pl.BlockSpec((tk, tn), lambda i, j, kk: (kk, j))],
        out_specs=pl.BlockSpec((tm, tn), lambda i, j, kk: (i, j)),
        out_shape=jax.ShapeDtypeStruct((m, n), F32), compiler_params=_params(), name=name)(a, b)


ANY = pl.BlockSpec(memory_space=pl.ANY)


def _row_spec(width, col=0):
    return pl.BlockSpec((TM, width), lambda i: (i, col))


def _full_spec(shape):
    return pl.BlockSpec(shape, lambda i: (0,) * len(shape))


def _embed_norm(x0, first, g, rider, name):
    tp = x0.shape[0] + TM
    nb = tp // TM
    nri, nro = len(rider["inputs"]), len(rider["out_shapes"])

    def body(*refs):
        x_ref, f_ref, g_ref = refs[:3]
        h_ref, xn_ref, r_ref = refs[3 + nri:6 + nri]
        i = pl.program_id(0)
        start, middle, finish = rider["make"](refs[3:3 + nri], refs[6 + nri:6 + nri + nro], refs[6 + nri + nro:])
        pl.when(i == 0)(start)
        pl.when(i == nb // 2)(middle)
        x = jnp.where(i == 0, f_ref[...], x_ref[...])
        r = lax.rsqrt(jnp.mean(x * x, axis=1, keepdims=True) + EPS)
        h_ref[...] = x
        xn_ref[...] = (x * r * g_ref[...]).astype(BF)
        r_ref[...] = r
        pl.when(i == nb - 1)(finish)

    outs = pl.pallas_call(
        body, grid=(nb,),
        in_specs=[pl.BlockSpec((TM, D), lambda i: (jnp.maximum(i - 1, 0), 0)), _full_spec((TM, D)), _full_spec((1, D))] + [ANY] * nri,
        out_specs=[_row_spec(D), _row_spec(D), _row_spec(1)] + [ANY] * nro,
        out_shape=(jax.ShapeDtypeStruct((tp, D), F32), jax.ShapeDtypeStruct((tp, D), BF), jax.ShapeDtypeStruct((tp, 1), F32))
                  + tuple(rider["out_shapes"]),
        scratch_shapes=rider["sems"], compiler_params=_params(), name=name)(x0, first, g, *rider["inputs"])
    return outs[0], outs[1], outs[2], list(outs[3:])


def _rms_bwd(dxn, h, rstd, g, dres, name, split_first=False):
    tp = h.shape[0]

    def body(dxn_ref, h_ref, r_ref, g_ref, dres_ref, *outs):
        r = r_ref[...]
        xh = h_ref[...] * r
        dxn_v = dxn_ref[...].astype(F32)
        dxh = dxn_v * g_ref[...]
        dh = r * (dxh - xh * jnp.mean(dxh * xh, axis=1, keepdims=True)) + dres_ref[...]
        if split_first:
            first_ref, dh_ref, dg_ref = outs

            @pl.when(pl.program_id(0) == 0)
            def _():
                first_ref[...] = dh
        else:
            dh_ref, dg_ref = outs
        dh_ref[...] = dh
        part = jnp.sum(dxn_v * xh, axis=0, keepdims=True)

        @pl.when(pl.program_id(0) == 0)
        def _():
            dg_ref[...] = part

        @pl.when(pl.program_id(0) > 0)
        def _():
            dg_ref[...] += part

    if split_first:
        out_specs = [_full_spec((TM, D)), pl.BlockSpec((TM, D), lambda i: (jnp.maximum(i - 1, 0), 0)), _full_spec((1, D))]
        out_shape = (jax.ShapeDtypeStruct((TM, D), F32), jax.ShapeDtypeStruct((tp - TM, D), F32), jax.ShapeDtypeStruct((1, D), F32))
    else:
        out_specs = [_row_spec(D), _full_spec((1, D))]
        out_shape = (jax.ShapeDtypeStruct((tp, D), F32), jax.ShapeDtypeStruct((1, D), F32))
    return pl.pallas_call(
        body, grid=(tp // TM,),
        in_specs=[_row_spec(D), _row_spec(D), _row_spec(1), _full_spec((1, D)), _row_spec(D)],
        out_specs=out_specs, out_shape=out_shape, compiler_params=_params(), name=name)(dxn, h, rstd, g, dres)


def _shift_down(x, halo, k):
    rk = pltpu.roll(x, k, 0)
    io = lax.broadcasted_iota(jnp.int32, (8, x.shape[1]), 0)
    top = jnp.where(io < k, pltpu.roll(halo, k, 0), rk[0:8])
    return top if x.shape[0] == 8 else jnp.concatenate([top, rk[8:]], axis=0)


def _shift_up(x, nxt, k):
    n = x.shape[0]
    rk = pltpu.roll(x, n - k, 0)
    io = lax.broadcasted_iota(jnp.int32, (8, x.shape[1]), 0)
    bot = jnp.where(io >= 8 - k, pltpu.roll(nxt, 8 - k, 0), rk[n - 8:n])
    return jnp.concatenate([rk[:n - 8], bot], axis=0)


def _conv_pre(x, halo, w_ref, b_ref):
    c = x * w_ref[3:4, :] + b_ref[...]
    shifted = []
    for k in (1, 2, 3):
        s = _shift_down(x, halo, k)
        shifted.append(s)
        c = c + s * w_ref[3 - k:4 - k, :]
    return c, shifted


def _qk_scale():
    col = lax.broadcasted_iota(jnp.int32, (1, D), 1)
    return jnp.where(col < NH * DQK, DQK ** -0.5, 1.0).astype(F32)


def _halo_prev_spec():
    return pl.BlockSpec((8, D), lambda i: (jnp.maximum(i * (TM // 8) - 1, 0), CB_MQK))


def _conv_fwd(pbig, w, b, name):
    tp = pbig.shape[0]

    def body(x_ref, halo_ref, w_ref, b_ref, o_ref):
        x = x_ref[...].astype(F32)
        halo = jnp.where(pl.program_id(0) > 0, halo_ref[...].astype(F32), 0.0)
        c, _ = _conv_pre(x, halo, w_ref, b_ref)
        o_ref[...] = (c * _sigmoid(c) * _qk_scale()).astype(BF)

    return pl.pallas_call(
        body, grid=(tp // TM,),
        in_specs=[_row_spec(D, CB_MQK), _halo_prev_spec(), _full_spec((4, D)), _full_spec((1, D))],
        out_specs=_row_spec(D), out_shape=jax.ShapeDtypeStruct((tp, D), BF),
        compiler_params=_params(), name=name)(pbig, pbig, w, b)


def _conv_bwd(dqk, pbig, w, b, dproj, name):
    tp = pbig.shape[0]
    nb = tp // TM

    def d_conv_out(d, x, halo, w_ref, b_ref):
        c, shifted = _conv_pre(x, halo, w_ref, b_ref)
        sg = _sigmoid(c)
        return d * _qk_scale() * (sg * (1.0 + c * (1.0 - sg))), shifted

    def body(d_ref, dn_ref, x_ref, halo_ref, xn_ref, w_ref, b_ref, _, o_ref, dwb_ref):
        i = pl.program_id(0)
        x = x_ref[...].astype(F32)
        halo = jnp.where(i > 0, halo_ref[...].astype(F32), 0.0)
        dc, shifted = d_conv_out(d_ref[...], x, halo, w_ref, b_ref)
        dc_next, _ = d_conv_out(dn_ref[...], xn_ref[...].astype(F32), x[TM - 8:TM], w_ref, b_ref)
        nxt = jnp.where(i < nb - 1, dc_next, 0.0)
        acc = dc * w_ref[3:4, :]
        for k in (1, 2, 3):
            acc = acc + _shift_up(dc, nxt, k) * w_ref[3 - k:4 - k, :]
        o_ref[...] = acc.astype(BF)
        taps = [shifted[2], shifted[1], shifted[0], x]
        rows = [jnp.sum(dc * t, axis=0, keepdims=True) for t in taps] + [jnp.sum(dc, axis=0, keepdims=True)]
        io = lax.broadcasted_iota(jnp.int32, (8, D), 0)
        part = jnp.zeros((8, D), F32)
        for r, v in enumerate(rows):
            part = jnp.where(io == r, v, part)

        @pl.when(pl.program_id(0) == 0)
        def _():
            dwb_ref[...] = part

        @pl.when(pl.program_id(0) > 0)
        def _():
            dwb_ref[...] += part

    next8 = lambda col: pl.BlockSpec((8, D), lambda i: (jnp.minimum((i + 1) * (TM // 8), tp // 8 - 1), col))
    return pl.pallas_call(
        body, grid=(nb,),
        in_specs=[_row_spec(D), next8(0), _row_spec(D, CB_MQK), _halo_prev_spec(), next8(CB_MQK),
                  _full_spec((4, D)), _full_spec((1, D)), ANY],
        out_specs=[_row_spec(D, CB_MQK), _full_spec((8, D))],
        out_shape=(jax.ShapeDtypeStruct(dproj.shape, BF), jax.ShapeDtypeStruct((8, D), F32)),
        input_output_aliases={7: 0}, compiler_params=_params(), name=name)(dqk, dqk, pbig, pbig, pbig, w, b, dproj)


def _mm_fused(inputs, products, *, nt, m, n, tm, tn, outs, epilogue, name, nk=1, sub=None):
    dims = NT_DIMS if nt else (((1,), (0,)), ((), ()))
    nin = len(inputs)
    assert nk == 1 or (len(products) == 1 and sub is None)

    def body(*refs):
        in_refs, out_refs = refs[:nin], refs[nin:nin + len(outs)]
        i = pl.program_id(1)
        if sub is not None:
            lhs = {ia: in_refs[ia][...].astype(BF) for ia, _ in products}

            def dots(cols):
                return [lax.dot_general(lhs[ia], (in_refs[ib][cols, :] if nt else in_refs[ib][:, cols]).astype(BF),
                                        dims, preferred_element_type=F32) for ia, ib in products]

            slices = [slice(s, min(s + sub, tn)) for s in range(0, tn, sub)]
            prods = dots(slices[0])
            for idx, cols in enumerate(slices):
                nxt = dots(slices[idx + 1]) if idx + 1 < len(slices) else None
                epilogue(prods, in_refs, out_refs, i, cols)
                prods = nxt
            return
        prods = [lax.dot_general(in_refs[ia][...].astype(BF), in_refs[ib][...].astype(BF), dims, preferred_element_type=F32)
                 for ia, ib in products]
        if nk == 1:
            epilogue(prods, in_refs, out_refs, i, slice(None))
            return
        acc_ref = refs[-1]
        kk = pl.program_id(2)

        @pl.when(kk == 0)
        def _():
            acc_ref[...] = prods[0]

        @pl.when(jnp.logical_and(kk > 0, kk < nk - 1))
        def _():
            acc_ref[...] += prods[0]

        @pl.when(kk == nk - 1)
        def _():
            epilogue([acc_ref[...] + prods[0]], in_refs, out_refs, i, slice(None))

    return pl.pallas_call(
        body, grid=(n // tn, m // tm, nk), in_specs=[s for _, s in inputs], out_specs=[s for _, s in outs],
        out_shape=tuple(sh for sh, _ in outs), scratch_shapes=[pltpu.VMEM((tm, tn), F32)] if nk > 1 else [],
        compiler_params=_params(), name=name)(*[a for a, _ in inputs])


SUB_COLS = 256


def _cols_at(cols, offset):
    return slice(cols.start + offset, cols.stop + offset)


def _blk(rows, width, col=None, row=None):
    return pl.BlockSpec((rows, width), lambda j, i, kk: ((i if row is None else row(i)), (0 if col is None else col(j, kk))))


FF_TN = D_FF // 2


def _ffn_weight_rows(wg_t, wu_t):
    return jnp.concatenate([wg_t[0:FF_TN], wu_t[0:FF_TN], wg_t[FF_TN:], wu_t[FF_TN:]], axis=0)


def _ffn_in(hn, wgu_t, name):
    tp = hn.shape[0]
    tm = _mm_rows(tp)

    def epilogue(prods, in_refs, out_refs, i, cols):
        g, u = prods
        out_refs[0][:, cols] = g.astype(BF)
        out_refs[0][:, _cols_at(cols, FF_TN)] = u.astype(BF)
        out_refs[1][:, cols] = (g * _sigmoid(g) * u).astype(BF)

    wspec = lambda off: pl.BlockSpec((FF_TN, D), lambda j, i, kk: (2 * j + off, 0))
    return _mm_fused(
        [(hn, _blk(tm, D)), (wgu_t, wspec(0)), (wgu_t, wspec(1))], [(0, 1), (0, 2)], nt=True, m=tp, n=D_FF, tm=tm, tn=FF_TN,
        outs=[(jax.ShapeDtypeStruct((tp, 2 * D_FF), BF), _blk(tm, 2 * FF_TN, lambda j, kk: j)),
              (jax.ShapeDtypeStruct((tp, D_FF), BF), _blk(tm, FF_TN, lambda j, kk: j))],
        epilogue=epilogue, name=name, sub=SUB_COLS)


def _ffn_down_loss(ff, wdown, h1, target, gf, name):
    tp = ff.shape[0]

    def epilogue(prods, in_refs, out_refs, i, cols):
        live = (i > 0).astype(F32)
        g = in_refs[4][...]
        x = prods[0] + in_refs[2][...]
        r = lax.rsqrt(jnp.mean(x * x, axis=1, keepdims=True) + EPS)
        xh = x * r
        e = xh * g - in_refs[3][...]
        loss_part = 0.5 * live * jnp.sum(jnp.mean(e * e, axis=1, keepdims=True), axis=0, keepdims=True)
        dout = e * (live / D)
        dg_part = jnp.sum(dout * xh, axis=0, keepdims=True)
        dxh = dout * g
        out_refs[0][...] = r * (dxh - xh * jnp.mean(dxh * xh, axis=1, keepdims=True))

        @pl.when(i == 0)
        def _():
            out_refs[1][...] = loss_part
            out_refs[2][...] = dg_part

        @pl.when(i > 0)
        def _():
            out_refs[1][...] += loss_part
            out_refs[2][...] += dg_part

    const = lambda shape: pl.BlockSpec(shape, lambda j, i, kk: (0,) * len(shape))
    return _mm_fused(
        [(ff, _blk(TM, D_FF)), (wdown, const((D_FF, D))), (h1, _blk(TM, D)),
         (target, _blk(TM, D, row=lambda i: jnp.maximum(i - 1, 0))), (gf, const((1, D)))],
        [(0, 1)], nt=False, m=tp, n=D, tm=TM, tn=D,
        outs=[(jax.ShapeDtypeStruct((tp, D), F32), _blk(TM, D)), (jax.ShapeDtypeStruct((1, 1), F32), const((1, 1))),
              (jax.ShapeDtypeStruct((1, D), F32), const((1, D)))],
        epilogue=epilogue, name=name)


def _ffn_d_hidden(dh2, wdown, gu, name):
    tp = dh2.shape[0]

    def epilogue(prods, in_refs, out_refs, i, cols):
        d = prods[0]
        g = in_refs[2][:, cols].astype(F32)
        u = in_refs[2][:, _cols_at(cols, FF_TN)].astype(F32)
        sg = _sigmoid(g)
        out_refs[0][:, cols] = (d * u * sg * (1.0 + g * (1.0 - sg))).astype(BF)
        out_refs[0][:, _cols_at(cols, FF_TN)] = (d * g * sg).astype(BF)

    return _mm_fused(
        [(dh2, _blk(TM, D)), (wdown, pl.BlockSpec((FF_TN, D), lambda j, i, kk: (j, 0))), (gu, _blk(TM, 2 * FF_TN, lambda j, kk: j))],
        [(0, 1)], nt=True, m=tp, n=D_FF, tm=TM, tn=FF_TN,
        outs=[(jax.ShapeDtypeStruct((tp, 2 * D_FF), BF), _blk(TM, 2 * FF_TN, lambda j, kk: j))],
        epilogue=epilogue, name=name, sub=SUB_COLS)[0]


def _ffn_d_in(dgu, wgu_t, h1, rstd, g2, dh2, name):
    tp = dgu.shape[0]

    def epilogue(prods, in_refs, out_refs, i, cols):
        r = in_refs[3][...]
        xh = in_refs[2][...] * r
        dxn = prods[0]
        dxh = dxn * in_refs[4][...]
        out_refs[0][...] = r * (dxh - xh * jnp.mean(dxh * xh, axis=1, keepdims=True)) + in_refs[5][...]
        part = jnp.sum(dxn * xh, axis=0, keepdims=True)

        @pl.when(i == 0)
        def _():
            out_refs[1][...] = part

        @pl.when(i > 0)
        def _():
            out_refs[1][...] += part

    const = lambda shape: pl.BlockSpec(shape, lambda j, i, kk: (0,) * len(shape))
    return _mm_fused(
        [(dgu, _blk(TM, 2 * D_FF)), (wgu_t, const((2 * D_FF, D))),
         (h1, _blk(TM, D)), (rstd, _blk(TM, 1)), (g2, const((1, D))), (dh2, _blk(TM, D))],
        [(0, 1)], nt=False, m=tp, n=D, tm=TM, tn=D,
        outs=[(jax.ShapeDtypeStruct((tp, D), F32), _blk(TM, D)), (jax.ShapeDtypeStruct((1, D), F32), const((1, D)))],
        epilogue=epilogue, name=name)


def _branch_merge(y_m, y_g, wbm, wbg, pbig, name):
    tp = y_m.shape[0]

    def epilogue(prods, in_refs, out_refs, i, cols):
        pm, pg = prods[0].astype(BF), prods[1].astype(BF)
        out_refs[0][:, cols] = pm
        out_refs[1][:, cols] = pg
        out_refs[2][:, cols] = (_sigmoid(in_refs[4][:, cols].astype(F32)) * pm.astype(F32)
                                + _sigmoid(in_refs[5][:, cols].astype(F32)) * pg.astype(F32)).astype(BF)

    const = lambda shape: pl.BlockSpec(shape, lambda j, i, kk: (0,) * len(shape))
    shp = jax.ShapeDtypeStruct((tp, D), BF)
    return _mm_fused(
        [(y_m, _blk(TM, D)), (wbm, const((D, D))), (y_g, _blk(TM, D)), (wbg, const((D, D))),
         (pbig, _blk(TM, D, lambda j, kk: CB_GM)), (pbig, _blk(TM, D, lambda j, kk: CB_GG))],
        [(0, 1), (2, 3)], nt=False, m=tp, n=D, tm=TM, tn=D,
        outs=[(shp, _blk(TM, D)), (shp, _blk(TM, D)), (shp, _blk(TM, D))], epilogue=epilogue, name=name, sub=SUB_COLS)


def _merge_d(dh1, wout, pm, pg, pbig, name):
    tp = dh1.shape[0]

    def epilogue(prods, in_refs, out_refs, i, cols):
        d = prods[0]
        sm = _sigmoid(in_refs[4][:, cols].astype(F32))
        sg = _sigmoid(in_refs[5][:, cols].astype(F32))
        out_refs[0][:, cols] = (d * sm).astype(BF)
        out_refs[1][:, cols] = (d * sg).astype(BF)
        out_refs[2][:, cols] = (d * in_refs[2][:, cols].astype(F32) * sm * (1.0 - sm)).astype(BF)
        out_refs[2][:, _cols_at(cols, D)] = (d * in_refs[3][:, cols].astype(F32) * sg * (1.0 - sg)).astype(BF)

    const = lambda shape: pl.BlockSpec(shape, lambda j, i, kk: (0,) * len(shape))
    shp = jax.ShapeDtypeStruct((tp, D), BF)
    return _mm_fused(
        [(dh1, _blk(TM, D)), (wout, const((D, D))), (pm, _blk(TM, D)), (pg, _blk(TM, D)),
         (pbig, _blk(TM, D, lambda j, kk: CB_GM)), (pbig, _blk(TM, D, lambda j, kk: CB_GG))],
        [(0, 1)], nt=True, m=tp, n=D, tm=TM, tn=D,
        outs=[(shp, _blk(TM, D)), (shp, _blk(TM, D)),
              (jax.ShapeDtypeStruct((tp, N_ALL), BF), _blk(TM, 2 * D, lambda j, kk: CB_GM // 2))],
        epilogue=epilogue, name=name, sub=SUB_COLS)


def _out_proj_norm(merged, wout, h0, g2, name):
    tp = merged.shape[0]
    tm = _mm_rows(tp)

    def epilogue(prods, in_refs, out_refs, i, cols):
        x = prods[0] + in_refs[2][...]
        r = lax.rsqrt(jnp.mean(x * x, axis=1, keepdims=True) + EPS)
        out_refs[0][...] = x
        out_refs[1][...] = (x * r * in_refs[3][...]).astype(BF)
        out_refs[2][...] = r

    const = lambda shape: pl.BlockSpec(shape, lambda j, i, kk: (0,) * len(shape))
    return _mm_fused(
        [(merged, _blk(tm, D)), (wout, const((D, D))), (h0, _blk(tm, D)), (g2, const((1, D)))],
        [(0, 1)], nt=False, m=tp, n=D, tm=tm, tn=D,
        outs=[(jax.ShapeDtypeStruct((tp, D), F32), _blk(tm, D)), (jax.ShapeDtypeStruct((tp, D), BF), _blk(tm, D)),
              (jax.ShapeDtypeStruct((tp, 1), F32), _blk(tm, 1))],
        epilogue=epilogue, name=name)


def _adamw(w, g, m, v, name):
    rows, cols = w.shape
    by_cols = rows % 128 != 0 and cols % 128 == 0 and rows * cols > 128 * 1024
    tr = rows if (by_cols or rows % 128 != 0) else 128
    tc = 128 if by_cols else cols

    def body(w_ref, g_ref, m_ref, v_ref, d_ref, nm_ref, nv_ref):
        gv = g_ref[...]
        nm = ADAM_B1 * m_ref[...] + (1.0 - ADAM_B1) * gv
        nv = ADAM_B2 * v_ref[...] + (1.0 - ADAM_B2) * (gv * gv)
        m_hat = nm / (1.0 - ADAM_B1 ** ADAM_STEP)
        v_hat = nv / (1.0 - ADAM_B2 ** ADAM_STEP)
        d_ref[...] = -ADAM_LR * (m_hat / (jnp.sqrt(v_hat) + ADAM_EPS) + ADAM_WD * w_ref[...])
        nm_ref[...] = nm
        nv_ref[...] = nv

    spec = pl.BlockSpec((tr, tc), (lambda i: (0, i)) if by_cols else (lambda i: (i, 0)))
    shp = jax.ShapeDtypeStruct((rows, cols), F32)
    return pl.pallas_call(body, grid=(cols // tc if by_cols else rows // tr,), in_specs=[spec] * 4, out_specs=[spec] * 3,
                          out_shape=(shp,) * 3, compiler_params=_params(), name=name)(w, g, m, v)


def _place_small(dsmall, dproj, name):
    tp = dsmall.shape[0]

    def body(s_ref, _, o_ref):
        o_ref[...] = s_ref[...]

    return pl.pallas_call(
        body, grid=(tp // TM,), in_specs=[_row_spec(N_SMALL), ANY], out_specs=_row_spec(N_SMALL, N_BIG // N_SMALL),
        out_shape=jax.ShapeDtypeStruct(dproj.shape, dproj.dtype), input_output_aliases={1: 0},
        compiler_params=_params(), name=name)(dsmall, dproj)


def _row_tile(rows, cap=512):
    best = rows
    for cand in range(8, min(rows, cap) + 1, 8):
        if rows % cand == 0:
            best = cand
    return best


def _add2(a, b, out_dtype, name):
    rows, cols = a.shape
    tr = _row_tile(rows)

    def body(a_ref, b_ref, o_ref):
        o_ref[...] = (a_ref[...] + b_ref[...]).astype(o_ref.dtype)

    spec = pl.BlockSpec((tr, cols), lambda i: (i, 0))
    return pl.pallas_call(body, grid=(rows // tr,), in_specs=[spec] * 2, out_specs=spec,
                          out_shape=jax.ShapeDtypeStruct((rows, cols), out_dtype), compiler_params=_params(), name=name)(a, b)


def _add4(first, rest, name):
    rows, cols = first.shape
    tr = _row_tile(rows, 256)

    def body(f_ref, r_ref, o_ref):
        up = lambda v: v.astype(F32)
        o_ref[...] = ((up(f_ref[...]) + up(r_ref[0])) + up(r_ref[1])) + up(r_ref[2])

    return pl.pallas_call(body, grid=(rows // tr,),
                          in_specs=[pl.BlockSpec((tr, cols), lambda i: (i, 0)), pl.BlockSpec((3, tr, cols), lambda i: (0, i, 0))],
                          out_specs=pl.BlockSpec((tr, cols), lambda i: (i, 0)),
                          out_shape=jax.ShapeDtypeStruct((rows, cols), F32), compiler_params=_params(), name=name)(first, rest)


def _chunk_consts(length=CHUNK):
    r2 = lax.broadcasted_iota(jnp.int32, (length, length), 0)
    c2 = lax.broadcasted_iota(jnp.int32, (length, length), 1)
    tri = r2 >= c2
    return dict(tri=tri, tril_f=tri.astype(F32), triu_f=(r2 <= c2).astype(F32),
                lane=lax.broadcasted_iota(jnp.int32, (length, N_SMALL), 1),
                rowio=lax.broadcasted_iota(jnp.int32, (length, 1), 0),
                ones=jnp.ones((length, N_SMALL), F32))


def _valid_rows(block, c):
    row = block * TM + c * CHUNK + lax.broadcasted_iota(jnp.int32, (CHUNK, 1), 0)
    return row >= FIRST_VALID


def _col(x, lane, idx):
    return jnp.sum(jnp.where(lane == idx, x, 0.0), axis=1, keepdims=True)


def _last_row(x, rowio):
    return jnp.sum(jnp.where(rowio == rowio.shape[0] - 1, x, 0.0), axis=0, keepdims=True)


def _sum_all(x):
    return jnp.sum(jnp.sum(x, axis=1, keepdims=True), axis=0, keepdims=True)


def _headnorm_fwd(hm, gain, gate_act):
    rs = lax.rsqrt(jnp.mean(hm * hm, axis=1, keepdims=True) + EPS)
    return hm * rs * gain * gate_act


def _headnorm_bwd(dy, hm, gain, gate_act):
    rs = lax.rsqrt(jnp.mean(hm * hm, axis=1, keepdims=True) + EPS)
    xh = hm * rs
    dact = dy * xh * gain
    dgain = jnp.sum(dy * gate_act * xh, axis=0, keepdims=True)
    dxh = dy * gate_act * gain
    dhm = rs * (dxh - xh * jnp.mean(dxh * xh, axis=1, keepdims=True))
    return dhm, dact, dgain


def _mlstm_gates(sm, gbias, valid, k):
    pre = sm + gbias
    lf = jnp.where(valid, _logsig(pre), 0.0)
    b_all = _nn(k["tril_f"], lf, precision=HI)
    li_all = jnp.where(valid, pre, NEG)
    return pre, li_all, b_all


def _mlstm_open(h, qh, kh, c_st, li_all, b_all, k):
    lane = k["lane"]
    sel = jnp.where(lane == h, 1.0, 0.0) - jnp.where(lane == NH + h, 1.0, 0.0)
    x = jnp.where(lane < NH, li_all, jnp.where(lane < 2 * NH, b_all, 0.0))
    cb = c_st.astype(BF)
    return dict(ubc=_nt(sel, x, precision=HI), sim=_nt(qh, kh), cb=cb, cq=_nt(qh, cb))


def _mlstm_weights(h, f, qh, vh, li_all, b_all, n_row, m11, k):
    lane, tri, rowio = k["lane"], k["tri"], k["rowio"]
    b_col = _col(b_all, lane, NH + h)
    li_col = _col(li_all, lane, h)
    dmat = jnp.where(tri, b_col + f["ubc"], NEG)
    m_row = jnp.maximum(b_col + m11, jnp.max(dmat, axis=1, keepdims=True))
    e = jnp.exp(dmat - m_row)
    w_mat = e * f["sim"]
    a = jnp.exp(b_col + m11 - m_row)
    qf = qh.astype(F32)
    nq = jnp.sum(qf * n_row, axis=1, keepdims=True)
    g = _last_row(b_col, rowio)
    wlog = g - b_col + li_col
    m_new = jnp.maximum(g + m11, jnp.max(wlog, axis=0, keepdims=True))
    a_s = jnp.exp(g + m11 - m_new)
    w = jnp.exp(wlog - m_new)
    return dict(f, e=e, w_mat=w_mat, a=a, qf=qf, nq=nq, m_row=m_row, m_new=m_new, a_s=a_s, w=w,
                wv=_nn(w_mat.astype(BF), vh))


def _mlstm_out(f):
    num = f["a"] * f["cq"] + f["wv"]
    den = f["a"] * f["nq"] + jnp.sum(f["w_mat"], axis=1, keepdims=True)
    floor = jnp.exp(-f["m_row"])
    r = jnp.maximum(jnp.abs(den), floor)
    return dict(f, den=den, floor=floor, r=r, hm=num / r)


def _mlstm_fwd(qk, pbig, small, gbias, headg, name):
    tp = qk.shape[0]
    nb = tp // TM

    def body(qk_ref, v_ref, mo_ref, sm_ref, gb_ref, hg_ref, y_ref, cs_ref, ns_ref, c_scr, n_scr):
        blk = pl.program_id(0)

        @pl.when(blk == 0)
        def _():
            c_scr[...] = jnp.zeros_like(c_scr)
            n_scr[...] = jnp.zeros_like(n_scr)

        k = _chunk_consts()
        io8 = lax.broadcasted_iota(jnp.int32, (8, DQK), 0)

        def chunk(c, carry):
            r0 = pl.multiple_of(c * CHUNK, CHUNK)
            rows = pl.ds(r0, CHUNK)
            valid = _valid_rows(blk, c)
            _, li_all, b_all = _mlstm_gates(sm_ref[rows, :], gb_ref[...], valid, k)
            heads = range(NH)
            qs = [qk_ref[rows, h * DQK:(h + 1) * DQK] for h in heads]
            ks = [qk_ref[rows, NH * DQK + h * DQK:NH * DQK + (h + 1) * DQK] for h in heads]
            vs = [v_ref[rows, h * DV:(h + 1) * DV] for h in heads]
            cst = [c_scr[h] for h in heads]
            nrow = [n_scr[h, 0:1, :] for h in heads]
            m11 = [jnp.max(n_scr[h, 1:2, :], axis=1, keepdims=True) for h in heads]
            f = [_mlstm_open(h, qs[h], ks[h], cst[h], li_all, b_all, k) for h in heads]
            f = [_mlstm_weights(h, f[h], qs[h], vs[h], li_all, b_all, nrow[h], m11[h], k) for h in heads]
            wk = [f[h]["w"] * ks[h].astype(F32) for h in heads]
            kv = [_tn(vs[h], wk[h].astype(BF)) for h in heads]
            for h in heads:
                hm = _mlstm_out(f[h])["hm"]
                gate = _sigmoid(mo_ref[rows, h * DV:(h + 1) * DV].astype(F32))
                y_ref[rows, h * DV:(h + 1) * DV] = _headnorm_fwd(hm, hg_ref[:, h * DV:(h + 1) * DV], gate).astype(BF)
                cs_ref[c, h] = f[h]["cb"]
                ns_ref[c, h] = jnp.where(io8 == 0, nrow[h], jnp.where(io8 == 1, m11[h], 0.0))
                c_scr[h] = f[h]["a_s"] * cst[h] + kv[h]
                n_scr[h, 0:1, :] = f[h]["a_s"] * nrow[h] + jnp.sum(wk[h], axis=0, keepdims=True)
                n_scr[h, 1:2, :] = jnp.broadcast_to(f[h]["m_new"], (1, DQK))
            return carry

        lax.fori_loop(0, CPB, chunk, 0, unroll=2)

    return pl.pallas_call(
        body, grid=(nb,),
        in_specs=[_row_spec(D), _row_spec(D, CB_MV), _row_spec(D, CB_MO), _row_spec(N_SMALL), _full_spec((1, N_SMALL)), _full_spec((1, D))],
        out_specs=[_row_spec(D), pl.BlockSpec((CPB, NH, DV, DQK), lambda i: (i, 0, 0, 0)),
                   pl.BlockSpec((CPB, NH, 8, DQK), lambda i: (i, 0, 0, 0))],
        out_shape=(jax.ShapeDtypeStruct((tp, D), BF), jax.ShapeDtypeStruct((tp // CHUNK, NH, DV, DQK), BF),
                   jax.ShapeDtypeStruct((tp // CHUNK, NH, 8, DQK), F32)),
        scratch_shapes=[pltpu.VMEM((NH, DV, DQK), F32), pltpu.VMEM((NH, 8, DQK), F32)],
        compiler_params=_params(), name=name)(qk, pbig, pbig, small, gbias, headg)


def _mlstm_bwd(dy, qk, pbig, small, gbias, headg, cs, ns, dproj, name, ride=()):
    tp = qk.shape[0]
    nb = tp // TM
    nr = len(ride)

    def body(*refs):
        dy_ref, qk_ref, v_ref, mo_ref, sm_ref, gb_ref, hg_ref, cs_ref, ns_ref = refs[:9]
        ride_in = refs[10:10 + nr]
        dqk_ref, dproj_ref, dsm_ref, dgb_ref, dhg_ref = refs[10 + nr:15 + nr]
        ride_out = refs[15 + nr:15 + 2 * nr]
        dc_scr, dn_scr = refs[15 + 2 * nr:17 + 2 * nr]
        step = pl.program_id(0)
        blk = nb - 1 - step
        sent = _scatter_copies(ride_in, ride_out, *refs[17 + 2 * nr:]) if nr else []

        @pl.when(step == 0)
        def _():
            dc_scr[...] = jnp.zeros_like(dc_scr)
            dn_scr[...] = jnp.zeros_like(dn_scr)
            dgb_ref[...] = jnp.zeros_like(dgb_ref)
            dhg_ref[...] = jnp.zeros_like(dhg_ref)
            for cp in sent:
                cp.start()

        k = _chunk_consts()
        lane, rowio = k["lane"], k["rowio"]

        def chunk(cc, carry):
            c = CPB - 1 - cc
            r0 = pl.multiple_of(c * CHUNK, CHUNK)
            rows = pl.ds(r0, CHUNK)
            valid = _valid_rows(blk, c)
            pre, li_all, b_all = _mlstm_gates(sm_ref[rows, :], gb_ref[...], valid, k)
            dli_all = jnp.zeros((CHUNK, N_SMALL), F32)
            db_all = jnp.zeros((CHUNK, N_SMALL), F32)
            heads = range(NH)
            qs = [qk_ref[rows, h * DQK:(h + 1) * DQK] for h in heads]
            ks = [qk_ref[rows, NH * DQK + h * DQK:NH * DQK + (h + 1) * DQK] for h in heads]
            vs = [v_ref[rows, h * DV:(h + 1) * DV] for h in heads]
            cst = [cs_ref[c, h].astype(F32) for h in heads]
            nrow = [ns_ref[c, h, 0:1, :] for h in heads]
            m11 = [jnp.max(ns_ref[c, h, 1:2, :], axis=1, keepdims=True) for h in heads]
            f = [_mlstm_open(h, qs[h], ks[h], cst[h], li_all, b_all, k) for h in heads]
            f = [_mlstm_weights(h, f[h], qs[h], vs[h], li_all, b_all, nrow[h], m11[h], k) for h in heads]
            f = [_mlstm_out(f[h]) for h in heads]
            t = []
            for h in heads:
                gain = hg_ref[:, h * DV:(h + 1) * DV]
                gate = _sigmoid(mo_ref[rows, h * DV:(h + 1) * DV].astype(F32))
                dhm, dgate, dgain = _headnorm_bwd(dy_ref[rows, h * DV:(h + 1) * DV].astype(F32), f[h]["hm"], gain, gate)
                dproj_ref[rows, D + h * DV:D + (h + 1) * DV] = (dgate * gate * (1.0 - gate)).astype(BF)
                dhg_ref[:, h * DV:(h + 1) * DV] += dgain
                r, den = f[h]["r"], f[h]["den"]
                dnum = dhm / r
                dr = -jnp.sum(dhm * f[h]["hm"], axis=1, keepdims=True) / r
                dden = jnp.where(jnp.abs(den) > f[h]["floor"], dr * jnp.sign(den), 0.0)
                dnb = dnum.astype(BF)
                dc_new = dc_scr[h]
                dcb = dc_new.astype(BF)
                t.append(dict(dnum=dnum, dden=dden, dnb=dnb, dc_new=dc_new, dn_new=dn_scr[h],
                              dwm=_nt(dnb, vs[h]), vdc=_nn(vs[h], dcb), kdc=_nt(ks[h], dcb)))
            for h in heads:
                dw_mat = t[h]["dwm"] + t[h]["dden"]
                dsim = (f[h]["e"] * dw_mat).astype(BF)
                gm = f[h]["w_mat"] * dw_mat
                t[h].update(gm=gm, dv0=_tn(f[h]["w_mat"].astype(BF), t[h]["dnb"]), dq0=_nn(dsim, ks[h]),
                            dq1=_nn(t[h]["dnb"], f[h]["cb"]), dk0=_tn(dsim, qs[h]),
                            dcq=_tn((f[h]["a"] * t[h]["dnum"]).astype(BF), qs[h]), cs2=_tn(gm, k["ones"], precision=HI))
            for h in heads:
                a, w, a_s = f[h]["a"], f[h]["w"], f[h]["a_s"]
                dnum, dden, dc_new, dn_new, vdc, gm = (t[h][n] for n in ("dnum", "dden", "dc_new", "dn_new", "vdc", "gm"))
                kf = ks[h].astype(F32)
                dproj_ref[rows, h * DV:(h + 1) * DV] = (t[h]["dv0"] + w * t[h]["kdc"]).astype(BF)
                adden = a * dden
                dqk_ref[rows, h * DQK:(h + 1) * DQK] = t[h]["dq0"] + a * t[h]["dq1"] + adden * nrow[h]
                dqk_ref[rows, NH * DQK + h * DQK:NH * DQK + (h + 1) * DQK] = t[h]["dk0"] + w * vdc + w * dn_new
                da = jnp.sum(dnum * f[h]["cq"], axis=1, keepdims=True) + dden * f[h]["nq"]
                dw = jnp.sum(vdc * kf, axis=1, keepdims=True) + jnp.sum(kf * dn_new, axis=1, keepdims=True)
                da_s = _sum_all(dc_new * cst[h]) + jnp.sum(dn_new * nrow[h], axis=1, keepdims=True)
                wdw = w * dw
                rs = jnp.sum(gm, axis=1, keepdims=True)
                cs_col = _col(t[h]["cs2"], lane, 0)
                dg = a_s * da_s + jnp.sum(wdw, axis=0, keepdims=True)
                db = a * da + rs - cs_col - wdw + jnp.where(rowio == CHUNK - 1, dg, 0.0)
                dli_all = dli_all + jnp.where(lane == h, cs_col + wdw, 0.0)
                db_all = db_all + jnp.where(lane == NH + h, db, 0.0)
                dc_scr[h] = a_s * dc_new + t[h]["dcq"]
                dn_scr[h] = a_s * dn_new + jnp.sum(adden * f[h]["qf"], axis=0, keepdims=True)
            dlf_all = _nn(k["triu_f"], db_all, precision=HI)
            dsm = jnp.where(valid, dli_all + dlf_all * _sigmoid(-pre), 0.0)
            dsm = jnp.where(lane < 2 * NH, dsm, 0.0)
            dsm_ref[rows, :] = dsm
            dgb_ref[0:1, :] += jnp.sum(dsm, axis=0, keepdims=True)
            return carry

        lax.fori_loop(0, CPB, chunk, 0, unroll=2)

        if nr:
            @pl.when(step == nb - 1)
            def _():
                for cp in sent:
                    cp.wait_recv()
                for cp in sent:
                    cp.wait_send()

    rev = lambda col: (lambda i: (nb - 1 - i, col))
    rspec = lambda width, col=0: pl.BlockSpec((TM, width), rev(col))
    ride_shapes, ride_sems = _scatter_shapes(ride) if nr else ((), [])
    outs = pl.pallas_call(
        body, grid=(nb,),
        in_specs=[rspec(D), rspec(D), rspec(D, CB_MV), rspec(D, CB_MO), rspec(N_SMALL), _full_spec((1, N_SMALL)), _full_spec((1, D)),
                  pl.BlockSpec((CPB, NH, DV, DQK), lambda i: (nb - 1 - i, 0, 0, 0)),
                  pl.BlockSpec((CPB, NH, 8, DQK), lambda i: (nb - 1 - i, 0, 0, 0)), ANY] + [ANY] * nr,
        out_specs=[rspec(D), rspec(2 * D, CB_MV // 2), rspec(N_SMALL), _full_spec((8, N_SMALL)), _full_spec((1, D))] + [ANY] * nr,
        out_shape=(jax.ShapeDtypeStruct((tp, D), F32), jax.ShapeDtypeStruct(dproj.shape, BF),
                   jax.ShapeDtypeStruct((tp, N_SMALL), F32), jax.ShapeDtypeStruct((8, N_SMALL), F32),
                   jax.ShapeDtypeStruct((1, D), F32)) + tuple(ride_shapes),
        scratch_shapes=[pltpu.VMEM((NH, DV, DQK), F32), pltpu.VMEM((NH, 1, DQK), F32)] + ride_sems,
        input_output_aliases={9: 1}, compiler_params=_params(), name=name)(dy, qk, pbig, pbig, small, gbias, headg, cs, ns, dproj, *ride)
    return tuple(outs[:5]) + (list(outs[5:]),)


def _gla_loga(sm_ref, a2_ref, a2b_ref, blk):
    za = _nn(sm_ref[...].astype(BF), a2_ref[...]) + a2b_ref[...]
    row = blk * TM + lax.broadcasted_iota(jnp.int32, (TM, 1), 0)
    return za, jnp.where(row >= FIRST_VALID, _logsig(za) / G_TAU, 0.0)


def _gla_head(h, q_ref, k_ref, rows, bc, btot, k):
    sl = slice(h * DQK, (h + 1) * DQK)
    bch = bc[:, sl]
    bth = btot[:, sl]
    gq = q_ref[rows, h * DQK:(h + 1) * DQK].astype(F32)
    gk = k_ref[rows, NH * DQK + h * DQK:NH * DQK + (h + 1) * DQK].astype(F32)
    e_pos = jnp.exp(bch) * (DQK ** -0.5)
    e_neg = jnp.exp(-bch)
    e_end = jnp.exp(bth - bch)
    qd = gq * e_pos
    ki = gk * e_neg
    ke = gk * e_end
    att = jnp.where(k["tri"], _nt(qd.astype(BF), ki.astype(BF)), 0.0)
    return dict(e_pos=e_pos, e_neg=e_neg, e_end=e_end, qd=qd, ki=ki, ke=ke, att=att, decay=jnp.exp(bth))


def _gla_fwd(pbig, small, a2p, a2b, headg, name):
    tp = pbig.shape[0]
    nb = tp // TM

    def body(qk_ref, v_ref, gr_ref, sm_ref, a2_ref, a2b_ref, hg_ref, y_ref, ss_ref, s_scr, lg_scr):
        blk = pl.program_id(0)

        @pl.when(blk == 0)
        def _():
            s_scr[...] = jnp.zeros_like(s_scr)

        k = _chunk_consts(G_CHUNK)
        _, loga = _gla_loga(sm_ref, a2_ref, a2b_ref, blk)
        lg_scr[...] = loga

        def chunk(c, carry):
            r0 = pl.multiple_of(c * G_CHUNK, G_CHUNK)
            rows = pl.ds(r0, G_CHUNK)
            bc = _nn(k["tril_f"], lg_scr[rows, :], precision=HI)
            btot = _last_row(bc, k["rowio"])
            heads = range(NH)
            f = [_gla_head(h, qk_ref, qk_ref, rows, bc, btot, k) for h in heads]
            vs = [v_ref[rows, h * DV:(h + 1) * DV] for h in heads]
            sst = [s_scr[h] for h in heads]
            sbs = [s.astype(BF) for s in sst]
            inter = [_nt(f[h]["qd"].astype(BF), sbs[h]) for h in heads]
            intra = [_nn(f[h]["att"].astype(BF), vs[h]) for h in heads]
            kv = [_tn(vs[h], f[h]["ke"].astype(BF)) for h in heads]
            for h in heads:
                gr = gr_ref[rows, h * DV:(h + 1) * DV].astype(F32)
                y_ref[rows, h * DV:(h + 1) * DV] = _headnorm_fwd(intra[h] + inter[h], hg_ref[:, h * DV:(h + 1) * DV],
                                                                   gr * _sigmoid(gr)).astype(BF)
                ss_ref[c, h] = sbs[h]
                s_scr[h] = sst[h] * f[h]["decay"] + kv[h]
            return carry

        lax.fori_loop(0, G_CPB, chunk, 0, unroll=2)

    return pl.pallas_call(
        body, grid=(nb,),
        in_specs=[_row_spec(D, CB_GQK), _row_spec(D, CB_GV), _row_spec(D, CB_GR), _row_spec(N_SMALL),
                  _full_spec((N_SMALL, NH * DQK)), _full_spec((1, NH * DQK)), _full_spec((1, D))],
        out_specs=[_row_spec(D), pl.BlockSpec((G_CPB, NH, DV, DQK), lambda i: (i, 0, 0, 0))],
        out_shape=(jax.ShapeDtypeStruct((tp, D), BF), jax.ShapeDtypeStruct((tp // G_CHUNK, NH, DV, DQK), BF)),
        scratch_shapes=[pltpu.VMEM((NH, DV, DQK), F32), pltpu.VMEM((TM, NH * DQK), F32)],
        compiler_params=_params(), name=name)(pbig, pbig, pbig, small, a2p, a2b, headg)


def _gla_bwd(dy, pbig, small, a2p, a2b, headg, ss, dsm_m, dproj, name):
    tp = pbig.shape[0]
    nb = tp // TM
    nqk = NH * DQK

    def body(dy_ref, qk_ref, v_ref, gr_ref, sm_ref, a2_ref, a2b_ref, hg_ref, ss_ref, dsmm_ref, _,
             dproj_ref, dsm_ref, da2_ref, da2b_ref, dhg_ref, ds_scr, lg_scr, dza_scr):
        step = pl.program_id(0)
        blk = nb - 1 - step

        @pl.when(step == 0)
        def _():
            ds_scr[...] = jnp.zeros_like(ds_scr)
            da2_ref[...] = jnp.zeros_like(da2_ref)
            da2b_ref[...] = jnp.zeros_like(da2b_ref)
            dhg_ref[...] = jnp.zeros_like(dhg_ref)

        k = _chunk_consts(G_CHUNK)
        rowio = k["rowio"]
        za, loga = _gla_loga(sm_ref, a2_ref, a2b_ref, blk)
        lg_scr[...] = loga

        def chunk(cc, carry):
            c = G_CPB - 1 - cc
            r0 = pl.multiple_of(c * G_CHUNK, G_CHUNK)
            rows = pl.ds(r0, G_CHUNK)
            bc = _nn(k["tril_f"], lg_scr[rows, :], precision=HI)
            btot = _last_row(bc, rowio)
            heads = range(NH)
            f = [_gla_head(h, qk_ref, qk_ref, rows, bc, btot, k) for h in heads]
            vs = [v_ref[rows, h * DV:(h + 1) * DV] for h in heads]
            sbs = [ss_ref[c, h] for h in heads]
            qdb = [f[h]["qd"].astype(BF) for h in heads]
            attb = [f[h]["att"].astype(BF) for h in heads]
            inter = [_nt(qdb[h], sbs[h]) for h in heads]
            intra = [_nn(attb[h], vs[h]) for h in heads]
            dsn = [ds_scr[h] for h in heads]
            dsb = [d.astype(BF) for d in dsn]
            dke = [_nn(vs[h], dsb[h]) for h in heads]
            dv1 = [_nt(f[h]["ke"].astype(BF), dsb[h]) for h in heads]
            t = []
            for h in heads:
                gr = gr_ref[rows, h * DV:(h + 1) * DV].astype(F32)
                sg = _sigmoid(gr)
                gain = hg_ref[:, h * DV:(h + 1) * DV]
                do, dact, dgain = _headnorm_bwd(dy_ref[rows, h * DV:(h + 1) * DV].astype(F32), intra[h] + inter[h], gain, gr * sg)
                dproj_ref[rows, 2 * D + h * DV:2 * D + (h + 1) * DV] = (dact * sg * (1.0 + gr * (1.0 - sg))).astype(BF)
                dhg_ref[:, h * DV:(h + 1) * DV] += dgain
                dob = do.astype(BF)
                t.append(dict(dob=dob, datt=_nt(dob, vs[h]), dv0=_tn(attb[h], dob), dq1=_nn(dob, sbs[h]), dsq=_tn(dob, qdb[h])))
            for h in heads:
                datt = jnp.where(k["tri"], t[h]["datt"], 0.0).astype(BF)
                t[h].update(dq0=_nn(datt, f[h]["ki"].astype(BF)), dki=_tn(datt, qdb[h]))
            dbc_parts = []
            for h in heads:
                dqd = t[h]["dq0"] + t[h]["dq1"]
                dki = t[h]["dki"]
                dproj_ref[rows, D + h * DV:D + (h + 1) * DV] = (t[h]["dv0"] + dv1[h]).astype(BF)
                dproj_ref[rows, h * DQK:(h + 1) * DQK] = (dqd * f[h]["e_pos"]).astype(BF)
                dproj_ref[rows, nqk + h * DQK:nqk + (h + 1) * DQK] = (dki * f[h]["e_neg"] + dke[h] * f[h]["e_end"]).astype(BF)
                dke_ke = dke[h] * f[h]["ke"]
                dbtot = (jnp.sum(dke_ke, axis=0, keepdims=True)
                         + jnp.sum(dsn[h] * sbs[h].astype(F32), axis=0, keepdims=True) * f[h]["decay"])
                dbc_parts.append(dqd * f[h]["qd"] - dki * f[h]["ki"] - dke_ke + jnp.where(rowio == G_CHUNK - 1, dbtot, 0.0))
                ds_scr[h] = dsn[h] * f[h]["decay"] + t[h]["dsq"]
            dbc = jnp.concatenate(dbc_parts, axis=1)
            dza_scr[rows, :] = _nn(k["triu_f"], dbc, precision=HI)
            return carry

        lax.fori_loop(0, G_CPB, chunk, 0, unroll=2)
        row = blk * TM + lax.broadcasted_iota(jnp.int32, (TM, 1), 0)
        dza = jnp.where(row >= FIRST_VALID, dza_scr[...] * (_sigmoid(-za) / G_TAU), 0.0)
        dzb = dza.astype(BF)
        dsm_ref[...] = (_nt(dzb, a2_ref[...]) + dsmm_ref[...]).astype(BF)
        da2_ref[...] += _tn(sm_ref[...].astype(BF), dzb)
        da2b_ref[...] += jnp.sum(dza, axis=0, keepdims=True)

    rspec = lambda width, col=0: pl.BlockSpec((TM, width), lambda i: (nb - 1 - i, col))
    return pl.pallas_call(
        body, grid=(nb,),
        in_specs=[rspec(D), rspec(D, CB_GQK), rspec(D, CB_GV), rspec(D, CB_GR), rspec(N_SMALL),
                  _full_spec((N_SMALL, nqk)), _full_spec((1, nqk)), _full_spec((1, D)),
                  pl.BlockSpec((G_CPB, NH, DV, DQK), lambda i: (nb - 1 - i, 0, 0, 0)), rspec(N_SMALL), ANY],
        out_specs=[rspec(3 * D, 0), rspec(N_SMALL), _full_spec((N_SMALL, nqk)), _full_spec((1, nqk)), _full_spec((1, D))],
        out_shape=(jax.ShapeDtypeStruct(dproj.shape, BF),
                   jax.ShapeDtypeStruct((tp, N_SMALL), BF), jax.ShapeDtypeStruct((N_SMALL, nqk), F32),
                   jax.ShapeDtypeStruct((1, nqk), F32), jax.ShapeDtypeStruct((1, D), F32)),
        scratch_shapes=[pltpu.VMEM((NH, DV, DQK), F32), pltpu.VMEM((TM, nqk), F32), pltpu.VMEM((TM, nqk), F32)],
        input_output_aliases={10: 0}, compiler_params=_params(), name=name)(dy, pbig, pbig, pbig, small, a2p, a2b, headg, ss, dsm_m, dproj)


PIECE_BYTES = 1 << 20
MAX_PIECES = 32


def _place():
    return lax.axis_index("x"), lax.axis_index("y"), lax.axis_index("c")


def _piece_rows(rows, row_bytes, align):
    want = min(MAX_PIECES, max(1, -(-rows * row_bytes // PIECE_BYTES)))
    best = rows
    for k in range(1, want + 1):
        if rows % k == 0 and (rows // k) % align == 0:
            best = rows // k
    return best


def _remote(src, dst, send_sems, recv_sems, k, to):
    return pltpu.make_async_remote_copy(src_ref=src, dst_ref=dst, send_sem=send_sems.at[k], recv_sem=recv_sems.at[k],
                                        device_id=to, device_id_type=MESH)


def _all_gather_chips(p, name):
    rd = _gather_rider(p)

    def body(*refs):
        start, middle, finish = rd["make"](refs[:1], refs[1:2], refs[2:])
        start()
        middle()
        finish()

    return pl.pallas_call(body, in_specs=[ANY], out_specs=[ANY], out_shape=rd["out_shapes"], scratch_shapes=rd["sems"],
                          name=name)(p)[0]


def _gather_rider(p):
    r, n = p.shape
    rh = r // 2
    align = 32 // p.dtype.itemsize
    assert r % (2 * align) == 0
    cr = _piece_rows(rh, n * p.dtype.itemsize, align)

    def make(in_refs, out_refs, sem_refs):
        p_ref, o_ref = in_refs[0], out_refs[0]
        send_sems, recv_sems = sem_refs
        x, y, c = _place()
        chips = [(1 - x, y), (x, 1 - y), (1 - x, 1 - y)]
        sib = (x, y, 1 - c)

        def half(hc, piece=None):
            if piece is None:
                return pl.ds(pl.multiple_of(hc * rh, align), rh)
            return pl.ds(pl.multiple_of(hc * rh + piece * cr, align), cr)

        first = [_remote(p_ref.at[half(c)], o_ref.at[j, half(c)], send_sems, recv_sems, j, (*chip, c))
                 for j, chip in enumerate(chips)]
        passed = [[_remote(o_ref.at[j, half(c, i)], o_ref.at[j, half(c, i)], send_sems, recv_sems, 3 + j, sib)
                   for i in range(rh // cr)] for j in range(3)]
        blocks = [_remote(o_ref.at[j, half(c)], o_ref.at[j, half(1 - c)], send_sems, recv_sems, 3 + j, sib) for j in range(3)]

        def start():
            for cp in first:
                cp.start()

        def middle():
            for j, cp in enumerate(first):
                cp.wait_recv()
                for piece in passed[j]:
                    piece.start()

        def finish():
            for block in blocks:
                block.wait_send()
                block.wait_recv()
            for cp in first:
                cp.wait_send()

        return start, middle, finish

    return dict(inputs=[p], out_shapes=(jax.ShapeDtypeStruct((3, r, n), p.dtype),),
                sems=[pltpu.SemaphoreType.DMA((6,)), pltpu.SemaphoreType.DMA((6,))], make=make)


def _scatter_rider(items):
    out_shapes, sems = _scatter_shapes(items)

    def make(in_refs, out_refs, sem_refs):
        sent = _scatter_copies(in_refs, out_refs, *sem_refs)

        def start():
            for cp in sent:
                cp.start()

        def finish():
            for cp in sent:
                cp.wait_recv()
            for cp in sent:
                cp.wait_send()

        return start, (lambda: None), finish

    return dict(inputs=list(items), out_shapes=out_shapes, sems=sems, make=make)


def _by_chip(mine, others):
    me = 2 * lax.axis_index("x") + lax.axis_index("y")
    by_mask = jnp.stack([mine, others[1], others[0], others[2]])
    return [lax.dynamic_index_in_dim(by_mask, q ^ me, 0, keepdims=False) for q in range(4)]


def _swap_halves(items, name):
    k = len(items)

    def body(*refs):
        a_refs, got_refs = refs[:k], refs[k:2 * k]
        send_sems, recv_sems = refs[2 * k:]
        x, y, c = _place()
        sib = (x, y, 1 - c)
        for i, a in enumerate(items):
            _, r, n = a.shape
            rh = r // 2
            cr = _piece_rows(rh, n * a.dtype.itemsize, 8)
            for q in range(4):
                for t in range(rh // cr):
                    other = pl.ds(pl.multiple_of((1 - c) * rh + t * cr, 8), cr)
                    _remote(a_refs[i].at[q, other], got_refs[i].at[q, pl.ds(t * cr, cr)], send_sems, recv_sems, i, sib).start()
        for i, a in enumerate(items):
            block = _remote(a_refs[i].at[:, pl.ds(0, a.shape[1] // 2)], got_refs[i], send_sems, recv_sems, i, sib)
            block.wait_send()
            block.wait_recv()

    return pl.pallas_call(
        body, in_specs=[ANY] * k, out_specs=[ANY] * k,
        out_shape=tuple(jax.ShapeDtypeStruct((4, a.shape[1] // 2, a.shape[2]), a.dtype) for a in items),
        scratch_shapes=[pltpu.SemaphoreType.DMA((k,)), pltpu.SemaphoreType.DMA((k,))], name=name)(*items)


def _scatter_copies(s_refs, o_refs, send_sems, recv_sems):
    x, y, c = _place()
    chips = [(1 - x, y), (x, 1 - y), (1 - x, 1 - y)]
    copies = []
    for i in range(len(s_refs)):
        for j, (cx, cy) in enumerate(chips):
            for other in (0, 1):
                copies.append(_remote(s_refs[i].at[2 * cx + cy], o_refs[i].at[j, c], send_sems, recv_sems,
                                      COPIES_PER_ITEM * i + 2 * j + other, (cx, cy, (1 - c) if other else c)))
        copies.append(_remote(s_refs[i].at[2 * x + y], o_refs[i].at[3, c], send_sems, recv_sems,
                              COPIES_PER_ITEM * i + 6, (x, y, 1 - c)))
    return copies


COPIES_PER_ITEM = 7


def _scatter_shapes(items):
    k = len(items)
    return (tuple(jax.ShapeDtypeStruct((4, 2) + s.shape[1:], s.dtype) for s in items),
            [pltpu.SemaphoreType.DMA((COPIES_PER_ITEM * k,)), pltpu.SemaphoreType.DMA((COPIES_PER_ITEM * k,))])


def _scatter_chips(items, name):
    k = len(items)

    def body(*refs):
        sent = _scatter_copies(refs[:k], refs[k:2 * k], *refs[2 * k:])
        for cp in sent:
            cp.start()
        for cp in sent:
            cp.wait_recv()
        for cp in sent:
            cp.wait_send()

    out_shape, scratch = _scatter_shapes(items)
    return pl.pallas_call(body, in_specs=[ANY] * k, out_specs=[ANY] * k, out_shape=out_shape, scratch_shapes=scratch,
                          name=name)(*items)


SMALL_ROWS = 16
SMALL_GRAD_ROWS = 48
SMALL_SHARD_SHAPES = [(N_META, 256), (4, 256), (G_RANK, 128), (NH, 64), (NH, 64)]
REPL_SHAPES = [(1, D), (1, D), (1, 2, NH), (1, NH * DQK), (1, D), (D,)]
W_IN_SHARD = 2054
W_IN_BLOCK = 2080


def _pack_small(parts, rows=SMALL_ROWS):
    flat = jnp.concatenate([p.reshape(-1) for p in parts])
    return jnp.pad(flat, (0, rows * D - flat.shape[0])).reshape(rows, D)


def _unpack_small(block, shapes):
    flat, out, off = block.reshape(-1), [], 0
    for shp in shapes:
        n = 1
        for s in shp:
            n *= s
        out.append(flat[off:off + n].reshape(shp))
        off += n
    return out


def _proj_rows_from_w_in(w_in_t):
    w_big = jnp.concatenate([w_in_t[3080:5128], w_in_t[5144:6168], w_in_t[0:1024], w_in_t[6168:8216],
                             w_in_t[1024:2048], w_in_t[2056:3080]], axis=0)
    w_small = jnp.concatenate([w_in_t[2048:2056], w_in_t[5128:5144], jnp.zeros((N_SMALL - 24, D), w_in_t.dtype)], axis=0)
    return w_big, w_small


def _w_in_from_proj_rows(d_wall_t):
    big, small = d_wall_t[0:N_BIG], d_wall_t[N_BIG:N_ALL]
    return jnp.concatenate([big[3072:4096], big[6144:7168], small[0:8], big[7168:8192], big[0:2048],
                            small[8:24], big[2048:3072], big[4096:6144]], axis=0)


def kernel(x, meta_tokens, norm1_g, w_in, conv_w, conv_b, m_gate_b, g_a2, g_a2_b, m_head_g, g_head_g, w_branch_m, w_branch_g, w_out, norm2_g, w_ff_gate, w_ff_up, w_ff_down, final_g, loss_target, m_meta_tokens, m_norm1_g, m_w_in, m_conv_w, m_conv_b, m_m_gate_b, m_g_a2, m_g_a2_b, m_m_head_g, m_g_head_g, m_w_branch_m, m_w_branch_g, m_w_out, m_norm2_g, m_w_ff_gate, m_w_ff_up, m_w_ff_down, m_final_g, v_meta_tokens, v_norm1_g, v_w_in, v_conv_w, v_conv_b, v_m_gate_b, v_g_a2, v_g_a2_b, v_m_head_g, v_g_head_g, v_w_branch_m, v_w_branch_g, v_w_out, v_norm2_g, v_w_ff_gate, v_w_ff_up, v_w_ff_down, v_final_g):
    w = _gather_weights(w_in, w_branch_m, w_branch_g, w_out, w_ff_gate, w_ff_up, w_ff_down, meta_tokens, conv_w, g_a2, m_head_g, g_head_g)
    loss_local, dx, grads = _local_step(x[0], loss_target[0], w, norm1_g, conv_b, m_gate_b, g_a2_b, norm2_g, final_g, _Reducer())

    weights = [w_in, w_branch_m, w_branch_g, w_out, w_ff_gate, w_ff_up, w_ff_down, meta_tokens, conv_w, g_a2, m_head_g, g_head_g,
               norm1_g, conv_b, m_gate_b, g_a2_b, norm2_g, final_g]
    moms = [m_w_in, m_w_branch_m, m_w_branch_g, m_w_out, m_w_ff_gate, m_w_ff_up, m_w_ff_down, m_meta_tokens, m_conv_w, m_g_a2,
            m_m_head_g, m_g_head_g, m_norm1_g, m_conv_b, m_m_gate_b, m_g_a2_b, m_norm2_g, m_final_g]
    vels = [v_w_in, v_w_branch_m, v_w_branch_g, v_w_out, v_w_ff_gate, v_w_ff_up, v_w_ff_down, v_meta_tokens, v_conv_w, v_g_a2,
            v_m_head_g, v_g_head_g, v_norm1_g, v_conv_b, v_m_gate_b, v_g_a2_b, v_norm2_g, v_final_g]
    res = {}
    for nm, wt, g, m, v in zip(PACK_ORDER, weights, grads, moms, vels):
        if nm in TRANSPOSED_GRADS:
            to2d = lambda a: jnp.swapaxes(a, -1, -2).reshape(a.shape[-1], a.shape[-2])
            back = lambda a: jnp.swapaxes(a, 0, 1).reshape(wt.shape)
        else:
            to2d = lambda a: a.reshape(wt.size // wt.shape[-1], wt.shape[-1])
            back = lambda a: a.reshape(wt.shape)
        d, nm_, nv_ = _adamw(to2d(wt), g, to2d(m), to2d(v), "adamw_" + nm)
        res[nm] = (back(g), back(d), back(nm_), back(nv_))

    order = ["meta_tokens", "norm1_g", "w_in", "conv_w", "conv_b", "m_gate_b", "g_a2", "g_a2_b", "m_head_g", "g_head_g",
             "w_branch_m", "w_branch_g", "w_out", "norm2_g", "w_ff_gate", "w_ff_up", "w_ff_down", "final_g"]
    loss = lax.psum(loss_local[0, 0], ("x", "y", "c"))
    grad_x = dx.reshape(x.shape)
    return (loss, grad_x, *[res[n][0] for n in order], *[res[n][1] for n in order],
            *[res[n][2] for n in order], *[res[n][3] for n in order])


TRANSPOSED_GRADS = ("w_in", "w_ff_gate", "w_ff_up")
PACK_ORDER = ["w_in", "w_branch_m", "w_branch_g", "w_out", "w_ff_gate", "w_ff_up", "w_ff_down", "meta_tokens", "conv_w", "g_a2",
              "m_head_g", "g_head_g", "norm1_g", "conv_b", "m_gate_b", "g_a2_b", "norm2_g", "final_g"]


def _gather_weights(w_in, w_branch_m, w_branch_g, w_out, w_ff_gate, w_ff_up, w_ff_down, meta_tokens, conv_w, g_a2, m_head_g, g_head_g):
    bf = lambda a: a.astype(BF)
    rows_local = jnp.concatenate([bf(w_branch_m[0]), bf(w_branch_g[0]), bf(w_out[0]), bf(w_ff_down[0]),
                                  bf(w_ff_gate[0].T), bf(w_ff_up[0].T)], axis=0)
    win_local = jnp.pad(bf(w_in[0].T), ((0, W_IN_BLOCK - W_IN_SHARD), (0, 0)))
    small_local = _pack_small([meta_tokens, conv_w[0], g_a2[0], m_head_g[0], g_head_g[0]])
    small_all = _by_chip(small_local, _all_gather_chips(small_local, "gather_small"))
    small_sh = [_unpack_small(small_all[q], SMALL_SHARD_SHAPES) for q in range(4)]
    cat = lambda i: jnp.concatenate([s[i] for s in small_sh], axis=-1)
    return dict(win_local=win_local, rows_local=rows_local, meta=cat(0), convw=cat(1), ga2=cat(2),
                mhg=cat(3).reshape(1, D), ghg=cat(4).reshape(1, D))


def _row_weights(rows_local, gathered):
    rows_all = jnp.stack(_by_chip(rows_local, gathered))
    cut = lambda lo, hi: rows_all[:, lo:hi].reshape(4 * (hi - lo), D)
    return cut(0, 256), cut(256, 512), cut(512, 768), cut(768, 1472), _ffn_weight_rows(cut(1472, 2176), cut(2176, 2880))


def _local_step(x0, target, w, norm1_g, conv_b, m_gate_b, g_a2_b, norm2_g, final_g, reducer):
    meta_f, convw_f, ga2_f, mhg_f, ghg_f = w["meta"], w["convw"], w["ga2"], w["mhg"], w["ghg"]
    gbias =jnp.concatenate([m_gate_b.reshape(1, 2 * NH), jnp.zeros((1, N_SMALL - 2 * NH), F32)], axis=1)
    a2p = jnp.concatenate([jnp.zeros((8, NH * DQK), F32), ga2_f, jnp.zeros((N_SMALL - 24, NH * DQK), F32)], axis=0).astype(BF)
    convb = conv_b.reshape(1, D)
    g1 = norm1_g.reshape(1, D)
    g2 = norm2_g.reshape(1, D)
    gf = final_g.reshape(1, D)
    first = jnp.concatenate([jnp.zeros((FIRST_VALID, D), F32), meta_f], axis=0)

    h0, xn1, rstd1, win_gathered = _embed_norm(x0, first, g1, _gather_rider(w["win_local"]), "rms1")
    win_all = _by_chip(w["win_local"], win_gathered[0])
    w_in_f = jnp.concatenate([win_all[q][0:W_IN_SHARD] for q in range(4)], axis=0)
    w_big, w_small = _proj_rows_from_w_in(w_in_f)
    w_all = jnp.concatenate([w_big, w_small], axis=0)
    pbig, rows_gathered = _mm(xn1, w_big, nt=True, out_dtype=BF, tn=2 * D, name="proj_big", rider=_gather_rider(w["rows_local"]))
    wbm, wbg, wout, wdown, wgu_t = _row_weights(w["rows_local"], rows_gathered[0])
    small = _mm(xn1, w_small, nt=True, out_dtype=F32, tn=N_SMALL, name="proj_small")
    qk = _conv_fwd(pbig, convw_f, convb, "conv_fwd")
    y_m, m_cs, m_ns = _mlstm_fwd(qk, pbig, small, gbias, mhg_f, "mlstm_fwd")
    y_g, g_ss = _gla_fwd(pbig, small, a2p, g_a2_b, ghg_f, "gla_fwd")
    p_m, p_g, merged = _branch_merge(y_m, y_g, wbm, wbg, pbig, "branch_merge")
    h1, hn, rstd2 = _out_proj_norm(merged, wout, h0, g2, "out_proj")
    gu, ff = _ffn_in(hn, wgu_t, "ff_in")
    dh2, loss_local, d_final_g = _ffn_down_loss(ff, wdown, h1, target, gf, "ff_down_loss")

    d_wdown = _mm_tn(ff, dh2, tm=FF_TN, tn=D, name="dw_ff_down")
    dgu = _ffn_d_hidden(dh2, wdown, gu, "d_ff")
    d_wgu_t = _mm_tn(dgu, hn, tm=FF_TN, tn=D, name="dw_ff_in")
    dh1, d_g2 = _ffn_d_in(dgu, wgu_t, h1, rstd2, g2, dh2, "d_hn")
    d_wout = _mm_tn(merged, dh1, tm=D, tn=D, name="dw_out")
    dp_m, dp_g, dproj = _merge_d(dh1, wout, p_m, p_g, pbig, "d_merged")
    dy_m = _mm(dp_m, wbm, nt=True, out_dtype=BF, tn=D, name="d_ym")
    dy_g = _mm(dp_g, wbg, nt=True, out_dtype=BF, tn=D, name="d_yg")
    d_wbm = _mm_tn(y_m, dp_m, tm=D, tn=D, name="dw_branch_m")
    d_wbg = _mm_tn(y_g, dp_g, tm=D, tn=D, name="dw_branch_g")
    fq = D_FF // 4
    gu4 = jnp.transpose(d_wgu_t.reshape(2, 2, 2, fq, D), (0, 2, 1, 3, 4)).reshape(4, 2 * fq, D)
    sq4 = jnp.concatenate([d_wbm.reshape(4, 256, D), d_wbg.reshape(4, 256, D), d_wout.reshape(4, 256, D)], axis=1)
    sums_a = reducer.partial_sums([sq4, d_wdown.reshape(4, fq, D), gu4], BF, "a")
    dqk_m, dproj, dsm_m, d_gbias, d_mhg, recv_a = _mlstm_bwd(dy_m, qk, pbig, small, gbias, mhg_f, m_cs, m_ns, dproj,
                                                              "mlstm_bwd", ride=sums_a)
    dproj, d_convwb = _conv_bwd(dqk_m, pbig, convw_f, convb, dproj, "conv_bwd")
    dproj, dsmall, d_a2p, d_a2b, d_ghg = _gla_bwd(dy_g, pbig, small, a2p, g_a2_b, ghg_f, g_ss, dsm_m, dproj, "gla_bwd")
    dproj = _place_small(dsmall, dproj, "dproj_small")
    d_win = _w_in_from_proj_rows(_mm_tn(dproj, xn1, tm=PROJ_TK, tn=D, name="dw_in"))
    pad = jnp.zeros((W_IN_BLOCK - W_IN_SHARD, D), F32)
    win4 = jnp.stack([jnp.concatenate([d_win[q * W_IN_SHARD:(q + 1) * W_IN_SHARD], pad], axis=0) for q in range(4)])
    sums_b = reducer.partial_sums([win4], BF, "b")
    dxn, recv_b = _mm(dproj, w_all, nt=False, out_dtype=F32, tn=D, tm=TM, name="d_xn", rider=_scatter_rider(sums_b),
                      single_buffer_b=True)
    dh_first, dx, d_g1 = _rms_bwd(dxn, h0, rstd1, g1, dh1, "rms1_bwd", split_first=True)

    small_sharded = [dh_first[FIRST_VALID:TM], d_convwb[0:4], d_a2p[8:24], d_mhg.reshape(NH, DV), d_ghg.reshape(NH, DV)]
    replicated = [d_g1, d_convwb[4:5], d_gbias[0:1, 0:2 * NH].reshape(1, 2, NH), d_a2b, d_g2, d_final_g.reshape(D)]
    small4 = jnp.broadcast_to(_pack_small(small_sharded + replicated, SMALL_GRAD_ROWS)[None], (4, SMALL_GRAD_ROWS, D))
    sums_c = reducer.partial_sums([small4], F32, "c")
    recv_c = reducer.scatter(sums_c, "c")
    sq, down, gu, win, smalls = reducer.finish(sums_a + sums_b + sums_c, recv_a + recv_b + recv_c, in_chip_order=[4])
    smalls = _unpack_small(smalls, [g.shape for g in small_sharded + replicated])
    me = 2 * lax.axis_index("x") + lax.axis_index("y")
    smalls = ([lax.dynamic_slice_in_dim(g, me * shp[1], shp[1], axis=1) for g, shp in zip(smalls, SMALL_SHARD_SHAPES)]
              + smalls[len(SMALL_SHARD_SHAPES):])
    grads = ([win[0:W_IN_SHARD], sq[0:256], sq[256:512], sq[512:768], gu[0:fq], gu[fq:2 * fq], down]
             + [g.reshape(g.size // g.shape[-1], g.shape[-1]) for g in smalls])
    return loss_local, dx, grads


class _Reducer:
    def partial_sums(self, items, dtype, tag):
        c = lax.axis_index("c")
        got = _swap_halves(items, "reduce_siblings_" + tag)
        sums = []
        for i, (a, g) in enumerate(zip(items, got)):
            rh, n = g.shape[1], g.shape[2]
            own = lax.dynamic_slice_in_dim(a, c * rh, rh, axis=1)
            sums.append(_add2(own.reshape(-1, n), g.reshape(-1, n), dtype, f"reduce_add2_{tag}{i}").reshape(g.shape))
        return sums

    def scatter(self, sums, tag):
        return list(_scatter_chips(sums, "reduce_chips_" + tag))

    def finish(self, sums, from_chips, in_chip_order):
        c = lax.axis_index("c")
        me = 2 * lax.axis_index("x") + lax.axis_index("y")
        full = []
        for i, (s, r) in enumerate(zip(sums, from_chips)):
            rh, n = s.shape[1], s.shape[2]
            mine = lax.dynamic_index_in_dim(s, me, 0, keepdims=False)
            sib = lax.dynamic_index_in_dim(r[3], 1 - c, 0, keepdims=False)
            own = jnp.concatenate([jnp.where(c == 0, mine, sib), jnp.where(c == 0, sib, mine)], axis=0)
            others = r[0:3].reshape(3, 2 * rh, n)
            if i in in_chip_order:
                by_chip = _by_chip(own, others)
                own, others = by_chip[0], jnp.stack(by_chip[1:])
            full.append(_add4(own, others, f"reduce_add4_{i}"))
        return full
```

```python
import jax
import jax.numpy as jnp
from jax import lax
from jax.experimental import pallas as pl
from jax.experimental.pallas import tpu as pltpu

F32 = jnp.float32
BF = jnp.bfloat16
HI = lax.Precision.HIGHEST
MESH = pl.DeviceIdType.MESH

D = 1024
N_META = 16
CHUNK = 128
EPS = 1e-6
NH = 4
DV = 256
DQK = 128
G_RANK = 16
G_TAU = 16.0
D_FF = 2816
TM = 512
FIRST_VALID = TM - N_META
CPB = TM // CHUNK
G_CHUNK = 256
G_CPB = TM // G_CHUNK
NEG = -1e30
N_BIG = 8192
CB_GQK, CB_GV, CB_GR, CB_MQK, CB_GM, CB_GG, CB_MV, CB_MO = range(8)
N_SMALL = 128
N_ALL = N_BIG + N_SMALL
PROJ_TK = N_ALL // 5
VMEM_LIMIT = 56 * 1024 * 1024

ADAM_LR, ADAM_B1, ADAM_B2, ADAM_EPS, ADAM_WD, ADAM_STEP = 0.001, 0.9, 0.999, 1e-08, 0.01, 10

NT_DIMS = (((1,), (1,)), ((), ()))
TN_DIMS = (((0,), (0,)), ((), ()))


def _nt(a, b, **kw):
    return lax.dot_general(a, b, NT_DIMS, preferred_element_type=F32, **kw)


def _tn(a, b, **kw):
    return lax.dot_general(a, b, TN_DIMS, preferred_element_type=F32, **kw)


def _nn(a, b, **kw):
    return jnp.dot(a, b, preferred_element_type=F32, **kw)


def _params(**kw):
    return pltpu.CompilerParams(vmem_limit_bytes=VMEM_LIMIT, **kw)


def _sigmoid(x):
    return 0.5 * jnp.tanh(0.5 * x) + 0.5


def _logsig(x):
    return jnp.minimum(x, 0.0) - jnp.log(1.0 + jnp.exp(-jnp.abs(x)))


def _mm_rows(rows):
    return 3 * TM if rows % (3 * TM) == 0 else TM


def _mm(a, b, *, nt, out_dtype, tn, tk=None, tm=None, name, rider=None, single_buffer_b=False):
    m, k = a.shape
    n = b.shape[0] if nt else b.shape[1]
    tk = k if tk is None else tk
    tm = _mm_rows(m) if tm is None else tm
    nk = k // tk
    nj, ni = n // tn, m // tm
    nr_in = len(rider["inputs"]) if rider else 0
    nr_out = len(rider["out_shapes"]) if rider else 0
    assert m % tm == 0 and n % tn == 0 and k % tk == 0
    dims = NT_DIMS if nt else (((1,), (0,)), ((), ()))

    def body(*refs):
        a_ref, b_ref = refs[:2]
        o_ref = refs[2 + nr_in]
        j, i, kk = pl.program_id(0), pl.program_id(1), pl.program_id(2)
        step = (j * ni + i) * nk + kk
        if rider:
            start, middle, finish = rider["make"](refs[2:2 + nr_in], refs[3 + nr_in:3 + nr_in + nr_out],
                                                  refs[3 + nr_in + nr_out:5 + nr_in + nr_out])
            pl.when(step == 0)(start)
            pl.when(step == (nj * ni * nk) // 2)(middle)

        part = lax.dot_general(a_ref[...].astype(BF), b_ref[...].astype(BF), dims, preferred_element_type=F32)
        if nk == 1:
            o_ref[...] = part.astype(o_ref.dtype)
        else:
            acc_ref = refs[-1]

            @pl.when(kk == 0)
            def _():
                acc_ref[...] = part

            @pl.when(jnp.logical_and(kk > 0, kk < nk - 1))
            def _():
                acc_ref[...] += part

            @pl.when(kk == nk - 1)
            def _():
                o_ref[...] = (acc_ref[...] + part).astype(o_ref.dtype)

        if rider:
            pl.when(step == nj * ni * nk - 1)(finish)

    outs = pl.pallas_call(
        body, grid=(nj, ni, nk),
        in_specs=[pl.BlockSpec((tm, tk), lambda j, i, kk: (i, kk)),
                  pl.BlockSpec((tn, tk) if nt else (tk, tn), (lambda j, i, kk: (j, kk)) if nt else (lambda j, i, kk: (kk, j)),
                               pipeline_mode=pl.Buffered(1) if single_buffer_b else None)]
                 + [ANY] * nr_in,
        out_specs=[pl.BlockSpec((tm, tn), lambda j, i, kk: (i, j))] + [ANY] * nr_out,
        out_shape=(jax.ShapeDtypeStruct((m, n), out_dtype),) + (tuple(rider["out_shapes"]) if rider else ()),
        scratch_shapes=(rider["sems"] if rider else []) + ([pltpu.VMEM((tm, tn), F32)] if nk > 1 else []),
        compiler_params=_params(), name=name)(a, b, *(rider["inputs"] if rider else []))
    return (outs[0], list(outs[1:])) if rider else outs[0]


def _mm_tn(a, b, *, tm, tn, tk=None, name):
    t, m = a.shape
    n = b.shape[1]
    tk = _mm_rows(t) if tk is None else tk
    assert t % tk == 0 and m % tm == 0 and n % tn == 0

    def body(a_ref, b_ref, o_ref):
        part = _tn(a_ref[...].astype(BF), b_ref[...].astype(BF))

        @pl.when(pl.program_id(2) == 0)
        def _():
            o_ref[...] = part

        @pl.when(pl.program_id(2) > 0)
        def _():
            o_ref[...] += part

    return pl.pallas_call(
        body, grid=(m // tm, n // tn, t // tk),
        in_specs=[pl.BlockSpec((tk, tm), lambda i, j, kk: (kk, i)), pl.BlockSpec((tk, tn), lambda i, j, kk: (kk, j))],
        out_specs=pl.BlockSpec((tm, tn), lambda i, j, kk: (i, j)),
        out_shape=jax.ShapeDtypeStruct((m, n), F32), compiler_params=_params(), name=name)(a, b)


ANY = pl.BlockSpec(memory_space=pl.ANY)


def _row_spec(width, col=0):
    return pl.BlockSpec((TM, width), lambda i: (i, col))


def _full_spec(shape):
    return pl.BlockSpec(shape, lambda i: (0,) * len(shape))


def _embed_norm(x0, first, g, rider, name):
    tp = x0.shape[0] + TM
    nb = tp // TM
    nri, nro = len(rider["inputs"]), len(rider["out_shapes"])

    def body(*refs):
        x_ref, f_ref, g_ref = refs[:3]
        h_ref, xn_ref, r_ref = refs[3 + nri:6 + nri]
        i = pl.program_id(0)
        start, middle, finish = rider["make"](refs[3:3 + nri], refs[6 + nri:6 + nri + nro], refs[6 + nri + nro:])
        pl.when(i == 0)(start)
        pl.when(i == nb // 2)(middle)
        x = jnp.where(i == 0, f_ref[...], x_ref[...])
        r = lax.rsqrt(jnp.mean(x * x, axis=1, keepdims=True) + EPS)
        h_ref[...] = x
        xn_ref[...] = (x * r * g_ref[...]).astype(BF)
        r_ref[...] = r
        pl.when(i == nb - 1)(finish)

    outs = pl.pallas_call(
        body, grid=(nb,),
        in_specs=[pl.BlockSpec((TM, D), lambda i: (jnp.maximum(i - 1, 0), 0)), _full_spec((TM, D)), _full_spec((1, D))] + [ANY] * nri,
        out_specs=[_row_spec(D), _row_spec(D), _row_spec(1)] + [ANY] * nro,
        out_shape=(jax.ShapeDtypeStruct((tp, D), F32), jax.ShapeDtypeStruct((tp, D), BF), jax.ShapeDtypeStruct((tp, 1), F32))
                  + tuple(rider["out_shapes"]),
        scratch_shapes=rider["sems"], compiler_params=_params(), name=name)(x0, first, g, *rider["inputs"])
    return outs[0], outs[1], outs[2], list(outs[3:])


def _rms_bwd(dxn, h, rstd, g, dres, name, split_first=False):
    tp = h.shape[0]

    def body(dxn_ref, h_ref, r_ref, g_ref, dres_ref, *outs):
        r = r_ref[...]
        xh = h_ref[...] * r
        dxn_v = dxn_ref[...].astype(F32)
        dxh = dxn_v * g_ref[...]
        dh = r * (dxh - xh * jnp.mean(dxh * xh, axis=1, keepdims=True)) + dres_ref[...]
        if split_first:
            first_ref, dh_ref, dg_ref = outs

            @pl.when(pl.program_id(0) == 0)
            def _():
                first_ref[...] = dh
        else:
            dh_ref, dg_ref = outs
        dh_ref[...] = dh
        part = jnp.sum(dxn_v * xh, axis=0, keepdims=True)

        @pl.when(pl.program_id(0) == 0)
        def _():
            dg_ref[...] = part

        @pl.when(pl.program_id(0) > 0)
        def _():
            dg_ref[...] += part

    if split_first:
        out_specs = [_full_spec((TM, D)), pl.BlockSpec((TM, D), lambda i: (jnp.maximum(i - 1, 0), 0)), _full_spec((1, D))]
        out_shape = (jax.ShapeDtypeStruct((TM, D), F32), jax.ShapeDtypeStruct((tp - TM, D), F32), jax.ShapeDtypeStruct((1, D), F32))
    else:
        out_specs = [_row_spec(D), _full_spec((1, D))]
        out_shape = (jax.ShapeDtypeStruct((tp, D), F32), jax.ShapeDtypeStruct((1, D), F32))
    return pl.pallas_call(
        body, grid=(tp // TM,),
        in_specs=[_row_spec(D), _row_spec(D), _row_spec(1), _full_spec((1, D)), _row_spec(D)],
        out_specs=out_specs, out_shape=out_shape, compiler_params=_params(), name=name)(dxn, h, rstd, g, dres)


def _shift_down(x, halo, k):
    rk = pltpu.roll(x, k, 0)
    io = lax.broadcasted_iota(jnp.int32, (8, x.shape[1]), 0)
    top = jnp.where(io < k, pltpu.roll(halo, k, 0), rk[0:8])
    return top if x.shape[0] == 8 else jnp.concatenate([top, rk[8:]], axis=0)


def _shift_up(x, nxt, k):
    n = x.shape[0]
    rk = pltpu.roll(x, n - k, 0)
    io = lax.broadcasted_iota(jnp.int32, (8, x.shape[1]), 0)
    bot = jnp.where(io >= 8 - k, pltpu.roll(nxt, 8 - k, 0), rk[n - 8:n])
    return jnp.concatenate([rk[:n - 8], bot], axis=0)


def _conv_pre(x, halo, w_ref, b_ref):
    c = x * w_ref[3:4, :] + b_ref[...]
    shifted = []
    for k in (1, 2, 3):
        s = _shift_down(x, halo, k)
        shifted.append(s)
        c = c + s * w_ref[3 - k:4 - k, :]
    return c, shifted


def _qk_scale():
    col = lax.broadcasted_iota(jnp.int32, (1, D), 1)
    return jnp.where(col < NH * DQK, DQK ** -0.5, 1.0).astype(F32)


def _halo_prev_spec():
    return pl.BlockSpec((8, D), lambda i: (jnp.maximum(i * (TM // 8) - 1, 0), CB_MQK))


def _conv_fwd(pbig, w, b, name):
    tp = pbig.shape[0]

    def body(x_ref, halo_ref, w_ref, b_ref, o_ref):
        x = x_ref[...].astype(F32)
        halo = jnp.where(pl.program_id(0) > 0, halo_ref[...].astype(F32), 0.0)
        c, _ = _conv_pre(x, halo, w_ref, b_ref)
        o_ref[...] = (c * _sigmoid(c) * _qk_scale()).astype(BF)

    return pl.pallas_call(
        body, grid=(tp // TM,),
        in_specs=[_row_spec(D, CB_MQK), _halo_prev_spec(), _full_spec((4, D)), _full_spec((1, D))],
        out_specs=_row_spec(D), out_shape=jax.ShapeDtypeStruct((tp, D), BF),
        compiler_params=_params(), name=name)(pbig, pbig, w, b)


def _conv_bwd(dqk, pbig, w, b, dproj, name):
    tp = pbig.shape[0]
    nb = tp // TM

    def d_conv_out(d, x, halo, w_ref, b_ref):
        c, shifted = _conv_pre(x, halo, w_ref, b_ref)
        sg = _sigmoid(c)
        return d * _qk_scale() * (sg * (1.0 + c * (1.0 - sg))), shifted

    def body(d_ref, dn_ref, x_ref, halo_ref, xn_ref, w_ref, b_ref, _, o_ref, dwb_ref):
        i = pl.program_id(0)
        x = x_ref[...].astype(F32)
        halo = jnp.where(i > 0, halo_ref[...].astype(F32), 0.0)
        dc, shifted = d_conv_out(d_ref[...], x, halo, w_ref, b_ref)
        dc_next, _ = d_conv_out(dn_ref[...], xn_ref[...].astype(F32), x[TM - 8:TM], w_ref, b_ref)
        nxt = jnp.where(i < nb - 1, dc_next, 0.0)
        acc = dc * w_ref[3:4, :]
        for k in (1, 2, 3):
            acc = acc + _shift_up(dc, nxt, k) * w_ref[3 - k:4 - k, :]
        o_ref[...] = acc.astype(BF)
        taps = [shifted[2], shifted[1], shifted[0], x]
        rows = [jnp.sum(dc * t, axis=0, keepdims=True) for t in taps] + [jnp.sum(dc, axis=0, keepdims=True)]
        io = lax.broadcasted_iota(jnp.int32, (8, D), 0)
        part = jnp.zeros((8, D), F32)
        for r, v in enumerate(rows):
            part = jnp.where(io == r, v, part)

        @pl.when(pl.program_id(0) == 0)
        def _():
            dwb_ref[...] = part

        @pl.when(pl.program_id(0) > 0)
        def _():
            dwb_ref[...] += part

    next8 = lambda col: pl.BlockSpec((8, D), lambda i: (jnp.minimum((i + 1) * (TM // 8), tp // 8 - 1), col))
    return pl.pallas_call(
        body, grid=(nb,),
        in_specs=[_row_spec(D), next8(0), _row_spec(D, CB_MQK), _halo_prev_spec(), next8(CB_MQK),
                  _full_spec((4, D)), _full_spec((1, D)), ANY],
        out_specs=[_row_spec(D, CB_MQK), _full_spec((8, D))],
        out_shape=(jax.ShapeDtypeStruct(dproj.shape, BF), jax.ShapeDtypeStruct((8, D), F32)),
        input_output_aliases={7: 0}, compiler_params=_params(), name=name)(dqk, dqk, pbig, pbig, pbig, w, b, dproj)


def _mm_fused(inputs, products, *, nt, m, n, tm, tn, outs, epilogue, name, nk=1, sub=None):
    dims = NT_DIMS if nt else (((1,), (0,)), ((), ()))
    nin = len(inputs)
    assert nk == 1 or (len(products) == 1 and sub is None)

    def body(*refs):
        in_refs, out_refs = refs[:nin], refs[nin:nin + len(outs)]
        i = pl.program_id(1)
        if sub is not None:
            lhs = {ia: in_refs[ia][...].astype(BF) for ia, _ in products}

            def dots(cols):
                return [lax.dot_general(lhs[ia], (in_refs[ib][cols, :] if nt else in_refs[ib][:, cols]).astype(BF),
                                        dims, preferred_element_type=F32) for ia, ib in products]

            slices = [slice(s, min(s + sub, tn)) for s in range(0, tn, sub)]
            prods = dots(slices[0])
            for idx, cols in enumerate(slices):
                nxt = dots(slices[idx + 1]) if idx + 1 < len(slices) else None
                epilogue(prods, in_refs, out_refs, i, cols)
                prods = nxt
            return
        prods = [lax.dot_general(in_refs[ia][...].astype(BF), in_refs[ib][...].astype(BF), dims, preferred_element_type=F32)
                 for ia, ib in products]
        if nk == 1:
            epilogue(prods, in_refs, out_refs, i, slice(None))
            return
        acc_ref = refs[-1]
        kk = pl.program_id(2)

        @pl.when(kk == 0)
        def _():
            acc_ref[...] = prods[0]

        @pl.when(jnp.logical_and(kk > 0, kk < nk - 1))
        def _():
            acc_ref[...] += prods[0]

        @pl.when(kk == nk - 1)
        def _():
            epilogue([acc_ref[...] + prods[0]], in_refs, out_refs, i, slice(None))

    return pl.pallas_call(
        body, grid=(n // tn, m // tm, nk), in_specs=[s for _, s in inputs], out_specs=[s for _, s in outs],
        out_shape=tuple(sh for sh, _ in outs), scratch_shapes=[pltpu.VMEM((tm, tn), F32)] if nk > 1 else [],
        compiler_params=_params(), name=name)(*[a for a, _ in inputs])


SUB_COLS = 256


def _cols_at(cols, offset):
    return slice(cols.start + offset, cols.stop + offset)


def _blk(rows, width, col=None, row=None):
    return pl.BlockSpec((rows, width), lambda j, i, kk: ((i if row is None else row(i)), (0 if col is None else col(j, kk))))


FF_TN = D_FF // 2


def _ffn_weight_rows(wg_t, wu_t):
    return jnp.concatenate([wg_t[0:FF_TN], wu_t[0:FF_TN], wg_t[FF_TN:], wu_t[FF_TN:]], axis=0)


def _ffn_in(hn, wgu_t, name):
    tp = hn.shape[0]
    tm = _mm_rows(tp)

    def epilogue(prods, in_refs, out_refs, i, cols):
        g, u = prods
        out_refs[0][:, cols] = g.astype(BF)
        out_refs[0][:, _cols_at(cols, FF_TN)] = u.astype(BF)
        out_refs[1][:, cols] = (g * _sigmoid(g) * u).astype(BF)

    wspec = lambda off: pl.BlockSpec((FF_TN, D), lambda j, i, kk: (2 * j + off, 0))
    return _mm_fused(
        [(hn, _blk(tm, D)), (wgu_t, wspec(0)), (wgu_t, wspec(1))], [(0, 1), (0, 2)], nt=True, m=tp, n=D_FF, tm=tm, tn=FF_TN,
        outs=[(jax.ShapeDtypeStruct((tp, 2 * D_FF), BF), _blk(tm, 2 * FF_TN, lambda j, kk: j)),
              (jax.ShapeDtypeStruct((tp, D_FF), BF), _blk(tm, FF_TN, lambda j, kk: j))],
        epilogue=epilogue, name=name, sub=SUB_COLS)


def _ffn_down_loss(ff, wdown, h1, target, gf, name):
    tp = ff.shape[0]

    def epilogue(prods, in_refs, out_refs, i, cols):
        live = (i > 0).astype(F32)
        g = in_refs[4][...]
        x = prods[0] + in_refs[2][...]
        r = lax.rsqrt(jnp.mean(x * x, axis=1, keepdims=True) + EPS)
        xh = x * r
        e = xh * g - in_refs[3][...]
        loss_part = 0.5 * live * jnp.sum(jnp.mean(e * e, axis=1, keepdims=True), axis=0, keepdims=True)
        dout = e * (live / D)
        dg_part = jnp.sum(dout * xh, axis=0, keepdims=True)
        dxh = dout * g
        out_refs[0][...] = r * (dxh - xh * jnp.mean(dxh * xh, axis=1, keepdims=True))

        @pl.when(i == 0)
        def _():
            out_refs[1][...] = loss_part
            out_refs[2][...] = dg_part

        @pl.when(i > 0)
        def _():
            out_refs[1][...] += loss_part
            out_refs[2][...] += dg_part

    const = lambda shape: pl.BlockSpec(shape, lambda j, i, kk: (0,) * len(shape))
    return _mm_fused(
        [(ff, _blk(TM, D_FF)), (wdown, const((D_FF, D))), (h1, _blk(TM, D)),
         (target, _blk(TM, D, row=lambda i: jnp.maximum(i - 1, 0))), (gf, const((1, D)))],
        [(0, 1)], nt=False, m=tp, n=D, tm=TM, tn=D,
        outs=[(jax.ShapeDtypeStruct((tp, D), F32), _blk(TM, D)), (jax.ShapeDtypeStruct((1, 1), F32), const((1, 1))),
              (jax.ShapeDtypeStruct((1, D), F32), const((1, D)))],
        epilogue=epilogue, name=name)


def _ffn_d_hidden(dh2, wdown, gu, name):
    tp = dh2.shape[0]

    def epilogue(prods, in_refs, out_refs, i, cols):
        d = prods[0]
        g = in_refs[2][:, cols].astype(F32)
        u = in_refs[2][:, _cols_at(cols, FF_TN)].astype(F32)
        sg = _sigmoid(g)
        out_refs[0][:, cols] = (d * u * sg * (1.0 + g * (1.0 - sg))).astype(BF)
        out_refs[0][:, _cols_at(cols, FF_TN)] = (d * g * sg).astype(BF)

    return _mm_fused(
        [(dh2, _blk(TM, D)), (wdown, pl.BlockSpec((FF_TN, D), lambda j, i, kk: (j, 0))), (gu, _blk(TM, 2 * FF_TN, lambda j, kk: j))],
        [(0, 1)], nt=True, m=tp, n=D_FF, tm=TM, tn=FF_TN,
        outs=[(jax.ShapeDtypeStruct((tp, 2 * D_FF), BF), _blk(TM, 2 * FF_TN, lambda j, kk: j))],
        epilogue=epilogue, name=name, sub=SUB_COLS)[0]


def _ffn_d_in(dgu, wgu_t, h1, rstd, g2, dh2, name):
    tp = dgu.shape[0]

    def epilogue(prods, in_refs, out_refs, i, cols):
        r = in_refs[3][...]
        xh = in_refs[2][...] * r
        dxn = prods[0]
        dxh = dxn * in_refs[4][...]
        out_refs[0][...] = r * (dxh - xh * jnp.mean(dxh * xh, axis=1, keepdims=True)) + in_refs[5][...]
        part = jnp.sum(dxn * xh, axis=0, keepdims=True)

        @pl.when(i == 0)
        def _():
            out_refs[1][...] = part

        @pl.when(i > 0)
        def _():
            out_refs[1][...] += part

    const = lambda shape: pl.BlockSpec(shape, lambda j, i, kk: (0,) * len(shape))
    return _mm_fused(
        [(dgu, _blk(TM, 2 * D_FF)), (wgu_t, const((2 * D_FF, D))),
         (h1, _blk(TM, D)), (rstd, _blk(TM, 1)), (g2, const((1, D))), (dh2, _blk(TM, D))],
        [(0, 1)], nt=False, m=tp, n=D, tm=TM, tn=D,
        outs=[(jax.ShapeDtypeStruct((tp, D), F32), _blk(TM, D)), (jax.ShapeDtypeStruct((1, D), F32), const((1, D)))],
        epilogue=epilogue, name=name)


def _branch_merge(y_m, y_g, wbm, wbg, pbig, name):
    tp = y_m.shape[0]

    def epilogue(prods, in_refs, out_refs, i, cols):
        pm, pg = prods[0].astype(BF), prods[1].astype(BF)
        out_refs[0][:, cols] = pm
        out_refs[1][:, cols] = pg
        out_refs[2][:, cols] = (_sigmoid(in_refs[4][:, cols].astype(F32)) * pm.astype(F32)
                                + _sigmoid(in_refs[5][:, cols].astype(F32)) * pg.astype(F32)).astype(BF)

    const = lambda shape: pl.BlockSpec(shape, lambda j, i, kk: (0,) * len(shape))
    shp = jax.ShapeDtypeStruct((tp, D), BF)
    return _mm_fused(
        [(y_m, _blk(TM, D)), (wbm, const((D, D))), (y_g, _blk(TM, D)), (wbg, const((D, D))),
         (pbig, _blk(TM, D, lambda j, kk: CB_GM)), (pbig, _blk(TM, D, lambda j, kk: CB_GG))],
        [(0, 1), (2, 3)], nt=False, m=tp, n=D, tm=TM, tn=D,
        outs=[(shp, _blk(TM, D)), (shp, _blk(TM, D)), (shp, _blk(TM, D))], epilogue=epilogue, name=name, sub=SUB_COLS)


def _merge_d(dh1, wout, pm, pg, pbig, name):
    tp = dh1.shape[0]

    def epilogue(prods, in_refs, out_refs, i, cols):
        d = prods[0]
        sm = _sigmoid(in_refs[4][:, cols].astype(F32))
        sg = _sigmoid(in_refs[5][:, cols].astype(F32))
        out_refs[0][:, cols] = (d * sm).astype(BF)
        out_refs[1][:, cols] = (d * sg).astype(BF)
        out_refs[2][:, cols] = (d * in_refs[2][:, cols].astype(F32) * sm * (1.0 - sm)).astype(BF)
        out_refs[2][:, _cols_at(cols, D)] = (d * in_refs[3][:, cols].astype(F32) * sg * (1.0 - sg)).astype(BF)

    const = lambda shape: pl.BlockSpec(shape, lambda j, i, kk: (0,) * len(shape))
    shp = jax.ShapeDtypeStruct((tp, D), BF)
    return _mm_fused(
        [(dh1, _blk(TM, D)), (wout, const((D, D))), (pm, _blk(TM, D)), (pg, _blk(TM, D)),
         (pbig, _blk(TM, D, lambda j, kk: CB_GM)), (pbig, _blk(TM, D, lambda j, kk: CB_GG))],
        [(0, 1)], nt=True, m=tp, n=D, tm=TM, tn=D,
        outs=[(shp, _blk(TM, D)), (shp, _blk(TM, D)),
              (jax.ShapeDtypeStruct((tp, N_ALL), BF), _blk(TM, 2 * D, lambda j, kk: CB_GM // 2))],
        epilogue=epilogue, name=name, sub=SUB_COLS)


def _out_proj_norm(merged, wout, h0, g2, name):
    tp = merged.shape[0]
    tm = _mm_rows(tp)

    def epilogue(prods, in_refs, out_refs, i, cols):
        x = prods[0] + in_refs[2][...]
        r = lax.rsqrt(jnp.mean(x * x, axis=1, keepdims=True) + EPS)
        out_refs[0][...] = x
        out_refs[1][...] = (x * r * in_refs[3][...]).astype(BF)
        out_refs[2][...] = r

    const = lambda shape: pl.BlockSpec(shape, lambda j, i, kk: (0,) * len(shape))
    return _mm_fused(
        [(merged, _blk(tm, D)), (wout, const((D, D))), (h0, _blk(tm, D)), (g2, const((1, D)))],
        [(0, 1)], nt=False, m=tp, n=D, tm=tm, tn=D,
        outs=[(jax.ShapeDtypeStruct((tp, D), F32), _blk(tm, D)), (jax.ShapeDtypeStruct((tp, D), BF), _blk(tm, D)),
              (jax.ShapeDtypeStruct((tp, 1), F32), _blk(tm, 1))],
        epilogue=epilogue, name=name)


def _adamw(w, g, m, v, name):
    rows, cols = w.shape
    by_cols = rows % 128 != 0 and cols % 128 == 0 and rows * cols > 128 * 1024
    tr = rows if (by_cols or rows % 128 != 0) else 128
    tc = 128 if by_cols else cols

    def body(w_ref, g_ref, m_ref, v_ref, d_ref, nm_ref, nv_ref):
        gv = g_ref[...]
        nm = ADAM_B1 * m_ref[...] + (1.0 - ADAM_B1) * gv
        nv = ADAM_B2 * v_ref[...] + (1.0 - ADAM_B2) * (gv * gv)
        m_hat = nm / (1.0 - ADAM_B1 ** ADAM_STEP)
        v_hat = nv / (1.0 - ADAM_B2 ** ADAM_STEP)
        d_ref[...] = -ADAM_LR * (m_hat / (jnp.sqrt(v_hat) + ADAM_EPS) + ADAM_WD * w_ref[...])
        nm_ref[...] = nm
        nv_ref[...] = nv

    spec = pl.BlockSpec((tr, tc), (lambda i: (0, i)) if by_cols else (lambda i: (i, 0)))
    shp = jax.ShapeDtypeStruct((rows, cols), F32)
    return pl.pallas_call(body, grid=(cols // tc if by_cols else rows // tr,), in_specs=[spec] * 4, out_specs=[spec] * 3,
                          out_shape=(shp,) * 3, compiler_params=_params(), name=name)(w, g, m, v)


def _place_small(dsmall, dproj, name):
    tp = dsmall.shape[0]

    def body(s_ref, _, o_ref):
        o_ref[...] = s_ref[...]

    return pl.pallas_call(
        body, grid=(tp // TM,), in_specs=[_row_spec(N_SMALL), ANY], out_specs=_row_spec(N_SMALL, N_BIG // N_SMALL),
        out_shape=jax.ShapeDtypeStruct(dproj.shape, dproj.dtype), input_output_aliases={1: 0},
        compiler_params=_params(), name=name)(dsmall, dproj)


def _row_tile(rows, cap=512):
    best = rows
    for cand in range(8, min(rows, cap) + 1, 8):
        if rows % cand == 0:
            best = cand
    return best


def _add2(a, b, out_dtype, name):
    rows, cols = a.shape
    tr = _row_tile(rows)

    def body(a_ref, b_ref, o_ref):
        o_ref[...] = (a_ref[...] + b_ref[...]).astype(o_ref.dtype)

    spec = pl.BlockSpec((tr, cols), lambda i: (i, 0))
    return pl.pallas_call(body, grid=(rows // tr,), in_specs=[spec] * 2, out_specs=spec,
                          out_shape=jax.ShapeDtypeStruct((rows, cols), out_dtype), compiler_params=_params(), name=name)(a, b)


def _add4(first, rest, name):
    rows, cols = first.shape
    tr = _row_tile(rows, 256)

    def body(f_ref, r_ref, o_ref):
        up = lambda v: v.astype(F32)
        o_ref[...] = ((up(f_ref[...]) + up(r_ref[0])) + up(r_ref[1])) + up(r_ref[2])

    return pl.pallas_call(body, grid=(rows // tr,),
                          in_specs=[pl.BlockSpec((tr, cols), lambda i: (i, 0)), pl.BlockSpec((3, tr, cols), lambda i: (0, i, 0))],
                          out_specs=pl.BlockSpec((tr, cols), lambda i: (i, 0)),
                          out_shape=jax.ShapeDtypeStruct((rows, cols), F32), compiler_params=_params(), name=name)(first, rest)


def _chunk_consts(length=CHUNK):
    r2 = lax.broadcasted_iota(jnp.int32, (length, length), 0)
    c2 = lax.broadcasted_iota(jnp.int32, (length, length), 1)
    tri = r2 >= c2
    return dict(tri=tri, tril_f=tri.astype(F32), triu_f=(r2 <= c2).astype(F32),
                lane=lax.broadcasted_iota(jnp.int32, (length, N_SMALL), 1),
                rowio=lax.broadcasted_iota(jnp.int32, (length, 1), 0),
                ones=jnp.ones((length, N_SMALL), F32))


def _valid_rows(block, c):
    row = block * TM + c * CHUNK + lax.broadcasted_iota(jnp.int32, (CHUNK, 1), 0)
    return row >= FIRST_VALID


def _col(x, lane, idx):
    return jnp.sum(jnp.where(lane == idx, x, 0.0), axis=1, keepdims=True)


def _last_row(x, rowio):
    return jnp.sum(jnp.where(rowio == rowio.shape[0] - 1, x, 0.0), axis=0, keepdims=True)


def _sum_all(x):
    return jnp.sum(jnp.sum(x, axis=1, keepdims=True), axis=0, keepdims=True)


def _headnorm_fwd(hm, gain, gate_act):
    rs = lax.rsqrt(jnp.mean(hm * hm, axis=1, keepdims=True) + EPS)
    return hm * rs * gain * gate_act


def _headnorm_bwd(dy, hm, gain, gate_act):
    rs = lax.rsqrt(jnp.mean(hm * hm, axis=1, keepdims=True) + EPS)
    xh = hm * rs
    dact = dy * xh * gain
    dgain = jnp.sum(dy * gate_act * xh, axis=0, keepdims=True)
    dxh = dy * gate_act * gain
    dhm = rs * (dxh - xh * jnp.mean(dxh * xh, axis=1, keepdims=True))
    return dhm, dact, dgain


def _mlstm_gates(sm, gbias, valid, k):
    pre = sm + gbias
    lf = jnp.where(valid, _logsig(pre), 0.0)
    b_all = _nn(k["tril_f"], lf, precision=HI)
    li_all = jnp.where(valid, pre, NEG)
    return pre, li_all, b_all


def _mlstm_open(h, qh, kh, c_st, li_all, b_all, k):
    lane = k["lane"]
    sel = jnp.where(lane == h, 1.0, 0.0) - jnp.where(lane == NH + h, 1.0, 0.0)
    x = jnp.where(lane < NH, li_all, jnp.where(lane < 2 * NH, b_all, 0.0))
    cb = c_st.astype(BF)
    return dict(ubc=_nt(sel, x, precision=HI), sim=_nt(qh, kh), cb=cb, cq=_nt(qh, cb))


def _mlstm_weights(h, f, qh, vh, li_all, b_all, n_row, m11, k):
    lane, tri, rowio = k["lane"], k["tri"], k["rowio"]
    b_col = _col(b_all, lane, NH + h)
    li_col = _col(li_all, lane, h)
    dmat = jnp.where(tri, b_col + f["ubc"], NEG)
    m_row = jnp.maximum(b_col + m11, jnp.max(dmat, axis=1, keepdims=True))
    e = jnp.exp(dmat - m_row)
    w_mat = e * f["sim"]
    a = jnp.exp(b_col + m11 - m_row)
    qf = qh.astype(F32)
    nq = jnp.sum(qf * n_row, axis=1, keepdims=True)
    g = _last_row(b_col, rowio)
    wlog = g - b_col + li_col
    m_new = jnp.maximum(g + m11, jnp.max(wlog, axis=0, keepdims=True))
    a_s = jnp.exp(g + m11 - m_new)
    w = jnp.exp(wlog - m_new)
    return dict(f, e=e, w_mat=w_mat, a=a, qf=qf, nq=nq, m_row=m_row, m_new=m_new, a_s=a_s, w=w,
                wv=_nn(w_mat.astype(BF), vh))


def _mlstm_out(f):
    num = f["a"] * f["cq"] + f["wv"]
    den = f["a"] * f["nq"] + jnp.sum(f["w_mat"], axis=1, keepdims=True)
    floor = jnp.exp(-f["m_row"])
    r = jnp.maximum(jnp.abs(den), floor)
    return dict(f, den=den, floor=floor, r=r, hm=num / r)


def _mlstm_fwd(qk, pbig, small, gbias, headg, name):
    tp = qk.shape[0]
    nb = tp // TM

    def body(qk_ref, v_ref, mo_ref, sm_ref, gb_ref, hg_ref, y_ref, cs_ref, ns_ref, c_scr, n_scr):
        blk = pl.program_id(0)

        @pl.when(blk == 0)
        def _():
            c_scr[...] = jnp.zeros_like(c_scr)
            n_scr[...] = jnp.zeros_like(n_scr)

        k = _chunk_consts()
        io8 = lax.broadcasted_iota(jnp.int32, (8, DQK), 0)

        def chunk(c, carry):
            r0 = pl.multiple_of(c * CHUNK, CHUNK)
            rows = pl.ds(r0, CHUNK)
            valid = _valid_rows(blk, c)
            _, li_all, b_all = _mlstm_gates(sm_ref[rows, :], gb_ref[...], valid, k)
            heads = range(NH)
            qs = [qk_ref[rows, h * DQK:(h + 1) * DQK] for h in heads]
            ks = [qk_ref[rows, NH * DQK + h * DQK:NH * DQK + (h + 1) * DQK] for h in heads]
            vs = [v_ref[rows, h * DV:(h + 1) * DV] for h in heads]
            cst = [c_scr[h] for h in heads]
            nrow = [n_scr[h, 0:1, :] for h in heads]
            m11 = [jnp.max(n_scr[h, 1:2, :], axis=1, keepdims=True) for h in heads]
            f = [_mlstm_open(h, qs[h], ks[h], cst[h], li_all, b_all, k) for h in heads]
            f = [_mlstm_weights(h, f[h], qs[h], vs[h], li_all, b_all, nrow[h], m11[h], k) for h in heads]
            wk = [f[h]["w"] * ks[h].astype(F32) for h in heads]
            kv = [_tn(vs[h], wk[h].astype(BF)) for h in heads]
            for h in heads:
                hm = _mlstm_out(f[h])["hm"]
                gate = _sigmoid(mo_ref[rows, h * DV:(h + 1) * DV].astype(F32))
                y_ref[rows, h * DV:(h + 1) * DV] = _headnorm_fwd(hm, hg_ref[:, h * DV:(h + 1) * DV], gate).astype(BF)
                cs_ref[c, h] = f[h]["cb"]
                ns_ref[c, h] = jnp.where(io8 == 0, nrow[h], jnp.where(io8 == 1, m11[h], 0.0))
                c_scr[h] = f[h]["a_s"] * cst[h] + kv[h]
                n_scr[h, 0:1, :] = f[h]["a_s"] * nrow[h] + jnp.sum(wk[h], axis=0, keepdims=True)
                n_scr[h, 1:2, :] = jnp.broadcast_to(f[h]["m_new"], (1, DQK))
            return carry

        lax.fori_loop(0, CPB, chunk, 0, unroll=2)

    return pl.pallas_call(
        body, grid=(nb,),
        in_specs=[_row_spec(D), _row_spec(D, CB_MV), _row_spec(D, CB_MO), _row_spec(N_SMALL), _full_spec((1, N_SMALL)), _full_spec((1, D))],
        out_specs=[_row_spec(D), pl.BlockSpec((CPB, NH, DV, DQK), lambda i: (i, 0, 0, 0)),
                   pl.BlockSpec((CPB, NH, 8, DQK), lambda i: (i, 0, 0, 0))],
        out_shape=(jax.ShapeDtypeStruct((tp, D), BF), jax.ShapeDtypeStruct((tp // CHUNK, NH, DV, DQK), BF),
                   jax.ShapeDtypeStruct((tp // CHUNK, NH, 8, DQK), F32)),
        scratch_shapes=[pltpu.VMEM((NH, DV, DQK), F32), pltpu.VMEM((NH, 8, DQK), F32)],
        compiler_params=_params(), name=name)(qk, pbig, pbig, small, gbias, headg)


def _mlstm_bwd(dy, qk, pbig, small, gbias, headg, cs, ns, dproj, name, ride=()):
    tp = qk.shape[0]
    nb = tp // TM
    nr = len(ride)

    def body(*refs):
        dy_ref, qk_ref, v_ref, mo_ref, sm_ref, gb_ref, hg_ref, cs_ref, ns_ref = refs[:9]
        ride_in = refs[10:10 + nr]
        dqk_ref, dproj_ref, dsm_ref, dgb_ref, dhg_ref = refs[10 + nr:15 + nr]
        ride_out = refs[15 + nr:15 + 2 * nr]
        dc_scr, dn_scr = refs[15 + 2 * nr:17 + 2 * nr]
        step = pl.program_id(0)
        blk = nb - 1 - step
        sent = _scatter_copies(ride_in, ride_out, *refs[17 + 2 * nr:]) if nr else []

        @pl.when(step == 0)
        def _():
            dc_scr[...] = jnp.zeros_like(dc_scr)
            dn_scr[...] = jnp.zeros_like(dn_scr)
            dgb_ref[...] = jnp.zeros_like(dgb_ref)
            dhg_ref[...] = jnp.zeros_like(dhg_ref)
            for cp in sent:
                cp.start()

        k = _chunk_consts()
        lane, rowio = k["lane"], k["rowio"]

        def chunk(cc, carry):
            c = CPB - 1 - cc
            r0 = pl.multiple_of(c * CHUNK, CHUNK)
            rows = pl.ds(r0, CHUNK)
            valid = _valid_rows(blk, c)
            pre, li_all, b_all = _mlstm_gates(sm_ref[rows, :], gb_ref[...], valid, k)
            dli_all = jnp.zeros((CHUNK, N_SMALL), F32)
            db_all = jnp.zeros((CHUNK, N_SMALL), F32)
            heads = range(NH)
            qs = [qk_ref[rows, h * DQK:(h + 1) * DQK] for h in heads]
            ks = [qk_ref[rows, NH * DQK + h * DQK:NH * DQK + (h + 1) * DQK] for h in heads]
            vs = [v_ref[rows, h * DV:(h + 1) * DV] for h in heads]
            cst = [cs_ref[c, h].astype(F32) for h in heads]
            nrow = [ns_ref[c, h, 0:1, :] for h in heads]
            m11 = [jnp.max(ns_ref[c, h, 1:2, :], axis=1, keepdims=True) for h in heads]
            f = [_mlstm_open(h, qs[h], ks[h], cst[h], li_all, b_all, k) for h in heads]
            f = [_mlstm_weights(h, f[h], qs[h], vs[h], li_all, b_all, nrow[h], m11[h], k) for h in heads]
            f = [_mlstm_out(f[h]) for h in heads]
            t = []
            for h in heads:
                gain = hg_ref[:, h * DV:(h + 1) * DV]
                gate = _sigmoid(mo_ref[rows, h * DV:(h + 1) * DV].astype(F32))
                dhm, dgate, dgain = _headnorm_bwd(dy_ref[rows, h * DV:(h + 1) * DV].astype(F32), f[h]["hm"], gain, gate)
                dproj_ref[rows, D + h * DV:D + (h + 1) * DV] = (dgate * gate * (1.0 - gate)).astype(BF)
                dhg_ref[:, h * DV:(h + 1) * DV] += dgain
                r, den = f[h]["r"], f[h]["den"]
                dnum = dhm / r
                dr = -jnp.sum(dhm * f[h]["hm"], axis=1, keepdims=True) / r
                dden = jnp.where(jnp.abs(den) > f[h]["floor"], dr * jnp.sign(den), 0.0)
                dnb = dnum.astype(BF)
                dc_new = dc_scr[h]
                dcb = dc_new.astype(BF)
                t.append(dict(dnum=dnum, dden=dden, dnb=dnb, dc_new=dc_new, dn_new=dn_scr[h],
                              dwm=_nt(dnb, vs[h]), vdc=_nn(vs[h], dcb), kdc=_nt(ks[h], dcb)))
            for h in heads:
                dw_mat = t[h]["dwm"] + t[h]["dden"]
                dsim = (f[h]["e"] * dw_mat).astype(BF)
                gm = f[h]["w_mat"] * dw_mat
                t[h].update(gm=gm, dv0=_tn(f[h]["w_mat"].astype(BF), t[h]["dnb"]), dq0=_nn(dsim, ks[h]),
                            dq1=_nn(t[h]["dnb"], f[h]["cb"]), dk0=_tn(dsim, qs[h]),
                            dcq=_tn((f[h]["a"] * t[h]["dnum"]).astype(BF), qs[h]), cs2=_tn(gm, k["ones"], precision=HI))
            for h in heads:
                a, w, a_s = f[h]["a"], f[h]["w"], f[h]["a_s"]
                dnum, dden, dc_new, dn_new, vdc, gm = (t[h][n] for n in ("dnum", "dden", "dc_new", "dn_new", "vdc", "gm"))
                kf = ks[h].astype(F32)
                dproj_ref[rows, h * DV:(h + 1) * DV] = (t[h]["dv0"] + w * t[h]["kdc"]).astype(BF)
                adden = a * dden
                dqk_ref[rows, h * DQK:(h + 1) * DQK] = t[h]["dq0"] + a * t[h]["dq1"] + adden * nrow[h]
                dqk_ref[rows, NH * DQK + h * DQK:NH * DQK + (h + 1) * DQK] = t[h]["dk0"] + w * vdc + w * dn_new
                da = jnp.sum(dnum * f[h]["cq"], axis=1, keepdims=True) + dden * f[h]["nq"]
                dw = jnp.sum(vdc * kf, axis=1, keepdims=True) + jnp.sum(kf * dn_new, axis=1, keepdims=True)
                da_s = _sum_all(dc_new * cst[h]) + jnp.sum(dn_new * nrow[h], axis=1, keepdims=True)
                wdw = w * dw
                rs = jnp.sum(gm, axis=1, keepdims=True)
                cs_col = _col(t[h]["cs2"], lane, 0)
                dg = a_s * da_s + jnp.sum(wdw, axis=0, keepdims=True)
                db = a * da + rs - cs_col - wdw + jnp.where(rowio == CHUNK - 1, dg, 0.0)
                dli_all = dli_all + jnp.where(lane == h, cs_col + wdw, 0.0)
                db_all = db_all + jnp.where(lane == NH + h, db, 0.0)
                dc_scr[h] = a_s * dc_new + t[h]["dcq"]
                dn_scr[h] = a_s * dn_new + jnp.sum(adden * f[h]["qf"], axis=0, keepdims=True)
            dlf_all = _nn(k["triu_f"], db_all, precision=HI)
            dsm = jnp.where(valid, dli_all + dlf_all * _sigmoid(-pre), 0.0)
            dsm = jnp.where(lane < 2 * NH, dsm, 0.0)
            dsm_ref[rows, :] = dsm
            dgb_ref[0:1, :] += jnp.sum(dsm, axis=0, keepdims=True)
            return carry

        lax.fori_loop(0, CPB, chunk, 0, unroll=2)

        if nr:
            @pl.when(step == nb - 1)
            def _():
                for cp in sent:
                    cp.wait_recv()
                for cp in sent:
                    cp.wait_send()

    rev = lambda col: (lambda i: (nb - 1 - i, col))
    rspec = lambda width, col=0: pl.BlockSpec((TM, width), rev(col))
    ride_shapes, ride_sems = _scatter_shapes(ride) if nr else ((), [])
    outs = pl.pallas_call(
        body, grid=(nb,),
        in_specs=[rspec(D), rspec(D), rspec(D, CB_MV), rspec(D, CB_MO), rspec(N_SMALL), _full_spec((1, N_SMALL)), _full_spec((1, D)),
                  pl.BlockSpec((CPB, NH, DV, DQK), lambda i: (nb - 1 - i, 0, 0, 0)),
                  pl.BlockSpec((CPB, NH, 8, DQK), lambda i: (nb - 1 - i, 0, 0, 0)), ANY] + [ANY] * nr,
        out_specs=[rspec(D), rspec(2 * D, CB_MV // 2), rspec(N_SMALL), _full_spec((8, N_SMALL)), _full_spec((1, D))] + [ANY] * nr,
        out_shape=(jax.ShapeDtypeStruct((tp, D), F32), jax.ShapeDtypeStruct(dproj.shape, BF),
                   jax.ShapeDtypeStruct((tp, N_SMALL), F32), jax.ShapeDtypeStruct((8, N_SMALL), F32),
                   jax.ShapeDtypeStruct((1, D), F32)) + tuple(ride_shapes),
        scratch_shapes=[pltpu.VMEM((NH, DV, DQK), F32), pltpu.VMEM((NH, 1, DQK), F32)] + ride_sems,
        input_output_aliases={9: 1}, compiler_params=_params(), name=name)(dy, qk, pbig, pbig, small, gbias, headg, cs, ns, dproj, *ride)
    return tuple(outs[:5]) + (list(outs[5:]),)


def _gla_loga(sm_ref, a2_ref, a2b_ref, blk):
    za = _nn(sm_ref[...].astype(BF), a2_ref[...]) + a2b_ref[...]
    row = blk * TM + lax.broadcasted_iota(jnp.int32, (TM, 1), 0)
    return za, jnp.where(row >= FIRST_VALID, _logsig(za) / G_TAU, 0.0)


def _gla_head(h, q_ref, k_ref, rows, bc, btot, k):
    sl = slice(h * DQK, (h + 1) * DQK)
    bch = bc[:, sl]
    bth = btot[:, sl]
    gq = q_ref[rows, h * DQK:(h + 1) * DQK].astype(F32)
    gk = k_ref[rows, NH * DQK + h * DQK:NH * DQK + (h + 1) * DQK].astype(F32)
    e_pos = jnp.exp(bch) * (DQK ** -0.5)
    e_neg = jnp.exp(-bch)
    e_end = jnp.exp(bth - bch)
    qd = gq * e_pos
    ki = gk * e_neg
    ke = gk * e_end
    att = jnp.where(k["tri"], _nt(qd.astype(BF), ki.astype(BF)), 0.0)
    return dict(e_pos=e_pos, e_neg=e_neg, e_end=e_end, qd=qd, ki=ki, ke=ke, att=att, decay=jnp.exp(bth))


def _gla_fwd(pbig, small, a2p, a2b, headg, name):
    tp = pbig.shape[0]
    nb = tp // TM

    def body(qk_ref, v_ref, gr_ref, sm_ref, a2_ref, a2b_ref, hg_ref, y_ref, ss_ref, s_scr, lg_scr):
        blk = pl.program_id(0)

        @pl.when(blk == 0)
        def _():
            s_scr[...] = jnp.zeros_like(s_scr)

        k = _chunk_consts(G_CHUNK)
        _, loga = _gla_loga(sm_ref, a2_ref, a2b_ref, blk)
        lg_scr[...] = loga

        def chunk(c, carry):
            r0 = pl.multiple_of(c * G_CHUNK, G_CHUNK)
            rows = pl.ds(r0, G_CHUNK)
            bc = _nn(k["tril_f"], lg_scr[rows, :], precision=HI)
            btot = _last_row(bc, k["rowio"])
            heads = range(NH)
            f = [_gla_head(h, qk_ref, qk_ref, rows, bc, btot, k) for h in heads]
            vs = [v_ref[rows, h * DV:(h + 1) * DV] for h in heads]
            sst = [s_scr[h] for h in heads]
            sbs = [s.astype(BF) for s in sst]
            inter = [_nt(f[h]["qd"].astype(BF), sbs[h]) for h in heads]
            intra = [_nn(f[h]["att"].astype(BF), vs[h]) for h in heads]
            kv = [_tn(vs[h], f[h]["ke"].astype(BF)) for h in heads]
            for h in heads:
                gr = gr_ref[rows, h * DV:(h + 1) * DV].astype(F32)
                y_ref[rows, h * DV:(h + 1) * DV] = _headnorm_fwd(intra[h] + inter[h], hg_ref[:, h * DV:(h + 1) * DV],
                                                                   gr * _sigmoid(gr)).astype(BF)
                ss_ref[c, h] = sbs[h]
                s_scr[h] = sst[h] * f[h]["decay"] + kv[h]
            return carry

        lax.fori_loop(0, G_CPB, chunk, 0, unroll=2)

    return pl.pallas_call(
        body, grid=(nb,),
        in_specs=[_row_spec(D, CB_GQK), _row_spec(D, CB_GV), _row_spec(D, CB_GR), _row_spec(N_SMALL),
                  _full_spec((N_SMALL, NH * DQK)), _full_spec((1, NH * DQK)), _full_spec((1, D))],
        out_specs=[_row_spec(D), pl.BlockSpec((G_CPB, NH, DV, DQK), lambda i: (i, 0, 0, 0))],
        out_shape=(jax.ShapeDtypeStruct((tp, D), BF), jax.ShapeDtypeStruct((tp // G_CHUNK, NH, DV, DQK), BF)),
        scratch_shapes=[pltpu.VMEM((NH, DV, DQK), F32), pltpu.VMEM((TM, NH * DQK), F32)],
        compiler_params=_params(), name=name)(pbig, pbig, pbig, small, a2p, a2b, headg)


def _gla_bwd(dy, pbig, small, a2p, a2b, headg, ss, dsm_m, dproj, name):
    tp = pbig.shape[0]
    nb = tp // TM
    nqk = NH * DQK

    def body(dy_ref, qk_ref, v_ref, gr_ref, sm_ref, a2_ref, a2b_ref, hg_ref, ss_ref, dsmm_ref, _,
             dproj_ref, dsm_ref, da2_ref, da2b_ref, dhg_ref, ds_scr, lg_scr, dza_scr):
        step = pl.program_id(0)
        blk = nb - 1 - step

        @pl.when(step == 0)
        def _():
            ds_scr[...] = jnp.zeros_like(ds_scr)
            da2_ref[...] = jnp.zeros_like(da2_ref)
            da2b_ref[...] = jnp.zeros_like(da2b_ref)
            dhg_ref[...] = jnp.zeros_like(dhg_ref)

        k = _chunk_consts(G_CHUNK)
        rowio = k["rowio"]
        za, loga = _gla_loga(sm_ref, a2_ref, a2b_ref, blk)
        lg_scr[...] = loga

        def chunk(cc, carry):
            c = G_CPB - 1 - cc
            r0 = pl.multiple_of(c * G_CHUNK, G_CHUNK)
            rows = pl.ds(r0, G_CHUNK)
            bc = _nn(k["tril_f"], lg_scr[rows, :], precision=HI)
            btot = _last_row(bc, rowio)
            heads = range(NH)
            f = [_gla_head(h, qk_ref, qk_ref, rows, bc, btot, k) for h in heads]
            vs = [v_ref[rows, h * DV:(h + 1) * DV] for h in heads]
            sbs = [ss_ref[c, h] for h in heads]
            qdb = [f[h]["qd"].astype(BF) for h in heads]
            attb = [f[h]["att"].astype(BF) for h in heads]
            inter = [_nt(qdb[h], sbs[h]) for h in heads]
            intra = [_nn(attb[h], vs[h]) for h in heads]
            dsn = [ds_scr[h] for h in heads]
            dsb = [d.astype(BF) for d in dsn]
            dke = [_nn(vs[h], dsb[h]) for h in heads]
            dv1 = [_nt(f[h]["ke"].astype(BF), dsb[h]) for h in heads]
            t = []
            for h in heads:
                gr = gr_ref[rows, h * DV:(h + 1) * DV].astype(F32)
                sg = _sigmoid(gr)
                gain = hg_ref[:, h * DV:(h + 1) * DV]
                do, dact, dgain = _headnorm_bwd(dy_ref[rows, h * DV:(h + 1) * DV].astype(F32), intra[h] + inter[h], gain, gr * sg)
                dproj_ref[rows, 2 * D + h * DV:2 * D + (h + 1) * DV] = (dact * sg * (1.0 + gr * (1.0 - sg))).astype(BF)
                dhg_ref[:, h * DV:(h + 1) * DV] += dgain
                dob = do.astype(BF)
                t.append(dict(dob=dob, datt=_nt(dob, vs[h]), dv0=_tn(attb[h], dob), dq1=_nn(dob, sbs[h]), dsq=_tn(dob, qdb[h])))
            for h in heads:
                datt = jnp.where(k["tri"], t[h]["datt"], 0.0).astype(BF)
                t[h].update(dq0=_nn(datt, f[h]["ki"].astype(BF)), dki=_tn(datt, qdb[h]))
            dbc_parts = []
            for h in heads:
                dqd = t[h]["dq0"] + t[h]["dq1"]
                dki = t[h]["dki"]
                dproj_ref[rows, D + h * DV:D + (h + 1) * DV] = (t[h]["dv0"] + dv1[h]).astype(BF)
                dproj_ref[rows, h * DQK:(h + 1) * DQK] = (dqd * f[h]["e_pos"]).astype(BF)
                dproj_ref[rows, nqk + h * DQK:nqk + (h + 1) * DQK] = (dki * f[h]["e_neg"] + dke[h] * f[h]["e_end"]).astype(BF)
                dke_ke = dke[h] * f[h]["ke"]
                dbtot = (jnp.sum(dke_ke, axis=0, keepdims=True)
                         + jnp.sum(dsn[h] * sbs[h].astype(F32), axis=0, keepdims=True) * f[h]["decay"])
                dbc_parts.append(dqd * f[h]["qd"] - dki * f[h]["ki"] - dke_ke + jnp.where(rowio == G_CHUNK - 1, dbtot, 0.0))
                ds_scr[h] = dsn[h] * f[h]["decay"] + t[h]["dsq"]
            dbc = jnp.concatenate(dbc_parts, axis=1)
            dza_scr[rows, :] = _nn(k["triu_f"], dbc, precision=HI)
            return carry

        lax.fori_loop(0, G_CPB, chunk, 0, unroll=2)
        row = blk * TM + lax.broadcasted_iota(jnp.int32, (TM, 1), 0)
        dza = jnp.where(row >= FIRST_VALID, dza_scr[...] * (_sigmoid(-za) / G_TAU), 0.0)
        dzb = dza.astype(BF)
        dsm_ref[...] = (_nt(dzb, a2_ref[...]) + dsmm_ref[...]).astype(BF)
        da2_ref[...] += _tn(sm_ref[...].astype(BF), dzb)
        da2b_ref[...] += jnp.sum(dza, axis=0, keepdims=True)

    rspec = lambda width, col=0: pl.BlockSpec((TM, width), lambda i: (nb - 1 - i, col))
    return pl.pallas_call(
        body, grid=(nb,),
        in_specs=[rspec(D), rspec(D, CB_GQK), rspec(D, CB_GV), rspec(D, CB_GR), rspec(N_SMALL),
                  _full_spec((N_SMALL, nqk)), _full_spec((1, nqk)), _full_spec((1, D)),
                  pl.BlockSpec((G_CPB, NH, DV, DQK), lambda i: (nb - 1 - i, 0, 0, 0)), rspec(N_SMALL), ANY],
        out_specs=[rspec(3 * D, 0), rspec(N_SMALL), _full_spec((N_SMALL, nqk)), _full_spec((1, nqk)), _full_spec((1, D))],
        out_shape=(jax.ShapeDtypeStruct(dproj.shape, BF),
                   jax.ShapeDtypeStruct((tp, N_SMALL), BF), jax.ShapeDtypeStruct((N_SMALL, nqk), F32),
                   jax.ShapeDtypeStruct((1, nqk), F32), jax.ShapeDtypeStruct((1, D), F32)),
        scratch_shapes=[pltpu.VMEM((NH, DV, DQK), F32), pltpu.VMEM((TM, nqk), F32), pltpu.VMEM((TM, nqk), F32)],
        input_output_aliases={10: 0}, compiler_params=_params(), name=name)(dy, pbig, pbig, pbig, small, a2p, a2b, headg, ss, dsm_m, dproj)


PIECE_BYTES = 1 << 20
MAX_PIECES = 32


def _place():
    return lax.axis_index("x"), lax.axis_index("y"), lax.axis_index("c")


def _piece_rows(rows, row_bytes, align):
    want = min(MAX_PIECES, max(1, -(-rows * row_bytes // PIECE_BYTES)))
    best = rows
    for k in range(1, want + 1):
        if rows % k == 0 and (rows // k) % align == 0:
            best = rows // k
    return best


def _remote(src, dst, send_sems, recv_sems, k, to):
    return pltpu.make_async_remote_copy(src_ref=src, dst_ref=dst, send_sem=send_sems.at[k], recv_sem=recv_sems.at[k],
                                        device_id=to, device_id_type=MESH)


def _all_gather_chips(p, name):
    rd = _gather_rider(p)

    def body(*refs):
        start, middle, finish = rd["make"](refs[:1], refs[1:2], refs[2:])
        start()
        middle()
        finish()

    return pl.pallas_call(body, in_specs=[ANY], out_specs=[ANY], out_shape=rd["out_shapes"], scratch_shapes=rd["sems"],
                          name=name)(p)[0]


def _gather_rider(p):
    r, n = p.shape
    rh = r // 2
    align = 32 // p.dtype.itemsize
    assert r % (2 * align) == 0
    cr = _piece_rows(rh, n * p.dtype.itemsize, align)

    def make(in_refs, out_refs, sem_refs):
        p_ref, o_ref = in_refs[0], out_refs[0]
        send_sems, recv_sems = sem_refs
        x, y, c = _place()
        chips = [(1 - x, y), (x, 1 - y), (1 - x, 1 - y)]
        sib = (x, y, 1 - c)

        def half(hc, piece=None):
            if piece is None:
                return pl.ds(pl.multiple_of(hc * rh, align), rh)
            return pl.ds(pl.multiple_of(hc * rh + piece * cr, align), cr)

        first = [_remote(p_ref.at[half(c)], o_ref.at[j, half(c)], send_sems, recv_sems, j, (*chip, c))
                 for j, chip in enumerate(chips)]
        passed = [[_remote(o_ref.at[j, half(c, i)], o_ref.at[j, half(c, i)], send_sems, recv_sems, 3 + j, sib)
                   for i in range(rh // cr)] for j in range(3)]
        blocks = [_remote(o_ref.at[j, half(c)], o_ref.at[j, half(1 - c)], send_sems, recv_sems, 3 + j, sib) for j in range(3)]

        def start():
            for cp in first:
                cp.start()

        def middle():
            for j, cp in enumerate(first):
                cp.wait_recv()
                for piece in passed[j]:
                    piece.start()

        def finish():
            for block in blocks:
                block.wait_send()
                block.wait_recv()
            for cp in first:
                cp.wait_send()

        return start, middle, finish

    return dict(inputs=[p], out_shapes=(jax.ShapeDtypeStruct((3, r, n), p.dtype),),
                sems=[pltpu.SemaphoreType.DMA((6,)), pltpu.SemaphoreType.DMA((6,))], make=make)


def _scatter_rider(items):
    out_shapes, sems = _scatter_shapes(items)

    def make(in_refs, out_refs, sem_refs):
        sent = _scatter_copies(in_refs, out_refs, *sem_refs)

        def start():
            for cp in sent:
                cp.start()

        def finish():
            for cp in sent:
                cp.wait_recv()
            for cp in sent:
                cp.wait_send()

        return start, (lambda: None), finish

    return dict(inputs=list(items), out_shapes=out_shapes, sems=sems, make=make)


def _by_chip(mine, others):
    me = 2 * lax.axis_index("x") + lax.axis_index("y")
    by_mask = jnp.stack([mine, others[1], others[0], others[2]])
    return [lax.dynamic_index_in_dim(by_mask, q ^ me, 0, keepdims=False) for q in range(4)]


def _swap_halves(items, name):
    k = len(items)

    def body(*refs):
        a_refs, got_refs = refs[:k], refs[k:2 * k]
        send_sems, recv_sems = refs[2 * k:]
        x, y, c = _place()
        sib = (x, y, 1 - c)
        for i, a in enumerate(items):
            _, r, n = a.shape
            rh = r // 2
            cr = _piece_rows(rh, n * a.dtype.itemsize, 8)
            for q in range(4):
                for t in range(rh // cr):
                    other = pl.ds(pl.multiple_of((1 - c) * rh + t * cr, 8), cr)
                    _remote(a_refs[i].at[q, other], got_refs[i].at[q, pl.ds(t * cr, cr)], send_sems, recv_sems, i, sib).start()
        for i, a in enumerate(items):
            block = _remote(a_refs[i].at[:, pl.ds(0, a.shape[1] // 2)], got_refs[i], send_sems, recv_sems, i, sib)
            block.wait_send()
            block.wait_recv()

    return pl.pallas_call(
        body, in_specs=[ANY] * k, out_specs=[ANY] * k,
        out_shape=tuple(jax.ShapeDtypeStruct((4, a.shape[1] // 2, a.shape[2]), a.dtype) for a in items),
        scratch_shapes=[pltpu.SemaphoreType.DMA((k,)), pltpu.SemaphoreType.DMA((k,))], name=name)(*items)


def _scatter_copies(s_refs, o_refs, send_sems, recv_sems):
    x, y, c = _place()
    chips = [(1 - x, y), (x, 1 - y), (1 - x, 1 - y)]
    copies = []
    for i in range(len(s_refs)):
        for j, (cx, cy) in enumerate(chips):
            for other in (0, 1):
                copies.append(_remote(s_refs[i].at[2 * cx + cy], o_refs[i].at[j, c], send_sems, recv_sems,
                                      COPIES_PER_ITEM * i + 2 * j + other, (cx, cy, (1 - c) if other else c)))
        copies.append(_remote(s_refs[i].at[2 * x + y], o_refs[i].at[3, c], send_sems, recv_sems,
                              COPIES_PER_ITEM * i + 6, (x, y, 1 - c)))
    return copies


COPIES_PER_ITEM = 7


def _scatter_shapes(items):
    k = len(items)
    return (tuple(jax.ShapeDtypeStruct((4, 2) + s.shape[1:], s.dtype) for s in items),
            [pltpu.SemaphoreType.DMA((COPIES_PER_ITEM * k,)), pltpu.SemaphoreType.DMA((COPIES_PER_ITEM * k,))])


def _scatter_chips(items, name):
    k = len(items)

    def body(*refs):
        sent = _scatter_copies(refs[:k], refs[k:2 * k], *refs[2 * k:])
        for cp in sent:
            cp.start()
        for cp in sent:
            cp.wait_recv()
        for cp in sent:
            cp.wait_send()

    out_shape, scratch = _scatter_shapes(items)
    return pl.pallas_call(body, in_specs=[ANY] * k, out_specs=[ANY] * k, out_shape=out_shape, scratch_shapes=scratch,
                          name=name)(*items)


SMALL_ROWS = 16
SMALL_GRAD_ROWS = 48
SMALL_SHARD_SHAPES = [(N_META, 256), (4, 256), (G_RANK, 128), (NH, 64), (NH, 64)]
REPL_SHAPES = [(1, D), (1, D), (1, 2, NH), (1, NH * DQK), (1, D), (D,)]
W_IN_SHARD = 2054
W_IN_BLOCK = 2080


def _pack_small(parts, rows=SMALL_ROWS):
    flat = jnp.concatenate([p.reshape(-1) for p in parts])
    return jnp.pad(flat, (0, rows * D - flat.shape[0])).reshape(rows, D)


def _unpack_small(block, shapes):
    flat, out, off = block.reshape(-1), [], 0
    for shp in shapes:
        n = 1
        for s in shp:
            n *= s
        out.append(flat[off:off + n].reshape(shp))
        off += n
    return out


def _proj_rows_from_w_in(w_in_t):
    w_big = jnp.concatenate([w_in_t[3080:5128], w_in_t[5144:6168], w_in_t[0:1024], w_in_t[6168:8216],
                             w_in_t[1024:2048], w_in_t[2056:3080]], axis=0)
    w_small = jnp.concatenate([w_in_t[2048:2056], w_in_t[5128:5144], jnp.zeros((N_SMALL - 24, D), w_in_t.dtype)], axis=0)
    return w_big, w_small


def _w_in_from_proj_rows(d_wall_t):
    big, small = d_wall_t[0:N_BIG], d_wall_t[N_BIG:N_ALL]
    return jnp.concatenate([big[3072:4096], big[6144:7168], small[0:8], big[7168:8192], big[0:2048],
                            small[8:24], big[2048:3072], big[4096:6144]], axis=0)


def kernel(x, meta_tokens, norm1_g, w_in, conv_w, conv_b, m_gate_b, g_a2, g_a2_b, m_head_g, g_head_g, w_branch_m, w_branch_g, w_out, norm2_g, w_ff_gate, w_ff_up, w_ff_down, final_g, loss_target, m_meta_tokens, m_norm1_g, m_w_in, m_conv_w, m_conv_b, m_m_gate_b, m_g_a2, m_g_a2_b, m_m_head_g, m_g_head_g, m_w_branch_m, m_w_branch_g, m_w_out, m_norm2_g, m_w_ff_gate, m_w_ff_up, m_w_ff_down, m_final_g, v_meta_tokens, v_norm1_g, v_w_in, v_conv_w, v_conv_b, v_m_gate_b, v_g_a2, v_g_a2_b, v_m_head_g, v_g_head_g, v_w_branch_m, v_w_branch_g, v_w_out, v_norm2_g, v_w_ff_gate, v_w_ff_up, v_w_ff_down, v_final_g):
    w = _gather_weights(w_in, w_branch_m, w_branch_g, w_out, w_ff_gate, w_ff_up, w_ff_down, meta_tokens, conv_w, g_a2, m_head_g, g_head_g)
    loss_local, dx, grads = _local_step(x[0], loss_target[0], w, norm1_g, conv_b, m_gate_b, g_a2_b, norm2_g, final_g, _Reducer())

    weights = [w_in, w_branch_m, w_branch_g, w_out, w_ff_gate, w_ff_up, w_ff_down, meta_tokens, conv_w, g_a2, m_head_g, g_head_g,
               norm1_g, conv_b, m_gate_b, g_a2_b, norm2_g, final_g]
    moms = [m_w_in, m_w_branch_m, m_w_branch_g, m_w_out, m_w_ff_gate, m_w_ff_up, m_w_ff_down, m_meta_tokens, m_conv_w, m_g_a2,
            m_m_head_g, m_g_head_g, m_norm1_g, m_conv_b, m_m_gate_b, m_g_a2_b, m_norm2_g, m_final_g]
    vels = [v_w_in, v_w_branch_m, v_w_branch_g, v_w_out, v_w_ff_gate, v_w_ff_up, v_w_ff_down, v_meta_tokens, v_conv_w, v_g_a2,
            v_m_head_g, v_g_head_g, v_norm1_g, v_conv_b, v_m_gate_b, v_g_a2_b, v_norm2_g, v_final_g]
    res = {}
    for nm, wt, g, m, v in zip(PACK_ORDER, weights, grads, moms, vels):
        if nm in TRANSPOSED_GRADS:
            to2d = lambda a: jnp.swapaxes(a, -1, -2).reshape(a.shape[-1], a.shape[-2])
            back = lambda a: jnp.swapaxes(a, 0, 1).reshape(wt.shape)
        else:
            to2d = lambda a: a.reshape(wt.size // wt.shape[-1], wt.shape[-1])
            back = lambda a: a.reshape(wt.shape)
        d, nm_, nv_ = _adamw(to2d(wt), g, to2d(m), to2d(v), "adamw_" + nm)
        res[nm] = (back(g), back(d), back(nm_), back(nv_))

    order = ["meta_tokens", "norm1_g", "w_in", "conv_w", "conv_b", "m_gate_b", "g_a2", "g_a2_b", "m_head_g", "g_head_g",
             "w_branch_m", "w_branch_g", "w_out", "norm2_g", "w_ff_gate", "w_ff_up", "w_ff_down", "final_g"]
    loss = lax.psum(loss_local[0, 0], ("x", "y", "c"))
    grad_x = dx.reshape(x.shape)
    return (loss, grad_x, *[res[n][0] for n in order], *[res[n][1] for n in order],
            *[res[n][2] for n in order], *[res[n][3] for n in order])


TRANSPOSED_GRADS = ("w_in", "w_ff_gate", "w_ff_up")
PACK_ORDER = ["w_in", "w_branch_m", "w_branch_g", "w_out", "w_ff_gate", "w_ff_up", "w_ff_down", "meta_tokens", "conv_w", "g_a2",
              "m_head_g", "g_head_g", "norm1_g", "conv_b", "m_gate_b", "g_a2_b", "norm2_g", "final_g"]


def _gather_weights(w_in, w_branch_m, w_branch_g, w_out, w_ff_gate, w_ff_up, w_ff_down, meta_tokens, conv_w, g_a2, m_head_g, g_head_g):
    bf = lambda a: a.astype(BF)
    rows_local = jnp.concatenate([bf(w_branch_m[0]), bf(w_branch_g[0]), bf(w_out[0]), bf(w_ff_down[0]),
                                  bf(w_ff_gate[0].T), bf(w_ff_up[0].T)], axis=0)
    win_local = jnp.pad(bf(w_in[0].T), ((0, W_IN_BLOCK - W_IN_SHARD), (0, 0)))
    small_local = _pack_small([meta_tokens, conv_w[0], g_a2[0], m_head_g[0], g_head_g[0]])
    small_all = _by_chip(small_local, _all_gather_chips(small_local, "gather_small"))
    small_sh = [_unpack_small(small_all[q], SMALL_SHARD_SHAPES) for q in range(4)]
    cat = lambda i: jnp.concatenate([s[i] for s in small_sh], axis=-1)
    return dict(win_local=win_local, rows_local=rows_local, meta=cat(0), convw=cat(1), ga2=cat(2),
                mhg=cat(3).reshape(1, D), ghg=cat(4).reshape(1, D))


def _row_weights(rows_local, gathered):
    rows_all = jnp.stack(_by_chip(rows_local, gathered))
    cut = lambda lo, hi: rows_all[:, lo:hi].reshape(4 * (hi - lo), D)
    return cut(0, 256), cut(256, 512), cut(512, 768), cut(768, 1472), _ffn_weight_rows(cut(1472, 2176), cut(2176, 2880))


def _local_step(x0, target, w, norm1_g, conv_b, m_gate_b, g_a2_b, norm2_g, final_g, reducer):
    meta_f, convw_f, ga2_f, mhg_f, ghg_f = w["meta"], w["convw"], w["ga2"], w["mhg"], w["ghg"]
    gbias =jnp.concatenate([m_gate_b.reshape(1, 2 * NH), jnp.zeros((1, N_SMALL - 2 * NH), F32)], axis=1)
    a2p = jnp.concatenate([jnp.zeros((8, NH * DQK), F32), ga2_f, jnp.zeros((N_SMALL - 24, NH * DQK), F32)], axis=0).astype(BF)
    convb = conv_b.reshape(1, D)
    g1 = norm1_g.reshape(1, D)
    g2 = norm2_g.reshape(1, D)
    gf = final_g.reshape(1, D)
    first = jnp.concatenate([jnp.zeros((FIRST_VALID, D), F32), meta_f], axis=0)

    h0, xn1, rstd1, win_gathered = _embed_norm(x0, first, g1, _gather_rider(w["win_local"]), "rms1")
    win_all = _by_chip(w["win_local"], win_gathered[0])
    w_in_f = jnp.concatenate([win_all[q][0:W_IN_SHARD] for q in range(4)], axis=0)
    w_big, w_small = _proj_rows_from_w_in(w_in_f)
    w_all = jnp.concatenate([w_big, w_small], axis=0)
    pbig, rows_gathered = _mm(xn1, w_big, nt=True, out_dtype=BF, tn=2 * D, name="proj_big", rider=_gather_rider(w["rows_local"]))
    wbm, wbg, wout, wdown, wgu_t = _row_weights(w["rows_local"], rows_gathered[0])
    small = _mm(xn1, w_small, nt=True, out_dtype=F32, tn=N_SMALL, name="proj_small")
    qk = _conv_fwd(pbig, convw_f, convb, "conv_fwd")
    y_m, m_cs, m_ns = _mlstm_fwd(qk, pbig, small, gbias, mhg_f, "mlstm_fwd")
    y_g, g_ss = _gla_fwd(pbig, small, a2p, g_a2_b, ghg_f, "gla_fwd")
    p_m, p_g, merged = _branch_merge(y_m, y_g, wbm, wbg, pbig, "branch_merge")
    h1, hn, rstd2 = _out_proj_norm(merged, wout, h0, g2, "out_proj")
    gu, ff = _ffn_in(hn, wgu_t, "ff_in")
    dh2, loss_local, d_final_g = _ffn_down_loss(ff, wdown, h1, target, gf, "ff_down_loss")

    d_wdown = _mm_tn(ff, dh2, tm=FF_TN, tn=D, name="dw_ff_down")
    dgu = _ffn_d_hidden(dh2, wdown, gu, "d_ff")
    d_wgu_t = _mm_tn(dgu, hn, tm=FF_TN, tn=D, name="dw_ff_in")
    dh1, d_g2 = _ffn_d_in(dgu, wgu_t, h1, rstd2, g2, dh2, "d_hn")
    d_wout = _mm_tn(merged, dh1, tm=D, tn=D, name="dw_out")
    dp_m, dp_g, dproj = _merge_d(dh1, wout, p_m, p_g, pbig, "d_merged")
    dy_m = _mm(dp_m, wbm, nt=True, out_dtype=BF, tn=D, name="d_ym")
    dy_g = _mm(dp_g, wbg, nt=True, out_dtype=BF, tn=D, name="d_yg")
    d_wbm = _mm_tn(y_m, dp_m, tm=D, tn=D, name="dw_branch_m")
    d_wbg = _mm_tn(y_g, dp_g, tm=D, tn=D, name="dw_branch_g")
    fq = D_FF // 4
    gu4 = jnp.transpose(d_wgu_t.reshape(2, 2, 2, fq, D), (0, 2, 1, 3, 4)).reshape(4, 2 * fq, D)
    sq4 = jnp.concatenate([d_wbm.reshape(4, 256, D), d_wbg.reshape(4, 256, D), d_wout.reshape(4, 256, D)], axis=1)
    sums_a = reducer.partial_sums([sq4, d_wdown.reshape(4, fq, D), gu4], BF, "a")
    dqk_m, dproj, dsm_m, d_gbias, d_mhg, recv_a = _mlstm_bwd(dy_m, qk, pbig, small, gbias, mhg_f, m_cs, m_ns, dproj,
                                                              "mlstm_bwd", ride=sums_a)
    dproj, d_convwb = _conv_bwd(dqk_m, pbig, convw_f, convb, dproj, "conv_bwd")
    dproj, dsmall, d_a2p, d_a2b, d_ghg = _gla_bwd(dy_g, pbig, small, a2p, g_a2_b, ghg_f, g_ss, dsm_m, dproj, "gla_bwd")
    dproj = _place_small(dsmall, dproj, "dproj_small")
    d_win = _w_in_from_proj_rows(_mm_tn(dproj, xn1, tm=PROJ_TK, tn=D, name="dw_in"))
    pad = jnp.zeros((W_IN_BLOCK - W_IN_SHARD, D), F32)
    win4 = jnp.stack([jnp.concatenate([d_win[q * W_IN_SHARD:(q + 1) * W_IN_SHARD], pad], axis=0) for q in range(4)])
    sums_b = reducer.partial_sums([win4], BF, "b")
    dxn, recv_b = _mm(dproj, w_all, nt=False, out_dtype=F32, tn=D, tm=TM, name="d_xn", rider=_scatter_rider(sums_b),
                      single_buffer_b=True)
    dh_first, dx, d_g1 = _rms_bwd(dxn, h0, rstd1, g1, dh1, "rms1_bwd", split_first=True)

    small_sharded = [dh_first[FIRST_VALID:TM], d_convwb[0:4], d_a2p[8:24], d_mhg.reshape(NH, DV), d_ghg.reshape(NH, DV)]
    replicated = [d_g1, d_convwb[4:5], d_gbias[0:1, 0:2 * NH].reshape(1, 2, NH), d_a2b, d_g2, d_final_g.reshape(D)]
    small4 = jnp.broadcast_to(_pack_small(small_sharded + replicated, SMALL_GRAD_ROWS)[None], (4, SMALL_GRAD_ROWS, D))
    sums_c = reducer.partial_sums([small4], F32, "c")
    recv_c = reducer.scatter(sums_c, "c")
    sq, down, gu, win, smalls = reducer.finish(sums_a + sums_b + sums_c, recv_a + recv_b + recv_c, in_chip_order=[4])
    smalls = _unpack_small(smalls, [g.shape for g in small_sharded + replicated])
    me = 2 * lax.axis_index("x") + lax.axis_index("y")
    smalls = ([lax.dynamic_slice_in_dim(g, me * shp[1], shp[1], axis=1) for g, shp in zip(smalls, SMALL_SHARD_SHAPES)]
              + smalls[len(SMALL_SHARD_SHAPES):])
    grads = ([win[0:W_IN_SHARD], sq[0:256], sq[256:512], sq[512:768], gu[0:fq], gu[fq:2 * fq], down]
             + [g.reshape(g.size // g.shape[-1], g.shape[-1]) for g in smalls])
    return loss_local, dx, grads


class _Reducer:
    def partial_sums(self, items, dtype, tag):
        c = lax.axis_index("c")
        got = _swap_halves(items, "reduce_siblings_" + tag)
        sums = []
        for i, (a, g) in enumerate(zip(items, got)):
            rh, n = g.shape[1], g.shape[2]
            own = lax.dynamic_slice_in_dim(a, c * rh, rh, axis=1)
            sums.append(_add2(own.reshape(-1, n), g.reshape(-1, n), dtype, f"reduce_add2_{tag}{i}").reshape(g.shape))
        return sums

    def scatter(self, sums, tag):
        return list(_scatter_chips(sums, "reduce_chips_" + tag))

    def finish(self, sums, from_chips, in_chip_order):
        c = lax.axis_index("c")
        me = 2 * lax.axis_index("x") + lax.axis_index("y")
        full = []
        for i, (s, r) in enumerate(zip(sums, from_chips)):
            rh, n = s.shape[1], s.shape[2]
            mine = lax.dynamic_index_in_dim(s, me, 0, keepdims=False)
            sib = lax.dynamic_index_in_dim(r[3], 1 - c, 0, keepdims=False)
            own = jnp.concatenate([jnp.where(c == 0, mine, sib), jnp.where(c == 0, sib, mine)], axis=0)
            others = r.reshape(4, 2 * rh, n)
            if i in in_chip_order:
                by_chip = _by_chip(own, others)
                own, others = by_chip[0], jnp.stack(by_chip[1:])
            full.append(_add4(own, others, f"reduce_add4_{i}"))
        return full
```

```python
import jax
import jax.numpy as jnp
from jax import lax
from jax.experimental import pallas as pl
from jax.experimental.pallas import tpu as pltpu

F32 = jnp.float32
BF = jnp.bfloat16
HI = lax.Precision.HIGHEST
MESH = pl.DeviceIdType.MESH

D = 1024
N_META = 16
CHUNK = 128
EPS = 1e-6
NH = 4
DV = 256
DQK = 128
G_RANK = 16
G_TAU = 16.0
D_FF = 2816
TM = 512
FIRST_VALID = TM - N_META
CPB = TM // CHUNK
G_CHUNK = 256
G_CPB = TM // G_CHUNK
NEG = -1e30
N_BIG = 8192
CB_GQK, CB_GV, CB_GR, CB_MQK, CB_GM, CB_GG, CB_MV, CB_MO = range(8)
N_SMALL = 128
N_ALL = N_BIG + N_SMALL
PROJ_TK = N_ALL // 5
VMEM_LIMIT = 56 * 1024 * 1024

ADAM_LR, ADAM_B1, ADAM_B2, ADAM_EPS, ADAM_WD, ADAM_STEP = 0.001, 0.9, 0.999, 1e-08, 0.01, 10

NT_DIMS = (((1,), (1,)), ((), ()))
TN_DIMS = (((0,), (0,)), ((), ()))


def _nt(a, b, **kw):
    return lax.dot_general(a, b, NT_DIMS, preferred_element_type=F32, **kw)


def _tn(a, b, **kw):
    return lax.dot_general(a, b, TN_DIMS, preferred_element_type=F32, **kw)


def _nn(a, b, **kw):
    return jnp.dot(a, b, preferred_element_type=F32, **kw)


def _params(**kw):
    return pltpu.CompilerParams(vmem_limit_bytes=VMEM_LIMIT, **kw)


def _sigmoid(x):
    return 0.5 * jnp.tanh(0.5 * x) + 0.5


def _logsig(x):
    return jnp.minimum(x, 0.0) - jnp.log(1.0 + jnp.exp(-jnp.abs(x)))


def _mm_rows(rows):
    return 3 * TM if rows % (3 * TM) == 0 else TM


def _mm(a, b, *, nt, out_dtype, tn, tk=None, tm=None, name, rider=None, single_buffer_b=False):
    m, k = a.shape
    n = b.shape[0] if nt else b.shape[1]
    tk = k if tk is None else tk
    tm = _mm_rows(m) if tm is None else tm
    nk = k // tk
    nj, ni = n // tn, m // tm
    nr_in = len(rider["inputs"]) if rider else 0
    nr_out = len(rider["out_shapes"]) if rider else 0
    assert m % tm == 0 and n % tn == 0 and k % tk == 0
    dims = NT_DIMS if nt else (((1,), (0,)), ((), ()))

    def body(*refs):
        a_ref, b_ref = refs[:2]
        o_ref = refs[2 + nr_in]
        j, i, kk = pl.program_id(0), pl.program_id(1), pl.program_id(2)
        step = (j * ni + i) * nk + kk
        if rider:
            start, middle, finish = rider["make"](refs[2:2 + nr_in], refs[3 + nr_in:3 + nr_in + nr_out],
                                                  refs[3 + nr_in + nr_out:5 + nr_in + nr_out])
            pl.when(step == 0)(start)
            pl.when(step == (nj * ni * nk) // 2)(middle)

        part = lax.dot_general(a_ref[...].astype(BF), b_ref[...].astype(BF), dims, preferred_element_type=F32)
        if nk == 1:
            o_ref[...] = part.astype(o_ref.dtype)
        else:
            acc_ref = refs[-1]

            @pl.when(kk == 0)
            def _():
                acc_ref[...] = part

            @pl.when(jnp.logical_and(kk > 0, kk < nk - 1))
            def _():
                acc_ref[...] += part

            @pl.when(kk == nk - 1)
            def _():
                o_ref[...] = (acc_ref[...] + part).astype(o_ref.dtype)

        if rider:
            pl.when(step == nj * ni * nk - 1)(finish)

    outs = pl.pallas_call(
        body, grid=(nj, ni, nk),
        in_specs=[pl.BlockSpec((tm, tk), lambda j, i, kk: (i, kk)),
                  pl.BlockSpec((tn, tk) if nt else (tk, tn), (lambda j, i, kk: (j, kk)) if nt else (lambda j, i, kk: (kk, j)),
                               pipeline_mode=pl.Buffered(1) if single_buffer_b else None)]
                 + [ANY] * nr_in,
        out_specs=[pl.BlockSpec((tm, tn), lambda j, i, kk: (i, j))] + [ANY] * nr_out,
        out_shape=(jax.ShapeDtypeStruct((m, n), out_dtype),) + (tuple(rider["out_shapes"]) if rider else ()),
        scratch_shapes=(rider["sems"] if rider else []) + ([pltpu.VMEM((tm, tn), F32)] if nk > 1 else []),
        compiler_params=_params(), name=name)(a, b, *(rider["inputs"] if rider else []))
    return (outs[0], list(outs[1:])) if rider else outs[0]


def _mm_tn(a, b, *, tm, tn, tk=None, name):
    t, m = a.shape
    n = b.shape[1]
    tk = _mm_rows(t) if tk is None else tk
    assert t % tk == 0 and m % tm == 0 and n % tn == 0

    def body(a_ref, b_ref, o_ref):
        part = _tn(a_ref[...].astype(BF), b_ref[...].astype(BF))

        @pl.when(pl.program_id(2) == 0)
        def _():
            o_ref[...] = part

        @pl.when(pl.program_id(2) > 0)
        def _():
            o_ref[...] += part

    return pl.pallas_call(
        body, grid=(m // tm, n // tn, t // tk),
        in_specs=[pl.BlockSpec((tk, tm), lambda i, j, kk: (kk, i)), pl.BlockSpec((tk, tn), lambda i, j, kk: (kk, j))],
        out_specs=pl.BlockSpec((tm, tn), lambda i, j, kk: (i, j)),
        out_shape=jax.ShapeDtypeStruct((m, n), F32), compiler_params=_params(), name=name)(a, b)


ANY = pl.BlockSpec(memory_space=pl.ANY)


def _row_spec(width, col=0):
    return pl.BlockSpec((TM, width), lambda i: (i, col))


def _full_spec(shape):
    return pl.BlockSpec(shape, lambda i: (0,) * len(shape))


def _embed_norm(x0, first, g, rider, name):
    tp = x0.shape[0] + TM
    nb = tp // TM
    nri, nro = len(rider["inputs"]), len(rider["out_shapes"])

    def body(*refs):
        x_ref, f_ref, g_ref = refs[:3]
        h_ref, xn_ref, r_ref = refs[3 + nri:6 + nri]
        i = pl.program_id(0)
        start, middle, finish = rider["make"](refs[3:3 + nri], refs[6 + nri:6 + nri + nro], refs[6 + nri + nro:])
        pl.when(i == 0)(start)
        pl.when(i == nb // 2)(middle)
        x = jnp.where(i == 0, f_ref[...], x_ref[...])
        r = lax.rsqrt(jnp.mean(x * x, axis=1, keepdims=True) + EPS)
        h_ref[...] = x
        xn_ref[...] = (x * r * g_ref[...]).astype(BF)
        r_ref[...] = r
        pl.when(i == nb - 1)(finish)

    outs = pl.pallas_call(
        body, grid=(nb,),
        in_specs=[pl.BlockSpec((TM, D), lambda i: (jnp.maximum(i - 1, 0), 0)), _full_spec((TM, D)), _full_spec((1, D))] + [ANY] * nri,
        out_specs=[_row_spec(D), _row_spec(D), _row_spec(1)] + [ANY] * nro,
        out_shape=(jax.ShapeDtypeStruct((tp, D), F32), jax.ShapeDtypeStruct((tp, D), BF), jax.ShapeDtypeStruct((tp, 1), F32))
                  + tuple(rider["out_shapes"]),
        scratch_shapes=rider["sems"], compiler_params=_params(), name=name)(x0, first, g, *rider["inputs"])
    return outs[0], outs[1], outs[2], list(outs[3:])


def _rms_bwd(dxn, h, rstd, g, dres, name, split_first=False):
    tp = h.shape[0]

    def body(dxn_ref, h_ref, r_ref, g_ref, dres_ref, *outs):
        r = r_ref[...]
        xh = h_ref[...] * r
        dxn_v = dxn_ref[...].astype(F32)
        dxh = dxn_v * g_ref[...]
        dh = r * (dxh - xh * jnp.mean(dxh * xh, axis=1, keepdims=True)) + dres_ref[...]
        if split_first:
            first_ref, dh_ref, dg_ref = outs

            @pl.when(pl.program_id(0) == 0)
            def _():
                first_ref[...] = dh
        else:
            dh_ref, dg_ref = outs
        dh_ref[...] = dh
        part = jnp.sum(dxn_v * xh, axis=0, keepdims=True)

        @pl.when(pl.program_id(0) == 0)
        def _():
            dg_ref[...] = part

        @pl.when(pl.program_id(0) > 0)
        def _():
            dg_ref[...] += part

    if split_first:
        out_specs = [_full_spec((TM, D)), pl.BlockSpec((TM, D), lambda i: (jnp.maximum(i - 1, 0), 0)), _full_spec((1, D))]
        out_shape = (jax.ShapeDtypeStruct((TM, D), F32), jax.ShapeDtypeStruct((tp - TM, D), F32), jax.ShapeDtypeStruct((1, D), F32))
    else:
        out_specs = [_row_spec(D), _full_spec((1, D))]
        out_shape = (jax.ShapeDtypeStruct((tp, D), F32), jax.ShapeDtypeStruct((1, D), F32))
    return pl.pallas_call(
        body, grid=(tp // TM,),
        in_specs=[_row_spec(D), _row_spec(D), _row_spec(1), _full_spec((1, D)), _row_spec(D)],
        out_specs=out_specs, out_shape=out_shape, compiler_params=_params(), name=name)(dxn, h, rstd, g, dres)


def _shift_down(x, halo, k):
    rk = pltpu.roll(x, k, 0)
    io = lax.broadcasted_iota(jnp.int32, (8, x.shape[1]), 0)
    top = jnp.where(io < k, pltpu.roll(halo, k, 0), rk[0:8])
    return top if x.shape[0] == 8 else jnp.concatenate([top, rk[8:]], axis=0)


def _shift_up(x, nxt, k):
    n = x.shape[0]
    rk = pltpu.roll(x, n - k, 0)
    io = lax.broadcasted_iota(jnp.int32, (8, x.shape[1]), 0)
    bot = jnp.where(io >= 8 - k, pltpu.roll(nxt, 8 - k, 0), rk[n - 8:n])
    return jnp.concatenate([rk[:n - 8], bot], axis=0)


def _conv_pre(x, halo, w_ref, b_ref):
    c = x * w_ref[3:4, :] + b_ref[...]
    shifted = []
    for k in (1, 2, 3):
        s = _shift_down(x, halo, k)
        shifted.append(s)
        c = c + s * w_ref[3 - k:4 - k, :]
    return c, shifted


def _qk_scale():
    col = lax.broadcasted_iota(jnp.int32, (1, D), 1)
    return jnp.where(col < NH * DQK, DQK ** -0.5, 1.0).astype(F32)


def _halo_prev_spec():
    return pl.BlockSpec((8, D), lambda i: (jnp.maximum(i * (TM // 8) - 1, 0), CB_MQK))


def _conv_fwd(pbig, w, b, name):
    tp = pbig.shape[0]

    def body(x_ref, halo_ref, w_ref, b_ref, o_ref):
        x = x_ref[...].astype(F32)
        halo = jnp.where(pl.program_id(0) > 0, halo_ref[...].astype(F32), 0.0)
        c, _ = _conv_pre(x, halo, w_ref, b_ref)
        o_ref[...] = (c * _sigmoid(c) * _qk_scale()).astype(BF)

    return pl.pallas_call(
        body, grid=(tp // TM,),
        in_specs=[_row_spec(D, CB_MQK), _halo_prev_spec(), _full_spec((4, D)), _full_spec((1, D))],
        out_specs=_row_spec(D), out_shape=jax.ShapeDtypeStruct((tp, D), BF),
        compiler_params=_params(), name=name)(pbig, pbig, w, b)


def _conv_bwd(dqk, pbig, w, b, dproj, name):
    tp = pbig.shape[0]
    nb = tp // TM

    def d_conv_out(d, x, halo, w_ref, b_ref):
        c, shifted = _conv_pre(x, halo, w_ref, b_ref)
        sg = _sigmoid(c)
        return d * _qk_scale() * (sg * (1.0 + c * (1.0 - sg))), shifted

    def body(d_ref, dn_ref, x_ref, halo_ref, xn_ref, w_ref, b_ref, _, o_ref, dwb_ref):
        i = pl.program_id(0)
        x = x_ref[...].astype(F32)
        halo = jnp.where(i > 0, halo_ref[...].astype(F32), 0.0)
        dc, shifted = d_conv_out(d_ref[...], x, halo, w_ref, b_ref)
        dc_next, _ = d_conv_out(dn_ref[...], xn_ref[...].astype(F32), x[TM - 8:TM], w_ref, b_ref)
        nxt = jnp.where(i < nb - 1, dc_next, 0.0)
        acc = dc * w_ref[3:4, :]
        for k in (1, 2, 3):
            acc = acc + _shift_up(dc, nxt, k) * w_ref[3 - k:4 - k, :]
        o_ref[...] = acc.astype(BF)
        taps = [shifted[2], shifted[1], shifted[0], x]
        rows = [jnp.sum(dc * t, axis=0, keepdims=True) for t in taps] + [jnp.sum(dc, axis=0, keepdims=True)]
        io = lax.broadcasted_iota(jnp.int32, (8, D), 0)
        part = jnp.zeros((8, D), F32)
        for r, v in enumerate(rows):
            part = jnp.where(io == r, v, part)

        @pl.when(pl.program_id(0) == 0)
        def _():
            dwb_ref[...] = part

        @pl.when(pl.program_id(0) > 0)
        def _():
            dwb_ref[...] += part

    next8 = lambda col: pl.BlockSpec((8, D), lambda i: (jnp.minimum((i + 1) * (TM // 8), tp // 8 - 1), col))
    return pl.pallas_call(
        body, grid=(nb,),
        in_specs=[_row_spec(D), next8(0), _row_spec(D, CB_MQK), _halo_prev_spec(), next8(CB_MQK),
                  _full_spec((4, D)), _full_spec((1, D)), ANY],
        out_specs=[_row_spec(D, CB_MQK), _full_spec((8, D))],
        out_shape=(jax.ShapeDtypeStruct(dproj.shape, BF), jax.ShapeDtypeStruct((8, D), F32)),
        input_output_aliases={7: 0}, compiler_params=_params(), name=name)(dqk, dqk, pbig, pbig, pbig, w, b, dproj)


def _mm_fused(inputs, products, *, nt, m, n, tm, tn, outs, epilogue, name, nk=1, sub=None):
    dims = NT_DIMS if nt else (((1,), (0,)), ((), ()))
    nin = len(inputs)
    assert nk == 1 or (len(products) == 1 and sub is None)

    def body(*refs):
        in_refs, out_refs = refs[:nin], refs[nin:nin + len(outs)]
        i = pl.program_id(1)
        if sub is not None:
            lhs = {ia: in_refs[ia][...].astype(BF) for ia, _ in products}

            def dots(cols):
                return [lax.dot_general(lhs[ia], (in_refs[ib][cols, :] if nt else in_refs[ib][:, cols]).astype(BF),
                                        dims, preferred_element_type=F32) for ia, ib in products]

            slices = [slice(s, min(s + sub, tn)) for s in range(0, tn, sub)]
            prods = dots(slices[0])
            for idx, cols in enumerate(slices):
                nxt = dots(slices[idx + 1]) if idx + 1 < len(slices) else None
                epilogue(prods, in_refs, out_refs, i, cols)
                prods = nxt
            return
        prods = [lax.dot_general(in_refs[ia][...].astype(BF), in_refs[ib][...].astype(BF), dims, preferred_element_type=F32)
                 for ia, ib in products]
        if nk == 1:
            epilogue(prods, in_refs, out_refs, i, slice(None))
            return
        acc_ref = refs[-1]
        kk = pl.program_id(2)

        @pl.when(kk == 0)
        def _():
            acc_ref[...] = prods[0]

        @pl.when(jnp.logical_and(kk > 0, kk < nk - 1))
        def _():
            acc_ref[...] += prods[0]

        @pl.when(kk == nk - 1)
        def _():
            epilogue([acc_ref[...] + prods[0]], in_refs, out_refs, i, slice(None))

    return pl.pallas_call(
        body, grid=(n // tn, m // tm, nk), in_specs=[s for _, s in inputs], out_specs=[s for _, s in outs],
        out_shape=tuple(sh for sh, _ in outs), scratch_shapes=[pltpu.VMEM((tm, tn), F32)] if nk > 1 else [],
        compiler_params=_params(), name=name)(*[a for a, _ in inputs])


SUB_COLS = 256


def _cols_at(cols, offset):
    return slice(cols.start + offset, cols.stop + offset)


def _blk(rows, width, col=None, row=None):
    return pl.BlockSpec((rows, width), lambda j, i, kk: ((i if row is None else row(i)), (0 if col is None else col(j, kk))))


FF_TN = D_FF // 2


def _ffn_weight_rows(wg_t, wu_t):
    return jnp.concatenate([wg_t[0:FF_TN], wu_t[0:FF_TN], wg_t[FF_TN:], wu_t[FF_TN:]], axis=0)


def _ffn_in(hn, wgu_t, name):
    tp = hn.shape[0]
    tm = _mm_rows(tp)

    def epilogue(prods, in_refs, out_refs, i, cols):
        g, u = prods
        out_refs[0][:, cols] = g.astype(BF)
        out_refs[0][:, _cols_at(cols, FF_TN)] = u.astype(BF)
        out_refs[1][:, cols] = (g * _sigmoid(g) * u).astype(BF)

    wspec = lambda off: pl.BlockSpec((FF_TN, D), lambda j, i, kk: (2 * j + off, 0))
    return _mm_fused(
        [(hn, _blk(tm, D)), (wgu_t, wspec(0)), (wgu_t, wspec(1))], [(0, 1), (0, 2)], nt=True, m=tp, n=D_FF, tm=tm, tn=FF_TN,
        outs=[(jax.ShapeDtypeStruct((tp, 2 * D_FF), BF), _blk(tm, 2 * FF_TN, lambda j, kk: j)),
              (jax.ShapeDtypeStruct((tp, D_FF), BF), _blk(tm, FF_TN, lambda j, kk: j))],
        epilogue=epilogue, name=name, sub=SUB_COLS)


def _ffn_down_loss(ff, wdown, h1, target, gf, name):
    tp = ff.shape[0]

    def epilogue(prods, in_refs, out_refs, i, cols):
        live = (i > 0).astype(F32)
        g = in_refs[4][...]
        x = prods[0] + in_refs[2][...]
        r = lax.rsqrt(jnp.mean(x * x, axis=1, keepdims=True) + EPS)
        xh = x * r
        e = xh * g - in_refs[3][...]
        loss_part = 0.5 * live * jnp.sum(jnp.mean(e * e, axis=1, keepdims=True), axis=0, keepdims=True)
        dout = e * (live / D)
        dg_part = jnp.sum(dout * xh, axis=0, keepdims=True)
        dxh = dout * g
        out_refs[0][...] = r * (dxh - xh * jnp.mean(dxh * xh, axis=1, keepdims=True))

        @pl.when(i == 0)
        def _():
            out_refs[1][...] = loss_part
            out_refs[2][...] = dg_part

        @pl.when(i > 0)
        def _():
            out_refs[1][...] += loss_part
            out_refs[2][...] += dg_part

    const = lambda shape: pl.BlockSpec(shape, lambda j, i, kk: (0,) * len(shape))
    return _mm_fused(
        [(ff, _blk(TM, D_FF)), (wdown, const((D_FF, D))), (h1, _blk(TM, D)),
         (target, _blk(TM, D, row=lambda i: jnp.maximum(i - 1, 0))), (gf, const((1, D)))],
        [(0, 1)], nt=False, m=tp, n=D, tm=TM, tn=D,
        outs=[(jax.ShapeDtypeStruct((tp, D), F32), _blk(TM, D)), (jax.ShapeDtypeStruct((1, 1), F32), const((1, 1))),
              (jax.ShapeDtypeStruct((1, D), F32), const((1, D)))],
        epilogue=epilogue, name=name)


def _ffn_d_hidden(dh2, wdown, gu, name):
    tp = dh2.shape[0]
    tm = _mm_rows(tp)

    def epilogue(prods, in_refs, out_refs, i, cols):
        d = prods[0]
        g = in_refs[2][:, cols].astype(F32)
        u = in_refs[2][:, _cols_at(cols, FF_TN)].astype(F32)
        sg = _sigmoid(g)
        out_refs[0][:, cols] = (d * u * sg * (1.0 + g * (1.0 - sg))).astype(BF)
        out_refs[0][:, _cols_at(cols, FF_TN)] = (d * g * sg).astype(BF)

    return _mm_fused(
        [(dh2, _blk(tm, D)), (wdown, pl.BlockSpec((FF_TN, D), lambda j, i, kk: (j, 0), pipeline_mode=pl.Buffered(1))),
         (gu, _blk(tm, 2 * FF_TN, lambda j, kk: j))],
        [(0, 1)], nt=True, m=tp, n=D_FF, tm=tm, tn=FF_TN,
        outs=[(jax.ShapeDtypeStruct((tp, 2 * D_FF), BF), _blk(tm, 2 * FF_TN, lambda j, kk: j))],
        epilogue=epilogue, name=name, sub=SUB_COLS)[0]


def _ffn_d_in(dgu, wgu_t, h1, rstd, g2, dh2, name):
    tp = dgu.shape[0]

    def epilogue(prods, in_refs, out_refs, i, cols):
        r = in_refs[3][...]
        xh = in_refs[2][...] * r
        dxn = prods[0]
        dxh = dxn * in_refs[4][...]
        out_refs[0][...] = r * (dxh - xh * jnp.mean(dxh * xh, axis=1, keepdims=True)) + in_refs[5][...]
        part = jnp.sum(dxn * xh, axis=0, keepdims=True)

        @pl.when(i == 0)
        def _():
            out_refs[1][...] = part

        @pl.when(i > 0)
        def _():
            out_refs[1][...] += part

    const = lambda shape: pl.BlockSpec(shape, lambda j, i, kk: (0,) * len(shape))
    return _mm_fused(
        [(dgu, _blk(TM, 2 * D_FF)), (wgu_t, const((2 * D_FF, D))),
         (h1, _blk(TM, D)), (rstd, _blk(TM, 1)), (g2, const((1, D))), (dh2, _blk(TM, D))],
        [(0, 1)], nt=False, m=tp, n=D, tm=TM, tn=D,
        outs=[(jax.ShapeDtypeStruct((tp, D), F32), _blk(TM, D)), (jax.ShapeDtypeStruct((1, D), F32), const((1, D)))],
        epilogue=epilogue, name=name)


def _branch_merge(y_m, y_g, wbm, wbg, pbig, name):
    tp = y_m.shape[0]

    def epilogue(prods, in_refs, out_refs, i, cols):
        pm, pg = prods[0].astype(BF), prods[1].astype(BF)
        out_refs[0][:, cols] = pm
        out_refs[1][:, cols] = pg
        out_refs[2][:, cols] = (_sigmoid(in_refs[4][:, cols].astype(F32)) * pm.astype(F32)
                                + _sigmoid(in_refs[5][:, cols].astype(F32)) * pg.astype(F32)).astype(BF)

    const = lambda shape: pl.BlockSpec(shape, lambda j, i, kk: (0,) * len(shape))
    shp = jax.ShapeDtypeStruct((tp, D), BF)
    return _mm_fused(
        [(y_m, _blk(TM, D)), (wbm, const((D, D))), (y_g, _blk(TM, D)), (wbg, const((D, D))),
         (pbig, _blk(TM, D, lambda j, kk: CB_GM)), (pbig, _blk(TM, D, lambda j, kk: CB_GG))],
        [(0, 1), (2, 3)], nt=False, m=tp, n=D, tm=TM, tn=D,
        outs=[(shp, _blk(TM, D)), (shp, _blk(TM, D)), (shp, _blk(TM, D))], epilogue=epilogue, name=name, sub=SUB_COLS)


def _merge_d(dh1, wout, pm, pg, pbig, name):
    tp = dh1.shape[0]

    def epilogue(prods, in_refs, out_refs, i, cols):
        d = prods[0]
        sm = _sigmoid(in_refs[4][:, cols].astype(F32))
        sg = _sigmoid(in_refs[5][:, cols].astype(F32))
        out_refs[0][:, cols] = (d * sm).astype(BF)
        out_refs[1][:, cols] = (d * sg).astype(BF)
        out_refs[2][:, cols] = (d * in_refs[2][:, cols].astype(F32) * sm * (1.0 - sm)).astype(BF)
        out_refs[2][:, _cols_at(cols, D)] = (d * in_refs[3][:, cols].astype(F32) * sg * (1.0 - sg)).astype(BF)

    const = lambda shape: pl.BlockSpec(shape, lambda j, i, kk: (0,) * len(shape))
    shp = jax.ShapeDtypeStruct((tp, D), BF)
    return _mm_fused(
        [(dh1, _blk(TM, D)), (wout, const((D, D))), (pm, _blk(TM, D)), (pg, _blk(TM, D)),
         (pbig, _blk(TM, D, lambda j, kk: CB_GM)), (pbig, _blk(TM, D, lambda j, kk: CB_GG))],
        [(0, 1)], nt=True, m=tp, n=D, tm=TM, tn=D,
        outs=[(shp, _blk(TM, D)), (shp, _blk(TM, D)),
              (jax.ShapeDtypeStruct((tp, N_ALL), BF), _blk(TM, 2 * D, lambda j, kk: CB_GM // 2))],
        epilogue=epilogue, name=name, sub=SUB_COLS)


def _out_proj_norm(merged, wout, h0, g2, name):
    tp = merged.shape[0]
    tm = _mm_rows(tp)

    def epilogue(prods, in_refs, out_refs, i, cols):
        x = prods[0] + in_refs[2][...]
        r = lax.rsqrt(jnp.mean(x * x, axis=1, keepdims=True) + EPS)
        out_refs[0][...] = x
        out_refs[1][...] = (x * r * in_refs[3][...]).astype(BF)
        out_refs[2][...] = r

    const = lambda shape: pl.BlockSpec(shape, lambda j, i, kk: (0,) * len(shape))
    return _mm_fused(
        [(merged, _blk(tm, D)), (wout, const((D, D))), (h0, _blk(tm, D)), (g2, const((1, D)))],
        [(0, 1)], nt=False, m=tp, n=D, tm=tm, tn=D,
        outs=[(jax.ShapeDtypeStruct((tp, D), F32), _blk(tm, D)), (jax.ShapeDtypeStruct((tp, D), BF), _blk(tm, D)),
              (jax.ShapeDtypeStruct((tp, 1), F32), _blk(tm, 1))],
        epilogue=epilogue, name=name)


def _adamw(w, g, m, v, name):
    rows, cols = w.shape
    by_cols = rows % 128 != 0 and cols % 128 == 0 and rows * cols > 128 * 1024
    tr = rows if (by_cols or rows % 128 != 0) else 128
    tc = 128 if by_cols else cols

    def body(w_ref, g_ref, m_ref, v_ref, d_ref, nm_ref, nv_ref):
        gv = g_ref[...]
        nm = ADAM_B1 * m_ref[...] + (1.0 - ADAM_B1) * gv
        nv = ADAM_B2 * v_ref[...] + (1.0 - ADAM_B2) * (gv * gv)
        m_hat = nm / (1.0 - ADAM_B1 ** ADAM_STEP)
        v_hat = nv / (1.0 - ADAM_B2 ** ADAM_STEP)
        d_ref[...] = -ADAM_LR * (m_hat / (jnp.sqrt(v_hat) + ADAM_EPS) + ADAM_WD * w_ref[...])
        nm_ref[...] = nm
        nv_ref[...] = nv

    spec = pl.BlockSpec((tr, tc), (lambda i: (0, i)) if by_cols else (lambda i: (i, 0)))
    shp = jax.ShapeDtypeStruct((rows, cols), F32)
    return pl.pallas_call(body, grid=(cols // tc if by_cols else rows // tr,), in_specs=[spec] * 4, out_specs=[spec] * 3,
                          out_shape=(shp,) * 3, compiler_params=_params(), name=name)(w, g, m, v)


def _place_small(dsmall, dproj, name):
    tp = dsmall.shape[0]

    def body(s_ref, _, o_ref):
        o_ref[...] = s_ref[...]

    return pl.pallas_call(
        body, grid=(tp // TM,), in_specs=[_row_spec(N_SMALL), ANY], out_specs=_row_spec(N_SMALL, N_BIG // N_SMALL),
        out_shape=jax.ShapeDtypeStruct(dproj.shape, dproj.dtype), input_output_aliases={1: 0},
        compiler_params=_params(), name=name)(dsmall, dproj)


def _row_tile(rows, cap=512):
    best = rows
    for cand in range(8, min(rows, cap) + 1, 8):
        if rows % cand == 0:
            best = cand
    return best


def _add2(a, b, out_dtype, name):
    rows, cols = a.shape
    tr = _row_tile(rows)

    def body(a_ref, b_ref, o_ref):
        o_ref[...] = (a_ref[...] + b_ref[...]).astype(o_ref.dtype)

    spec = pl.BlockSpec((tr, cols), lambda i: (i, 0))
    return pl.pallas_call(body, grid=(rows // tr,), in_specs=[spec] * 2, out_specs=spec,
                          out_shape=jax.ShapeDtypeStruct((rows, cols), out_dtype), compiler_params=_params(), name=name)(a, b)


def _add4(first, rest, name):
    rows, cols = first.shape
    tr = _row_tile(rows, 256)

    def body(f_ref, r_ref, o_ref):
        up = lambda v: v.astype(F32)
        o_ref[...] = ((up(f_ref[...]) + up(r_ref[0])) + up(r_ref[1])) + up(r_ref[2])

    return pl.pallas_call(body, grid=(rows // tr,),
                          in_specs=[pl.BlockSpec((tr, cols), lambda i: (i, 0)), pl.BlockSpec((3, tr, cols), lambda i: (0, i, 0))],
                          out_specs=pl.BlockSpec((tr, cols), lambda i: (i, 0)),
                          out_shape=jax.ShapeDtypeStruct((rows, cols), F32), compiler_params=_params(), name=name)(first, rest)


def _chunk_consts(length=CHUNK):
    r2 = lax.broadcasted_iota(jnp.int32, (length, length), 0)
    c2 = lax.broadcasted_iota(jnp.int32, (length, length), 1)
    tri = r2 >= c2
    return dict(tri=tri, tril_f=tri.astype(F32), triu_f=(r2 <= c2).astype(F32),
                lane=lax.broadcasted_iota(jnp.int32, (length, N_SMALL), 1),
                rowio=lax.broadcasted_iota(jnp.int32, (length, 1), 0),
                ones=jnp.ones((length, N_SMALL), F32))


def _valid_rows(block, c):
    row = block * TM + c * CHUNK + lax.broadcasted_iota(jnp.int32, (CHUNK, 1), 0)
    return row >= FIRST_VALID


def _col(x, lane, idx):
    return jnp.sum(jnp.where(lane == idx, x, 0.0), axis=1, keepdims=True)


def _last_row(x, rowio):
    return jnp.sum(jnp.where(rowio == rowio.shape[0] - 1, x, 0.0), axis=0, keepdims=True)


def _sum_all(x):
    return jnp.sum(jnp.sum(x, axis=1, keepdims=True), axis=0, keepdims=True)


def _headnorm_fwd(hm, gain, gate_act):
    rs = lax.rsqrt(jnp.mean(hm * hm, axis=1, keepdims=True) + EPS)
    return hm * rs * gain * gate_act


def _headnorm_bwd(dy, hm, gain, gate_act):
    rs = lax.rsqrt(jnp.mean(hm * hm, axis=1, keepdims=True) + EPS)
    xh = hm * rs
    dact = dy * xh * gain
    dgain = jnp.sum(dy * gate_act * xh, axis=0, keepdims=True)
    dxh = dy * gate_act * gain
    dhm = rs * (dxh - xh * jnp.mean(dxh * xh, axis=1, keepdims=True))
    return dhm, dact, dgain


def _mlstm_gates(sm, gbias, valid, k):
    pre = sm + gbias
    lf = jnp.where(valid, _logsig(pre), 0.0)
    b_all = _nn(k["tril_f"], lf, precision=HI)
    li_all = jnp.where(valid, pre, NEG)
    return pre, li_all, b_all


def _mlstm_open(h, qh, kh, c_st, li_all, b_all, k):
    lane = k["lane"]
    sel = jnp.where(lane == h, 1.0, 0.0) - jnp.where(lane == NH + h, 1.0, 0.0)
    x = jnp.where(lane < NH, li_all, jnp.where(lane < 2 * NH, b_all, 0.0))
    cb = c_st.astype(BF)
    return dict(ubc=_nt(sel, x, precision=HI), sim=_nt(qh, kh), cb=cb, cq=_nt(qh, cb))


def _mlstm_weights(h, f, qh, vh, li_all, b_all, n_row, m11, k):
    lane, tri, rowio = k["lane"], k["tri"], k["rowio"]
    b_col = _col(b_all, lane, NH + h)
    li_col = _col(li_all, lane, h)
    dmat = jnp.where(tri, b_col + f["ubc"], NEG)
    m_row = jnp.maximum(b_col + m11, jnp.max(dmat, axis=1, keepdims=True))
    e = jnp.exp(dmat - m_row)
    w_mat = e * f["sim"]
    a = jnp.exp(b_col + m11 - m_row)
    qf = qh.astype(F32)
    nq = jnp.sum(qf * n_row, axis=1, keepdims=True)
    g = _last_row(b_col, rowio)
    wlog = g - b_col + li_col
    m_new = jnp.maximum(g + m11, jnp.max(wlog, axis=0, keepdims=True))
    a_s = jnp.exp(g + m11 - m_new)
    w = jnp.exp(wlog - m_new)
    return dict(f, e=e, w_mat=w_mat, a=a, qf=qf, nq=nq, m_row=m_row, m_new=m_new, a_s=a_s, w=w,
                wv=_nn(w_mat.astype(BF), vh))


def _mlstm_out(f):
    num = f["a"] * f["cq"] + f["wv"]
    den = f["a"] * f["nq"] + jnp.sum(f["w_mat"], axis=1, keepdims=True)
    floor = jnp.exp(-f["m_row"])
    r = jnp.maximum(jnp.abs(den), floor)
    return dict(f, den=den, floor=floor, r=r, hm=num / r)


def _mlstm_fwd(qk, pbig, small, gbias, headg, name):
    tp = qk.shape[0]
    nb = tp // TM

    def body(qk_ref, v_ref, mo_ref, sm_ref, gb_ref, hg_ref, y_ref, cs_ref, ns_ref, c_scr, n_scr):
        blk = pl.program_id(0)

        @pl.when(blk == 0)
        def _():
            c_scr[...] = jnp.zeros_like(c_scr)
            n_scr[...] = jnp.zeros_like(n_scr)

        k = _chunk_consts()
        io8 = lax.broadcasted_iota(jnp.int32, (8, DQK), 0)

        def chunk(c, carry):
            r0 = pl.multiple_of(c * CHUNK, CHUNK)
            rows = pl.ds(r0, CHUNK)
            valid = _valid_rows(blk, c)
            _, li_all, b_all = _mlstm_gates(sm_ref[rows, :], gb_ref[...], valid, k)
            heads = range(NH)
            qs = [qk_ref[rows, h * DQK:(h + 1) * DQK] for h in heads]
            ks = [qk_ref[rows, NH * DQK + h * DQK:NH * DQK + (h + 1) * DQK] for h in heads]
            vs = [v_ref[rows, h * DV:(h + 1) * DV] for h in heads]
            cst = [c_scr[h] for h in heads]
            nrow = [n_scr[h, 0:1, :] for h in heads]
            m11 = [jnp.max(n_scr[h, 1:2, :], axis=1, keepdims=True) for h in heads]
            f = [_mlstm_open(h, qs[h], ks[h], cst[h], li_all, b_all, k) for h in heads]
            f = [_mlstm_weights(h, f[h], qs[h], vs[h], li_all, b_all, nrow[h], m11[h], k) for h in heads]
            wk = [f[h]["w"] * ks[h].astype(F32) for h in heads]
            kv = [_tn(vs[h], wk[h].astype(BF)) for h in heads]
            for h in heads:
                hm = _mlstm_out(f[h])["hm"]
                gate = _sigmoid(mo_ref[rows, h * DV:(h + 1) * DV].astype(F32))
                y_ref[rows, h * DV:(h + 1) * DV] = _headnorm_fwd(hm, hg_ref[:, h * DV:(h + 1) * DV], gate).astype(BF)
                cs_ref[c, h] = f[h]["cb"]
                ns_ref[c, h] = jnp.where(io8 == 0, nrow[h], jnp.where(io8 == 1, m11[h], 0.0))
                c_scr[h] = f[h]["a_s"] * cst[h] + kv[h]
                n_scr[h, 0:1, :] = f[h]["a_s"] * nrow[h] + jnp.sum(wk[h], axis=0, keepdims=True)
                n_scr[h, 1:2, :] = jnp.broadcast_to(f[h]["m_new"], (1, DQK))
            return carry

        lax.fori_loop(0, CPB, chunk, 0, unroll=2)

    return pl.pallas_call(
        body, grid=(nb,),
        in_specs=[_row_spec(D), _row_spec(D, CB_MV), _row_spec(D, CB_MO), _row_spec(N_SMALL), _full_spec((1, N_SMALL)), _full_spec((1, D))],
        out_specs=[_row_spec(D), pl.BlockSpec((CPB, NH, DV, DQK), lambda i: (i, 0, 0, 0)),
                   pl.BlockSpec((CPB, NH, 8, DQK), lambda i: (i, 0, 0, 0))],
        out_shape=(jax.ShapeDtypeStruct((tp, D), BF), jax.ShapeDtypeStruct((tp // CHUNK, NH, DV, DQK), BF),
                   jax.ShapeDtypeStruct((tp // CHUNK, NH, 8, DQK), F32)),
        scratch_shapes=[pltpu.VMEM((NH, DV, DQK), F32), pltpu.VMEM((NH, 8, DQK), F32)],
        compiler_params=_params(), name=name)(qk, pbig, pbig, small, gbias, headg)


def _mlstm_bwd(dy, qk, pbig, small, gbias, headg, cs, ns, dproj, name, ride=()):
    tp = qk.shape[0]
    nb = tp // TM
    nr = len(ride)

    def body(*refs):
        dy_ref, qk_ref, v_ref, mo_ref, sm_ref, gb_ref, hg_ref, cs_ref, ns_ref = refs[:9]
        ride_in = refs[10:10 + nr]
        dqk_ref, dproj_ref, dsm_ref, dgb_ref, dhg_ref = refs[10 + nr:15 + nr]
        ride_out = refs[15 + nr:15 + 2 * nr]
        dc_scr, dn_scr = refs[15 + 2 * nr:17 + 2 * nr]
        step = pl.program_id(0)
        blk = nb - 1 - step
        sent = _scatter_copies(ride_in, ride_out, *refs[17 + 2 * nr:]) if nr else []

        @pl.when(step == 0)
        def _():
            dc_scr[...] = jnp.zeros_like(dc_scr)
            dn_scr[...] = jnp.zeros_like(dn_scr)
            dgb_ref[...] = jnp.zeros_like(dgb_ref)
            dhg_ref[...] = jnp.zeros_like(dhg_ref)
            for cp in sent:
                cp.start()

        k = _chunk_consts()
        lane, rowio = k["lane"], k["rowio"]

        def chunk(cc, carry):
            c = CPB - 1 - cc
            r0 = pl.multiple_of(c * CHUNK, CHUNK)
            rows = pl.ds(r0, CHUNK)
            valid = _valid_rows(blk, c)
            pre, li_all, b_all = _mlstm_gates(sm_ref[rows, :], gb_ref[...], valid, k)
            dli_all = jnp.zeros((CHUNK, N_SMALL), F32)
            db_all = jnp.zeros((CHUNK, N_SMALL), F32)
            heads = range(NH)
            qs = [qk_ref[rows, h * DQK:(h + 1) * DQK] for h in heads]
            ks = [qk_ref[rows, NH * DQK + h * DQK:NH * DQK + (h + 1) * DQK] for h in heads]
            vs = [v_ref[rows, h * DV:(h + 1) * DV] for h in heads]
            cst = [cs_ref[c, h].astype(F32) for h in heads]
            nrow = [ns_ref[c, h, 0:1, :] for h in heads]
            m11 = [jnp.max(ns_ref[c, h, 1:2, :], axis=1, keepdims=True) for h in heads]
            f = [_mlstm_open(h, qs[h], ks[h], cst[h], li_all, b_all, k) for h in heads]
            f = [_mlstm_weights(h, f[h], qs[h], vs[h], li_all, b_all, nrow[h], m11[h], k) for h in heads]
            f = [_mlstm_out(f[h]) for h in heads]
            t = []
            for h in heads:
                gain = hg_ref[:, h * DV:(h + 1) * DV]
                gate = _sigmoid(mo_ref[rows, h * DV:(h + 1) * DV].astype(F32))
                dhm, dgate, dgain = _headnorm_bwd(dy_ref[rows, h * DV:(h + 1) * DV].astype(F32), f[h]["hm"], gain, gate)
                dproj_ref[rows, D + h * DV:D + (h + 1) * DV] = (dgate * gate * (1.0 - gate)).astype(BF)
                dhg_ref[:, h * DV:(h + 1) * DV] += dgain
                r, den = f[h]["r"], f[h]["den"]
                dnum = dhm / r
                dr = -jnp.sum(dhm * f[h]["hm"], axis=1, keepdims=True) / r
                dden = jnp.where(jnp.abs(den) > f[h]["floor"], dr * jnp.sign(den), 0.0)
                dnb = dnum.astype(BF)
                dc_new = dc_scr[h]
                dcb = dc_new.astype(BF)
                t.append(dict(dnum=dnum, dden=dden, dnb=dnb, dc_new=dc_new, dn_new=dn_scr[h],
                              dwm=_nt(dnb, vs[h]), vdc=_nn(vs[h], dcb), kdc=_nt(ks[h], dcb)))
            for h in heads:
                dw_mat = t[h]["dwm"] + t[h]["dden"]
                dsim = (f[h]["e"] * dw_mat).astype(BF)
                gm = f[h]["w_mat"] * dw_mat
                t[h].update(gm=gm, dv0=_tn(f[h]["w_mat"].astype(BF), t[h]["dnb"]), dq0=_nn(dsim, ks[h]),
                            dq1=_nn(t[h]["dnb"], f[h]["cb"]), dk0=_tn(dsim, qs[h]),
                            dcq=_tn((f[h]["a"] * t[h]["dnum"]).astype(BF), qs[h]), cs2=_tn(gm, k["ones"], precision=HI))
            for h in heads:
                a, w, a_s = f[h]["a"], f[h]["w"], f[h]["a_s"]
                dnum, dden, dc_new, dn_new, vdc, gm = (t[h][n] for n in ("dnum", "dden", "dc_new", "dn_new", "vdc", "gm"))
                kf = ks[h].astype(F32)
                dproj_ref[rows, h * DV:(h + 1) * DV] = (t[h]["dv0"] + w * t[h]["kdc"]).astype(BF)
                adden = a * dden
                dqk_ref[rows, h * DQK:(h + 1) * DQK] = t[h]["dq0"] + a * t[h]["dq1"] + adden * nrow[h]
                dqk_ref[rows, NH * DQK + h * DQK:NH * DQK + (h + 1) * DQK] = t[h]["dk0"] + w * vdc + w * dn_new
                da = jnp.sum(dnum * f[h]["cq"], axis=1, keepdims=True) + dden * f[h]["nq"]
                dw = jnp.sum(vdc * kf, axis=1, keepdims=True) + jnp.sum(kf * dn_new, axis=1, keepdims=True)
                da_s = _sum_all(dc_new * cst[h]) + jnp.sum(dn_new * nrow[h], axis=1, keepdims=True)
                wdw = w * dw
                rs = jnp.sum(gm, axis=1, keepdims=True)
                cs_col = _col(t[h]["cs2"], lane, 0)
                dg = a_s * da_s + jnp.sum(wdw, axis=0, keepdims=True)
                db = a * da + rs - cs_col - wdw + jnp.where(rowio == CHUNK - 1, dg, 0.0)
                dli_all = dli_all + jnp.where(lane == h, cs_col + wdw, 0.0)
                db_all = db_all + jnp.where(lane == NH + h, db, 0.0)
                dc_scr[h] = a_s * dc_new + t[h]["dcq"]
                dn_scr[h] = a_s * dn_new + jnp.sum(adden * f[h]["qf"], axis=0, keepdims=True)
            dlf_all = _nn(k["triu_f"], db_all, precision=HI)
            dsm = jnp.where(valid, dli_all + dlf_all * _sigmoid(-pre), 0.0)
            dsm = jnp.where(lane < 2 * NH, dsm, 0.0)
            dsm_ref[rows, :] = dsm
            dgb_ref[0:1, :] += jnp.sum(dsm, axis=0, keepdims=True)
            return carry

        lax.fori_loop(0, CPB, chunk, 0, unroll=2)

        if nr:
            @pl.when(step == nb - 1)
            def _():
                for cp in sent:
                    cp.wait_recv()
                for cp in sent:
                    cp.wait_send()

    rev = lambda col: (lambda i: (nb - 1 - i, col))
    rspec = lambda width, col=0: pl.BlockSpec((TM, width), rev(col))
    ride_shapes, ride_sems = _scatter_shapes(ride) if nr else ((), [])
    outs = pl.pallas_call(
        body, grid=(nb,),
        in_specs=[rspec(D), rspec(D), rspec(D, CB_MV), rspec(D, CB_MO), rspec(N_SMALL), _full_spec((1, N_SMALL)), _full_spec((1, D)),
                  pl.BlockSpec((CPB, NH, DV, DQK), lambda i: (nb - 1 - i, 0, 0, 0)),
                  pl.BlockSpec((CPB, NH, 8, DQK), lambda i: (nb - 1 - i, 0, 0, 0)), ANY] + [ANY] * nr,
        out_specs=[rspec(D), rspec(2 * D, CB_MV // 2), rspec(N_SMALL), _full_spec((8, N_SMALL)), _full_spec((1, D))] + [ANY] * nr,
        out_shape=(jax.ShapeDtypeStruct((tp, D), F32), jax.ShapeDtypeStruct(dproj.shape, BF),
                   jax.ShapeDtypeStruct((tp, N_SMALL), F32), jax.ShapeDtypeStruct((8, N_SMALL), F32),
                   jax.ShapeDtypeStruct((1, D), F32)) + tuple(ride_shapes),
        scratch_shapes=[pltpu.VMEM((NH, DV, DQK), F32), pltpu.VMEM((NH, 1, DQK), F32)] + ride_sems,
        input_output_aliases={9: 1}, compiler_params=_params(), name=name)(dy, qk, pbig, pbig, small, gbias, headg, cs, ns, dproj, *ride)
    return tuple(outs[:5]) + (list(outs[5:]),)


def _gla_loga(sm_ref, a2_ref, a2b_ref, blk):
    za = _nn(sm_ref[...].astype(BF), a2_ref[...]) + a2b_ref[...]
    row = blk * TM + lax.broadcasted_iota(jnp.int32, (TM, 1), 0)
    return za, jnp.where(row >= FIRST_VALID, _logsig(za) / G_TAU, 0.0)


def _gla_head(h, q_ref, k_ref, rows, bc, btot, k):
    sl = slice(h * DQK, (h + 1) * DQK)
    bch = bc[:, sl]
    bth = btot[:, sl]
    gq = q_ref[rows, h * DQK:(h + 1) * DQK].astype(F32)
    gk = k_ref[rows, NH * DQK + h * DQK:NH * DQK + (h + 1) * DQK].astype(F32)
    e_pos = jnp.exp(bch) * (DQK ** -0.5)
    e_neg = jnp.exp(-bch)
    e_end = jnp.exp(bth - bch)
    qd = gq * e_pos
    ki = gk * e_neg
    ke = gk * e_end
    att = jnp.where(k["tri"], _nt(qd.astype(BF), ki.astype(BF)), 0.0)
    return dict(e_pos=e_pos, e_neg=e_neg, e_end=e_end, qd=qd, ki=ki, ke=ke, att=att, decay=jnp.exp(bth))


def _gla_fwd(pbig, small, a2p, a2b, headg, name):
    tp = pbig.shape[0]
    nb = tp // TM

    def body(qk_ref, v_ref, gr_ref, sm_ref, a2_ref, a2b_ref, hg_ref, y_ref, ss_ref, s_scr, lg_scr):
        blk = pl.program_id(0)

        @pl.when(blk == 0)
        def _():
            s_scr[...] = jnp.zeros_like(s_scr)

        k = _chunk_consts(G_CHUNK)
        _, loga = _gla_loga(sm_ref, a2_ref, a2b_ref, blk)
        lg_scr[...] = loga

        def chunk(c, carry):
            r0 = pl.multiple_of(c * G_CHUNK, G_CHUNK)
            rows = pl.ds(r0, G_CHUNK)
            bc = _nn(k["tril_f"], lg_scr[rows, :], precision=HI)
            btot = _last_row(bc, k["rowio"])
            heads = range(NH)
            f = [_gla_head(h, qk_ref, qk_ref, rows, bc, btot, k) for h in heads]
            vs = [v_ref[rows, h * DV:(h + 1) * DV] for h in heads]
            sst = [s_scr[h] for h in heads]
            sbs = [s.astype(BF) for s in sst]
            inter = [_nt(f[h]["qd"].astype(BF), sbs[h]) for h in heads]
            intra = [_nn(f[h]["att"].astype(BF), vs[h]) for h in heads]
            kv = [_tn(vs[h], f[h]["ke"].astype(BF)) for h in heads]
            for h in heads:
                gr = gr_ref[rows, h * DV:(h + 1) * DV].astype(F32)
                y_ref[rows, h * DV:(h + 1) * DV] = _headnorm_fwd(intra[h] + inter[h], hg_ref[:, h * DV:(h + 1) * DV],
                                                                   gr * _sigmoid(gr)).astype(BF)
                ss_ref[c, h] = sbs[h]
                s_scr[h] = sst[h] * f[h]["decay"] + kv[h]
            return carry

        lax.fori_loop(0, G_CPB, chunk, 0, unroll=2)

    return pl.pallas_call(
        body, grid=(nb,),
        in_specs=[_row_spec(D, CB_GQK), _row_spec(D, CB_GV), _row_spec(D, CB_GR), _row_spec(N_SMALL),
                  _full_spec((N_SMALL, NH * DQK)), _full_spec((1, NH * DQK)), _full_spec((1, D))],
        out_specs=[_row_spec(D), pl.BlockSpec((G_CPB, NH, DV, DQK), lambda i: (i, 0, 0, 0))],
        out_shape=(jax.ShapeDtypeStruct((tp, D), BF), jax.ShapeDtypeStruct((tp // G_CHUNK, NH, DV, DQK), BF)),
        scratch_shapes=[pltpu.VMEM((NH, DV, DQK), F32), pltpu.VMEM((TM, NH * DQK), F32)],
        compiler_params=_params(), name=name)(pbig, pbig, pbig, small, a2p, a2b, headg)


def _gla_bwd(dy, pbig, small, a2p, a2b, headg, ss, dsm_m, dproj, name):
    tp = pbig.shape[0]
    nb = tp // TM
    nqk = NH * DQK

    def body(dy_ref, qk_ref, v_ref, gr_ref, sm_ref, a2_ref, a2b_ref, hg_ref, ss_ref, dsmm_ref, _,
             dproj_ref, dsm_ref, da2_ref, da2b_ref, dhg_ref, ds_scr, lg_scr, dza_scr):
        step = pl.program_id(0)
        blk = nb - 1 - step

        @pl.when(step == 0)
        def _():
            ds_scr[...] = jnp.zeros_like(ds_scr)
            da2_ref[...] = jnp.zeros_like(da2_ref)
            da2b_ref[...] = jnp.zeros_like(da2b_ref)
            dhg_ref[...] = jnp.zeros_like(dhg_ref)

        k = _chunk_consts(G_CHUNK)
        rowio = k["rowio"]
        za, loga = _gla_loga(sm_ref, a2_ref, a2b_ref, blk)
        lg_scr[...] = loga

        def chunk(cc, carry):
            c = G_CPB - 1 - cc
            r0 = pl.multiple_of(c * G_CHUNK, G_CHUNK)
            rows = pl.ds(r0, G_CHUNK)
            bc = _nn(k["tril_f"], lg_scr[rows, :], precision=HI)
            btot = _last_row(bc, rowio)
            heads = range(NH)
            f = [_gla_head(h, qk_ref, qk_ref, rows, bc, btot, k) for h in heads]
            vs = [v_ref[rows, h * DV:(h + 1) * DV] for h in heads]
            sbs = [ss_ref[c, h] for h in heads]
            qdb = [f[h]["qd"].astype(BF) for h in heads]
            attb = [f[h]["att"].astype(BF) for h in heads]
            inter = [_nt(qdb[h], sbs[h]) for h in heads]
            intra = [_nn(attb[h], vs[h]) for h in heads]
            dsn = [ds_scr[h] for h in heads]
            dsb = [d.astype(BF) for d in dsn]
            dke = [_nn(vs[h], dsb[h]) for h in heads]
            dv1 = [_nt(f[h]["ke"].astype(BF), dsb[h]) for h in heads]
            t = []
            for h in heads:
                gr = gr_ref[rows, h * DV:(h + 1) * DV].astype(F32)
                sg = _sigmoid(gr)
                gain = hg_ref[:, h * DV:(h + 1) * DV]
                do, dact, dgain = _headnorm_bwd(dy_ref[rows, h * DV:(h + 1) * DV].astype(F32), intra[h] + inter[h], gain, gr * sg)
                dproj_ref[rows, 2 * D + h * DV:2 * D + (h + 1) * DV] = (dact * sg * (1.0 + gr * (1.0 - sg))).astype(BF)
                dhg_ref[:, h * DV:(h + 1) * DV] += dgain
                dob = do.astype(BF)
                t.append(dict(dob=dob, datt=_nt(dob, vs[h]), dv0=_tn(attb[h], dob), dq1=_nn(dob, sbs[h]), dsq=_tn(dob, qdb[h])))
            for h in heads:
                datt = jnp.where(k["tri"], t[h]["datt"], 0.0).astype(BF)
                t[h].update(dq0=_nn(datt, f[h]["ki"].astype(BF)), dki=_tn(datt, qdb[h]))
            dbc_parts = []
            for h in heads:
                dqd = t[h]["dq0"] + t[h]["dq1"]
                dki = t[h]["dki"]
                dproj_ref[rows, D + h * DV:D + (h + 1) * DV] = (t[h]["dv0"] + dv1[h]).astype(BF)
                dproj_ref[rows, h * DQK:(h + 1) * DQK] = (dqd * f[h]["e_pos"]).astype(BF)
                dproj_ref[rows, nqk + h * DQK:nqk + (h + 1) * DQK] = (dki * f[h]["e_neg"] + dke[h] * f[h]["e_end"]).astype(BF)
                dke_ke = dke[h] * f[h]["ke"]
                dbtot = (jnp.sum(dke_ke, axis=0, keepdims=True)
                         + jnp.sum(dsn[h] * sbs[h].astype(F32), axis=0, keepdims=True) * f[h]["decay"])
                dbc_parts.append(dqd * f[h]["qd"] - dki * f[h]["ki"] - dke_ke + jnp.where(rowio == G_CHUNK - 1, dbtot, 0.0))
                ds_scr[h] = dsn[h] * f[h]["decay"] + t[h]["dsq"]
            dbc = jnp.concatenate(dbc_parts, axis=1)
            dza_scr[rows, :] = _nn(k["triu_f"], dbc, precision=HI)
            return carry

        lax.fori_loop(0, G_CPB, chunk, 0, unroll=2)
        row = blk * TM + lax.broadcasted_iota(jnp.int32, (TM, 1), 0)
        dza = jnp.where(row >= FIRST_VALID, dza_scr[...] * (_sigmoid(-za) / G_TAU), 0.0)
        dzb = dza.astype(BF)
        dsm_ref[...] = (_nt(dzb, a2_ref[...]) + dsmm_ref[...]).astype(BF)
        da2_ref[...] += _tn(sm_ref[...].astype(BF), dzb)
        da2b_ref[...] += jnp.sum(dza, axis=0, keepdims=True)

    rspec = lambda width, col=0: pl.BlockSpec((TM, width), lambda i: (nb - 1 - i, col))
    return pl.pallas_call(
        body, grid=(nb,),
        in_specs=[rspec(D), rspec(D, CB_GQK), rspec(D, CB_GV), rspec(D, CB_GR), rspec(N_SMALL),
                  _full_spec((N_SMALL, nqk)), _full_spec((1, nqk)), _full_spec((1, D)),
                  pl.BlockSpec((G_CPB, NH, DV, DQK), lambda i: (nb - 1 - i, 0, 0, 0)), rspec(N_SMALL), ANY],
        out_specs=[rspec(3 * D, 0), rspec(N_SMALL), _full_spec((N_SMALL, nqk)), _full_spec((1, nqk)), _full_spec((1, D))],
        out_shape=(jax.ShapeDtypeStruct(dproj.shape, BF),
                   jax.ShapeDtypeStruct((tp, N_SMALL), BF), jax.ShapeDtypeStruct((N_SMALL, nqk), F32),
                   jax.ShapeDtypeStruct((1, nqk), F32), jax.ShapeDtypeStruct((1, D), F32)),
        scratch_shapes=[pltpu.VMEM((NH, DV, DQK), F32), pltpu.VMEM((TM, nqk), F32), pltpu.VMEM((TM, nqk), F32)],
        input_output_aliases={10: 0}, compiler_params=_params(), name=name)(dy, pbig, pbig, pbig, small, a2p, a2b, headg, ss, dsm_m, dproj)


PIECE_BYTES = 1 << 20
MAX_PIECES = 32


def _place():
    return lax.axis_index("x"), lax.axis_index("y"), lax.axis_index("c")


def _piece_rows(rows, row_bytes, align):
    want = min(MAX_PIECES, max(1, -(-rows * row_bytes // PIECE_BYTES)))
    best = rows
    for k in range(1, want + 1):
        if rows % k == 0 and (rows // k) % align == 0:
            best = rows // k
    return best


def _remote(src, dst, send_sems, recv_sems, k, to):
    return pltpu.make_async_remote_copy(src_ref=src, dst_ref=dst, send_sem=send_sems.at[k], recv_sem=recv_sems.at[k],
                                        device_id=to, device_id_type=MESH)


def _all_gather_chips(p, name):
    rd = _gather_rider(p)

    def body(*refs):
        start, middle, finish = rd["make"](refs[:1], refs[1:2], refs[2:])
        start()
        middle()
        finish()

    return pl.pallas_call(body, in_specs=[ANY], out_specs=[ANY], out_shape=rd["out_shapes"], scratch_shapes=rd["sems"],
                          name=name)(p)[0]


def _gather_rider(p):
    r, n = p.shape
    rh = r // 2
    align = 32 // p.dtype.itemsize
    assert r % (2 * align) == 0
    cr = _piece_rows(rh, n * p.dtype.itemsize, align)

    def make(in_refs, out_refs, sem_refs):
        p_ref, o_ref = in_refs[0], out_refs[0]
        send_sems, recv_sems = sem_refs
        x, y, c = _place()
        chips = [(1 - x, y), (x, 1 - y), (1 - x, 1 - y)]
        sib = (x, y, 1 - c)

        def half(hc, piece=None):
            if piece is None:
                return pl.ds(pl.multiple_of(hc * rh, align), rh)
            return pl.ds(pl.multiple_of(hc * rh + piece * cr, align), cr)

        first = [_remote(p_ref.at[half(c)], o_ref.at[j, half(c)], send_sems, recv_sems, j, (*chip, c))
                 for j, chip in enumerate(chips)]
        passed = [[_remote(o_ref.at[j, half(c, i)], o_ref.at[j, half(c, i)], send_sems, recv_sems, 3 + j, sib)
                   for i in range(rh // cr)] for j in range(3)]
        blocks = [_remote(o_ref.at[j, half(c)], o_ref.at[j, half(1 - c)], send_sems, recv_sems, 3 + j, sib) for j in range(3)]

        def start():
            for cp in first:
                cp.start()

        def middle():
            for j, cp in enumerate(first):
                cp.wait_recv()
                for piece in passed[j]:
                    piece.start()

        def finish():
            for block in blocks:
                block.wait_send()
                block.wait_recv()
            for cp in first:
                cp.wait_send()

        return start, middle, finish

    return dict(inputs=[p], out_shapes=(jax.ShapeDtypeStruct((3, r, n), p.dtype),),
                sems=[pltpu.SemaphoreType.DMA((6,)), pltpu.SemaphoreType.DMA((6,))], make=make)


def _scatter_rider(items):
    out_shapes, sems = _scatter_shapes(items)

    def make(in_refs, out_refs, sem_refs):
        sent = _scatter_copies(in_refs, out_refs, *sem_refs)

        def start():
            for cp in sent:
                cp.start()

        def finish():
            for cp in sent:
                cp.wait_recv()
            for cp in sent:
                cp.wait_send()

        return start, (lambda: None), finish

    return dict(inputs=list(items), out_shapes=out_shapes, sems=sems, make=make)


def _by_chip(mine, others):
    me = 2 * lax.axis_index("x") + lax.axis_index("y")
    by_mask = jnp.stack([mine, others[1], others[0], others[2]])
    return [lax.dynamic_index_in_dim(by_mask, q ^ me, 0, keepdims=False) for q in range(4)]


def _swap_halves(items, name):
    k = len(items)

    def body(*refs):
        a_refs, got_refs = refs[:k], refs[k:2 * k]
        send_sems, recv_sems = refs[2 * k:]
        x, y, c = _place()
        sib = (x, y, 1 - c)
        for i, a in enumerate(items):
            _, r, n = a.shape
            rh = r // 2
            cr = _piece_rows(rh, n * a.dtype.itemsize, 8)
            for q in range(4):
                for t in range(rh // cr):
                    other = pl.ds(pl.multiple_of((1 - c) * rh + t * cr, 8), cr)
                    _remote(a_refs[i].at[q, other], got_refs[i].at[q, pl.ds(t * cr, cr)], send_sems, recv_sems, i, sib).start()
        for i, a in enumerate(items):
            block = _remote(a_refs[i].at[:, pl.ds(0, a.shape[1] // 2)], got_refs[i], send_sems, recv_sems, i, sib)
            block.wait_send()
            block.wait_recv()

    return pl.pallas_call(
        body, in_specs=[ANY] * k, out_specs=[ANY] * k,
        out_shape=tuple(jax.ShapeDtypeStruct((4, a.shape[1] // 2, a.shape[2]), a.dtype) for a in items),
        scratch_shapes=[pltpu.SemaphoreType.DMA((k,)), pltpu.SemaphoreType.DMA((k,))], name=name)(*items)


def _scatter_copies(s_refs, o_refs, send_sems, recv_sems):
    x, y, c = _place()
    chips = [(1 - x, y), (x, 1 - y), (1 - x, 1 - y)]
    return [_remote(s_refs[i].at[2 * cx + cy], o_refs[i].at[j], send_sems, recv_sems, 3 * i + j, (cx, cy, c))
            for i in range(len(s_refs)) for j, (cx, cy) in enumerate(chips)]


def _scatter_shapes(items):
    k = len(items)
    return (tuple(jax.ShapeDtypeStruct((3,) + s.shape[1:], s.dtype) for s in items),
            [pltpu.SemaphoreType.DMA((3 * k,)), pltpu.SemaphoreType.DMA((3 * k,))])


def _scatter_chips(items, name):
    k = len(items)

    def body(*refs):
        sent = _scatter_copies(refs[:k], refs[k:2 * k], *refs[2 * k:])
        for cp in sent:
            cp.start()
        for cp in sent:
            cp.wait_recv()
        for cp in sent:
            cp.wait_send()

    out_shape, scratch = _scatter_shapes(items)
    return pl.pallas_call(body, in_specs=[ANY] * k, out_specs=[ANY] * k, out_shape=out_shape, scratch_shapes=scratch,
                          name=name)(*items)


def _join_halves(items, name):
    k = len(items)

    def body(*refs):
        f_refs, o_refs = refs[:k], refs[k:2 * k]
        send_sems, recv_sems = refs[2 * k:]
        x, y, c = _place()
        sib = (x, y, 1 - c)
        for i, f in enumerate(items):
            rh, n = f.shape
            cr = _piece_rows(rh, n * f.dtype.itemsize, 8)
            for t in range(rh // cr):
                rows = pl.ds(t * cr, cr)
                _remote(f_refs[i].at[rows], o_refs[i].at[rows], send_sems, recv_sems, i, sib).start()
        for i in range(k):
            block = _remote(f_refs[i], o_refs[i], send_sems, recv_sems, i, sib)
            block.wait_send()
            block.wait_recv()

    return pl.pallas_call(
        body, in_specs=[ANY] * k, out_specs=[ANY] * k, out_shape=tuple(jax.ShapeDtypeStruct(f.shape, f.dtype) for f in items),
        scratch_shapes=[pltpu.SemaphoreType.DMA((k,)), pltpu.SemaphoreType.DMA((k,))], name=name)(*items)


SMALL_ROWS = 16
SMALL_GRAD_ROWS = 48
SMALL_SHARD_SHAPES = [(N_META, 256), (4, 256), (G_RANK, 128), (NH, 64), (NH, 64)]
REPL_SHAPES = [(1, D), (1, D), (1, 2, NH), (1, NH * DQK), (1, D), (D,)]
W_IN_SHARD = 2054
W_IN_BLOCK = 2080


def _pack_small(parts, rows=SMALL_ROWS):
    flat = jnp.concatenate([p.reshape(-1) for p in parts])
    return jnp.pad(flat, (0, rows * D - flat.shape[0])).reshape(rows, D)


def _unpack_small(block, shapes):
    flat, out, off = block.reshape(-1), [], 0
    for shp in shapes:
        n = 1
        for s in shp:
            n *= s
        out.append(flat[off:off + n].reshape(shp))
        off += n
    return out


def _proj_rows_from_w_in(w_in_t):
    w_big = jnp.concatenate([w_in_t[3080:5128], w_in_t[5144:6168], w_in_t[0:1024], w_in_t[6168:8216],
                             w_in_t[1024:2048], w_in_t[2056:3080]], axis=0)
    w_small = jnp.concatenate([w_in_t[2048:2056], w_in_t[5128:5144], jnp.zeros((N_SMALL - 24, D), w_in_t.dtype)], axis=0)
    return w_big, w_small


def _w_in_from_proj_rows(d_wall_t):
    big, small = d_wall_t[0:N_BIG], d_wall_t[N_BIG:N_ALL]
    return jnp.concatenate([big[3072:4096], big[6144:7168], small[0:8], big[7168:8192], big[0:2048],
                            small[8:24], big[2048:3072], big[4096:6144]], axis=0)


def kernel(x, meta_tokens, norm1_g, w_in, conv_w, conv_b, m_gate_b, g_a2, g_a2_b, m_head_g, g_head_g, w_branch_m, w_branch_g, w_out, norm2_g, w_ff_gate, w_ff_up, w_ff_down, final_g, loss_target, m_meta_tokens, m_norm1_g, m_w_in, m_conv_w, m_conv_b, m_m_gate_b, m_g_a2, m_g_a2_b, m_m_head_g, m_g_head_g, m_w_branch_m, m_w_branch_g, m_w_out, m_norm2_g, m_w_ff_gate, m_w_ff_up, m_w_ff_down, m_final_g, v_meta_tokens, v_norm1_g, v_w_in, v_conv_w, v_conv_b, v_m_gate_b, v_g_a2, v_g_a2_b, v_m_head_g, v_g_head_g, v_w_branch_m, v_w_branch_g, v_w_out, v_norm2_g, v_w_ff_gate, v_w_ff_up, v_w_ff_down, v_final_g):
    w = _gather_weights(w_in, w_branch_m, w_branch_g, w_out, w_ff_gate, w_ff_up, w_ff_down, meta_tokens, conv_w, g_a2, m_head_g, g_head_g)
    loss_local, dx, grads = _local_step(x[0], loss_target[0], w, norm1_g, conv_b, m_gate_b, g_a2_b, norm2_g, final_g, _Reducer())

    weights = [w_in, w_branch_m, w_branch_g, w_out, w_ff_gate, w_ff_up, w_ff_down, meta_tokens, conv_w, g_a2, m_head_g, g_head_g,
               norm1_g, conv_b, m_gate_b, g_a2_b, norm2_g, final_g]
    moms = [m_w_in, m_w_branch_m, m_w_branch_g, m_w_out, m_w_ff_gate, m_w_ff_up, m_w_ff_down, m_meta_tokens, m_conv_w, m_g_a2,
            m_m_head_g, m_g_head_g, m_norm1_g, m_conv_b, m_m_gate_b, m_g_a2_b, m_norm2_g, m_final_g]
    vels = [v_w_in, v_w_branch_m, v_w_branch_g, v_w_out, v_w_ff_gate, v_w_ff_up, v_w_ff_down, v_meta_tokens, v_conv_w, v_g_a2,
            v_m_head_g, v_g_head_g, v_norm1_g, v_conv_b, v_m_gate_b, v_g_a2_b, v_norm2_g, v_final_g]
    res = {}
    for nm, wt, g, m, v in zip(PACK_ORDER, weights, grads, moms, vels):
        if nm in TRANSPOSED_GRADS:
            to2d = lambda a: jnp.swapaxes(a, -1, -2).reshape(a.shape[-1], a.shape[-2])
            back = lambda a: jnp.swapaxes(a, 0, 1).reshape(wt.shape)
        else:
            to2d = lambda a: a.reshape(wt.size // wt.shape[-1], wt.shape[-1])
            back = lambda a: a.reshape(wt.shape)
        d, nm_, nv_ = _adamw(to2d(wt), g, to2d(m), to2d(v), "adamw_" + nm)
        res[nm] = (back(g), back(d), back(nm_), back(nv_))

    order = ["meta_tokens", "norm1_g", "w_in", "conv_w", "conv_b", "m_gate_b", "g_a2", "g_a2_b", "m_head_g", "g_head_g",
             "w_branch_m", "w_branch_g", "w_out", "norm2_g", "w_ff_gate", "w_ff_up", "w_ff_down", "final_g"]
    loss = lax.psum(loss_local[0, 0], ("x", "y", "c"))
    grad_x = dx.reshape(x.shape)
    return (loss, grad_x, *[res[n][0] for n in order], *[res[n][1] for n in order],
            *[res[n][2] for n in order], *[res[n][3] for n in order])


TRANSPOSED_GRADS = ("w_in", "w_ff_gate", "w_ff_up")
PACK_ORDER = ["w_in", "w_branch_m", "w_branch_g", "w_out", "w_ff_gate", "w_ff_up", "w_ff_down", "meta_tokens", "conv_w", "g_a2",
              "m_head_g", "g_head_g", "norm1_g", "conv_b", "m_gate_b", "g_a2_b", "norm2_g", "final_g"]


def _gather_weights(w_in, w_branch_m, w_branch_g, w_out, w_ff_gate, w_ff_up, w_ff_down, meta_tokens, conv_w, g_a2, m_head_g, g_head_g):
    bf = lambda a: a.astype(BF)
    rows_local = jnp.concatenate([bf(w_branch_m[0]), bf(w_branch_g[0]), bf(w_out[0]), bf(w_ff_down[0]),
                                  bf(w_ff_gate[0].T), bf(w_ff_up[0].T)], axis=0)
    win_local = jnp.pad(bf(w_in[0].T), ((0, W_IN_BLOCK - W_IN_SHARD), (0, 0)))
    small_local = _pack_small([meta_tokens, conv_w[0], g_a2[0], m_head_g[0], g_head_g[0]])
    small_all = _by_chip(small_local, _all_gather_chips(small_local, "gather_small"))
    small_sh = [_unpack_small(small_all[q], SMALL_SHARD_SHAPES) for q in range(4)]
    cat = lambda i: jnp.concatenate([s[i] for s in small_sh], axis=-1)
    return dict(win_local=win_local, rows_local=rows_local, meta=cat(0), convw=cat(1), ga2=cat(2),
                mhg=cat(3).reshape(1, D), ghg=cat(4).reshape(1, D))


def _row_weights(rows_local, gathered):
    rows_all = jnp.stack(_by_chip(rows_local, gathered))
    cut = lambda lo, hi: rows_all[:, lo:hi].reshape(4 * (hi - lo), D)
    return cut(0, 256), cut(256, 512), cut(512, 768), cut(768, 1472), _ffn_weight_rows(cut(1472, 2176), cut(2176, 2880))


def _local_step(x0, target, w, norm1_g, conv_b, m_gate_b, g_a2_b, norm2_g, final_g, reducer):
    meta_f, convw_f, ga2_f, mhg_f, ghg_f = w["meta"], w["convw"], w["ga2"], w["mhg"], w["ghg"]
    gbias =jnp.concatenate([m_gate_b.reshape(1, 2 * NH), jnp.zeros((1, N_SMALL - 2 * NH), F32)], axis=1)
    a2p = jnp.concatenate([jnp.zeros((8, NH * DQK), F32), ga2_f, jnp.zeros((N_SMALL - 24, NH * DQK), F32)], axis=0).astype(BF)
    convb = conv_b.reshape(1, D)
    g1 = norm1_g.reshape(1, D)
    g2 = norm2_g.reshape(1, D)
    gf = final_g.reshape(1, D)
    first = jnp.concatenate([jnp.zeros((FIRST_VALID, D), F32), meta_f], axis=0)

    h0, xn1, rstd1, win_gathered = _embed_norm(x0, first, g1, _gather_rider(w["win_local"]), "rms1")
    win_all = _by_chip(w["win_local"], win_gathered[0])
    w_in_f = jnp.concatenate([win_all[q][0:W_IN_SHARD] for q in range(4)], axis=0)
    w_big, w_small = _proj_rows_from_w_in(w_in_f)
    w_all = jnp.concatenate([w_big, w_small], axis=0)
    pbig, rows_gathered = _mm(xn1, w_big, nt=True, out_dtype=BF, tn=2 * D, name="proj_big", rider=_gather_rider(w["rows_local"]))
    wbm, wbg, wout, wdown, wgu_t = _row_weights(w["rows_local"], rows_gathered[0])
    small = _mm(xn1, w_small, nt=True, out_dtype=F32, tn=N_SMALL, name="proj_small")
    qk = _conv_fwd(pbig, convw_f, convb, "conv_fwd")
    y_m, m_cs, m_ns = _mlstm_fwd(qk, pbig, small, gbias, mhg_f, "mlstm_fwd")
    y_g, g_ss = _gla_fwd(pbig, small, a2p, g_a2_b, ghg_f, "gla_fwd")
    p_m, p_g, merged = _branch_merge(y_m, y_g, wbm, wbg, pbig, "branch_merge")
    h1, hn, rstd2 = _out_proj_norm(merged, wout, h0, g2, "out_proj")
    gu, ff = _ffn_in(hn, wgu_t, "ff_in")
    dh2, loss_local, d_final_g = _ffn_down_loss(ff, wdown, h1, target, gf, "ff_down_loss")

    d_wdown = _mm_tn(ff, dh2, tm=FF_TN, tn=D, name="dw_ff_down")
    dgu = _ffn_d_hidden(dh2, wdown, gu, "d_ff")
    d_wgu_t = _mm_tn(dgu, hn, tm=FF_TN, tn=D, name="dw_ff_in")
    dh1, d_g2 = _ffn_d_in(dgu, wgu_t, h1, rstd2, g2, dh2, "d_hn")
    d_wout = _mm_tn(merged, dh1, tm=D, tn=D, name="dw_out")
    dp_m, dp_g, dproj = _merge_d(dh1, wout, p_m, p_g, pbig, "d_merged")
    dy_m = _mm(dp_m, wbm, nt=True, out_dtype=BF, tn=D, name="d_ym")
    dy_g = _mm(dp_g, wbg, nt=True, out_dtype=BF, tn=D, name="d_yg")
    d_wbm = _mm_tn(y_m, dp_m, tm=D, tn=D, name="dw_branch_m")
    d_wbg = _mm_tn(y_g, dp_g, tm=D, tn=D, name="dw_branch_g")
    fq = D_FF // 4
    gu4 = jnp.transpose(d_wgu_t.reshape(2, 2, 2, fq, D), (0, 2, 1, 3, 4)).reshape(4, 2 * fq, D)
    sq4 = jnp.concatenate([d_wbm.reshape(4, 256, D), d_wbg.reshape(4, 256, D), d_wout.reshape(4, 256, D)], axis=1)
    sums_a = reducer.partial_sums([sq4, d_wdown.reshape(4, fq, D), gu4], BF, "a")
    dqk_m, dproj, dsm_m, d_gbias, d_mhg, recv_a = _mlstm_bwd(dy_m, qk, pbig, small, gbias, mhg_f, m_cs, m_ns, dproj,
                                                              "mlstm_bwd", ride=sums_a)
    dproj, d_convwb = _conv_bwd(dqk_m, pbig, convw_f, convb, dproj, "conv_bwd")
    dproj, dsmall, d_a2p, d_a2b, d_ghg = _gla_bwd(dy_g, pbig, small, a2p, g_a2_b, ghg_f, g_ss, dsm_m, dproj, "gla_bwd")
    dproj = _place_small(dsmall, dproj, "dproj_small")
    d_win = _w_in_from_proj_rows(_mm_tn(dproj, xn1, tm=PROJ_TK, tn=D, name="dw_in"))
    pad = jnp.zeros((W_IN_BLOCK - W_IN_SHARD, D), F32)
    win4 = jnp.stack([jnp.concatenate([d_win[q * W_IN_SHARD:(q + 1) * W_IN_SHARD], pad], axis=0) for q in range(4)])
    sums_b = reducer.partial_sums([win4], BF, "b")
    dxn, recv_b = _mm(dproj, w_all, nt=False, out_dtype=F32, tn=D, tm=TM, name="d_xn", rider=_scatter_rider(sums_b),
                      single_buffer_b=True)
    dh_first, dx, d_g1 = _rms_bwd(dxn, h0, rstd1, g1, dh1, "rms1_bwd", split_first=True)

    small_sharded = [dh_first[FIRST_VALID:TM], d_convwb[0:4], d_a2p[8:24], d_mhg.reshape(NH, DV), d_ghg.reshape(NH, DV)]
    replicated = [d_g1, d_convwb[4:5], d_gbias[0:1, 0:2 * NH].reshape(1, 2, NH), d_a2b, d_g2, d_final_g.reshape(D)]
    small4 = jnp.broadcast_to(_pack_small(small_sharded + replicated, SMALL_GRAD_ROWS)[None], (4, SMALL_GRAD_ROWS, D))
    sums_c = reducer.partial_sums([small4], F32, "c")
    recv_c = reducer.scatter(sums_c, "c")
    sq, down, gu, win, smalls = reducer.finish(sums_a + sums_b + sums_c, recv_a + recv_b + recv_c, in_chip_order=[4])
    smalls = _unpack_small(smalls, [g.shape for g in small_sharded + replicated])
    me = 2 * lax.axis_index("x") + lax.axis_index("y")
    smalls = ([lax.dynamic_slice_in_dim(g, me * shp[1], shp[1], axis=1) for g, shp in zip(smalls, SMALL_SHARD_SHAPES)]
              + smalls[len(SMALL_SHARD_SHAPES):])
    grads = ([win[0:W_IN_SHARD], sq[0:256], sq[256:512], sq[512:768], gu[0:fq], gu[fq:2 * fq], down]
             + [g.reshape(g.size // g.shape[-1], g.shape[-1]) for g in smalls])
    return loss_local, dx, grads


class _Reducer:
    def partial_sums(self, items, dtype, tag):
        c = lax.axis_index("c")
        got = _swap_halves(items, "reduce_siblings_" + tag)
        sums = []
        for i, (a, g) in enumerate(zip(items, got)):
            rh, n = g.shape[1], g.shape[2]
            own = lax.dynamic_slice_in_dim(a, c * rh, rh, axis=1)
            sums.append(_add2(own.reshape(-1, n), g.reshape(-1, n), dtype, f"reduce_add2_{tag}{i}").reshape(g.shape))
        return sums

    def scatter(self, sums, tag):
        return list(_scatter_chips(sums, "reduce_chips_" + tag))

    def finish(self, sums, from_chips, in_chip_order):
        c = lax.axis_index("c")
        me = 2 * lax.axis_index("x") + lax.axis_index("y")
        halves = []
        for i, (s, f) in enumerate(zip(sums, from_chips)):
            mine = lax.dynamic_index_in_dim(s, me, 0, keepdims=False)
            if i in in_chip_order:
                by_chip = _by_chip(mine, f)
                mine, f = by_chip[0], jnp.stack(by_chip[1:])
            halves.append(_add4(mine, f, f"reduce_add4_{i}"))
        got = _join_halves(halves, "reduce_join")
        return [jnp.concatenate([jnp.where(c == 0, h, g), jnp.where(c == 0, g, h)], axis=0) for h, g in zip(halves, got)]
```
